```python
import jax, jax.numpy as jnp
from jax import lax
import numpy as np

D_MODEL = 1024
BATCH = 8
SEQ = 8192
DEPTH = 2

ATTN_WIDTH = D_MODEL // 2
HEAD_DIM = 64
N_ATTN_HEADS = ATTN_WIDTH // HEAD_DIM
CONV_WIDTH = D_MODEL - ATTN_WIDTH
CONV_K = 3
DILATED_BRANCHES = ((128, 1), (512, 4), (2048, 16))
BLOCK = 128
D_FF = 2816
N_SUB = 3
N_MOD = 3
IN_COLS = 3 * ATTN_WIDTH + 3 * CONV_WIDTH
EPS = 1e-6
NEG = -1e30

kernel_name = "hymba_dilated_attn_shortconv_macaron"


def _rmsnorm(x, g):
    xf = x.astype(jnp.float32)
    y = xf * lax.rsqrt(jnp.mean(xf * xf, axis=-1, keepdims=True) + EPS)
    return (y * g.astype(jnp.float32)).astype(x.dtype)


def _band_attention(q, k, v, span):
    n, L, h, hd = q.shape
    nb = L // BLOCK
    qb = q.reshape(n, nb, BLOCK, h, hd)
    kb = k.reshape(n, nb, BLOCK, h, hd)
    vb = v.reshape(n, nb, BLOCK, h, hd)
    zk = jnp.zeros_like(kb[:, :1])
    k2 = jnp.concatenate([jnp.concatenate([zk, kb[:, :-1]], axis=1), kb], axis=2)
    v2 = jnp.concatenate([jnp.concatenate([zk, vb[:, :-1]], axis=1), vb], axis=2)
    s = jnp.einsum('nbqhd,nbkhd->nbhqk', qb, k2).astype(jnp.float32) * (hd ** -0.5)
    qi = jnp.arange(BLOCK)[:, None] + BLOCK
    kj = jnp.arange(2 * BLOCK)[None, :]
    dist = qi - kj
    band = (dist >= 0) & (dist <= span)
    blk = jnp.arange(nb)[:, None, None]
    mask = band[None] & ((blk > 0) | (kj[None] >= BLOCK))
    s = jnp.where(mask[None, :, None], s, NEG)
    m = jnp.max(s, axis=-1, keepdims=True)
    p = jnp.exp(s - m)
    l = jnp.sum(p, axis=-1, keepdims=True)
    o = jnp.einsum('nbhqk,nbkhd->nbqhd', (p / l).astype(v.dtype), v2)
    lse = (m + jnp.log(l))[..., 0]
    return o.reshape(n, L, h, hd), lse.transpose(0, 1, 3, 2).reshape(n, L, h)


def _dilated_branch(q, k, v, window, dilation):
    b, s, h, hd = q.shape
    L = s // dilation
    Lp = -(-L // BLOCK) * BLOCK

    def to_res(t):
        t = t.reshape(b, L, dilation, h, hd).transpose(0, 2, 1, 3, 4).reshape(b * dilation, L, h, hd)
        return jnp.pad(t, ((0, 0), (0, Lp - L), (0, 0), (0, 0)))

    o, lse = _band_attention(to_res(q), to_res(k), to_res(v), window // dilation)
    o = o[:, :L].reshape(b, dilation, L, h, hd).transpose(0, 2, 1, 3, 4).reshape(b, s, h, hd)
    lse = lse[:, :L].reshape(b, dilation, L, h).transpose(0, 2, 1, 3).reshape(b, s, h)
    return o, lse


def _dilated_attention(q, k, v):
    outs, lses = [], []
    for window, dilation in DILATED_BRANCHES:
        o, lse = _dilated_branch(q, k, v, window, dilation)
        outs.append(o)
        lses.append(lse)
    wts = jax.nn.softmax(jnp.stack(lses, axis=-1), axis=-1)
    return jnp.einsum('bshr,rbshd->bshd', wts.astype(q.dtype), jnp.stack(outs, axis=0))


def _short_conv(u, w, bias):
    y = lax.conv_general_dilated(
        u, w[:, None, :].astype(u.dtype), window_strides=(1,),
        padding=((CONV_K - 1, 0),), dimension_numbers=('NWC', 'WIO', 'NWC'),
        feature_group_count=u.shape[-1])
    return y + bias


def _swiglu(h, w1, w2):
    g, up = jnp.split(h @ w1, 2, axis=-1)
    return (jax.nn.silu(g) * up) @ w2


def _mixer(h, w_in, q_g, k_g, conv_w, conv_b, w_out):
    b, s, _ = h.shape
    A, C = ATTN_WIDTH, CONV_WIDTH
    proj = h @ w_in
    q, k, v, gb, gc, u = jnp.split(proj, [A, 2 * A, 3 * A, 3 * A + C, 3 * A + 2 * C], axis=-1)
    q = _rmsnorm(q.reshape(b, s, N_ATTN_HEADS, HEAD_DIM), q_g)
    k = _rmsnorm(k.reshape(b, s, N_ATTN_HEADS, HEAD_DIM), k_g)
    v = v.reshape(b, s, N_ATTN_HEADS, HEAD_DIM)
    y_attn = _dilated_attention(q, k, v).reshape(b, s, A)
    y_conv = gb * _short_conv(gc * u, conv_w, conv_b)
    return jnp.concatenate([y_attn, y_conv], axis=-1) @ w_out


def _fwd_setup_inputs(seed: int = 0) -> dict:
    key = jax.random.key(seed)
    ks = jax.random.split(key, 14)
    D = D_MODEL
    nrm = jax.random.normal
    return {
        "x": nrm(ks[0], (BATCH, SEQ, D), jnp.float32),
        "c": nrm(ks[1], (BATCH, D), jnp.float32),
        "w_ada": nrm(ks[2], (DEPTH, D, N_SUB * N_MOD * D), jnp.float32) * (0.5 * D ** -0.5),
        "b_ada": nrm(ks[3], (DEPTH, N_SUB * N_MOD * D), jnp.float32) * 0.02,
        "norm_g": 1.0 + 0.02 * nrm(ks[4], (DEPTH, N_SUB, D), jnp.float32),
        "w_in": nrm(ks[5], (DEPTH, D, IN_COLS), jnp.float32) * D ** -0.5,
        "q_norm_g": 1.0 + 0.02 * nrm(ks[6], (DEPTH, HEAD_DIM), jnp.float32),
        "k_norm_g": 1.0 + 0.02 * nrm(ks[7], (DEPTH, HEAD_DIM), jnp.float32),
        "conv_w": nrm(ks[8], (DEPTH, CONV_K, CONV_WIDTH), jnp.float32) * CONV_K ** -0.5,
        "conv_b": nrm(ks[9], (DEPTH, CONV_WIDTH), jnp.float32) * 0.02,
        "w_out": nrm(ks[10], (DEPTH, D, D), jnp.float32) * D ** -0.5,
        "ffn_w1": nrm(ks[11], (DEPTH, 2, D, 2 * D_FF), jnp.float32) * D ** -0.5,
        "ffn_w2": nrm(ks[12], (DEPTH, 2, D_FF, D), jnp.float32) * D_FF ** -0.5,
    }


def _fwd_reference(x, c, w_ada, b_ada, norm_g, w_in, q_norm_g, k_norm_g, conv_w, conv_b,
              w_out, ffn_w1, ffn_w2):
    b = x.shape[0]
    for layer in range(DEPTH):
        mod = (jax.nn.silu(c) @ w_ada[layer] + b_ada[layer]).reshape(b, N_SUB, N_MOD, D_MODEL)
        shift = mod[:, :, 0, None, :]
        scale = mod[:, :, 1, None, :]
        gate = mod[:, :, 2, None, :]

        def ada(z, i):
            return _rmsnorm(z, norm_g[layer, i]) * (1.0 + scale[:, i]) + shift[:, i]

        x = x + 0.5 * gate[:, 0] * _swiglu(ada(x, 0), ffn_w1[layer, 0], ffn_w2[layer, 0])
        x = x + gate[:, 1] * _mixer(ada(x, 1), w_in[layer], q_norm_g[layer], k_norm_g[layer],
                                    conv_w[layer], conv_b[layer], w_out[layer])
        x = x + 0.5 * gate[:, 2] * _swiglu(ada(x, 2), ffn_w1[layer, 1], ffn_w2[layer, 1])
    return x


import jax as _jax
import jax.numpy as _jnp

TWIN_FORMAT = 'train_step'
FWD_PARAMS = ['x', 'c', 'w_ada', 'b_ada', 'norm_g', 'w_in', 'q_norm_g', 'k_norm_g', 'conv_w', 'conv_b', 'w_out', 'ffn_w1', 'ffn_w2']
TWIN_WEIGHTS = ['w_ada', 'b_ada', 'norm_g', 'w_in', 'q_norm_g', 'k_norm_g', 'conv_w', 'conv_b', 'w_out', 'ffn_w1', 'ffn_w2']
TWIN_DIFF_INPUT = 'x'
TWIN_INPUTS = ['x', 'c', 'w_ada', 'b_ada', 'norm_g', 'w_in', 'q_norm_g', 'k_norm_g', 'conv_w', 'conv_b', 'w_out', 'ffn_w1', 'ffn_w2', 'loss_target', 'm_w_ada', 'm_b_ada', 'm_norm_g', 'm_w_in', 'm_q_norm_g', 'm_k_norm_g', 'm_conv_w', 'm_conv_b', 'm_w_out', 'm_ffn_w1', 'm_ffn_w2', 'v_w_ada', 'v_b_ada', 'v_norm_g', 'v_w_in', 'v_q_norm_g', 'v_k_norm_g', 'v_conv_w', 'v_conv_b', 'v_w_out', 'v_ffn_w1', 'v_ffn_w2']
TWIN_OUTPUTS = ['loss', 'grad_x', 'grad_w_ada', 'grad_b_ada', 'grad_norm_g', 'grad_w_in', 'grad_q_norm_g', 'grad_k_norm_g', 'grad_conv_w', 'grad_conv_b', 'grad_w_out', 'grad_ffn_w1', 'grad_ffn_w2', 'delta_w_ada', 'delta_b_ada', 'delta_norm_g', 'delta_w_in', 'delta_q_norm_g', 'delta_k_norm_g', 'delta_conv_w', 'delta_conv_b', 'delta_w_out', 'delta_ffn_w1', 'delta_ffn_w2', 'new_m_w_ada', 'new_m_b_ada', 'new_m_norm_g', 'new_m_w_in', 'new_m_q_norm_g', 'new_m_k_norm_g', 'new_m_conv_w', 'new_m_conv_b', 'new_m_w_out', 'new_m_ffn_w1', 'new_m_ffn_w2', 'new_v_w_ada', 'new_v_b_ada', 'new_v_norm_g', 'new_v_w_in', 'new_v_q_norm_g', 'new_v_k_norm_g', 'new_v_conv_w', 'new_v_conv_b', 'new_v_w_out', 'new_v_ffn_w1', 'new_v_ffn_w2']
TWIN_LEAF_KINDS = {'loss': 'loss', 'grad_x': 'grad_x', 'grad_w_ada': 'grad_w', 'grad_b_ada': 'grad_w', 'grad_norm_g': 'grad_w', 'grad_w_in': 'grad_w', 'grad_q_norm_g': 'grad_w', 'grad_k_norm_g': 'grad_w', 'grad_conv_w': 'grad_w', 'grad_conv_b': 'grad_w', 'grad_w_out': 'grad_w', 'grad_ffn_w1': 'grad_w', 'grad_ffn_w2': 'grad_w', 'delta_w_ada': 'delta_w', 'delta_b_ada': 'delta_w', 'delta_norm_g': 'delta_w', 'delta_w_in': 'delta_w', 'delta_q_norm_g': 'delta_w', 'delta_k_norm_g': 'delta_w', 'delta_conv_w': 'delta_w', 'delta_conv_b': 'delta_w', 'delta_w_out': 'delta_w', 'delta_ffn_w1': 'delta_w', 'delta_ffn_w2': 'delta_w', 'new_m_w_ada': 'new_m', 'new_m_b_ada': 'new_m', 'new_m_norm_g': 'new_m', 'new_m_w_in': 'new_m', 'new_m_q_norm_g': 'new_m', 'new_m_k_norm_g': 'new_m', 'new_m_conv_w': 'new_m', 'new_m_conv_b': 'new_m', 'new_m_w_out': 'new_m', 'new_m_ffn_w1': 'new_m', 'new_m_ffn_w2': 'new_m', 'new_v_w_ada': 'new_v', 'new_v_b_ada': 'new_v', 'new_v_norm_g': 'new_v', 'new_v_w_in': 'new_v', 'new_v_q_norm_g': 'new_v', 'new_v_k_norm_g': 'new_v', 'new_v_conv_w': 'new_v', 'new_v_conv_b': 'new_v', 'new_v_w_out': 'new_v', 'new_v_ffn_w1': 'new_v', 'new_v_ffn_w2': 'new_v'}


def _forward(args):
    return _fwd_reference(*[args[k] for k in FWD_PARAMS])


def _output_shape():
    def fwd():
        inp = _fwd_setup_inputs(0)
        return _fwd_reference(*[inp[k] for k in FWD_PARAMS])
    out = _jax.eval_shape(fwd)
    return out.shape, out.dtype

N_MICROBATCH = 1
ADAM_LR = 0.001
ADAM_B1 = 0.9
ADAM_B2 = 0.999
ADAM_EPS = 1e-08
ADAM_WD = 0.01
ADAM_STEP = 10
PER_EXAMPLE_BATCH_AXIS = {'x': 0, 'c': 0, 'loss_target': 0}
SHARED_INPUTS = []
_WEIGHT_DTYPES = {'w_ada': _jnp.float32, 'b_ada': _jnp.float32, 'norm_g': _jnp.float32, 'w_in': _jnp.float32, 'q_norm_g': _jnp.float32, 'k_norm_g': _jnp.float32, 'conv_w': _jnp.float32, 'conv_b': _jnp.float32, 'w_out': _jnp.float32, 'ffn_w1': _jnp.float32, 'ffn_w2': _jnp.float32}
MOMENT_SCALE = {'w_ada': 1.738103e+00, 'b_ada': 4.928182e+00, 'norm_g': 7.868358e+00, 'w_in': 2.919841e-01, 'q_norm_g': 3.315548e-01, 'k_norm_g': 3.338738e-01, 'conv_w': 5.495951e+00, 'conv_b': 4.426513e-01, 'w_out': 2.676615e-01, 'ffn_w1': 3.695033e-02, 'ffn_w2': 6.029955e-02}


def _to_microbatches(a, axis):
    t = _jnp.moveaxis(a, axis, 0)
    t = t.reshape((N_MICROBATCH, t.shape[0] // N_MICROBATCH) + t.shape[1:])
    return _jnp.moveaxis(t, 1, axis + 1)


def setup_inputs(seed: int = 0) -> dict:
    inp = _fwd_setup_inputs(seed)
    key = _jax.random.fold_in(_jax.random.key(seed), 7919)
    shape, _ = _output_shape()
    out = dict(inp)
    out["loss_target"] = _jax.random.normal(_jax.random.fold_in(key, 0), shape, _jnp.float32)
    for i, name in enumerate(TWIN_WEIGHTS):
        w = inp[name].astype(_jnp.float32)
        if MOMENT_SCALE is None:
            s = _jnp.sqrt(_jnp.mean(_jnp.square(w)) + 1e-30)
        else:
            s = MOMENT_SCALE[name]
        km, kv = _jax.random.split(_jax.random.fold_in(key, i + 1))
        out[name] = w
        out["m_" + name] = s * _jax.random.normal(km, w.shape, _jnp.float32)
        out["v_" + name] = (s * s) * _jax.random.uniform(kv, w.shape, _jnp.float32, 0.5, 1.5)
    if N_MICROBATCH > 1:
        for name, axis in PER_EXAMPLE_BATCH_AXIS.items():
            out[name] = _to_microbatches(out[name], axis)
    return {'x': out['x'], 'c': out['c'], 'w_ada': out['w_ada'], 'b_ada': out['b_ada'], 'norm_g': out['norm_g'], 'w_in': out['w_in'], 'q_norm_g': out['q_norm_g'], 'k_norm_g': out['k_norm_g'], 'conv_w': out['conv_w'], 'conv_b': out['conv_b'], 'w_out': out['w_out'], 'ffn_w1': out['ffn_w1'], 'ffn_w2': out['ffn_w2'], 'loss_target': out['loss_target'], 'm_w_ada': out['m_w_ada'], 'm_b_ada': out['m_b_ada'], 'm_norm_g': out['m_norm_g'], 'm_w_in': out['m_w_in'], 'm_q_norm_g': out['m_q_norm_g'], 'm_k_norm_g': out['m_k_norm_g'], 'm_conv_w': out['m_conv_w'], 'm_conv_b': out['m_conv_b'], 'm_w_out': out['m_w_out'], 'm_ffn_w1': out['m_ffn_w1'], 'm_ffn_w2': out['m_ffn_w2'], 'v_w_ada': out['v_w_ada'], 'v_b_ada': out['v_b_ada'], 'v_norm_g': out['v_norm_g'], 'v_w_in': out['v_w_in'], 'v_q_norm_g': out['v_q_norm_g'], 'v_k_norm_g': out['v_k_norm_g'], 'v_conv_w': out['v_conv_w'], 'v_conv_b': out['v_conv_b'], 'v_w_out': out['v_w_out'], 'v_ffn_w1': out['v_ffn_w1'], 'v_ffn_w2': out['v_ffn_w2']}


def _loss(weights, diff, rest, loss_target):
    with _jax.named_scope("forward"):
        args = {**rest, TWIN_DIFF_INPUT: diff, **{k: w.astype(_WEIGHT_DTYPES[k]) for k, w in weights.items()}}
        y = _forward(args)
    with _jax.named_scope("loss_head"):
        err = _jnp.square(y.astype(_jnp.float32) - loss_target)
        return 0.5 * _jnp.sum(_jnp.mean(err, axis=-1)) if err.ndim else 0.5 * err


def _adamw(w, g, m, v):
    m = ADAM_B1 * m + (1.0 - ADAM_B1) * g
    v = ADAM_B2 * v + (1.0 - ADAM_B2) * _jnp.square(g)
    m_hat = m / (1.0 - ADAM_B1 ** ADAM_STEP)
    v_hat = v / (1.0 - ADAM_B2 ** ADAM_STEP)
    delta = -ADAM_LR * (m_hat / (_jnp.sqrt(v_hat) + ADAM_EPS) + ADAM_WD * w)
    return delta, m, v


def reference(x, c, w_ada, b_ada, norm_g, w_in, q_norm_g, k_norm_g, conv_w, conv_b, w_out, ffn_w1, ffn_w2, loss_target, m_w_ada, m_b_ada, m_norm_g, m_w_in, m_q_norm_g, m_k_norm_g, m_conv_w, m_conv_b, m_w_out, m_ffn_w1, m_ffn_w2, v_w_ada, v_b_ada, v_norm_g, v_w_in, v_q_norm_g, v_k_norm_g, v_conv_w, v_conv_b, v_w_out, v_ffn_w1, v_ffn_w2):
    given = dict(x=x, c=c, w_ada=w_ada, b_ada=b_ada, norm_g=norm_g, w_in=w_in, q_norm_g=q_norm_g, k_norm_g=k_norm_g, conv_w=conv_w, conv_b=conv_b, w_out=w_out, ffn_w1=ffn_w1, ffn_w2=ffn_w2, loss_target=loss_target, m_w_ada=m_w_ada, m_b_ada=m_b_ada, m_norm_g=m_norm_g, m_w_in=m_w_in, m_q_norm_g=m_q_norm_g, m_k_norm_g=m_k_norm_g, m_conv_w=m_conv_w, m_conv_b=m_conv_b, m_w_out=m_w_out, m_ffn_w1=m_ffn_w1, m_ffn_w2=m_ffn_w2, v_w_ada=v_w_ada, v_b_ada=v_b_ada, v_norm_g=v_norm_g, v_w_in=v_w_in, v_q_norm_g=v_q_norm_g, v_k_norm_g=v_k_norm_g, v_conv_w=v_conv_w, v_conv_b=v_conv_b, v_w_out=v_w_out, v_ffn_w1=v_ffn_w1, v_ffn_w2=v_ffn_w2)
    weights = {n: given[n] for n in TWIN_WEIGHTS}
    shared = {n: given[n] for n in SHARED_INPUTS}
    per_example = {n: given[n] for n in ['x', 'c']}
    grad_fn = _jax.value_and_grad(_loss, argnums=(0, 1))

    def one_microbatch(ex, loss_target):
        ex = dict(ex)
        diff = ex.pop(TWIN_DIFF_INPUT)
        return grad_fn(weights, diff, {**shared, **ex}, loss_target)

    if N_MICROBATCH == 1:
        loss, (grad_w, grad_x) = one_microbatch(per_example, given["loss_target"])
    else:
        def body(carry, xs):
            loss_sum, grad_sum = carry
            l_k, (gw_k, gx_k) = one_microbatch(xs[0], xs[1])
            with _jax.named_scope("update"):
                return (loss_sum + l_k, _jax.tree.map(_jnp.add, grad_sum, gw_k)), gx_k

        init = (_jnp.zeros((), _jnp.float32), _jax.tree.map(_jnp.zeros_like, weights))
        (loss, grad_w), grad_x = _jax.lax.scan(body, init, (per_example, given["loss_target"]))
    with _jax.named_scope("update"):
        delta_w, new_m, new_v = {}, {}, {}
        for n in TWIN_WEIGHTS:
            delta_w[n], new_m[n], new_v[n] = _adamw(weights[n], grad_w[n], given["m_" + n], given["v_" + n])
    return (loss, grad_x, *[grad_w[n] for n in TWIN_WEIGHTS], *[delta_w[n] for n in TWIN_WEIGHTS],
            *[new_m[n] for n in TWIN_WEIGHTS], *[new_v[n] for n in TWIN_WEIGHTS])
```

```python
import jax
import jax.numpy as jnp
from jax import lax
from jax.experimental import pallas as pl
from jax.experimental.pallas import tpu as pltpu

F32 = jnp.float32
BF16 = jnp.bfloat16
SDS = jax.ShapeDtypeStruct

N_DEV = 8
HEAD_DIM = 64
BLOCK = 128
DILATIONS = (1, 4, 16)
N_CHUNK = 4
EPS = 1e-6
NEG = -1e30
SM_SCALE = HEAD_DIM ** -0.5
LANES = 128
TOKEN_TILE = 512
VMEM_LIMIT_BYTES = 56 * 1024 * 1024

ADAM_LR = 0.001
ADAM_B1 = 0.9
ADAM_B2 = 0.999
ADAM_EPS = 1e-08
ADAM_WD = 0.01
ADAM_STEP = 10

MESH_ID = pl.DeviceIdType.MESH
ANY = pl.BlockSpec(memory_space=pl.ANY)


def _params(n_axes):
    return pltpu.CompilerParams(dimension_semantics=("arbitrary",) * n_axes,
                                vmem_limit_bytes=VMEM_LIMIT_BYTES)


def _dot(a, b):
    return jnp.dot(a, b, preferred_element_type=F32)


def _dot_nt(a, b):
    return lax.dot_general(a, b, (((1,), (1,)), ((), ())), preferred_element_type=F32)


def _dot_tn(a, b):
    return lax.dot_general(a, b, (((0,), (0,)), ((), ())), preferred_element_type=F32)


def _split_bf16(t):
    hi = t.astype(BF16)
    return hi, (t - hi.astype(F32)).astype(BF16)


def _my_place():
    x, y, c = lax.axis_index("x"), lax.axis_index("y"), lax.axis_index("c")
    return x, y, c


def _peer(place, k):
    x, y, c = place
    return ((1 - x) if k & 4 else x, (1 - y) if k & 2 else y, (1 - c) if k & 1 else c)


def _index(place):
    return 4 * place[0] + 2 * place[1] + place[2]


def _allgather_small(v, name):
    rows, cols = v.shape

    def body(x_ref, out_ref, send_sems, recv_sems, local_sem):
        me = _my_place()
        mine = pltpu.make_async_copy(x_ref, out_ref.at[_index(me)], local_sem)
        mine.start()
        sends = []
        for k in range(1, N_DEV):
            cp = pltpu.make_async_remote_copy(
                src_ref=x_ref, dst_ref=out_ref.at[_index(me)], send_sem=send_sems.at[k - 1],
                recv_sem=recv_sems.at[k - 1], device_id=_peer(me, k), device_id_type=MESH_ID)
            cp.start()
            sends.append(cp)
        for k in range(1, N_DEV):
            pltpu.make_async_remote_copy(
                src_ref=x_ref, dst_ref=out_ref.at[_index(_peer(me, k))], send_sem=send_sems.at[k - 1],
                recv_sem=recv_sems.at[k - 1], device_id=_peer(me, k), device_id_type=MESH_ID).wait_recv()
        for cp in sends:
            cp.wait_send()
        mine.wait()

    return pl.pallas_call(
        body, name=name,
        out_shape=SDS((N_DEV, rows, cols), v.dtype),
        in_specs=[pl.BlockSpec(memory_space=pltpu.VMEM)],
        out_specs=pl.BlockSpec(memory_space=pltpu.VMEM),
        scratch_shapes=[pltpu.SemaphoreType.DMA((N_DEV - 1,)), pltpu.SemaphoreType.DMA((N_DEV - 1,)),
                        pltpu.SemaphoreType.DMA],
    )(v)


def _allgather_big(arrs, name):
    n = len(arrs)

    def body(*refs):
        ins, outs = refs[:n], refs[n:2 * n]
        send_sems, recv_sems, local_sems = refs[2 * n:]
        x, y, c = _my_place()
        me, sib = (x, y, c), (x, y, 1 - c)
        chips = [(1 - x, y), (x, 1 - y), (1 - x, 1 - y)]

        def copy(i, k, block, to, src=None):
            dst = outs[i].at[_index(block)]
            return pltpu.make_async_remote_copy(
                src_ref=dst if src is None else src, dst_ref=dst, send_sem=send_sems.at[7 * i + k],
                recv_sem=recv_sems.at[7 * i + k], device_id=to, device_id_type=MESH_ID)

        mine = [pltpu.make_async_copy(ins[i], outs[i].at[_index(me)], local_sems.at[i]) for i in range(n)]
        for cp in mine:
            cp.start()
        first = []
        for i in range(n):
            first.append(copy(i, 0, me, sib, src=ins[i]))
            for j, chip in enumerate(chips):
                first.append(copy(i, 1 + j, me, (*chip, c), src=ins[i]))
        for cp in first:
            cp.start()
        passed = []
        for i in range(n):
            for j, chip in enumerate(chips):
                copy(i, 1 + j, (*chip, c), me).wait_recv()
                fwd = copy(i, 4 + j, (*chip, c), sib)
                fwd.start()
                passed.append(fwd)
        for i in range(n):
            copy(i, 0, sib, me).wait_recv()
            for j, chip in enumerate(chips):
                copy(i, 4 + j, (*chip, 1 - c), me).wait_recv()
        for cp in first + passed:
            cp.wait_send()
        for cp in mine:
            cp.wait()

    return pl.pallas_call(
        body, name=name,
        out_shape=[SDS((N_DEV,) + a.shape, a.dtype) for a in arrs],
        in_specs=[ANY] * n, out_specs=[ANY] * n,
        scratch_shapes=[pltpu.SemaphoreType.DMA((7 * n,)), pltpu.SemaphoreType.DMA((7 * n,)),
                        pltpu.SemaphoreType.DMA((n,))],
    )(*arrs)


def _exchange_pieces(grads, name):
    n = len(grads)

    def body(*refs):
        ins, outs = refs[:n], refs[n:2 * n]
        send_sems, recv_sems, local_sems = refs[2 * n:]
        me = _my_place()
        mine = [pltpu.make_async_copy(ins[i].at[_index(me)], outs[i].at[_index(me)], local_sems.at[i])
                for i in range(n)]
        for cp in mine:
            cp.start()
        sends = []
        for i in range(n):
            for k in range(1, N_DEV):
                peer = _peer(me, k)
                cp = pltpu.make_async_remote_copy(
                    src_ref=ins[i].at[_index(peer)], dst_ref=outs[i].at[_index(me)],
                    send_sem=send_sems.at[7 * i + k - 1], recv_sem=recv_sems.at[7 * i + k - 1],
                    device_id=peer, device_id_type=MESH_ID)
                cp.start()
                sends.append(cp)
        for i in range(n):
            for k in range(1, N_DEV):
                peer = _peer(me, k)
                pltpu.make_async_remote_copy(
                    src_ref=ins[i].at[_index(peer)], dst_ref=outs[i].at[_index(peer)],
                    send_sem=send_sems.at[7 * i + k - 1], recv_sem=recv_sems.at[7 * i + k - 1],
                    device_id=peer, device_id_type=MESH_ID).wait_recv()
        for cp in sends:
            cp.wait_send()
        for cp in mine:
            cp.wait()

    return pl.pallas_call(
        body, name=name,
        out_shape=[SDS(g.shape, g.dtype) for g in grads],
        in_specs=[ANY] * n, out_specs=[ANY] * n,
        scratch_shapes=[pltpu.SemaphoreType.DMA((7 * n,)), pltpu.SemaphoreType.DMA((7 * n,)),
                        pltpu.SemaphoreType.DMA((n,))],
    )(*grads)


def _ada_fwd(x, prm_ref):
    rstd = lax.rsqrt(jnp.mean(x * x, axis=-1, keepdims=True) + EPS)
    return (x * rstd * prm_ref[0:1, :]) * (1.0 + prm_ref[2:3, :]) + prm_ref[1:2, :]


def _ada_bwd(dh, x, prm_ref):
    ng, ops = prm_ref[0:1, :], 1.0 + prm_ref[2:3, :]
    rstd = lax.rsqrt(jnp.mean(x * x, axis=-1, keepdims=True) + EPS)
    xhat = x * rstd
    dxhat = dh * (ops * ng)
    dx = rstd * (dxhat - xhat * jnp.mean(dxhat * xhat, axis=-1, keepdims=True))
    dshift = jnp.sum(dh, axis=0, keepdims=True)
    dscale = jnp.sum(dh * (xhat * ng), axis=0, keepdims=True)
    dng = jnp.sum(dh * (xhat * ops), axis=0, keepdims=True)
    return dx, dshift, dscale, dng


def _head_mean(t, bd):
    hi, lo = _split_bf16(t)
    return (_dot(hi, bd) + _dot(lo, bd)) * (1.0 / HEAD_DIM)


def _shift_rows(z, before, tm):
    row = lax.broadcasted_iota(jnp.int32, z.shape, 0)
    z1 = jnp.where(row == 0, before[7:8, :], pltpu.roll(z, 1, 0))
    z2 = jnp.where(row == 0, before[6:7, :], jnp.where(row == 1, before[7:8, :], pltpu.roll(z, 2, 0)))
    return z1, z2


def _mod_part(c_all, w_ada, b_loc):
    n_layer, d_model, cols = w_ada.shape

    def body(c_ref, w_ref, b_ref, out_ref, sc_ref):
        cv = c_ref[...]
        sc = cv * jax.nn.sigmoid(cv)
        sc_ref[...] = sc
        a_hi, a_lo = _split_bf16(sc)
        w_hi, w_lo = _split_bf16(w_ref[...])
        out_ref[...] = _dot(a_hi, w_hi) + _dot(a_hi, w_lo) + _dot(a_lo, w_hi) + b_ref[...]

    return pl.pallas_call(
        body, name="mod_part", grid=(n_layer,),
        in_specs=[pl.BlockSpec((N_DEV, d_model), lambda l: (0, 0)),
                  pl.BlockSpec((None, d_model, cols), lambda l: (l, 0, 0)),
                  pl.BlockSpec((None, 1, cols), lambda l: (l, 0, 0))],
        out_specs=[pl.BlockSpec((None, N_DEV, cols), lambda l: (l, 0, 0)),
                   pl.BlockSpec((N_DEV, d_model), lambda l: (0, 0))],
        out_shape=[SDS((n_layer, N_DEV, cols), F32), SDS((N_DEV, d_model), F32)],
        compiler_params=_params(1),
    )(c_all, w_ada, b_loc)


def _w_ada_grad(sc_t, dmod_loc):
    d_model = sc_t.shape[0]
    n_layer, _, cols = dmod_loc.shape

    def body(s_ref, d_ref, out_ref):
        acc = s_ref[:, 0:1] * d_ref[0:1, :]
        for b in range(1, N_DEV):
            acc = acc + s_ref[:, b:b + 1] * d_ref[b:b + 1, :]
        out_ref[...] = acc

    return pl.pallas_call(
        body, name="w_ada_grad", grid=(n_layer,),
        in_specs=[pl.BlockSpec((d_model, N_DEV), lambda l: (0, 0)),
                  pl.BlockSpec((None, N_DEV, cols), lambda l: (l, 0, 0))],
        out_specs=pl.BlockSpec((None, d_model, cols), lambda l: (l, 0, 0)),
        out_shape=SDS((n_layer, d_model, cols), F32),
        compiler_params=_params(1),
    )(sc_t, dmod_loc)


def _sum_devices(g):
    _, rows, cols = g.shape

    def body(g_ref, out_ref):
        acc = g_ref[0]
        for i in range(1, N_DEV):
            acc = acc + g_ref[i]
        out_ref[...] = acc

    return pl.pallas_call(body, name="sum_devices", out_shape=SDS((rows, cols), F32))(g)


def _ffn_fwd(x, prm, w1r, w2g, l, sub, tm):
    t_len, d_model = x.shape
    f8, r8 = w1r.shape[-1], w2g.shape[-2]

    def body(x_ref, prm_ref, w1_ref, w2_ref, gu_ref, h_ref, y_ref, xo_ref, h_sc, acc):
        j = pl.program_id(1)

        @pl.when(j == 0)
        def _():
            h = _ada_fwd(x_ref[...], prm_ref).astype(BF16)
            h_sc[...] = h
            h_ref[...] = h
            acc[...] = jnp.zeros_like(acc)

        hb = h_sc[...]
        g = _dot(hb, w1_ref[0])
        u = _dot(hb, w1_ref[1])
        gu_ref[0] = g.astype(BF16)
        gu_ref[1] = u.astype(BF16)
        s = (g * jax.nn.sigmoid(g) * u).astype(BF16)
        acc[...] += _dot(s, w2_ref[...].reshape(2 * r8, d_model))

        @pl.when(j == N_CHUNK - 1)
        def _():
            yv = acc[...]
            y_ref[...] = yv.astype(BF16)
            xo_ref[...] = x_ref[...] + (0.5 * prm_ref[3:4, :]) * yv

    tile = pl.BlockSpec((tm, d_model), lambda m, j: (m, 0))
    return pl.pallas_call(
        body, name=f"ffn_fwd_{l}_{sub}", grid=(t_len // tm, N_CHUNK),
        in_specs=[tile, pl.BlockSpec((8, d_model), lambda m, j: (0, 0)),
                  pl.BlockSpec((2, None, None, None, d_model, f8), lambda m, j: (0, j, l, sub, 0, 0)),
                  pl.BlockSpec((2, None, None, r8, d_model), lambda m, j: (j, l, sub, 0, 0))],
        out_specs=[pl.BlockSpec((2, None, tm, f8), lambda m, j: (0, j, m, 0)), tile, tile, tile],
        out_shape=[SDS((2, N_CHUNK, t_len, f8), BF16), SDS((t_len, d_model), BF16),
                   SDS((t_len, d_model), BF16), SDS((t_len, d_model), F32)],
        scratch_shapes=[pltpu.VMEM((tm, d_model), BF16), pltpu.VMEM((tm, d_model), F32)],
        compiler_params=_params(2),
    )(x, prm, w1r, w2g)


def _ffn_bwd(dxo, y, x_in, prm, gu, w1r, w2g, l, sub, tm):
    t_len, d_model = x_in.shape
    f8, r8 = w1r.shape[-1], w2g.shape[-2]
    n_m = t_len // tm

    def body(dxo_ref, y_ref, x_ref, prm_ref, gu_ref, w1_ref, w2_ref,
             dx_ref, da_ref, s_ref, dy_ref, st_ref, dy_sc, acc):
        m, j = pl.program_id(0), pl.program_id(1)

        @pl.when((m == 0) & (j == 0))
        def _():
            st_ref[...] = jnp.zeros_like(st_ref)

        @pl.when(j == 0)
        def _():
            dxo_v = dxo_ref[...]
            dy = ((0.5 * prm_ref[3:4, :]) * dxo_v).astype(BF16)
            dy_sc[...] = dy
            dy_ref[...] = dy
            st_ref[3:4, :] += jnp.sum(0.5 * dxo_v * y_ref[...].astype(F32), axis=0, keepdims=True)
            acc[...] = jnp.zeros_like(acc)

        ds = _dot_nt(dy_sc[...], w2_ref[...].reshape(2 * r8, d_model))
        g = gu_ref[0].astype(F32)
        u = gu_ref[1].astype(F32)
        sig = jax.nn.sigmoid(g)
        sil = g * sig
        s_ref[...] = (sil * u).astype(BF16)
        dg = (ds * u * (sig * (1.0 + g * (1.0 - sig)))).astype(BF16)
        du = (ds * sil).astype(BF16)
        da_ref[0] = dg
        da_ref[1] = du
        acc[...] += _dot_nt(dg, w1_ref[0]) + _dot_nt(du, w1_ref[1])

        @pl.when(j == N_CHUNK - 1)
        def _():
            dx, dshift, dscale, dng = _ada_bwd(acc[...], x_ref[...], prm_ref)
            dx_ref[...] = dxo_ref[...] + dx
            st_ref[0:1, :] += dshift
            st_ref[1:2, :] += dscale
            st_ref[2:3, :] += dng

    tile = pl.BlockSpec((tm, d_model), lambda m, j: (m, 0))
    fixed = pl.BlockSpec((8, d_model), lambda m, j: (0, 0))
    return pl.pallas_call(
        body, name=f"ffn_bwd_{l}_{sub}", grid=(n_m, N_CHUNK),
        in_specs=[tile, tile, tile, fixed,
                  pl.BlockSpec((2, None, tm, f8), lambda m, j: (0, j, m, 0)),
                  pl.BlockSpec((2, None, None, None, d_model, f8), lambda m, j: (0, j, l, sub, 0, 0)),
                  pl.BlockSpec((2, None, None, r8, d_model), lambda m, j: (j, l, sub, 0, 0))],
        out_specs=[tile, pl.BlockSpec((2, None, tm, f8), lambda m, j: (0, j, m, 0)),
                   pl.BlockSpec((None, tm, f8), lambda m, j: (j, m, 0)), tile, fixed],
        out_shape=[SDS((t_len, d_model), F32), SDS((2, N_CHUNK, t_len, f8), BF16),
                   SDS((N_CHUNK, t_len, f8), BF16), SDS((t_len, d_model), BF16), SDS((8, d_model), F32)],
        scratch_shapes=[pltpu.VMEM((tm, d_model), BF16), pltpu.VMEM((tm, d_model), F32)],
        compiler_params=_params(2),
    )(dxo, y, x_in, prm, gu, w1r, w2g)


def _matmul_tn(a, b, a_spec, b_spec, out_shape, out_spec, n_piece, t_len, tk, name):
    def body(a_ref, b_ref, o_ref, acc):
        k = pl.program_id(1)

        @pl.when(k == 0)
        def _():
            acc[...] = jnp.zeros_like(acc)

        acc[...] += _dot_tn(a_ref[...], b_ref[...])

        @pl.when(k == pl.num_programs(1) - 1)
        def _():
            o_ref[...] = acc[...].astype(o_ref.dtype)

    acc_shape = tuple(d for d in out_spec.block_shape if d is not None)
    return pl.pallas_call(
        body, name=name, grid=(n_piece, t_len // tk),
        in_specs=[a_spec, b_spec], out_specs=out_spec, out_shape=out_shape,
        scratch_shapes=[pltpu.VMEM(acc_shape, F32)],
        compiler_params=_params(2),
    )(a, b)


def _mixer_in(x, prm, wing, qkg, cprm, bd, l, tm):
    t_len, d_model = x.shape
    a_w = d_model // 2
    c_w = d_model - a_w
    c8 = wing.shape[-1]
    p_c = N_DEV * c8

    def body(x_ref, prm_ref, w_ref, qkg_ref, cprm_ref, bd_ref, proj_ref, qkn_ref, h_ref, ycv_ref, pf, carry):
        @pl.when(pl.program_id(0) == 0)
        def _():
            carry[...] = jnp.zeros_like(carry)

        h = _ada_fwd(x_ref[...], prm_ref).astype(BF16)
        h_ref[...] = h
        for j in range(N_DEV):
            pf[:, j * c8:(j + 1) * c8] = _dot(h, w_ref[j])
        proj_ref[...] = pf[...].astype(BF16)
        bdm = bd_ref[...]
        for i in range(2):
            t = pf[:, i * a_w:(i + 1) * a_w]
            rstd = lax.rsqrt(_head_mean(t * t, bdm) + EPS)
            qkn_ref[:, i * a_w:(i + 1) * a_w] = (t * rstd * qkg_ref[i:i + 1, :]).astype(BF16)
        gb = pf[:, 3 * a_w:3 * a_w + c_w]
        z = pf[:, 3 * a_w + c_w:3 * a_w + 2 * c_w] * pf[:, 3 * a_w + 2 * c_w:3 * a_w + 3 * c_w]
        z1, z2 = _shift_rows(z, carry[...], tm)
        cv = cprm_ref[0:1, :] * z2 + cprm_ref[1:2, :] * z1 + cprm_ref[2:3, :] * z + cprm_ref[3:4, :]
        ycv_ref[...] = (gb * cv).astype(BF16)
        carry[...] = z[tm - 8:tm, :]

    return pl.pallas_call(
        body, name=f"mixer_in_{l}", grid=(t_len // tm,),
        in_specs=[pl.BlockSpec((tm, d_model), lambda m: (m, 0)),
                  pl.BlockSpec((8, d_model), lambda m: (0, 0)),
                  pl.BlockSpec((N_DEV, None, d_model, c8), lambda m: (0, l, 0, 0)),
                  pl.BlockSpec((8, a_w), lambda m: (0, 0)),
                  pl.BlockSpec((8, c_w), lambda m: (0, 0)),
                  pl.BlockSpec((a_w, a_w), lambda m: (0, 0))],
        out_specs=[pl.BlockSpec((tm, p_c), lambda m: (m, 0)),
                   pl.BlockSpec((tm, 2 * a_w), lambda m: (m, 0)),
                   pl.BlockSpec((tm, d_model), lambda m: (m, 0)),
                   pl.BlockSpec((tm, c_w), lambda m: (m, 0))],
        out_shape=[SDS((t_len, p_c), BF16), SDS((t_len, 2 * a_w), BF16),
                   SDS((t_len, d_model), BF16), SDS((t_len, c_w), BF16)],
        scratch_shapes=[pltpu.VMEM((tm, p_c), F32), pltpu.VMEM((8, c_w), F32)],
        compiler_params=_params(1),
    )(x, prm, wing, qkg, cprm, bd)


def _band_mask(first_block):
    qi = lax.broadcasted_iota(jnp.int32, (BLOCK, 2 * BLOCK), 0)
    kj = lax.broadcasted_iota(jnp.int32, (BLOCK, 2 * BLOCK), 1)
    lowest = jnp.where(first_block, BLOCK, 0)
    return (kj >= qi) & (kj <= qi + BLOCK) & (kj >= lowest)


def _attn_fwd(qkn, proj, dil, l):
    t_len, a2 = qkn.shape
    a_w = a2 // 2
    p_c = proj.shape[1]
    pb = p_c // a_w
    seq = t_len // dil
    n_b = seq // BLOCK
    qv = qkn.reshape(seq, dil * a2)
    pv = proj.reshape(seq, dil * p_c)

    def body(q_ref, kc_ref, kp_ref, vc_ref, vp_ref, o_ref, lse_ref):
        valid = _band_mask(pl.program_id(1) == 0)
        low = lax.broadcasted_iota(jnp.int32, (BLOCK, LANES), 1) < HEAD_DIM
        for p in range(a_w // LANES):
            sl = slice(p * LANES, (p + 1) * LANES)
            qp = q_ref[:, sl]
            k2 = jnp.concatenate([kp_ref[:, sl], kc_ref[:, sl]], axis=0)
            v2 = jnp.concatenate([vp_ref[:, sl], vc_ref[:, sl]], axis=0)
            o_h, lse_h = [], []
            for half in (low, ~low):
                s = _dot_nt(jnp.where(half, qp, jnp.zeros_like(qp)), k2) * SM_SCALE
                s = jnp.where(valid, s, NEG)
                mx = jnp.max(s, axis=-1, keepdims=True)
                e = jnp.exp(s - mx)
                den = jnp.sum(e, axis=-1, keepdims=True)
                o_h.append(_dot((e / den).astype(BF16), v2))
                lse_h.append(mx + jnp.log(den))
            o_ref[:, sl] = jnp.where(low, o_h[0], o_h[1]).astype(BF16)
            lse_ref[:, sl] = jnp.where(low, lse_h[0], lse_h[1])

    def blk(f):
        return pl.BlockSpec((BLOCK, a_w), f)

    return pl.pallas_call(
        body, name=f"attn_fwd_{l}_d{dil}", grid=(dil, n_b),
        in_specs=[blk(lambda r, b: (b, 2 * r)),
                  blk(lambda r, b: (b, 2 * r + 1)),
                  blk(lambda r, b: (jnp.maximum(b - 1, 0), 2 * r + 1)),
                  blk(lambda r, b: (b, pb * r + 2)),
                  blk(lambda r, b: (jnp.maximum(b - 1, 0), pb * r + 2))],
        out_specs=[blk(lambda r, b: (b, r)), blk(lambda r, b: (b, r))],
        out_shape=[SDS((seq, dil * a_w), BF16), SDS((seq, dil * a_w), F32)],
        compiler_params=_params(2),
    )(qv, qv, qv, pv, pv)


def _attn_combine(outs, lses, ycv, tm, l):
    t_len, a_w = lses[0].shape
    c_w = ycv.shape[1]

    def body(o0, o1, o2, l0, l1, l2, y_ref, cat_ref, lse_ref):
        ls = [l0[...], l1[...], l2[...]]
        mx = jnp.maximum(jnp.maximum(ls[0], ls[1]), ls[2])
        es = [jnp.exp(t - mx) for t in ls]
        den = es[0] + es[1] + es[2]
        ya = ((es[0] / den) * o0[...].astype(F32) + (es[1] / den) * o1[...].astype(F32)
              + (es[2] / den) * o2[...].astype(F32))
        cat_ref[:, :a_w] = ya.astype(BF16)
        cat_ref[:, a_w:] = y_ref[...]
        lse_ref[...] = mx + jnp.log(den)

    ta = pl.BlockSpec((tm, a_w), lambda m: (m, 0))
    return pl.pallas_call(
        body, name=f"attn_combine_{l}", grid=(t_len // tm,),
        in_specs=[ta] * 6 + [pl.BlockSpec((tm, c_w), lambda m: (m, 0))],
        out_specs=[pl.BlockSpec((tm, a_w + c_w), lambda m: (m, 0)), ta],
        out_shape=[SDS((t_len, a_w + c_w), BF16), SDS((t_len, a_w), F32)],
        compiler_params=_params(1),
    )(*[o.reshape(t_len, a_w) for o in outs], *[t.reshape(t_len, a_w) for t in lses], ycv)


def _mixer_out(cat, x, prm, woutg, l, tm):
    t_len, d_model = x.shape
    r8 = woutg.shape[-2]

    def body(cat_ref, x_ref, prm_ref, w_ref, xo_ref, y_ref):
        yv = _dot(cat_ref[...], w_ref[...].reshape(N_DEV * r8, d_model))
        y_ref[...] = yv.astype(BF16)
        xo_ref[...] = x_ref[...] + prm_ref[3:4, :] * yv

    tile = pl.BlockSpec((tm, d_model), lambda m: (m, 0))
    return pl.pallas_call(
        body, name=f"mixer_out_{l}", grid=(t_len // tm,),
        in_specs=[tile, tile, pl.BlockSpec((8, d_model), lambda m: (0, 0)),
                  pl.BlockSpec((N_DEV, None, r8, d_model), lambda m: (0, l, 0, 0))],
        out_specs=[tile, tile],
        out_shape=[SDS((t_len, d_model), F32), SDS((t_len, d_model), BF16)],
        compiler_params=_params(1),
    )(cat, x, prm, woutg)


def _mixer_bwd_out(dxo, ymix, prm, woutg, proj, cprm, l, tm):
    t_len, d_model = dxo.shape
    a_w = d_model // 2
    c_w = d_model - a_w
    r8 = woutg.shape[-2]
    n_m = t_len // tm
    hb = tm // 8

    def body(dxo_ref, ym_ref, prm_ref, w_ref, pc_ref, halo_ref, cprm_ref,
             dya_ref, dpc_ref, dy_ref, st_ref, cs_ref, carry):
        step = pl.program_id(0)
        tile_i = n_m - 1 - step

        @pl.when(step == 0)
        def _():
            st_ref[...] = jnp.zeros_like(st_ref)
            cs_ref[...] = jnp.zeros_like(cs_ref)
            carry[...] = jnp.zeros_like(carry)

        dxo_v = dxo_ref[...]
        dy = (prm_ref[3:4, :] * dxo_v).astype(BF16)
        dy_ref[...] = dy
        st_ref[3:4, :] += jnp.sum(dxo_v * ym_ref[...].astype(F32), axis=0, keepdims=True)
        dcat = _dot_nt(dy, w_ref[...].reshape(N_DEV * r8, d_model))
        dya_ref[...] = dcat[:, :a_w].astype(BF16)
        dyc = dcat[:, a_w:]
        gb = pc_ref[:, :c_w].astype(F32)
        gc = pc_ref[:, c_w:2 * c_w].astype(F32)
        u = pc_ref[:, 2 * c_w:].astype(F32)
        z = gc * u
        before = halo_ref[:, c_w:2 * c_w].astype(F32) * halo_ref[:, 2 * c_w:].astype(F32)
        before = jnp.where(tile_i > 0, before, jnp.zeros_like(before))
        z1, z2 = _shift_rows(z, before, tm)
        w0, w1, w2 = cprm_ref[0:1, :], cprm_ref[1:2, :], cprm_ref[2:3, :]
        cv = w0 * z2 + w1 * z1 + w2 * z + cprm_ref[3:4, :]
        dcv = dyc * gb
        cs_ref[0:1, :] += jnp.sum(dcv * z2, axis=0, keepdims=True)
        cs_ref[1:2, :] += jnp.sum(dcv * z1, axis=0, keepdims=True)
        cs_ref[2:3, :] += jnp.sum(dcv * z, axis=0, keepdims=True)
        cs_ref[3:4, :] += jnp.sum(dcv, axis=0, keepdims=True)
        row = lax.broadcasted_iota(jnp.int32, dcv.shape, 0)
        after = carry[...]
        d1 = jnp.where(row == tm - 1, after[0:1, :], pltpu.roll(dcv, tm - 1, 0))
        d2 = jnp.where(row == tm - 2, after[0:1, :],
                       jnp.where(row == tm - 1, after[1:2, :], pltpu.roll(dcv, tm - 2, 0)))
        dz = w2 * dcv + w1 * d1 + w0 * d2
        dpc_ref[:, :c_w] = (dyc * cv).astype(BF16)
        dpc_ref[:, c_w:2 * c_w] = (dz * u).astype(BF16)
        dpc_ref[:, 2 * c_w:] = (dz * gc).astype(BF16)
        carry[...] = dcv[0:8, :]

    def rev(width, col=0):
        return pl.BlockSpec((tm, width), lambda s: (n_m - 1 - s, col))

    fixed_d = pl.BlockSpec((8, d_model), lambda s: (0, 0))
    fixed_c = pl.BlockSpec((8, c_w), lambda s: (0, 0))
    return pl.pallas_call(
        body, name=f"mixer_bwd_out_{l}", grid=(n_m,),
        in_specs=[rev(d_model), rev(d_model), fixed_d,
                  pl.BlockSpec((N_DEV, None, r8, d_model), lambda s: (0, l, 0, 0)),
                  rev(3 * c_w, 1),
                  pl.BlockSpec((8, 3 * c_w), lambda s: (jnp.maximum((n_m - 1 - s) * hb - 1, 0), 1)),
                  fixed_c],
        out_specs=[rev(a_w), rev(3 * c_w), rev(d_model), fixed_d, fixed_c],
        out_shape=[SDS((t_len, a_w), BF16), SDS((t_len, 3 * c_w), BF16), SDS((t_len, d_model), BF16),
                   SDS((8, d_model), F32), SDS((8, c_w), F32)],
        scratch_shapes=[pltpu.VMEM((8, c_w), F32)],
        compiler_params=_params(1),
    )(dxo, ymix, prm, woutg, proj, proj, cprm)


def _attn_bwd(qkn, proj, dya, cat, lse, dil, l):
    t_len, a2 = qkn.shape
    a_w = a2 // 2
    p_c = proj.shape[1]
    pb = p_c // a_w
    d_model = cat.shape[1]
    cb = d_model // a_w
    seq = t_len // dil
    n_b = seq // BLOCK
    qv = qkn.reshape(seq, dil * a2)
    pv = proj.reshape(seq, dil * p_c)
    dov = dya.reshape(seq, dil * a_w)
    cv = cat.reshape(seq, dil * d_model)
    lv = lse.reshape(seq, dil * a_w)

    def body(q_ref, kc_ref, kp_ref, vc_ref, vp_ref, do_ref, ya_ref, lse_ref,
             dq_ref, dk_ref, dv_ref, ck, cvv):
        j = pl.program_id(1)

        @pl.when(j < n_b)
        def _():
            valid = _band_mask(j == 0)
            low = lax.broadcasted_iota(jnp.int32, (BLOCK, LANES), 1) < HEAD_DIM
            for p in range(a_w // LANES):
                sl = slice(p * LANES, (p + 1) * LANES)
                qp = q_ref[:, sl]
                k2 = jnp.concatenate([kp_ref[:, sl], kc_ref[:, sl]], axis=0)
                v2 = jnp.concatenate([vp_ref[:, sl], vc_ref[:, sl]], axis=0)
                dop = do_ref[:, sl]
                prod = dop.astype(F32) * ya_ref[:, sl].astype(F32)
                lsep = lse_ref[:, sl]
                dq_h = []
                dk2 = jnp.zeros((2 * BLOCK, LANES), F32)
                dv2 = jnp.zeros((2 * BLOCK, LANES), F32)
                for hh, half in enumerate((low, ~low)):
                    qh = jnp.where(half, qp, jnp.zeros_like(qp))
                    doh = jnp.where(half, dop, jnp.zeros_like(dop))
                    dsum = jnp.sum(jnp.where(half, prod, 0.0), axis=-1, keepdims=True)
                    lse_h = lsep[:, hh * HEAD_DIM:hh * HEAD_DIM + 1]
                    s = _dot_nt(qh, k2) * SM_SCALE
                    pr = jnp.exp(jnp.where(valid, s, NEG) - lse_h)
                    dp = _dot_nt(doh, v2)
                    ds = (pr * (dp - dsum) * SM_SCALE).astype(BF16)
                    dq_h.append(_dot(ds, k2))
                    dk2 = dk2 + _dot_tn(ds, qh)
                    dv2 = dv2 + _dot_tn(pr.astype(BF16), doh)
                dq_ref[:, sl] = jnp.where(low, dq_h[0], dq_h[1]).astype(BF16)

                @pl.when(j > 0)
                def _():
                    dk_ref[:, sl] = (ck[:, sl] + dk2[:BLOCK]).astype(BF16)
                    dv_ref[:, sl] = (cvv[:, sl] + dv2[:BLOCK]).astype(BF16)

                ck[:, sl] = dk2[BLOCK:]
                cvv[:, sl] = dv2[BLOCK:]

        @pl.when(j == n_b)
        def _():
            dk_ref[...] = ck[...].astype(BF16)
            dv_ref[...] = cvv[...].astype(BF16)

    def blk(f):
        return pl.BlockSpec((BLOCK, a_w), f)

    def cur(j):
        return jnp.minimum(j, n_b - 1)

    def prev(j):
        return jnp.maximum(jnp.minimum(j, n_b - 1) - 1, 0)

    def late(j):
        return jnp.maximum(j - 1, 0)

    return pl.pallas_call(
        body, name=f"attn_bwd_{l}_d{dil}", grid=(dil, n_b + 1),
        in_specs=[blk(lambda r, j: (cur(j), 2 * r)),
                  blk(lambda r, j: (cur(j), 2 * r + 1)),
                  blk(lambda r, j: (prev(j), 2 * r + 1)),
                  blk(lambda r, j: (cur(j), pb * r + 2)),
                  blk(lambda r, j: (prev(j), pb * r + 2)),
                  blk(lambda r, j: (cur(j), r)),
                  blk(lambda r, j: (cur(j), cb * r)),
                  blk(lambda r, j: (cur(j), r))],
        out_specs=[blk(lambda r, j: (cur(j), r)), blk(lambda r, j: (late(j), r)), blk(lambda r, j: (late(j), r))],
        out_shape=[SDS((seq, dil * a_w), BF16)] * 3,
        scratch_shapes=[pltpu.VMEM((BLOCK, a_w), F32), pltpu.VMEM((BLOCK, a_w), F32)],
        compiler_params=_params(2),
    )(qv, qv, qv, pv, pv, dov, cv, lv)


def _mixer_bwd_in(dqs, dks, dvs, proj, dpc, dxo, x_in, prm, wing, qkg, bd, l, tm):
    t_len, d_model = x_in.shape
    a_w = d_model // 2
    c_w = d_model - a_w
    c8 = wing.shape[-1]
    p_c = N_DEV * c8

    def body(q0, q1, q2, k0, k1, k2, v0, v1, v2, pa_ref, dpc_ref, dxo_ref, x_ref, prm_ref, w_ref,
             qkg_ref, bd_ref, dx_ref, dproj_ref, st_ref, qs_ref):
        @pl.when(pl.program_id(0) == 0)
        def _():
            st_ref[...] = jnp.zeros_like(st_ref)
            qs_ref[...] = jnp.zeros_like(qs_ref)

        bdm = bd_ref[...]
        grads = [q0[...].astype(F32) + q1[...].astype(F32) + q2[...].astype(F32),
                 k0[...].astype(F32) + k1[...].astype(F32) + k2[...].astype(F32)]
        for i in range(2):
            t = pa_ref[:, i * a_w:(i + 1) * a_w].astype(F32)
            rstd = lax.rsqrt(_head_mean(t * t, bdm) + EPS)
            that = t * rstd
            qs_ref[i:i + 1, :] += jnp.sum(grads[i] * that, axis=0, keepdims=True)
            dhat = grads[i] * qkg_ref[i:i + 1, :]
            dt = rstd * (dhat - that * _head_mean(dhat * that, bdm))
            dproj_ref[:, i * a_w:(i + 1) * a_w] = dt.astype(BF16)
        dproj_ref[:, 2 * a_w:3 * a_w] = (v0[...].astype(F32) + v1[...].astype(F32)
                                         + v2[...].astype(F32)).astype(BF16)
        dproj_ref[:, 3 * a_w:] = dpc_ref[...]
        dh = _dot_nt(dproj_ref[:, 0:c8], w_ref[0])
        for j in range(1, N_DEV):
            dh = dh + _dot_nt(dproj_ref[:, j * c8:(j + 1) * c8], w_ref[j])
        dx, dshift, dscale, dng = _ada_bwd(dh, x_ref[...], prm_ref)
        dx_ref[...] = dxo_ref[...] + dx
        st_ref[0:1, :] += dshift
        st_ref[1:2, :] += dscale
        st_ref[2:3, :] += dng

    ta = pl.BlockSpec((tm, a_w), lambda m: (m, 0))
    tile = pl.BlockSpec((tm, d_model), lambda m: (m, 0))
    return pl.pallas_call(
        body, name=f"mixer_bwd_in_{l}", grid=(t_len // tm,),
        in_specs=[ta] * 9 + [pl.BlockSpec((tm, 3 * a_w), lambda m: (m, 0)),
                             pl.BlockSpec((tm, 3 * c_w), lambda m: (m, 0)),
                             tile, tile, pl.BlockSpec((8, d_model), lambda m: (0, 0)),
                             pl.BlockSpec((N_DEV, None, d_model, c8), lambda m: (0, l, 0, 0)),
                             pl.BlockSpec((8, a_w), lambda m: (0, 0)),
                             pl.BlockSpec((a_w, a_w), lambda m: (0, 0))],
        out_specs=[tile, pl.BlockSpec((tm, p_c), lambda m: (m, 0)),
                   pl.BlockSpec((8, d_model), lambda m: (0, 0)), pl.BlockSpec((8, a_w), lambda m: (0, 0))],
        out_shape=[SDS((t_len, d_model), F32), SDS((t_len, p_c), BF16), SDS((8, d_model), F32), SDS((8, a_w), F32)],
        compiler_params=_params(1),
    )(*[t.reshape(t_len, a_w) for t in (*dqs, *dks, *dvs)], proj, dpc, dxo, x_in, prm, wing, qkg, bd)


def _loss_head(x, target, tm):
    t_len, d_model = x.shape

    def body(x_ref, t_ref, dx_ref, loss_ref):
        @pl.when(pl.program_id(0) == 0)
        def _():
            loss_ref[...] = jnp.zeros_like(loss_ref)

        diff = x_ref[...] - t_ref[...]
        dx_ref[...] = diff * (1.0 / d_model)
        per_token = jnp.sum(diff * diff, axis=-1, keepdims=True) * (1.0 / d_model)
        loss_ref[...] += 0.5 * jnp.sum(per_token)

    tile = pl.BlockSpec((tm, d_model), lambda m: (m, 0))
    return pl.pallas_call(
        body, name="loss_head", grid=(t_len // tm,),
        in_specs=[tile, tile],
        out_specs=[tile, pl.BlockSpec((8, LANES), lambda m: (0, 0))],
        out_shape=[SDS((t_len, d_model), F32), SDS((8, LANES), F32)],
        compiler_params=_params(1),
    )(x, target)


def _row_tile(rows, cols):
    best = None
    for t in range(16, rows + 1, 16):
        if rows % t == 0 and t * cols * 4 <= (1 << 20):
            best = t
    return best if best is not None else rows


def _adamw(pieces, w, m, v, name):
    n_p, rows, cols = pieces.shape
    tr = _row_tile(rows, cols)
    c1 = 1.0 - ADAM_B1 ** ADAM_STEP
    c2 = 1.0 - ADAM_B2 ** ADAM_STEP

    def body(p_ref, w_ref, m_ref, v_ref, g_ref, d_ref, nm_ref, nv_ref):
        g = p_ref[0].astype(F32)
        for i in range(1, n_p):
            g = g + p_ref[i].astype(F32)
        g_ref[...] = g
        nm = ADAM_B1 * m_ref[...] + (1.0 - ADAM_B1) * g
        nv = ADAM_B2 * v_ref[...] + (1.0 - ADAM_B2) * (g * g)
        nm_ref[...] = nm
        nv_ref[...] = nv
        d_ref[...] = -ADAM_LR * ((nm / c1) / (jnp.sqrt(nv / c2) + ADAM_EPS) + ADAM_WD * w_ref[...])

    tile = pl.BlockSpec((tr, cols), lambda i: (i, 0))
    return pl.pallas_call(
        body, name=name, grid=(rows // tr,),
        in_specs=[pl.BlockSpec((n_p, tr, cols), lambda i: (0, i, 0)), tile, tile, tile],
        out_specs=[tile] * 4,
        out_shape=[SDS((rows, cols), F32)] * 4,
        compiler_params=_params(1),
    )(pieces, w, m, v)


def _pack(vecs):
    flat = jnp.concatenate([v.reshape(-1).astype(F32) for v in vecs])
    rows = -(-flat.shape[0] // (8 * LANES)) * 8
    return jnp.pad(flat, (0, rows * LANES - flat.shape[0])).reshape(rows, LANES)


def _unpack(packed, shapes):
    lead = packed.shape[:-2]
    flat = packed.reshape(lead + (-1,))
    out, off = [], 0
    for s in shapes:
        size = 1
        for d in s:
            size *= d
        out.append(flat[..., off:off + size].reshape(lead + tuple(s)))
        off += size
    return out


def kernel(x, c, w_ada, b_ada, norm_g, w_in, q_norm_g, k_norm_g, conv_w, conv_b, w_out, ffn_w1, ffn_w2, loss_target, m_w_ada, m_b_ada, m_norm_g, m_w_in, m_q_norm_g, m_k_norm_g, m_conv_w, m_conv_b, m_w_out, m_ffn_w1, m_ffn_w2, v_w_ada, v_b_ada, v_norm_g, v_w_in, v_q_norm_g, v_k_norm_g, v_conv_w, v_conv_b, v_w_out, v_ffn_w1, v_ffn_w2):
    _, t_len, d_model = x.shape
    n_layer = w_ada.shape[0]
    a_w = d_model // 2
    c_w = d_model - a_w
    n_head = a_w // HEAD_DIM
    ada_cols = w_ada.shape[-1]
    tm = min(TOKEN_TILE, t_len)
    me = _index(_my_place())
    x2 = x.reshape(t_len, d_model)
    target = loss_target.reshape(t_len, d_model)

    small_shapes = [(d_model,), norm_g.shape, conv_w.shape]
    gathered = _allgather_small(_pack([c, norm_g, conv_w]), "gather_small_inputs")
    c_all, ng_parts, cw_parts = _unpack(gathered, small_shapes)
    norm_g_full = jnp.moveaxis(ng_parts, 0, 2).reshape(n_layer, 3, d_model)
    conv_w_full = jnp.moveaxis(cw_parts, 0, 2).reshape(n_layer, 3, c_w)

    b_loc = lax.dynamic_slice_in_dim(b_ada, me * ada_cols, ada_cols, axis=1).reshape(n_layer, 1, ada_cols)
    mod_part, silu_c = _mod_part(c_all, w_ada, b_loc)
    mod_all = _allgather_small(_pack([mod_part]), "gather_mod")
    mod_all = _unpack(mod_all, [mod_part.shape])[0]
    mod_mine = lax.dynamic_index_in_dim(mod_all, me, axis=2, keepdims=False)
    mod = jnp.moveaxis(mod_mine, 0, 1).reshape(n_layer, 3, 3, d_model)

    def prm_of(l, sub):
        rows = jnp.stack([norm_g_full[l, sub], mod[l, sub, 0], mod[l, sub, 1], mod[l, sub, 2]])
        return jnp.pad(rows, ((0, 4), (0, 0)))

    prm = [[prm_of(l, sub) for sub in range(3)] for l in range(n_layer)]
    qkg = [jnp.pad(jnp.stack([jnp.tile(q_norm_g[l], n_head), jnp.tile(k_norm_g[l], n_head)]), ((0, 6), (0, 0)))
           for l in range(n_layer)]
    cprm = [jnp.pad(jnp.concatenate([conv_w_full[l], conv_b[l][None]]), ((0, 4), (0, 0))) for l in range(n_layer)]
    head_of = jnp.arange(a_w) // HEAD_DIM
    bd = (head_of[:, None] == head_of[None, :]).astype(BF16)

    wing, woutg, w1g, w2g = _allgather_big(
        [w_in.astype(BF16), w_out.astype(BF16), ffn_w1.astype(BF16), ffn_w2.astype(BF16)], "gather_weights")
    w1r = w1g.reshape((2, N_CHUNK) + w1g.shape[1:])
    f8 = ffn_w1.shape[-1]
    r8 = ffn_w2.shape[-2]
    c8 = w_in.shape[-1]
    o8 = w_out.shape[-2]

    saved = []
    xc = x2
    for l in range(n_layer):
        keep = {"x0": xc}
        keep["gu0"], keep["h0"], keep["y0"], xc = _ffn_fwd(xc, prm[l][0], w1r, w2g, l, 0, tm)
        keep["x1"] = xc
        proj, qkn, keep["h1"], ycv = _mixer_in(xc, prm[l][1], wing, qkg[l], cprm[l], bd, l, tm)
        branches = [_attn_fwd(qkn, proj, dil, l) for dil in DILATIONS]
        cat, lse = _attn_combine([b[0] for b in branches], [b[1] for b in branches], ycv, tm, l)
        xc, keep["ymix"] = _mixer_out(cat, xc, prm[l][1], woutg, l, tm)
        keep.update(proj=proj, qkn=qkn, cat=cat, lse=lse)
        keep["x2"] = xc
        keep["gu2"], keep["h2"], keep["y2"], xc = _ffn_fwd(xc, prm[l][2], w1r, w2g, l, 1, tm)
        saved.append(keep)

    dx, loss_blk = _loss_head(xc, target, tm)
    loss = lax.psum(loss_blk[0, 0], ("x", "y", "c"))

    tk = min(TOKEN_TILE, t_len)
    g_w1 = [[None, None] for _ in range(n_layer)]
    g_w2 = [[None, None] for _ in range(n_layer)]
    g_win = [None] * n_layer
    g_wout = [None] * n_layer
    stats = [[None] * 3 for _ in range(n_layer)]
    conv_stats = [None] * n_layer
    qk_stats = [None] * n_layer

    def ffn_backward(dxo, keep, l, sub, which):
        dx_in, da, s_act, dy, st = _ffn_bwd(dxo, keep[f"y{which}"], keep[f"x{which}"], prm[l][which],
                                            keep[f"gu{which}"], w1r, w2g, l, sub, tm)
        g_w1[l][sub] = _matmul_tn(
            keep[f"h{which}"], da.reshape(N_DEV, t_len, f8),
            pl.BlockSpec((tk, d_model), lambda p, k: (k, 0)),
            pl.BlockSpec((None, tk, f8), lambda p, k: (p, k, 0)),
            SDS((N_DEV, d_model, f8), BF16), pl.BlockSpec((None, d_model, f8), lambda p, k: (p, 0, 0)),
            N_DEV, t_len, tk, f"grad_w1_{l}_{sub}")
        g_w2[l][sub] = _matmul_tn(
            s_act, dy,
            pl.BlockSpec((None, tk, f8), lambda p, k: (p, k, 0)),
            pl.BlockSpec((tk, d_model), lambda p, k: (k, 0)),
            SDS((N_CHUNK, f8, d_model), BF16), pl.BlockSpec((None, f8, d_model), lambda p, k: (p, 0, 0)),
            N_CHUNK, t_len, tk, f"grad_w2_{l}_{sub}")
        return dx_in, st

    for l in reversed(range(n_layer)):
        keep = saved[l]
        dx, stats[l][2] = ffn_backward(dx, keep, l, 1, 2)
        dya, dpc, dy, st_gate, conv_stats[l] = _mixer_bwd_out(dx, keep["ymix"], prm[l][1], woutg, keep["proj"],
                                                              cprm[l], l, tm)
        g_wout[l] = _matmul_tn(
            keep["cat"], dy,
            pl.BlockSpec((tk, d_model), lambda p, k: (k, 0)),
            pl.BlockSpec((tk, d_model), lambda p, k: (k, 0)),
            SDS((d_model, d_model), BF16), pl.BlockSpec((d_model, d_model), lambda p, k: (0, 0)),
            1, t_len, tk, f"grad_wout_{l}")
        parts = [_attn_bwd(keep["qkn"], keep["proj"], dya, keep["cat"], keep["lse"], dil, l) for dil in DILATIONS]
        dx, dproj, st_norm, qk_stats[l] = _mixer_bwd_in(
            [p[0] for p in parts], [p[1] for p in parts], [p[2] for p in parts],
            keep["proj"], dpc, dx, keep["x1"], prm[l][1], wing, qkg[l], bd, l, tm)
        stats[l][1] = st_norm + st_gate
        g_win[l] = _matmul_tn(
            keep["h1"], dproj,
            pl.BlockSpec((tk, d_model), lambda p, k: (k, 0)),
            pl.BlockSpec((tk, c8), lambda p, k: (k, p)),
            SDS((N_DEV, d_model, c8), BF16), pl.BlockSpec((None, d_model, c8), lambda p, k: (p, 0, 0)),
            N_DEV, t_len, tk, f"grad_win_{l}")
        dx, stats[l][0] = ffn_backward(dx, keep, l, 0, 0)
    grad_x = dx.reshape(x.shape)

    gp_win = jnp.stack(g_win, axis=1)
    gp_wout = jnp.stack([g.reshape(N_DEV, o8, d_model) for g in g_wout], axis=1)
    gp_w1 = jnp.stack([jnp.stack(g, axis=1) for g in g_w1], axis=1)
    gp_w2 = jnp.stack([jnp.stack([g.reshape(N_DEV, r8, d_model) for g in gl], axis=1) for gl in g_w2], axis=1)
    land = _exchange_pieces([gp_win, gp_wout, gp_w1, gp_w2], "exchange_weight_grads")

    def update(pieces, w, m, v, name):
        shape = w.shape
        cols = shape[-1]
        outs = _adamw(pieces.reshape(pieces.shape[0], -1, cols), w.reshape(-1, cols), m.reshape(-1, cols),
                      v.reshape(-1, cols), name)
        return [o.reshape(shape) for o in outs]

    res = {
        "w_in": update(land[0], w_in, m_w_in, v_w_in, "adamw_w_in"),
        "w_out": update(land[1], w_out, m_w_out, v_w_out, "adamw_w_out"),
        "ffn_w1": update(land[2], ffn_w1, m_ffn_w1, v_ffn_w1, "adamw_ffn_w1"),
        "ffn_w2": update(land[3], ffn_w2, m_ffn_w2, v_ffn_w2, "adamw_ffn_w2"),
    }

    dmod = jnp.stack([jnp.stack([stats[l][sub][r] for sub in range(3) for r in (0, 1, 3)]) for l in range(n_layer)])
    dng = jnp.stack([jnp.stack([stats[l][sub][2] for sub in range(3)]) for l in range(n_layer)])
    dcw = jnp.stack([conv_stats[l][0:3] for l in range(n_layer)])
    dcb = jnp.stack([conv_stats[l][3] for l in range(n_layer)])
    dqk = jnp.stack([qk_stats[l][0:2].reshape(2, n_head, HEAD_DIM).sum(axis=1) for l in range(n_layer)])
    part_shapes = [dmod.shape, dng.shape, dcw.shape, dcb.shape, dqk.shape]
    parts_all = _allgather_small(_pack([dmod, dng, dcw, dcb, dqk]), "gather_small_grads")
    dmod_all = _unpack(parts_all, part_shapes)[0].reshape(N_DEV, n_layer, 9 * d_model)
    summed = _unpack(_sum_devices(parts_all), part_shapes)
    g_b_ada = summed[0].reshape(n_layer, 9 * d_model)
    ng_cols = norm_g.shape[-1]
    g_norm_g = lax.dynamic_slice_in_dim(summed[1], me * ng_cols, ng_cols, axis=2)
    cw_cols = conv_w.shape[-1]
    g_conv_w = lax.dynamic_slice_in_dim(summed[2], me * cw_cols, cw_cols, axis=2)
    g_conv_b = summed[3]
    g_q, g_k = summed[4][:, 0], summed[4][:, 1]
    dmod_loc = lax.dynamic_slice_in_dim(dmod_all, me * ada_cols, ada_cols, axis=2)
    g_w_ada = _w_ada_grad(silu_c.T, jnp.moveaxis(dmod_loc, 0, 1))
    res["w_ada"] = update(g_w_ada[None], w_ada, m_w_ada, v_w_ada, "adamw_w_ada")

    names = ["b_ada", "norm_g", "q_norm_g", "k_norm_g", "conv_w", "conv_b"]
    g_small = [g_b_ada, g_norm_g, g_q, g_k, g_conv_w, g_conv_b]
    w_small = [b_ada, norm_g, q_norm_g, k_norm_g, conv_w, conv_b]
    m_small = [m_b_ada, m_norm_g, m_q_norm_g, m_k_norm_g, m_conv_w, m_conv_b]
    v_small = [v_b_ada, v_norm_g, v_q_norm_g, v_k_norm_g, v_conv_w, v_conv_b]
    outs = _adamw(_pack(g_small)[None], _pack(w_small), _pack(m_small), _pack(v_small), "adamw_small")
    shapes = [w.shape for w in w_small]
    unpacked = [_unpack(o, shapes) for o in outs]
    for i, nme in enumerate(names):
        res[nme] = [unpacked[k][i] for k in range(4)]

    order = ["w_ada", "b_ada", "norm_g", "w_in", "q_norm_g", "k_norm_g", "conv_w", "conv_b", "w_out", "ffn_w1", "ffn_w2"]
    return (loss, grad_x, *[res[n][0] for n in order], *[res[n][1] for n in order],
            *[res[n][2] for n in order], *[res[n][3] for n in order])
```

```python
import jax
import jax.numpy as jnp
from jax import lax
from jax.experimental import pallas as pl
from jax.experimental.pallas import tpu as pltpu

F32 = jnp.float32
BF16 = jnp.bfloat16
SDS = jax.ShapeDtypeStruct

N_DEV = 8
HEAD_DIM = 64
BLOCK = 128
DILATIONS = (1, 4, 16)
N_CHUNK = 4
EPS = 1e-6
NEG = -1e30
SM_SCALE = HEAD_DIM ** -0.5
LANES = 128
TOKEN_TILE = 512
VMEM_LIMIT_BYTES = 56 * 1024 * 1024

ADAM_LR = 0.001
ADAM_B1 = 0.9
ADAM_B2 = 0.999
ADAM_EPS = 1e-08
ADAM_WD = 0.01
ADAM_STEP = 10

MESH_ID = pl.DeviceIdType.MESH
ANY = pl.BlockSpec(memory_space=pl.ANY)


def _params(n_axes):
    return pltpu.CompilerParams(dimension_semantics=("arbitrary",) * n_axes,
                                vmem_limit_bytes=VMEM_LIMIT_BYTES)


def _dot(a, b):
    return jnp.dot(a, b, preferred_element_type=F32)


def _dot_nt(a, b):
    return lax.dot_general(a, b, (((1,), (1,)), ((), ())), preferred_element_type=F32)


def _dot_tn(a, b):
    return lax.dot_general(a, b, (((0,), (0,)), ((), ())), preferred_element_type=F32)


def _split_bf16(t):
    hi = t.astype(BF16)
    return hi, (t - hi.astype(F32)).astype(BF16)


def _my_place():
    x, y, c = lax.axis_index("x"), lax.axis_index("y"), lax.axis_index("c")
    return x, y, c


def _peer(place, k):
    x, y, c = place
    return ((1 - x) if k & 4 else x, (1 - y) if k & 2 else y, (1 - c) if k & 1 else c)


def _index(place):
    return 4 * place[0] + 2 * place[1] + place[2]


def _allgather_small(v, name):
    rows, cols = v.shape

    def body(x_ref, out_ref, send_sems, recv_sems, local_sem):
        me = _my_place()
        mine = pltpu.make_async_copy(x_ref, out_ref.at[_index(me)], local_sem)
        mine.start()
        sends = []
        for k in range(1, N_DEV):
            cp = pltpu.make_async_remote_copy(
                src_ref=x_ref, dst_ref=out_ref.at[_index(me)], send_sem=send_sems.at[k - 1],
                recv_sem=recv_sems.at[k - 1], device_id=_peer(me, k), device_id_type=MESH_ID)
            cp.start()
            sends.append(cp)
        for k in range(1, N_DEV):
            pltpu.make_async_remote_copy(
                src_ref=x_ref, dst_ref=out_ref.at[_index(_peer(me, k))], send_sem=send_sems.at[k - 1],
                recv_sem=recv_sems.at[k - 1], device_id=_peer(me, k), device_id_type=MESH_ID).wait_recv()
        for cp in sends:
            cp.wait_send()
        mine.wait()

    return pl.pallas_call(
        body, name=name,
        out_shape=SDS((N_DEV, rows, cols), v.dtype),
        in_specs=[pl.BlockSpec(memory_space=pltpu.VMEM)],
        out_specs=pl.BlockSpec(memory_space=pltpu.VMEM),
        scratch_shapes=[pltpu.SemaphoreType.DMA((N_DEV - 1,)), pltpu.SemaphoreType.DMA((N_DEV - 1,)),
                        pltpu.SemaphoreType.DMA],
    )(v)


def _allgather_big(arrs, name):
    n = len(arrs)

    def body(*refs):
        ins, outs = refs[:n], refs[n:2 * n]
        send_sems, recv_sems, local_sems = refs[2 * n:]
        x, y, c = _my_place()
        me, sib = (x, y, c), (x, y, 1 - c)
        chips = [(1 - x, y), (x, 1 - y), (1 - x, 1 - y)]

        def copy(i, k, block, to, src=None):
            dst = outs[i].at[_index(block)]
            return pltpu.make_async_remote_copy(
                src_ref=dst if src is None else src, dst_ref=dst, send_sem=send_sems.at[7 * i + k],
                recv_sem=recv_sems.at[7 * i + k], device_id=to, device_id_type=MESH_ID)

        mine = [pltpu.make_async_copy(ins[i], outs[i].at[_index(me)], local_sems.at[i]) for i in range(n)]
        for cp in mine:
            cp.start()
        first = []
        for i in range(n):
            first.append(copy(i, 0, me, sib, src=ins[i]))
            for j, chip in enumerate(chips):
                first.append(copy(i, 1 + j, me, (*chip, c), src=ins[i]))
        for cp in first:
            cp.start()
        passed = []
        for i in range(n):
            for j, chip in enumerate(chips):
                copy(i, 1 + j, (*chip, c), me).wait_recv()
                fwd = copy(i, 4 + j, (*chip, c), sib)
                fwd.start()
                passed.append(fwd)
        for i in range(n):
            copy(i, 0, sib, me).wait_recv()
            for j, chip in enumerate(chips):
                copy(i, 4 + j, (*chip, 1 - c), me).wait_recv()
        for cp in first + passed:
            cp.wait_send()
        for cp in mine:
            cp.wait()

    return pl.pallas_call(
        body, name=name,
        out_shape=[SDS((N_DEV,) + a.shape, a.dtype) for a in arrs],
        in_specs=[ANY] * n, out_specs=[ANY] * n,
        scratch_shapes=[pltpu.SemaphoreType.DMA((7 * n,)), pltpu.SemaphoreType.DMA((7 * n,)),
                        pltpu.SemaphoreType.DMA((n,))],
    )(*arrs)


def _exchange_pieces(grads, name):
    n = len(grads)

    def body(*refs):
        ins, outs = refs[:n], refs[n:2 * n]
        send_sems, recv_sems, local_sems = refs[2 * n:]
        me = _my_place()
        mine = [pltpu.make_async_copy(ins[i].at[_index(me)], outs[i].at[_index(me)], local_sems.at[i])
                for i in range(n)]
        for cp in mine:
            cp.start()
        sends = []
        for i in range(n):
            for k in range(1, N_DEV):
                peer = _peer(me, k)
                cp = pltpu.make_async_remote_copy(
                    src_ref=ins[i].at[_index(peer)], dst_ref=outs[i].at[_index(me)],
                    send_sem=send_sems.at[7 * i + k - 1], recv_sem=recv_sems.at[7 * i + k - 1],
                    device_id=peer, device_id_type=MESH_ID)
                cp.start()
                sends.append(cp)
        for i in range(n):
            for k in range(1, N_DEV):
                peer = _peer(me, k)
                pltpu.make_async_remote_copy(
                    src_ref=ins[i].at[_index(peer)], dst_ref=outs[i].at[_index(peer)],
                    send_sem=send_sems.at[7 * i + k - 1], recv_sem=recv_sems.at[7 * i + k - 1],
                    device_id=peer, device_id_type=MESH_ID).wait_recv()
        for cp in sends:
            cp.wait_send()
        for cp in mine:
            cp.wait()

    return pl.pallas_call(
        body, name=name,
        out_shape=[SDS(g.shape, g.dtype) for g in grads],
        in_specs=[ANY] * n, out_specs=[ANY] * n,
        scratch_shapes=[pltpu.SemaphoreType.DMA((7 * n,)), pltpu.SemaphoreType.DMA((7 * n,)),
                        pltpu.SemaphoreType.DMA((n,))],
    )(*grads)


def _ada_fwd(x, prm_ref):
    rstd = lax.rsqrt(jnp.mean(x * x, axis=-1, keepdims=True) + EPS)
    return (x * rstd * prm_ref[0:1, :]) * (1.0 + prm_ref[2:3, :]) + prm_ref[1:2, :]


def _ada_bwd(dh, x, prm_ref):
    ng, ops = prm_ref[0:1, :], 1.0 + prm_ref[2:3, :]
    rstd = lax.rsqrt(jnp.mean(x * x, axis=-1, keepdims=True) + EPS)
    xhat = x * rstd
    dxhat = dh * (ops * ng)
    dx = rstd * (dxhat - xhat * jnp.mean(dxhat * xhat, axis=-1, keepdims=True))
    dshift = jnp.sum(dh, axis=0, keepdims=True)
    dscale = jnp.sum(dh * (xhat * ng), axis=0, keepdims=True)
    dng = jnp.sum(dh * (xhat * ops), axis=0, keepdims=True)
    return dx, dshift, dscale, dng


def _head_mean(t, bd):
    hi, lo = _split_bf16(t)
    return (_dot(hi, bd) + _dot(lo, bd)) * (1.0 / HEAD_DIM)


def _shift_rows(z, before, tm):
    row = lax.broadcasted_iota(jnp.int32, z.shape, 0)
    z1 = jnp.where(row == 0, before[7:8, :], pltpu.roll(z, 1, 0))
    z2 = jnp.where(row == 0, before[6:7, :], jnp.where(row == 1, before[7:8, :], pltpu.roll(z, 2, 0)))
    return z1, z2


def _mod_part(c_all, w_ada, b_loc):
    n_layer, d_model, cols = w_ada.shape

    def body(c_ref, w_ref, b_ref, out_ref, sc_ref):
        cv = c_ref[...]
        sc = cv * jax.nn.sigmoid(cv)
        sc_ref[...] = sc
        a_hi, a_lo = _split_bf16(sc)
        w_hi, w_lo = _split_bf16(w_ref[...])
        out_ref[...] = _dot(a_hi, w_hi) + _dot(a_hi, w_lo) + _dot(a_lo, w_hi) + b_ref[...]

    return pl.pallas_call(
        body, name="mod_part", grid=(n_layer,),
        in_specs=[pl.BlockSpec((N_DEV, d_model), lambda l: (0, 0)),
                  pl.BlockSpec((None, d_model, cols), lambda l: (l, 0, 0)),
                  pl.BlockSpec((None, 1, cols), lambda l: (l, 0, 0))],
        out_specs=[pl.BlockSpec((None, N_DEV, cols), lambda l: (l, 0, 0)),
                   pl.BlockSpec((N_DEV, d_model), lambda l: (0, 0))],
        out_shape=[SDS((n_layer, N_DEV, cols), F32), SDS((N_DEV, d_model), F32)],
        compiler_params=_params(1),
    )(c_all, w_ada, b_loc)


def _w_ada_grad(sc_t, dmod_loc):
    d_model = sc_t.shape[0]
    n_layer, _, cols = dmod_loc.shape

    def body(s_ref, d_ref, out_ref):
        acc = s_ref[:, 0:1] * d_ref[0:1, :]
        for b in range(1, N_DEV):
            acc = acc + s_ref[:, b:b + 1] * d_ref[b:b + 1, :]
        out_ref[...] = acc

    return pl.pallas_call(
        body, name="w_ada_grad", grid=(n_layer,),
        in_specs=[pl.BlockSpec((d_model, N_DEV), lambda l: (0, 0)),
                  pl.BlockSpec((None, N_DEV, cols), lambda l: (l, 0, 0))],
        out_specs=pl.BlockSpec((None, d_model, cols), lambda l: (l, 0, 0)),
        out_shape=SDS((n_layer, d_model, cols), F32),
        compiler_params=_params(1),
    )(sc_t, dmod_loc)


def _sum_devices(g):
    _, rows, cols = g.shape

    def body(g_ref, out_ref):
        acc = g_ref[0]
        for i in range(1, N_DEV):
            acc = acc + g_ref[i]
        out_ref[...] = acc

    return pl.pallas_call(body, name="sum_devices", out_shape=SDS((rows, cols), F32))(g)


def _ffn_fwd(x, prm, w1r, w2g, l, sub, tm):
    t_len, d_model = x.shape
    f8, r8 = w1r.shape[-1], w2g.shape[-2]

    def body(x_ref, prm_ref, w1_ref, w2_ref, gu_ref, h_ref, y_ref, xo_ref, h_sc, acc):
        j = pl.program_id(1)

        @pl.when(j == 0)
        def _():
            h = _ada_fwd(x_ref[...], prm_ref).astype(BF16)
            h_sc[...] = h
            h_ref[...] = h
            acc[...] = jnp.zeros_like(acc)

        hb = h_sc[...]
        g = _dot(hb, w1_ref[0])
        u = _dot(hb, w1_ref[1])
        gu_ref[0] = g.astype(BF16)
        gu_ref[1] = u.astype(BF16)
        s = (g * jax.nn.sigmoid(g) * u).astype(BF16)
        acc[...] += _dot(s, w2_ref[...].reshape(2 * r8, d_model))

        @pl.when(j == N_CHUNK - 1)
        def _():
            yv = acc[...]
            y_ref[...] = yv.astype(BF16)
            xo_ref[...] = x_ref[...] + (0.5 * prm_ref[3:4, :]) * yv

    tile = pl.BlockSpec((tm, d_model), lambda m, j: (m, 0))
    return pl.pallas_call(
        body, name=f"ffn_fwd_{l}_{sub}", grid=(t_len // tm, N_CHUNK),
        in_specs=[tile, pl.BlockSpec((8, d_model), lambda m, j: (0, 0)),
                  pl.BlockSpec((2, None, None, None, d_model, f8), lambda m, j: (0, j, l, sub, 0, 0)),
                  pl.BlockSpec((2, None, None, r8, d_model), lambda m, j: (j, l, sub, 0, 0))],
        out_specs=[pl.BlockSpec((2, None, tm, f8), lambda m, j: (0, j, m, 0)), tile, tile, tile],
        out_shape=[SDS((2, N_CHUNK, t_len, f8), BF16), SDS((t_len, d_model), BF16),
                   SDS((t_len, d_model), BF16), SDS((t_len, d_model), F32)],
        scratch_shapes=[pltpu.VMEM((tm, d_model), BF16), pltpu.VMEM((tm, d_model), F32)],
        compiler_params=_params(2),
    )(x, prm, w1r, w2g)


def _ffn_bwd(dxo, y, x_in, prm, gu, w1r, w2g, l, sub, tm):
    t_len, d_model = x_in.shape
    f8, r8 = w1r.shape[-1], w2g.shape[-2]
    n_m = t_len // tm

    def body(dxo_ref, y_ref, x_ref, prm_ref, gu_ref, w1_ref, w2_ref,
             dx_ref, da_ref, s_ref, dy_ref, st_ref, dy_sc, acc):
        m, j = pl.program_id(0), pl.program_id(1)

        @pl.when((m == 0) & (j == 0))
        def _():
            st_ref[...] = jnp.zeros_like(st_ref)

        @pl.when(j == 0)
        def _():
            dxo_v = dxo_ref[...]
            dy = ((0.5 * prm_ref[3:4, :]) * dxo_v).astype(BF16)
            dy_sc[...] = dy
            dy_ref[...] = dy
            st_ref[3:4, :] += jnp.sum(0.5 * dxo_v * y_ref[...].astype(F32), axis=0, keepdims=True)
            acc[...] = jnp.zeros_like(acc)

        ds = _dot_nt(dy_sc[...], w2_ref[...].reshape(2 * r8, d_model))
        g = gu_ref[0].astype(F32)
        u = gu_ref[1].astype(F32)
        sig = jax.nn.sigmoid(g)
        sil = g * sig
        s_ref[...] = (sil * u).astype(BF16)
        dg = (ds * u * (sig * (1.0 + g * (1.0 - sig)))).astype(BF16)
        du = (ds * sil).astype(BF16)
        da_ref[0] = dg
        da_ref[1] = du
        acc[...] += _dot_nt(dg, w1_ref[0]) + _dot_nt(du, w1_ref[1])

        @pl.when(j == N_CHUNK - 1)
        def _():
            dx, dshift, dscale, dng = _ada_bwd(acc[...], x_ref[...], prm_ref)
            dx_ref[...] = dxo_ref[...] + dx
            st_ref[0:1, :] += dshift
            st_ref[1:2, :] += dscale
            st_ref[2:3, :] += dng

    tile = pl.BlockSpec((tm, d_model), lambda m, j: (m, 0))
    fixed = pl.BlockSpec((8, d_model), lambda m, j: (0, 0))
    return pl.pallas_call(
        body, name=f"ffn_bwd_{l}_{sub}", grid=(n_m, N_CHUNK),
        in_specs=[tile, tile, tile, fixed,
                  pl.BlockSpec((2, None, tm, f8), lambda m, j: (0, j, m, 0)),
                  pl.BlockSpec((2, None, None, None, d_model, f8), lambda m, j: (0, j, l, sub, 0, 0)),
                  pl.BlockSpec((2, None, None, r8, d_model), lambda m, j: (j, l, sub, 0, 0))],
        out_specs=[tile, pl.BlockSpec((2, None, tm, f8), lambda m, j: (0, j, m, 0)),
                   pl.BlockSpec((None, tm, f8), lambda m, j: (j, m, 0)), tile, fixed],
        out_shape=[SDS((t_len, d_model), F32), SDS((2, N_CHUNK, t_len, f8), BF16),
                   SDS((N_CHUNK, t_len, f8), BF16), SDS((t_len, d_model), BF16), SDS((8, d_model), F32)],
        scratch_shapes=[pltpu.VMEM((tm, d_model), BF16), pltpu.VMEM((tm, d_model), F32)],
        compiler_params=_params(2),
    )(dxo, y, x_in, prm, gu, w1r, w2g)


def _matmul_tn(a, b, a_spec, b_spec, out_shape, out_spec, n_piece, t_len, tk, name):
    def body(a_ref, b_ref, o_ref, acc):
        k = pl.program_id(1)

        @pl.when(k == 0)
        def _():
            acc[...] = jnp.zeros_like(acc)

        acc[...] += _dot_tn(a_ref[...], b_ref[...])

        @pl.when(k == pl.num_programs(1) - 1)
        def _():
            o_ref[...] = acc[...].astype(o_ref.dtype)

    acc_shape = tuple(d for d in out_spec.block_shape if d is not None)
    return pl.pallas_call(
        body, name=name, grid=(n_piece, t_len // tk),
        in_specs=[a_spec, b_spec], out_specs=out_spec, out_shape=out_shape,
        scratch_shapes=[pltpu.VMEM(acc_shape, F32)],
        compiler_params=_params(2),
    )(a, b)


def _groups(tm, width):
    return pltpu.VMEM((width // LANES, tm, LANES), F32)


def _to_groups(val, dst_ref, first=0):
    for g in range(val.shape[1] // LANES):
        dst_ref[first + g] = val[:, g * LANES:(g + 1) * LANES]


def _from_groups(src_ref):
    return jnp.concatenate([src_ref[g] for g in range(src_ref.shape[0])], axis=1)


def _split_residues(src_ref, out_ref, dil, tm):
    for g in range(src_ref.shape[0]):
        for r in range(dil):
            out_ref[r, :, g * LANES:(g + 1) * LANES] = (
                src_ref.at[g][pl.ds(r, tm // dil, stride=dil), :].astype(out_ref.dtype))


def _merge_residues(in_ref, dst_ref, dil, tm, add=False):
    for g in range(dst_ref.shape[0]):
        for r in range(dil):
            rows = pl.ds(r, tm // dil, stride=dil)
            val = in_ref[r, :, g * LANES:(g + 1) * LANES].astype(F32)
            dst = dst_ref.at[g]
            dst[rows, :] = (dst[rows, :] + val) if add else val


def _res_spec(dil, tm, width, index=lambda m: m):
    return pl.BlockSpec((dil, tm // dil, width), lambda m: (0, index(m), 0))


def _mixer_in(x, prm, wing, qkg, cprm, bd, l, tm):
    t_len, d_model = x.shape
    a_w = d_model // 2
    c_w = d_model - a_w
    c8 = wing.shape[-1]
    p_c = N_DEV * c8
    dils = DILATIONS[1:]

    def body(x_ref, prm_ref, w_ref, qkg_ref, cprm_ref, bd_ref, proj_ref, qkn_ref, h_ref, ycv_ref, *rest):
        res_refs, (pf, qsc, vsc, carry) = rest[:2 * len(dils)], rest[2 * len(dils):]

        @pl.when(pl.program_id(0) == 0)
        def _():
            carry[...] = jnp.zeros_like(carry)

        h = _ada_fwd(x_ref[...], prm_ref).astype(BF16)
        h_ref[...] = h
        for j in range(N_DEV):
            pf[:, j * c8:(j + 1) * c8] = _dot(h, w_ref[j])
        proj_ref[...] = pf[...].astype(BF16)
        bdm = bd_ref[...]
        for i in range(2):
            t = pf[:, i * a_w:(i + 1) * a_w]
            rstd = lax.rsqrt(_head_mean(t * t, bdm) + EPS)
            tn = t * rstd * qkg_ref[i:i + 1, :]
            qkn_ref[:, i * a_w:(i + 1) * a_w] = tn.astype(BF16)
            _to_groups(tn, qsc, i * (a_w // LANES))
        _to_groups(pf[:, 2 * a_w:3 * a_w], vsc)
        for i, dil in enumerate(dils):
            _split_residues(qsc, res_refs[2 * i], dil, tm)
            _split_residues(vsc, res_refs[2 * i + 1], dil, tm)
        gb = pf[:, 3 * a_w:3 * a_w + c_w]
        z = pf[:, 3 * a_w + c_w:3 * a_w + 2 * c_w] * pf[:, 3 * a_w + 2 * c_w:3 * a_w + 3 * c_w]
        z1, z2 = _shift_rows(z, carry[...], tm)
        cv = cprm_ref[0:1, :] * z2 + cprm_ref[1:2, :] * z1 + cprm_ref[2:3, :] * z + cprm_ref[3:4, :]
        ycv_ref[...] = (gb * cv).astype(BF16)
        carry[...] = z[tm - 8:tm, :]

    res_specs, res_shapes = [], []
    for dil in dils:
        res_specs += [_res_spec(dil, tm, 2 * a_w), _res_spec(dil, tm, a_w)]
        res_shapes += [SDS((dil, t_len // dil, 2 * a_w), BF16), SDS((dil, t_len // dil, a_w), BF16)]
    return pl.pallas_call(
        body, name=f"mixer_in_{l}", grid=(t_len // tm,),
        in_specs=[pl.BlockSpec((tm, d_model), lambda m: (m, 0)),
                  pl.BlockSpec((8, d_model), lambda m: (0, 0)),
                  pl.BlockSpec((N_DEV, None, d_model, c8), lambda m: (0, l, 0, 0)),
                  pl.BlockSpec((8, a_w), lambda m: (0, 0)),
                  pl.BlockSpec((8, c_w), lambda m: (0, 0)),
                  pl.BlockSpec((a_w, a_w), lambda m: (0, 0))],
        out_specs=[pl.BlockSpec((tm, p_c), lambda m: (m, 0)),
                   pl.BlockSpec((tm, 2 * a_w), lambda m: (m, 0)),
                   pl.BlockSpec((tm, d_model), lambda m: (m, 0)),
                   pl.BlockSpec((tm, c_w), lambda m: (m, 0))] + res_specs,
        out_shape=[SDS((t_len, p_c), BF16), SDS((t_len, 2 * a_w), BF16),
                   SDS((t_len, d_model), BF16), SDS((t_len, c_w), BF16)] + res_shapes,
        scratch_shapes=[pltpu.VMEM((tm, p_c), F32), _groups(tm, 2 * a_w), _groups(tm, a_w),
                        pltpu.VMEM((8, c_w), F32)],
        compiler_params=_params(1),
    )(x, prm, wing, qkg, cprm, bd)


def _band_mask(first_block):
    qi = lax.broadcasted_iota(jnp.int32, (BLOCK, 2 * BLOCK), 0)
    kj = lax.broadcasted_iota(jnp.int32, (BLOCK, 2 * BLOCK), 1)
    lowest = jnp.where(first_block, BLOCK, 0)
    return (kj >= qi) & (kj <= qi + BLOCK) & (kj >= lowest)


def _attn_fwd(qk, v, v_col, l):
    dil, seq, a2 = qk.shape
    a_w = a2 // 2
    n_b = seq // BLOCK

    def body(q_ref, kc_ref, kp_ref, vc_ref, vp_ref, o_ref, lse_ref):
        valid = _band_mask(pl.program_id(1) == 0)
        low = lax.broadcasted_iota(jnp.int32, (BLOCK, LANES), 1) < HEAD_DIM
        for p in range(a_w // LANES):
            sl = slice(p * LANES, (p + 1) * LANES)
            qp = q_ref[:, sl]
            k2 = jnp.concatenate([kp_ref[:, sl], kc_ref[:, sl]], axis=0)
            v2 = jnp.concatenate([vp_ref[:, sl], vc_ref[:, sl]], axis=0)
            o_h, lse_h = [], []
            for half in (low, ~low):
                s = _dot_nt(jnp.where(half, qp, jnp.zeros_like(qp)), k2) * SM_SCALE
                s = jnp.where(valid, s, NEG)
                mx = jnp.max(s, axis=-1, keepdims=True)
                e = jnp.exp(s - mx)
                den = jnp.sum(e, axis=-1, keepdims=True)
                o_h.append(_dot((e / den).astype(BF16), v2))
                lse_h.append(mx + jnp.log(den))
            o_ref[:, sl] = jnp.where(low, o_h[0], o_h[1]).astype(BF16)
            lse_ref[:, sl] = jnp.where(low, lse_h[0], lse_h[1])

    def blk(f):
        return pl.BlockSpec((None, BLOCK, a_w), f)

    return pl.pallas_call(
        body, name=f"attn_fwd_{l}_d{dil}", grid=(dil, n_b),
        in_specs=[blk(lambda r, b: (r, b, 0)),
                  blk(lambda r, b: (r, b, 1)),
                  blk(lambda r, b: (r, jnp.maximum(b - 1, 0), 1)),
                  blk(lambda r, b: (r, b, v_col)),
                  blk(lambda r, b: (r, jnp.maximum(b - 1, 0), v_col))],
        out_specs=[blk(lambda r, b: (r, b, 0)), blk(lambda r, b: (r, b, 0))],
        out_shape=[SDS((dil, seq, a_w), BF16), SDS((dil, seq, a_w), F32)],
        compiler_params=_params(2),
    )(qk, qk, qk, v, v)


def _attn_combine(outs, lses, ycv, tm, l):
    t_len, c_w = ycv.shape
    a_w = outs[0].shape[-1]
    dils = DILATIONS[1:]
    n_r = len(dils)

    def body(o1, l1, *rest):
        o_res, l_res, y_ref = rest[:n_r], rest[n_r:2 * n_r], rest[2 * n_r]
        cat_ref, lse_ref = rest[2 * n_r + 1], rest[2 * n_r + 2]
        ya_res, lse_res = rest[2 * n_r + 3:3 * n_r + 3], rest[3 * n_r + 3:4 * n_r + 3]
        scr = rest[4 * n_r + 3:]
        so, sl, sya, slse = scr[:n_r], scr[n_r:2 * n_r], scr[2 * n_r], scr[2 * n_r + 1]
        for i, dil in enumerate(dils):
            _merge_residues(o_res[i], so[i], dil, tm)
            _merge_residues(l_res[i], sl[i], dil, tm)
        ls = [l1[...]] + [_from_groups(s) for s in sl]
        ov = [o1[...].astype(F32)] + [_from_groups(s) for s in so]
        mx = jnp.maximum(jnp.maximum(ls[0], ls[1]), ls[2])
        es = [jnp.exp(t - mx) for t in ls]
        den = es[0] + es[1] + es[2]
        ya = (es[0] / den) * ov[0] + (es[1] / den) * ov[1] + (es[2] / den) * ov[2]
        lse = mx + jnp.log(den)
        _to_groups(ya, sya)
        _to_groups(lse, slse)
        cat_ref[:, :a_w] = ya.astype(BF16)
        cat_ref[:, a_w:] = y_ref[...]
        lse_ref[...] = lse
        for i, dil in enumerate(dils):
            _split_residues(sya, ya_res[i], dil, tm)
            _split_residues(slse, lse_res[i], dil, tm)

    ta = pl.BlockSpec((tm, a_w), lambda m: (m, 0))
    res = [_res_spec(dil, tm, a_w) for dil in dils]
    return pl.pallas_call(
        body, name=f"attn_combine_{l}", grid=(t_len // tm,),
        in_specs=[ta, ta] + res + res + [pl.BlockSpec((tm, c_w), lambda m: (m, 0))],
        out_specs=[pl.BlockSpec((tm, a_w + c_w), lambda m: (m, 0)), ta] + res + res,
        out_shape=[SDS((t_len, a_w + c_w), BF16), SDS((t_len, a_w), F32)]
        + [SDS((dil, t_len // dil, a_w), BF16) for dil in dils]
        + [SDS((dil, t_len // dil, a_w), F32) for dil in dils],
        scratch_shapes=[_groups(tm, a_w)] * (2 * n_r + 2),
        compiler_params=_params(1),
    )(outs[0].reshape(t_len, a_w), lses[0].reshape(t_len, a_w), *outs[1:], *lses[1:], ycv)


def _mixer_out(cat, x, prm, woutg, l, tm):
    t_len, d_model = x.shape
    r8 = woutg.shape[-2]

    def body(cat_ref, x_ref, prm_ref, w_ref, xo_ref, y_ref):
        yv = _dot(cat_ref[...], w_ref[...].reshape(N_DEV * r8, d_model))
        y_ref[...] = yv.astype(BF16)
        xo_ref[...] = x_ref[...] + prm_ref[3:4, :] * yv

    tile = pl.BlockSpec((tm, d_model), lambda m: (m, 0))
    return pl.pallas_call(
        body, name=f"mixer_out_{l}", grid=(t_len // tm,),
        in_specs=[tile, tile, pl.BlockSpec((8, d_model), lambda m: (0, 0)),
                  pl.BlockSpec((N_DEV, None, r8, d_model), lambda m: (0, l, 0, 0))],
        out_specs=[tile, tile],
        out_shape=[SDS((t_len, d_model), F32), SDS((t_len, d_model), BF16)],
        compiler_params=_params(1),
    )(cat, x, prm, woutg)


def _mixer_bwd_out(dxo, ymix, prm, woutg, proj, cprm, l, tm):
    t_len, d_model = dxo.shape
    a_w = d_model // 2
    c_w = d_model - a_w
    r8 = woutg.shape[-2]
    n_m = t_len // tm
    hb = tm // 8

    dils = DILATIONS[1:]

    def body(dxo_ref, ym_ref, prm_ref, w_ref, pc_ref, halo_ref, cprm_ref,
             dya_ref, dpc_ref, dy_ref, st_ref, cs_ref, *rest):
        dya_res, (carry, dsc) = rest[:len(dils)], rest[len(dils):]
        step = pl.program_id(0)
        tile_i = n_m - 1 - step

        @pl.when(step == 0)
        def _():
            st_ref[...] = jnp.zeros_like(st_ref)
            cs_ref[...] = jnp.zeros_like(cs_ref)
            carry[...] = jnp.zeros_like(carry)

        dxo_v = dxo_ref[...]
        dy = (prm_ref[3:4, :] * dxo_v).astype(BF16)
        dy_ref[...] = dy
        st_ref[3:4, :] += jnp.sum(dxo_v * ym_ref[...].astype(F32), axis=0, keepdims=True)
        dcat = _dot_nt(dy, w_ref[...].reshape(N_DEV * r8, d_model))
        _to_groups(dcat[:, :a_w], dsc)
        dya_ref[...] = dcat[:, :a_w].astype(BF16)
        for i, dil in enumerate(dils):
            _split_residues(dsc, dya_res[i], dil, tm)
        dyc = dcat[:, a_w:]
        gb = pc_ref[:, :c_w].astype(F32)
        gc = pc_ref[:, c_w:2 * c_w].astype(F32)
        u = pc_ref[:, 2 * c_w:].astype(F32)
        z = gc * u
        before = halo_ref[:, c_w:2 * c_w].astype(F32) * halo_ref[:, 2 * c_w:].astype(F32)
        before = jnp.where(tile_i > 0, before, jnp.zeros_like(before))
        z1, z2 = _shift_rows(z, before, tm)
        w0, w1, w2 = cprm_ref[0:1, :], cprm_ref[1:2, :], cprm_ref[2:3, :]
        cv = w0 * z2 + w1 * z1 + w2 * z + cprm_ref[3:4, :]
        dcv = dyc * gb
        cs_ref[0:1, :] += jnp.sum(dcv * z2, axis=0, keepdims=True)
        cs_ref[1:2, :] += jnp.sum(dcv * z1, axis=0, keepdims=True)
        cs_ref[2:3, :] += jnp.sum(dcv * z, axis=0, keepdims=True)
        cs_ref[3:4, :] += jnp.sum(dcv, axis=0, keepdims=True)
        row = lax.broadcasted_iota(jnp.int32, dcv.shape, 0)
        after = carry[...]
        d1 = jnp.where(row == tm - 1, after[0:1, :], pltpu.roll(dcv, tm - 1, 0))
        d2 = jnp.where(row == tm - 2, after[0:1, :],
                       jnp.where(row == tm - 1, after[1:2, :], pltpu.roll(dcv, tm - 2, 0)))
        dz = w2 * dcv + w1 * d1 + w0 * d2
        dpc_ref[:, :c_w] = (dyc * cv).astype(BF16)
        dpc_ref[:, c_w:2 * c_w] = (dz * u).astype(BF16)
        dpc_ref[:, 2 * c_w:] = (dz * gc).astype(BF16)
        carry[...] = dcv[0:8, :]

    def rev(width, col=0):
        return pl.BlockSpec((tm, width), lambda s: (n_m - 1 - s, col))

    fixed_d = pl.BlockSpec((8, d_model), lambda s: (0, 0))
    fixed_c = pl.BlockSpec((8, c_w), lambda s: (0, 0))
    return pl.pallas_call(
        body, name=f"mixer_bwd_out_{l}", grid=(n_m,),
        in_specs=[rev(d_model), rev(d_model), fixed_d,
                  pl.BlockSpec((N_DEV, None, r8, d_model), lambda s: (0, l, 0, 0)),
                  rev(3 * c_w, 1),
                  pl.BlockSpec((8, 3 * c_w), lambda s: (jnp.maximum((n_m - 1 - s) * hb - 1, 0), 1)),
                  fixed_c],
        out_specs=[rev(a_w), rev(3 * c_w), rev(d_model), fixed_d, fixed_c]
        + [_res_spec(dil, tm, a_w, lambda s: n_m - 1 - s) for dil in dils],
        out_shape=[SDS((t_len, a_w), BF16), SDS((t_len, 3 * c_w), BF16), SDS((t_len, d_model), BF16),
                   SDS((8, d_model), F32), SDS((8, c_w), F32)]
        + [SDS((dil, t_len // dil, a_w), BF16) for dil in dils],
        scratch_shapes=[pltpu.VMEM((8, c_w), F32), _groups(tm, a_w)],
        compiler_params=_params(1),
    )(dxo, ymix, prm, woutg, proj, proj, cprm)


def _attn_bwd(qk, v, v_col, dya, ya, lse, l):
    dil, seq, a2 = qk.shape
    a_w = a2 // 2
    n_b = seq // BLOCK

    def body(q_ref, kc_ref, kp_ref, vc_ref, vp_ref, do_ref, ya_ref, lse_ref,
             dq_ref, dk_ref, dv_ref, ck, cvv):
        j = pl.program_id(1)

        @pl.when(j < n_b)
        def _():
            valid = _band_mask(j == 0)
            low = lax.broadcasted_iota(jnp.int32, (BLOCK, LANES), 1) < HEAD_DIM
            for p in range(a_w // LANES):
                sl = slice(p * LANES, (p + 1) * LANES)
                qp = q_ref[:, sl]
                k2 = jnp.concatenate([kp_ref[:, sl], kc_ref[:, sl]], axis=0)
                v2 = jnp.concatenate([vp_ref[:, sl], vc_ref[:, sl]], axis=0)
                dop = do_ref[:, sl]
                prod = dop.astype(F32) * ya_ref[:, sl].astype(F32)
                lsep = lse_ref[:, sl]
                dq_h = []
                dk2 = jnp.zeros((2 * BLOCK, LANES), F32)
                dv2 = jnp.zeros((2 * BLOCK, LANES), F32)
                for hh, half in enumerate((low, ~low)):
                    qh = jnp.where(half, qp, jnp.zeros_like(qp))
                    doh = jnp.where(half, dop, jnp.zeros_like(dop))
                    dsum = jnp.sum(jnp.where(half, prod, 0.0), axis=-1, keepdims=True)
                    lse_h = lsep[:, hh * HEAD_DIM:hh * HEAD_DIM + 1]
                    s = _dot_nt(qh, k2) * SM_SCALE
                    pr = jnp.exp(jnp.where(valid, s, NEG) - lse_h)
                    dp = _dot_nt(doh, v2)
                    ds = (pr * (dp - dsum) * SM_SCALE).astype(BF16)
                    dq_h.append(_dot(ds, k2))
                    dk2 = dk2 + _dot_tn(ds, qh)
                    dv2 = dv2 + _dot_tn(pr.astype(BF16), doh)
                dq_ref[:, sl] = jnp.where(low, dq_h[0], dq_h[1]).astype(BF16)

                @pl.when(j > 0)
                def _():
                    dk_ref[:, sl] = (ck[:, sl] + dk2[:BLOCK]).astype(BF16)
                    dv_ref[:, sl] = (cvv[:, sl] + dv2[:BLOCK]).astype(BF16)

                ck[:, sl] = dk2[BLOCK:]
                cvv[:, sl] = dv2[BLOCK:]

        @pl.when(j == n_b)
        def _():
            dk_ref[...] = ck[...].astype(BF16)
            dv_ref[...] = cvv[...].astype(BF16)

    def blk(f):
        return pl.BlockSpec((None, BLOCK, a_w), f)

    def cur(j):
        return jnp.minimum(j, n_b - 1)

    def prev(j):
        return jnp.maximum(jnp.minimum(j, n_b - 1) - 1, 0)

    def late(j):
        return jnp.maximum(j - 1, 0)

    return pl.pallas_call(
        body, name=f"attn_bwd_{l}_d{dil}", grid=(dil, n_b + 1),
        in_specs=[blk(lambda r, j: (r, cur(j), 0)),
                  blk(lambda r, j: (r, cur(j), 1)),
                  blk(lambda r, j: (r, prev(j), 1)),
                  blk(lambda r, j: (r, cur(j), v_col)),
                  blk(lambda r, j: (r, prev(j), v_col)),
                  blk(lambda r, j: (r, cur(j), 0)),
                  blk(lambda r, j: (r, cur(j), 0)),
                  blk(lambda r, j: (r, cur(j), 0))],
        out_specs=[blk(lambda r, j: (r, cur(j), 0)), blk(lambda r, j: (r, late(j), 0)),
                   blk(lambda r, j: (r, late(j), 0))],
        out_shape=[SDS((dil, seq, a_w), BF16)] * 3,
        scratch_shapes=[pltpu.VMEM((BLOCK, a_w), F32), pltpu.VMEM((BLOCK, a_w), F32)],
        compiler_params=_params(2),
    )(qk, qk, qk, v, v, dya, ya, lse)


def _mixer_bwd_in(parts, proj, dpc, dxo, x_in, prm, wing, qkg, bd, l, tm):
    t_len, d_model = x_in.shape
    a_w = d_model // 2
    c_w = d_model - a_w
    c8 = wing.shape[-1]
    p_c = N_DEV * c8

    dils = DILATIONS[1:]
    n_in = 3 * len(DILATIONS)

    def body(*refs):
        nat, res = refs[:3], refs[3:n_in]
        (pa_ref, dpc_ref, dxo_ref, x_ref, prm_ref, w_ref, qkg_ref, bd_ref,
         dx_ref, dproj_ref, st_ref, qs_ref) = refs[n_in:n_in + 12]
        sums = refs[n_in + 12:]

        @pl.when(pl.program_id(0) == 0)
        def _():
            st_ref[...] = jnp.zeros_like(st_ref)
            qs_ref[...] = jnp.zeros_like(qs_ref)

        for t in range(3):
            _to_groups(nat[t][...].astype(F32), sums[t])
            for i, dil in enumerate(dils):
                _merge_residues(res[3 * i + t], sums[t], dil, tm, add=True)
        bdm = bd_ref[...]
        grads = [_from_groups(sums[0]), _from_groups(sums[1])]
        for i in range(2):
            t = pa_ref[:, i * a_w:(i + 1) * a_w].astype(F32)
            rstd = lax.rsqrt(_head_mean(t * t, bdm) + EPS)
            that = t * rstd
            qs_ref[i:i + 1, :] += jnp.sum(grads[i] * that, axis=0, keepdims=True)
            dhat = grads[i] * qkg_ref[i:i + 1, :]
            dt = rstd * (dhat - that * _head_mean(dhat * that, bdm))
            dproj_ref[:, i * a_w:(i + 1) * a_w] = dt.astype(BF16)
        dproj_ref[:, 2 * a_w:3 * a_w] = _from_groups(sums[2]).astype(BF16)
        dproj_ref[:, 3 * a_w:] = dpc_ref[...]
        dh = _dot_nt(dproj_ref[:, 0:c8], w_ref[0])
        for j in range(1, N_DEV):
            dh = dh + _dot_nt(dproj_ref[:, j * c8:(j + 1) * c8], w_ref[j])
        dx, dshift, dscale, dng = _ada_bwd(dh, x_ref[...], prm_ref)
        dx_ref[...] = dxo_ref[...] + dx
        st_ref[0:1, :] += dshift
        st_ref[1:2, :] += dscale
        st_ref[2:3, :] += dng

    ta = pl.BlockSpec((tm, a_w), lambda m: (m, 0))
    tile = pl.BlockSpec((tm, d_model), lambda m: (m, 0))
    return pl.pallas_call(
        body, name=f"mixer_bwd_in_{l}", grid=(t_len // tm,),
        in_specs=[ta] * 3 + [_res_spec(dil, tm, a_w) for dil in dils for _ in range(3)]
                 + [pl.BlockSpec((tm, 3 * a_w), lambda m: (m, 0)),
                             pl.BlockSpec((tm, 3 * c_w), lambda m: (m, 0)),
                             tile, tile, pl.BlockSpec((8, d_model), lambda m: (0, 0)),
                             pl.BlockSpec((N_DEV, None, d_model, c8), lambda m: (0, l, 0, 0)),
                             pl.BlockSpec((8, a_w), lambda m: (0, 0)),
                             pl.BlockSpec((a_w, a_w), lambda m: (0, 0))],
        out_specs=[tile, pl.BlockSpec((tm, p_c), lambda m: (m, 0)),
                   pl.BlockSpec((8, d_model), lambda m: (0, 0)), pl.BlockSpec((8, a_w), lambda m: (0, 0))],
        out_shape=[SDS((t_len, d_model), F32), SDS((t_len, p_c), BF16), SDS((8, d_model), F32), SDS((8, a_w), F32)],
        scratch_shapes=[_groups(tm, a_w)] * 3,
        compiler_params=_params(1),
    )(*[t.reshape(t_len, a_w) for t in parts[0]], *[t for p in parts[1:] for t in p],
      proj, dpc, dxo, x_in, prm, wing, qkg, bd)


def _loss_head(x, target, tm):
    t_len, d_model = x.shape

    def body(x_ref, t_ref, dx_ref, loss_ref):
        @pl.when(pl.program_id(0) == 0)
        def _():
            loss_ref[...] = jnp.zeros_like(loss_ref)

        diff = x_ref[...] - t_ref[...]
        dx_ref[...] = diff * (1.0 / d_model)
        per_token = jnp.sum(diff * diff, axis=-1, keepdims=True) * (1.0 / d_model)
        loss_ref[...] += 0.5 * jnp.sum(per_token)

    tile = pl.BlockSpec((tm, d_model), lambda m: (m, 0))
    return pl.pallas_call(
        body, name="loss_head", grid=(t_len // tm,),
        in_specs=[tile, tile],
        out_specs=[tile, pl.BlockSpec((8, LANES), lambda m: (0, 0))],
        out_shape=[SDS((t_len, d_model), F32), SDS((8, LANES), F32)],
        compiler_params=_params(1),
    )(x, target)


def _row_tile(rows, cols):
    best = None
    for t in range(16, rows + 1, 16):
        if rows % t == 0 and t * cols * 4 <= (1 << 20):
            best = t
    return best if best is not None else rows


def _adamw(pieces, w, m, v, name):
    n_p, rows, cols = pieces.shape
    tr = _row_tile(rows, cols)
    c1 = 1.0 - ADAM_B1 ** ADAM_STEP
    c2 = 1.0 - ADAM_B2 ** ADAM_STEP

    def body(p_ref, w_ref, m_ref, v_ref, g_ref, d_ref, nm_ref, nv_ref):
        g = p_ref[0].astype(F32)
        for i in range(1, n_p):
            g = g + p_ref[i].astype(F32)
        g_ref[...] = g
        nm = ADAM_B1 * m_ref[...] + (1.0 - ADAM_B1) * g
        nv = ADAM_B2 * v_ref[...] + (1.0 - ADAM_B2) * (g * g)
        nm_ref[...] = nm
        nv_ref[...] = nv
        d_ref[...] = -ADAM_LR * ((nm / c1) / (jnp.sqrt(nv / c2) + ADAM_EPS) + ADAM_WD * w_ref[...])

    tile = pl.BlockSpec((tr, cols), lambda i: (i, 0))
    return pl.pallas_call(
        body, name=name, grid=(rows // tr,),
        in_specs=[pl.BlockSpec((n_p, tr, cols), lambda i: (0, i, 0)), tile, tile, tile],
        out_specs=[tile] * 4,
        out_shape=[SDS((rows, cols), F32)] * 4,
        compiler_params=_params(1),
    )(pieces, w, m, v)


def _pack(vecs):
    flat = jnp.concatenate([v.reshape(-1).astype(F32) for v in vecs])
    rows = -(-flat.shape[0] // (8 * LANES)) * 8
    return jnp.pad(flat, (0, rows * LANES - flat.shape[0])).reshape(rows, LANES)


def _unpack(packed, shapes):
    lead = packed.shape[:-2]
    flat = packed.reshape(lead + (-1,))
    out, off = [], 0
    for s in shapes:
        size = 1
        for d in s:
            size *= d
        out.append(flat[..., off:off + size].reshape(lead + tuple(s)))
        off += size
    return out


def kernel(x, c, w_ada, b_ada, norm_g, w_in, q_norm_g, k_norm_g, conv_w, conv_b, w_out, ffn_w1, ffn_w2, loss_target, m_w_ada, m_b_ada, m_norm_g, m_w_in, m_q_norm_g, m_k_norm_g, m_conv_w, m_conv_b, m_w_out, m_ffn_w1, m_ffn_w2, v_w_ada, v_b_ada, v_norm_g, v_w_in, v_q_norm_g, v_k_norm_g, v_conv_w, v_conv_b, v_w_out, v_ffn_w1, v_ffn_w2):
    _, t_len, d_model = x.shape
    n_layer = w_ada.shape[0]
    a_w = d_model // 2
    c_w = d_model - a_w
    n_head = a_w // HEAD_DIM
    ada_cols = w_ada.shape[-1]
    tm = min(TOKEN_TILE, t_len)
    me = _index(_my_place())
    x2 = x.reshape(t_len, d_model)
    target = loss_target.reshape(t_len, d_model)

    small_shapes = [(d_model,), norm_g.shape, conv_w.shape]
    gathered = _allgather_small(_pack([c, norm_g, conv_w]), "gather_small_inputs")
    c_all, ng_parts, cw_parts = _unpack(gathered, small_shapes)
    norm_g_full = jnp.moveaxis(ng_parts, 0, 2).reshape(n_layer, 3, d_model)
    conv_w_full = jnp.moveaxis(cw_parts, 0, 2).reshape(n_layer, 3, c_w)

    b_loc = lax.dynamic_slice_in_dim(b_ada, me * ada_cols, ada_cols, axis=1).reshape(n_layer, 1, ada_cols)
    mod_part, silu_c = _mod_part(c_all, w_ada, b_loc)
    mod_all = _allgather_small(_pack([mod_part]), "gather_mod")
    mod_all = _unpack(mod_all, [mod_part.shape])[0]
    mod_mine = lax.dynamic_index_in_dim(mod_all, me, axis=2, keepdims=False)
    mod = jnp.moveaxis(mod_mine, 0, 1).reshape(n_layer, 3, 3, d_model)

    def prm_of(l, sub):
        rows = jnp.stack([norm_g_full[l, sub], mod[l, sub, 0], mod[l, sub, 1], mod[l, sub, 2]])
        return jnp.pad(rows, ((0, 4), (0, 0)))

    prm = [[prm_of(l, sub) for sub in range(3)] for l in range(n_layer)]
    qkg = [jnp.pad(jnp.stack([jnp.tile(q_norm_g[l], n_head), jnp.tile(k_norm_g[l], n_head)]), ((0, 6), (0, 0)))
           for l in range(n_layer)]
    cprm = [jnp.pad(jnp.concatenate([conv_w_full[l], conv_b[l][None]]), ((0, 4), (0, 0))) for l in range(n_layer)]
    head_of = jnp.arange(a_w) // HEAD_DIM
    bd = (head_of[:, None] == head_of[None, :]).astype(BF16)

    wing, woutg, w1g, w2g = _allgather_big(
        [w_in.astype(BF16), w_out.astype(BF16), ffn_w1.astype(BF16), ffn_w2.astype(BF16)], "gather_weights")
    w1r = w1g.reshape((2, N_CHUNK) + w1g.shape[1:])
    f8 = ffn_w1.shape[-1]
    r8 = ffn_w2.shape[-2]
    c8 = w_in.shape[-1]
    o8 = w_out.shape[-2]

    saved = []
    xc = x2
    for l in range(n_layer):
        keep = {"x0": xc}
        keep["gu0"], keep["h0"], keep["y0"], xc = _ffn_fwd(xc, prm[l][0], w1r, w2g, l, 0, tm)
        keep["x1"] = xc
        proj, qkn, keep["h1"], ycv, *res = _mixer_in(xc, prm[l][1], wing, qkg[l], cprm[l], bd, l, tm)
        qkv = [(qkn[None], proj[None], 2)] + [(res[2 * i], res[2 * i + 1], 0) for i in range(len(DILATIONS) - 1)]
        branches = [_attn_fwd(*t, l) for t in qkv]
        cat, lse, *res = _attn_combine([b[0] for b in branches], [b[1] for b in branches], ycv, tm, l)
        n_r = len(DILATIONS) - 1
        keep["ya"] = [cat[None]] + res[:n_r]
        keep["lse"] = [lse[None]] + res[n_r:]
        xc, keep["ymix"] = _mixer_out(cat, xc, prm[l][1], woutg, l, tm)
        keep.update(proj=proj, qkv=qkv, cat=cat)
        keep["x2"] = xc
        keep["gu2"], keep["h2"], keep["y2"], xc = _ffn_fwd(xc, prm[l][2], w1r, w2g, l, 1, tm)
        saved.append(keep)

    dx, loss_blk = _loss_head(xc, target, tm)
    loss = lax.psum(loss_blk[0, 0], ("x", "y", "c"))

    tk = min(TOKEN_TILE, t_len)
    g_w1 = [[None, None] for _ in range(n_layer)]
    g_w2 = [[None, None] for _ in range(n_layer)]
    g_win = [None] * n_layer
    g_wout = [None] * n_layer
    stats = [[None] * 3 for _ in range(n_layer)]
    conv_stats = [None] * n_layer
    qk_stats = [None] * n_layer

    def ffn_backward(dxo, keep, l, sub, which):
        dx_in, da, s_act, dy, st = _ffn_bwd(dxo, keep[f"y{which}"], keep[f"x{which}"], prm[l][which],
                                            keep[f"gu{which}"], w1r, w2g, l, sub, tm)
        g_w1[l][sub] = _matmul_tn(
            keep[f"h{which}"], da.reshape(N_DEV, t_len, f8),
            pl.BlockSpec((tk, d_model), lambda p, k: (k, 0)),
            pl.BlockSpec((None, tk, f8), lambda p, k: (p, k, 0)),
            SDS((N_DEV, d_model, f8), BF16), pl.BlockSpec((None, d_model, f8), lambda p, k: (p, 0, 0)),
            N_DEV, t_len, tk, f"grad_w1_{l}_{sub}")
        g_w2[l][sub] = _matmul_tn(
            s_act, dy,
            pl.BlockSpec((None, tk, f8), lambda p, k: (p, k, 0)),
            pl.BlockSpec((tk, d_model), lambda p, k: (k, 0)),
            SDS((N_CHUNK, f8, d_model), BF16), pl.BlockSpec((None, f8, d_model), lambda p, k: (p, 0, 0)),
            N_CHUNK, t_len, tk, f"grad_w2_{l}_{sub}")
        return dx_in, st

    for l in reversed(range(n_layer)):
        keep = saved[l]
        dx, stats[l][2] = ffn_backward(dx, keep, l, 1, 2)
        dya, dpc, dy, st_gate, conv_stats[l], *dya_res = _mixer_bwd_out(
            dx, keep["ymix"], prm[l][1], woutg, keep["proj"], cprm[l], l, tm)
        dyas = [dya[None]] + dya_res
        g_wout[l] = _matmul_tn(
            keep["cat"], dy,
            pl.BlockSpec((tk, d_model), lambda p, k: (k, 0)),
            pl.BlockSpec((tk, d_model), lambda p, k: (k, 0)),
            SDS((d_model, d_model), BF16), pl.BlockSpec((d_model, d_model), lambda p, k: (0, 0)),
            1, t_len, tk, f"grad_wout_{l}")
        parts = [_attn_bwd(*keep["qkv"][i], dyas[i], keep["ya"][i], keep["lse"][i], l)
                 for i in range(len(DILATIONS))]
        dx, dproj, st_norm, qk_stats[l] = _mixer_bwd_in(
            parts, keep["proj"], dpc, dx, keep["x1"], prm[l][1], wing, qkg[l], bd, l, tm)
        stats[l][1] = st_norm + st_gate
        g_win[l] = _matmul_tn(
            keep["h1"], dproj,
            pl.BlockSpec((tk, d_model), lambda p, k: (k, 0)),
            pl.BlockSpec((tk, c8), lambda p, k: (k, p)),
            SDS((N_DEV, d_model, c8), BF16), pl.BlockSpec((None, d_model, c8), lambda p, k: (p, 0, 0)),
            N_DEV, t_len, tk, f"grad_win_{l}")
        dx, stats[l][0] = ffn_backward(dx, keep, l, 0, 0)
    grad_x = dx.reshape(x.shape)

    gp_win = jnp.stack(g_win, axis=1)
    gp_wout = jnp.stack([g.reshape(N_DEV, o8, d_model) for g in g_wout], axis=1)
    gp_w1 = jnp.stack([jnp.stack(g, axis=1) for g in g_w1], axis=1)
    gp_w2 = jnp.stack([jnp.stack([g.reshape(N_DEV, r8, d_model) for g in gl], axis=1) for gl in g_w2], axis=1)
    land = _exchange_pieces([gp_win, gp_wout, gp_w1, gp_w2], "exchange_weight_grads")

    def update(pieces, w, m, v, name):
        shape = w.shape
        cols = shape[-1]
        outs = _adamw(pieces.reshape(pieces.shape[0], -1, cols), w.reshape(-1, cols), m.reshape(-1, cols),
                      v.reshape(-1, cols), name)
        return [o.reshape(shape) for o in outs]

    res = {
        "w_in": update(land[0], w_in, m_w_in, v_w_in, "adamw_w_in"),
        "w_out": update(land[1], w_out, m_w_out, v_w_out, "adamw_w_out"),
        "ffn_w1": update(land[2], ffn_w1, m_ffn_w1, v_ffn_w1, "adamw_ffn_w1"),
        "ffn_w2": update(land[3], ffn_w2, m_ffn_w2, v_ffn_w2, "adamw_ffn_w2"),
    }

    dmod = jnp.stack([jnp.stack([stats[l][sub][r] for sub in range(3) for r in (0, 1, 3)]) for l in range(n_layer)])
    dng = jnp.stack([jnp.stack([stats[l][sub][2] for sub in range(3)]) for l in range(n_layer)])
    dcw = jnp.stack([conv_stats[l][0:3] for l in range(n_layer)])
    dcb = jnp.stack([conv_stats[l][3] for l in range(n_layer)])
    dqk = jnp.stack([qk_stats[l][0:2].reshape(2, n_head, HEAD_DIM).sum(axis=1) for l in range(n_layer)])
    part_shapes = [dmod.shape, dng.shape, dcw.shape, dcb.shape, dqk.shape]
    parts_all = _allgather_small(_pack([dmod, dng, dcw, dcb, dqk]), "gather_small_grads")
    dmod_all = _unpack(parts_all, part_shapes)[0].reshape(N_DEV, n_layer, 9 * d_model)
    summed = _unpack(_sum_devices(parts_all), part_shapes)
    g_b_ada = summed[0].reshape(n_layer, 9 * d_model)
    ng_cols = norm_g.shape[-1]
    g_norm_g = lax.dynamic_slice_in_dim(summed[1], me * ng_cols, ng_cols, axis=2)
    cw_cols = conv_w.shape[-1]
    g_conv_w = lax.dynamic_slice_in_dim(summed[2], me * cw_cols, cw_cols, axis=2)
    g_conv_b = summed[3]
    g_q, g_k = summed[4][:, 0], summed[4][:, 1]
    dmod_loc = lax.dynamic_slice_in_dim(dmod_all, me * ada_cols, ada_cols, axis=2)
    g_w_ada = _w_ada_grad(silu_c.T, jnp.moveaxis(dmod_loc, 0, 1))
    res["w_ada"] = update(g_w_ada[None], w_ada, m_w_ada, v_w_ada, "adamw_w_ada")

    names = ["b_ada", "norm_g", "q_norm_g", "k_norm_g", "conv_w", "conv_b"]
    g_small = [g_b_ada, g_norm_g, g_q, g_k, g_conv_w, g_conv_b]
    w_small = [b_ada, norm_g, q_norm_g, k_norm_g, conv_w, conv_b]
    m_small = [m_b_ada, m_norm_g, m_q_norm_g, m_k_norm_g, m_conv_w, m_conv_b]
    v_small = [v_b_ada, v_norm_g, v_q_norm_g, v_k_norm_g, v_conv_w, v_conv_b]
    outs = _adamw(_pack(g_small)[None], _pack(w_small), _pack(m_small), _pack(v_small), "adamw_small")
    shapes = [w.shape for w in w_small]
    unpacked = [_unpack(o, shapes) for o in outs]
    for i, nme in enumerate(names):
        res[nme] = [unpacked[k][i] for k in range(4)]

    order = ["w_ada", "b_ada", "norm_g", "w_in", "q_norm_g", "k_norm_g", "conv_w", "conv_b", "w_out", "ffn_w1", "ffn_w2"]
    return (loss, grad_x, *[res[n][0] for n in order], *[res[n][1] for n in order],
            *[res[n][2] for n in order], *[res[n][3] for n in order])
```

```python
import jax
import jax.numpy as jnp
from jax import lax
from jax.experimental import pallas as pl
from jax.experimental.pallas import tpu as pltpu

F32 = jnp.float32
BF16 = jnp.bfloat16
SDS = jax.ShapeDtypeStruct

N_DEV = 8
HEAD_DIM = 64
BLOCK = 128
DILATIONS = (1, 4, 16)
N_CHUNK = 4
EPS = 1e-6
NEG = -1e30
SM_SCALE = HEAD_DIM ** -0.5
LANES = 128
TOKEN_TILE = 512
VMEM_LIMIT_BYTES = 56 * 1024 * 1024

ADAM_LR = 0.001
ADAM_B1 = 0.9
ADAM_B2 = 0.999
ADAM_EPS = 1e-08
ADAM_WD = 0.01
ADAM_STEP = 10

MESH_ID = pl.DeviceIdType.MESH
ANY = pl.BlockSpec(memory_space=pl.ANY)
HBM_SPEC = pl.BlockSpec(memory_space=pltpu.HBM)
SEM_SPEC = pl.BlockSpec(memory_space=pltpu.SEMAPHORE)
SIDE_EFFECT = pltpu.SideEffectType.DATAFLOW_SIDE_EFFECTING


def _params(n_axes):
    return pltpu.CompilerParams(dimension_semantics=("arbitrary",) * n_axes,
                                vmem_limit_bytes=VMEM_LIMIT_BYTES)


def _dot(a, b):
    return jnp.dot(a, b, preferred_element_type=F32)


def _dot_nt(a, b):
    return lax.dot_general(a, b, (((1,), (1,)), ((), ())), preferred_element_type=F32)


def _dot_tn(a, b):
    return lax.dot_general(a, b, (((0,), (0,)), ((), ())), preferred_element_type=F32)


def _split_bf16(t):
    hi = t.astype(BF16)
    return hi, (t - hi.astype(F32)).astype(BF16)


def _my_place():
    x, y, c = lax.axis_index("x"), lax.axis_index("y"), lax.axis_index("c")
    return x, y, c


def _peer(place, k):
    x, y, c = place
    return ((1 - x) if k & 4 else x, (1 - y) if k & 2 else y, (1 - c) if k & 1 else c)


def _index(place):
    return 4 * place[0] + 2 * place[1] + place[2]


def _allgather_small(v, name):
    rows, cols = v.shape

    def body(x_ref, out_ref, send_sems, recv_sems, local_sem):
        me = _my_place()
        mine = pltpu.make_async_copy(x_ref, out_ref.at[_index(me)], local_sem)
        mine.start()
        sends = []
        for k in range(1, N_DEV):
            cp = pltpu.make_async_remote_copy(
                src_ref=x_ref, dst_ref=out_ref.at[_index(me)], send_sem=send_sems.at[k - 1],
                recv_sem=recv_sems.at[k - 1], device_id=_peer(me, k), device_id_type=MESH_ID)
            cp.start()
            sends.append(cp)
        for k in range(1, N_DEV):
            pltpu.make_async_remote_copy(
                src_ref=x_ref, dst_ref=out_ref.at[_index(_peer(me, k))], send_sem=send_sems.at[k - 1],
                recv_sem=recv_sems.at[k - 1], device_id=_peer(me, k), device_id_type=MESH_ID).wait_recv()
        for cp in sends:
            cp.wait_send()
        mine.wait()

    return pl.pallas_call(
        body, name=name,
        out_shape=SDS((N_DEV, rows, cols), v.dtype),
        in_specs=[pl.BlockSpec(memory_space=pltpu.VMEM)],
        out_specs=pl.BlockSpec(memory_space=pltpu.VMEM),
        scratch_shapes=[pltpu.SemaphoreType.DMA((N_DEV - 1,)), pltpu.SemaphoreType.DMA((N_DEV - 1,)),
                        pltpu.SemaphoreType.DMA],
    )(v)


def _allgather_big(arrs, name):
    n = len(arrs)

    def body(*refs):
        ins, outs = refs[:n], refs[n:2 * n]
        send_sems, recv_sems, local_sems = refs[2 * n:]
        x, y, c = _my_place()
        me, sib = (x, y, c), (x, y, 1 - c)
        chips = [(1 - x, y), (x, 1 - y), (1 - x, 1 - y)]

        def copy(i, k, block, to, src=None):
            dst = outs[i].at[_index(block)]
            return pltpu.make_async_remote_copy(
                src_ref=dst if src is None else src, dst_ref=dst, send_sem=send_sems.at[7 * i + k],
                recv_sem=recv_sems.at[7 * i + k], device_id=to, device_id_type=MESH_ID)

        mine = [pltpu.make_async_copy(ins[i], outs[i].at[_index(me)], local_sems.at[i]) for i in range(n)]
        for cp in mine:
            cp.start()
        first = []
        for i in range(n):
            first.append(copy(i, 0, me, sib, src=ins[i]))
            for j, chip in enumerate(chips):
                first.append(copy(i, 1 + j, me, (*chip, c), src=ins[i]))
        for cp in first:
            cp.start()
        passed = []
        for i in range(n):
            for j, chip in enumerate(chips):
                copy(i, 1 + j, (*chip, c), me).wait_recv()
                fwd = copy(i, 4 + j, (*chip, c), sib)
                fwd.start()
                passed.append(fwd)
        for i in range(n):
            copy(i, 0, sib, me).wait_recv()
            for j, chip in enumerate(chips):
                copy(i, 4 + j, (*chip, 1 - c), me).wait_recv()
        for cp in first + passed:
            cp.wait_send()
        for cp in mine:
            cp.wait()

    return pl.pallas_call(
        body, name=name,
        out_shape=[SDS((N_DEV,) + a.shape, a.dtype) for a in arrs],
        in_specs=[ANY] * n, out_specs=[ANY] * n,
        scratch_shapes=[pltpu.SemaphoreType.DMA((7 * n,)), pltpu.SemaphoreType.DMA((7 * n,)),
                        pltpu.SemaphoreType.DMA((n,))],
    )(*arrs)


def _own_slots(srcs, whole, name):
    n = len(srcs)

    def body(*refs):
        ins, outs, sems = refs[:n], refs[n:2 * n], refs[2 * n]
        me = _index(_my_place())
        copies = [pltpu.make_async_copy(ins[i] if whole else ins[i].at[me], outs[i].at[me], sems.at[i])
                  for i in range(n)]
        for cp in copies:
            cp.start()
        for cp in copies:
            cp.wait()

    return pl.pallas_call(
        body, name=name,
        out_shape=[SDS((N_DEV,) + a.shape if whole else a.shape, a.dtype) for a in srcs],
        in_specs=[ANY] * n, out_specs=[ANY] * n,
        scratch_shapes=[pltpu.SemaphoreType.DMA((n,))],
    )(*srcs)


def _send_start(srcs, lands, whole, after, name):
    n = len(srcs)

    def body(*refs):
        src_r, land_r = refs[:n], refs[n:2 * n]
        send, recv = refs[2 * n + 1:3 * n + 1], refs[3 * n + 1:4 * n + 1]
        token = refs[-1]
        me = _my_place()
        for i in range(n):
            for k in range(1, N_DEV):
                peer = _peer(me, k)
                pltpu.make_async_remote_copy(
                    src_ref=src_r[i] if whole else src_r[i].at[_index(peer)], dst_ref=land_r[i].at[_index(me)],
                    send_sem=send[i], recv_sem=recv[i], device_id=peer, device_id_type=MESH_ID).start()
        token[...] = jnp.zeros_like(token)

    arrs = list(srcs) + list(lands)
    outs = pl.pallas_call(
        body, name=name,
        out_shape=[pltpu.SemaphoreType.DMA(())] * (2 * n) + [pltpu.HBM(a.shape, a.dtype) for a in arrs]
        + [SDS((8, LANES), F32)],
        in_specs=[HBM_SPEC] * (2 * n) + [ANY],
        out_specs=[SEM_SPEC] * (2 * n) + [HBM_SPEC] * (2 * n) + [pl.BlockSpec(memory_space=pltpu.VMEM)],
        input_output_aliases={i: 2 * n + i for i in range(2 * n)},
        compiler_params=pltpu.CompilerParams(has_side_effects=SIDE_EFFECT),
    )(*[pltpu.with_memory_space_constraint(a, pltpu.HBM) for a in arrs], after)
    return outs[:n], outs[n:2 * n], outs[2 * n:3 * n], outs[3 * n:4 * n], outs[-1]


def _send_wait(srcs, lands, sends, recvs, after, name):
    n = len(srcs)

    def body(*refs):
        land_r = refs[n:2 * n]
        send, recv = refs[2 * n:3 * n], refs[3 * n:4 * n]
        me = _my_place()
        for i in range(n):
            seven = land_r[i].at[pl.ds(0, N_DEV - 1)]
            cp = pltpu.make_async_remote_copy(src_ref=seven, dst_ref=seven, send_sem=send[i], recv_sem=recv[i],
                                              device_id=_peer(me, 1), device_id_type=MESH_ID)
            cp.wait_send()
            cp.wait_recv()

    arrs = list(srcs) + list(lands)
    outs = pl.pallas_call(
        body, name=name,
        out_shape=[pltpu.HBM(a.shape, a.dtype) for a in arrs],
        in_specs=[HBM_SPEC] * (2 * n) + [SEM_SPEC] * (2 * n) + [ANY],
        out_specs=[HBM_SPEC] * (2 * n),
        input_output_aliases={i: i for i in range(2 * n)},
        compiler_params=pltpu.CompilerParams(has_side_effects=SIDE_EFFECT),
    )(*arrs, *sends, *recvs, after)
    return outs[n:]


def _ada_fwd(x, prm_ref):
    rstd = lax.rsqrt(jnp.mean(x * x, axis=-1, keepdims=True) + EPS)
    return (x * rstd * prm_ref[0:1, :]) * (1.0 + prm_ref[2:3, :]) + prm_ref[1:2, :]


def _ada_bwd(dh, x, prm_ref):
    ng, ops = prm_ref[0:1, :], 1.0 + prm_ref[2:3, :]
    rstd = lax.rsqrt(jnp.mean(x * x, axis=-1, keepdims=True) + EPS)
    xhat = x * rstd
    dxhat = dh * (ops * ng)
    dx = rstd * (dxhat - xhat * jnp.mean(dxhat * xhat, axis=-1, keepdims=True))
    dshift = jnp.sum(dh, axis=0, keepdims=True)
    dscale = jnp.sum(dh * (xhat * ng), axis=0, keepdims=True)
    dng = jnp.sum(dh * (xhat * ops), axis=0, keepdims=True)
    return dx, dshift, dscale, dng


def _head_mean(t, bd):
    hi, lo = _split_bf16(t)
    return (_dot(hi, bd) + _dot(lo, bd)) * (1.0 / HEAD_DIM)


def _shift_rows(z, before, tm):
    row = lax.broadcasted_iota(jnp.int32, z.shape, 0)
    z1 = jnp.where(row == 0, before[7:8, :], pltpu.roll(z, 1, 0))
    z2 = jnp.where(row == 0, before[6:7, :], jnp.where(row == 1, before[7:8, :], pltpu.roll(z, 2, 0)))
    return z1, z2


def _mod_part(c_all, w_ada, b_loc):
    n_layer, d_model, cols = w_ada.shape

    def body(c_ref, w_ref, b_ref, out_ref, sc_ref):
        cv = c_ref[...]
        sc = cv * jax.nn.sigmoid(cv)
        sc_ref[...] = sc
        a_hi, a_lo = _split_bf16(sc)
        w_hi, w_lo = _split_bf16(w_ref[...])
        out_ref[...] = _dot(a_hi, w_hi) + _dot(a_hi, w_lo) + _dot(a_lo, w_hi) + b_ref[...]

    return pl.pallas_call(
        body, name="mod_part", grid=(n_layer,),
        in_specs=[pl.BlockSpec((N_DEV, d_model), lambda l: (0, 0)),
                  pl.BlockSpec((None, d_model, cols), lambda l: (l, 0, 0)),
                  pl.BlockSpec((None, 1, cols), lambda l: (l, 0, 0))],
        out_specs=[pl.BlockSpec((None, N_DEV, cols), lambda l: (l, 0, 0)),
                   pl.BlockSpec((N_DEV, d_model), lambda l: (0, 0))],
        out_shape=[SDS((n_layer, N_DEV, cols), F32), SDS((N_DEV, d_model), F32)],
        compiler_params=_params(1),
    )(c_all, w_ada, b_loc)


def _w_ada_grad(sc_t, dmod_loc):
    d_model = sc_t.shape[0]
    n_layer, _, cols = dmod_loc.shape

    def body(s_ref, d_ref, out_ref):
        acc = s_ref[:, 0:1] * d_ref[0:1, :]
        for b in range(1, N_DEV):
            acc = acc + s_ref[:, b:b + 1] * d_ref[b:b + 1, :]
        out_ref[...] = acc

    return pl.pallas_call(
        body, name="w_ada_grad", grid=(n_layer,),
        in_specs=[pl.BlockSpec((d_model, N_DEV), lambda l: (0, 0)),
                  pl.BlockSpec((None, N_DEV, cols), lambda l: (l, 0, 0))],
        out_specs=pl.BlockSpec((None, d_model, cols), lambda l: (l, 0, 0)),
        out_shape=SDS((n_layer, d_model, cols), F32),
        compiler_params=_params(1),
    )(sc_t, dmod_loc)


def _sum_devices(g):
    _, rows, cols = g.shape

    def body(g_ref, out_ref):
        acc = g_ref[0]
        for i in range(1, N_DEV):
            acc = acc + g_ref[i]
        out_ref[...] = acc

    return pl.pallas_call(body, name="sum_devices", out_shape=SDS((rows, cols), F32))(g)


def _ffn_fwd(x, prm, w1r, w2g, l, sub, tm):
    t_len, d_model = x.shape
    f8, r8 = w1r.shape[-1], w2g.shape[-2]

    def body(x_ref, prm_ref, w1_ref, w2_ref, gu_ref, h_ref, y_ref, xo_ref, h_sc, acc):
        j = pl.program_id(1)

        @pl.when(j == 0)
        def _():
            h = _ada_fwd(x_ref[...], prm_ref).astype(BF16)
            h_sc[...] = h
            h_ref[...] = h
            acc[...] = jnp.zeros_like(acc)

        hb = h_sc[...]
        g = _dot(hb, w1_ref[0])
        u = _dot(hb, w1_ref[1])
        gu_ref[0] = g.astype(BF16)
        gu_ref[1] = u.astype(BF16)
        s = (g * jax.nn.sigmoid(g) * u).astype(BF16)
        acc[...] += _dot(s, w2_ref[...].reshape(2 * r8, d_model))

        @pl.when(j == N_CHUNK - 1)
        def _():
            yv = acc[...]
            y_ref[...] = yv.astype(BF16)
            xo_ref[...] = x_ref[...] + (0.5 * prm_ref[3:4, :]) * yv

    tile = pl.BlockSpec((tm, d_model), lambda m, j: (m, 0))
    return pl.pallas_call(
        body, name=f"ffn_fwd_{l}_{sub}", grid=(t_len // tm, N_CHUNK),
        in_specs=[tile, pl.BlockSpec((8, d_model), lambda m, j: (0, 0)),
                  pl.BlockSpec((2, None, d_model, f8), lambda m, j: (0, j, 0, 0)),
                  pl.BlockSpec((2, r8, d_model), lambda m, j: (j, 0, 0))],
        out_specs=[pl.BlockSpec((2, None, tm, f8), lambda m, j: (0, j, m, 0)), tile, tile, tile],
        out_shape=[SDS((2, N_CHUNK, t_len, f8), BF16), SDS((t_len, d_model), BF16),
                   SDS((t_len, d_model), BF16), SDS((t_len, d_model), F32)],
        scratch_shapes=[pltpu.VMEM((tm, d_model), BF16), pltpu.VMEM((tm, d_model), F32)],
        compiler_params=_params(2),
    )(x, prm, w1r, w2g)


def _ffn_bwd(dxo, y, x_in, prm, gu, w1r, w2g, l, sub, tm):
    t_len, d_model = x_in.shape
    f8, r8 = w1r.shape[-1], w2g.shape[-2]
    n_m = t_len // tm

    def body(dxo_ref, y_ref, x_ref, prm_ref, gu_ref, w1_ref, w2_ref,
             dx_ref, da_ref, s_ref, dy_ref, st_ref, dy_sc, acc):
        m, j = pl.program_id(0), pl.program_id(1)

        @pl.when((m == 0) & (j == 0))
        def _():
            st_ref[...] = jnp.zeros_like(st_ref)

        @pl.when(j == 0)
        def _():
            dxo_v = dxo_ref[...]
            dy = ((0.5 * prm_ref[3:4, :]) * dxo_v).astype(BF16)
            dy_sc[...] = dy
            dy_ref[...] = dy
            st_ref[3:4, :] += jnp.sum(0.5 * dxo_v * y_ref[...].astype(F32), axis=0, keepdims=True)
            acc[...] = jnp.zeros_like(acc)

        ds = _dot_nt(dy_sc[...], w2_ref[...].reshape(2 * r8, d_model))
        g = gu_ref[0].astype(F32)
        u = gu_ref[1].astype(F32)
        sig = jax.nn.sigmoid(g)
        sil = g * sig
        s_ref[...] = (sil * u).astype(BF16)
        dg = (ds * u * (sig * (1.0 + g * (1.0 - sig)))).astype(BF16)
        du = (ds * sil).astype(BF16)
        da_ref[0] = dg
        da_ref[1] = du
        acc[...] += _dot_nt(dg, w1_ref[0]) + _dot_nt(du, w1_ref[1])

        @pl.when(j == N_CHUNK - 1)
        def _():
            dx, dshift, dscale, dng = _ada_bwd(acc[...], x_ref[...], prm_ref)
            dx_ref[...] = dxo_ref[...] + dx
            st_ref[0:1, :] += dshift
            st_ref[1:2, :] += dscale
            st_ref[2:3, :] += dng

    tile = pl.BlockSpec((tm, d_model), lambda m, j: (m, 0))
    fixed = pl.BlockSpec((8, d_model), lambda m, j: (0, 0))
    return pl.pallas_call(
        body, name=f"ffn_bwd_{l}_{sub}", grid=(n_m, N_CHUNK),
        in_specs=[tile, tile, tile, fixed,
                  pl.BlockSpec((2, None, tm, f8), lambda m, j: (0, j, m, 0)),
                  pl.BlockSpec((2, None, d_model, f8), lambda m, j: (0, j, 0, 0)),
                  pl.BlockSpec((2, r8, d_model), lambda m, j: (j, 0, 0))],
        out_specs=[tile, pl.BlockSpec((2, None, tm, f8), lambda m, j: (0, j, m, 0)),
                   pl.BlockSpec((None, tm, f8), lambda m, j: (j, m, 0)), tile, fixed],
        out_shape=[SDS((t_len, d_model), F32), SDS((2, N_CHUNK, t_len, f8), BF16),
                   SDS((N_CHUNK, t_len, f8), BF16), SDS((t_len, d_model), BF16), SDS((8, d_model), F32)],
        scratch_shapes=[pltpu.VMEM((tm, d_model), BF16), pltpu.VMEM((tm, d_model), F32)],
        compiler_params=_params(2),
    )(dxo, y, x_in, prm, gu, w1r, w2g)


def _matmul_tn(a, b, a_spec, b_spec, out_shape, out_spec, n_piece, t_len, tk, name):
    def body(a_ref, b_ref, o_ref, acc):
        k = pl.program_id(1)

        @pl.when(k == 0)
        def _():
            acc[...] = jnp.zeros_like(acc)

        acc[...] += _dot_tn(a_ref[...], b_ref[...])

        @pl.when(k == pl.num_programs(1) - 1)
        def _():
            o_ref[...] = acc[...].astype(o_ref.dtype)

    acc_shape = tuple(d for d in out_spec.block_shape if d is not None)
    return pl.pallas_call(
        body, name=name, grid=(n_piece, t_len // tk),
        in_specs=[a_spec, b_spec], out_specs=out_spec, out_shape=out_shape,
        scratch_shapes=[pltpu.VMEM(acc_shape, F32)],
        compiler_params=_params(2),
    )(a, b)


def _groups(tm, width):
    return pltpu.VMEM((width // LANES, tm, LANES), F32)


def _to_groups(val, dst_ref, first=0):
    for g in range(val.shape[1] // LANES):
        dst_ref[first + g] = val[:, g * LANES:(g + 1) * LANES]


def _from_groups(src_ref):
    return jnp.concatenate([src_ref[g] for g in range(src_ref.shape[0])], axis=1)


def _split_residues(src_ref, out_ref, dil, tm):
    for g in range(src_ref.shape[0]):
        for r in range(dil):
            out_ref[r, :, g * LANES:(g + 1) * LANES] = (
                src_ref.at[g][pl.ds(r, tm // dil, stride=dil), :].astype(out_ref.dtype))


def _merge_residues(in_ref, dst_ref, dil, tm, add=False):
    for g in range(dst_ref.shape[0]):
        for r in range(dil):
            rows = pl.ds(r, tm // dil, stride=dil)
            val = in_ref[r, :, g * LANES:(g + 1) * LANES].astype(F32)
            dst = dst_ref.at[g]
            dst[rows, :] = (dst[rows, :] + val) if add else val


def _res_spec(dil, tm, width, index=lambda m: m):
    return pl.BlockSpec((dil, tm // dil, width), lambda m: (0, index(m), 0))


def _mixer_in(x, prm, wing, qkg, cprm, bd, l, tm):
    t_len, d_model = x.shape
    a_w = d_model // 2
    c_w = d_model - a_w
    c8 = wing.shape[-1]
    p_c = N_DEV * c8
    dils = DILATIONS[1:]

    def body(x_ref, prm_ref, w_ref, qkg_ref, cprm_ref, bd_ref, proj_ref, qkn_ref, h_ref, ycv_ref, *rest):
        res_refs, (pf, qsc, vsc, carry) = rest[:2 * len(dils)], rest[2 * len(dils):]

        @pl.when(pl.program_id(0) == 0)
        def _():
            carry[...] = jnp.zeros_like(carry)

        h = _ada_fwd(x_ref[...], prm_ref).astype(BF16)
        h_ref[...] = h
        for j in range(N_DEV):
            pf[:, j * c8:(j + 1) * c8] = _dot(h, w_ref[j])
        proj_ref[...] = pf[...].astype(BF16)
        bdm = bd_ref[...]
        for i in range(2):
            t = pf[:, i * a_w:(i + 1) * a_w]
            rstd = lax.rsqrt(_head_mean(t * t, bdm) + EPS)
            tn = t * rstd * qkg_ref[i:i + 1, :]
            qkn_ref[:, i * a_w:(i + 1) * a_w] = tn.astype(BF16)
            _to_groups(tn, qsc, i * (a_w // LANES))
        _to_groups(pf[:, 2 * a_w:3 * a_w], vsc)
        for i, dil in enumerate(dils):
            _split_residues(qsc, res_refs[2 * i], dil, tm)
            _split_residues(vsc, res_refs[2 * i + 1], dil, tm)
        gb = pf[:, 3 * a_w:3 * a_w + c_w]
        z = pf[:, 3 * a_w + c_w:3 * a_w + 2 * c_w] * pf[:, 3 * a_w + 2 * c_w:3 * a_w + 3 * c_w]
        z1, z2 = _shift_rows(z, carry[...], tm)
        cv = cprm_ref[0:1, :] * z2 + cprm_ref[1:2, :] * z1 + cprm_ref[2:3, :] * z + cprm_ref[3:4, :]
        ycv_ref[...] = (gb * cv).astype(BF16)
        carry[...] = z[tm - 8:tm, :]

    res_specs, res_shapes = [], []
    for dil in dils:
        res_specs += [_res_spec(dil, tm, 2 * a_w), _res_spec(dil, tm, a_w)]
        res_shapes += [SDS((dil, t_len // dil, 2 * a_w), BF16), SDS((dil, t_len // dil, a_w), BF16)]
    return pl.pallas_call(
        body, name=f"mixer_in_{l}", grid=(t_len // tm,),
        in_specs=[pl.BlockSpec((tm, d_model), lambda m: (m, 0)),
                  pl.BlockSpec((8, d_model), lambda m: (0, 0)),
                  pl.BlockSpec((N_DEV, d_model, c8), lambda m: (0, 0, 0)),
                  pl.BlockSpec((8, a_w), lambda m: (0, 0)),
                  pl.BlockSpec((8, c_w), lambda m: (0, 0)),
                  pl.BlockSpec((a_w, a_w), lambda m: (0, 0))],
        out_specs=[pl.BlockSpec((tm, p_c), lambda m: (m, 0)),
                   pl.BlockSpec((tm, 2 * a_w), lambda m: (m, 0)),
                   pl.BlockSpec((tm, d_model), lambda m: (m, 0)),
                   pl.BlockSpec((tm, c_w), lambda m: (m, 0))] + res_specs,
        out_shape=[SDS((t_len, p_c), BF16), SDS((t_len, 2 * a_w), BF16),
                   SDS((t_len, d_model), BF16), SDS((t_len, c_w), BF16)] + res_shapes,
        scratch_shapes=[pltpu.VMEM((tm, p_c), F32), _groups(tm, 2 * a_w), _groups(tm, a_w),
                        pltpu.VMEM((8, c_w), F32)],
        compiler_params=_params(1),
    )(x, prm, wing, qkg, cprm, bd)


def _band_mask(first_block):
    qi = lax.broadcasted_iota(jnp.int32, (BLOCK, 2 * BLOCK), 0)
    kj = lax.broadcasted_iota(jnp.int32, (BLOCK, 2 * BLOCK), 1)
    lowest = jnp.where(first_block, BLOCK, 0)
    return (kj >= qi) & (kj <= qi + BLOCK) & (kj >= lowest)


def _attn_fwd(qk, v, v_col, l):
    dil, seq, a2 = qk.shape
    a_w = a2 // 2
    n_b = seq // BLOCK

    def body(q_ref, kc_ref, kp_ref, vc_ref, vp_ref, o_ref, lse_ref):
        valid = _band_mask(pl.program_id(1) == 0)
        low = lax.broadcasted_iota(jnp.int32, (BLOCK, LANES), 1) < HEAD_DIM
        for p in range(a_w // LANES):
            sl = slice(p * LANES, (p + 1) * LANES)
            qp = q_ref[:, sl]
            k2 = jnp.concatenate([kp_ref[:, sl], kc_ref[:, sl]], axis=0)
            v2 = jnp.concatenate([vp_ref[:, sl], vc_ref[:, sl]], axis=0)
            o_h, lse_h = [], []
            for half in (low, ~low):
                s = _dot_nt(jnp.where(half, qp, jnp.zeros_like(qp)), k2) * SM_SCALE
                s = jnp.where(valid, s, NEG)
                mx = jnp.max(s, axis=-1, keepdims=True)
                e = jnp.exp(s - mx)
                den = jnp.sum(e, axis=-1, keepdims=True)
                o_h.append(_dot((e / den).astype(BF16), v2))
                lse_h.append(mx + jnp.log(den))
            o_ref[:, sl] = jnp.where(low, o_h[0], o_h[1]).astype(BF16)
            lse_ref[:, sl] = jnp.where(low, lse_h[0], lse_h[1])

    def blk(f):
        return pl.BlockSpec((None, BLOCK, a_w), f)

    return pl.pallas_call(
        body, name=f"attn_fwd_{l}_d{dil}", grid=(dil, n_b),
        in_specs=[blk(lambda r, b: (r, b, 0)),
                  blk(lambda r, b: (r, b, 1)),
                  blk(lambda r, b: (r, jnp.maximum(b - 1, 0), 1)),
                  blk(lambda r, b: (r, b, v_col)),
                  blk(lambda r, b: (r, jnp.maximum(b - 1, 0), v_col))],
        out_specs=[blk(lambda r, b: (r, b, 0)), blk(lambda r, b: (r, b, 0))],
        out_shape=[SDS((dil, seq, a_w), BF16), SDS((dil, seq, a_w), F32)],
        compiler_params=_params(2),
    )(qk, qk, qk, v, v)


def _attn_combine(outs, lses, ycv, tm, l):
    t_len, c_w = ycv.shape
    a_w = outs[0].shape[-1]
    dils = DILATIONS[1:]
    n_r = len(dils)

    def body(o1, l1, *rest):
        o_res, l_res, y_ref = rest[:n_r], rest[n_r:2 * n_r], rest[2 * n_r]
        cat_ref, lse_ref = rest[2 * n_r + 1], rest[2 * n_r + 2]
        ya_res, lse_res = rest[2 * n_r + 3:3 * n_r + 3], rest[3 * n_r + 3:4 * n_r + 3]
        scr = rest[4 * n_r + 3:]
        so, sl, sya, slse = scr[:n_r], scr[n_r:2 * n_r], scr[2 * n_r], scr[2 * n_r + 1]
        for i, dil in enumerate(dils):
            _merge_residues(o_res[i], so[i], dil, tm)
            _merge_residues(l_res[i], sl[i], dil, tm)
        ls = [l1[...]] + [_from_groups(s) for s in sl]
        ov = [o1[...].astype(F32)] + [_from_groups(s) for s in so]
        mx = jnp.maximum(jnp.maximum(ls[0], ls[1]), ls[2])
        es = [jnp.exp(t - mx) for t in ls]
        den = es[0] + es[1] + es[2]
        ya = (es[0] / den) * ov[0] + (es[1] / den) * ov[1] + (es[2] / den) * ov[2]
        lse = mx + jnp.log(den)
        _to_groups(ya, sya)
        _to_groups(lse, slse)
        cat_ref[:, :a_w] = ya.astype(BF16)
        cat_ref[:, a_w:] = y_ref[...]
        lse_ref[...] = lse
        for i, dil in enumerate(dils):
            _split_residues(sya, ya_res[i], dil, tm)
            _split_residues(slse, lse_res[i], dil, tm)

    ta = pl.BlockSpec((tm, a_w), lambda m: (m, 0))
    res = [_res_spec(dil, tm, a_w) for dil in dils]
    return pl.pallas_call(
        body, name=f"attn_combine_{l}", grid=(t_len // tm,),
        in_specs=[ta, ta] + res + res + [pl.BlockSpec((tm, c_w), lambda m: (m, 0))],
        out_specs=[pl.BlockSpec((tm, a_w + c_w), lambda m: (m, 0)), ta] + res + res,
        out_shape=[SDS((t_len, a_w + c_w), BF16), SDS((t_len, a_w), F32)]
        + [SDS((dil, t_len // dil, a_w), BF16) for dil in dils]
        + [SDS((dil, t_len // dil, a_w), F32) for dil in dils],
        scratch_shapes=[_groups(tm, a_w)] * (2 * n_r + 2),
        compiler_params=_params(1),
    )(outs[0].reshape(t_len, a_w), lses[0].reshape(t_len, a_w), *outs[1:], *lses[1:], ycv)


def _mixer_out(cat, x, prm, woutg, l, tm):
    t_len, d_model = x.shape
    r8 = woutg.shape[-2]

    def body(cat_ref, x_ref, prm_ref, w_ref, xo_ref, y_ref):
        yv = _dot(cat_ref[...], w_ref[...].reshape(N_DEV * r8, d_model))
        y_ref[...] = yv.astype(BF16)
        xo_ref[...] = x_ref[...] + prm_ref[3:4, :] * yv

    tile = pl.BlockSpec((tm, d_model), lambda m: (m, 0))
    return pl.pallas_call(
        body, name=f"mixer_out_{l}", grid=(t_len // tm,),
        in_specs=[tile, tile, pl.BlockSpec((8, d_model), lambda m: (0, 0)),
                  pl.BlockSpec((N_DEV, r8, d_model), lambda m: (0, 0, 0))],
        out_specs=[tile, tile],
        out_shape=[SDS((t_len, d_model), F32), SDS((t_len, d_model), BF16)],
        compiler_params=_params(1),
    )(cat, x, prm, woutg)


def _mixer_bwd_out(dxo, ymix, prm, woutg, proj, cprm, l, tm):
    t_len, d_model = dxo.shape
    a_w = d_model // 2
    c_w = d_model - a_w
    r8 = woutg.shape[-2]
    n_m = t_len // tm
    hb = tm // 8

    dils = DILATIONS[1:]

    def body(dxo_ref, ym_ref, prm_ref, w_ref, pc_ref, halo_ref, cprm_ref,
             dya_ref, dpc_ref, dy_ref, st_ref, cs_ref, *rest):
        dya_res, (carry, dsc) = rest[:len(dils)], rest[len(dils):]
        step = pl.program_id(0)
        tile_i = n_m - 1 - step

        @pl.when(step == 0)
        def _():
            st_ref[...] = jnp.zeros_like(st_ref)
            cs_ref[...] = jnp.zeros_like(cs_ref)
            carry[...] = jnp.zeros_like(carry)

        dxo_v = dxo_ref[...]
        dy = (prm_ref[3:4, :] * dxo_v).astype(BF16)
        dy_ref[...] = dy
        st_ref[3:4, :] += jnp.sum(dxo_v * ym_ref[...].astype(F32), axis=0, keepdims=True)
        dcat = _dot_nt(dy, w_ref[...].reshape(N_DEV * r8, d_model))
        _to_groups(dcat[:, :a_w], dsc)
        dya_ref[...] = dcat[:, :a_w].astype(BF16)
        for i, dil in enumerate(dils):
            _split_residues(dsc, dya_res[i], dil, tm)
        dyc = dcat[:, a_w:]
        gb = pc_ref[:, :c_w].astype(F32)
        gc = pc_ref[:, c_w:2 * c_w].astype(F32)
        u = pc_ref[:, 2 * c_w:].astype(F32)
        z = gc * u
        before = halo_ref[:, c_w:2 * c_w].astype(F32) * halo_ref[:, 2 * c_w:].astype(F32)
        before = jnp.where(tile_i > 0, before, jnp.zeros_like(before))
        z1, z2 = _shift_rows(z, before, tm)
        w0, w1, w2 = cprm_ref[0:1, :], cprm_ref[1:2, :], cprm_ref[2:3, :]
        cv = w0 * z2 + w1 * z1 + w2 * z + cprm_ref[3:4, :]
        dcv = dyc * gb
        cs_ref[0:1, :] += jnp.sum(dcv * z2, axis=0, keepdims=True)
        cs_ref[1:2, :] += jnp.sum(dcv * z1, axis=0, keepdims=True)
        cs_ref[2:3, :] += jnp.sum(dcv * z, axis=0, keepdims=True)
        cs_ref[3:4, :] += jnp.sum(dcv, axis=0, keepdims=True)
        row = lax.broadcasted_iota(jnp.int32, dcv.shape, 0)
        after = carry[...]
        d1 = jnp.where(row == tm - 1, after[0:1, :], pltpu.roll(dcv, tm - 1, 0))
        d2 = jnp.where(row == tm - 2, after[0:1, :],
                       jnp.where(row == tm - 1, after[1:2, :], pltpu.roll(dcv, tm - 2, 0)))
        dz = w2 * dcv + w1 * d1 + w0 * d2
        dpc_ref[:, :c_w] = (dyc * cv).astype(BF16)
        dpc_ref[:, c_w:2 * c_w] = (dz * u).astype(BF16)
        dpc_ref[:, 2 * c_w:] = (dz * gc).astype(BF16)
        carry[...] = dcv[0:8, :]

    def rev(width, col=0):
        return pl.BlockSpec((tm, width), lambda s: (n_m - 1 - s, col))

    fixed_d = pl.BlockSpec((8, d_model), lambda s: (0, 0))
    fixed_c = pl.BlockSpec((8, c_w), lambda s: (0, 0))
    return pl.pallas_call(
        body, name=f"mixer_bwd_out_{l}", grid=(n_m,),
        in_specs=[rev(d_model), rev(d_model), fixed_d,
                  pl.BlockSpec((N_DEV, r8, d_model), lambda s: (0, 0, 0)),
                  rev(3 * c_w, 1),
                  pl.BlockSpec((8, 3 * c_w), lambda s: (jnp.maximum((n_m - 1 - s) * hb - 1, 0), 1)),
                  fixed_c],
        out_specs=[rev(a_w), rev(3 * c_w), rev(d_model), fixed_d, fixed_c]
        + [_res_spec(dil, tm, a_w, lambda s: n_m - 1 - s) for dil in dils],
        out_shape=[SDS((t_len, a_w), BF16), SDS((t_len, 3 * c_w), BF16), SDS((t_len, d_model), BF16),
                   SDS((8, d_model), F32), SDS((8, c_w), F32)]
        + [SDS((dil, t_len // dil, a_w), BF16) for dil in dils],
        scratch_shapes=[pltpu.VMEM((8, c_w), F32), _groups(tm, a_w)],
        compiler_params=_params(1),
    )(dxo, ymix, prm, woutg, proj, proj, cprm)


def _attn_bwd(qk, v, v_col, dya, ya, lse, l):
    dil, seq, a2 = qk.shape
    a_w = a2 // 2
    n_b = seq // BLOCK

    def body(q_ref, kc_ref, kp_ref, vc_ref, vp_ref, do_ref, ya_ref, lse_ref,
             dq_ref, dk_ref, dv_ref, ck, cvv):
        j = pl.program_id(1)

        @pl.when(j < n_b)
        def _():
            valid = _band_mask(j == 0)
            low = lax.broadcasted_iota(jnp.int32, (BLOCK, LANES), 1) < HEAD_DIM
            for p in range(a_w // LANES):
                sl = slice(p * LANES, (p + 1) * LANES)
                qp = q_ref[:, sl]
                k2 = jnp.concatenate([kp_ref[:, sl], kc_ref[:, sl]], axis=0)
                v2 = jnp.concatenate([vp_ref[:, sl], vc_ref[:, sl]], axis=0)
                dop = do_ref[:, sl]
                prod = dop.astype(F32) * ya_ref[:, sl].astype(F32)
                lsep = lse_ref[:, sl]
                dq_h = []
                dk2 = jnp.zeros((2 * BLOCK, LANES), F32)
                dv2 = jnp.zeros((2 * BLOCK, LANES), F32)
                for hh, half in enumerate((low, ~low)):
                    qh = jnp.where(half, qp, jnp.zeros_like(qp))
                    doh = jnp.where(half, dop, jnp.zeros_like(dop))
                    dsum = jnp.sum(jnp.where(half, prod, 0.0), axis=-1, keepdims=True)
                    lse_h = lsep[:, hh * HEAD_DIM:hh * HEAD_DIM + 1]
                    s = _dot_nt(qh, k2) * SM_SCALE
                    pr = jnp.exp(jnp.where(valid, s, NEG) - lse_h)
                    dp = _dot_nt(doh, v2)
                    ds = (pr * (dp - dsum) * SM_SCALE).astype(BF16)
                    dq_h.append(_dot(ds, k2))
                    dk2 = dk2 + _dot_tn(ds, qh)
                    dv2 = dv2 + _dot_tn(pr.astype(BF16), doh)
                dq_ref[:, sl] = jnp.where(low, dq_h[0], dq_h[1]).astype(BF16)

                @pl.when(j > 0)
                def _():
                    dk_ref[:, sl] = (ck[:, sl] + dk2[:BLOCK]).astype(BF16)
                    dv_ref[:, sl] = (cvv[:, sl] + dv2[:BLOCK]).astype(BF16)

                ck[:, sl] = dk2[BLOCK:]
                cvv[:, sl] = dv2[BLOCK:]

        @pl.when(j == n_b)
        def _():
            dk_ref[...] = ck[...].astype(BF16)
            dv_ref[...] = cvv[...].astype(BF16)

    def blk(f):
        return pl.BlockSpec((None, BLOCK, a_w), f)

    def cur(j):
        return jnp.minimum(j, n_b - 1)

    def prev(j):
        return jnp.maximum(jnp.minimum(j, n_b - 1) - 1, 0)

    def late(j):
        return jnp.maximum(j - 1, 0)

    return pl.pallas_call(
        body, name=f"attn_bwd_{l}_d{dil}", grid=(dil, n_b + 1),
        in_specs=[blk(lambda r, j: (r, cur(j), 0)),
                  blk(lambda r, j: (r, cur(j), 1)),
                  blk(lambda r, j: (r, prev(j), 1)),
                  blk(lambda r, j: (r, cur(j), v_col)),
                  blk(lambda r, j: (r, prev(j), v_col)),
                  blk(lambda r, j: (r, cur(j), 0)),
                  blk(lambda r, j: (r, cur(j), 0)),
                  blk(lambda r, j: (r, cur(j), 0))],
        out_specs=[blk(lambda r, j: (r, cur(j), 0)), blk(lambda r, j: (r, late(j), 0)),
                   blk(lambda r, j: (r, late(j), 0))],
        out_shape=[SDS((dil, seq, a_w), BF16)] * 3,
        scratch_shapes=[pltpu.VMEM((BLOCK, a_w), F32), pltpu.VMEM((BLOCK, a_w), F32)],
        compiler_params=_params(2),
    )(qk, qk, qk, v, v, dya, ya, lse)


def _mixer_bwd_in(parts, proj, dpc, dxo, x_in, prm, wing, qkg, bd, l, tm):
    t_len, d_model = x_in.shape
    a_w = d_model // 2
    c_w = d_model - a_w
    c8 = wing.shape[-1]
    p_c = N_DEV * c8

    dils = DILATIONS[1:]
    n_in = 3 * len(DILATIONS)

    def body(*refs):
        nat, res = refs[:3], refs[3:n_in]
        (pa_ref, dpc_ref, dxo_ref, x_ref, prm_ref, w_ref, qkg_ref, bd_ref,
         dx_ref, dproj_ref, st_ref, qs_ref) = refs[n_in:n_in + 12]
        sums = refs[n_in + 12:]

        @pl.when(pl.program_id(0) == 0)
        def _():
            st_ref[...] = jnp.zeros_like(st_ref)
            qs_ref[...] = jnp.zeros_like(qs_ref)

        for t in range(3):
            _to_groups(nat[t][...].astype(F32), sums[t])
            for i, dil in enumerate(dils):
                _merge_residues(res[3 * i + t], sums[t], dil, tm, add=True)
        bdm = bd_ref[...]
        grads = [_from_groups(sums[0]), _from_groups(sums[1])]
        for i in range(2):
            t = pa_ref[:, i * a_w:(i + 1) * a_w].astype(F32)
            rstd = lax.rsqrt(_head_mean(t * t, bdm) + EPS)
            that = t * rstd
            qs_ref[i:i + 1, :] += jnp.sum(grads[i] * that, axis=0, keepdims=True)
            dhat = grads[i] * qkg_ref[i:i + 1, :]
            dt = rstd * (dhat - that * _head_mean(dhat * that, bdm))
            dproj_ref[:, i * a_w:(i + 1) * a_w] = dt.astype(BF16)
        dproj_ref[:, 2 * a_w:3 * a_w] = _from_groups(sums[2]).astype(BF16)
        dproj_ref[:, 3 * a_w:] = dpc_ref[...]
        dh = _dot_nt(dproj_ref[:, 0:c8], w_ref[0])
        for j in range(1, N_DEV):
            dh = dh + _dot_nt(dproj_ref[:, j * c8:(j + 1) * c8], w_ref[j])
        dx, dshift, dscale, dng = _ada_bwd(dh, x_ref[...], prm_ref)
        dx_ref[...] = dxo_ref[...] + dx
        st_ref[0:1, :] += dshift
        st_ref[1:2, :] += dscale
        st_ref[2:3, :] += dng

    ta = pl.BlockSpec((tm, a_w), lambda m: (m, 0))
    tile = pl.BlockSpec((tm, d_model), lambda m: (m, 0))
    return pl.pallas_call(
        body, name=f"mixer_bwd_in_{l}", grid=(t_len // tm,),
        in_specs=[ta] * 3 + [_res_spec(dil, tm, a_w) for dil in dils for _ in range(3)]
                 + [pl.BlockSpec((tm, 3 * a_w), lambda m: (m, 0)),
                             pl.BlockSpec((tm, 3 * c_w), lambda m: (m, 0)),
                             tile, tile, pl.BlockSpec((8, d_model), lambda m: (0, 0)),
                             pl.BlockSpec((N_DEV, d_model, c8), lambda m: (0, 0, 0)),
                             pl.BlockSpec((8, a_w), lambda m: (0, 0)),
                             pl.BlockSpec((a_w, a_w), lambda m: (0, 0))],
        out_specs=[tile, pl.BlockSpec((tm, p_c), lambda m: (m, 0)),
                   pl.BlockSpec((8, d_model), lambda m: (0, 0)), pl.BlockSpec((8, a_w), lambda m: (0, 0))],
        out_shape=[SDS((t_len, d_model), F32), SDS((t_len, p_c), BF16), SDS((8, d_model), F32), SDS((8, a_w), F32)],
        scratch_shapes=[_groups(tm, a_w)] * 3,
        compiler_params=_params(1),
    )(*[t.reshape(t_len, a_w) for t in parts[0]], *[t for p in parts[1:] for t in p],
      proj, dpc, dxo, x_in, prm, wing, qkg, bd)


def _loss_head(x, target, tm):
    t_len, d_model = x.shape

    def body(x_ref, t_ref, dx_ref, loss_ref):
        @pl.when(pl.program_id(0) == 0)
        def _():
            loss_ref[...] = jnp.zeros_like(loss_ref)

        diff = x_ref[...] - t_ref[...]
        dx_ref[...] = diff * (1.0 / d_model)
        per_token = jnp.sum(diff * diff, axis=-1, keepdims=True) * (1.0 / d_model)
        loss_ref[...] += 0.5 * jnp.sum(per_token)

    tile = pl.BlockSpec((tm, d_model), lambda m: (m, 0))
    return pl.pallas_call(
        body, name="loss_head", grid=(t_len // tm,),
        in_specs=[tile, tile],
        out_specs=[tile, pl.BlockSpec((8, LANES), lambda m: (0, 0))],
        out_shape=[SDS((t_len, d_model), F32), SDS((8, LANES), F32)],
        compiler_params=_params(1),
    )(x, target)


def _row_tile(rows, cols):
    best = None
    for t in range(16, rows + 1, 16):
        if rows % t == 0 and t * cols * 4 <= (1 << 20):
            best = t
    return best if best is not None else rows


def _adamw(pieces, w, m, v, name):
    n_q = len(pieces)
    n_p, rows, cols = pieces[0].shape
    tr = _row_tile(rows, cols)
    n_i = rows // tr
    c1 = 1.0 - ADAM_B1 ** ADAM_STEP
    c2 = 1.0 - ADAM_B2 ** ADAM_STEP

    def body(*refs):
        p_refs = refs[:n_q]
        w_ref, m_ref, v_ref, g_ref, d_ref, nm_ref, nv_ref = refs[n_q:]
        for q in range(n_q):
            @pl.when(pl.program_id(0) == q)
            def _():
                g = p_refs[q][0].astype(F32)
                for i in range(1, n_p):
                    g = g + p_refs[q][i].astype(F32)
                g_ref[...] = g
                nm = ADAM_B1 * m_ref[...] + (1.0 - ADAM_B1) * g
                nv = ADAM_B2 * v_ref[...] + (1.0 - ADAM_B2) * (g * g)
                nm_ref[...] = nm
                nv_ref[...] = nv
                d_ref[...] = -ADAM_LR * ((nm / c1) / (jnp.sqrt(nv / c2) + ADAM_EPS) + ADAM_WD * w_ref[...])

    def piece_spec(q):
        return pl.BlockSpec((n_p, tr, cols), lambda a, i: (0, jnp.where(a == q, i, 0), 0))

    tile = pl.BlockSpec((tr, cols), lambda a, i: (a * n_i + i, 0))
    return pl.pallas_call(
        body, name=name, grid=(n_q, n_i),
        in_specs=[piece_spec(q) for q in range(n_q)] + [tile, tile, tile],
        out_specs=[tile] * 4,
        out_shape=[SDS((n_q * rows, cols), F32)] * 4,
        compiler_params=_params(2),
    )(*pieces, w, m, v)


def _pack(vecs):
    flat = jnp.concatenate([v.reshape(-1).astype(F32) for v in vecs])
    rows = -(-flat.shape[0] // (8 * LANES)) * 8
    return jnp.pad(flat, (0, rows * LANES - flat.shape[0])).reshape(rows, LANES)


def _unpack(packed, shapes):
    lead = packed.shape[:-2]
    flat = packed.reshape(lead + (-1,))
    out, off = [], 0
    for s in shapes:
        size = 1
        for d in s:
            size *= d
        out.append(flat[..., off:off + size].reshape(lead + tuple(s)))
        off += size
    return out


def kernel(x, c, w_ada, b_ada, norm_g, w_in, q_norm_g, k_norm_g, conv_w, conv_b, w_out, ffn_w1, ffn_w2, loss_target, m_w_ada, m_b_ada, m_norm_g, m_w_in, m_q_norm_g, m_k_norm_g, m_conv_w, m_conv_b, m_w_out, m_ffn_w1, m_ffn_w2, v_w_ada, v_b_ada, v_norm_g, v_w_in, v_q_norm_g, v_k_norm_g, v_conv_w, v_conv_b, v_w_out, v_ffn_w1, v_ffn_w2):
    _, t_len, d_model = x.shape
    n_layer = w_ada.shape[0]
    a_w = d_model // 2
    c_w = d_model - a_w
    n_head = a_w // HEAD_DIM
    ada_cols = w_ada.shape[-1]
    tm = min(TOKEN_TILE, t_len)
    me = _index(_my_place())
    x2 = x.reshape(t_len, d_model)
    target = loss_target.reshape(t_len, d_model)

    small_shapes = [(d_model,), norm_g.shape, conv_w.shape]
    gathered = _allgather_small(_pack([c, norm_g, conv_w]), "gather_small_inputs")
    c_all, ng_parts, cw_parts = _unpack(gathered, small_shapes)
    norm_g_full = jnp.moveaxis(ng_parts, 0, 2).reshape(n_layer, 3, d_model)
    conv_w_full = jnp.moveaxis(cw_parts, 0, 2).reshape(n_layer, 3, c_w)

    b_loc = lax.dynamic_slice_in_dim(b_ada, me * ada_cols, ada_cols, axis=1).reshape(n_layer, 1, ada_cols)
    mod_part, silu_c = _mod_part(c_all, w_ada, b_loc)
    mod_all = _allgather_small(_pack([mod_part]), "gather_mod")
    mod_all = _unpack(mod_all, [mod_part.shape])[0]
    mod_mine = lax.dynamic_index_in_dim(mod_all, me, axis=2, keepdims=False)
    mod = jnp.moveaxis(mod_mine, 0, 1).reshape(n_layer, 3, 3, d_model)

    def prm_of(l, sub):
        rows = jnp.stack([norm_g_full[l, sub], mod[l, sub, 0], mod[l, sub, 1], mod[l, sub, 2]])
        return jnp.pad(rows, ((0, 4), (0, 0)))

    prm = [[prm_of(l, sub) for sub in range(3)] for l in range(n_layer)]
    qkg = [jnp.pad(jnp.stack([jnp.tile(q_norm_g[l], n_head), jnp.tile(k_norm_g[l], n_head)]), ((0, 6), (0, 0)))
           for l in range(n_layer)]
    cprm = [jnp.pad(jnp.concatenate([conv_w_full[l], conv_b[l][None]]), ((0, 4), (0, 0))) for l in range(n_layer)]
    head_of = jnp.arange(a_w) // HEAD_DIM
    bd = (head_of[:, None] == head_of[None, :]).astype(BF16)

    f8 = ffn_w1.shape[-1]
    r8 = ffn_w2.shape[-2]
    c8 = w_in.shape[-1]
    o8 = w_out.shape[-2]
    win_b = [w_in[l].astype(BF16) for l in range(n_layer)]
    wout_b = [w_out[l].astype(BF16) for l in range(n_layer)]
    w1_b = [[ffn_w1[l, s].astype(BF16) for s in range(2)] for l in range(n_layer)]
    w2_b = [[ffn_w2[l, s].astype(BF16) for s in range(2)] for l in range(n_layer)]
    w1_first, w2_first = _allgather_big([w1_b[0][0], w2_b[0][0]], "gather_first_weights")
    groups = [[win_b[0], wout_b[0]], [w1_b[0][1], w2_b[0][1]]]
    for l in range(1, n_layer):
        groups += [[w1_b[l][0], w2_b[l][0]], [win_b[l], wout_b[l]], [w1_b[l][1], w2_b[l][1]]]
    flat = [a for grp in groups for a in grp]
    w_sends, w_recvs, w_srcs, w_lands, w_token = _send_start(
        flat, _own_slots(flat, True, "weights_own_slots"), True, w1_first, "weights_start")

    def gathered(gi, after):
        lo = sum(len(grp) for grp in groups[:gi])
        hi = lo + len(groups[gi])
        return _send_wait(w_srcs[lo:hi], w_lands[lo:hi], w_sends[lo:hi], w_recvs[lo:hi], after, f"weights_wait_{gi}")

    def chunked(w1g):
        return w1g.reshape(2, N_CHUNK, d_model, f8)

    saved = []
    xc = x2
    for l in range(n_layer):
        keep = {"x0": xc}
        if l == 0:
            w1g, w2g = w1_first, w2_first
            prm_first = prm[0][0] + w_token[0, 0]
        else:
            w1g, w2g = gathered(3 * l - 1, xc)
            prm_first = prm[l][0]
        keep["w_ffn0"] = (chunked(w1g), w2g)
        keep["gu0"], keep["h0"], keep["y0"], xc = _ffn_fwd(xc, prm_first, *keep["w_ffn0"], l, 0, tm)
        keep["x1"] = xc
        wing, woutg = gathered(0 if l == 0 else 3 * l, xc)
        keep.update(wing=wing, woutg=woutg)
        proj, qkn, keep["h1"], ycv, *res = _mixer_in(xc, prm[l][1], wing, qkg[l], cprm[l], bd, l, tm)
        qkv = [(qkn[None], proj[None], 2)] + [(res[2 * i], res[2 * i + 1], 0) for i in range(len(DILATIONS) - 1)]
        branches = [_attn_fwd(*t, l) for t in qkv]
        cat, lse, *res = _attn_combine([b[0] for b in branches], [b[1] for b in branches], ycv, tm, l)
        n_r = len(DILATIONS) - 1
        keep["ya"] = [cat[None]] + res[:n_r]
        keep["lse"] = [lse[None]] + res[n_r:]
        xc, keep["ymix"] = _mixer_out(cat, xc, prm[l][1], woutg, l, tm)
        keep.update(proj=proj, qkv=qkv, cat=cat)
        keep["x2"] = xc
        w1g, w2g = gathered(1 if l == 0 else 3 * l + 1, xc)
        keep["w_ffn2"] = (chunked(w1g), w2g)
        keep["gu2"], keep["h2"], keep["y2"], xc = _ffn_fwd(xc, prm[l][2], *keep["w_ffn2"], l, 1, tm)
        saved.append(keep)

    dx, loss_blk = _loss_head(xc, target, tm)
    loss = lax.psum(loss_blk[0, 0], ("x", "y", "c"))

    tk = min(TOKEN_TILE, t_len)
    g_w1 = [[None, None] for _ in range(n_layer)]
    g_w2 = [[None, None] for _ in range(n_layer)]
    g_win = [None] * n_layer
    g_wout = [None] * n_layer
    stats = [[None] * 3 for _ in range(n_layer)]
    conv_stats = [None] * n_layer
    qk_stats = [None] * n_layer

    started = []
    dummy = jnp.zeros((8, LANES), F32)

    def start_exchange(gs, name):
        sends, recvs, srcs, lands, token = _send_start(
            gs, _own_slots(gs, False, f"grads_own_slots_{name}"), False, dummy, f"grads_start_{name}")
        started.append((srcs, lands, sends, recvs))
        return token[0, 0]

    def wait_exchange(entries, after, name):
        srcs, lands, sends, recvs = [sum((list(e[i]) for e in entries), []) for i in range(4)]
        done = _send_wait(srcs, lands, sends, recvs, after, name)
        out, off = [], 0
        for e in entries:
            out.append(done[off:off + len(e[0])])
            off += len(e[0])
        return out

    def ffn_backward(dxo, keep, l, sub, which, prm_dep):
        dx_in, da, s_act, dy, st = _ffn_bwd(dxo, keep[f"y{which}"], keep[f"x{which}"], prm_dep,
                                            keep[f"gu{which}"], *keep[f"w_ffn{which}"], l, sub, tm)
        g_w1[l][sub] = _matmul_tn(
            keep[f"h{which}"], da.reshape(N_DEV, t_len, f8),
            pl.BlockSpec((tk, d_model), lambda p, k: (k, 0)),
            pl.BlockSpec((None, tk, f8), lambda p, k: (p, k, 0)),
            SDS((N_DEV, d_model, f8), BF16), pl.BlockSpec((None, d_model, f8), lambda p, k: (p, 0, 0)),
            N_DEV, t_len, tk, f"grad_w1_{l}_{sub}")
        g_w2[l][sub] = _matmul_tn(
            s_act, dy,
            pl.BlockSpec((None, tk, f8), lambda p, k: (p, k, 0)),
            pl.BlockSpec((tk, d_model), lambda p, k: (k, 0)),
            SDS((N_CHUNK, f8, d_model), BF16), pl.BlockSpec((None, f8, d_model), lambda p, k: (p, 0, 0)),
            N_CHUNK, t_len, tk, f"grad_w2_{l}_{sub}")
        token = start_exchange([g_w1[l][sub], g_w2[l][sub].reshape(N_DEV, r8, d_model)], f"ffn_{l}_{sub}")
        return dx_in, st, token

    token = 0.0
    for l in reversed(range(n_layer)):
        keep = saved[l]
        wing, woutg = keep["wing"], keep["woutg"]
        dx, stats[l][2], token = ffn_backward(dx, keep, l, 1, 2, prm[l][2] + token)
        dya, dpc, dy, st_gate, conv_stats[l], *dya_res = _mixer_bwd_out(
            dx, keep["ymix"], prm[l][1] + token, woutg, keep["proj"], cprm[l], l, tm)
        dyas = [dya[None]] + dya_res
        g_wout[l] = _matmul_tn(
            keep["cat"], dy,
            pl.BlockSpec((tk, d_model), lambda p, k: (k, 0)),
            pl.BlockSpec((tk, d_model), lambda p, k: (k, 0)),
            SDS((d_model, d_model), BF16), pl.BlockSpec((d_model, d_model), lambda p, k: (0, 0)),
            1, t_len, tk, f"grad_wout_{l}")
        token = start_exchange([g_wout[l].reshape(N_DEV, o8, d_model)], f"wout_{l}")
        parts = [_attn_bwd(*keep["qkv"][i], dyas[i], keep["ya"][i], keep["lse"][i], l)
                 for i in range(len(DILATIONS))]
        dx, dproj, st_norm, qk_stats[l] = _mixer_bwd_in(
            parts, keep["proj"], dpc, dx, keep["x1"], prm[l][1] + token, wing, qkg[l], bd, l, tm)
        stats[l][1] = st_norm + st_gate
        g_win[l] = _matmul_tn(
            keep["h1"], dproj,
            pl.BlockSpec((tk, d_model), lambda p, k: (k, 0)),
            pl.BlockSpec((tk, c8), lambda p, k: (k, p)),
            SDS((N_DEV, d_model, c8), BF16), pl.BlockSpec((None, d_model, c8), lambda p, k: (p, 0, 0)),
            N_DEV, t_len, tk, f"grad_win_{l}")
        token = start_exchange([g_win[l]], f"win_{l}")
        dx, stats[l][0], token = ffn_backward(dx, keep, l, 0, 0, prm[l][0] + token)
    grad_x = dx.reshape(x.shape)

    def update(pieces, w, m, v, name):
        shape = w.shape
        cols = shape[-1]
        outs = _adamw([p.reshape(p.shape[0], -1, cols) for p in pieces], w.reshape(-1, cols),
                      m.reshape(-1, cols), v.reshape(-1, cols), name)
        return [o.reshape(shape) for o in outs]

    early = wait_exchange(started[:-1], dx, "grads_wait_early")
    lands = {}
    for i, l in enumerate(reversed(range(n_layer))):
        got = early[4 * i:4 * i + 4]
        lands[("ffn", l, 1)], lands[("wout", l)], lands[("win", l)] = got[0], got[1][0], got[2][0]
        if len(got) == 4:
            lands[("ffn", l, 0)] = got[3]
    res = {
        "w_in": update([lands[("win", l)] for l in range(n_layer)], w_in, m_w_in, v_w_in, "adamw_w_in"),
        "w_out": update([lands[("wout", l)] for l in range(n_layer)], w_out, m_w_out, v_w_out, "adamw_w_out"),
    }
    lands[("ffn", 0, 0)] = wait_exchange(started[-1:], res["w_in"][1], "grads_wait_last")[0]
    ffn_lands = [lands[("ffn", l, s)] for l in range(n_layer) for s in range(2)]
    res["ffn_w1"] = update([p[0] for p in ffn_lands], ffn_w1, m_ffn_w1, v_ffn_w1, "adamw_ffn_w1")
    res["ffn_w2"] = update([p[1] for p in ffn_lands], ffn_w2, m_ffn_w2, v_ffn_w2, "adamw_ffn_w2")

    dmod = jnp.stack([jnp.stack([stats[l][sub][r] for sub in range(3) for r in (0, 1, 3)]) for l in range(n_layer)])
    dng = jnp.stack([jnp.stack([stats[l][sub][2] for sub in range(3)]) for l in range(n_layer)])
    dcw = jnp.stack([conv_stats[l][0:3] for l in range(n_layer)])
    dcb = jnp.stack([conv_stats[l][3] for l in range(n_layer)])
    dqk = jnp.stack([qk_stats[l][0:2].reshape(2, n_head, HEAD_DIM).sum(axis=1) for l in range(n_layer)])
    part_shapes = [dmod.shape, dng.shape, dcw.shape, dcb.shape, dqk.shape]
    parts_all = _allgather_small(_pack([dmod, dng, dcw, dcb, dqk]), "gather_small_grads")
    dmod_all = _unpack(parts_all, part_shapes)[0].reshape(N_DEV, n_layer, 9 * d_model)
    summed = _unpack(_sum_devices(parts_all), part_shapes)
    g_b_ada = summed[0].reshape(n_layer, 9 * d_model)
    ng_cols = norm_g.shape[-1]
    g_norm_g = lax.dynamic_slice_in_dim(summed[1], me * ng_cols, ng_cols, axis=2)
    cw_cols = conv_w.shape[-1]
    g_conv_w = lax.dynamic_slice_in_dim(summed[2], me * cw_cols, cw_cols, axis=2)
    g_conv_b = summed[3]
    g_q, g_k = summed[4][:, 0], summed[4][:, 1]
    dmod_loc = lax.dynamic_slice_in_dim(dmod_all, me * ada_cols, ada_cols, axis=2)
    g_w_ada = _w_ada_grad(silu_c.T, jnp.moveaxis(dmod_loc, 0, 1))
    res["w_ada"] = update([g_w_ada[None]], w_ada, m_w_ada, v_w_ada, "adamw_w_ada")

    names = ["b_ada", "norm_g", "q_norm_g", "k_norm_g", "conv_w", "conv_b"]
    g_small = [g_b_ada, g_norm_g, g_q, g_k, g_conv_w, g_conv_b]
    w_small = [b_ada, norm_g, q_norm_g, k_norm_g, conv_w, conv_b]
    m_small = [m_b_ada, m_norm_g, m_q_norm_g, m_k_norm_g, m_conv_w, m_conv_b]
    v_small = [v_b_ada, v_norm_g, v_q_norm_g, v_k_norm_g, v_conv_w, v_conv_b]
    outs = _adamw([_pack(g_small)[None]], _pack(w_small), _pack(m_small), _pack(v_small), "adamw_small")
    shapes = [w.shape for w in w_small]
    unpacked = [_unpack(o, shapes) for o in outs]
    for i, nme in enumerate(names):
        res[nme] = [unpacked[k][i] for k in range(4)]

    order = ["w_ada", "b_ada", "norm_g", "w_in", "q_norm_g", "k_norm_g", "conv_w", "conv_b", "w_out", "ffn_w1", "ffn_w2"]
    return (loss, grad_x, *[res[n][0] for n in order], *[res[n][1] for n in order],
            *[res[n][2] for n in order], *[res[n][3] for n in order])
```

```python
import jax
import jax.numpy as jnp
from jax import lax
from jax.experimental import pallas as pl
from jax.experimental.pallas import tpu as pltpu

F32 = jnp.float32
BF16 = jnp.bfloat16
SDS = jax.ShapeDtypeStruct

N_DEV = 8
HEAD_DIM = 64
BLOCK = 128
DILATIONS = (1, 4, 16)
N_CHUNK = 4
EPS = 1e-6
NEG = -1e30
SM_SCALE = HEAD_DIM ** -0.5
LANES = 128
TOKEN_TILE = 512
VMEM_LIMIT_BYTES = 56 * 1024 * 1024

ADAM_LR = 0.001
ADAM_B1 = 0.9
ADAM_B2 = 0.999
ADAM_EPS = 1e-08
ADAM_WD = 0.01
ADAM_STEP = 10

MESH_ID = pl.DeviceIdType.MESH
ANY = pl.BlockSpec(memory_space=pl.ANY)
HBM_SPEC = pl.BlockSpec(memory_space=pltpu.HBM)
SEM_SPEC = pl.BlockSpec(memory_space=pltpu.SEMAPHORE)
SIDE_EFFECT = pltpu.SideEffectType.DATAFLOW_SIDE_EFFECTING


def _params(n_axes):
    return pltpu.CompilerParams(dimension_semantics=("arbitrary",) * n_axes,
                                vmem_limit_bytes=VMEM_LIMIT_BYTES)


def _dot(a, b):
    return jnp.dot(a, b, preferred_element_type=F32)


def _dot_nt(a, b):
    return lax.dot_general(a, b, (((1,), (1,)), ((), ())), preferred_element_type=F32)


def _dot_tn(a, b):
    return lax.dot_general(a, b, (((0,), (0,)), ((), ())), preferred_element_type=F32)


def _split_bf16(t):
    hi = t.astype(BF16)
    return hi, (t - hi.astype(F32)).astype(BF16)


def _my_place():
    x, y, c = lax.axis_index("x"), lax.axis_index("y"), lax.axis_index("c")
    return x, y, c


def _peer(place, k):
    x, y, c = place
    return ((1 - x) if k & 4 else x, (1 - y) if k & 2 else y, (1 - c) if k & 1 else c)


def _index(place):
    return 4 * place[0] + 2 * place[1] + place[2]


def _allgather_small(v, name):
    rows, cols = v.shape

    def body(x_ref, out_ref, send_sems, recv_sems, local_sem):
        me = _my_place()
        mine = pltpu.make_async_copy(x_ref, out_ref.at[_index(me)], local_sem)
        mine.start()
        sends = []
        for k in range(1, N_DEV):
            cp = pltpu.make_async_remote_copy(
                src_ref=x_ref, dst_ref=out_ref.at[_index(me)], send_sem=send_sems.at[k - 1],
                recv_sem=recv_sems.at[k - 1], device_id=_peer(me, k), device_id_type=MESH_ID)
            cp.start()
            sends.append(cp)
        for k in range(1, N_DEV):
            pltpu.make_async_remote_copy(
                src_ref=x_ref, dst_ref=out_ref.at[_index(_peer(me, k))], send_sem=send_sems.at[k - 1],
                recv_sem=recv_sems.at[k - 1], device_id=_peer(me, k), device_id_type=MESH_ID).wait_recv()
        for cp in sends:
            cp.wait_send()
        mine.wait()

    return pl.pallas_call(
        body, name=name,
        out_shape=SDS((N_DEV, rows, cols), v.dtype),
        in_specs=[pl.BlockSpec(memory_space=pltpu.VMEM)],
        out_specs=pl.BlockSpec(memory_space=pltpu.VMEM),
        scratch_shapes=[pltpu.SemaphoreType.DMA((N_DEV - 1,)), pltpu.SemaphoreType.DMA((N_DEV - 1,)),
                        pltpu.SemaphoreType.DMA],
    )(v)


def _allgather_big(lands, name):
    n = len(lands)

    def body(*refs):
        outs = refs[n:2 * n]
        send_sems, recv_sems = refs[2 * n:]
        x, y, c = _my_place()
        me, sib = (x, y, c), (x, y, 1 - c)
        chips = [(1 - x, y), (x, 1 - y), (1 - x, 1 - y)]

        def copy(i, k, block, to):
            dst = outs[i].at[_index(block)]
            return pltpu.make_async_remote_copy(
                src_ref=dst, dst_ref=dst, send_sem=send_sems.at[7 * i + k],
                recv_sem=recv_sems.at[7 * i + k], device_id=to, device_id_type=MESH_ID)

        first = []
        for i in range(n):
            first.append(copy(i, 0, me, sib))
            for j, chip in enumerate(chips):
                first.append(copy(i, 1 + j, me, (*chip, c)))
        for cp in first:
            cp.start()
        passed = []
        for i in range(n):
            for j, chip in enumerate(chips):
                copy(i, 1 + j, (*chip, c), me).wait_recv()
                fwd = copy(i, 4 + j, (*chip, c), sib)
                fwd.start()
                passed.append(fwd)
        for i in range(n):
            copy(i, 0, sib, me).wait_recv()
            for j, chip in enumerate(chips):
                copy(i, 4 + j, (*chip, 1 - c), me).wait_recv()
        for cp in first + passed:
            cp.wait_send()

    return pl.pallas_call(
        body, name=name,
        out_shape=[SDS(a.shape, a.dtype) for a in lands],
        in_specs=[ANY] * n, out_specs=[ANY] * n,
        input_output_aliases={i: i for i in range(n)},
        scratch_shapes=[pltpu.SemaphoreType.DMA((7 * n,)), pltpu.SemaphoreType.DMA((7 * n,))],
    )(*lands)


def _place_own(src, lead, me_arr, name):
    rows, cols = src.shape[-2:]
    tr = _row_tile(rows, cols)

    def body(me_ref, s_ref, o_ref):
        o_ref[...] = s_ref[...].astype(BF16)

    if lead is None:
        in_spec = pl.BlockSpec((None, tr, cols), lambda i, me_ref: (me_ref[0], i, 0))
    else:
        in_spec = pl.BlockSpec((None,) * len(lead) + (tr, cols), lambda i, me_ref: (*lead, i, 0))
    return pl.pallas_call(
        body, name=name,
        grid_spec=pltpu.PrefetchScalarGridSpec(
            num_scalar_prefetch=1, grid=(rows // tr,), in_specs=[in_spec],
            out_specs=pl.BlockSpec((None, tr, cols), lambda i, me_ref: (me_ref[0], i, 0))),
        out_shape=SDS((N_DEV, rows, cols), BF16),
        compiler_params=_params(1),
    )(me_arr, src)


def _send_start(lands, pieces, after, name):
    n = len(lands)
    arrs = list(lands) + list(pieces or [])
    n_a = len(arrs)

    def body(*refs):
        land_r, piece_r = refs[:n], refs[n:n_a]
        send, recv = refs[n_a + 1:n_a + 1 + n], refs[n_a + 1 + n:n_a + 1 + 2 * n]
        token = refs[-1]
        me = _my_place()
        for i in range(n):
            for k in range(1, N_DEV):
                peer = _peer(me, k)
                src = piece_r[i].at[_index(peer)] if pieces else land_r[i].at[_index(me)]
                pltpu.make_async_remote_copy(
                    src_ref=src, dst_ref=land_r[i].at[_index(me)], send_sem=send[i], recv_sem=recv[i],
                    device_id=peer, device_id_type=MESH_ID).start()
        token[...] = jnp.zeros_like(token)

    outs = pl.pallas_call(
        body, name=name,
        out_shape=[pltpu.SemaphoreType.DMA(())] * (2 * n) + [pltpu.HBM(a.shape, a.dtype) for a in arrs]
        + [SDS((8, LANES), F32)],
        in_specs=[HBM_SPEC] * n_a + [ANY],
        out_specs=[SEM_SPEC] * (2 * n) + [HBM_SPEC] * n_a + [pl.BlockSpec(memory_space=pltpu.VMEM)],
        input_output_aliases={i: 2 * n + i for i in range(n_a)},
        compiler_params=pltpu.CompilerParams(has_side_effects=SIDE_EFFECT),
    )(*[pltpu.with_memory_space_constraint(a, pltpu.HBM) for a in arrs], after)
    return outs[:n], outs[n:2 * n], outs[2 * n:2 * n + n_a], outs[-1]


def _send_wait(thru, sends, recvs, after, name):
    n = len(sends)
    n_a = len(thru)

    def body(*refs):
        land_r = refs[:n]
        send, recv = refs[n_a:n_a + n], refs[n_a + n:n_a + 2 * n]
        me = _my_place()
        for i in range(n):
            seven = land_r[i].at[pl.ds(0, N_DEV - 1)]
            cp = pltpu.make_async_remote_copy(src_ref=seven, dst_ref=seven, send_sem=send[i], recv_sem=recv[i],
                                              device_id=_peer(me, 1), device_id_type=MESH_ID)
            cp.wait_send()
            cp.wait_recv()

    outs = pl.pallas_call(
        body, name=name,
        out_shape=[pltpu.HBM(a.shape, a.dtype) for a in thru],
        in_specs=[HBM_SPEC] * n_a + [SEM_SPEC] * (2 * n) + [ANY],
        out_specs=[HBM_SPEC] * n_a,
        input_output_aliases={i: i for i in range(n_a)},
        compiler_params=pltpu.CompilerParams(has_side_effects=SIDE_EFFECT),
    )(*thru, *sends, *recvs, after)
    return outs[:n]


def _ada_fwd(x, prm_ref):
    rstd = lax.rsqrt(jnp.mean(x * x, axis=-1, keepdims=True) + EPS)
    return (x * rstd * prm_ref[0:1, :]) * (1.0 + prm_ref[2:3, :]) + prm_ref[1:2, :]


def _ada_bwd(dh, x, prm_ref):
    ng, ops = prm_ref[0:1, :], 1.0 + prm_ref[2:3, :]
    rstd = lax.rsqrt(jnp.mean(x * x, axis=-1, keepdims=True) + EPS)
    xhat = x * rstd
    dxhat = dh * (ops * ng)
    dx = rstd * (dxhat - xhat * jnp.mean(dxhat * xhat, axis=-1, keepdims=True))
    dshift = jnp.sum(dh, axis=0, keepdims=True)
    dscale = jnp.sum(dh * (xhat * ng), axis=0, keepdims=True)
    dng = jnp.sum(dh * (xhat * ops), axis=0, keepdims=True)
    return dx, dshift, dscale, dng


def _head_mean(t, bd):
    hi, lo = _split_bf16(t)
    return (_dot(hi, bd) + _dot(lo, bd)) * (1.0 / HEAD_DIM)


def _shift_rows(z, before, tm):
    row = lax.broadcasted_iota(jnp.int32, z.shape, 0)
    z1 = jnp.where(row == 0, before[7:8, :], pltpu.roll(z, 1, 0))
    z2 = jnp.where(row == 0, before[6:7, :], jnp.where(row == 1, before[7:8, :], pltpu.roll(z, 2, 0)))
    return z1, z2


def _mod_part(c_all, w_ada, b_loc):
    n_layer, d_model, cols = w_ada.shape

    def body(c_ref, w_ref, b_ref, out_ref, sc_ref):
        cv = c_ref[...]
        sc = cv * jax.nn.sigmoid(cv)
        sc_ref[...] = sc
        a_hi, a_lo = _split_bf16(sc)
        w_hi, w_lo = _split_bf16(w_ref[...])
        out_ref[...] = _dot(a_hi, w_hi) + _dot(a_hi, w_lo) + _dot(a_lo, w_hi) + b_ref[...]

    return pl.pallas_call(
        body, name="mod_part", grid=(n_layer,),
        in_specs=[pl.BlockSpec((N_DEV, d_model), lambda l: (0, 0)),
                  pl.BlockSpec((None, d_model, cols), lambda l: (l, 0, 0)),
                  pl.BlockSpec((None, 1, cols), lambda l: (l, 0, 0))],
        out_specs=[pl.BlockSpec((None, N_DEV, cols), lambda l: (l, 0, 0)),
                   pl.BlockSpec((N_DEV, d_model), lambda l: (0, 0))],
        out_shape=[SDS((n_layer, N_DEV, cols), F32), SDS((N_DEV, d_model), F32)],
        compiler_params=_params(1),
    )(c_all, w_ada, b_loc)


def _w_ada_grad(sc_t, dmod_loc):
    d_model = sc_t.shape[0]
    n_layer, _, cols = dmod_loc.shape

    def body(s_ref, d_ref, out_ref):
        acc = s_ref[:, 0:1] * d_ref[0:1, :]
        for b in range(1, N_DEV):
            acc = acc + s_ref[:, b:b + 1] * d_ref[b:b + 1, :]
        out_ref[...] = acc

    return pl.pallas_call(
        body, name="w_ada_grad", grid=(n_layer,),
        in_specs=[pl.BlockSpec((d_model, N_DEV), lambda l: (0, 0)),
                  pl.BlockSpec((None, N_DEV, cols), lambda l: (l, 0, 0))],
        out_specs=pl.BlockSpec((None, d_model, cols), lambda l: (l, 0, 0)),
        out_shape=SDS((n_layer, d_model, cols), F32),
        compiler_params=_params(1),
    )(sc_t, dmod_loc)


def _sum_devices(g):
    _, rows, cols = g.shape

    def body(g_ref, out_ref):
        acc = g_ref[0]
        for i in range(1, N_DEV):
            acc = acc + g_ref[i]
        out_ref[...] = acc

    return pl.pallas_call(body, name="sum_devices", out_shape=SDS((rows, cols), F32))(g)


def _ffn_fwd(x, prm, w1r, w2g, l, sub, tm):
    t_len, d_model = x.shape
    f8, r8 = w1r.shape[-1], w2g.shape[-2]

    def body(x_ref, prm_ref, w1_ref, w2_ref, gu_ref, h_ref, y_ref, xo_ref, h_sc, acc):
        j = pl.program_id(1)

        @pl.when(j == 0)
        def _():
            h = _ada_fwd(x_ref[...], prm_ref).astype(BF16)
            h_sc[...] = h
            h_ref[...] = h
            acc[...] = jnp.zeros_like(acc)

        hb = h_sc[...]
        g = _dot(hb, w1_ref[0])
        u = _dot(hb, w1_ref[1])
        gu_ref[0] = g.astype(BF16)
        gu_ref[1] = u.astype(BF16)
        s = (g * jax.nn.sigmoid(g) * u).astype(BF16)
        acc[...] += _dot(s, w2_ref[...].reshape(2 * r8, d_model))

        @pl.when(j == N_CHUNK - 1)
        def _():
            yv = acc[...]
            y_ref[...] = yv.astype(BF16)
            xo_ref[...] = x_ref[...] + (0.5 * prm_ref[3:4, :]) * yv

    tile = pl.BlockSpec((tm, d_model), lambda m, j: (m, 0))
    return pl.pallas_call(
        body, name=f"ffn_fwd_{l}_{sub}", grid=(t_len // tm, N_CHUNK),
        in_specs=[tile, pl.BlockSpec((8, d_model), lambda m, j: (0, 0)),
                  pl.BlockSpec((2, None, d_model, f8), lambda m, j: (0, j, 0, 0)),
                  pl.BlockSpec((2, r8, d_model), lambda m, j: (j, 0, 0))],
        out_specs=[pl.BlockSpec((2, None, tm, f8), lambda m, j: (0, j, m, 0)), tile, tile, tile],
        out_shape=[SDS((2, N_CHUNK, t_len, f8), BF16), SDS((t_len, d_model), BF16),
                   SDS((t_len, d_model), BF16), SDS((t_len, d_model), F32)],
        scratch_shapes=[pltpu.VMEM((tm, d_model), BF16), pltpu.VMEM((tm, d_model), F32)],
        compiler_params=_params(2),
    )(x, prm, w1r, w2g)


def _ffn_bwd(dxo, y, x_in, prm, gu, w1r, w2g, l, sub, tm):
    t_len, d_model = x_in.shape
    f8, r8 = w1r.shape[-1], w2g.shape[-2]
    n_m = t_len // tm

    def body(dxo_ref, y_ref, x_ref, prm_ref, gu_ref, w1_ref, w2_ref,
             dx_ref, da_ref, s_ref, dy_ref, st_ref, dy_sc, acc):
        m, j = pl.program_id(0), pl.program_id(1)

        @pl.when((m == 0) & (j == 0))
        def _():
            st_ref[...] = jnp.zeros_like(st_ref)

        @pl.when(j == 0)
        def _():
            dxo_v = dxo_ref[...]
            dy = ((0.5 * prm_ref[3:4, :]) * dxo_v).astype(BF16)
            dy_sc[...] = dy
            dy_ref[...] = dy
            st_ref[3:4, :] += jnp.sum(0.5 * dxo_v * y_ref[...].astype(F32), axis=0, keepdims=True)
            acc[...] = jnp.zeros_like(acc)

        ds = _dot_nt(dy_sc[...], w2_ref[...].reshape(2 * r8, d_model))
        g = gu_ref[0].astype(F32)
        u = gu_ref[1].astype(F32)
        sig = jax.nn.sigmoid(g)
        sil = g * sig
        s_ref[...] = (sil * u).astype(BF16)
        dg = (ds * u * (sig * (1.0 + g * (1.0 - sig)))).astype(BF16)
        du = (ds * sil).astype(BF16)
        da_ref[0] = dg
        da_ref[1] = du
        acc[...] += _dot_nt(dg, w1_ref[0]) + _dot_nt(du, w1_ref[1])

        @pl.when(j == N_CHUNK - 1)
        def _():
            dx, dshift, dscale, dng = _ada_bwd(acc[...], x_ref[...], prm_ref)
            dx_ref[...] = dxo_ref[...] + dx
            st_ref[0:1, :] += dshift
            st_ref[1:2, :] += dscale
            st_ref[2:3, :] += dng

    tile = pl.BlockSpec((tm, d_model), lambda m, j: (m, 0))
    fixed = pl.BlockSpec((8, d_model), lambda m, j: (0, 0))
    return pl.pallas_call(
        body, name=f"ffn_bwd_{l}_{sub}", grid=(n_m, N_CHUNK),
        in_specs=[tile, tile, tile, fixed,
                  pl.BlockSpec((2, None, tm, f8), lambda m, j: (0, j, m, 0)),
                  pl.BlockSpec((2, None, d_model, f8), lambda m, j: (0, j, 0, 0)),
                  pl.BlockSpec((2, r8, d_model), lambda m, j: (j, 0, 0))],
        out_specs=[tile, pl.BlockSpec((2, None, tm, f8), lambda m, j: (0, j, m, 0)),
                   pl.BlockSpec((None, tm, f8), lambda m, j: (j, m, 0)), tile, fixed],
        out_shape=[SDS((t_len, d_model), F32), SDS((2, N_CHUNK, t_len, f8), BF16),
                   SDS((N_CHUNK, t_len, f8), BF16), SDS((t_len, d_model), BF16), SDS((8, d_model), F32)],
        scratch_shapes=[pltpu.VMEM((tm, d_model), BF16), pltpu.VMEM((tm, d_model), F32)],
        compiler_params=_params(2),
    )(dxo, y, x_in, prm, gu, w1r, w2g)


def _matmul_tn(a, b, a_spec, b_spec, out_shape, out_spec, n_piece, t_len, tk, name, after):
    def body(a_ref, b_ref, after_ref, o_ref, acc):
        k = pl.program_id(1)

        @pl.when(k == 0)
        def _():
            acc[...] = jnp.zeros_like(acc)

        acc[...] += _dot_tn(a_ref[...], b_ref[...])

        @pl.when(k == pl.num_programs(1) - 1)
        def _():
            o_ref[...] = acc[...].astype(o_ref.dtype)

    acc_shape = tuple(d for d in out_spec.block_shape if d is not None)
    return pl.pallas_call(
        body, name=name, grid=(n_piece, t_len // tk),
        in_specs=[a_spec, b_spec, ANY], out_specs=out_spec, out_shape=out_shape,
        scratch_shapes=[pltpu.VMEM(acc_shape, F32)],
        compiler_params=_params(2),
    )(a, b, after)


def _groups(tm, width):
    return pltpu.VMEM((width // LANES, tm, LANES), F32)


def _to_groups(val, dst_ref, first=0):
    for g in range(val.shape[1] // LANES):
        dst_ref[first + g] = val[:, g * LANES:(g + 1) * LANES]


def _from_groups(src_ref):
    return jnp.concatenate([src_ref[g] for g in range(src_ref.shape[0])], axis=1)


def _split_residues(src_ref, out_ref, dil, tm):
    for g in range(src_ref.shape[0]):
        for r in range(dil):
            out_ref[r, :, g * LANES:(g + 1) * LANES] = (
                src_ref.at[g][pl.ds(r, tm // dil, stride=dil), :].astype(out_ref.dtype))


def _merge_residues(in_ref, dst_ref, dil, tm, add=False):
    for g in range(dst_ref.shape[0]):
        for r in range(dil):
            rows = pl.ds(r, tm // dil, stride=dil)
            val = in_ref[r, :, g * LANES:(g + 1) * LANES].astype(F32)
            dst = dst_ref.at[g]
            dst[rows, :] = (dst[rows, :] + val) if add else val


def _res_spec(dil, tm, width, index=lambda m: m):
    return pl.BlockSpec((dil, tm // dil, width), lambda m: (0, index(m), 0))


def _mixer_in(x, prm, wing, qkg, cprm, bd, l, tm):
    t_len, d_model = x.shape
    a_w = d_model // 2
    c_w = d_model - a_w
    c8 = wing.shape[-1]
    p_c = N_DEV * c8
    dils = DILATIONS[1:]

    def body(x_ref, prm_ref, w_ref, qkg_ref, cprm_ref, bd_ref, proj_ref, qkn_ref, h_ref, ycv_ref, *rest):
        res_refs, (pf, qsc, vsc, carry) = rest[:2 * len(dils)], rest[2 * len(dils):]

        @pl.when(pl.program_id(0) == 0)
        def _():
            carry[...] = jnp.zeros_like(carry)

        h = _ada_fwd(x_ref[...], prm_ref).astype(BF16)
        h_ref[...] = h
        for j in range(N_DEV):
            pf[:, j * c8:(j + 1) * c8] = _dot(h, w_ref[j])
        proj_ref[...] = pf[...].astype(BF16)
        bdm = bd_ref[...]
        for i in range(2):
            t = pf[:, i * a_w:(i + 1) * a_w]
            rstd = lax.rsqrt(_head_mean(t * t, bdm) + EPS)
            tn = t * rstd * qkg_ref[i:i + 1, :]
            qkn_ref[:, i * a_w:(i + 1) * a_w] = tn.astype(BF16)
            _to_groups(tn, qsc, i * (a_w // LANES))
        _to_groups(pf[:, 2 * a_w:3 * a_w], vsc)
        for i, dil in enumerate(dils):
            _split_residues(qsc, res_refs[2 * i], dil, tm)
            _split_residues(vsc, res_refs[2 * i + 1], dil, tm)
        gb = pf[:, 3 * a_w:3 * a_w + c_w]
        z = pf[:, 3 * a_w + c_w:3 * a_w + 2 * c_w] * pf[:, 3 * a_w + 2 * c_w:3 * a_w + 3 * c_w]
        z1, z2 = _shift_rows(z, carry[...], tm)
        cv = cprm_ref[0:1, :] * z2 + cprm_ref[1:2, :] * z1 + cprm_ref[2:3, :] * z + cprm_ref[3:4, :]
        ycv_ref[...] = (gb * cv).astype(BF16)
        carry[...] = z[tm - 8:tm, :]

    res_specs, res_shapes = [], []
    for dil in dils:
        res_specs += [_res_spec(dil, tm, 2 * a_w), _res_spec(dil, tm, a_w)]
        res_shapes += [SDS((dil, t_len // dil, 2 * a_w), BF16), SDS((dil, t_len // dil, a_w), BF16)]
    return pl.pallas_call(
        body, name=f"mixer_in_{l}", grid=(t_len // tm,),
        in_specs=[pl.BlockSpec((tm, d_model), lambda m: (m, 0)),
                  pl.BlockSpec((8, d_model), lambda m: (0, 0)),
                  pl.BlockSpec((N_DEV, d_model, c8), lambda m: (0, 0, 0)),
                  pl.BlockSpec((8, a_w), lambda m: (0, 0)),
                  pl.BlockSpec((8, c_w), lambda m: (0, 0)),
                  pl.BlockSpec((a_w, a_w), lambda m: (0, 0))],
        out_specs=[pl.BlockSpec((tm, p_c), lambda m: (m, 0)),
                   pl.BlockSpec((tm, 2 * a_w), lambda m: (m, 0)),
                   pl.BlockSpec((tm, d_model), lambda m: (m, 0)),
                   pl.BlockSpec((tm, c_w), lambda m: (m, 0))] + res_specs,
        out_shape=[SDS((t_len, p_c), BF16), SDS((t_len, 2 * a_w), BF16),
                   SDS((t_len, d_model), BF16), SDS((t_len, c_w), BF16)] + res_shapes,
        scratch_shapes=[pltpu.VMEM((tm, p_c), F32), _groups(tm, 2 * a_w), _groups(tm, a_w),
                        pltpu.VMEM((8, c_w), F32)],
        compiler_params=_params(1),
    )(x, prm, wing, qkg, cprm, bd)


def _band_mask(first_block):
    qi = lax.broadcasted_iota(jnp.int32, (BLOCK, 2 * BLOCK), 0)
    kj = lax.broadcasted_iota(jnp.int32, (BLOCK, 2 * BLOCK), 1)
    lowest = jnp.where(first_block, BLOCK, 0)
    return (kj >= qi) & (kj <= qi + BLOCK) & (kj >= lowest)


def _attn_fwd(qk, v, v_col, l):
    dil, seq, a2 = qk.shape
    a_w = a2 // 2
    n_b = seq // BLOCK

    def body(q_ref, kc_ref, kp_ref, vc_ref, vp_ref, o_ref, lse_ref):
        valid = _band_mask(pl.program_id(1) == 0)
        low = lax.broadcasted_iota(jnp.int32, (BLOCK, LANES), 1) < HEAD_DIM
        for p in range(a_w // LANES):
            sl = slice(p * LANES, (p + 1) * LANES)
            qp = q_ref[:, sl]
            k2 = jnp.concatenate([kp_ref[:, sl], kc_ref[:, sl]], axis=0)
            v2 = jnp.concatenate([vp_ref[:, sl], vc_ref[:, sl]], axis=0)
            o_h, lse_h = [], []
            for half in (low, ~low):
                s = _dot_nt(jnp.where(half, qp, jnp.zeros_like(qp)), k2) * SM_SCALE
                s = jnp.where(valid, s, NEG)
                mx = jnp.max(s, axis=-1, keepdims=True)
                e = jnp.exp(s - mx)
                den = jnp.sum(e, axis=-1, keepdims=True)
                o_h.append(_dot((e / den).astype(BF16), v2))
                lse_h.append(mx + jnp.log(den))
            o_ref[:, sl] = jnp.where(low, o_h[0], o_h[1]).astype(BF16)
            lse_ref[:, sl] = jnp.where(low, lse_h[0], lse_h[1])

    def blk(f):
        return pl.BlockSpec((None, BLOCK, a_w), f)

    return pl.pallas_call(
        body, name=f"attn_fwd_{l}_d{dil}", grid=(dil, n_b),
        in_specs=[blk(lambda r, b: (r, b, 0)),
                  blk(lambda r, b: (r, b, 1)),
                  blk(lambda r, b: (r, jnp.maximum(b - 1, 0), 1)),
                  blk(lambda r, b: (r, b, v_col)),
                  blk(lambda r, b: (r, jnp.maximum(b - 1, 0), v_col))],
        out_specs=[blk(lambda r, b: (r, b, 0)), blk(lambda r, b: (r, b, 0))],
        out_shape=[SDS((dil, seq, a_w), BF16), SDS((dil, seq, a_w), F32)],
        compiler_params=_params(2),
    )(qk, qk, qk, v, v)


def _attn_combine(outs, lses, ycv, tm, l):
    t_len, c_w = ycv.shape
    a_w = outs[0].shape[-1]
    dils = DILATIONS[1:]
    n_r = len(dils)

    def body(o1, l1, *rest):
        o_res, l_res, y_ref = rest[:n_r], rest[n_r:2 * n_r], rest[2 * n_r]
        cat_ref, lse_ref = rest[2 * n_r + 1], rest[2 * n_r + 2]
        ya_res, lse_res = rest[2 * n_r + 3:3 * n_r + 3], rest[3 * n_r + 3:4 * n_r + 3]
        scr = rest[4 * n_r + 3:]
        so, sl, sya, slse = scr[:n_r], scr[n_r:2 * n_r], scr[2 * n_r], scr[2 * n_r + 1]
        for i, dil in enumerate(dils):
            _merge_residues(o_res[i], so[i], dil, tm)
            _merge_residues(l_res[i], sl[i], dil, tm)
        ls = [l1[...]] + [_from_groups(s) for s in sl]
        ov = [o1[...].astype(F32)] + [_from_groups(s) for s in so]
        mx = jnp.maximum(jnp.maximum(ls[0], ls[1]), ls[2])
        es = [jnp.exp(t - mx) for t in ls]
        den = es[0] + es[1] + es[2]
        ya = (es[0] / den) * ov[0] + (es[1] / den) * ov[1] + (es[2] / den) * ov[2]
        lse = mx + jnp.log(den)
        _to_groups(ya, sya)
        _to_groups(lse, slse)
        cat_ref[:, :a_w] = ya.astype(BF16)
        cat_ref[:, a_w:] = y_ref[...]
        lse_ref[...] = lse
        for i, dil in enumerate(dils):
            _split_residues(sya, ya_res[i], dil, tm)
            _split_residues(slse, lse_res[i], dil, tm)

    ta = pl.BlockSpec((tm, a_w), lambda m: (m, 0))
    res = [_res_spec(dil, tm, a_w) for dil in dils]
    return pl.pallas_call(
        body, name=f"attn_combine_{l}", grid=(t_len // tm,),
        in_specs=[ta, ta] + res + res + [pl.BlockSpec((tm, c_w), lambda m: (m, 0))],
        out_specs=[pl.BlockSpec((tm, a_w + c_w), lambda m: (m, 0)), ta] + res + res,
        out_shape=[SDS((t_len, a_w + c_w), BF16), SDS((t_len, a_w), F32)]
        + [SDS((dil, t_len // dil, a_w), BF16) for dil in dils]
        + [SDS((dil, t_len // dil, a_w), F32) for dil in dils],
        scratch_shapes=[_groups(tm, a_w)] * (2 * n_r + 2),
        compiler_params=_params(1),
    )(outs[0].reshape(t_len, a_w), lses[0].reshape(t_len, a_w), *outs[1:], *lses[1:], ycv)


def _mixer_out(cat, x, prm, woutg, l, tm):
    t_len, d_model = x.shape
    r8 = woutg.shape[-2]

    def body(cat_ref, x_ref, prm_ref, w_ref, xo_ref, y_ref):
        yv = _dot(cat_ref[...], w_ref[...].reshape(N_DEV * r8, d_model))
        y_ref[...] = yv.astype(BF16)
        xo_ref[...] = x_ref[...] + prm_ref[3:4, :] * yv

    tile = pl.BlockSpec((tm, d_model), lambda m: (m, 0))
    return pl.pallas_call(
        body, name=f"mixer_out_{l}", grid=(t_len // tm,),
        in_specs=[tile, tile, pl.BlockSpec((8, d_model), lambda m: (0, 0)),
                  pl.BlockSpec((N_DEV, r8, d_model), lambda m: (0, 0, 0))],
        out_specs=[tile, tile],
        out_shape=[SDS((t_len, d_model), F32), SDS((t_len, d_model), BF16)],
        compiler_params=_params(1),
    )(cat, x, prm, woutg)


def _mixer_bwd_out(dxo, ymix, prm, woutg, proj, cprm, l, tm):
    t_len, d_model = dxo.shape
    a_w = d_model // 2
    c_w = d_model - a_w
    r8 = woutg.shape[-2]
    n_m = t_len // tm
    hb = tm // 8

    dils = DILATIONS[1:]

    def body(dxo_ref, ym_ref, prm_ref, w_ref, pc_ref, halo_ref, cprm_ref,
             dya_ref, dpc_ref, dy_ref, st_ref, cs_ref, *rest):
        dya_res, (carry, dsc) = rest[:len(dils)], rest[len(dils):]
        step = pl.program_id(0)
        tile_i = n_m - 1 - step

        @pl.when(step == 0)
        def _():
            st_ref[...] = jnp.zeros_like(st_ref)
            cs_ref[...] = jnp.zeros_like(cs_ref)
            carry[...] = jnp.zeros_like(carry)

        dxo_v = dxo_ref[...]
        dy = (prm_ref[3:4, :] * dxo_v).astype(BF16)
        dy_ref[...] = dy
        st_ref[3:4, :] += jnp.sum(dxo_v * ym_ref[...].astype(F32), axis=0, keepdims=True)
        dcat = _dot_nt(dy, w_ref[...].reshape(N_DEV * r8, d_model))
        _to_groups(dcat[:, :a_w], dsc)
        dya_ref[...] = dcat[:, :a_w].astype(BF16)
        for i, dil in enumerate(dils):
            _split_residues(dsc, dya_res[i], dil, tm)
        dyc = dcat[:, a_w:]
        gb = pc_ref[:, :c_w].astype(F32)
        gc = pc_ref[:, c_w:2 * c_w].astype(F32)
        u = pc_ref[:, 2 * c_w:].astype(F32)
        z = gc * u
        before = halo_ref[:, c_w:2 * c_w].astype(F32) * halo_ref[:, 2 * c_w:].astype(F32)
        before = jnp.where(tile_i > 0, before, jnp.zeros_like(before))
        z1, z2 = _shift_rows(z, before, tm)
        w0, w1, w2 = cprm_ref[0:1, :], cprm_ref[1:2, :], cprm_ref[2:3, :]
        cv = w0 * z2 + w1 * z1 + w2 * z + cprm_ref[3:4, :]
        dcv = dyc * gb
        cs_ref[0:1, :] += jnp.sum(dcv * z2, axis=0, keepdims=True)
        cs_ref[1:2, :] += jnp.sum(dcv * z1, axis=0, keepdims=True)
        cs_ref[2:3, :] += jnp.sum(dcv * z, axis=0, keepdims=True)
        cs_ref[3:4, :] += jnp.sum(dcv, axis=0, keepdims=True)
        row = lax.broadcasted_iota(jnp.int32, dcv.shape, 0)
        after = carry[...]
        d1 = jnp.where(row == tm - 1, after[0:1, :], pltpu.roll(dcv, tm - 1, 0))
        d2 = jnp.where(row == tm - 2, after[0:1, :],
                       jnp.where(row == tm - 1, after[1:2, :], pltpu.roll(dcv, tm - 2, 0)))
        dz = w2 * dcv + w1 * d1 + w0 * d2
        dpc_ref[:, :c_w] = (dyc * cv).astype(BF16)
        dpc_ref[:, c_w:2 * c_w] = (dz * u).astype(BF16)
        dpc_ref[:, 2 * c_w:] = (dz * gc).astype(BF16)
        carry[...] = dcv[0:8, :]

    def rev(width, col=0):
        return pl.BlockSpec((tm, width), lambda s: (n_m - 1 - s, col))

    fixed_d = pl.BlockSpec((8, d_model), lambda s: (0, 0))
    fixed_c = pl.BlockSpec((8, c_w), lambda s: (0, 0))
    return pl.pallas_call(
        body, name=f"mixer_bwd_out_{l}", grid=(n_m,),
        in_specs=[rev(d_model), rev(d_model), fixed_d,
                  pl.BlockSpec((N_DEV, r8, d_model), lambda s: (0, 0, 0)),
                  rev(3 * c_w, 1),
                  pl.BlockSpec((8, 3 * c_w), lambda s: (jnp.maximum((n_m - 1 - s) * hb - 1, 0), 1)),
                  fixed_c],
        out_specs=[rev(a_w), rev(3 * c_w), rev(d_model), fixed_d, fixed_c]
        + [_res_spec(dil, tm, a_w, lambda s: n_m - 1 - s) for dil in dils],
        out_shape=[SDS((t_len, a_w), BF16), SDS((t_len, 3 * c_w), BF16), SDS((t_len, d_model), BF16),
                   SDS((8, d_model), F32), SDS((8, c_w), F32)]
        + [SDS((dil, t_len // dil, a_w), BF16) for dil in dils],
        scratch_shapes=[pltpu.VMEM((8, c_w), F32), _groups(tm, a_w)],
        compiler_params=_params(1),
    )(dxo, ymix, prm, woutg, proj, proj, cprm)


def _attn_bwd(qk, v, v_col, dya, ya, lse, l):
    dil, seq, a2 = qk.shape
    a_w = a2 // 2
    n_b = seq // BLOCK

    def body(q_ref, kc_ref, kp_ref, vc_ref, vp_ref, do_ref, ya_ref, lse_ref,
             dq_ref, dk_ref, dv_ref, ck, cvv):
        j = pl.program_id(1)

        @pl.when(j < n_b)
        def _():
            valid = _band_mask(j == 0)
            low = lax.broadcasted_iota(jnp.int32, (BLOCK, LANES), 1) < HEAD_DIM
            for p in range(a_w // LANES):
                sl = slice(p * LANES, (p + 1) * LANES)
                qp = q_ref[:, sl]
                k2 = jnp.concatenate([kp_ref[:, sl], kc_ref[:, sl]], axis=0)
                v2 = jnp.concatenate([vp_ref[:, sl], vc_ref[:, sl]], axis=0)
                dop = do_ref[:, sl]
                prod = dop.astype(F32) * ya_ref[:, sl].astype(F32)
                lsep = lse_ref[:, sl]
                dq_h = []
                dk2 = jnp.zeros((2 * BLOCK, LANES), F32)
                dv2 = jnp.zeros((2 * BLOCK, LANES), F32)
                for hh, half in enumerate((low, ~low)):
                    qh = jnp.where(half, qp, jnp.zeros_like(qp))
                    doh = jnp.where(half, dop, jnp.zeros_like(dop))
                    dsum = jnp.sum(jnp.where(half, prod, 0.0), axis=-1, keepdims=True)
                    lse_h = lsep[:, hh * HEAD_DIM:hh * HEAD_DIM + 1]
                    s = _dot_nt(qh, k2) * SM_SCALE
                    pr = jnp.exp(jnp.where(valid, s, NEG) - lse_h)
                    dp = _dot_nt(doh, v2)
                    ds = (pr * (dp - dsum) * SM_SCALE).astype(BF16)
                    dq_h.append(_dot(ds, k2))
                    dk2 = dk2 + _dot_tn(ds, qh)
                    dv2 = dv2 + _dot_tn(pr.astype(BF16), doh)
                dq_ref[:, sl] = jnp.where(low, dq_h[0], dq_h[1]).astype(BF16)

                @pl.when(j > 0)
                def _():
                    dk_ref[:, sl] = (ck[:, sl] + dk2[:BLOCK]).astype(BF16)
                    dv_ref[:, sl] = (cvv[:, sl] + dv2[:BLOCK]).astype(BF16)

                ck[:, sl] = dk2[BLOCK:]
                cvv[:, sl] = dv2[BLOCK:]

        @pl.when(j == n_b)
        def _():
            dk_ref[...] = ck[...].astype(BF16)
            dv_ref[...] = cvv[...].astype(BF16)

    def blk(f):
        return pl.BlockSpec((None, BLOCK, a_w), f)

    def cur(j):
        return jnp.minimum(j, n_b - 1)

    def prev(j):
        return jnp.maximum(jnp.minimum(j, n_b - 1) - 1, 0)

    def late(j):
        return jnp.maximum(j - 1, 0)

    return pl.pallas_call(
        body, name=f"attn_bwd_{l}_d{dil}", grid=(dil, n_b + 1),
        in_specs=[blk(lambda r, j: (r, cur(j), 0)),
                  blk(lambda r, j: (r, cur(j), 1)),
                  blk(lambda r, j: (r, prev(j), 1)),
                  blk(lambda r, j: (r, cur(j), v_col)),
                  blk(lambda r, j: (r, prev(j), v_col)),
                  blk(lambda r, j: (r, cur(j), 0)),
                  blk(lambda r, j: (r, cur(j), 0)),
                  blk(lambda r, j: (r, cur(j), 0))],
        out_specs=[blk(lambda r, j: (r, cur(j), 0)), blk(lambda r, j: (r, late(j), 0)),
                   blk(lambda r, j: (r, late(j), 0))],
        out_shape=[SDS((dil, seq, a_w), BF16)] * 3,
        scratch_shapes=[pltpu.VMEM((BLOCK, a_w), F32), pltpu.VMEM((BLOCK, a_w), F32)],
        compiler_params=_params(2),
    )(qk, qk, qk, v, v, dya, ya, lse)


def _mixer_bwd_in(parts, proj, dpc, dxo, x_in, prm, wing, qkg, bd, l, tm):
    t_len, d_model = x_in.shape
    a_w = d_model // 2
    c_w = d_model - a_w
    c8 = wing.shape[-1]
    p_c = N_DEV * c8

    dils = DILATIONS[1:]
    n_in = 3 * len(DILATIONS)

    def body(*refs):
        nat, res = refs[:3], refs[3:n_in]
        (pa_ref, dpc_ref, dxo_ref, x_ref, prm_ref, w_ref, qkg_ref, bd_ref,
         dx_ref, dproj_ref, st_ref, qs_ref) = refs[n_in:n_in + 12]
        sums = refs[n_in + 12:]

        @pl.when(pl.program_id(0) == 0)
        def _():
            st_ref[...] = jnp.zeros_like(st_ref)
            qs_ref[...] = jnp.zeros_like(qs_ref)

        for t in range(3):
            _to_groups(nat[t][...].astype(F32), sums[t])
            for i, dil in enumerate(dils):
                _merge_residues(res[3 * i + t], sums[t], dil, tm, add=True)
        bdm = bd_ref[...]
        grads = [_from_groups(sums[0]), _from_groups(sums[1])]
        for i in range(2):
            t = pa_ref[:, i * a_w:(i + 1) * a_w].astype(F32)
            rstd = lax.rsqrt(_head_mean(t * t, bdm) + EPS)
            that = t * rstd
            qs_ref[i:i + 1, :] += jnp.sum(grads[i] * that, axis=0, keepdims=True)
            dhat = grads[i] * qkg_ref[i:i + 1, :]
            dt = rstd * (dhat - that * _head_mean(dhat * that, bdm))
            dproj_ref[:, i * a_w:(i + 1) * a_w] = dt.astype(BF16)
        dproj_ref[:, 2 * a_w:3 * a_w] = _from_groups(sums[2]).astype(BF16)
        dproj_ref[:, 3 * a_w:] = dpc_ref[...]
        dh = _dot_nt(dproj_ref[:, 0:c8], w_ref[0])
        for j in range(1, N_DEV):
            dh = dh + _dot_nt(dproj_ref[:, j * c8:(j + 1) * c8], w_ref[j])
        dx, dshift, dscale, dng = _ada_bwd(dh, x_ref[...], prm_ref)
        dx_ref[...] = dxo_ref[...] + dx
        st_ref[0:1, :] += dshift
        st_ref[1:2, :] += dscale
        st_ref[2:3, :] += dng

    ta = pl.BlockSpec((tm, a_w), lambda m: (m, 0))
    tile = pl.BlockSpec((tm, d_model), lambda m: (m, 0))
    return pl.pallas_call(
        body, name=f"mixer_bwd_in_{l}", grid=(t_len // tm,),
        in_specs=[ta] * 3 + [_res_spec(dil, tm, a_w) for dil in dils for _ in range(3)]
                 + [pl.BlockSpec((tm, 3 * a_w), lambda m: (m, 0)),
                             pl.BlockSpec((tm, 3 * c_w), lambda m: (m, 0)),
                             tile, tile, pl.BlockSpec((8, d_model), lambda m: (0, 0)),
                             pl.BlockSpec((N_DEV, d_model, c8), lambda m: (0, 0, 0)),
                             pl.BlockSpec((8, a_w), lambda m: (0, 0)),
                             pl.BlockSpec((a_w, a_w), lambda m: (0, 0))],
        out_specs=[tile, pl.BlockSpec((tm, p_c), lambda m: (m, 0)),
                   pl.BlockSpec((8, d_model), lambda m: (0, 0)), pl.BlockSpec((8, a_w), lambda m: (0, 0))],
        out_shape=[SDS((t_len, d_model), F32), SDS((t_len, p_c), BF16), SDS((8, d_model), F32), SDS((8, a_w), F32)],
        scratch_shapes=[_groups(tm, a_w)] * 3,
        compiler_params=_params(1),
    )(*[t.reshape(t_len, a_w) for t in parts[0]], *[t for p in parts[1:] for t in p],
      proj, dpc, dxo, x_in, prm, wing, qkg, bd)


def _loss_head(x, target, tm):
    t_len, d_model = x.shape

    def body(x_ref, t_ref, dx_ref, loss_ref):
        @pl.when(pl.program_id(0) == 0)
        def _():
            loss_ref[...] = jnp.zeros_like(loss_ref)

        diff = x_ref[...] - t_ref[...]
        dx_ref[...] = diff * (1.0 / d_model)
        per_token = jnp.sum(diff * diff, axis=-1, keepdims=True) * (1.0 / d_model)
        loss_ref[...] += 0.5 * jnp.sum(per_token)

    tile = pl.BlockSpec((tm, d_model), lambda m: (m, 0))
    return pl.pallas_call(
        body, name="loss_head", grid=(t_len // tm,),
        in_specs=[tile, tile],
        out_specs=[tile, pl.BlockSpec((8, LANES), lambda m: (0, 0))],
        out_shape=[SDS((t_len, d_model), F32), SDS((8, LANES), F32)],
        compiler_params=_params(1),
    )(x, target)


def _row_tile(rows, cols):
    best = None
    for t in range(16, rows + 1, 16):
        if rows % t == 0 and t * cols * 4 <= (1 << 20):
            best = t
    return best if best is not None else rows


def _adamw(pieces, w, m, v, name):
    n_q = len(pieces)
    n_p, rows, cols = pieces[0].shape
    tr = _row_tile(rows, cols)
    n_i = rows // tr
    c1 = 1.0 - ADAM_B1 ** ADAM_STEP
    c2 = 1.0 - ADAM_B2 ** ADAM_STEP

    def body(*refs):
        p_refs = refs[:n_q]
        w_ref, m_ref, v_ref, g_ref, d_ref, nm_ref, nv_ref = refs[n_q:]
        for q in range(n_q):
            @pl.when(pl.program_id(0) == q)
            def _():
                g = p_refs[q][0].astype(F32)
                for i in range(1, n_p):
                    g = g + p_refs[q][i].astype(F32)
                g_ref[...] = g
                nm = ADAM_B1 * m_ref[...] + (1.0 - ADAM_B1) * g
                nv = ADAM_B2 * v_ref[...] + (1.0 - ADAM_B2) * (g * g)
                nm_ref[...] = nm
                nv_ref[...] = nv
                d_ref[...] = -ADAM_LR * ((nm / c1) / (jnp.sqrt(nv / c2) + ADAM_EPS) + ADAM_WD * w_ref[...])

    def piece_spec(q):
        return pl.BlockSpec((n_p, tr, cols), lambda a, i: (0, jnp.where(a == q, i, 0), 0))

    tile = pl.BlockSpec((tr, cols), lambda a, i: (a * n_i + i, 0))
    return pl.pallas_call(
        body, name=name, grid=(n_q, n_i),
        in_specs=[piece_spec(q) for q in range(n_q)] + [tile, tile, tile],
        out_specs=[tile] * 4,
        out_shape=[SDS((n_q * rows, cols), F32)] * 4,
        compiler_params=_params(2),
    )(*pieces, w, m, v)


def _pack(vecs):
    flat = jnp.concatenate([v.reshape(-1).astype(F32) for v in vecs])
    rows = -(-flat.shape[0] // (8 * LANES)) * 8
    return jnp.pad(flat, (0, rows * LANES - flat.shape[0])).reshape(rows, LANES)


def _unpack(packed, shapes):
    lead = packed.shape[:-2]
    flat = packed.reshape(lead + (-1,))
    out, off = [], 0
    for s in shapes:
        size = 1
        for d in s:
            size *= d
        out.append(flat[..., off:off + size].reshape(lead + tuple(s)))
        off += size
    return out


def kernel(x, c, w_ada, b_ada, norm_g, w_in, q_norm_g, k_norm_g, conv_w, conv_b, w_out, ffn_w1, ffn_w2, loss_target, m_w_ada, m_b_ada, m_norm_g, m_w_in, m_q_norm_g, m_k_norm_g, m_conv_w, m_conv_b, m_w_out, m_ffn_w1, m_ffn_w2, v_w_ada, v_b_ada, v_norm_g, v_w_in, v_q_norm_g, v_k_norm_g, v_conv_w, v_conv_b, v_w_out, v_ffn_w1, v_ffn_w2):
    _, t_len, d_model = x.shape
    n_layer = w_ada.shape[0]
    a_w = d_model // 2
    c_w = d_model - a_w
    n_head = a_w // HEAD_DIM
    ada_cols = w_ada.shape[-1]
    tm = min(TOKEN_TILE, t_len)
    me = _index(_my_place())
    x2 = x.reshape(t_len, d_model)
    target = loss_target.reshape(t_len, d_model)

    small_shapes = [(d_model,), norm_g.shape, conv_w.shape]
    gathered = _allgather_small(_pack([c, norm_g, conv_w]), "gather_small_inputs")
    c_all, ng_parts, cw_parts = _unpack(gathered, small_shapes)
    norm_g_full = jnp.moveaxis(ng_parts, 0, 2).reshape(n_layer, 3, d_model)
    conv_w_full = jnp.moveaxis(cw_parts, 0, 2).reshape(n_layer, 3, c_w)

    b_loc = lax.dynamic_slice_in_dim(b_ada, me * ada_cols, ada_cols, axis=1).reshape(n_layer, 1, ada_cols)
    mod_part, silu_c = _mod_part(c_all, w_ada, b_loc)
    mod_all = _allgather_small(_pack([mod_part]), "gather_mod")
    mod_all = _unpack(mod_all, [mod_part.shape])[0]
    mod_mine = lax.dynamic_index_in_dim(mod_all, me, axis=2, keepdims=False)
    mod = jnp.moveaxis(mod_mine, 0, 1).reshape(n_layer, 3, 3, d_model)

    def prm_of(l, sub):
        rows = jnp.stack([norm_g_full[l, sub], mod[l, sub, 0], mod[l, sub, 1], mod[l, sub, 2]])
        return jnp.pad(rows, ((0, 4), (0, 0)))

    prm = [[prm_of(l, sub) for sub in range(3)] for l in range(n_layer)]
    qkg = [jnp.pad(jnp.stack([jnp.tile(q_norm_g[l], n_head), jnp.tile(k_norm_g[l], n_head)]), ((0, 6), (0, 0)))
           for l in range(n_layer)]
    cprm = [jnp.pad(jnp.concatenate([conv_w_full[l], conv_b[l][None]]), ((0, 4), (0, 0))) for l in range(n_layer)]
    head_of = jnp.arange(a_w) // HEAD_DIM
    bd = (head_of[:, None] == head_of[None, :]).astype(BF16)

    f8 = ffn_w1.shape[-1]
    r8 = ffn_w2.shape[-2]
    c8 = w_in.shape[-1]
    o8 = w_out.shape[-2]
    me_arr = jnp.reshape(me, (1,)).astype(jnp.int32)
    win_b = [_place_own(w_in, (l,), me_arr, f"own_w_in_{l}") for l in range(n_layer)]
    wout_b = [_place_own(w_out, (l,), me_arr, f"own_w_out_{l}") for l in range(n_layer)]
    w1_b = [[_place_own(ffn_w1, (l, s), me_arr, f"own_w1_{l}_{s}") for s in range(2)] for l in range(n_layer)]
    w2_b = [[_place_own(ffn_w2, (l, s), me_arr, f"own_w2_{l}_{s}") for s in range(2)] for l in range(n_layer)]
    w1_first, w2_first = _allgather_big([w1_b[0][0], w2_b[0][0]], "gather_first_weights")
    groups = [[win_b[0], wout_b[0]], [w1_b[0][1], w2_b[0][1]]]
    for l in range(1, n_layer):
        groups += [[w1_b[l][0], w2_b[l][0]], [win_b[l], wout_b[l]], [w1_b[l][1], w2_b[l][1]]]
    flat = [a for grp in groups for a in grp]
    w_sends, w_recvs, w_lands, w_token = _send_start(flat, None, w1_first, "weights_start")

    def gathered(gi, after):
        lo = sum(len(grp) for grp in groups[:gi])
        hi = lo + len(groups[gi])
        return _send_wait(w_lands[lo:hi], w_sends[lo:hi], w_recvs[lo:hi], after, f"weights_wait_{gi}")

    def chunked(w1g):
        return w1g.reshape(2, N_CHUNK, d_model, f8)

    saved = []
    xc = x2
    for l in range(n_layer):
        keep = {"x0": xc}
        if l == 0:
            w1g, w2g = w1_first, w2_first
            prm_first = prm[0][0] + w_token[0, 0]
        else:
            w1g, w2g = gathered(3 * l - 1, xc)
            prm_first = prm[l][0]
        keep["w_ffn0"] = (chunked(w1g), w2g)
        keep["gu0"], keep["h0"], keep["y0"], xc = _ffn_fwd(xc, prm_first, *keep["w_ffn0"], l, 0, tm)
        keep["x1"] = xc
        wing, woutg = gathered(0 if l == 0 else 3 * l, xc)
        keep.update(wing=wing, woutg=woutg)
        proj, qkn, keep["h1"], ycv, *res = _mixer_in(xc, prm[l][1], wing, qkg[l], cprm[l], bd, l, tm)
        qkv = [(qkn[None], proj[None], 2)] + [(res[2 * i], res[2 * i + 1], 0) for i in range(len(DILATIONS) - 1)]
        branches = [_attn_fwd(*t, l) for t in qkv]
        cat, lse, *res = _attn_combine([b[0] for b in branches], [b[1] for b in branches], ycv, tm, l)
        n_r = len(DILATIONS) - 1
        keep["ya"] = [cat[None]] + res[:n_r]
        keep["lse"] = [lse[None]] + res[n_r:]
        xc, keep["ymix"] = _mixer_out(cat, xc, prm[l][1], woutg, l, tm)
        keep.update(proj=proj, qkv=qkv, cat=cat)
        keep["x2"] = xc
        w1g, w2g = gathered(1 if l == 0 else 3 * l + 1, xc)
        keep["w_ffn2"] = (chunked(w1g), w2g)
        keep["gu2"], keep["h2"], keep["y2"], xc = _ffn_fwd(xc, prm[l][2], *keep["w_ffn2"], l, 1, tm)
        saved.append(keep)

    dx, loss_blk = _loss_head(xc, target, tm)
    loss = lax.psum(loss_blk[0, 0], ("x", "y", "c"))

    tk = min(TOKEN_TILE, t_len)
    stats =[[None] * 3 for _ in range(n_layer)]
    conv_stats = [None] * n_layer
    qk_stats = [None] * n_layer

    started = {}
    dummy = jnp.zeros((8, LANES), F32)

    def start_exchange(g, name):
        sends, recvs, thru, token = _send_start(
            [_place_own(g, None, me_arr, f"own_grad_{name}")], [g], dummy, f"grads_start_{name}")
        started[name] = (thru[0], thru[1], sends[0], recvs[0])
        return token

    def wait_exchange(names, after, name):
        ent = [started[k] for k in names]
        done = _send_wait([e[0] for e in ent] + [e[1] for e in ent], [e[2] for e in ent], [e[3] for e in ent],
                          after, name)
        return dict(zip(names, done))

    def ffn_backward(dxo, keep, l, sub, which, prm_dep):
        dx_in, da, s_act, dy, st = _ffn_bwd(dxo, keep[f"y{which}"], keep[f"x{which}"], prm_dep,
                                            keep[f"gu{which}"], *keep[f"w_ffn{which}"], l, sub, tm)
        g_w2 = _matmul_tn(
            s_act, dy,
            pl.BlockSpec((None, tk, f8), lambda p, k: (p, k, 0)),
            pl.BlockSpec((tk, d_model), lambda p, k: (k, 0)),
            SDS((N_CHUNK, f8, d_model), BF16), pl.BlockSpec((None, f8, d_model), lambda p, k: (p, 0, 0)),
            N_CHUNK, t_len, tk, f"grad_w2_{l}_{sub}", dummy)
        token = start_exchange(g_w2.reshape(N_DEV, r8, d_model), f"w2_{l}_{sub}")
        g_w1 = _matmul_tn(
            keep[f"h{which}"], da.reshape(N_DEV, t_len, f8),
            pl.BlockSpec((tk, d_model), lambda p, k: (k, 0)),
            pl.BlockSpec((None, tk, f8), lambda p, k: (p, k, 0)),
            SDS((N_DEV, d_model, f8), BF16), pl.BlockSpec((None, d_model, f8), lambda p, k: (p, 0, 0)),
            N_DEV, t_len, tk, f"grad_w1_{l}_{sub}", token)
        token = start_exchange(g_w1, f"w1_{l}_{sub}")
        return dx_in, st, token[0, 0]

    token = 0.0
    for l in reversed(range(n_layer)):
        keep = saved[l]
        wing, woutg = keep["wing"], keep["woutg"]
        dx, stats[l][2], token = ffn_backward(dx, keep, l, 1, 2, prm[l][2] + token)
        dya, dpc, dy, st_gate, conv_stats[l], *dya_res = _mixer_bwd_out(
            dx, keep["ymix"], prm[l][1] + token, woutg, keep["proj"], cprm[l], l, tm)
        dyas = [dya[None]] + dya_res
        g_wout = _matmul_tn(
            keep["cat"], dy,
            pl.BlockSpec((tk, d_model), lambda p, k: (k, 0)),
            pl.BlockSpec((tk, d_model), lambda p, k: (k, 0)),
            SDS((d_model, d_model), BF16), pl.BlockSpec((d_model, d_model), lambda p, k: (0, 0)),
            1, t_len, tk, f"grad_wout_{l}", dummy)
        token = start_exchange(g_wout.reshape(N_DEV, o8, d_model), f"wout_{l}")[0, 0]
        parts = [_attn_bwd(*keep["qkv"][i], dyas[i], keep["ya"][i], keep["lse"][i], l)
                 for i in range(len(DILATIONS))]
        dx, dproj, st_norm, qk_stats[l] = _mixer_bwd_in(
            parts, keep["proj"], dpc, dx, keep["x1"], prm[l][1] + token, wing, qkg[l], bd, l, tm)
        stats[l][1] = st_norm + st_gate
        g_win = _matmul_tn(
            keep["h1"], dproj,
            pl.BlockSpec((tk, d_model), lambda p, k: (k, 0)),
            pl.BlockSpec((tk, c8), lambda p, k: (k, p)),
            SDS((N_DEV, d_model, c8), BF16), pl.BlockSpec((None, d_model, c8), lambda p, k: (p, 0, 0)),
            N_DEV, t_len, tk, f"grad_win_{l}", dummy)
        token = start_exchange(g_win, f"win_{l}")[0, 0]
        dx, stats[l][0], token = ffn_backward(dx, keep, l, 0, 0, prm[l][0] + token)
    grad_x = dx.reshape(x.shape)

    def update(pieces, w, m, v, name):
        shape = w.shape
        cols = shape[-1]
        outs = _adamw([p.reshape(p.shape[0], -1, cols) for p in pieces], w.reshape(-1, cols),
                      m.reshape(-1, cols), v.reshape(-1, cols), name)
        return [o.reshape(shape) for o in outs]

    last = "w1_0_0"
    lands = wait_exchange([k for k in started if k != last], dx, "grads_wait_early")
    res = {
        "w_in": update([lands[f"win_{l}"] for l in range(n_layer)], w_in, m_w_in, v_w_in, "adamw_w_in"),
        "w_out": update([lands[f"wout_{l}"] for l in range(n_layer)], w_out, m_w_out, v_w_out, "adamw_w_out"),
        "ffn_w2": update([lands[f"w2_{l}_{s}"] for l in range(n_layer) for s in range(2)],
                         ffn_w2, m_ffn_w2, v_ffn_w2, "adamw_ffn_w2"),
    }

    dmod = jnp.stack([jnp.stack([stats[l][sub][r] for sub in range(3) for r in (0, 1, 3)]) for l in range(n_layer)])
    dng = jnp.stack([jnp.stack([stats[l][sub][2] for sub in range(3)]) for l in range(n_layer)])
    dcw = jnp.stack([conv_stats[l][0:3] for l in range(n_layer)])
    dcb = jnp.stack([conv_stats[l][3] for l in range(n_layer)])
    dqk = jnp.stack([qk_stats[l][0:2].reshape(2, n_head, HEAD_DIM).sum(axis=1) for l in range(n_layer)])
    part_shapes = [dmod.shape, dng.shape, dcw.shape, dcb.shape, dqk.shape]
    parts_all = _allgather_small(_pack([dmod, dng, dcw, dcb, dqk]), "gather_small_grads")
    dmod_all = _unpack(parts_all, part_shapes)[0].reshape(N_DEV, n_layer, 9 * d_model)
    summed = _unpack(_sum_devices(parts_all), part_shapes)
    g_b_ada = summed[0].reshape(n_layer, 9 * d_model)
    ng_cols = norm_g.shape[-1]
    g_norm_g = lax.dynamic_slice_in_dim(summed[1], me * ng_cols, ng_cols, axis=2)
    cw_cols = conv_w.shape[-1]
    g_conv_w = lax.dynamic_slice_in_dim(summed[2], me * cw_cols, cw_cols, axis=2)
    g_conv_b = summed[3]
    g_q, g_k = summed[4][:, 0], summed[4][:, 1]
    dmod_loc = lax.dynamic_slice_in_dim(dmod_all, me * ada_cols, ada_cols, axis=2)
    g_w_ada = _w_ada_grad(silu_c.T, jnp.moveaxis(dmod_loc, 0, 1))
    res["w_ada"] = update([g_w_ada[None]], w_ada, m_w_ada, v_w_ada, "adamw_w_ada")

    names = ["b_ada", "norm_g", "q_norm_g", "k_norm_g", "conv_w", "conv_b"]
    g_small = [g_b_ada, g_norm_g, g_q, g_k, g_conv_w, g_conv_b]
    w_small = [b_ada, norm_g, q_norm_g, k_norm_g, conv_w, conv_b]
    m_small = [m_b_ada, m_norm_g, m_q_norm_g, m_k_norm_g, m_conv_w, m_conv_b]
    v_small = [v_b_ada, v_norm_g, v_q_norm_g, v_k_norm_g, v_conv_w, v_conv_b]
    outs = _adamw([_pack(g_small)[None]], _pack(w_small), _pack(m_small), _pack(v_small), "adamw_small")
    shapes = [w.shape for w in w_small]
    unpacked = [_unpack(o, shapes) for o in outs]
    for i, nme in enumerate(names):
        res[nme] = [unpacked[k][i] for k in range(4)]

    follows = (res["w_in"][1][0, :1, :1] + res["w_out"][1][0, :1, :1] + res["ffn_w2"][1][0, 0, :1, :1]
               + res["w_ada"][1][0, :1, :1] + outs[1][:1, :1])
    lands.update(wait_exchange([last], follows, "grads_wait_last"))
    res["ffn_w1"] = update([lands[f"w1_{l}_{s}"] for l in range(n_layer) for s in range(2)],
                           ffn_w1, m_ffn_w1, v_ffn_w1, "adamw_ffn_w1")

    order = ["w_ada", "b_ada", "norm_g", "w_in", "q_norm_g", "k_norm_g", "conv_w", "conv_b", "w_out", "ffn_w1", "ffn_w2"]
    return (loss, grad_x, *[res[n][0] for n in order], *[res[n][1] for n in order],
            *[res[n][2] for n in order], *[res[n][3] for n in order])
```

```python
import jax
import jax.numpy as jnp
from jax import lax
from jax.experimental import pallas as pl
from jax.experimental.pallas import tpu as pltpu

F32 = jnp.float32
BF16 = jnp.bfloat16
SDS = jax.ShapeDtypeStruct

N_DEV = 8
HEAD_DIM = 64
BLOCK = 128
DILATIONS = (1, 4, 16)
N_CHUNK = 4
EPS = 1e-6
NEG = -1e30
SM_SCALE = HEAD_DIM ** -0.5
LANES = 128
MXU_WIDTH = 256
TOKEN_TILE = 512
GRAD_TOKEN_TILE = 2048
VMEM_LIMIT_BYTES = 56 * 1024 * 1024

ADAM_LR = 0.001
ADAM_B1 = 0.9
ADAM_B2 = 0.999
ADAM_EPS = 1e-08
ADAM_WD = 0.01
ADAM_STEP = 10

MESH_ID = pl.DeviceIdType.MESH
ANY = pl.BlockSpec(memory_space=pl.ANY)
HBM_SPEC = pl.BlockSpec(memory_space=pltpu.HBM)
SEM_SPEC = pl.BlockSpec(memory_space=pltpu.SEMAPHORE)
SIDE_EFFECT = pltpu.SideEffectType.DATAFLOW_SIDE_EFFECTING


def _params(n_axes):
    return pltpu.CompilerParams(dimension_semantics=("arbitrary",) * n_axes,
                                vmem_limit_bytes=VMEM_LIMIT_BYTES)


def _dot(a, b):
    return jnp.dot(a, b, preferred_element_type=F32)


def _dot_nt(a, b):
    return lax.dot_general(a, b, (((1,), (1,)), ((), ())), preferred_element_type=F32)


def _dot_tn(a, b):
    return lax.dot_general(a, b, (((0,), (0,)), ((), ())), preferred_element_type=F32)


def _split_bf16(t):
    hi = t.astype(BF16)
    return hi, (t - hi.astype(F32)).astype(BF16)


def _my_place():
    x, y, c = lax.axis_index("x"), lax.axis_index("y"), lax.axis_index("c")
    return x, y, c


def _peer(place, k):
    x, y, c = place
    return ((1 - x) if k & 4 else x, (1 - y) if k & 2 else y, (1 - c) if k & 1 else c)


def _index(place):
    return 4 * place[0] + 2 * place[1] + place[2]


def _allgather_small(v, name):
    rows, cols = v.shape

    def body(x_ref, out_ref, send_sems, recv_sems, local_sem):
        me = _my_place()
        mine = pltpu.make_async_copy(x_ref, out_ref.at[_index(me)], local_sem)
        mine.start()
        sends = []
        for k in range(1, N_DEV):
            cp = pltpu.make_async_remote_copy(
                src_ref=x_ref, dst_ref=out_ref.at[_index(me)], send_sem=send_sems.at[k - 1],
                recv_sem=recv_sems.at[k - 1], device_id=_peer(me, k), device_id_type=MESH_ID)
            cp.start()
            sends.append(cp)
        for k in range(1, N_DEV):
            pltpu.make_async_remote_copy(
                src_ref=x_ref, dst_ref=out_ref.at[_index(_peer(me, k))], send_sem=send_sems.at[k - 1],
                recv_sem=recv_sems.at[k - 1], device_id=_peer(me, k), device_id_type=MESH_ID).wait_recv()
        for cp in sends:
            cp.wait_send()
        mine.wait()

    return pl.pallas_call(
        body, name=name,
        out_shape=SDS((N_DEV, rows, cols), v.dtype),
        in_specs=[pl.BlockSpec(memory_space=pltpu.VMEM)],
        out_specs=pl.BlockSpec(memory_space=pltpu.VMEM),
        scratch_shapes=[pltpu.SemaphoreType.DMA((N_DEV - 1,)), pltpu.SemaphoreType.DMA((N_DEV - 1,)),
                        pltpu.SemaphoreType.DMA],
    )(v)


def _allgather_big(lands, name):
    n = len(lands)

    def body(*refs):
        outs = refs[n:2 * n]
        send_sems, recv_sems = refs[2 * n:]
        x, y, c = _my_place()
        me, sib = (x, y, c), (x, y, 1 - c)
        chips = [(1 - x, y), (x, 1 - y), (1 - x, 1 - y)]

        def copy(i, k, block, to):
            dst = outs[i].at[_index(block)]
            return pltpu.make_async_remote_copy(
                src_ref=dst, dst_ref=dst, send_sem=send_sems.at[7 * i + k],
                recv_sem=recv_sems.at[7 * i + k], device_id=to, device_id_type=MESH_ID)

        first = []
        for i in range(n):
            first.append(copy(i, 0, me, sib))
            for j, chip in enumerate(chips):
                first.append(copy(i, 1 + j, me, (*chip, c)))
        for cp in first:
            cp.start()
        passed = []
        for i in range(n):
            for j, chip in enumerate(chips):
                copy(i, 1 + j, (*chip, c), me).wait_recv()
                fwd = copy(i, 4 + j, (*chip, c), sib)
                fwd.start()
                passed.append(fwd)
        for i in range(n):
            copy(i, 0, sib, me).wait_recv()
            for j, chip in enumerate(chips):
                copy(i, 4 + j, (*chip, 1 - c), me).wait_recv()
        for cp in first + passed:
            cp.wait_send()

    return pl.pallas_call(
        body, name=name,
        out_shape=[SDS(a.shape, a.dtype) for a in lands],
        in_specs=[ANY] * n, out_specs=[ANY] * n,
        input_output_aliases={i: i for i in range(n)},
        scratch_shapes=[pltpu.SemaphoreType.DMA((7 * n,)), pltpu.SemaphoreType.DMA((7 * n,))],
    )(*lands)


def _place_own(src, lead, me_arr, name):
    rows, cols = src.shape[-2:]
    tr = _row_tile(rows, cols)

    def body(me_ref, s_ref, o_ref):
        o_ref[...] = s_ref[...].astype(BF16)

    if lead is None:
        in_spec = pl.BlockSpec((None, tr, cols), lambda i, me_ref: (me_ref[0], i, 0))
    else:
        in_spec = pl.BlockSpec((None,) * len(lead) + (tr, cols), lambda i, me_ref: (*lead, i, 0))
    return pl.pallas_call(
        body, name=name,
        grid_spec=pltpu.PrefetchScalarGridSpec(
            num_scalar_prefetch=1, grid=(rows // tr,), in_specs=[in_spec],
            out_specs=pl.BlockSpec((None, tr, cols), lambda i, me_ref: (me_ref[0], i, 0))),
        out_shape=SDS((N_DEV, rows, cols), BF16),
        compiler_params=_params(1),
    )(me_arr, src)


def _send_start(lands, pieces, after, name):
    n = len(lands)
    arrs = list(lands) + list(pieces or [])
    n_a = len(arrs)

    def body(*refs):
        land_r, piece_r = refs[:n], refs[n:n_a]
        send, recv = refs[n_a + 1:n_a + 1 + n], refs[n_a + 1 + n:n_a + 1 + 2 * n]
        token = refs[-1]
        me = _my_place()
        for i in range(n):
            for k in range(1, N_DEV):
                peer = _peer(me, k)
                src = piece_r[i].at[_index(peer)] if pieces else land_r[i].at[_index(me)]
                pltpu.make_async_remote_copy(
                    src_ref=src, dst_ref=land_r[i].at[_index(me)], send_sem=send[i], recv_sem=recv[i],
                    device_id=peer, device_id_type=MESH_ID).start()
        token[...] = jnp.zeros_like(token)

    outs = pl.pallas_call(
        body, name=name,
        out_shape=[pltpu.SemaphoreType.DMA(())] * (2 * n) + [pltpu.HBM(a.shape, a.dtype) for a in arrs]
        + [SDS((8, LANES), F32)],
        in_specs=[HBM_SPEC] * n_a + [ANY],
        out_specs=[SEM_SPEC] * (2 * n) + [HBM_SPEC] * n_a + [pl.BlockSpec(memory_space=pltpu.VMEM)],
        input_output_aliases={i: 2 * n + i for i in range(n_a)},
        compiler_params=pltpu.CompilerParams(has_side_effects=SIDE_EFFECT),
    )(*[pltpu.with_memory_space_constraint(a, pltpu.HBM) for a in arrs], after)
    return outs[:n], outs[n:2 * n], outs[2 * n:2 * n + n_a], outs[-1]


def _send_wait(thru, sends, recvs, after, name):
    n = len(sends)
    n_a = len(thru)

    def body(*refs):
        land_r = refs[:n]
        send, recv = refs[n_a:n_a + n], refs[n_a + n:n_a + 2 * n]
        me = _my_place()
        for i in range(n):
            seven = land_r[i].at[pl.ds(0, N_DEV - 1)]
            cp = pltpu.make_async_remote_copy(src_ref=seven, dst_ref=seven, send_sem=send[i], recv_sem=recv[i],
                                              device_id=_peer(me, 1), device_id_type=MESH_ID)
            cp.wait_send()
            cp.wait_recv()

    outs = pl.pallas_call(
        body, name=name,
        out_shape=[pltpu.HBM(a.shape, a.dtype) for a in thru],
        in_specs=[HBM_SPEC] * n_a + [SEM_SPEC] * (2 * n) + [ANY],
        out_specs=[HBM_SPEC] * n_a,
        input_output_aliases={i: i for i in range(n_a)},
        compiler_params=pltpu.CompilerParams(has_side_effects=SIDE_EFFECT),
    )(*thru, *sends, *recvs, after)
    return outs[:n]


def _ada_fwd(x, prm_ref):
    rstd = lax.rsqrt(jnp.mean(x * x, axis=-1, keepdims=True) + EPS)
    return (x * rstd * prm_ref[0:1, :]) * (1.0 + prm_ref[2:3, :]) + prm_ref[1:2, :]


def _ada_bwd(dh, x, prm_ref):
    ng, ops = prm_ref[0:1, :], 1.0 + prm_ref[2:3, :]
    rstd = lax.rsqrt(jnp.mean(x * x, axis=-1, keepdims=True) + EPS)
    xhat = x * rstd
    dxhat = dh * (ops * ng)
    dx = rstd * (dxhat - xhat * jnp.mean(dxhat * xhat, axis=-1, keepdims=True))
    dshift = jnp.sum(dh, axis=0, keepdims=True)
    dscale = jnp.sum(dh * (xhat * ng), axis=0, keepdims=True)
    dng = jnp.sum(dh * (xhat * ops), axis=0, keepdims=True)
    return dx, dshift, dscale, dng


def _head_mean(t, bd):
    hi, lo = _split_bf16(t)
    g = bd.shape[0]
    parts = [_dot(hi[:, i:i + g], bd) + _dot(lo[:, i:i + g], bd) for i in range(0, t.shape[1], g)]
    return jnp.concatenate(parts, axis=1) * (1.0 / HEAD_DIM)


def _shift_rows(z, before, tm):
    row = lax.broadcasted_iota(jnp.int32, z.shape, 0)
    z1 = jnp.where(row == 0, before[7:8, :], pltpu.roll(z, 1, 0))
    z2 = jnp.where(row == 0, before[6:7, :], jnp.where(row == 1, before[7:8, :], pltpu.roll(z, 2, 0)))
    return z1, z2


def _mod_part(c_all, w_ada, b_loc):
    n_layer, d_model, cols = w_ada.shape

    def body(c_ref, w_ref, b_ref, out_ref, sc_ref):
        cv = c_ref[...]
        sc = cv * jax.nn.sigmoid(cv)
        sc_ref[...] = sc
        a_hi, a_lo = _split_bf16(sc)
        w_hi, w_lo = _split_bf16(w_ref[...])
        out_ref[...] = _dot(a_hi, w_hi) + _dot(a_hi, w_lo) + _dot(a_lo, w_hi) + b_ref[...]

    return pl.pallas_call(
        body, name="mod_part", grid=(n_layer,),
        in_specs=[pl.BlockSpec((N_DEV, d_model), lambda l: (0, 0)),
                  pl.BlockSpec((None, d_model, cols), lambda l: (l, 0, 0)),
                  pl.BlockSpec((None, 1, cols), lambda l: (l, 0, 0))],
        out_specs=[pl.BlockSpec((None, N_DEV, cols), lambda l: (l, 0, 0)),
                   pl.BlockSpec((N_DEV, d_model), lambda l: (0, 0))],
        out_shape=[SDS((n_layer, N_DEV, cols), F32), SDS((N_DEV, d_model), F32)],
        compiler_params=_params(1),
    )(c_all, w_ada, b_loc)


def _w_ada_grad(sc_t, dmod_loc):
    d_model = sc_t.shape[0]
    n_layer, _, cols = dmod_loc.shape

    def body(s_ref, d_ref, out_ref):
        acc = s_ref[:, 0:1] * d_ref[0:1, :]
        for b in range(1, N_DEV):
            acc = acc + s_ref[:, b:b + 1] * d_ref[b:b + 1, :]
        out_ref[...] = acc

    return pl.pallas_call(
        body, name="w_ada_grad", grid=(n_layer,),
        in_specs=[pl.BlockSpec((d_model, N_DEV), lambda l: (0, 0)),
                  pl.BlockSpec((None, N_DEV, cols), lambda l: (l, 0, 0))],
        out_specs=pl.BlockSpec((None, d_model, cols), lambda l: (l, 0, 0)),
        out_shape=SDS((n_layer, d_model, cols), F32),
        compiler_params=_params(1),
    )(sc_t, dmod_loc)


def _sum_devices(g):
    _, rows, cols = g.shape

    def body(g_ref, out_ref):
        acc = g_ref[0]
        for i in range(1, N_DEV):
            acc = acc + g_ref[i]
        out_ref[...] = acc

    return pl.pallas_call(body, name="sum_devices", out_shape=SDS((rows, cols), F32))(g)


def _ffn_fwd(x, prm, w1r, w2g, l, sub, tm):
    t_len, d_model = x.shape
    f8, r8 = w1r.shape[-1], w2g.shape[-2]

    def body(x_ref, prm_ref, w1_ref, w2_ref, gu_ref, h_ref, y_ref, xo_ref, h_sc, acc):
        j = pl.program_id(1)

        @pl.when(j == 0)
        def _():
            h = _ada_fwd(x_ref[...], prm_ref).astype(BF16)
            h_sc[...] = h
            h_ref[...] = h
            acc[...] = jnp.zeros_like(acc)

        hb = h_sc[...]
        g = _dot(hb, w1_ref[0])
        u = _dot(hb, w1_ref[1])
        gu_ref[0] = g.astype(BF16)
        gu_ref[1] = u.astype(BF16)
        s = (g * jax.nn.sigmoid(g) * u).astype(BF16)
        acc[...] += _dot(s, w2_ref[...].reshape(2 * r8, d_model))

        @pl.when(j == N_CHUNK - 1)
        def _():
            yv = acc[...]
            y_ref[...] = yv.astype(BF16)
            xo_ref[...] = x_ref[...] + (0.5 * prm_ref[3:4, :]) * yv

    tile = pl.BlockSpec((tm, d_model), lambda m, j: (m, 0))
    return pl.pallas_call(
        body, name=f"ffn_fwd_{l}_{sub}", grid=(t_len // tm, N_CHUNK),
        in_specs=[tile, pl.BlockSpec((8, d_model), lambda m, j: (0, 0)),
                  pl.BlockSpec((2, None, d_model, f8), lambda m, j: (0, j, 0, 0)),
                  pl.BlockSpec((2, r8, d_model), lambda m, j: (j, 0, 0))],
        out_specs=[pl.BlockSpec((2, None, tm, f8), lambda m, j: (0, j, m, 0)), tile, tile, tile],
        out_shape=[SDS((2, N_CHUNK, t_len, f8), BF16), SDS((t_len, d_model), BF16),
                   SDS((t_len, d_model), BF16), SDS((t_len, d_model), F32)],
        scratch_shapes=[pltpu.VMEM((tm, d_model), BF16), pltpu.VMEM((tm, d_model), F32)],
        compiler_params=_params(2),
    )(x, prm, w1r, w2g)


def _ffn_bwd(dxo, y, x_in, prm, gu, w1r, w2g, l, sub, tm):
    t_len, d_model = x_in.shape
    f8, r8 = w1r.shape[-1], w2g.shape[-2]
    n_m = t_len // tm

    def body(dxo_ref, y_ref, x_ref, prm_ref, gu_ref, w1_ref, w2_ref,
             dx_ref, da_ref, s_ref, dy_ref, st_ref, dy_sc, acc):
        m, j = pl.program_id(0), pl.program_id(1)

        @pl.when((m == 0) & (j == 0))
        def _():
            st_ref[...] = jnp.zeros_like(st_ref)

        @pl.when(j == 0)
        def _():
            dxo_v = dxo_ref[...]
            dy = ((0.5 * prm_ref[3:4, :]) * dxo_v).astype(BF16)
            dy_sc[...] = dy
            dy_ref[...] = dy
            st_ref[3:4, :] += jnp.sum(0.5 * dxo_v * y_ref[...].astype(F32), axis=0, keepdims=True)
            acc[...] = jnp.zeros_like(acc)

        ds = _dot_nt(dy_sc[...], w2_ref[...].reshape(2 * r8, d_model))
        g = gu_ref[0].astype(F32)
        u = gu_ref[1].astype(F32)
        sig = jax.nn.sigmoid(g)
        sil = g * sig
        s_ref[...] = (sil * u).astype(BF16)
        dg = (ds * u * (sig * (1.0 + g * (1.0 - sig)))).astype(BF16)
        du = (ds * sil).astype(BF16)
        da_ref[0] = dg
        da_ref[1] = du
        acc[...] += _dot_nt(dg, w1_ref[0]) + _dot_nt(du, w1_ref[1])

        @pl.when(j == N_CHUNK - 1)
        def _():
            dx, dshift, dscale, dng = _ada_bwd(acc[...], x_ref[...], prm_ref)
            dx_ref[...] = dxo_ref[...] + dx
            st_ref[0:1, :] += dshift
            st_ref[1:2, :] += dscale
            st_ref[2:3, :] += dng

    tile = pl.BlockSpec((tm, d_model), lambda m, j: (m, 0))
    fixed = pl.BlockSpec((8, d_model), lambda m, j: (0, 0))
    return pl.pallas_call(
        body, name=f"ffn_bwd_{l}_{sub}", grid=(n_m, N_CHUNK),
        in_specs=[tile, tile, tile, fixed,
                  pl.BlockSpec((2, None, tm, f8), lambda m, j: (0, j, m, 0)),
                  pl.BlockSpec((2, None, d_model, f8), lambda m, j: (0, j, 0, 0)),
                  pl.BlockSpec((2, r8, d_model), lambda m, j: (j, 0, 0))],
        out_specs=[tile, pl.BlockSpec((2, None, tm, f8), lambda m, j: (0, j, m, 0)),
                   pl.BlockSpec((None, tm, f8), lambda m, j: (j, m, 0)), tile, fixed],
        out_shape=[SDS((t_len, d_model), F32), SDS((2, N_CHUNK, t_len, f8), BF16),
                   SDS((N_CHUNK, t_len, f8), BF16), SDS((t_len, d_model), BF16), SDS((8, d_model), F32)],
        scratch_shapes=[pltpu.VMEM((tm, d_model), BF16), pltpu.VMEM((tm, d_model), F32)],
        compiler_params=_params(2),
    )(dxo, y, x_in, prm, gu, w1r, w2g)


def _matmul_tn(a, b, a_spec, b_spec, out_shape, out_spec, n_piece, t_len, tk, name, after):
    def body(a_ref, b_ref, after_ref, o_ref, acc):
        k = pl.program_id(1)

        @pl.when(k == 0)
        def _():
            acc[...] = jnp.zeros_like(acc)

        acc[...] += _dot_tn(a_ref[...], b_ref[...])

        @pl.when(k == pl.num_programs(1) - 1)
        def _():
            o_ref[...] = acc[...].astype(o_ref.dtype)

    acc_shape = tuple(d for d in out_spec.block_shape if d is not None)
    return pl.pallas_call(
        body, name=name, grid=(n_piece, t_len // tk),
        in_specs=[a_spec, b_spec, ANY], out_specs=out_spec, out_shape=out_shape,
        scratch_shapes=[pltpu.VMEM(acc_shape, F32)],
        compiler_params=_params(2),
    )(a, b, after)


def _groups(tm, width):
    return pltpu.VMEM((width // LANES, tm, LANES), F32)


def _to_groups(val, dst_ref, first=0):
    for g in range(val.shape[1] // LANES):
        dst_ref[first + g] = val[:, g * LANES:(g + 1) * LANES]


def _from_groups(src_ref):
    return jnp.concatenate([src_ref[g] for g in range(src_ref.shape[0])], axis=1)


def _split_residues(src_ref, out_ref, dil, tm):
    for g in range(src_ref.shape[0]):
        for r in range(dil):
            out_ref[r, :, g * LANES:(g + 1) * LANES] = (
                src_ref.at[g][pl.ds(r, tm // dil, stride=dil), :].astype(out_ref.dtype))


def _merge_residues(in_ref, dst_ref, dil, tm, add=False):
    for g in range(dst_ref.shape[0]):
        for r in range(dil):
            rows = pl.ds(r, tm // dil, stride=dil)
            val = in_ref[r, :, g * LANES:(g + 1) * LANES].astype(F32)
            dst = dst_ref.at[g]
            dst[rows, :] = (dst[rows, :] + val) if add else val


def _res_spec(dil, tm, width, index=lambda m: m):
    return pl.BlockSpec((dil, tm // dil, width), lambda m: (0, index(m), 0))


def _mixer_in(x, prm, wing, qkg, cprm, bd, l, tm):
    t_len, d_model = x.shape
    a_w = d_model // 2
    c_w = d_model - a_w
    c8 = wing.shape[-1]
    p_c = N_DEV * c8
    dils = DILATIONS[1:]

    def body(x_ref, prm_ref, w_ref, qkg_ref, cprm_ref, bd_ref, proj_ref, qkn_ref, h_ref, ycv_ref, *rest):
        res_refs, (pf, qsc, vsc, carry) = rest[:2 * len(dils)], rest[2 * len(dils):]

        @pl.when(pl.program_id(0) == 0)
        def _():
            carry[...] = jnp.zeros_like(carry)

        h = _ada_fwd(x_ref[...], prm_ref).astype(BF16)
        h_ref[...] = h
        for j in range(N_DEV):
            pf[:, j * c8:(j + 1) * c8] = _dot(h, w_ref[j])
        proj_ref[...] = pf[...].astype(BF16)
        bdm = bd_ref[...]
        for i in range(2):
            t = pf[:, i * a_w:(i + 1) * a_w]
            rstd = lax.rsqrt(_head_mean(t * t, bdm) + EPS)
            tn = t * rstd * qkg_ref[i:i + 1, :]
            qkn_ref[:, i * a_w:(i + 1) * a_w] = tn.astype(BF16)
            _to_groups(tn, qsc, i * (a_w // LANES))
        _to_groups(pf[:, 2 * a_w:3 * a_w], vsc)
        for i, dil in enumerate(dils):
            _split_residues(qsc, res_refs[2 * i], dil, tm)
            _split_residues(vsc, res_refs[2 * i + 1], dil, tm)
        gb = pf[:, 3 * a_w:3 * a_w + c_w]
        z = pf[:, 3 * a_w + c_w:3 * a_w + 2 * c_w] * pf[:, 3 * a_w + 2 * c_w:3 * a_w + 3 * c_w]
        z1, z2 = _shift_rows(z, carry[...], tm)
        cv = cprm_ref[0:1, :] * z2 + cprm_ref[1:2, :] * z1 + cprm_ref[2:3, :] * z + cprm_ref[3:4, :]
        ycv_ref[...] = (gb * cv).astype(BF16)
        carry[...] = z[tm - 8:tm, :]

    res_specs, res_shapes = [], []
    for dil in dils:
        res_specs += [_res_spec(dil, tm, 2 * a_w), _res_spec(dil, tm, a_w)]
        res_shapes += [SDS((dil, t_len // dil, 2 * a_w), BF16), SDS((dil, t_len // dil, a_w), BF16)]
    return pl.pallas_call(
        body, name=f"mixer_in_{l}", grid=(t_len // tm,),
        in_specs=[pl.BlockSpec((tm, d_model), lambda m: (m, 0)),
                  pl.BlockSpec((8, d_model), lambda m: (0, 0)),
                  pl.BlockSpec((N_DEV, d_model, c8), lambda m: (0, 0, 0)),
                  pl.BlockSpec((8, a_w), lambda m: (0, 0)),
                  pl.BlockSpec((8, c_w), lambda m: (0, 0)),
                  pl.BlockSpec(bd.shape, lambda m: (0, 0))],
        out_specs=[pl.BlockSpec((tm, p_c), lambda m: (m, 0)),
                   pl.BlockSpec((tm, 2 * a_w), lambda m: (m, 0)),
                   pl.BlockSpec((tm, d_model), lambda m: (m, 0)),
                   pl.BlockSpec((tm, c_w), lambda m: (m, 0))] + res_specs,
        out_shape=[SDS((t_len, p_c), BF16), SDS((t_len, 2 * a_w), BF16),
                   SDS((t_len, d_model), BF16), SDS((t_len, c_w), BF16)] + res_shapes,
        scratch_shapes=[pltpu.VMEM((tm, p_c), F32), _groups(tm, 2 * a_w), _groups(tm, a_w),
                        pltpu.VMEM((8, c_w), F32)],
        compiler_params=_params(1),
    )(x, prm, wing, qkg, cprm, bd)


def _band_mask(first_block):
    qi = lax.broadcasted_iota(jnp.int32, (BLOCK, 2 * BLOCK), 0)
    kj = lax.broadcasted_iota(jnp.int32, (BLOCK, 2 * BLOCK), 1)
    lowest = jnp.where(first_block, BLOCK, 0)
    return (kj >= qi) & (kj <= qi + BLOCK) & (kj >= lowest)


def _attn_fwd(qk, v, v_col, l):
    dil, seq, a2 = qk.shape
    a_w = a2 // 2
    n_b = seq // BLOCK

    def body(q_ref, kc_ref, kp_ref, vc_ref, vp_ref, o_ref, lse_ref):
        valid = _band_mask(pl.program_id(1) == 0)
        low = lax.broadcasted_iota(jnp.int32, (BLOCK, LANES), 1) < HEAD_DIM
        n_pair = a_w // LANES
        slices = [slice(p * LANES, (p + 1) * LANES) for p in range(n_pair)]
        scores = []
        for sl in slices:
            qp = q_ref[:, sl]
            k2 = jnp.concatenate([kp_ref[:, sl], kc_ref[:, sl]], axis=0)
            scores += [_dot_nt(jnp.where(half, qp, jnp.zeros_like(qp)), k2) for half in (low, ~low)]
        probs, lses = [], []
        for s in scores:
            s = jnp.where(valid, s * SM_SCALE, NEG)
            mx = jnp.max(s, axis=-1, keepdims=True)
            e = jnp.exp(s - mx)
            den = jnp.sum(e, axis=-1, keepdims=True)
            probs.append((e * (1.0 / den)).astype(BF16))
            lses.append(mx + jnp.log(den))
        for p, sl in enumerate(slices):
            v2 = jnp.concatenate([vp_ref[:, sl], vc_ref[:, sl]], axis=0)
            o_ref[:, sl] = jnp.where(low, _dot(probs[2 * p], v2), _dot(probs[2 * p + 1], v2)).astype(BF16)
            lse_ref[:, sl] = jnp.where(low, lses[2 * p], lses[2 * p + 1])

    def blk(f):
        return pl.BlockSpec((None, BLOCK, a_w), f)

    return pl.pallas_call(
        body, name=f"attn_fwd_{l}_d{dil}", grid=(dil, n_b),
        in_specs=[blk(lambda r, b: (r, b, 0)),
                  blk(lambda r, b: (r, b, 1)),
                  blk(lambda r, b: (r, jnp.maximum(b - 1, 0), 1)),
                  blk(lambda r, b: (r, b, v_col)),
                  blk(lambda r, b: (r, jnp.maximum(b - 1, 0), v_col))],
        out_specs=[blk(lambda r, b: (r, b, 0)), blk(lambda r, b: (r, b, 0))],
        out_shape=[SDS((dil, seq, a_w), BF16), SDS((dil, seq, a_w), F32)],
        compiler_params=_params(2),
    )(qk, qk, qk, v, v)


def _attn_combine(outs, lses, ycv, tm, l):
    t_len, c_w = ycv.shape
    a_w = outs[0].shape[-1]
    dils = DILATIONS[1:]
    n_r = len(dils)

    def body(o1, l1, *rest):
        o_res, l_res, y_ref = rest[:n_r], rest[n_r:2 * n_r], rest[2 * n_r]
        cat_ref, lse_ref = rest[2 * n_r + 1], rest[2 * n_r + 2]
        ya_res, lse_res = rest[2 * n_r + 3:3 * n_r + 3], rest[3 * n_r + 3:4 * n_r + 3]
        scr = rest[4 * n_r + 3:]
        so, sl, sya, slse = scr[:n_r], scr[n_r:2 * n_r], scr[2 * n_r], scr[2 * n_r + 1]
        for i, dil in enumerate(dils):
            _merge_residues(o_res[i], so[i], dil, tm)
            _merge_residues(l_res[i], sl[i], dil, tm)
        ls = [l1[...]] + [_from_groups(s) for s in sl]
        ov = [o1[...].astype(F32)] + [_from_groups(s) for s in so]
        mx = jnp.maximum(jnp.maximum(ls[0], ls[1]), ls[2])
        es = [jnp.exp(t - mx) for t in ls]
        den = es[0] + es[1] + es[2]
        ya = (es[0] * ov[0] + es[1] * ov[1] + es[2] * ov[2]) * (1.0 / den)
        lse = mx + jnp.log(den)
        _to_groups(ya, sya)
        _to_groups(lse, slse)
        cat_ref[:, :a_w] = ya.astype(BF16)
        cat_ref[:, a_w:] = y_ref[...]
        lse_ref[...] = lse
        for i, dil in enumerate(dils):
            _split_residues(sya, ya_res[i], dil, tm)
            _split_residues(slse, lse_res[i], dil, tm)

    ta = pl.BlockSpec((tm, a_w), lambda m: (m, 0))
    res = [_res_spec(dil, tm, a_w) for dil in dils]
    return pl.pallas_call(
        body, name=f"attn_combine_{l}", grid=(t_len // tm,),
        in_specs=[ta, ta] + res + res + [pl.BlockSpec((tm, c_w), lambda m: (m, 0))],
        out_specs=[pl.BlockSpec((tm, a_w + c_w), lambda m: (m, 0)), ta] + res + res,
        out_shape=[SDS((t_len, a_w + c_w), BF16), SDS((t_len, a_w), F32)]
        + [SDS((dil, t_len // dil, a_w), BF16) for dil in dils]
        + [SDS((dil, t_len // dil, a_w), F32) for dil in dils],
        scratch_shapes=[_groups(tm, a_w)] * (2 * n_r + 2),
        compiler_params=_params(1),
    )(outs[0].reshape(t_len, a_w), lses[0].reshape(t_len, a_w), *outs[1:], *lses[1:], ycv)


def _mixer_out(cat, x, prm, woutg, l, tm):
    t_len, d_model = x.shape
    r8 = woutg.shape[-2]

    def body(cat_ref, x_ref, prm_ref, w_ref, xo_ref, y_ref):
        yv = _dot(cat_ref[...], w_ref[...].reshape(N_DEV * r8, d_model))
        y_ref[...] = yv.astype(BF16)
        xo_ref[...] = x_ref[...] + prm_ref[3:4, :] * yv

    tile = pl.BlockSpec((tm, d_model), lambda m: (m, 0))
    return pl.pallas_call(
        body, name=f"mixer_out_{l}", grid=(t_len // tm,),
        in_specs=[tile, tile, pl.BlockSpec((8, d_model), lambda m: (0, 0)),
                  pl.BlockSpec((N_DEV, r8, d_model), lambda m: (0, 0, 0))],
        out_specs=[tile, tile],
        out_shape=[SDS((t_len, d_model), F32), SDS((t_len, d_model), BF16)],
        compiler_params=_params(1),
    )(cat, x, prm, woutg)


def _mixer_bwd_out(dxo, ymix, prm, woutg, proj, cprm, l, tm):
    t_len, d_model = dxo.shape
    a_w = d_model // 2
    c_w = d_model - a_w
    r8 = woutg.shape[-2]
    n_m = t_len // tm
    hb = tm // 8

    dils = DILATIONS[1:]

    def body(dxo_ref, ym_ref, prm_ref, w_ref, pc_ref, halo_ref, cprm_ref,
             dya_ref, dpc_ref, dy_ref, st_ref, cs_ref, *rest):
        dya_res, (carry, dsc) = rest[:len(dils)], rest[len(dils):]
        step = pl.program_id(0)
        tile_i = n_m - 1 - step

        @pl.when(step == 0)
        def _():
            st_ref[...] = jnp.zeros_like(st_ref)
            cs_ref[...] = jnp.zeros_like(cs_ref)
            carry[...] = jnp.zeros_like(carry)

        dxo_v = dxo_ref[...]
        dy = (prm_ref[3:4, :] * dxo_v).astype(BF16)
        dy_ref[...] = dy
        st_ref[3:4, :] += jnp.sum(dxo_v * ym_ref[...].astype(F32), axis=0, keepdims=True)
        dcat = _dot_nt(dy, w_ref[...].reshape(N_DEV * r8, d_model))
        _to_groups(dcat[:, :a_w], dsc)
        dya_ref[...] = dcat[:, :a_w].astype(BF16)
        for i, dil in enumerate(dils):
            _split_residues(dsc, dya_res[i], dil, tm)
        dyc = dcat[:, a_w:]
        gb = pc_ref[:, :c_w].astype(F32)
        gc = pc_ref[:, c_w:2 * c_w].astype(F32)
        u = pc_ref[:, 2 * c_w:].astype(F32)
        z = gc * u
        before = halo_ref[:, c_w:2 * c_w].astype(F32) * halo_ref[:, 2 * c_w:].astype(F32)
        before = jnp.where(tile_i > 0, before, jnp.zeros_like(before))
        z1, z2 = _shift_rows(z, before, tm)
        w0, w1, w2 = cprm_ref[0:1, :], cprm_ref[1:2, :], cprm_ref[2:3, :]
        cv = w0 * z2 + w1 * z1 + w2 * z + cprm_ref[3:4, :]
        dcv = dyc * gb
        cs_ref[0:1, :] += jnp.sum(dcv * z2, axis=0, keepdims=True)
        cs_ref[1:2, :] += jnp.sum(dcv * z1, axis=0, keepdims=True)
        cs_ref[2:3, :] += jnp.sum(dcv * z, axis=0, keepdims=True)
        cs_ref[3:4, :] += jnp.sum(dcv, axis=0, keepdims=True)
        row = lax.broadcasted_iota(jnp.int32, dcv.shape, 0)
        after = carry[...]
        d1 = jnp.where(row == tm - 1, after[0:1, :], pltpu.roll(dcv, tm - 1, 0))
        d2 = jnp.where(row == tm - 2, after[0:1, :],
                       jnp.where(row == tm - 1, after[1:2, :], pltpu.roll(dcv, tm - 2, 0)))
        dz = w2 * dcv + w1 * d1 + w0 * d2
        dpc_ref[:, :c_w] = (dyc * cv).astype(BF16)
        dpc_ref[:, c_w:2 * c_w] = (dz * u).astype(BF16)
        dpc_ref[:, 2 * c_w:] = (dz * gc).astype(BF16)
        carry[...] = dcv[0:8, :]

    def rev(width, col=0):
        return pl.BlockSpec((tm, width), lambda s: (n_m - 1 - s, col))

    fixed_d = pl.BlockSpec((8, d_model), lambda s: (0, 0))
    fixed_c = pl.BlockSpec((8, c_w), lambda s: (0, 0))
    return pl.pallas_call(
        body, name=f"mixer_bwd_out_{l}", grid=(n_m,),
        in_specs=[rev(d_model), rev(d_model), fixed_d,
                  pl.BlockSpec((N_DEV, r8, d_model), lambda s: (0, 0, 0)),
                  rev(3 * c_w, 1),
                  pl.BlockSpec((8, 3 * c_w), lambda s: (jnp.maximum((n_m - 1 - s) * hb - 1, 0), 1)),
                  fixed_c],
        out_specs=[rev(a_w), rev(3 * c_w), rev(d_model), fixed_d, fixed_c]
        + [_res_spec(dil, tm, a_w, lambda s: n_m - 1 - s) for dil in dils],
        out_shape=[SDS((t_len, a_w), BF16), SDS((t_len, 3 * c_w), BF16), SDS((t_len, d_model), BF16),
                   SDS((8, d_model), F32), SDS((8, c_w), F32)]
        + [SDS((dil, t_len // dil, a_w), BF16) for dil in dils],
        scratch_shapes=[pltpu.VMEM((8, c_w), F32), _groups(tm, a_w)],
        compiler_params=_params(1),
    )(dxo, ymix, prm, woutg, proj, proj, cprm)


def _attn_bwd(qk, v, v_col, dya, ya, lse, l):
    dil, seq, a2 = qk.shape
    a_w = a2 // 2
    n_b = seq // BLOCK

    def body(q_ref, kc_ref, kp_ref, vc_ref, vp_ref, do_ref, ya_ref, lse_ref,
             dq_ref, dk_ref, dv_ref, ck, cvv):
        j = pl.program_id(1)

        @pl.when((pl.program_id(0) == 0) & (j == 0))
        def _():
            ck[...] = jnp.zeros_like(ck)
            cvv[...] = jnp.zeros_like(cvv)

        @pl.when(j < n_b)
        def _():
            valid = _band_mask(j == 0)
            low = lax.broadcasted_iota(jnp.int32, (BLOCK, LANES), 1) < HEAD_DIM
            slices = [slice(p * LANES, (p + 1) * LANES) for p in range(a_w // LANES)]
            heads, scores, dps = [], [], []
            for sl in slices:
                qp = q_ref[:, sl]
                k2 = jnp.concatenate([kp_ref[:, sl], kc_ref[:, sl]], axis=0)
                v2 = jnp.concatenate([vp_ref[:, sl], vc_ref[:, sl]], axis=0)
                dop = do_ref[:, sl]
                prod = dop.astype(F32) * ya_ref[:, sl].astype(F32)
                lsep = lse_ref[:, sl]
                for hh, half in enumerate((low, ~low)):
                    qh = jnp.where(half, qp, jnp.zeros_like(qp))
                    doh = jnp.where(half, dop, jnp.zeros_like(dop))
                    dsum = jnp.sum(jnp.where(half, prod, 0.0), axis=-1, keepdims=True)
                    heads.append((qh, doh, dsum, lsep[:, hh * HEAD_DIM:hh * HEAD_DIM + 1]))
                    scores.append(_dot_nt(qh, k2))
                    dps.append(_dot_nt(doh, v2))
            dss, prs = [], []
            for (qh, doh, dsum, lse_h), s, dp in zip(heads, scores, dps):
                pr = jnp.exp(jnp.where(valid, s * SM_SCALE, NEG) - lse_h)
                dss.append((pr * (dp - dsum) * SM_SCALE).astype(BF16))
                prs.append(pr.astype(BF16))
            for p, sl in enumerate(slices):
                k2 = jnp.concatenate([kp_ref[:, sl], kc_ref[:, sl]], axis=0)
                h0, h1 = heads[2 * p], heads[2 * p + 1]
                dq_ref[:, sl] = jnp.where(low, _dot(dss[2 * p], k2), _dot(dss[2 * p + 1], k2)).astype(BF16)
                dk2 = _dot_tn(dss[2 * p], h0[0]) + _dot_tn(dss[2 * p + 1], h1[0])
                dv2 = _dot_tn(prs[2 * p], h0[1]) + _dot_tn(prs[2 * p + 1], h1[1])
                dk_ref[:, sl] = (ck[:, sl] + dk2[:BLOCK]).astype(BF16)
                dv_ref[:, sl] = (cvv[:, sl] + dv2[:BLOCK]).astype(BF16)
                ck[:, sl] = dk2[BLOCK:]
                cvv[:, sl] = dv2[BLOCK:]

        @pl.when(j == n_b)
        def _():
            dk_ref[...] = ck[...].astype(BF16)
            dv_ref[...] = cvv[...].astype(BF16)

    def blk(f):
        return pl.BlockSpec((None, BLOCK, a_w), f)

    def cur(j):
        return jnp.minimum(j, n_b - 1)

    def prev(j):
        return jnp.maximum(jnp.minimum(j, n_b - 1) - 1, 0)

    def late(j):
        return jnp.maximum(j - 1, 0)

    return pl.pallas_call(
        body, name=f"attn_bwd_{l}_d{dil}", grid=(dil, n_b + 1),
        in_specs=[blk(lambda r, j: (r, cur(j), 0)),
                  blk(lambda r, j: (r, cur(j), 1)),
                  blk(lambda r, j: (r, prev(j), 1)),
                  blk(lambda r, j: (r, cur(j), v_col)),
                  blk(lambda r, j: (r, prev(j), v_col)),
                  blk(lambda r, j: (r, cur(j), 0)),
                  blk(lambda r, j: (r, cur(j), 0)),
                  blk(lambda r, j: (r, cur(j), 0))],
        out_specs=[blk(lambda r, j: (r, cur(j), 0)), blk(lambda r, j: (r, late(j), 0)),
                   blk(lambda r, j: (r, late(j), 0))],
        out_shape=[SDS((dil, seq, a_w), BF16)] * 3,
        scratch_shapes=[pltpu.VMEM((BLOCK, a_w), F32), pltpu.VMEM((BLOCK, a_w), F32)],
        compiler_params=_params(2),
    )(qk, qk, qk, v, v, dya, ya, lse)


def _mixer_bwd_in(parts, proj, dpc, dxo, x_in, prm, wing, qkg, bd, l, tm):
    t_len, d_model = x_in.shape
    a_w = d_model // 2
    c_w = d_model - a_w
    c8 = wing.shape[-1]
    p_c = N_DEV * c8

    dils = DILATIONS[1:]
    n_in = 3 * len(DILATIONS)

    def body(*refs):
        nat, res = refs[:3], refs[3:n_in]
        (pa_ref, dpc_ref, dxo_ref, x_ref, prm_ref, w_ref, qkg_ref, bd_ref,
         dx_ref, dproj_ref, st_ref, qs_ref) = refs[n_in:n_in + 12]
        sums = refs[n_in + 12:]

        @pl.when(pl.program_id(0) == 0)
        def _():
            st_ref[...] = jnp.zeros_like(st_ref)
            qs_ref[...] = jnp.zeros_like(qs_ref)

        for t in range(3):
            _to_groups(nat[t][...].astype(F32), sums[t])
            for i, dil in enumerate(dils):
                _merge_residues(res[3 * i + t], sums[t], dil, tm, add=True)
        bdm = bd_ref[...]
        grads = [_from_groups(sums[0]), _from_groups(sums[1])]
        for i in range(2):
            t = pa_ref[:, i * a_w:(i + 1) * a_w].astype(F32)
            rstd = lax.rsqrt(_head_mean(t * t, bdm) + EPS)
            that = t * rstd
            qs_ref[i:i + 1, :] += jnp.sum(grads[i] * that, axis=0, keepdims=True)
            dhat = grads[i] * qkg_ref[i:i + 1, :]
            dt = rstd * (dhat - that * _head_mean(dhat * that, bdm))
            dproj_ref[:, i * a_w:(i + 1) * a_w] = dt.astype(BF16)
        dproj_ref[:, 2 * a_w:3 * a_w] = _from_groups(sums[2]).astype(BF16)
        dproj_ref[:, 3 * a_w:] = dpc_ref[...]
        dh = _dot_nt(dproj_ref[:, 0:c8], w_ref[0])
        for j in range(1, N_DEV):
            dh = dh + _dot_nt(dproj_ref[:, j * c8:(j + 1) * c8], w_ref[j])
        dx, dshift, dscale, dng = _ada_bwd(dh, x_ref[...], prm_ref)
        dx_ref[...] = dxo_ref[...] + dx
        st_ref[0:1, :] += dshift
        st_ref[1:2, :] += dscale
        st_ref[2:3, :] += dng

    ta = pl.BlockSpec((tm, a_w), lambda m: (m, 0))
    tile = pl.BlockSpec((tm, d_model), lambda m: (m, 0))
    return pl.pallas_call(
        body, name=f"mixer_bwd_in_{l}", grid=(t_len // tm,),
        in_specs=[ta] * 3 + [_res_spec(dil, tm, a_w) for dil in dils for _ in range(3)]
                 + [pl.BlockSpec((tm, 3 * a_w), lambda m: (m, 0)),
                             pl.BlockSpec((tm, 3 * c_w), lambda m: (m, 0)),
                             tile, tile, pl.BlockSpec((8, d_model), lambda m: (0, 0)),
                             pl.BlockSpec((N_DEV, d_model, c8), lambda m: (0, 0, 0)),
                             pl.BlockSpec((8, a_w), lambda m: (0, 0)),
                             pl.BlockSpec(bd.shape, lambda m: (0, 0))],
        out_specs=[tile, pl.BlockSpec((tm, p_c), lambda m: (m, 0)),
                   pl.BlockSpec((8, d_model), lambda m: (0, 0)), pl.BlockSpec((8, a_w), lambda m: (0, 0))],
        out_shape=[SDS((t_len, d_model), F32), SDS((t_len, p_c), BF16), SDS((8, d_model), F32), SDS((8, a_w), F32)],
        scratch_shapes=[_groups(tm, a_w)] * 3,
        compiler_params=_params(1),
    )(*[t.reshape(t_len, a_w) for t in parts[0]], *[t for p in parts[1:] for t in p],
      proj, dpc, dxo, x_in, prm, wing, qkg, bd)


def _loss_head(x, target, tm):
    t_len, d_model = x.shape

    def body(x_ref, t_ref, dx_ref, loss_ref):
        @pl.when(pl.program_id(0) == 0)
        def _():
            loss_ref[...] = jnp.zeros_like(loss_ref)

        diff = x_ref[...] - t_ref[...]
        dx_ref[...] = diff * (1.0 / d_model)
        per_token = jnp.sum(diff * diff, axis=-1, keepdims=True) * (1.0 / d_model)
        loss_ref[...] += 0.5 * jnp.sum(per_token)

    tile = pl.BlockSpec((tm, d_model), lambda m: (m, 0))
    return pl.pallas_call(
        body, name="loss_head", grid=(t_len // tm,),
        in_specs=[tile, tile],
        out_specs=[tile, pl.BlockSpec((8, LANES), lambda m: (0, 0))],
        out_shape=[SDS((t_len, d_model), F32), SDS((8, LANES), F32)],
        compiler_params=_params(1),
    )(x, target)


def _row_tile(rows, cols):
    best = None
    for t in range(16, rows + 1, 16):
        if rows % t == 0 and t * cols * 4 <= (1 << 20):
            best = t
    return best if best is not None else rows


def _adamw(pieces, w, m, v, name):
    n_q = len(pieces)
    n_p, rows, cols = pieces[0].shape
    tr = _row_tile(rows, cols)
    n_i = rows // tr
    c1 = 1.0 - ADAM_B1 ** ADAM_STEP
    c2 = 1.0 - ADAM_B2 ** ADAM_STEP

    def body(*refs):
        p_refs = refs[:n_q]
        w_ref, m_ref, v_ref, g_ref, d_ref, nm_ref, nv_ref = refs[n_q:]
        for q in range(n_q):
            @pl.when(pl.program_id(0) == q)
            def _():
                g = p_refs[q][0].astype(F32)
                for i in range(1, n_p):
                    g = g + p_refs[q][i].astype(F32)
                g_ref[...] = g
                nm = ADAM_B1 * m_ref[...] + (1.0 - ADAM_B1) * g
                nv = ADAM_B2 * v_ref[...] + (1.0 - ADAM_B2) * (g * g)
                nm_ref[...] = nm
                nv_ref[...] = nv
                d_ref[...] = -ADAM_LR * ((nm / c1) / (jnp.sqrt(nv / c2) + ADAM_EPS) + ADAM_WD * w_ref[...])

    def piece_spec(q):
        return pl.BlockSpec((n_p, tr, cols), lambda a, i: (0, jnp.where(a == q, i, 0), 0))

    tile = pl.BlockSpec((tr, cols), lambda a, i: (a * n_i + i, 0))
    return pl.pallas_call(
        body, name=name, grid=(n_q, n_i),
        in_specs=[piece_spec(q) for q in range(n_q)] + [tile, tile, tile],
        out_specs=[tile] * 4,
        out_shape=[SDS((n_q * rows, cols), F32)] * 4,
        compiler_params=_params(2),
    )(*pieces, w, m, v)


def _pack(vecs):
    flat = jnp.concatenate([v.reshape(-1).astype(F32) for v in vecs])
    rows = -(-flat.shape[0] // (8 * LANES)) * 8
    return jnp.pad(flat, (0, rows * LANES - flat.shape[0])).reshape(rows, LANES)


def _unpack(packed, shapes):
    lead = packed.shape[:-2]
    flat = packed.reshape(lead + (-1,))
    out, off = [], 0
    for s in shapes:
        size = 1
        for d in s:
            size *= d
        out.append(flat[..., off:off + size].reshape(lead + tuple(s)))
        off += size
    return out


def kernel(x, c, w_ada, b_ada, norm_g, w_in, q_norm_g, k_norm_g, conv_w, conv_b, w_out, ffn_w1, ffn_w2, loss_target, m_w_ada, m_b_ada, m_norm_g, m_w_in, m_q_norm_g, m_k_norm_g, m_conv_w, m_conv_b, m_w_out, m_ffn_w1, m_ffn_w2, v_w_ada, v_b_ada, v_norm_g, v_w_in, v_q_norm_g, v_k_norm_g, v_conv_w, v_conv_b, v_w_out, v_ffn_w1, v_ffn_w2):
    _, t_len, d_model = x.shape
    n_layer = w_ada.shape[0]
    a_w = d_model // 2
    c_w = d_model - a_w
    n_head = a_w // HEAD_DIM
    ada_cols = w_ada.shape[-1]
    tm = min(TOKEN_TILE, t_len)
    me = _index(_my_place())
    x2 = x.reshape(t_len, d_model)
    target = loss_target.reshape(t_len, d_model)

    small_shapes = [(d_model,), norm_g.shape, conv_w.shape]
    gathered = _allgather_small(_pack([c, norm_g, conv_w]), "gather_small_inputs")
    c_all, ng_parts, cw_parts = _unpack(gathered, small_shapes)
    norm_g_full = jnp.moveaxis(ng_parts, 0, 2).reshape(n_layer, 3, d_model)
    conv_w_full = jnp.moveaxis(cw_parts, 0, 2).reshape(n_layer, 3, c_w)

    b_loc = lax.dynamic_slice_in_dim(b_ada, me * ada_cols, ada_cols, axis=1).reshape(n_layer, 1, ada_cols)
    mod_part, silu_c = _mod_part(c_all, w_ada, b_loc)
    mod_all = _allgather_small(_pack([mod_part]), "gather_mod")
    mod_all = _unpack(mod_all, [mod_part.shape])[0]
    mod_mine = lax.dynamic_index_in_dim(mod_all, me, axis=2, keepdims=False)
    mod = jnp.moveaxis(mod_mine, 0, 1).reshape(n_layer, 3, 3, d_model)

    def prm_of(l, sub):
        rows = jnp.stack([norm_g_full[l, sub], mod[l, sub, 0], mod[l, sub, 1], mod[l, sub, 2]])
        return jnp.pad(rows, ((0, 4), (0, 0)))

    prm = [[prm_of(l, sub) for sub in range(3)] for l in range(n_layer)]
    qkg = [jnp.pad(jnp.stack([jnp.tile(q_norm_g[l], n_head), jnp.tile(k_norm_g[l], n_head)]), ((0, 6), (0, 0)))
           for l in range(n_layer)]
    cprm = [jnp.pad(jnp.concatenate([conv_w_full[l], conv_b[l][None]]), ((0, 4), (0, 0))) for l in range(n_layer)]
    head_of = jnp.arange(min(MXU_WIDTH, a_w)) // HEAD_DIM
    bd = (head_of[:, None] == head_of[None, :]).astype(BF16)

    f8 = ffn_w1.shape[-1]
    r8 = ffn_w2.shape[-2]
    c8 = w_in.shape[-1]
    o8 = w_out.shape[-2]
    me_arr = jnp.reshape(me, (1,)).astype(jnp.int32)
    win_b = [_place_own(w_in, (l,), me_arr, f"own_w_in_{l}") for l in range(n_layer)]
    wout_b = [_place_own(w_out, (l,), me_arr, f"own_w_out_{l}") for l in range(n_layer)]
    w1_b = [[_place_own(ffn_w1, (l, s), me_arr, f"own_w1_{l}_{s}") for s in range(2)] for l in range(n_layer)]
    w2_b = [[_place_own(ffn_w2, (l, s), me_arr, f"own_w2_{l}_{s}") for s in range(2)] for l in range(n_layer)]
    w1_first, w2_first = _allgather_big([w1_b[0][0], w2_b[0][0]], "gather_first_weights")
    groups = [[win_b[0], wout_b[0]], [w1_b[0][1], w2_b[0][1]]]
    for l in range(1, n_layer):
        groups += [[w1_b[l][0], w2_b[l][0]], [win_b[l], wout_b[l]], [w1_b[l][1], w2_b[l][1]]]
    flat = [a for grp in groups for a in grp]
    w_sends, w_recvs, w_lands, w_token = _send_start(flat, None, w1_first, "weights_start")

    def gathered(gi, after):
        lo = sum(len(grp) for grp in groups[:gi])
        hi = lo + len(groups[gi])
        return _send_wait(w_lands[lo:hi], w_sends[lo:hi], w_recvs[lo:hi], after, f"weights_wait_{gi}")

    def chunked(w1g):
        return w1g.reshape(2, N_CHUNK, d_model, f8)

    saved = []
    xc = x2
    for l in range(n_layer):
        keep = {"x0": xc}
        if l == 0:
            w1g, w2g = w1_first, w2_first
            prm_first = prm[0][0] + w_token[0, 0]
        else:
            w1g, w2g = gathered(3 * l - 1, xc)
            prm_first = prm[l][0]
        keep["w_ffn0"] = (chunked(w1g), w2g)
        keep["gu0"], keep["h0"], keep["y0"], xc = _ffn_fwd(xc, prm_first, *keep["w_ffn0"], l, 0, tm)
        keep["x1"] = xc
        wing, woutg = gathered(0 if l == 0 else 3 * l, xc)
        keep.update(wing=wing, woutg=woutg)
        proj, qkn, keep["h1"], ycv, *res = _mixer_in(xc, prm[l][1], wing, qkg[l], cprm[l], bd, l, tm)
        qkv = [(qkn[None], proj[None], 2)] + [(res[2 * i], res[2 * i + 1], 0) for i in range(len(DILATIONS) - 1)]
        branches = [_attn_fwd(*t, l) for t in qkv]
        cat, lse, *res = _attn_combine([b[0] for b in branches], [b[1] for b in branches], ycv, tm, l)
        n_r = len(DILATIONS) - 1
        keep["ya"] = [cat[None]] + res[:n_r]
        keep["lse"] = [lse[None]] + res[n_r:]
        xc, keep["ymix"] = _mixer_out(cat, xc, prm[l][1], woutg, l, tm)
        keep.update(proj=proj, qkv=qkv, cat=cat)
        keep["x2"] = xc
        w1g, w2g = gathered(1 if l == 0 else 3 * l + 1, xc)
        keep["w_ffn2"] = (chunked(w1g), w2g)
        keep["gu2"], keep["h2"], keep["y2"], xc = _ffn_fwd(xc, prm[l][2], *keep["w_ffn2"], l, 1, tm)
        saved.append(keep)

    dx, loss_blk = _loss_head(xc, target, tm)
    loss = lax.psum(loss_blk[0, 0], ("x", "y", "c"))

    tk = min(GRAD_TOKEN_TILE, t_len)
    stats =[[None] * 3 for _ in range(n_layer)]
    conv_stats = [None] * n_layer
    qk_stats = [None] * n_layer

    started = {}
    dummy = jnp.zeros((8, LANES), F32)

    def start_exchange(g, name):
        sends, recvs, thru, token = _send_start(
            [_place_own(g, None, me_arr, f"own_grad_{name}")], [g], dummy, f"grads_start_{name}")
        started[name] = (thru[0], thru[1], sends[0], recvs[0])
        return token

    def wait_exchange(names, after, name):
        ent = [started[k] for k in names]
        done = _send_wait([e[0] for e in ent] + [e[1] for e in ent], [e[2] for e in ent], [e[3] for e in ent],
                          after, name)
        return dict(zip(names, done))

    def ffn_backward(dxo, keep, l, sub, which, prm_dep):
        dx_in, da, s_act, dy, st = _ffn_bwd(dxo, keep[f"y{which}"], keep[f"x{which}"], prm_dep,
                                            keep[f"gu{which}"], *keep[f"w_ffn{which}"], l, sub, tm)
        g_w2 = _matmul_tn(
            s_act, dy,
            pl.BlockSpec((None, tk, f8), lambda p, k: (p, k, 0)),
            pl.BlockSpec((tk, d_model), lambda p, k: (k, 0)),
            SDS((N_CHUNK, f8, d_model), BF16), pl.BlockSpec((None, f8, d_model), lambda p, k: (p, 0, 0)),
            N_CHUNK, t_len, tk, f"grad_w2_{l}_{sub}", dummy)
        token = start_exchange(g_w2.reshape(N_DEV, r8, d_model), f"w2_{l}_{sub}")
        g_w1 = _matmul_tn(
            keep[f"h{which}"], da.reshape(N_DEV, t_len, f8),
            pl.BlockSpec((tk, d_model), lambda p, k: (k, 0)),
            pl.BlockSpec((None, tk, f8), lambda p, k: (p, k, 0)),
            SDS((N_DEV, d_model, f8), BF16), pl.BlockSpec((None, d_model, f8), lambda p, k: (p, 0, 0)),
            N_DEV, t_len, tk, f"grad_w1_{l}_{sub}", token)
        token = start_exchange(g_w1, f"w1_{l}_{sub}")
        return dx_in, st, token[0, 0]

    token = 0.0
    for l in reversed(range(n_layer)):
        keep = saved[l]
        wing, woutg = keep["wing"], keep["woutg"]
        dx, stats[l][2], token = ffn_backward(dx, keep, l, 1, 2, prm[l][2] + token)
        dya, dpc, dy, st_gate, conv_stats[l], *dya_res = _mixer_bwd_out(
            dx, keep["ymix"], prm[l][1] + token, woutg, keep["proj"], cprm[l], l, tm)
        dyas = [dya[None]] + dya_res
        g_wout = _matmul_tn(
            keep["cat"], dy,
            pl.BlockSpec((tk, d_model), lambda p, k: (k, 0)),
            pl.BlockSpec((tk, d_model), lambda p, k: (k, 0)),
            SDS((d_model, d_model), BF16), pl.BlockSpec((d_model, d_model), lambda p, k: (0, 0)),
            1, t_len, tk, f"grad_wout_{l}", dummy)
        token = start_exchange(g_wout.reshape(N_DEV, o8, d_model), f"wout_{l}")[0, 0]
        parts = [_attn_bwd(*keep["qkv"][i], dyas[i], keep["ya"][i], keep["lse"][i], l)
                 for i in range(len(DILATIONS))]
        dx, dproj, st_norm, qk_stats[l] = _mixer_bwd_in(
            parts, keep["proj"], dpc, dx, keep["x1"], prm[l][1] + token, wing, qkg[l], bd, l, tm)
        stats[l][1] = st_norm + st_gate
        g_win = _matmul_tn(
            keep["h1"], dproj,
            pl.BlockSpec((tk, d_model), lambda p, k: (k, 0)),
            pl.BlockSpec((tk, c8), lambda p, k: (k, p)),
            SDS((N_DEV, d_model, c8), BF16), pl.BlockSpec((None, d_model, c8), lambda p, k: (p, 0, 0)),
            N_DEV, t_len, tk, f"grad_win_{l}", dummy)
        token = start_exchange(g_win, f"win_{l}")[0, 0]
        dx, stats[l][0], token = ffn_backward(dx, keep, l, 0, 0, prm[l][0] + token)
    grad_x = dx.reshape(x.shape)

    def update(pieces, w, m, v, name):
        shape = w.shape
        cols = shape[-1]
        outs = _adamw([p.reshape(p.shape[0], -1, cols) for p in pieces], w.reshape(-1, cols),
                      m.reshape(-1, cols), v.reshape(-1, cols), name)
        return [o.reshape(shape) for o in outs]

    last = "w1_0_0"
    lands = wait_exchange([k for k in started if k != last], dx, "grads_wait_early")
    res = {
        "w_in": update([lands[f"win_{l}"] for l in range(n_layer)], w_in, m_w_in, v_w_in, "adamw_w_in"),
        "w_out": update([lands[f"wout_{l}"] for l in range(n_layer)], w_out, m_w_out, v_w_out, "adamw_w_out"),
        "ffn_w2": update([lands[f"w2_{l}_{s}"] for l in range(n_layer) for s in range(2)],
                         ffn_w2, m_ffn_w2, v_ffn_w2, "adamw_ffn_w2"),
    }

    dmod = jnp.stack([jnp.stack([stats[l][sub][r] for sub in range(3) for r in (0, 1, 3)]) for l in range(n_layer)])
    dng = jnp.stack([jnp.stack([stats[l][sub][2] for sub in range(3)]) for l in range(n_layer)])
    dcw = jnp.stack([conv_stats[l][0:3] for l in range(n_layer)])
    dcb = jnp.stack([conv_stats[l][3] for l in range(n_layer)])
    dqk = jnp.stack([qk_stats[l][0:2].reshape(2, n_head, HEAD_DIM).sum(axis=1) for l in range(n_layer)])
    part_shapes = [dmod.shape, dng.shape, dcw.shape, dcb.shape, dqk.shape]
    parts_all = _allgather_small(_pack([dmod, dng, dcw, dcb, dqk]), "gather_small_grads")
    dmod_all = _unpack(parts_all, part_shapes)[0].reshape(N_DEV, n_layer, 9 * d_model)
    summed = _unpack(_sum_devices(parts_all), part_shapes)
    g_b_ada = summed[0].reshape(n_layer, 9 * d_model)
    ng_cols = norm_g.shape[-1]
    g_norm_g = lax.dynamic_slice_in_dim(summed[1], me * ng_cols, ng_cols, axis=2)
    cw_cols = conv_w.shape[-1]
    g_conv_w = lax.dynamic_slice_in_dim(summed[2], me * cw_cols, cw_cols, axis=2)
    g_conv_b = summed[3]
    g_q, g_k = summed[4][:, 0], summed[4][:, 1]
    dmod_loc = lax.dynamic_slice_in_dim(dmod_all, me * ada_cols, ada_cols, axis=2)
    g_w_ada = _w_ada_grad(silu_c.T, jnp.moveaxis(dmod_loc, 0, 1))
    res["w_ada"] = update([g_w_ada[None]], w_ada, m_w_ada, v_w_ada, "adamw_w_ada")

    names = ["b_ada", "norm_g", "q_norm_g", "k_norm_g", "conv_w", "conv_b"]
    g_small = [g_b_ada, g_norm_g, g_q, g_k, g_conv_w, g_conv_b]
    w_small = [b_ada, norm_g, q_norm_g, k_norm_g, conv_w, conv_b]
    m_small = [m_b_ada, m_norm_g, m_q_norm_g, m_k_norm_g, m_conv_w, m_conv_b]
    v_small = [v_b_ada, v_norm_g, v_q_norm_g, v_k_norm_g, v_conv_w, v_conv_b]
    outs = _adamw([_pack(g_small)[None]], _pack(w_small), _pack(m_small), _pack(v_small), "adamw_small")
    shapes = [w.shape for w in w_small]
    unpacked = [_unpack(o, shapes) for o in outs]
    for i, nme in enumerate(names):
        res[nme] = [unpacked[k][i] for k in range(4)]

    follows = (res["w_in"][1][0, :1, :1] + res["w_out"][1][0, :1, :1] + res["ffn_w2"][1][0, 0, :1, :1]
               + res["w_ada"][1][0, :1, :1] + outs[1][:1, :1])
    lands.update(wait_exchange([last], follows, "grads_wait_last"))
    res["ffn_w1"] = update([lands[f"w1_{l}_{s}"] for l in range(n_layer) for s in range(2)],
                           ffn_w1, m_ffn_w1, v_ffn_w1, "adamw_ffn_w1")

    order = ["w_ada", "b_ada", "norm_g", "w_in", "q_norm_g", "k_norm_g", "conv_w", "conv_b", "w_out", "ffn_w1", "ffn_w2"]
    return (loss, grad_x, *[res[n][0] for n in order], *[res[n][1] for n in order],
            *[res[n][2] for n in order], *[res[n][3] for n in order])
```

```python
import jax
import jax.numpy as jnp
from jax import lax
from jax.experimental import pallas as pl
from jax.experimental.pallas import tpu as pltpu

F32 = jnp.float32
BF16 = jnp.bfloat16
SDS = jax.ShapeDtypeStruct

N_DEV = 8
HEAD_DIM = 64
BLOCK = 128
DILATIONS = (1, 4, 16)
N_CHUNK = 4
EPS = 1e-6
NEG = -1e30
SM_SCALE = HEAD_DIM ** -0.5
LANES = 128
MXU_WIDTH = 256
TOKEN_TILE = 512
GRAD_TOKEN_TILE = 2048
VMEM_LIMIT_BYTES = 56 * 1024 * 1024

ADAM_LR = 0.001
ADAM_B1 = 0.9
ADAM_B2 = 0.999
ADAM_EPS = 1e-08
ADAM_WD = 0.01
ADAM_STEP = 10

MESH_ID = pl.DeviceIdType.MESH
ANY = pl.BlockSpec(memory_space=pl.ANY)
HBM_SPEC = pl.BlockSpec(memory_space=pltpu.HBM)
SEM_SPEC = pl.BlockSpec(memory_space=pltpu.SEMAPHORE)
SIDE_EFFECT = pltpu.SideEffectType.DATAFLOW_SIDE_EFFECTING


def _params(n_axes):
    return pltpu.CompilerParams(dimension_semantics=("arbitrary",) * n_axes,
                                vmem_limit_bytes=VMEM_LIMIT_BYTES)


def _dot(a, b):
    return jnp.dot(a, b, preferred_element_type=F32)


def _dot_nt(a, b):
    return lax.dot_general(a, b, (((1,), (1,)), ((), ())), preferred_element_type=F32)


def _dot_tn(a, b):
    return lax.dot_general(a, b, (((0,), (0,)), ((), ())), preferred_element_type=F32)


def _row_halves(n):
    return (slice(0, n // 2), slice(n // 2, n))


def _split_bf16(t):
    hi = t.astype(BF16)
    return hi, (t - hi.astype(F32)).astype(BF16)


def _my_place():
    x, y, c = lax.axis_index("x"), lax.axis_index("y"), lax.axis_index("c")
    return x, y, c


def _peer(place, k):
    x, y, c = place
    return ((1 - x) if k & 4 else x, (1 - y) if k & 2 else y, (1 - c) if k & 1 else c)


def _index(place):
    return 4 * place[0] + 2 * place[1] + place[2]


def _allgather_small(v, name):
    rows, cols = v.shape

    def body(x_ref, out_ref, send_sems, recv_sems, local_sem):
        me = _my_place()
        mine = pltpu.make_async_copy(x_ref, out_ref.at[_index(me)], local_sem)
        mine.start()
        sends = []
        for k in range(1, N_DEV):
            cp = pltpu.make_async_remote_copy(
                src_ref=x_ref, dst_ref=out_ref.at[_index(me)], send_sem=send_sems.at[k - 1],
                recv_sem=recv_sems.at[k - 1], device_id=_peer(me, k), device_id_type=MESH_ID)
            cp.start()
            sends.append(cp)
        for k in range(1, N_DEV):
            pltpu.make_async_remote_copy(
                src_ref=x_ref, dst_ref=out_ref.at[_index(_peer(me, k))], send_sem=send_sems.at[k - 1],
                recv_sem=recv_sems.at[k - 1], device_id=_peer(me, k), device_id_type=MESH_ID).wait_recv()
        for cp in sends:
            cp.wait_send()
        mine.wait()

    return pl.pallas_call(
        body, name=name,
        out_shape=SDS((N_DEV, rows, cols), v.dtype),
        in_specs=[pl.BlockSpec(memory_space=pltpu.VMEM)],
        out_specs=pl.BlockSpec(memory_space=pltpu.VMEM),
        scratch_shapes=[pltpu.SemaphoreType.DMA((N_DEV - 1,)), pltpu.SemaphoreType.DMA((N_DEV - 1,)),
                        pltpu.SemaphoreType.DMA],
    )(v)


def _allgather_big(lands, name):
    n = len(lands)

    def body(*refs):
        outs = refs[n:2 * n]
        send_sems, recv_sems = refs[2 * n:]
        x, y, c = _my_place()
        me, sib = (x, y, c), (x, y, 1 - c)
        chips = [(1 - x, y), (x, 1 - y), (1 - x, 1 - y)]

        def copy(i, k, block, to):
            dst = outs[i].at[_index(block)]
            return pltpu.make_async_remote_copy(
                src_ref=dst, dst_ref=dst, send_sem=send_sems.at[7 * i + k],
                recv_sem=recv_sems.at[7 * i + k], device_id=to, device_id_type=MESH_ID)

        first = []
        for i in range(n):
            first.append(copy(i, 0, me, sib))
            for j, chip in enumerate(chips):
                first.append(copy(i, 1 + j, me, (*chip, c)))
        for cp in first:
            cp.start()
        passed = []
        for i in range(n):
            for j, chip in enumerate(chips):
                copy(i, 1 + j, (*chip, c), me).wait_recv()
                fwd = copy(i, 4 + j, (*chip, c), sib)
                fwd.start()
                passed.append(fwd)
        for i in range(n):
            copy(i, 0, sib, me).wait_recv()
            for j, chip in enumerate(chips):
                copy(i, 4 + j, (*chip, 1 - c), me).wait_recv()
        for cp in first + passed:
            cp.wait_send()

    return pl.pallas_call(
        body, name=name,
        out_shape=[SDS(a.shape, a.dtype) for a in lands],
        in_specs=[ANY] * n, out_specs=[ANY] * n,
        input_output_aliases={i: i for i in range(n)},
        scratch_shapes=[pltpu.SemaphoreType.DMA((7 * n,)), pltpu.SemaphoreType.DMA((7 * n,))],
    )(*lands)


def _place_own(src, lead, me_arr, name):
    rows, cols = src.shape[-2:]
    tr = _row_tile(rows, cols)

    def body(me_ref, s_ref, o_ref):
        o_ref[...] = s_ref[...].astype(BF16)

    if lead is None:
        in_spec = pl.BlockSpec((None, tr, cols), lambda i, me_ref: (me_ref[0], i, 0))
    else:
        in_spec = pl.BlockSpec((None,) * len(lead) + (tr, cols), lambda i, me_ref: (*lead, i, 0))
    return pl.pallas_call(
        body, name=name,
        grid_spec=pltpu.PrefetchScalarGridSpec(
            num_scalar_prefetch=1, grid=(rows // tr,), in_specs=[in_spec],
            out_specs=pl.BlockSpec((None, tr, cols), lambda i, me_ref: (me_ref[0], i, 0))),
        out_shape=SDS((N_DEV, rows, cols), BF16),
        compiler_params=_params(1),
    )(me_arr, src)


def _send_start(lands, pieces, after, name):
    n = len(lands)
    arrs = list(lands) + list(pieces or [])
    n_a = len(arrs)

    def body(*refs):
        land_r, piece_r = refs[:n], refs[n:n_a]
        send, recv = refs[n_a + 1:n_a + 1 + n], refs[n_a + 1 + n:n_a + 1 + 2 * n]
        token = refs[-1]
        me = _my_place()
        for i in range(n):
            for k in range(1, N_DEV):
                peer = _peer(me, k)
                src = piece_r[i].at[_index(peer)] if pieces else land_r[i].at[_index(me)]
                pltpu.make_async_remote_copy(
                    src_ref=src, dst_ref=land_r[i].at[_index(me)], send_sem=send[i], recv_sem=recv[i],
                    device_id=peer, device_id_type=MESH_ID).start()
        token[...] = jnp.zeros_like(token)

    outs = pl.pallas_call(
        body, name=name,
        out_shape=[pltpu.SemaphoreType.DMA(())] * (2 * n) + [pltpu.HBM(a.shape, a.dtype) for a in arrs]
        + [SDS((8, LANES), F32)],
        in_specs=[HBM_SPEC] * n_a + [ANY],
        out_specs=[SEM_SPEC] * (2 * n) + [HBM_SPEC] * n_a + [pl.BlockSpec(memory_space=pltpu.VMEM)],
        input_output_aliases={i: 2 * n + i for i in range(n_a)},
        compiler_params=pltpu.CompilerParams(has_side_effects=SIDE_EFFECT),
    )(*[pltpu.with_memory_space_constraint(a, pltpu.HBM) for a in arrs], after)
    return outs[:n], outs[n:2 * n], outs[2 * n:2 * n + n_a], outs[-1]


def _send_wait(thru, sends, recvs, after, name):
    n = len(sends)
    n_a = len(thru)

    def body(*refs):
        land_r = refs[:n]
        send, recv = refs[n_a:n_a + n], refs[n_a + n:n_a + 2 * n]
        me = _my_place()
        for i in range(n):
            seven = land_r[i].at[pl.ds(0, N_DEV - 1)]
            cp = pltpu.make_async_remote_copy(src_ref=seven, dst_ref=seven, send_sem=send[i], recv_sem=recv[i],
                                              device_id=_peer(me, 1), device_id_type=MESH_ID)
            cp.wait_send()
            cp.wait_recv()

    outs = pl.pallas_call(
        body, name=name,
        out_shape=[pltpu.HBM(a.shape, a.dtype) for a in thru],
        in_specs=[HBM_SPEC] * n_a + [SEM_SPEC] * (2 * n) + [ANY],
        out_specs=[HBM_SPEC] * n_a,
        input_output_aliases={i: i for i in range(n_a)},
        compiler_params=pltpu.CompilerParams(has_side_effects=SIDE_EFFECT),
    )(*thru, *sends, *recvs, after)
    return outs[:n]


def _ada_fwd(x, prm_ref):
    rstd = lax.rsqrt(jnp.mean(x * x, axis=-1, keepdims=True) + EPS)
    return (x * rstd * prm_ref[0:1, :]) * (1.0 + prm_ref[2:3, :]) + prm_ref[1:2, :]


def _ada_bwd(dh, x, prm_ref):
    ng, ops = prm_ref[0:1, :], 1.0 + prm_ref[2:3, :]
    rstd = lax.rsqrt(jnp.mean(x * x, axis=-1, keepdims=True) + EPS)
    xhat = x * rstd
    dxhat = dh * (ops * ng)
    dx = rstd * (dxhat - xhat * jnp.mean(dxhat * xhat, axis=-1, keepdims=True))
    dshift = jnp.sum(dh, axis=0, keepdims=True)
    dscale = jnp.sum(dh * (xhat * ng), axis=0, keepdims=True)
    dng = jnp.sum(dh * (xhat * ops), axis=0, keepdims=True)
    return dx, dshift, dscale, dng


def _head_mean(t, bd):
    hi, lo = _split_bf16(t)
    g = bd.shape[0]
    parts = [_dot(hi[:, i:i + g], bd) + _dot(lo[:, i:i + g], bd) for i in range(0, t.shape[1], g)]
    return jnp.concatenate(parts, axis=1) * (1.0 / HEAD_DIM)


def _shift_rows(z, before, tm):
    row = lax.broadcasted_iota(jnp.int32, z.shape, 0)
    z1 = jnp.where(row == 0, before[7:8, :], pltpu.roll(z, 1, 0))
    z2 = jnp.where(row == 0, before[6:7, :], jnp.where(row == 1, before[7:8, :], pltpu.roll(z, 2, 0)))
    return z1, z2


def _mod_part(c_all, w_ada, b_loc):
    n_layer, d_model, cols = w_ada.shape

    def body(c_ref, w_ref, b_ref, out_ref, sc_ref):
        cv = c_ref[...]
        sc = cv * jax.nn.sigmoid(cv)
        sc_ref[...] = sc
        a_hi, a_lo = _split_bf16(sc)
        w_hi, w_lo = _split_bf16(w_ref[...])
        out_ref[...] = _dot(a_hi, w_hi) + _dot(a_hi, w_lo) + _dot(a_lo, w_hi) + b_ref[...]

    return pl.pallas_call(
        body, name="mod_part", grid=(n_layer,),
        in_specs=[pl.BlockSpec((N_DEV, d_model), lambda l: (0, 0)),
                  pl.BlockSpec((None, d_model, cols), lambda l: (l, 0, 0)),
                  pl.BlockSpec((None, 1, cols), lambda l: (l, 0, 0))],
        out_specs=[pl.BlockSpec((None, N_DEV, cols), lambda l: (l, 0, 0)),
                   pl.BlockSpec((N_DEV, d_model), lambda l: (0, 0))],
        out_shape=[SDS((n_layer, N_DEV, cols), F32), SDS((N_DEV, d_model), F32)],
        compiler_params=_params(1),
    )(c_all, w_ada, b_loc)


def _w_ada_grad(sc_t, dmod_loc):
    d_model = sc_t.shape[0]
    n_layer, _, cols = dmod_loc.shape

    def body(s_ref, d_ref, out_ref):
        acc = s_ref[:, 0:1] * d_ref[0:1, :]
        for b in range(1, N_DEV):
            acc = acc + s_ref[:, b:b + 1] * d_ref[b:b + 1, :]
        out_ref[...] = acc

    return pl.pallas_call(
        body, name="w_ada_grad", grid=(n_layer,),
        in_specs=[pl.BlockSpec((d_model, N_DEV), lambda l: (0, 0)),
                  pl.BlockSpec((None, N_DEV, cols), lambda l: (l, 0, 0))],
        out_specs=pl.BlockSpec((None, d_model, cols), lambda l: (l, 0, 0)),
        out_shape=SDS((n_layer, d_model, cols), F32),
        compiler_params=_params(1),
    )(sc_t, dmod_loc)


def _sum_devices(g):
    _, rows, cols = g.shape

    def body(g_ref, out_ref):
        acc = g_ref[0]
        for i in range(1, N_DEV):
            acc = acc + g_ref[i]
        out_ref[...] = acc

    return pl.pallas_call(body, name="sum_devices", out_shape=SDS((rows, cols), F32))(g)


def _ffn_fwd(x, prm, w1r, w2g, l, sub, tm):
    t_len, d_model = x.shape
    f8, r8 = w1r.shape[-1], w2g.shape[-2]

    def body(x_ref, prm_ref, w1_ref, w2_ref, act_ref, h_ref, y_ref, xo_ref, h_sc, acc):
        j = pl.program_id(1)

        @pl.when(j == 0)
        def _():
            h = _ada_fwd(x_ref[...], prm_ref).astype(BF16)
            h_sc[...] = h
            h_ref[...] = h
            acc[...] = jnp.zeros_like(acc)

        w2 = w2_ref[...].reshape(2 * r8, d_model)
        for rows in _row_halves(tm):
            hb = h_sc[rows, :]
            g = _dot(hb, w1_ref[0])
            u = _dot(hb, w1_ref[1])
            sig = jax.nn.sigmoid(g)
            sil = g * sig
            s = (sil * u).astype(BF16)
            act_ref[0, rows, :] = (u * (sig * (1.0 + g * (1.0 - sig)))).astype(BF16)
            act_ref[1, rows, :] = sil.astype(BF16)
            act_ref[2, rows, :] = s
            acc[rows, :] += _dot(s, w2)

        @pl.when(j == N_CHUNK - 1)
        def _():
            yv = acc[...]
            y_ref[...] = yv.astype(BF16)
            xo_ref[...] = x_ref[...] + (0.5 * prm_ref[3:4, :]) * yv

    tile = pl.BlockSpec((tm, d_model), lambda m, j: (m, 0))
    return pl.pallas_call(
        body, name=f"ffn_fwd_{l}_{sub}", grid=(t_len // tm, N_CHUNK),
        in_specs=[tile, pl.BlockSpec((8, d_model), lambda m, j: (0, 0)),
                  pl.BlockSpec((2, None, d_model, f8), lambda m, j: (0, j, 0, 0)),
                  pl.BlockSpec((2, r8, d_model), lambda m, j: (j, 0, 0))],
        out_specs=[pl.BlockSpec((3, None, tm, f8), lambda m, j: (0, j, m, 0)), tile, tile, tile],
        out_shape=[SDS((3, N_CHUNK, t_len, f8), BF16), SDS((t_len, d_model), BF16),
                   SDS((t_len, d_model), BF16), SDS((t_len, d_model), F32)],
        scratch_shapes=[pltpu.VMEM((tm, d_model), BF16), pltpu.VMEM((tm, d_model), F32)],
        compiler_params=_params(2),
    )(x, prm, w1r, w2g)


def _ffn_bwd(dxo, y, x_in, prm, act, w1r, w2g, l, sub, tm):
    t_len, d_model = x_in.shape
    f8, r8 = w1r.shape[-1], w2g.shape[-2]
    n_m = t_len // tm

    def body(dxo_ref, y_ref, x_ref, prm_ref, act_ref, w1_ref, w2_ref,
             dx_ref, da_ref, dy_ref, st_ref, dy_sc, acc):
        m, j = pl.program_id(0), pl.program_id(1)

        @pl.when((m == 0) & (j == 0))
        def _():
            st_ref[...] = jnp.zeros_like(st_ref)

        @pl.when(j == 0)
        def _():
            dxo_v = dxo_ref[...]
            dy = ((0.5 * prm_ref[3:4, :]) * dxo_v).astype(BF16)
            dy_sc[...] = dy
            dy_ref[...] = dy
            st_ref[3:4, :] += jnp.sum(0.5 * dxo_v * y_ref[...].astype(F32), axis=0, keepdims=True)
            acc[...] = jnp.zeros_like(acc)

        w2 = w2_ref[...].reshape(2 * r8, d_model)
        for rows in _row_halves(tm):
            ds = _dot_nt(dy_sc[rows, :], w2)
            dg = (ds * act_ref[0, rows, :].astype(F32)).astype(BF16)
            du = (ds * act_ref[1, rows, :].astype(F32)).astype(BF16)
            da_ref[0, rows, :] = dg
            da_ref[1, rows, :] = du
            acc[rows, :] += _dot_nt(dg, w1_ref[0]) + _dot_nt(du, w1_ref[1])

        @pl.when(j == N_CHUNK - 1)
        def _():
            dx, dshift, dscale, dng = _ada_bwd(acc[...], x_ref[...], prm_ref)
            dx_ref[...] = dxo_ref[...] + dx
            st_ref[0:1, :] += dshift
            st_ref[1:2, :] += dscale
            st_ref[2:3, :] += dng

    tile = pl.BlockSpec((tm, d_model), lambda m, j: (m, 0))
    fixed = pl.BlockSpec((8, d_model), lambda m, j: (0, 0))
    return pl.pallas_call(
        body, name=f"ffn_bwd_{l}_{sub}", grid=(n_m, N_CHUNK),
        in_specs=[tile, tile, tile, fixed,
                  pl.BlockSpec((2, None, tm, f8), lambda m, j: (0, j, m, 0)),
                  pl.BlockSpec((2, None, d_model, f8), lambda m, j: (0, j, 0, 0)),
                  pl.BlockSpec((2, r8, d_model), lambda m, j: (j, 0, 0))],
        out_specs=[tile, pl.BlockSpec((2, None, tm, f8), lambda m, j: (0, j, m, 0)), tile, fixed],
        out_shape=[SDS((t_len, d_model), F32), SDS((2, N_CHUNK, t_len, f8), BF16),
                   SDS((t_len, d_model), BF16), SDS((8, d_model), F32)],
        scratch_shapes=[pltpu.VMEM((tm, d_model), BF16), pltpu.VMEM((tm, d_model), F32)],
        compiler_params=_params(2),
    )(dxo, y, x_in, prm, act, w1r, w2g)


def _matmul_tn(a, b, a_spec, b_spec, out_shape, out_spec, n_piece, t_len, tk, name, after):
    def body(a_ref, b_ref, after_ref, o_ref, acc):
        k = pl.program_id(1)

        @pl.when(k == 0)
        def _():
            acc[...] = jnp.zeros_like(acc)

        m_out, n_out = acc.shape
        if (m_out // 2) % LANES == 0:
            for rows in _row_halves(m_out):
                acc[rows, :] += _dot_tn(a_ref[:, rows], b_ref[...])
        else:
            for cols in _row_halves(n_out):
                acc[:, cols] += _dot_tn(a_ref[...], b_ref[:, cols])

        @pl.when(k == pl.num_programs(1) - 1)
        def _():
            o_ref[...] = acc[...].astype(o_ref.dtype)

    acc_shape = tuple(d for d in out_spec.block_shape if d is not None)
    return pl.pallas_call(
        body, name=name, grid=(n_piece, t_len // tk),
        in_specs=[a_spec, b_spec, ANY], out_specs=out_spec, out_shape=out_shape,
        scratch_shapes=[pltpu.VMEM(acc_shape, F32)],
        compiler_params=_params(2),
    )(a, b, after)


def _groups(tm, width):
    return pltpu.VMEM((width // LANES, tm, LANES), F32)


def _to_groups(val, dst_ref, first=0):
    for g in range(val.shape[1] // LANES):
        dst_ref[first + g] = val[:, g * LANES:(g + 1) * LANES]


def _from_groups(src_ref):
    return jnp.concatenate([src_ref[g] for g in range(src_ref.shape[0])], axis=1)


def _split_residues(src_ref, out_ref, dil, tm):
    for g in range(src_ref.shape[0]):
        for r in range(dil):
            out_ref[r, :, g * LANES:(g + 1) * LANES] = (
                src_ref.at[g][pl.ds(r, tm // dil, stride=dil), :].astype(out_ref.dtype))


def _merge_residues(in_ref, dst_ref, dil, tm, add=False):
    for g in range(dst_ref.shape[0]):
        for r in range(dil):
            rows = pl.ds(r, tm // dil, stride=dil)
            val = in_ref[r, :, g * LANES:(g + 1) * LANES].astype(F32)
            dst = dst_ref.at[g]
            dst[rows, :] = (dst[rows, :] + val) if add else val


def _res_spec(dil, tm, width, index=lambda m: m):
    return pl.BlockSpec((dil, tm // dil, width), lambda m: (0, index(m), 0))


def _mixer_in(x, prm, wing, qkg, cprm, bd, l, tm):
    t_len, d_model = x.shape
    a_w = d_model // 2
    c_w = d_model - a_w
    c8 = wing.shape[-1]
    p_c = N_DEV * c8
    dils = DILATIONS[1:]

    def body(x_ref, prm_ref, w_ref, qkg_ref, cprm_ref, bd_ref, proj_ref, qkn_ref, h_ref, ycv_ref, *rest):
        res_refs, (pf, qsc, vsc, carry) = rest[:2 * len(dils)], rest[2 * len(dils):]

        @pl.when(pl.program_id(0) == 0)
        def _():
            carry[...] = jnp.zeros_like(carry)

        h = _ada_fwd(x_ref[...], prm_ref).astype(BF16)
        h_ref[...] = h
        for j in range(N_DEV):
            pf[:, j * c8:(j + 1) * c8] = _dot(h, w_ref[j])
        proj_ref[...] = pf[...].astype(BF16)
        bdm = bd_ref[...]
        for i in range(2):
            t = pf[:, i * a_w:(i + 1) * a_w]
            rstd = lax.rsqrt(_head_mean(t * t, bdm) + EPS)
            tn = t * rstd * qkg_ref[i:i + 1, :]
            qkn_ref[:, i * a_w:(i + 1) * a_w] = tn.astype(BF16)
            _to_groups(tn, qsc, i * (a_w // LANES))
        _to_groups(pf[:, 2 * a_w:3 * a_w], vsc)
        for i, dil in enumerate(dils):
            _split_residues(qsc, res_refs[2 * i], dil, tm)
            _split_residues(vsc, res_refs[2 * i + 1], dil, tm)
        gb = pf[:, 3 * a_w:3 * a_w + c_w]
        z = pf[:, 3 * a_w + c_w:3 * a_w + 2 * c_w] * pf[:, 3 * a_w + 2 * c_w:3 * a_w + 3 * c_w]
        z1, z2 = _shift_rows(z, carry[...], tm)
        cv = cprm_ref[0:1, :] * z2 + cprm_ref[1:2, :] * z1 + cprm_ref[2:3, :] * z + cprm_ref[3:4, :]
        ycv_ref[...] = (gb * cv).astype(BF16)
        carry[...] = z[tm - 8:tm, :]

    res_specs, res_shapes = [], []
    for dil in dils:
        res_specs += [_res_spec(dil, tm, 2 * a_w), _res_spec(dil, tm, a_w)]
        res_shapes += [SDS((dil, t_len // dil, 2 * a_w), BF16), SDS((dil, t_len // dil, a_w), BF16)]
    return pl.pallas_call(
        body, name=f"mixer_in_{l}", grid=(t_len // tm,),
        in_specs=[pl.BlockSpec((tm, d_model), lambda m: (m, 0)),
                  pl.BlockSpec((8, d_model), lambda m: (0, 0)),
                  pl.BlockSpec((N_DEV, d_model, c8), lambda m: (0, 0, 0)),
                  pl.BlockSpec((8, a_w), lambda m: (0, 0)),
                  pl.BlockSpec((8, c_w), lambda m: (0, 0)),
                  pl.BlockSpec(bd.shape, lambda m: (0, 0))],
        out_specs=[pl.BlockSpec((tm, p_c), lambda m: (m, 0)),
                   pl.BlockSpec((tm, 2 * a_w), lambda m: (m, 0)),
                   pl.BlockSpec((tm, d_model), lambda m: (m, 0)),
                   pl.BlockSpec((tm, c_w), lambda m: (m, 0))] + res_specs,
        out_shape=[SDS((t_len, p_c), BF16), SDS((t_len, 2 * a_w), BF16),
                   SDS((t_len, d_model), BF16), SDS((t_len, c_w), BF16)] + res_shapes,
        scratch_shapes=[pltpu.VMEM((tm, p_c), F32), _groups(tm, 2 * a_w), _groups(tm, a_w),
                        pltpu.VMEM((8, c_w), F32)],
        compiler_params=_params(1),
    )(x, prm, wing, qkg, cprm, bd)


def _band_mask(first_block):
    qi = lax.broadcasted_iota(jnp.int32, (BLOCK, 2 * BLOCK), 0)
    kj = lax.broadcasted_iota(jnp.int32, (BLOCK, 2 * BLOCK), 1)
    lowest = jnp.where(first_block, BLOCK, 0)
    return (kj >= qi) & (kj <= qi + BLOCK) & (kj >= lowest)


def _attn_fwd(qk, v, v_col, l):
    dil, seq, a2 = qk.shape
    a_w = a2 // 2
    n_b = seq // BLOCK

    def body(q_ref, kc_ref, vc_ref, o_ref, lse_ref, kp_ref, vp_ref):
        @pl.when(pl.program_id(1) == 0)
        def _():
            kp_ref[...] = jnp.zeros_like(kp_ref)
            vp_ref[...] = jnp.zeros_like(vp_ref)

        valid = _band_mask(pl.program_id(1) == 0)
        low = lax.broadcasted_iota(jnp.int32, (BLOCK, LANES), 1) < HEAD_DIM
        n_pair = a_w // LANES
        slices = [slice(p * LANES, (p + 1) * LANES) for p in range(n_pair)]
        scores = []
        for sl in slices:
            qp = q_ref[:, sl]
            k2 = jnp.concatenate([kp_ref[:, sl], kc_ref[:, sl]], axis=0)
            scores += [_dot_nt(jnp.where(half, qp, jnp.zeros_like(qp)), k2) for half in (low, ~low)]
        probs, lses = [], []
        for s in scores:
            s = jnp.where(valid, s * SM_SCALE, NEG)
            mx = jnp.max(s, axis=-1, keepdims=True)
            e = jnp.exp(s - mx)
            den = jnp.sum(e, axis=-1, keepdims=True)
            probs.append((e * (1.0 / den)).astype(BF16))
            lses.append(mx + jnp.log(den))
        for p, sl in enumerate(slices):
            v2 = jnp.concatenate([vp_ref[:, sl], vc_ref[:, sl]], axis=0)
            o_ref[:, sl] = jnp.where(low, _dot(probs[2 * p], v2), _dot(probs[2 * p + 1], v2)).astype(BF16)
            lse_ref[:, sl] = jnp.where(low, lses[2 * p], lses[2 * p + 1])
        kp_ref[...] = kc_ref[...]
        vp_ref[...] = vc_ref[...]

    def blk(f):
        return pl.BlockSpec((None, BLOCK, a_w), f)

    return pl.pallas_call(
        body, name=f"attn_fwd_{l}_d{dil}", grid=(dil, n_b),
        in_specs=[blk(lambda r, b: (r, b, 0)),
                  blk(lambda r, b: (r, b, 1)),
                  blk(lambda r, b: (r, b, v_col))],
        out_specs=[blk(lambda r, b: (r, b, 0)), blk(lambda r, b: (r, b, 0))],
        out_shape=[SDS((dil, seq, a_w), BF16), SDS((dil, seq, a_w), F32)],
        scratch_shapes=[pltpu.VMEM((BLOCK, a_w), BF16), pltpu.VMEM((BLOCK, a_w), BF16)],
        compiler_params=_params(2),
    )(qk, qk, v)


def _attn_combine(outs, lses, ycv, tm, l):
    t_len, c_w = ycv.shape
    a_w = outs[0].shape[-1]
    dils = DILATIONS[1:]
    n_r = len(dils)

    def body(o1, l1, *rest):
        o_res, l_res, y_ref = rest[:n_r], rest[n_r:2 * n_r], rest[2 * n_r]
        cat_ref, lse_ref = rest[2 * n_r + 1], rest[2 * n_r + 2]
        ya_res, lse_res = rest[2 * n_r + 3:3 * n_r + 3], rest[3 * n_r + 3:4 * n_r + 3]
        scr = rest[4 * n_r + 3:]
        so, sl, sya, slse = scr[:n_r], scr[n_r:2 * n_r], scr[2 * n_r], scr[2 * n_r + 1]
        for i, dil in enumerate(dils):
            _merge_residues(o_res[i], so[i], dil, tm)
            _merge_residues(l_res[i], sl[i], dil, tm)
        ls = [l1[...]] + [_from_groups(s) for s in sl]
        ov = [o1[...].astype(F32)] + [_from_groups(s) for s in so]
        mx = jnp.maximum(jnp.maximum(ls[0], ls[1]), ls[2])
        es = [jnp.exp(t - mx) for t in ls]
        den = es[0] + es[1] + es[2]
        ya = (es[0] * ov[0] + es[1] * ov[1] + es[2] * ov[2]) * (1.0 / den)
        lse = mx + jnp.log(den)
        _to_groups(ya, sya)
        _to_groups(lse, slse)
        cat_ref[:, :a_w] = ya.astype(BF16)
        cat_ref[:, a_w:] = y_ref[...]
        lse_ref[...] = lse
        for i, dil in enumerate(dils):
            _split_residues(sya, ya_res[i], dil, tm)
            _split_residues(slse, lse_res[i], dil, tm)

    ta = pl.BlockSpec((tm, a_w), lambda m: (m, 0))
    res = [_res_spec(dil, tm, a_w) for dil in dils]
    return pl.pallas_call(
        body, name=f"attn_combine_{l}", grid=(t_len // tm,),
        in_specs=[ta, ta] + res + res + [pl.BlockSpec((tm, c_w), lambda m: (m, 0))],
        out_specs=[pl.BlockSpec((tm, a_w + c_w), lambda m: (m, 0)), ta] + res + res,
        out_shape=[SDS((t_len, a_w + c_w), BF16), SDS((t_len, a_w), F32)]
        + [SDS((dil, t_len // dil, a_w), BF16) for dil in dils]
        + [SDS((dil, t_len // dil, a_w), F32) for dil in dils],
        scratch_shapes=[_groups(tm, a_w)] * (2 * n_r + 2),
        compiler_params=_params(1),
    )(outs[0].reshape(t_len, a_w), lses[0].reshape(t_len, a_w), *outs[1:], *lses[1:], ycv)


def _mixer_out(cat, x, prm, woutg, l, tm):
    t_len, d_model = x.shape
    r8 = woutg.shape[-2]

    def body(cat_ref, x_ref, prm_ref, w_ref, xo_ref, y_ref):
        w = w_ref[...].reshape(N_DEV * r8, d_model)
        for rows in _row_halves(tm):
            yv = _dot(cat_ref[rows, :], w)
            y_ref[rows, :] = yv.astype(BF16)
            xo_ref[rows, :] = x_ref[rows, :] + prm_ref[3:4, :] * yv

    tile = pl.BlockSpec((tm, d_model), lambda m: (m, 0))
    return pl.pallas_call(
        body, name=f"mixer_out_{l}", grid=(t_len // tm,),
        in_specs=[tile, tile, pl.BlockSpec((8, d_model), lambda m: (0, 0)),
                  pl.BlockSpec((N_DEV, r8, d_model), lambda m: (0, 0, 0))],
        out_specs=[tile, tile],
        out_shape=[SDS((t_len, d_model), F32), SDS((t_len, d_model), BF16)],
        compiler_params=_params(1),
    )(cat, x, prm, woutg)


def _mixer_bwd_out(dxo, ymix, prm, woutg, proj, cprm, l, tm):
    t_len, d_model = dxo.shape
    a_w = d_model // 2
    c_w = d_model - a_w
    r8 = woutg.shape[-2]
    n_m = t_len // tm
    hb = tm // 8

    dils = DILATIONS[1:]

    def body(dxo_ref, ym_ref, prm_ref, w_ref, pc_ref, halo_ref, cprm_ref,
             dya_ref, dpc_ref, dy_ref, st_ref, cs_ref, *rest):
        dya_res, (carry, dsc) = rest[:len(dils)], rest[len(dils):]
        step = pl.program_id(0)
        tile_i = n_m - 1 - step

        @pl.when(step == 0)
        def _():
            st_ref[...] = jnp.zeros_like(st_ref)
            cs_ref[...] = jnp.zeros_like(cs_ref)
            carry[...] = jnp.zeros_like(carry)

        dxo_v = dxo_ref[...]
        dy = (prm_ref[3:4, :] * dxo_v).astype(BF16)
        dy_ref[...] = dy
        st_ref[3:4, :] += jnp.sum(dxo_v * ym_ref[...].astype(F32), axis=0, keepdims=True)
        w = w_ref[...].reshape(N_DEV * r8, d_model)
        dcat = jnp.concatenate([_dot_nt(dy[rows, :], w) for rows in _row_halves(tm)], axis=0)
        _to_groups(dcat[:, :a_w], dsc)
        dya_ref[...] = dcat[:, :a_w].astype(BF16)
        for i, dil in enumerate(dils):
            _split_residues(dsc, dya_res[i], dil, tm)
        dyc = dcat[:, a_w:]
        gb = pc_ref[:, :c_w].astype(F32)
        gc = pc_ref[:, c_w:2 * c_w].astype(F32)
        u = pc_ref[:, 2 * c_w:].astype(F32)
        z = gc * u
        before = halo_ref[:, c_w:2 * c_w].astype(F32) * halo_ref[:, 2 * c_w:].astype(F32)
        before = jnp.where(tile_i > 0, before, jnp.zeros_like(before))
        z1, z2 = _shift_rows(z, before, tm)
        w0, w1, w2 = cprm_ref[0:1, :], cprm_ref[1:2, :], cprm_ref[2:3, :]
        cv = w0 * z2 + w1 * z1 + w2 * z + cprm_ref[3:4, :]
        dcv = dyc * gb
        cs_ref[0:1, :] += jnp.sum(dcv * z2, axis=0, keepdims=True)
        cs_ref[1:2, :] += jnp.sum(dcv * z1, axis=0, keepdims=True)
        cs_ref[2:3, :] += jnp.sum(dcv * z, axis=0, keepdims=True)
        cs_ref[3:4, :] += jnp.sum(dcv, axis=0, keepdims=True)
        row = lax.broadcasted_iota(jnp.int32, dcv.shape, 0)
        after = carry[...]
        d1 = jnp.where(row == tm - 1, after[0:1, :], pltpu.roll(dcv, tm - 1, 0))
        d2 = jnp.where(row == tm - 2, after[0:1, :],
                       jnp.where(row == tm - 1, after[1:2, :], pltpu.roll(dcv, tm - 2, 0)))
        dz = w2 * dcv + w1 * d1 + w0 * d2
        dpc_ref[:, :c_w] = (dyc * cv).astype(BF16)
        dpc_ref[:, c_w:2 * c_w] = (dz * u).astype(BF16)
        dpc_ref[:, 2 * c_w:] = (dz * gc).astype(BF16)
        carry[...] = dcv[0:8, :]

    def rev(width, col=0):
        return pl.BlockSpec((tm, width), lambda s: (n_m - 1 - s, col))

    fixed_d = pl.BlockSpec((8, d_model), lambda s: (0, 0))
    fixed_c = pl.BlockSpec((8, c_w), lambda s: (0, 0))
    return pl.pallas_call(
        body, name=f"mixer_bwd_out_{l}", grid=(n_m,),
        in_specs=[rev(d_model), rev(d_model), fixed_d,
                  pl.BlockSpec((N_DEV, r8, d_model), lambda s: (0, 0, 0)),
                  rev(3 * c_w, 1),
                  pl.BlockSpec((8, 3 * c_w), lambda s: (jnp.maximum((n_m - 1 - s) * hb - 1, 0), 1)),
                  fixed_c],
        out_specs=[rev(a_w), rev(3 * c_w), rev(d_model), fixed_d, fixed_c]
        + [_res_spec(dil, tm, a_w, lambda s: n_m - 1 - s) for dil in dils],
        out_shape=[SDS((t_len, a_w), BF16), SDS((t_len, 3 * c_w), BF16), SDS((t_len, d_model), BF16),
                   SDS((8, d_model), F32), SDS((8, c_w), F32)]
        + [SDS((dil, t_len // dil, a_w), BF16) for dil in dils],
        scratch_shapes=[pltpu.VMEM((8, c_w), F32), _groups(tm, a_w)],
        compiler_params=_params(1),
    )(dxo, ymix, prm, woutg, proj, proj, cprm)


def _attn_bwd(qk, v, v_col, dya, ya, lse, l):
    dil, seq, a2 = qk.shape
    a_w = a2 // 2
    n_b = seq // BLOCK

    def body(q_ref, kc_ref, vc_ref, do_ref, ya_ref, lse_ref,
             dq_ref, dk_ref, dv_ref, ck, cvv, kp_ref, vp_ref):
        j = pl.program_id(1)

        @pl.when(j == 0)
        def _():
            kp_ref[...] = jnp.zeros_like(kp_ref)
            vp_ref[...] = jnp.zeros_like(vp_ref)

        @pl.when((pl.program_id(0) == 0) & (j == 0))
        def _():
            ck[...] = jnp.zeros_like(ck)
            cvv[...] = jnp.zeros_like(cvv)

        @pl.when(j < n_b)
        def _():
            valid = _band_mask(j == 0)
            low = lax.broadcasted_iota(jnp.int32, (BLOCK, LANES), 1) < HEAD_DIM
            slices = [slice(p * LANES, (p + 1) * LANES) for p in range(a_w // LANES)]
            heads, scores, dps = [], [], []
            for sl in slices:
                qp = q_ref[:, sl]
                k2 = jnp.concatenate([kp_ref[:, sl], kc_ref[:, sl]], axis=0)
                v2 = jnp.concatenate([vp_ref[:, sl], vc_ref[:, sl]], axis=0)
                dop = do_ref[:, sl]
                prod = dop.astype(F32) * ya_ref[:, sl].astype(F32)
                lsep = lse_ref[:, sl]
                for hh, half in enumerate((low, ~low)):
                    qh = jnp.where(half, qp, jnp.zeros_like(qp))
                    doh = jnp.where(half, dop, jnp.zeros_like(dop))
                    dsum = jnp.sum(jnp.where(half, prod, 0.0), axis=-1, keepdims=True)
                    heads.append((qh, doh, dsum, lsep[:, hh * HEAD_DIM:hh * HEAD_DIM + 1]))
                    scores.append(_dot_nt(qh, k2))
                    dps.append(_dot_nt(doh, v2))
            dss, prs = [], []
            for (qh, doh, dsum, lse_h), s, dp in zip(heads, scores, dps):
                pr = jnp.exp(jnp.where(valid, s * SM_SCALE, NEG) - lse_h)
                dss.append((pr * (dp - dsum) * SM_SCALE).astype(BF16))
                prs.append(pr.astype(BF16))
            for p, sl in enumerate(slices):
                k2 = jnp.concatenate([kp_ref[:, sl], kc_ref[:, sl]], axis=0)
                h0, h1 = heads[2 * p], heads[2 * p + 1]
                dq_ref[:, sl] = jnp.where(low, _dot(dss[2 * p], k2), _dot(dss[2 * p + 1], k2)).astype(BF16)
                dk2 = _dot_tn(dss[2 * p], h0[0]) + _dot_tn(dss[2 * p + 1], h1[0])
                dv2 = _dot_tn(prs[2 * p], h0[1]) + _dot_tn(prs[2 * p + 1], h1[1])
                dk_ref[:, sl] = (ck[:, sl] + dk2[:BLOCK]).astype(BF16)
                dv_ref[:, sl] = (cvv[:, sl] + dv2[:BLOCK]).astype(BF16)
                ck[:, sl] = dk2[BLOCK:]
                cvv[:, sl] = dv2[BLOCK:]
            kp_ref[...] = kc_ref[...]
            vp_ref[...] = vc_ref[...]

        @pl.when(j == n_b)
        def _():
            dk_ref[...] = ck[...].astype(BF16)
            dv_ref[...] = cvv[...].astype(BF16)

    def blk(f):
        return pl.BlockSpec((None, BLOCK, a_w), f)

    def cur(j):
        return jnp.minimum(j, n_b - 1)

    def late(j):
        return jnp.maximum(j - 1, 0)

    return pl.pallas_call(
        body, name=f"attn_bwd_{l}_d{dil}", grid=(dil, n_b + 1),
        in_specs=[blk(lambda r, j: (r, cur(j), 0)),
                  blk(lambda r, j: (r, cur(j), 1)),
                  blk(lambda r, j: (r, cur(j), v_col)),
                  blk(lambda r, j: (r, cur(j), 0)),
                  blk(lambda r, j: (r, cur(j), 0)),
                  blk(lambda r, j: (r, cur(j), 0))],
        out_specs=[blk(lambda r, j: (r, cur(j), 0)), blk(lambda r, j: (r, late(j), 0)),
                   blk(lambda r, j: (r, late(j), 0))],
        out_shape=[SDS((dil, seq, a_w), BF16)] * 3,
        scratch_shapes=[pltpu.VMEM((BLOCK, a_w), F32), pltpu.VMEM((BLOCK, a_w), F32),
                        pltpu.VMEM((BLOCK, a_w), BF16), pltpu.VMEM((BLOCK, a_w), BF16)],
        compiler_params=_params(2),
    )(qk, qk, v, dya, ya, lse)


def _mixer_bwd_in(parts, proj, dpc, dxo, x_in, prm, wing, qkg, bd, l, tm):
    t_len, d_model = x_in.shape
    a_w = d_model // 2
    c_w = d_model - a_w
    c8 = wing.shape[-1]
    p_c = N_DEV * c8

    dils = DILATIONS[1:]
    n_in = 3 * len(DILATIONS)

    def body(*refs):
        nat, res = refs[:3], refs[3:n_in]
        (pa_ref, dpc_ref, dxo_ref, x_ref, prm_ref, w_ref, qkg_ref, bd_ref,
         dx_ref, dproj_ref, st_ref, qs_ref) = refs[n_in:n_in + 12]
        sums = refs[n_in + 12:]

        @pl.when(pl.program_id(0) == 0)
        def _():
            st_ref[...] = jnp.zeros_like(st_ref)
            qs_ref[...] = jnp.zeros_like(qs_ref)

        for t in range(3):
            _to_groups(nat[t][...].astype(F32), sums[t])
            for i, dil in enumerate(dils):
                _merge_residues(res[3 * i + t], sums[t], dil, tm, add=True)
        bdm = bd_ref[...]
        grads = [_from_groups(sums[0]), _from_groups(sums[1])]
        for i in range(2):
            t = pa_ref[:, i * a_w:(i + 1) * a_w].astype(F32)
            rstd = lax.rsqrt(_head_mean(t * t, bdm) + EPS)
            that = t * rstd
            qs_ref[i:i + 1, :] += jnp.sum(grads[i] * that, axis=0, keepdims=True)
            dhat = grads[i] * qkg_ref[i:i + 1, :]
            dt = rstd * (dhat - that * _head_mean(dhat * that, bdm))
            dproj_ref[:, i * a_w:(i + 1) * a_w] = dt.astype(BF16)
        dproj_ref[:, 2 * a_w:3 * a_w] = _from_groups(sums[2]).astype(BF16)
        dproj_ref[:, 3 * a_w:] = dpc_ref[...]
        dh = _dot_nt(dproj_ref[:, 0:c8], w_ref[0])
        for j in range(1, N_DEV):
            dh = dh + _dot_nt(dproj_ref[:, j * c8:(j + 1) * c8], w_ref[j])
        dx, dshift, dscale, dng = _ada_bwd(dh, x_ref[...], prm_ref)
        dx_ref[...] = dxo_ref[...] + dx
        st_ref[0:1, :] += dshift
        st_ref[1:2, :] += dscale
        st_ref[2:3, :] += dng

    ta = pl.BlockSpec((tm, a_w), lambda m: (m, 0))
    tile = pl.BlockSpec((tm, d_model), lambda m: (m, 0))
    return pl.pallas_call(
        body, name=f"mixer_bwd_in_{l}", grid=(t_len // tm,),
        in_specs=[ta] * 3 + [_res_spec(dil, tm, a_w) for dil in dils for _ in range(3)]
                 + [pl.BlockSpec((tm, 3 * a_w), lambda m: (m, 0)),
                             pl.BlockSpec((tm, 3 * c_w), lambda m: (m, 0)),
                             tile, tile, pl.BlockSpec((8, d_model), lambda m: (0, 0)),
                             pl.BlockSpec((N_DEV, d_model, c8), lambda m: (0, 0, 0)),
                             pl.BlockSpec((8, a_w), lambda m: (0, 0)),
                             pl.BlockSpec(bd.shape, lambda m: (0, 0))],
        out_specs=[tile, pl.BlockSpec((tm, p_c), lambda m: (m, 0)),
                   pl.BlockSpec((8, d_model), lambda m: (0, 0)), pl.BlockSpec((8, a_w), lambda m: (0, 0))],
        out_shape=[SDS((t_len, d_model), F32), SDS((t_len, p_c), BF16), SDS((8, d_model), F32), SDS((8, a_w), F32)],
        scratch_shapes=[_groups(tm, a_w)] * 3,
        compiler_params=_params(1),
    )(*[t.reshape(t_len, a_w) for t in parts[0]], *[t for p in parts[1:] for t in p],
      proj, dpc, dxo, x_in, prm, wing, qkg, bd)


def _loss_head(x, target, tm):
    t_len, d_model = x.shape

    def body(x_ref, t_ref, dx_ref, loss_ref):
        @pl.when(pl.program_id(0) == 0)
        def _():
            loss_ref[...] = jnp.zeros_like(loss_ref)

        diff = x_ref[...] - t_ref[...]
        dx_ref[...] = diff * (1.0 / d_model)
        per_token = jnp.sum(diff * diff, axis=-1, keepdims=True) * (1.0 / d_model)
        loss_ref[...] += 0.5 * jnp.sum(per_token)

    tile = pl.BlockSpec((tm, d_model), lambda m: (m, 0))
    return pl.pallas_call(
        body, name="loss_head", grid=(t_len // tm,),
        in_specs=[tile, tile],
        out_specs=[tile, pl.BlockSpec((8, LANES), lambda m: (0, 0))],
        out_shape=[SDS((t_len, d_model), F32), SDS((8, LANES), F32)],
        compiler_params=_params(1),
    )(x, target)


def _row_tile(rows, cols):
    best = None
    for t in range(16, rows + 1, 16):
        if rows % t == 0 and t * cols * 4 <= (1 << 20):
            best = t
    return best if best is not None else rows


def _adamw(pieces, w, m, v, name):
    n_q = len(pieces)
    n_p, rows, cols = pieces[0].shape
    tr = _row_tile(rows, cols)
    n_i = rows // tr
    c1 = 1.0 - ADAM_B1 ** ADAM_STEP
    c2 = 1.0 - ADAM_B2 ** ADAM_STEP

    def body(*refs):
        p_refs = refs[:n_q]
        w_ref, m_ref, v_ref, g_ref, d_ref, nm_ref, nv_ref = refs[n_q:]
        for q in range(n_q):
            @pl.when(pl.program_id(0) == q)
            def _():
                g = p_refs[q][0].astype(F32)
                for i in range(1, n_p):
                    g = g + p_refs[q][i].astype(F32)
                g_ref[...] = g
                nm = ADAM_B1 * m_ref[...] + (1.0 - ADAM_B1) * g
                nv = ADAM_B2 * v_ref[...] + (1.0 - ADAM_B2) * (g * g)
                nm_ref[...] = nm
                nv_ref[...] = nv
                d_ref[...] = -ADAM_LR * ((nm / c1) / (jnp.sqrt(nv / c2) + ADAM_EPS) + ADAM_WD * w_ref[...])

    def piece_spec(q):
        return pl.BlockSpec((n_p, tr, cols), lambda a, i: (0, jnp.where(a == q, i, 0), 0))

    tile = pl.BlockSpec((tr, cols), lambda a, i: (a * n_i + i, 0))
    return pl.pallas_call(
        body, name=name, grid=(n_q, n_i),
        in_specs=[piece_spec(q) for q in range(n_q)] + [tile, tile, tile],
        out_specs=[tile] * 4,
        out_shape=[SDS((n_q * rows, cols), F32)] * 4,
        compiler_params=_params(2),
    )(*pieces, w, m, v)


def _pack(vecs):
    flat = jnp.concatenate([v.reshape(-1).astype(F32) for v in vecs])
    rows = -(-flat.shape[0] // (8 * LANES)) * 8
    return jnp.pad(flat, (0, rows * LANES - flat.shape[0])).reshape(rows, LANES)


def _unpack(packed, shapes):
    lead = packed.shape[:-2]
    flat = packed.reshape(lead + (-1,))
    out, off = [], 0
    for s in shapes:
        size = 1
        for d in s:
            size *= d
        out.append(flat[..., off:off + size].reshape(lead + tuple(s)))
        off += size
    return out


def kernel(x, c, w_ada, b_ada, norm_g, w_in, q_norm_g, k_norm_g, conv_w, conv_b, w_out, ffn_w1, ffn_w2, loss_target, m_w_ada, m_b_ada, m_norm_g, m_w_in, m_q_norm_g, m_k_norm_g, m_conv_w, m_conv_b, m_w_out, m_ffn_w1, m_ffn_w2, v_w_ada, v_b_ada, v_norm_g, v_w_in, v_q_norm_g, v_k_norm_g, v_conv_w, v_conv_b, v_w_out, v_ffn_w1, v_ffn_w2):
    _, t_len, d_model = x.shape
    n_layer = w_ada.shape[0]
    a_w = d_model // 2
    c_w = d_model - a_w
    n_head = a_w // HEAD_DIM
    ada_cols = w_ada.shape[-1]
    tm = min(TOKEN_TILE, t_len)
    me = _index(_my_place())
    x2 = x.reshape(t_len, d_model)
    target = loss_target.reshape(t_len, d_model)

    small_shapes = [(d_model,), norm_g.shape, conv_w.shape]
    gathered = _allgather_small(_pack([c, norm_g, conv_w]), "gather_small_inputs")
    c_all, ng_parts, cw_parts = _unpack(gathered, small_shapes)
    norm_g_full = jnp.moveaxis(ng_parts, 0, 2).reshape(n_layer, 3, d_model)
    conv_w_full = jnp.moveaxis(cw_parts, 0, 2).reshape(n_layer, 3, c_w)

    b_loc = lax.dynamic_slice_in_dim(b_ada, me * ada_cols, ada_cols, axis=1).reshape(n_layer, 1, ada_cols)
    mod_part, silu_c = _mod_part(c_all, w_ada, b_loc)
    mod_all = _allgather_small(_pack([mod_part]), "gather_mod")
    mod_all = _unpack(mod_all, [mod_part.shape])[0]
    mod_mine = lax.dynamic_index_in_dim(mod_all, me, axis=2, keepdims=False)
    mod = jnp.moveaxis(mod_mine, 0, 1).reshape(n_layer, 3, 3, d_model)

    def prm_of(l, sub):
        rows = jnp.stack([norm_g_full[l, sub], mod[l, sub, 0], mod[l, sub, 1], mod[l, sub, 2]])
        return jnp.pad(rows, ((0, 4), (0, 0)))

    prm = [[prm_of(l, sub) for sub in range(3)] for l in range(n_layer)]
    qkg = [jnp.pad(jnp.stack([jnp.tile(q_norm_g[l], n_head), jnp.tile(k_norm_g[l], n_head)]), ((0, 6), (0, 0)))
           for l in range(n_layer)]
    cprm = [jnp.pad(jnp.concatenate([conv_w_full[l], conv_b[l][None]]), ((0, 4), (0, 0))) for l in range(n_layer)]
    head_of = jnp.arange(min(MXU_WIDTH, a_w)) // HEAD_DIM
    bd = (head_of[:, None] == head_of[None, :]).astype(BF16)

    f8 = ffn_w1.shape[-1]
    r8 = ffn_w2.shape[-2]
    c8 = w_in.shape[-1]
    o8 = w_out.shape[-2]
    me_arr = jnp.reshape(me, (1,)).astype(jnp.int32)
    win_b = [_place_own(w_in, (l,), me_arr, f"own_w_in_{l}") for l in range(n_layer)]
    wout_b = [_place_own(w_out, (l,), me_arr, f"own_w_out_{l}") for l in range(n_layer)]
    w1_b = [[_place_own(ffn_w1, (l, s), me_arr, f"own_w1_{l}_{s}") for s in range(2)] for l in range(n_layer)]
    w2_b = [[_place_own(ffn_w2, (l, s), me_arr, f"own_w2_{l}_{s}") for s in range(2)] for l in range(n_layer)]
    w1_first, w2_first = _allgather_big([w1_b[0][0], w2_b[0][0]], "gather_first_weights")
    groups = [[win_b[0], wout_b[0]], [w1_b[0][1], w2_b[0][1]]]
    for l in range(1, n_layer):
        groups += [[w1_b[l][0], w2_b[l][0]], [win_b[l], wout_b[l]], [w1_b[l][1], w2_b[l][1]]]
    flat = [a for grp in groups for a in grp]
    w_sends, w_recvs, w_lands, w_token = _send_start(flat, None, w1_first, "weights_start")

    def gathered(gi, after):
        lo = sum(len(grp) for grp in groups[:gi])
        hi = lo + len(groups[gi])
        return _send_wait(w_lands[lo:hi], w_sends[lo:hi], w_recvs[lo:hi], after, f"weights_wait_{gi}")

    def chunked(w1g):
        return w1g.reshape(2, N_CHUNK, d_model, f8)

    saved = []
    xc = x2
    for l in range(n_layer):
        keep = {"x0": xc}
        if l == 0:
            w1g, w2g = w1_first, w2_first
            prm_first = prm[0][0] + w_token[0, 0]
        else:
            w1g, w2g = gathered(3 * l - 1, xc)
            prm_first = prm[l][0]
        keep["w_ffn0"] = (chunked(w1g), w2g)
        keep["act0"], keep["h0"], keep["y0"], xc = _ffn_fwd(xc, prm_first, *keep["w_ffn0"], l, 0, tm)
        keep["x1"] = xc
        wing, woutg = gathered(0 if l == 0 else 3 * l, xc)
        keep.update(wing=wing, woutg=woutg)
        proj, qkn, keep["h1"], ycv, *res = _mixer_in(xc, prm[l][1], wing, qkg[l], cprm[l], bd, l, tm)
        qkv = [(qkn[None], proj[None], 2)] + [(res[2 * i], res[2 * i + 1], 0) for i in range(len(DILATIONS) - 1)]
        branches = [_attn_fwd(*t, l) for t in qkv]
        cat, lse, *res = _attn_combine([b[0] for b in branches], [b[1] for b in branches], ycv, tm, l)
        n_r = len(DILATIONS) - 1
        keep["ya"] = [cat[None]] + res[:n_r]
        keep["lse"] = [lse[None]] + res[n_r:]
        xc, keep["ymix"] = _mixer_out(cat, xc, prm[l][1], woutg, l, tm)
        keep.update(proj=proj, qkv=qkv, cat=cat)
        keep["x2"] = xc
        w1g, w2g = gathered(1 if l == 0 else 3 * l + 1, xc)
        keep["w_ffn2"] = (chunked(w1g), w2g)
        keep["act2"], keep["h2"], keep["y2"], xc = _ffn_fwd(xc, prm[l][2], *keep["w_ffn2"], l, 1, tm)
        saved.append(keep)

    dx, loss_blk = _loss_head(xc, target, tm)
    loss = lax.psum(loss_blk[0, 0], ("x", "y", "c"))

    tk = min(GRAD_TOKEN_TILE, t_len)
    stats =[[None] * 3 for _ in range(n_layer)]
    conv_stats = [None] * n_layer
    qk_stats = [None] * n_layer

    started = {}
    dummy = jnp.zeros((8, LANES), F32)

    def start_exchange(g, name):
        sends, recvs, thru, token = _send_start(
            [_place_own(g, None, me_arr, f"own_grad_{name}")], [g], dummy, f"grads_start_{name}")
        started[name] = (thru[0], thru[1], sends[0], recvs[0])
        return token

    def wait_exchange(names, after, name):
        ent = [started[k] for k in names]
        done = _send_wait([e[0] for e in ent] + [e[1] for e in ent], [e[2] for e in ent], [e[3] for e in ent],
                          after, name)
        return dict(zip(names, done))

    def ffn_backward(dxo, keep, l, sub, which, prm_dep):
        dx_in, da, dy, st = _ffn_bwd(dxo, keep[f"y{which}"], keep[f"x{which}"], prm_dep,
                                     keep[f"act{which}"], *keep[f"w_ffn{which}"], l, sub, tm)
        g_w2 = _matmul_tn(
            keep[f"act{which}"], dy,
            pl.BlockSpec((None, None, tk, f8), lambda p, k: (2, p, k, 0)),
            pl.BlockSpec((tk, d_model), lambda p, k: (k, 0)),
            SDS((N_CHUNK, f8, d_model), BF16), pl.BlockSpec((None, f8, d_model), lambda p, k: (p, 0, 0)),
            N_CHUNK, t_len, tk, f"grad_w2_{l}_{sub}", dummy)
        token = start_exchange(g_w2.reshape(N_DEV, r8, d_model), f"w2_{l}_{sub}")
        g_w1 = _matmul_tn(
            keep[f"h{which}"], da.reshape(N_DEV, t_len, f8),
            pl.BlockSpec((tk, d_model), lambda p, k: (k, 0)),
            pl.BlockSpec((None, tk, f8), lambda p, k: (p, k, 0)),
            SDS((N_DEV, d_model, f8), BF16), pl.BlockSpec((None, d_model, f8), lambda p, k: (p, 0, 0)),
            N_DEV, t_len, tk, f"grad_w1_{l}_{sub}", token)
        token = start_exchange(g_w1, f"w1_{l}_{sub}")
        return dx_in, st, token[0, 0]

    token = 0.0
    for l in reversed(range(n_layer)):
        keep = saved[l]
        wing, woutg = keep["wing"], keep["woutg"]
        dx, stats[l][2], token = ffn_backward(dx, keep, l, 1, 2, prm[l][2] + token)
        dya, dpc, dy, st_gate, conv_stats[l], *dya_res = _mixer_bwd_out(
            dx, keep["ymix"], prm[l][1] + token, woutg, keep["proj"], cprm[l], l, tm)
        dyas = [dya[None]] + dya_res
        g_wout = _matmul_tn(
            keep["cat"], dy,
            pl.BlockSpec((tk, d_model), lambda p, k: (k, 0)),
            pl.BlockSpec((tk, d_model), lambda p, k: (k, 0)),
            SDS((d_model, d_model), BF16), pl.BlockSpec((d_model, d_model), lambda p, k: (0, 0)),
            1, t_len, tk, f"grad_wout_{l}", dummy)
        token = start_exchange(g_wout.reshape(N_DEV, o8, d_model), f"wout_{l}")[0, 0]
        parts = [_attn_bwd(*keep["qkv"][i], dyas[i], keep["ya"][i], keep["lse"][i], l)
                 for i in range(len(DILATIONS))]
        dx, dproj, st_norm, qk_stats[l] = _mixer_bwd_in(
            parts, keep["proj"], dpc, dx, keep["x1"], prm[l][1] + token, wing, qkg[l], bd, l, tm)
        stats[l][1] = st_norm + st_gate
        g_win = _matmul_tn(
            keep["h1"], dproj,
            pl.BlockSpec((tk, d_model), lambda p, k: (k, 0)),
            pl.BlockSpec((tk, c8), lambda p, k: (k, p)),
            SDS((N_DEV, d_model, c8), BF16), pl.BlockSpec((None, d_model, c8), lambda p, k: (p, 0, 0)),
            N_DEV, t_len, tk, f"grad_win_{l}", dummy)
        token = start_exchange(g_win, f"win_{l}")[0, 0]
        dx, stats[l][0], token = ffn_backward(dx, keep, l, 0, 0, prm[l][0] + token)
    grad_x = dx.reshape(x.shape)

    def update(pieces, w, m, v, name):
        shape = w.shape
        cols = shape[-1]
        outs = _adamw([p.reshape(p.shape[0], -1, cols) for p in pieces], w.reshape(-1, cols),
                      m.reshape(-1, cols), v.reshape(-1, cols), name)
        return [o.reshape(shape) for o in outs]

    last = "w1_0_0"
    lands = wait_exchange([k for k in started if k != last], dx, "grads_wait_early")
    res = {
        "w_in": update([lands[f"win_{l}"] for l in range(n_layer)], w_in, m_w_in, v_w_in, "adamw_w_in"),
        "w_out": update([lands[f"wout_{l}"] for l in range(n_layer)], w_out, m_w_out, v_w_out, "adamw_w_out"),
        "ffn_w2": update([lands[f"w2_{l}_{s}"] for l in range(n_layer) for s in range(2)],
                         ffn_w2, m_ffn_w2, v_ffn_w2, "adamw_ffn_w2"),
    }

    dmod = jnp.stack([jnp.stack([stats[l][sub][r] for sub in range(3) for r in (0, 1, 3)]) for l in range(n_layer)])
    dng = jnp.stack([jnp.stack([stats[l][sub][2] for sub in range(3)]) for l in range(n_layer)])
    dcw = jnp.stack([conv_stats[l][0:3] for l in range(n_layer)])
    dcb = jnp.stack([conv_stats[l][3] for l in range(n_layer)])
    dqk = jnp.stack([qk_stats[l][0:2].reshape(2, n_head, HEAD_DIM).sum(axis=1) for l in range(n_layer)])
    part_shapes = [dmod.shape, dng.shape, dcw.shape, dcb.shape, dqk.shape]
    parts_all = _allgather_small(_pack([dmod, dng, dcw, dcb, dqk]), "gather_small_grads")
    dmod_all = _unpack(parts_all, part_shapes)[0].reshape(N_DEV, n_layer, 9 * d_model)
    summed = _unpack(_sum_devices(parts_all), part_shapes)
    g_b_ada = summed[0].reshape(n_layer, 9 * d_model)
    ng_cols = norm_g.shape[-1]
    g_norm_g = lax.dynamic_slice_in_dim(summed[1], me * ng_cols, ng_cols, axis=2)
    cw_cols = conv_w.shape[-1]
    g_conv_w = lax.dynamic_slice_in_dim(summed[2], me * cw_cols, cw_cols, axis=2)
    g_conv_b = summed[3]
    g_q, g_k = summed[4][:, 0], summed[4][:, 1]
    dmod_loc = lax.dynamic_slice_in_dim(dmod_all, me * ada_cols, ada_cols, axis=2)
    g_w_ada = _w_ada_grad(silu_c.T, jnp.moveaxis(dmod_loc, 0, 1))
    res["w_ada"] = update([g_w_ada[None]], w_ada, m_w_ada, v_w_ada, "adamw_w_ada")

    names = ["b_ada", "norm_g", "q_norm_g", "k_norm_g", "conv_w", "conv_b"]
    g_small = [g_b_ada, g_norm_g, g_q, g_k, g_conv_w, g_conv_b]
    w_small = [b_ada, norm_g, q_norm_g, k_norm_g, conv_w, conv_b]
    m_small = [m_b_ada, m_norm_g, m_q_norm_g, m_k_norm_g, m_conv_w, m_conv_b]
    v_small = [v_b_ada, v_norm_g, v_q_norm_g, v_k_norm_g, v_conv_w, v_conv_b]
    outs = _adamw([_pack(g_small)[None]], _pack(w_small), _pack(m_small), _pack(v_small), "adamw_small")
    shapes = [w.shape for w in w_small]
    unpacked = [_unpack(o, shapes) for o in outs]
    for i, nme in enumerate(names):
        res[nme] = [unpacked[k][i] for k in range(4)]

    follows = (res["w_in"][1][0, :1, :1] + res["w_out"][1][0, :1, :1] + res["ffn_w2"][1][0, 0, :1, :1]
               + res["w_ada"][1][0, :1, :1] + outs[1][:1, :1])
    lands.update(wait_exchange([last], follows, "grads_wait_last"))
    res["ffn_w1"] = update([lands[f"w1_{l}_{s}"] for l in range(n_layer) for s in range(2)],
                           ffn_w1, m_ffn_w1, v_ffn_w1, "adamw_ffn_w1")

    order = ["w_ada", "b_ada", "norm_g", "w_in", "q_norm_g", "k_norm_g", "conv_w", "conv_b", "w_out", "ffn_w1", "ffn_w2"]
    return (loss, grad_x, *[res[n][0] for n in order], *[res[n][1] for n in order],
            *[res[n][2] for n in order], *[res[n][3] for n in order])
```

```python
import jax
import jax.numpy as jnp
from jax import lax
from jax.experimental import pallas as pl
from jax.experimental.pallas import tpu as pltpu

F32 = jnp.float32
BF16 = jnp.bfloat16
SDS = jax.ShapeDtypeStruct

N_DEV = 8
HEAD_DIM = 64
BLOCK = 128
DILATIONS = (1, 4, 16)
N_CHUNK = 4
EPS = 1e-6
NEG = -1e30
SM_SCALE = HEAD_DIM ** -0.5
LANES = 128
MXU_WIDTH = 256
TOKEN_TILE = 512
FFN_TOKEN_TILE = 1024
GRAD_TOKEN_TILE = 2048
VMEM_LIMIT_BYTES = 56 * 1024 * 1024

ADAM_LR = 0.001
ADAM_B1 = 0.9
ADAM_B2 = 0.999
ADAM_EPS = 1e-08
ADAM_WD = 0.01
ADAM_STEP = 10

MESH_ID = pl.DeviceIdType.MESH
ANY = pl.BlockSpec(memory_space=pl.ANY)
HBM_SPEC = pl.BlockSpec(memory_space=pltpu.HBM)
SEM_SPEC = pl.BlockSpec(memory_space=pltpu.SEMAPHORE)
SIDE_EFFECT = pltpu.SideEffectType.DATAFLOW_SIDE_EFFECTING


def _params(n_axes):
    return pltpu.CompilerParams(dimension_semantics=("arbitrary",) * n_axes,
                                vmem_limit_bytes=VMEM_LIMIT_BYTES)


def _dot(a, b):
    return jnp.dot(a, b, preferred_element_type=F32)


def _dot_nt(a, b):
    return lax.dot_general(a, b, (((1,), (1,)), ((), ())), preferred_element_type=F32)


def _dot_tn(a, b):
    return lax.dot_general(a, b, (((0,), (0,)), ((), ())), preferred_element_type=F32)


def _row_halves(n):
    return (slice(0, n // 2), slice(n // 2, n))


def _split_bf16(t):
    hi = t.astype(BF16)
    return hi, (t - hi.astype(F32)).astype(BF16)


def _my_place():
    x, y, c = lax.axis_index("x"), lax.axis_index("y"), lax.axis_index("c")
    return x, y, c


def _peer(place, k):
    x, y, c = place
    return ((1 - x) if k & 4 else x, (1 - y) if k & 2 else y, (1 - c) if k & 1 else c)


def _index(place):
    return 4 * place[0] + 2 * place[1] + place[2]


def _allgather_small(v, name):
    rows, cols = v.shape

    def body(x_ref, out_ref, send_sems, recv_sems, local_sem):
        me = _my_place()
        mine = pltpu.make_async_copy(x_ref, out_ref.at[_index(me)], local_sem)
        mine.start()
        sends = []
        for k in range(1, N_DEV):
            cp = pltpu.make_async_remote_copy(
                src_ref=x_ref, dst_ref=out_ref.at[_index(me)], send_sem=send_sems.at[k - 1],
                recv_sem=recv_sems.at[k - 1], device_id=_peer(me, k), device_id_type=MESH_ID)
            cp.start()
            sends.append(cp)
        for k in range(1, N_DEV):
            pltpu.make_async_remote_copy(
                src_ref=x_ref, dst_ref=out_ref.at[_index(_peer(me, k))], send_sem=send_sems.at[k - 1],
                recv_sem=recv_sems.at[k - 1], device_id=_peer(me, k), device_id_type=MESH_ID).wait_recv()
        for cp in sends:
            cp.wait_send()
        mine.wait()

    return pl.pallas_call(
        body, name=name,
        out_shape=SDS((N_DEV, rows, cols), v.dtype),
        in_specs=[pl.BlockSpec(memory_space=pltpu.VMEM)],
        out_specs=pl.BlockSpec(memory_space=pltpu.VMEM),
        scratch_shapes=[pltpu.SemaphoreType.DMA((N_DEV - 1,)), pltpu.SemaphoreType.DMA((N_DEV - 1,)),
                        pltpu.SemaphoreType.DMA],
    )(v)


def _allgather_big(lands, name):
    n = len(lands)

    def body(*refs):
        outs = refs[n:2 * n]
        send_sems, recv_sems = refs[2 * n:]
        x, y, c = _my_place()
        me, sib = (x, y, c), (x, y, 1 - c)
        chips = [(1 - x, y), (x, 1 - y), (1 - x, 1 - y)]

        def copy(i, k, block, to):
            dst = outs[i].at[_index(block)]
            return pltpu.make_async_remote_copy(
                src_ref=dst, dst_ref=dst, send_sem=send_sems.at[7 * i + k],
                recv_sem=recv_sems.at[7 * i + k], device_id=to, device_id_type=MESH_ID)

        first = []
        for i in range(n):
            first.append(copy(i, 0, me, sib))
            for j, chip in enumerate(chips):
                first.append(copy(i, 1 + j, me, (*chip, c)))
        for cp in first:
            cp.start()
        passed = []
        for i in range(n):
            for j, chip in enumerate(chips):
                copy(i, 1 + j, (*chip, c), me).wait_recv()
                fwd = copy(i, 4 + j, (*chip, c), sib)
                fwd.start()
                passed.append(fwd)
        for i in range(n):
            copy(i, 0, sib, me).wait_recv()
            for j, chip in enumerate(chips):
                copy(i, 4 + j, (*chip, 1 - c), me).wait_recv()
        for cp in first + passed:
            cp.wait_send()

    return pl.pallas_call(
        body, name=name,
        out_shape=[SDS(a.shape, a.dtype) for a in lands],
        in_specs=[ANY] * n, out_specs=[ANY] * n,
        input_output_aliases={i: i for i in range(n)},
        scratch_shapes=[pltpu.SemaphoreType.DMA((7 * n,)), pltpu.SemaphoreType.DMA((7 * n,))],
    )(*lands)


def _place_own(src, lead, me_arr, name):
    rows, cols = src.shape[-2:]
    tr = _row_tile(rows, cols)

    def body(me_ref, s_ref, o_ref):
        o_ref[...] = s_ref[...].astype(BF16)

    if lead is None:
        in_spec = pl.BlockSpec((None, tr, cols), lambda i, me_ref: (me_ref[0], i, 0))
    else:
        in_spec = pl.BlockSpec((None,) * len(lead) + (tr, cols), lambda i, me_ref: (*lead, i, 0))
    return pl.pallas_call(
        body, name=name,
        grid_spec=pltpu.PrefetchScalarGridSpec(
            num_scalar_prefetch=1, grid=(rows // tr,), in_specs=[in_spec],
            out_specs=pl.BlockSpec((None, tr, cols), lambda i, me_ref: (me_ref[0], i, 0))),
        out_shape=SDS((N_DEV, rows, cols), BF16),
        compiler_params=_params(1),
    )(me_arr, src)


def _send_start(lands, pieces, after, name):
    n = len(lands)
    arrs = list(lands) + list(pieces or [])
    n_a = len(arrs)

    def body(*refs):
        land_r, piece_r = refs[:n], refs[n:n_a]
        send, recv = refs[n_a + 1:n_a + 1 + n], refs[n_a + 1 + n:n_a + 1 + 2 * n]
        token = refs[-1]
        me = _my_place()
        for i in range(n):
            for k in range(1, N_DEV):
                peer = _peer(me, k)
                src = piece_r[i].at[_index(peer)] if pieces else land_r[i].at[_index(me)]
                pltpu.make_async_remote_copy(
                    src_ref=src, dst_ref=land_r[i].at[_index(me)], send_sem=send[i], recv_sem=recv[i],
                    device_id=peer, device_id_type=MESH_ID).start()
        token[...] = jnp.zeros_like(token)

    outs = pl.pallas_call(
        body, name=name,
        out_shape=[pltpu.SemaphoreType.DMA(())] * (2 * n) + [pltpu.HBM(a.shape, a.dtype) for a in arrs]
        + [SDS((8, LANES), F32)],
        in_specs=[HBM_SPEC] * n_a + [ANY],
        out_specs=[SEM_SPEC] * (2 * n) + [HBM_SPEC] * n_a + [pl.BlockSpec(memory_space=pltpu.VMEM)],
        input_output_aliases={i: 2 * n + i for i in range(n_a)},
        compiler_params=pltpu.CompilerParams(has_side_effects=SIDE_EFFECT),
    )(*[pltpu.with_memory_space_constraint(a, pltpu.HBM) for a in arrs], after)
    return outs[:n], outs[n:2 * n], outs[2 * n:2 * n + n_a], outs[-1]


def _send_wait(thru, sends, recvs, after, name):
    n = len(sends)
    n_a = len(thru)

    def body(*refs):
        land_r = refs[:n]
        send, recv = refs[n_a:n_a + n], refs[n_a + n:n_a + 2 * n]
        me = _my_place()
        for i in range(n):
            seven = land_r[i].at[pl.ds(0, N_DEV - 1)]
            cp = pltpu.make_async_remote_copy(src_ref=seven, dst_ref=seven, send_sem=send[i], recv_sem=recv[i],
                                              device_id=_peer(me, 1), device_id_type=MESH_ID)
            cp.wait_send()
            cp.wait_recv()

    outs = pl.pallas_call(
        body, name=name,
        out_shape=[pltpu.HBM(a.shape, a.dtype) for a in thru],
        in_specs=[HBM_SPEC] * n_a + [SEM_SPEC] * (2 * n) + [ANY],
        out_specs=[HBM_SPEC] * n_a,
        input_output_aliases={i: i for i in range(n_a)},
        compiler_params=pltpu.CompilerParams(has_side_effects=SIDE_EFFECT),
    )(*thru, *sends, *recvs, after)
    return outs[:n]


def _ada_fwd(x, prm_ref):
    rstd = lax.rsqrt(jnp.mean(x * x, axis=-1, keepdims=True) + EPS)
    return (x * rstd * prm_ref[0:1, :]) * (1.0 + prm_ref[2:3, :]) + prm_ref[1:2, :]


def _ada_bwd(dh, x, prm_ref):
    ng, ops = prm_ref[0:1, :], 1.0 + prm_ref[2:3, :]
    rstd = lax.rsqrt(jnp.mean(x * x, axis=-1, keepdims=True) + EPS)
    xhat = x * rstd
    dxhat = dh * (ops * ng)
    dx = rstd * (dxhat - xhat * jnp.mean(dxhat * xhat, axis=-1, keepdims=True))
    dshift = jnp.sum(dh, axis=0, keepdims=True)
    dscale = jnp.sum(dh * (xhat * ng), axis=0, keepdims=True)
    dng = jnp.sum(dh * (xhat * ops), axis=0, keepdims=True)
    return dx, dshift, dscale, dng


def _head_mean(t, bd):
    hi, lo = _split_bf16(t)
    g = bd.shape[0]
    parts = [_dot(hi[:, i:i + g], bd) + _dot(lo[:, i:i + g], bd) for i in range(0, t.shape[1], g)]
    return jnp.concatenate(parts, axis=1) * (1.0 / HEAD_DIM)


def _shift_rows(z, before, tm):
    row = lax.broadcasted_iota(jnp.int32, z.shape, 0)
    z1 = jnp.where(row == 0, before[7:8, :], pltpu.roll(z, 1, 0))
    z2 = jnp.where(row == 0, before[6:7, :], jnp.where(row == 1, before[7:8, :], pltpu.roll(z, 2, 0)))
    return z1, z2


def _mod_part(c_all, w_ada, b_loc):
    n_layer, d_model, cols = w_ada.shape

    def body(c_ref, w_ref, b_ref, out_ref, sc_ref):
        cv = c_ref[...]
        sc = cv * jax.nn.sigmoid(cv)
        sc_ref[...] = sc
        a_hi, a_lo = _split_bf16(sc)
        w_hi, w_lo = _split_bf16(w_ref[...])
        out_ref[...] = _dot(a_hi, w_hi) + _dot(a_hi, w_lo) + _dot(a_lo, w_hi) + b_ref[...]

    return pl.pallas_call(
        body, name="mod_part", grid=(n_layer,),
        in_specs=[pl.BlockSpec((N_DEV, d_model), lambda l: (0, 0)),
                  pl.BlockSpec((None, d_model, cols), lambda l: (l, 0, 0)),
                  pl.BlockSpec((None, 1, cols), lambda l: (l, 0, 0))],
        out_specs=[pl.BlockSpec((None, N_DEV, cols), lambda l: (l, 0, 0)),
                   pl.BlockSpec((N_DEV, d_model), lambda l: (0, 0))],
        out_shape=[SDS((n_layer, N_DEV, cols), F32), SDS((N_DEV, d_model), F32)],
        compiler_params=_params(1),
    )(c_all, w_ada, b_loc)


def _w_ada_grad(sc_t, dmod_loc):
    d_model = sc_t.shape[0]
    n_layer, _, cols = dmod_loc.shape

    def body(s_ref, d_ref, out_ref):
        acc = s_ref[:, 0:1] * d_ref[0:1, :]
        for b in range(1, N_DEV):
            acc = acc + s_ref[:, b:b + 1] * d_ref[b:b + 1, :]
        out_ref[...] = acc

    return pl.pallas_call(
        body, name="w_ada_grad", grid=(n_layer,),
        in_specs=[pl.BlockSpec((d_model, N_DEV), lambda l: (0, 0)),
                  pl.BlockSpec((None, N_DEV, cols), lambda l: (l, 0, 0))],
        out_specs=pl.BlockSpec((None, d_model, cols), lambda l: (l, 0, 0)),
        out_shape=SDS((n_layer, d_model, cols), F32),
        compiler_params=_params(1),
    )(sc_t, dmod_loc)


def _sum_devices(g):
    _, rows, cols = g.shape

    def body(g_ref, out_ref):
        acc = g_ref[0]
        for i in range(1, N_DEV):
            acc = acc + g_ref[i]
        out_ref[...] = acc

    return pl.pallas_call(body, name="sum_devices", out_shape=SDS((rows, cols), F32))(g)


def _ffn_fwd(x, prm, w1r, w2g, l, sub, tm):
    t_len, d_model = x.shape
    f8, r8 = w1r.shape[-1], w2g.shape[-2]

    def body(x_ref, prm_ref, w1_ref, w2_ref, act_ref, h_ref, y_ref, xo_ref, h_sc, acc):
        j = pl.program_id(1)

        @pl.when(j == 0)
        def _():
            h = _ada_fwd(x_ref[...], prm_ref).astype(BF16)
            h_sc[...] = h
            h_ref[...] = h
            acc[...] = jnp.zeros_like(acc)

        w2 = w2_ref[...].reshape(2 * r8, d_model)
        for rows in _row_halves(tm):
            hb = h_sc[rows, :]
            g = _dot(hb, w1_ref[0])
            u = _dot(hb, w1_ref[1])
            sig = jax.nn.sigmoid(g)
            sil = g * sig
            s = (sil * u).astype(BF16)
            act_ref[0, rows, :] = (u * (sig * (1.0 + g * (1.0 - sig)))).astype(BF16)
            act_ref[1, rows, :] = sil.astype(BF16)
            act_ref[2, rows, :] = s
            acc[rows, :] += _dot(s, w2)

        @pl.when(j == N_CHUNK - 1)
        def _():
            yv = acc[...]
            y_ref[...] = yv.astype(BF16)
            xo_ref[...] = x_ref[...] + (0.5 * prm_ref[3:4, :]) * yv

    tile = pl.BlockSpec((tm, d_model), lambda m, j: (m, 0), pipeline_mode=pl.Buffered(1))
    return pl.pallas_call(
        body, name=f"ffn_fwd_{l}_{sub}", grid=(t_len // tm, N_CHUNK),
        in_specs=[tile, pl.BlockSpec((8, d_model), lambda m, j: (0, 0)),
                  pl.BlockSpec((2, None, d_model, f8), lambda m, j: (0, j, 0, 0)),
                  pl.BlockSpec((2, r8, d_model), lambda m, j: (j, 0, 0))],
        out_specs=[pl.BlockSpec((3, None, tm, f8), lambda m, j: (0, j, m, 0)), tile, tile, tile],
        out_shape=[SDS((3, N_CHUNK, t_len, f8), BF16), SDS((t_len, d_model), BF16),
                   SDS((t_len, d_model), BF16), SDS((t_len, d_model), F32)],
        scratch_shapes=[pltpu.VMEM((tm, d_model), BF16), pltpu.VMEM((tm, d_model), F32)],
        compiler_params=_params(2),
    )(x, prm, w1r, w2g)


def _ffn_bwd(dxo, y, x_in, prm, act, w1r, w2g, l, sub, tm):
    t_len, d_model = x_in.shape
    f8, r8 = w1r.shape[-1], w2g.shape[-2]
    n_m = t_len // tm

    def body(dxo_ref, y_ref, x_ref, prm_ref, act_ref, w1_ref, w2_ref,
             dx_ref, da_ref, dy_ref, st_ref, dy_sc, acc):
        m, j = pl.program_id(0), pl.program_id(1)

        @pl.when((m == 0) & (j == 0))
        def _():
            st_ref[...] = jnp.zeros_like(st_ref)

        @pl.when(j == 0)
        def _():
            dxo_v = dxo_ref[...]
            dy = ((0.5 * prm_ref[3:4, :]) * dxo_v).astype(BF16)
            dy_sc[...] = dy
            dy_ref[...] = dy
            st_ref[3:4, :] += jnp.sum(0.5 * dxo_v * y_ref[...].astype(F32), axis=0, keepdims=True)
            acc[...] = jnp.zeros_like(acc)

        w2 = w2_ref[...].reshape(2 * r8, d_model)
        for rows in _row_halves(tm):
            ds = _dot_nt(dy_sc[rows, :], w2)
            dg = (ds * act_ref[0, rows, :].astype(F32)).astype(BF16)
            du = (ds * act_ref[1, rows, :].astype(F32)).astype(BF16)
            da_ref[0, rows, :] = dg
            da_ref[1, rows, :] = du
            acc[rows, :] += _dot_nt(dg, w1_ref[0]) + _dot_nt(du, w1_ref[1])

        @pl.when(j == N_CHUNK - 1)
        def _():
            dx, dshift, dscale, dng = _ada_bwd(acc[...], x_ref[...], prm_ref)
            dx_ref[...] = dxo_ref[...] + dx
            st_ref[0:1, :] += dshift
            st_ref[1:2, :] += dscale
            st_ref[2:3, :] += dng

    tile = pl.BlockSpec((tm, d_model), lambda m, j: (m, 0), pipeline_mode=pl.Buffered(1))
    fixed = pl.BlockSpec((8, d_model), lambda m, j: (0, 0))
    return pl.pallas_call(
        body, name=f"ffn_bwd_{l}_{sub}", grid=(n_m, N_CHUNK),
        in_specs=[tile, tile, tile, fixed,
                  pl.BlockSpec((2, None, tm, f8), lambda m, j: (0, j, m, 0)),
                  pl.BlockSpec((2, None, d_model, f8), lambda m, j: (0, j, 0, 0)),
                  pl.BlockSpec((2, r8, d_model), lambda m, j: (j, 0, 0))],
        out_specs=[tile, pl.BlockSpec((2, None, tm, f8), lambda m, j: (0, j, m, 0)), tile, fixed],
        out_shape=[SDS((t_len, d_model), F32), SDS((2, N_CHUNK, t_len, f8), BF16),
                   SDS((t_len, d_model), BF16), SDS((8, d_model), F32)],
        scratch_shapes=[pltpu.VMEM((tm, d_model), BF16), pltpu.VMEM((tm, d_model), F32)],
        compiler_params=_params(2),
    )(dxo, y, x_in, prm, act, w1r, w2g)


def _matmul_tn(a, b, a_spec, b_spec, out_shape, out_spec, n_piece, t_len, tk, name, after):
    def body(a_ref, b_ref, after_ref, o_ref, acc):
        k = pl.program_id(1)

        @pl.when(k == 0)
        def _():
            acc[...] = jnp.zeros_like(acc)

        m_out, n_out = acc.shape
        if (m_out // 2) % LANES == 0:
            for rows in _row_halves(m_out):
                acc[rows, :] += _dot_tn(a_ref[:, rows], b_ref[...])
        else:
            for cols in _row_halves(n_out):
                acc[:, cols] += _dot_tn(a_ref[...], b_ref[:, cols])

        @pl.when(k == pl.num_programs(1) - 1)
        def _():
            o_ref[...] = acc[...].astype(o_ref.dtype)

    acc_shape = tuple(d for d in out_spec.block_shape if d is not None)
    return pl.pallas_call(
        body, name=name, grid=(n_piece, t_len // tk),
        in_specs=[a_spec, b_spec, ANY], out_specs=out_spec, out_shape=out_shape,
        scratch_shapes=[pltpu.VMEM(acc_shape, F32)],
        compiler_params=_params(2),
    )(a, b, after)


def _groups(tm, width):
    return pltpu.VMEM((width // LANES, tm, LANES), F32)


def _to_groups(val, dst_ref, first=0):
    for g in range(val.shape[1] // LANES):
        dst_ref[first + g] = val[:, g * LANES:(g + 1) * LANES]


def _from_groups(src_ref):
    return jnp.concatenate([src_ref[g] for g in range(src_ref.shape[0])], axis=1)


def _split_residues(src_ref, out_ref, dil, tm):
    for g in range(src_ref.shape[0]):
        for r in range(dil):
            out_ref[r, :, g * LANES:(g + 1) * LANES] = (
                src_ref.at[g][pl.ds(r, tm // dil, stride=dil), :].astype(out_ref.dtype))


def _merge_residues(in_ref, dst_ref, dil, tm, add=False):
    for g in range(dst_ref.shape[0]):
        for r in range(dil):
            rows = pl.ds(r, tm // dil, stride=dil)
            val = in_ref[r, :, g * LANES:(g + 1) * LANES].astype(F32)
            dst = dst_ref.at[g]
            dst[rows, :] = (dst[rows, :] + val) if add else val


def _res_spec(dil, tm, width, index=lambda m: m):
    return pl.BlockSpec((dil, tm // dil, width), lambda m: (0, index(m), 0))


def _mixer_in(x, prm, wing, qkg, cprm, bd, l, tm):
    t_len, d_model = x.shape
    a_w = d_model // 2
    c_w = d_model - a_w
    c8 = wing.shape[-1]
    p_c = N_DEV * c8
    dils = DILATIONS[1:]

    def body(x_ref, prm_ref, w_ref, qkg_ref, cprm_ref, bd_ref, proj_ref, qkn_ref, h_ref, ycv_ref, *rest):
        res_refs, (pf, qsc, vsc, carry) = rest[:2 * len(dils)], rest[2 * len(dils):]

        @pl.when(pl.program_id(0) == 0)
        def _():
            carry[...] = jnp.zeros_like(carry)

        h = _ada_fwd(x_ref[...], prm_ref).astype(BF16)
        h_ref[...] = h
        for j in range(N_DEV):
            pf[:, j * c8:(j + 1) * c8] = _dot(h, w_ref[j])
        proj_ref[...] = pf[...].astype(BF16)
        bdm = bd_ref[...]
        for i in range(2):
            t = pf[:, i * a_w:(i + 1) * a_w]
            rstd = lax.rsqrt(_head_mean(t * t, bdm) + EPS)
            tn = t * rstd * qkg_ref[i:i + 1, :]
            qkn_ref[:, i * a_w:(i + 1) * a_w] = tn.astype(BF16)
            _to_groups(tn, qsc, i * (a_w // LANES))
        _to_groups(pf[:, 2 * a_w:3 * a_w], vsc)
        for i, dil in enumerate(dils):
            _split_residues(qsc, res_refs[2 * i], dil, tm)
            _split_residues(vsc, res_refs[2 * i + 1], dil, tm)
        gb = pf[:, 3 * a_w:3 * a_w + c_w]
        z = pf[:, 3 * a_w + c_w:3 * a_w + 2 * c_w] * pf[:, 3 * a_w + 2 * c_w:3 * a_w + 3 * c_w]
        z1, z2 = _shift_rows(z, carry[...], tm)
        cv = cprm_ref[0:1, :] * z2 + cprm_ref[1:2, :] * z1 + cprm_ref[2:3, :] * z + cprm_ref[3:4, :]
        ycv_ref[...] = (gb * cv).astype(BF16)
        carry[...] = z[tm - 8:tm, :]

    res_specs, res_shapes = [], []
    for dil in dils:
        res_specs += [_res_spec(dil, tm, 2 * a_w), _res_spec(dil, tm, a_w)]
        res_shapes += [SDS((dil, t_len // dil, 2 * a_w), BF16), SDS((dil, t_len // dil, a_w), BF16)]
    return pl.pallas_call(
        body, name=f"mixer_in_{l}", grid=(t_len // tm,),
        in_specs=[pl.BlockSpec((tm, d_model), lambda m: (m, 0)),
                  pl.BlockSpec((8, d_model), lambda m: (0, 0)),
                  pl.BlockSpec((N_DEV, d_model, c8), lambda m: (0, 0, 0)),
                  pl.BlockSpec((8, a_w), lambda m: (0, 0)),
                  pl.BlockSpec((8, c_w), lambda m: (0, 0)),
                  pl.BlockSpec(bd.shape, lambda m: (0, 0))],
        out_specs=[pl.BlockSpec((tm, p_c), lambda m: (m, 0)),
                   pl.BlockSpec((tm, 2 * a_w), lambda m: (m, 0)),
                   pl.BlockSpec((tm, d_model), lambda m: (m, 0)),
                   pl.BlockSpec((tm, c_w), lambda m: (m, 0))] + res_specs,
        out_shape=[SDS((t_len, p_c), BF16), SDS((t_len, 2 * a_w), BF16),
                   SDS((t_len, d_model), BF16), SDS((t_len, c_w), BF16)] + res_shapes,
        scratch_shapes=[pltpu.VMEM((tm, p_c), F32), _groups(tm, 2 * a_w), _groups(tm, a_w),
                        pltpu.VMEM((8, c_w), F32)],
        compiler_params=_params(1),
    )(x, prm, wing, qkg, cprm, bd)


def _band_mask(first_block):
    qi = lax.broadcasted_iota(jnp.int32, (BLOCK, 2 * BLOCK), 0)
    kj = lax.broadcasted_iota(jnp.int32, (BLOCK, 2 * BLOCK), 1)
    lowest = jnp.where(first_block, BLOCK, 0)
    return (kj >= qi) & (kj <= qi + BLOCK) & (kj >= lowest)


def _attn_fwd(qk, v, v_col, l):
    dil, seq, a2 = qk.shape
    a_w = a2 // 2
    n_b = seq // BLOCK

    def body(q_ref, kc_ref, vc_ref, o_ref, lse_ref, kp_ref, vp_ref):
        @pl.when(pl.program_id(1) == 0)
        def _():
            kp_ref[...] = jnp.zeros_like(kp_ref)
            vp_ref[...] = jnp.zeros_like(vp_ref)

        valid = _band_mask(pl.program_id(1) == 0)
        low = lax.broadcasted_iota(jnp.int32, (BLOCK, LANES), 1) < HEAD_DIM
        n_pair = a_w // LANES
        slices = [slice(p * LANES, (p + 1) * LANES) for p in range(n_pair)]
        scores = []
        for sl in slices:
            qp = q_ref[:, sl]
            k2 = jnp.concatenate([kp_ref[:, sl], kc_ref[:, sl]], axis=0)
            scores += [_dot_nt(jnp.where(half, qp, jnp.zeros_like(qp)), k2) for half in (low, ~low)]
        probs, lses = [], []
        for s in scores:
            s = jnp.where(valid, s * SM_SCALE, NEG)
            mx = jnp.max(s, axis=-1, keepdims=True)
            e = jnp.exp(s - mx)
            den = jnp.sum(e, axis=-1, keepdims=True)
            probs.append((e * (1.0 / den)).astype(BF16))
            lses.append(mx + jnp.log(den))
        for p, sl in enumerate(slices):
            v2 = jnp.concatenate([vp_ref[:, sl], vc_ref[:, sl]], axis=0)
            o_ref[:, sl] = jnp.where(low, _dot(probs[2 * p], v2), _dot(probs[2 * p + 1], v2)).astype(BF16)
            lse_ref[:, sl] = jnp.where(low, lses[2 * p], lses[2 * p + 1])
        kp_ref[...] = kc_ref[...]
        vp_ref[...] = vc_ref[...]

    def blk(f):
        return pl.BlockSpec((None, BLOCK, a_w), f)

    return pl.pallas_call(
        body, name=f"attn_fwd_{l}_d{dil}", grid=(dil, n_b),
        in_specs=[blk(lambda r, b: (r, b, 0)),
                  blk(lambda r, b: (r, b, 1)),
                  blk(lambda r, b: (r, b, v_col))],
        out_specs=[blk(lambda r, b: (r, b, 0)), blk(lambda r, b: (r, b, 0))],
        out_shape=[SDS((dil, seq, a_w), BF16), SDS((dil, seq, a_w), F32)],
        scratch_shapes=[pltpu.VMEM((BLOCK, a_w), BF16), pltpu.VMEM((BLOCK, a_w), BF16)],
        compiler_params=_params(2),
    )(qk, qk, v)


def _attn_combine(outs, lses, ycv, tm, l):
    t_len, c_w = ycv.shape
    a_w = outs[0].shape[-1]
    dils = DILATIONS[1:]
    n_r = len(dils)

    def body(o1, l1, *rest):
        o_res, l_res, y_ref = rest[:n_r], rest[n_r:2 * n_r], rest[2 * n_r]
        cat_ref, lse_ref = rest[2 * n_r + 1], rest[2 * n_r + 2]
        ya_res, lse_res = rest[2 * n_r + 3:3 * n_r + 3], rest[3 * n_r + 3:4 * n_r + 3]
        scr = rest[4 * n_r + 3:]
        so, sl, sya, slse = scr[:n_r], scr[n_r:2 * n_r], scr[2 * n_r], scr[2 * n_r + 1]
        for i, dil in enumerate(dils):
            _merge_residues(o_res[i], so[i], dil, tm)
            _merge_residues(l_res[i], sl[i], dil, tm)
        ls = [l1[...]] + [_from_groups(s) for s in sl]
        ov = [o1[...].astype(F32)] + [_from_groups(s) for s in so]
        mx = jnp.maximum(jnp.maximum(ls[0], ls[1]), ls[2])
        es = [jnp.exp(t - mx) for t in ls]
        den = es[0] + es[1] + es[2]
        ya = (es[0] * ov[0] + es[1] * ov[1] + es[2] * ov[2]) * (1.0 / den)
        lse = mx + jnp.log(den)
        _to_groups(ya, sya)
        _to_groups(lse, slse)
        cat_ref[:, :a_w] = ya.astype(BF16)
        cat_ref[:, a_w:] = y_ref[...]
        lse_ref[...] = lse
        for i, dil in enumerate(dils):
            _split_residues(sya, ya_res[i], dil, tm)
            _split_residues(slse, lse_res[i], dil, tm)

    ta = pl.BlockSpec((tm, a_w), lambda m: (m, 0))
    res = [_res_spec(dil, tm, a_w) for dil in dils]
    return pl.pallas_call(
        body, name=f"attn_combine_{l}", grid=(t_len // tm,),
        in_specs=[ta, ta] + res + res + [pl.BlockSpec((tm, c_w), lambda m: (m, 0))],
        out_specs=[pl.BlockSpec((tm, a_w + c_w), lambda m: (m, 0)), ta] + res + res,
        out_shape=[SDS((t_len, a_w + c_w), BF16), SDS((t_len, a_w), F32)]
        + [SDS((dil, t_len // dil, a_w), BF16) for dil in dils]
        + [SDS((dil, t_len // dil, a_w), F32) for dil in dils],
        scratch_shapes=[_groups(tm, a_w)] * (2 * n_r + 2),
        compiler_params=_params(1),
    )(outs[0].reshape(t_len, a_w), lses[0].reshape(t_len, a_w), *outs[1:], *lses[1:], ycv)


def _mixer_out(cat, x, prm, woutg, l, tm):
    t_len, d_model = x.shape
    r8 = woutg.shape[-2]

    def body(cat_ref, x_ref, prm_ref, w_ref, xo_ref, y_ref):
        w = w_ref[...].reshape(N_DEV * r8, d_model)
        for rows in _row_halves(tm):
            yv = _dot(cat_ref[rows, :], w)
            y_ref[rows, :] = yv.astype(BF16)
            xo_ref[rows, :] = x_ref[rows, :] + prm_ref[3:4, :] * yv

    tile = pl.BlockSpec((tm, d_model), lambda m: (m, 0))
    return pl.pallas_call(
        body, name=f"mixer_out_{l}", grid=(t_len // tm,),
        in_specs=[tile, tile, pl.BlockSpec((8, d_model), lambda m: (0, 0)),
                  pl.BlockSpec((N_DEV, r8, d_model), lambda m: (0, 0, 0))],
        out_specs=[tile, tile],
        out_shape=[SDS((t_len, d_model), F32), SDS((t_len, d_model), BF16)],
        compiler_params=_params(1),
    )(cat, x, prm, woutg)


def _mixer_bwd_out(dxo, ymix, prm, woutg, proj, cprm, l, tm):
    t_len, d_model = dxo.shape
    a_w = d_model // 2
    c_w = d_model - a_w
    r8 = woutg.shape[-2]
    n_m = t_len // tm
    hb = tm // 8

    dils = DILATIONS[1:]

    def body(dxo_ref, ym_ref, prm_ref, w_ref, pc_ref, halo_ref, cprm_ref,
             dya_ref, dpc_ref, dy_ref, st_ref, cs_ref, *rest):
        dya_res, (carry, dsc) = rest[:len(dils)], rest[len(dils):]
        step = pl.program_id(0)
        tile_i = n_m - 1 - step

        @pl.when(step == 0)
        def _():
            st_ref[...] = jnp.zeros_like(st_ref)
            cs_ref[...] = jnp.zeros_like(cs_ref)
            carry[...] = jnp.zeros_like(carry)

        dxo_v = dxo_ref[...]
        dy = (prm_ref[3:4, :] * dxo_v).astype(BF16)
        dy_ref[...] = dy
        st_ref[3:4, :] += jnp.sum(dxo_v * ym_ref[...].astype(F32), axis=0, keepdims=True)
        w = w_ref[...].reshape(N_DEV * r8, d_model)
        dcat = jnp.concatenate([_dot_nt(dy[rows, :], w) for rows in _row_halves(tm)], axis=0)
        _to_groups(dcat[:, :a_w], dsc)
        dya_ref[...] = dcat[:, :a_w].astype(BF16)
        for i, dil in enumerate(dils):
            _split_residues(dsc, dya_res[i], dil, tm)
        dyc = dcat[:, a_w:]
        gb = pc_ref[:, :c_w].astype(F32)
        gc = pc_ref[:, c_w:2 * c_w].astype(F32)
        u = pc_ref[:, 2 * c_w:].astype(F32)
        z = gc * u
        before = halo_ref[:, c_w:2 * c_w].astype(F32) * halo_ref[:, 2 * c_w:].astype(F32)
        before = jnp.where(tile_i > 0, before, jnp.zeros_like(before))
        z1, z2 = _shift_rows(z, before, tm)
        w0, w1, w2 = cprm_ref[0:1, :], cprm_ref[1:2, :], cprm_ref[2:3, :]
        cv = w0 * z2 + w1 * z1 + w2 * z + cprm_ref[3:4, :]
        dcv = dyc * gb
        cs_ref[0:1, :] += jnp.sum(dcv * z2, axis=0, keepdims=True)
        cs_ref[1:2, :] += jnp.sum(dcv * z1, axis=0, keepdims=True)
        cs_ref[2:3, :] += jnp.sum(dcv * z, axis=0, keepdims=True)
        cs_ref[3:4, :] += jnp.sum(dcv, axis=0, keepdims=True)
        row = lax.broadcasted_iota(jnp.int32, dcv.shape, 0)
        after = carry[...]
        d1 = jnp.where(row == tm - 1, after[0:1, :], pltpu.roll(dcv, tm - 1, 0))
        d2 = jnp.where(row == tm - 2, after[0:1, :],
                       jnp.where(row == tm - 1, after[1:2, :], pltpu.roll(dcv, tm - 2, 0)))
        dz = w2 * dcv + w1 * d1 + w0 * d2
        dpc_ref[:, :c_w] = (dyc * cv).astype(BF16)
        dpc_ref[:, c_w:2 * c_w] = (dz * u).astype(BF16)
        dpc_ref[:, 2 * c_w:] = (dz * gc).astype(BF16)
        carry[...] = dcv[0:8, :]

    def rev(width, col=0):
        return pl.BlockSpec((tm, width), lambda s: (n_m - 1 - s, col))

    fixed_d = pl.BlockSpec((8, d_model), lambda s: (0, 0))
    fixed_c = pl.BlockSpec((8, c_w), lambda s: (0, 0))
    return pl.pallas_call(
        body, name=f"mixer_bwd_out_{l}", grid=(n_m,),
        in_specs=[rev(d_model), rev(d_model), fixed_d,
                  pl.BlockSpec((N_DEV, r8, d_model), lambda s: (0, 0, 0)),
                  rev(3 * c_w, 1),
                  pl.BlockSpec((8, 3 * c_w), lambda s: (jnp.maximum((n_m - 1 - s) * hb - 1, 0), 1)),
                  fixed_c],
        out_specs=[rev(a_w), rev(3 * c_w), rev(d_model), fixed_d, fixed_c]
        + [_res_spec(dil, tm, a_w, lambda s: n_m - 1 - s) for dil in dils],
        out_shape=[SDS((t_len, a_w), BF16), SDS((t_len, 3 * c_w), BF16), SDS((t_len, d_model), BF16),
                   SDS((8, d_model), F32), SDS((8, c_w), F32)]
        + [SDS((dil, t_len // dil, a_w), BF16) for dil in dils],
        scratch_shapes=[pltpu.VMEM((8, c_w), F32), _groups(tm, a_w)],
        compiler_params=_params(1),
    )(dxo, ymix, prm, woutg, proj, proj, cprm)


def _attn_bwd(qk, v, v_col, dya, ya, lse, l):
    dil, seq, a2 = qk.shape
    a_w = a2 // 2
    n_b = seq // BLOCK

    def body(q_ref, kc_ref, vc_ref, do_ref, ya_ref, lse_ref,
             dq_ref, dk_ref, dv_ref, ck, cvv, kp_ref, vp_ref):
        j = pl.program_id(1)

        @pl.when(j == 0)
        def _():
            kp_ref[...] = jnp.zeros_like(kp_ref)
            vp_ref[...] = jnp.zeros_like(vp_ref)

        @pl.when((pl.program_id(0) == 0) & (j == 0))
        def _():
            ck[...] = jnp.zeros_like(ck)
            cvv[...] = jnp.zeros_like(cvv)

        @pl.when(j < n_b)
        def _():
            valid = _band_mask(j == 0)
            low = lax.broadcasted_iota(jnp.int32, (BLOCK, LANES), 1) < HEAD_DIM
            slices = [slice(p * LANES, (p + 1) * LANES) for p in range(a_w // LANES)]
            heads, scores, dps = [], [], []
            for sl in slices:
                qp = q_ref[:, sl]
                k2 = jnp.concatenate([kp_ref[:, sl], kc_ref[:, sl]], axis=0)
                v2 = jnp.concatenate([vp_ref[:, sl], vc_ref[:, sl]], axis=0)
                dop = do_ref[:, sl]
                prod = dop.astype(F32) * ya_ref[:, sl].astype(F32)
                lsep = lse_ref[:, sl]
                for hh, half in enumerate((low, ~low)):
                    qh = jnp.where(half, qp, jnp.zeros_like(qp))
                    doh = jnp.where(half, dop, jnp.zeros_like(dop))
                    dsum = jnp.sum(jnp.where(half, prod, 0.0), axis=-1, keepdims=True)
                    heads.append((qh, doh, dsum, lsep[:, hh * HEAD_DIM:hh * HEAD_DIM + 1]))
                    scores.append(_dot_nt(qh, k2))
                    dps.append(_dot_nt(doh, v2))
            dss, prs = [], []
            for (qh, doh, dsum, lse_h), s, dp in zip(heads, scores, dps):
                pr = jnp.exp(jnp.where(valid, s * SM_SCALE, NEG) - lse_h)
                dss.append((pr * (dp - dsum) * SM_SCALE).astype(BF16))
                prs.append(pr.astype(BF16))
            for p, sl in enumerate(slices):
                k2 = jnp.concatenate([kp_ref[:, sl], kc_ref[:, sl]], axis=0)
                h0, h1 = heads[2 * p], heads[2 * p + 1]
                dq_ref[:, sl] = jnp.where(low, _dot(dss[2 * p], k2), _dot(dss[2 * p + 1], k2)).astype(BF16)
                dk2 = _dot_tn(dss[2 * p], h0[0]) + _dot_tn(dss[2 * p + 1], h1[0])
                dv2 = _dot_tn(prs[2 * p], h0[1]) + _dot_tn(prs[2 * p + 1], h1[1])
                dk_ref[:, sl] = (ck[:, sl] + dk2[:BLOCK]).astype(BF16)
                dv_ref[:, sl] = (cvv[:, sl] + dv2[:BLOCK]).astype(BF16)
                ck[:, sl] = dk2[BLOCK:]
                cvv[:, sl] = dv2[BLOCK:]
            kp_ref[...] = kc_ref[...]
            vp_ref[...] = vc_ref[...]

        @pl.when(j == n_b)
        def _():
            dk_ref[...] = ck[...].astype(BF16)
            dv_ref[...] = cvv[...].astype(BF16)

    def blk(f):
        return pl.BlockSpec((None, BLOCK, a_w), f)

    def cur(j):
        return jnp.minimum(j, n_b - 1)

    def late(j):
        return jnp.maximum(j - 1, 0)

    return pl.pallas_call(
        body, name=f"attn_bwd_{l}_d{dil}", grid=(dil, n_b + 1),
        in_specs=[blk(lambda r, j: (r, cur(j), 0)),
                  blk(lambda r, j: (r, cur(j), 1)),
                  blk(lambda r, j: (r, cur(j), v_col)),
                  blk(lambda r, j: (r, cur(j), 0)),
                  blk(lambda r, j: (r, cur(j), 0)),
                  blk(lambda r, j: (r, cur(j), 0))],
        out_specs=[blk(lambda r, j: (r, cur(j), 0)), blk(lambda r, j: (r, late(j), 0)),
                   blk(lambda r, j: (r, late(j), 0))],
        out_shape=[SDS((dil, seq, a_w), BF16)] * 3,
        scratch_shapes=[pltpu.VMEM((BLOCK, a_w), F32), pltpu.VMEM((BLOCK, a_w), F32),
                        pltpu.VMEM((BLOCK, a_w), BF16), pltpu.VMEM((BLOCK, a_w), BF16)],
        compiler_params=_params(2),
    )(qk, qk, v, dya, ya, lse)


def _mixer_bwd_in(parts, proj, dpc, dxo, x_in, prm, wing, qkg, bd, l, tm):
    t_len, d_model = x_in.shape
    a_w = d_model // 2
    c_w = d_model - a_w
    c8 = wing.shape[-1]
    p_c = N_DEV * c8

    dils = DILATIONS[1:]
    n_in = 3 * len(DILATIONS)

    def body(*refs):
        nat, res = refs[:3], refs[3:n_in]
        (pa_ref, dpc_ref, dxo_ref, x_ref, prm_ref, w_ref, qkg_ref, bd_ref,
         dx_ref, dproj_ref, st_ref, qs_ref) = refs[n_in:n_in + 12]
        sums = refs[n_in + 12:]

        @pl.when(pl.program_id(0) == 0)
        def _():
            st_ref[...] = jnp.zeros_like(st_ref)
            qs_ref[...] = jnp.zeros_like(qs_ref)

        for t in range(3):
            _to_groups(nat[t][...].astype(F32), sums[t])
            for i, dil in enumerate(dils):
                _merge_residues(res[3 * i + t], sums[t], dil, tm, add=True)
        bdm = bd_ref[...]
        grads = [_from_groups(sums[0]), _from_groups(sums[1])]
        for i in range(2):
            t = pa_ref[:, i * a_w:(i + 1) * a_w].astype(F32)
            rstd = lax.rsqrt(_head_mean(t * t, bdm) + EPS)
            that = t * rstd
            qs_ref[i:i + 1, :] += jnp.sum(grads[i] * that, axis=0, keepdims=True)
            dhat = grads[i] * qkg_ref[i:i + 1, :]
            dt = rstd * (dhat - that * _head_mean(dhat * that, bdm))
            dproj_ref[:, i * a_w:(i + 1) * a_w] = dt.astype(BF16)
        dproj_ref[:, 2 * a_w:3 * a_w] = _from_groups(sums[2]).astype(BF16)
        dproj_ref[:, 3 * a_w:] = dpc_ref[...]
        dh = _dot_nt(dproj_ref[:, 0:c8], w_ref[0])
        for j in range(1, N_DEV):
            dh = dh + _dot_nt(dproj_ref[:, j * c8:(j + 1) * c8], w_ref[j])
        dx, dshift, dscale, dng = _ada_bwd(dh, x_ref[...], prm_ref)
        dx_ref[...] = dxo_ref[...] + dx
        st_ref[0:1, :] += dshift
        st_ref[1:2, :] += dscale
        st_ref[2:3, :] += dng

    ta = pl.BlockSpec((tm, a_w), lambda m: (m, 0))
    tile = pl.BlockSpec((tm, d_model), lambda m: (m, 0))
    return pl.pallas_call(
        body, name=f"mixer_bwd_in_{l}", grid=(t_len // tm,),
        in_specs=[ta] * 3 + [_res_spec(dil, tm, a_w) for dil in dils for _ in range(3)]
                 + [pl.BlockSpec((tm, 3 * a_w), lambda m: (m, 0)),
                             pl.BlockSpec((tm, 3 * c_w), lambda m: (m, 0)),
                             tile, tile, pl.BlockSpec((8, d_model), lambda m: (0, 0)),
                             pl.BlockSpec((N_DEV, d_model, c8), lambda m: (0, 0, 0)),
                             pl.BlockSpec((8, a_w), lambda m: (0, 0)),
                             pl.BlockSpec(bd.shape, lambda m: (0, 0))],
        out_specs=[tile, pl.BlockSpec((tm, p_c), lambda m: (m, 0)),
                   pl.BlockSpec((8, d_model), lambda m: (0, 0)), pl.BlockSpec((8, a_w), lambda m: (0, 0))],
        out_shape=[SDS((t_len, d_model), F32), SDS((t_len, p_c), BF16), SDS((8, d_model), F32), SDS((8, a_w), F32)],
        scratch_shapes=[_groups(tm, a_w)] * 3,
        compiler_params=_params(1),
    )(*[t.reshape(t_len, a_w) for t in parts[0]], *[t for p in parts[1:] for t in p],
      proj, dpc, dxo, x_in, prm, wing, qkg, bd)


def _loss_head(x, target, tm):
    t_len, d_model = x.shape

    def body(x_ref, t_ref, dx_ref, loss_ref):
        @pl.when(pl.program_id(0) == 0)
        def _():
            loss_ref[...] = jnp.zeros_like(loss_ref)

        diff = x_ref[...] - t_ref[...]
        dx_ref[...] = diff * (1.0 / d_model)
        per_token = jnp.sum(diff * diff, axis=-1, keepdims=True) * (1.0 / d_model)
        loss_ref[...] += 0.5 * jnp.sum(per_token)

    tile = pl.BlockSpec((tm, d_model), lambda m: (m, 0))
    return pl.pallas_call(
        body, name="loss_head", grid=(t_len // tm,),
        in_specs=[tile, tile],
        out_specs=[tile, pl.BlockSpec((8, LANES), lambda m: (0, 0))],
        out_shape=[SDS((t_len, d_model), F32), SDS((8, LANES), F32)],
        compiler_params=_params(1),
    )(x, target)


def _row_tile(rows, cols):
    best = None
    for t in range(16, rows + 1, 16):
        if rows % t == 0 and t * cols * 4 <= (1 << 20):
            best = t
    return best if best is not None else rows


def _adamw(pieces, w, m, v, name):
    n_q = len(pieces)
    n_p, rows, cols = pieces[0].shape
    tr = _row_tile(rows, cols)
    n_i = rows // tr
    c1 = 1.0 - ADAM_B1 ** ADAM_STEP
    c2 = 1.0 - ADAM_B2 ** ADAM_STEP

    def body(*refs):
        p_refs = refs[:n_q]
        w_ref, m_ref, v_ref, g_ref, d_ref, nm_ref, nv_ref = refs[n_q:]
        for q in range(n_q):
            @pl.when(pl.program_id(0) == q)
            def _():
                g = p_refs[q][0].astype(F32)
                for i in range(1, n_p):
                    g = g + p_refs[q][i].astype(F32)
                g_ref[...] = g
                nm = ADAM_B1 * m_ref[...] + (1.0 - ADAM_B1) * g
                nv = ADAM_B2 * v_ref[...] + (1.0 - ADAM_B2) * (g * g)
                nm_ref[...] = nm
                nv_ref[...] = nv
                d_ref[...] = -ADAM_LR * ((nm / c1) / (jnp.sqrt(nv / c2) + ADAM_EPS) + ADAM_WD * w_ref[...])

    def piece_spec(q):
        return pl.BlockSpec((n_p, tr, cols), lambda a, i: (0, jnp.where(a == q, i, 0), 0))

    tile = pl.BlockSpec((tr, cols), lambda a, i: (a * n_i + i, 0))
    return pl.pallas_call(
        body, name=name, grid=(n_q, n_i),
        in_specs=[piece_spec(q) for q in range(n_q)] + [tile, tile, tile],
        out_specs=[tile] * 4,
        out_shape=[SDS((n_q * rows, cols), F32)] * 4,
        compiler_params=_params(2),
    )(*pieces, w, m, v)


def _pack(vecs):
    flat = jnp.concatenate([v.reshape(-1).astype(F32) for v in vecs])
    rows = -(-flat.shape[0] // (8 * LANES)) * 8
    return jnp.pad(flat, (0, rows * LANES - flat.shape[0])).reshape(rows, LANES)


def _unpack(packed, shapes):
    lead = packed.shape[:-2]
    flat = packed.reshape(lead + (-1,))
    out, off = [], 0
    for s in shapes:
        size = 1
        for d in s:
            size *= d
        out.append(flat[..., off:off + size].reshape(lead + tuple(s)))
        off += size
    return out


def kernel(x, c, w_ada, b_ada, norm_g, w_in, q_norm_g, k_norm_g, conv_w, conv_b, w_out, ffn_w1, ffn_w2, loss_target, m_w_ada, m_b_ada, m_norm_g, m_w_in, m_q_norm_g, m_k_norm_g, m_conv_w, m_conv_b, m_w_out, m_ffn_w1, m_ffn_w2, v_w_ada, v_b_ada, v_norm_g, v_w_in, v_q_norm_g, v_k_norm_g, v_conv_w, v_conv_b, v_w_out, v_ffn_w1, v_ffn_w2):
    _, t_len, d_model = x.shape
    n_layer = w_ada.shape[0]
    a_w = d_model // 2
    c_w = d_model - a_w
    n_head = a_w // HEAD_DIM
    ada_cols = w_ada.shape[-1]
    tm = min(TOKEN_TILE, t_len)
    tm_ffn = min(FFN_TOKEN_TILE, t_len)
    me = _index(_my_place())
    x2 = x.reshape(t_len, d_model)
    target = loss_target.reshape(t_len, d_model)

    small_shapes = [(d_model,), norm_g.shape, conv_w.shape]
    gathered = _allgather_small(_pack([c, norm_g, conv_w]), "gather_small_inputs")
    c_all, ng_parts, cw_parts = _unpack(gathered, small_shapes)
    norm_g_full = jnp.moveaxis(ng_parts, 0, 2).reshape(n_layer, 3, d_model)
    conv_w_full = jnp.moveaxis(cw_parts, 0, 2).reshape(n_layer, 3, c_w)

    b_loc = lax.dynamic_slice_in_dim(b_ada, me * ada_cols, ada_cols, axis=1).reshape(n_layer, 1, ada_cols)
    mod_part, silu_c = _mod_part(c_all, w_ada, b_loc)
    mod_all = _allgather_small(_pack([mod_part]), "gather_mod")
    mod_all = _unpack(mod_all, [mod_part.shape])[0]
    mod_mine = lax.dynamic_index_in_dim(mod_all, me, axis=2, keepdims=False)
    mod = jnp.moveaxis(mod_mine, 0, 1).reshape(n_layer, 3, 3, d_model)

    def prm_of(l, sub):
        rows = jnp.stack([norm_g_full[l, sub], mod[l, sub, 0], mod[l, sub, 1], mod[l, sub, 2]])
        return jnp.pad(rows, ((0, 4), (0, 0)))

    prm = [[prm_of(l, sub) for sub in range(3)] for l in range(n_layer)]
    qkg = [jnp.pad(jnp.stack([jnp.tile(q_norm_g[l], n_head), jnp.tile(k_norm_g[l], n_head)]), ((0, 6), (0, 0)))
           for l in range(n_layer)]
    cprm = [jnp.pad(jnp.concatenate([conv_w_full[l], conv_b[l][None]]), ((0, 4), (0, 0))) for l in range(n_layer)]
    head_of = jnp.arange(min(MXU_WIDTH, a_w)) // HEAD_DIM
    bd = (head_of[:, None] == head_of[None, :]).astype(BF16)

    f8 = ffn_w1.shape[-1]
    r8 = ffn_w2.shape[-2]
    c8 = w_in.shape[-1]
    o8 = w_out.shape[-2]
    me_arr = jnp.reshape(me, (1,)).astype(jnp.int32)
    win_b = [_place_own(w_in, (l,), me_arr, f"own_w_in_{l}") for l in range(n_layer)]
    wout_b = [_place_own(w_out, (l,), me_arr, f"own_w_out_{l}") for l in range(n_layer)]
    w1_b = [[_place_own(ffn_w1, (l, s), me_arr, f"own_w1_{l}_{s}") for s in range(2)] for l in range(n_layer)]
    w2_b = [[_place_own(ffn_w2, (l, s), me_arr, f"own_w2_{l}_{s}") for s in range(2)] for l in range(n_layer)]
    w1_first, w2_first = _allgather_big([w1_b[0][0], w2_b[0][0]], "gather_first_weights")
    groups = [[win_b[0], wout_b[0]], [w1_b[0][1], w2_b[0][1]]]
    for l in range(1, n_layer):
        groups += [[w1_b[l][0], w2_b[l][0]], [win_b[l], wout_b[l]], [w1_b[l][1], w2_b[l][1]]]
    flat = [a for grp in groups for a in grp]
    w_sends, w_recvs, w_lands, w_token = _send_start(flat, None, w1_first, "weights_start")

    def gathered(gi, after):
        lo = sum(len(grp) for grp in groups[:gi])
        hi = lo + len(groups[gi])
        return _send_wait(w_lands[lo:hi], w_sends[lo:hi], w_recvs[lo:hi], after, f"weights_wait_{gi}")

    def chunked(w1g):
        return w1g.reshape(2, N_CHUNK, d_model, f8)

    saved = []
    xc = x2
    for l in range(n_layer):
        keep = {"x0": xc}
        if l == 0:
            w1g, w2g = w1_first, w2_first
            prm_first = prm[0][0] + w_token[0, 0]
        else:
            w1g, w2g = gathered(3 * l - 1, xc)
            prm_first = prm[l][0]
        keep["w_ffn0"] = (chunked(w1g), w2g)
        keep["act0"], keep["h0"], keep["y0"], xc = _ffn_fwd(xc, prm_first, *keep["w_ffn0"], l, 0, tm_ffn)
        keep["x1"] = xc
        wing, woutg = gathered(0 if l == 0 else 3 * l, xc)
        keep.update(wing=wing, woutg=woutg)
        proj, qkn, keep["h1"], ycv, *res = _mixer_in(xc, prm[l][1], wing, qkg[l], cprm[l], bd, l, tm)
        qkv = [(qkn[None], proj[None], 2)] + [(res[2 * i], res[2 * i + 1], 0) for i in range(len(DILATIONS) - 1)]
        branches = [_attn_fwd(*t, l) for t in qkv]
        cat, lse, *res = _attn_combine([b[0] for b in branches], [b[1] for b in branches], ycv, tm, l)
        n_r = len(DILATIONS) - 1
        keep["ya"] = [cat[None]] + res[:n_r]
        keep["lse"] = [lse[None]] + res[n_r:]
        xc, keep["ymix"] = _mixer_out(cat, xc, prm[l][1], woutg, l, tm)
        keep.update(proj=proj, qkv=qkv, cat=cat)
        keep["x2"] = xc
        w1g, w2g = gathered(1 if l == 0 else 3 * l + 1, xc)
        keep["w_ffn2"] = (chunked(w1g), w2g)
        keep["act2"], keep["h2"], keep["y2"], xc = _ffn_fwd(xc, prm[l][2], *keep["w_ffn2"], l, 1, tm_ffn)
        saved.append(keep)

    dx, loss_blk = _loss_head(xc, target, tm)
    loss = lax.psum(loss_blk[0, 0], ("x", "y", "c"))

    tk = min(GRAD_TOKEN_TILE, t_len)
    stats =[[None] * 3 for _ in range(n_layer)]
    conv_stats = [None] * n_layer
    qk_stats = [None] * n_layer

    started = {}
    dummy = jnp.zeros((8, LANES), F32)

    def start_exchange(g, name):
        sends, recvs, thru, token = _send_start(
            [_place_own(g, None, me_arr, f"own_grad_{name}")], [g], dummy, f"grads_start_{name}")
        started[name] = (thru[0], thru[1], sends[0], recvs[0])
        return token

    def wait_exchange(names, after, name):
        ent = [started[k] for k in names]
        done = _send_wait([e[0] for e in ent] + [e[1] for e in ent], [e[2] for e in ent], [e[3] for e in ent],
                          after, name)
        return dict(zip(names, done))

    def ffn_backward(dxo, keep, l, sub, which, prm_dep):
        dx_in, da, dy, st = _ffn_bwd(dxo, keep[f"y{which}"], keep[f"x{which}"], prm_dep,
                                     keep[f"act{which}"], *keep[f"w_ffn{which}"], l, sub, tm_ffn)
        g_w2 = _matmul_tn(
            keep[f"act{which}"], dy,
            pl.BlockSpec((None, None, tk, f8), lambda p, k: (2, p, k, 0)),
            pl.BlockSpec((tk, d_model), lambda p, k: (k, 0)),
            SDS((N_CHUNK, f8, d_model), BF16), pl.BlockSpec((None, f8, d_model), lambda p, k: (p, 0, 0)),
            N_CHUNK, t_len, tk, f"grad_w2_{l}_{sub}", dummy)
        token = start_exchange(g_w2.reshape(N_DEV, r8, d_model), f"w2_{l}_{sub}")
        g_w1 = _matmul_tn(
            keep[f"h{which}"], da.reshape(N_DEV, t_len, f8),
            pl.BlockSpec((tk, d_model), lambda p, k: (k, 0)),
            pl.BlockSpec((None, tk, f8), lambda p, k: (p, k, 0)),
            SDS((N_DEV, d_model, f8), BF16), pl.BlockSpec((None, d_model, f8), lambda p, k: (p, 0, 0)),
            N_DEV, t_len, tk, f"grad_w1_{l}_{sub}", token)
        token = start_exchange(g_w1, f"w1_{l}_{sub}")
        return dx_in, st, token[0, 0]

    token = 0.0
    for l in reversed(range(n_layer)):
        keep = saved[l]
        wing, woutg = keep["wing"], keep["woutg"]
        dx, stats[l][2], token = ffn_backward(dx, keep, l, 1, 2, prm[l][2] + token)
        dya, dpc, dy, st_gate, conv_stats[l], *dya_res = _mixer_bwd_out(
            dx, keep["ymix"], prm[l][1] + token, woutg, keep["proj"], cprm[l], l, tm)
        dyas = [dya[None]] + dya_res
        g_wout = _matmul_tn(
            keep["cat"], dy,
            pl.BlockSpec((tk, d_model), lambda p, k: (k, 0)),
            pl.BlockSpec((tk, d_model), lambda p, k: (k, 0)),
            SDS((d_model, d_model), BF16), pl.BlockSpec((d_model, d_model), lambda p, k: (0, 0)),
            1, t_len, tk, f"grad_wout_{l}", dummy)
        token = start_exchange(g_wout.reshape(N_DEV, o8, d_model), f"wout_{l}")[0, 0]
        parts = [_attn_bwd(*keep["qkv"][i], dyas[i], keep["ya"][i], keep["lse"][i], l)
                 for i in range(len(DILATIONS))]
        dx, dproj, st_norm, qk_stats[l] = _mixer_bwd_in(
            parts, keep["proj"], dpc, dx, keep["x1"], prm[l][1] + token, wing, qkg[l], bd, l, tm)
        stats[l][1] = st_norm + st_gate
        g_win = _matmul_tn(
            keep["h1"], dproj,
            pl.BlockSpec((tk, d_model), lambda p, k: (k, 0)),
            pl.BlockSpec((tk, c8), lambda p, k: (k, p)),
            SDS((N_DEV, d_model, c8), BF16), pl.BlockSpec((None, d_model, c8), lambda p, k: (p, 0, 0)),
            N_DEV, t_len, tk, f"grad_win_{l}", dummy)
        token = start_exchange(g_win, f"win_{l}")[0, 0]
        dx, stats[l][0], token = ffn_backward(dx, keep, l, 0, 0, prm[l][0] + token)
    grad_x = dx.reshape(x.shape)

    def update(pieces, w, m, v, name):
        shape = w.shape
        cols = shape[-1]
        outs = _adamw([p.reshape(p.shape[0], -1, cols) for p in pieces], w.reshape(-1, cols),
                      m.reshape(-1, cols), v.reshape(-1, cols), name)
        return [o.reshape(shape) for o in outs]

    last = "w1_0_0"
    lands = wait_exchange([k for k in started if k != last], dx, "grads_wait_early")
    res = {
        "w_in": update([lands[f"win_{l}"] for l in range(n_layer)], w_in, m_w_in, v_w_in, "adamw_w_in"),
        "w_out": update([lands[f"wout_{l}"] for l in range(n_layer)], w_out, m_w_out, v_w_out, "adamw_w_out"),
        "ffn_w2": update([lands[f"w2_{l}_{s}"] for l in range(n_layer) for s in range(2)],
                         ffn_w2, m_ffn_w2, v_ffn_w2, "adamw_ffn_w2"),
    }

    dmod = jnp.stack([jnp.stack([stats[l][sub][r] for sub in range(3) for r in (0, 1, 3)]) for l in range(n_layer)])
    dng = jnp.stack([jnp.stack([stats[l][sub][2] for sub in range(3)]) for l in range(n_layer)])
    dcw = jnp.stack([conv_stats[l][0:3] for l in range(n_layer)])
    dcb = jnp.stack([conv_stats[l][3] for l in range(n_layer)])
    dqk = jnp.stack([qk_stats[l][0:2].reshape(2, n_head, HEAD_DIM).sum(axis=1) for l in range(n_layer)])
    part_shapes = [dmod.shape, dng.shape, dcw.shape, dcb.shape, dqk.shape]
    parts_all = _allgather_small(_pack([dmod, dng, dcw, dcb, dqk]), "gather_small_grads")
    dmod_all = _unpack(parts_all, part_shapes)[0].reshape(N_DEV, n_layer, 9 * d_model)
    summed = _unpack(_sum_devices(parts_all), part_shapes)
    g_b_ada = summed[0].reshape(n_layer, 9 * d_model)
    ng_cols = norm_g.shape[-1]
    g_norm_g = lax.dynamic_slice_in_dim(summed[1], me * ng_cols, ng_cols, axis=2)
    cw_cols = conv_w.shape[-1]
    g_conv_w = lax.dynamic_slice_in_dim(summed[2], me * cw_cols, cw_cols, axis=2)
    g_conv_b = summed[3]
    g_q, g_k = summed[4][:, 0], summed[4][:, 1]
    dmod_loc = lax.dynamic_slice_in_dim(dmod_all, me * ada_cols, ada_cols, axis=2)
    g_w_ada = _w_ada_grad(silu_c.T, jnp.moveaxis(dmod_loc, 0, 1))
    res["w_ada"] = update([g_w_ada[None]], w_ada, m_w_ada, v_w_ada, "adamw_w_ada")

    names = ["b_ada", "norm_g", "q_norm_g", "k_norm_g", "conv_w", "conv_b"]
    g_small = [g_b_ada, g_norm_g, g_q, g_k, g_conv_w, g_conv_b]
    w_small = [b_ada, norm_g, q_norm_g, k_norm_g, conv_w, conv_b]
    m_small = [m_b_ada, m_norm_g, m_q_norm_g, m_k_norm_g, m_conv_w, m_conv_b]
    v_small = [v_b_ada, v_norm_g, v_q_norm_g, v_k_norm_g, v_conv_w, v_conv_b]
    outs = _adamw([_pack(g_small)[None]], _pack(w_small), _pack(m_small), _pack(v_small), "adamw_small")
    shapes = [w.shape for w in w_small]
    unpacked = [_unpack(o, shapes) for o in outs]
    for i, nme in enumerate(names):
        res[nme] = [unpacked[k][i] for k in range(4)]

    follows = (res["w_in"][1][0, :1, :1] + res["w_out"][1][0, :1, :1] + res["ffn_w2"][1][0, 0, :1, :1]
               + res["w_ada"][1][0, :1, :1] + outs[1][:1, :1])
    lands.update(wait_exchange([last], follows, "grads_wait_last"))
    res["ffn_w1"] = update([lands[f"w1_{l}_{s}"] for l in range(n_layer) for s in range(2)],
                           ffn_w1, m_ffn_w1, v_ffn_w1, "adamw_ffn_w1")

    order = ["w_ada", "b_ada", "norm_g", "w_in", "q_norm_g", "k_norm_g", "conv_w", "conv_b", "w_out", "ffn_w1", "ffn_w2"]
    return (loss, grad_x, *[res[n][0] for n in order], *[res[n][1] for n in order],
            *[res[n][2] for n in order], *[res[n][3] for n in order])
```

```python
import jax
import jax.numpy as jnp
from jax import lax
from jax.experimental import pallas as pl
from jax.experimental.pallas import tpu as pltpu

F32 = jnp.float32
BF16 = jnp.bfloat16
SDS = jax.ShapeDtypeStruct

N_DEV = 8
HEAD_DIM = 64
BLOCK = 128
DILATIONS = (1, 4, 16)
ATTN_BLOCKS_PER_STEP = 4
N_CHUNK = 4
EPS = 1e-6
NEG = -1e30
SM_SCALE = HEAD_DIM ** -0.5
LANES = 128
MXU_WIDTH = 256
TOKEN_TILE = 512
FFN_TOKEN_TILE = 512
GRAD_TOKEN_TILE = 2048
VMEM_LIMIT_BYTES = 56 * 1024 * 1024

ADAM_LR = 0.001
ADAM_B1 = 0.9
ADAM_B2 = 0.999
ADAM_EPS = 1e-08
ADAM_WD = 0.01
ADAM_STEP = 10

MESH_ID = pl.DeviceIdType.MESH
ANY = pl.BlockSpec(memory_space=pl.ANY)
HBM_SPEC = pl.BlockSpec(memory_space=pltpu.HBM)
SEM_SPEC = pl.BlockSpec(memory_space=pltpu.SEMAPHORE)
SIDE_EFFECT = pltpu.SideEffectType.DATAFLOW_SIDE_EFFECTING


def _params(n_axes):
    return pltpu.CompilerParams(dimension_semantics=("arbitrary",) * n_axes,
                                vmem_limit_bytes=VMEM_LIMIT_BYTES)


def _dot(a, b):
    return jnp.dot(a, b, preferred_element_type=F32)


def _dot_nt(a, b):
    return lax.dot_general(a, b, (((1,), (1,)), ((), ())), preferred_element_type=F32)


def _dot_tn(a, b):
    return lax.dot_general(a, b, (((0,), (0,)), ((), ())), preferred_element_type=F32)


def _row_halves(n):
    return (slice(0, n // 2), slice(n // 2, n))


def _split_bf16(t):
    hi = t.astype(BF16)
    return hi, (t - hi.astype(F32)).astype(BF16)


def _my_place():
    x, y, c = lax.axis_index("x"), lax.axis_index("y"), lax.axis_index("c")
    return x, y, c


def _peer(place, k):
    x, y, c = place
    return ((1 - x) if k & 4 else x, (1 - y) if k & 2 else y, (1 - c) if k & 1 else c)


def _index(place):
    return 4 * place[0] + 2 * place[1] + place[2]


def _allgather_small(v, name):
    rows, cols = v.shape

    def body(x_ref, out_ref, send_sems, recv_sems, local_sem):
        me = _my_place()
        mine = pltpu.make_async_copy(x_ref, out_ref.at[_index(me)], local_sem)
        mine.start()
        sends = []
        for k in range(1, N_DEV):
            cp = pltpu.make_async_remote_copy(
                src_ref=x_ref, dst_ref=out_ref.at[_index(me)], send_sem=send_sems.at[k - 1],
                recv_sem=recv_sems.at[k - 1], device_id=_peer(me, k), device_id_type=MESH_ID)
            cp.start()
            sends.append(cp)
        for k in range(1, N_DEV):
            pltpu.make_async_remote_copy(
                src_ref=x_ref, dst_ref=out_ref.at[_index(_peer(me, k))], send_sem=send_sems.at[k - 1],
                recv_sem=recv_sems.at[k - 1], device_id=_peer(me, k), device_id_type=MESH_ID).wait_recv()
        for cp in sends:
            cp.wait_send()
        mine.wait()

    return pl.pallas_call(
        body, name=name,
        out_shape=SDS((N_DEV, rows, cols), v.dtype),
        in_specs=[pl.BlockSpec(memory_space=pltpu.VMEM)],
        out_specs=pl.BlockSpec(memory_space=pltpu.VMEM),
        scratch_shapes=[pltpu.SemaphoreType.DMA((N_DEV - 1,)), pltpu.SemaphoreType.DMA((N_DEV - 1,)),
                        pltpu.SemaphoreType.DMA],
    )(v)


def _allgather_big(lands, name):
    n = len(lands)

    def body(*refs):
        outs = refs[n:2 * n]
        send_sems, recv_sems = refs[2 * n:]
        x, y, c = _my_place()
        me, sib = (x, y, c), (x, y, 1 - c)
        chips = [(1 - x, y), (x, 1 - y), (1 - x, 1 - y)]

        def copy(i, k, block, to):
            dst = outs[i].at[_index(block)]
            return pltpu.make_async_remote_copy(
                src_ref=dst, dst_ref=dst, send_sem=send_sems.at[7 * i + k],
                recv_sem=recv_sems.at[7 * i + k], device_id=to, device_id_type=MESH_ID)

        first = []
        for i in range(n):
            first.append(copy(i, 0, me, sib))
            for j, chip in enumerate(chips):
                first.append(copy(i, 1 + j, me, (*chip, c)))
        for cp in first:
            cp.start()
        passed = []
        for i in range(n):
            for j, chip in enumerate(chips):
                copy(i, 1 + j, (*chip, c), me).wait_recv()
                fwd = copy(i, 4 + j, (*chip, c), sib)
                fwd.start()
                passed.append(fwd)
        for i in range(n):
            copy(i, 0, sib, me).wait_recv()
            for j, chip in enumerate(chips):
                copy(i, 4 + j, (*chip, 1 - c), me).wait_recv()
        for cp in first + passed:
            cp.wait_send()

    return pl.pallas_call(
        body, name=name,
        out_shape=[SDS(a.shape, a.dtype) for a in lands],
        in_specs=[ANY] * n, out_specs=[ANY] * n,
        input_output_aliases={i: i for i in range(n)},
        scratch_shapes=[pltpu.SemaphoreType.DMA((7 * n,)), pltpu.SemaphoreType.DMA((7 * n,))],
    )(*lands)


def _place_own(src, lead, me_arr, name):
    rows, cols = src.shape[-2:]
    tr = _row_tile(rows, cols)

    def body(me_ref, s_ref, o_ref):
        o_ref[...] = s_ref[...].astype(BF16)

    if lead is None:
        in_spec = pl.BlockSpec((None, tr, cols), lambda i, me_ref: (me_ref[0], i, 0))
    else:
        in_spec = pl.BlockSpec((None,) * len(lead) + (tr, cols), lambda i, me_ref: (*lead, i, 0))
    return pl.pallas_call(
        body, name=name,
        grid_spec=pltpu.PrefetchScalarGridSpec(
            num_scalar_prefetch=1, grid=(rows // tr,), in_specs=[in_spec],
            out_specs=pl.BlockSpec((None, tr, cols), lambda i, me_ref: (me_ref[0], i, 0))),
        out_shape=SDS((N_DEV, rows, cols), BF16),
        compiler_params=_params(1),
    )(me_arr, src)


def _send_start(lands, pieces, after, name):
    n = len(lands)
    arrs = list(lands) + list(pieces or [])
    n_a = len(arrs)

    def body(*refs):
        land_r, piece_r = refs[:n], refs[n:n_a]
        send, recv = refs[n_a + 1:n_a + 1 + n], refs[n_a + 1 + n:n_a + 1 + 2 * n]
        token = refs[-1]
        me = _my_place()
        for i in range(n):
            for k in range(1, N_DEV):
                peer = _peer(me, k)
                src = piece_r[i].at[_index(peer)] if pieces else land_r[i].at[_index(me)]
                pltpu.make_async_remote_copy(
                    src_ref=src, dst_ref=land_r[i].at[_index(me)], send_sem=send[i], recv_sem=recv[i],
                    device_id=peer, device_id_type=MESH_ID).start()
        token[...] = jnp.zeros_like(token)

    outs = pl.pallas_call(
        body, name=name,
        out_shape=[pltpu.SemaphoreType.DMA(())] * (2 * n) + [pltpu.HBM(a.shape, a.dtype) for a in arrs]
        + [SDS((8, LANES), F32)],
        in_specs=[HBM_SPEC] * n_a + [ANY],
        out_specs=[SEM_SPEC] * (2 * n) + [HBM_SPEC] * n_a + [pl.BlockSpec(memory_space=pltpu.VMEM)],
        input_output_aliases={i: 2 * n + i for i in range(n_a)},
        compiler_params=pltpu.CompilerParams(has_side_effects=SIDE_EFFECT),
    )(*[pltpu.with_memory_space_constraint(a, pltpu.HBM) for a in arrs], after)
    return outs[:n], outs[n:2 * n], outs[2 * n:2 * n + n_a], outs[-1]


def _send_wait(thru, sends, recvs, after, name):
    n = len(sends)
    n_a = len(thru)

    def body(*refs):
        land_r = refs[:n]
        send, recv = refs[n_a:n_a + n], refs[n_a + n:n_a + 2 * n]
        me = _my_place()
        for i in range(n):
            seven = land_r[i].at[pl.ds(0, N_DEV - 1)]
            cp = pltpu.make_async_remote_copy(src_ref=seven, dst_ref=seven, send_sem=send[i], recv_sem=recv[i],
                                              device_id=_peer(me, 1), device_id_type=MESH_ID)
            cp.wait_send()
            cp.wait_recv()

    outs = pl.pallas_call(
        body, name=name,
        out_shape=[pltpu.HBM(a.shape, a.dtype) for a in thru],
        in_specs=[HBM_SPEC] * n_a + [SEM_SPEC] * (2 * n) + [ANY],
        out_specs=[HBM_SPEC] * n_a,
        input_output_aliases={i: i for i in range(n_a)},
        compiler_params=pltpu.CompilerParams(has_side_effects=SIDE_EFFECT),
    )(*thru, *sends, *recvs, after)
    return outs[:n]


def _ada_fwd(x, prm_ref):
    rstd = lax.rsqrt(jnp.mean(x * x, axis=-1, keepdims=True) + EPS)
    return (x * rstd * prm_ref[0:1, :]) * (1.0 + prm_ref[2:3, :]) + prm_ref[1:2, :]


def _ada_bwd(dh, x, prm_ref):
    ng, ops = prm_ref[0:1, :], 1.0 + prm_ref[2:3, :]
    rstd = lax.rsqrt(jnp.mean(x * x, axis=-1, keepdims=True) + EPS)
    xhat = x * rstd
    dxhat = dh * (ops * ng)
    dx = rstd * (dxhat - xhat * jnp.mean(dxhat * xhat, axis=-1, keepdims=True))
    dshift = jnp.sum(dh, axis=0, keepdims=True)
    dscale = jnp.sum(dh * (xhat * ng), axis=0, keepdims=True)
    dng = jnp.sum(dh * (xhat * ops), axis=0, keepdims=True)
    return dx, dshift, dscale, dng


def _head_mean(t, bd):
    hi, lo = _split_bf16(t)
    g = bd.shape[0]
    parts = [_dot(hi[:, i:i + g], bd) + _dot(lo[:, i:i + g], bd) for i in range(0, t.shape[1], g)]
    return jnp.concatenate(parts, axis=1) * (1.0 / HEAD_DIM)


def _shift_rows(z, before, tm):
    row = lax.broadcasted_iota(jnp.int32, z.shape, 0)
    z1 = jnp.where(row == 0, before[7:8, :], pltpu.roll(z, 1, 0))
    z2 = jnp.where(row == 0, before[6:7, :], jnp.where(row == 1, before[7:8, :], pltpu.roll(z, 2, 0)))
    return z1, z2


def _mod_part(c_all, w_ada, b_loc):
    n_layer, d_model, cols = w_ada.shape

    def body(c_ref, w_ref, b_ref, out_ref, sc_ref):
        cv = c_ref[...]
        sc = cv * jax.nn.sigmoid(cv)
        sc_ref[...] = sc
        a_hi, a_lo = _split_bf16(sc)
        w_hi, w_lo = _split_bf16(w_ref[...])
        out_ref[...] = _dot(a_hi, w_hi) + _dot(a_hi, w_lo) + _dot(a_lo, w_hi) + b_ref[...]

    return pl.pallas_call(
        body, name="mod_part", grid=(n_layer,),
        in_specs=[pl.BlockSpec((N_DEV, d_model), lambda l: (0, 0)),
                  pl.BlockSpec((None, d_model, cols), lambda l: (l, 0, 0)),
                  pl.BlockSpec((None, 1, cols), lambda l: (l, 0, 0))],
        out_specs=[pl.BlockSpec((None, N_DEV, cols), lambda l: (l, 0, 0)),
                   pl.BlockSpec((N_DEV, d_model), lambda l: (0, 0))],
        out_shape=[SDS((n_layer, N_DEV, cols), F32), SDS((N_DEV, d_model), F32)],
        compiler_params=_params(1),
    )(c_all, w_ada, b_loc)


def _w_ada_grad(sc_t, dmod_loc):
    d_model = sc_t.shape[0]
    n_layer, _, cols = dmod_loc.shape

    def body(s_ref, d_ref, out_ref):
        acc = s_ref[:, 0:1] * d_ref[0:1, :]
        for b in range(1, N_DEV):
            acc = acc + s_ref[:, b:b + 1] * d_ref[b:b + 1, :]
        out_ref[...] = acc

    return pl.pallas_call(
        body, name="w_ada_grad", grid=(n_layer,),
        in_specs=[pl.BlockSpec((d_model, N_DEV), lambda l: (0, 0)),
                  pl.BlockSpec((None, N_DEV, cols), lambda l: (l, 0, 0))],
        out_specs=pl.BlockSpec((None, d_model, cols), lambda l: (l, 0, 0)),
        out_shape=SDS((n_layer, d_model, cols), F32),
        compiler_params=_params(1),
    )(sc_t, dmod_loc)


def _sum_devices(g):
    _, rows, cols = g.shape

    def body(g_ref, out_ref):
        acc = g_ref[0]
        for i in range(1, N_DEV):
            acc = acc + g_ref[i]
        out_ref[...] = acc

    return pl.pallas_call(body, name="sum_devices", out_shape=SDS((rows, cols), F32))(g)


def _ffn_fwd(x, prm, w1r, w2g, l, sub, tm):
    t_len, d_model = x.shape
    f8, r8 = w1r.shape[-1], w2g.shape[-2]

    def body(x_ref, prm_ref, w1_ref, w2_ref, act_ref, h_ref, y_ref, xo_ref, h_sc, acc):
        j = pl.program_id(1)

        @pl.when(j == 0)
        def _():
            h = _ada_fwd(x_ref[...], prm_ref).astype(BF16)
            h_sc[...] = h
            h_ref[...] = h
            acc[...] = jnp.zeros_like(acc)

        w2 = w2_ref[...].reshape(2 * r8, d_model)
        for rows in _row_halves(tm):
            hb = h_sc[rows, :]
            g = _dot(hb, w1_ref[0])
            u = _dot(hb, w1_ref[1])
            sig = jax.nn.sigmoid(g)
            sil = g * sig
            s = (sil * u).astype(BF16)
            act_ref[0, rows, :] = (u * (sig * (1.0 + g * (1.0 - sig)))).astype(BF16)
            act_ref[1, rows, :] = sil.astype(BF16)
            act_ref[2, rows, :] = s
            acc[rows, :] += _dot(s, w2)

        @pl.when(j == N_CHUNK - 1)
        def _():
            yv = acc[...]
            y_ref[...] = yv.astype(BF16)
            xo_ref[...] = x_ref[...] + (0.5 * prm_ref[3:4, :]) * yv

    tile = pl.BlockSpec((tm, d_model), lambda m, j: (m, 0))
    return pl.pallas_call(
        body, name=f"ffn_fwd_{l}_{sub}", grid=(t_len // tm, N_CHUNK),
        in_specs=[tile, pl.BlockSpec((8, d_model), lambda m, j: (0, 0)),
                  pl.BlockSpec((2, None, d_model, f8), lambda m, j: (0, j, 0, 0)),
                  pl.BlockSpec((2, r8, d_model), lambda m, j: (j, 0, 0))],
        out_specs=[pl.BlockSpec((3, None, tm, f8), lambda m, j: (0, j, m, 0)), tile, tile, tile],
        out_shape=[SDS((3, N_CHUNK, t_len, f8), BF16), SDS((t_len, d_model), BF16),
                   SDS((t_len, d_model), BF16), SDS((t_len, d_model), F32)],
        scratch_shapes=[pltpu.VMEM((tm, d_model), BF16), pltpu.VMEM((tm, d_model), F32)],
        compiler_params=_params(2),
    )(x, prm, w1r, w2g)


def _ffn_bwd(dxo, y, x_in, prm, act, w1r, w2g, l, sub, tm):
    t_len, d_model = x_in.shape
    f8, r8 = w1r.shape[-1], w2g.shape[-2]
    n_m = t_len // tm

    def body(dxo_ref, y_ref, x_ref, prm_ref, act_ref, w1_ref, w2_ref,
             dx_ref, da_ref, dy_ref, st_ref, dy_sc, acc):
        m, j = pl.program_id(0), pl.program_id(1)

        @pl.when((m == 0) & (j == 0))
        def _():
            st_ref[...] = jnp.zeros_like(st_ref)

        @pl.when(j == 0)
        def _():
            dxo_v = dxo_ref[...]
            dy = ((0.5 * prm_ref[3:4, :]) * dxo_v).astype(BF16)
            dy_sc[...] = dy
            dy_ref[...] = dy
            st_ref[3:4, :] += jnp.sum(0.5 * dxo_v * y_ref[...].astype(F32), axis=0, keepdims=True)
            acc[...] = jnp.zeros_like(acc)

        w2 = w2_ref[...].reshape(2 * r8, d_model)
        for rows in _row_halves(tm):
            ds = _dot_nt(dy_sc[rows, :], w2)
            dg = (ds * act_ref[0, rows, :].astype(F32)).astype(BF16)
            du = (ds * act_ref[1, rows, :].astype(F32)).astype(BF16)
            da_ref[0, rows, :] = dg
            da_ref[1, rows, :] = du
            acc[rows, :] += _dot_nt(dg, w1_ref[0]) + _dot_nt(du, w1_ref[1])

        @pl.when(j == N_CHUNK - 1)
        def _():
            dx, dshift, dscale, dng = _ada_bwd(acc[...], x_ref[...], prm_ref)
            dx_ref[...] = dxo_ref[...] + dx
            st_ref[0:1, :] += dshift
            st_ref[1:2, :] += dscale
            st_ref[2:3, :] += dng

    tile = pl.BlockSpec((tm, d_model), lambda m, j: (m, 0))
    fixed = pl.BlockSpec((8, d_model), lambda m, j: (0, 0))
    return pl.pallas_call(
        body, name=f"ffn_bwd_{l}_{sub}", grid=(n_m, N_CHUNK),
        in_specs=[tile, tile, tile, fixed,
                  pl.BlockSpec((2, None, tm, f8), lambda m, j: (0, j, m, 0)),
                  pl.BlockSpec((2, None, d_model, f8), lambda m, j: (0, j, 0, 0)),
                  pl.BlockSpec((2, r8, d_model), lambda m, j: (j, 0, 0))],
        out_specs=[tile, pl.BlockSpec((2, None, tm, f8), lambda m, j: (0, j, m, 0)), tile, fixed],
        out_shape=[SDS((t_len, d_model), F32), SDS((2, N_CHUNK, t_len, f8), BF16),
                   SDS((t_len, d_model), BF16), SDS((8, d_model), F32)],
        scratch_shapes=[pltpu.VMEM((tm, d_model), BF16), pltpu.VMEM((tm, d_model), F32)],
        compiler_params=_params(2),
    )(dxo, y, x_in, prm, act, w1r, w2g)


def _matmul_tn(a, b, a_spec, b_spec, out_shape, out_spec, n_piece, t_len, tk, name, after):
    def body(a_ref, b_ref, after_ref, o_ref, acc):
        k = pl.program_id(1)

        @pl.when(k == 0)
        def _():
            acc[...] = jnp.zeros_like(acc)

        m_out, n_out = acc.shape
        if (m_out // 2) % LANES == 0:
            for rows in _row_halves(m_out):
                acc[rows, :] += _dot_tn(a_ref[:, rows], b_ref[...])
        else:
            for cols in _row_halves(n_out):
                acc[:, cols] += _dot_tn(a_ref[...], b_ref[:, cols])

        @pl.when(k == pl.num_programs(1) - 1)
        def _():
            o_ref[...] = acc[...].astype(o_ref.dtype)

    acc_shape = tuple(d for d in out_spec.block_shape if d is not None)
    return pl.pallas_call(
        body, name=name, grid=(n_piece, t_len // tk),
        in_specs=[a_spec, b_spec, ANY], out_specs=out_spec, out_shape=out_shape,
        scratch_shapes=[pltpu.VMEM(acc_shape, F32)],
        compiler_params=_params(2),
    )(a, b, after)


def _groups(tm, width):
    return pltpu.VMEM((width // LANES, tm, LANES), F32)


def _to_groups(val, dst_ref, first=0):
    for g in range(val.shape[1] // LANES):
        dst_ref[first + g] = val[:, g * LANES:(g + 1) * LANES]


def _from_groups(src_ref):
    return jnp.concatenate([src_ref[g] for g in range(src_ref.shape[0])], axis=1)


def _split_residues(src_ref, out_ref, dil, tm):
    for g in range(src_ref.shape[0]):
        for r in range(dil):
            out_ref[r, :, g * LANES:(g + 1) * LANES] = (
                src_ref.at[g][pl.ds(r, tm // dil, stride=dil), :].astype(out_ref.dtype))


def _merge_residues(in_ref, dst_ref, dil, tm, add=False):
    for g in range(dst_ref.shape[0]):
        for r in range(dil):
            rows = pl.ds(r, tm // dil, stride=dil)
            val = in_ref[r, :, g * LANES:(g + 1) * LANES].astype(F32)
            dst = dst_ref.at[g]
            dst[rows, :] = (dst[rows, :] + val) if add else val


def _res_spec(dil, tm, width, index=lambda m: m):
    return pl.BlockSpec((dil, tm // dil, width), lambda m: (0, index(m), 0))


def _mixer_in(x, prm, wing, qkg, cprm, bd, l, tm):
    t_len, d_model = x.shape
    a_w = d_model // 2
    c_w = d_model - a_w
    c8 = wing.shape[-1]
    p_c = N_DEV * c8
    dils = DILATIONS[1:]

    def body(x_ref, prm_ref, w_ref, qkg_ref, cprm_ref, bd_ref, proj_ref, qkn_ref, h_ref, ycv_ref, *rest):
        res_refs, (pf, qsc, vsc, carry) = rest[:2 * len(dils)], rest[2 * len(dils):]

        @pl.when(pl.program_id(0) == 0)
        def _():
            carry[...] = jnp.zeros_like(carry)

        h = _ada_fwd(x_ref[...], prm_ref).astype(BF16)
        h_ref[...] = h
        for j in range(N_DEV):
            pf[:, j * c8:(j + 1) * c8] = _dot(h, w_ref[j])
        proj_ref[...] = pf[...].astype(BF16)
        bdm = bd_ref[...]
        for i in range(2):
            t = pf[:, i * a_w:(i + 1) * a_w]
            rstd = lax.rsqrt(_head_mean(t * t, bdm) + EPS)
            tn = t * rstd * qkg_ref[i:i + 1, :]
            qkn_ref[:, i * a_w:(i + 1) * a_w] = tn.astype(BF16)
            _to_groups(tn, qsc, i * (a_w // LANES))
        _to_groups(pf[:, 2 * a_w:3 * a_w], vsc)
        for i, dil in enumerate(dils):
            _split_residues(qsc, res_refs[2 * i], dil, tm)
            _split_residues(vsc, res_refs[2 * i + 1], dil, tm)
        gb = pf[:, 3 * a_w:3 * a_w + c_w]
        z = pf[:, 3 * a_w + c_w:3 * a_w + 2 * c_w] * pf[:, 3 * a_w + 2 * c_w:3 * a_w + 3 * c_w]
        z1, z2 = _shift_rows(z, carry[...], tm)
        cv = cprm_ref[0:1, :] * z2 + cprm_ref[1:2, :] * z1 + cprm_ref[2:3, :] * z + cprm_ref[3:4, :]
        ycv_ref[...] = (gb * cv).astype(BF16)
        carry[...] = z[tm - 8:tm, :]

    res_specs, res_shapes = [], []
    for dil in dils:
        res_specs += [_res_spec(dil, tm, 2 * a_w), _res_spec(dil, tm, a_w)]
        res_shapes += [SDS((dil, t_len // dil, 2 * a_w), BF16), SDS((dil, t_len // dil, a_w), BF16)]
    return pl.pallas_call(
        body, name=f"mixer_in_{l}", grid=(t_len // tm,),
        in_specs=[pl.BlockSpec((tm, d_model), lambda m: (m, 0)),
                  pl.BlockSpec((8, d_model), lambda m: (0, 0)),
                  pl.BlockSpec((N_DEV, d_model, c8), lambda m: (0, 0, 0)),
                  pl.BlockSpec((8, a_w), lambda m: (0, 0)),
                  pl.BlockSpec((8, c_w), lambda m: (0, 0)),
                  pl.BlockSpec(bd.shape, lambda m: (0, 0))],
        out_specs=[pl.BlockSpec((tm, p_c), lambda m: (m, 0)),
                   pl.BlockSpec((tm, 2 * a_w), lambda m: (m, 0)),
                   pl.BlockSpec((tm, d_model), lambda m: (m, 0)),
                   pl.BlockSpec((tm, c_w), lambda m: (m, 0))] + res_specs,
        out_shape=[SDS((t_len, p_c), BF16), SDS((t_len, 2 * a_w), BF16),
                   SDS((t_len, d_model), BF16), SDS((t_len, c_w), BF16)] + res_shapes,
        scratch_shapes=[pltpu.VMEM((tm, p_c), F32), _groups(tm, 2 * a_w), _groups(tm, a_w),
                        pltpu.VMEM((8, c_w), F32)],
        compiler_params=_params(1),
    )(x, prm, wing, qkg, cprm, bd)


def _band_mask(first_block):
    qi = lax.broadcasted_iota(jnp.int32, (BLOCK, 2 * BLOCK), 0)
    kj = lax.broadcasted_iota(jnp.int32, (BLOCK, 2 * BLOCK), 1)
    lowest = jnp.where(first_block, BLOCK, 0)
    return (kj >= qi) & (kj <= qi + BLOCK) & (kj >= lowest)


def _blocks_per_step(seq):
    n_sb = min(ATTN_BLOCKS_PER_STEP, seq // BLOCK)
    assert seq % (n_sb * BLOCK) == 0
    return n_sb


def _block_rows(sb):
    return slice(sb * BLOCK, (sb + 1) * BLOCK)


def _with_previous(prev_ref, cur_ref, sb, sl):
    before = prev_ref[:, sl] if sb == 0 else cur_ref[_block_rows(sb - 1), sl]
    return jnp.concatenate([before, cur_ref[_block_rows(sb), sl]], axis=0)


def _attn_fwd(qk, v, v_col, l):
    dil, seq, a2 = qk.shape
    a_w = a2 // 2
    n_sb = _blocks_per_step(seq)
    rows_step = n_sb * BLOCK

    def body(q_ref, kc_ref, vc_ref, o_ref, lse_ref, kp_ref, vp_ref):
        @pl.when(pl.program_id(1) == 0)
        def _():
            kp_ref[...] = jnp.zeros_like(kp_ref)
            vp_ref[...] = jnp.zeros_like(vp_ref)

        masks = [_band_mask(pl.program_id(1) == 0)] + [_band_mask(False)] * (n_sb - 1)
        low = lax.broadcasted_iota(jnp.int32, (BLOCK, LANES), 1) < HEAD_DIM
        units = [(sb, slice(p * LANES, (p + 1) * LANES)) for sb in range(n_sb) for p in range(a_w // LANES)]
        scores = []
        for sb, sl in units:
            qp = q_ref[_block_rows(sb), sl]
            k2 = _with_previous(kp_ref, kc_ref, sb, sl)
            scores.append([_dot_nt(jnp.where(half, qp, jnp.zeros_like(qp)), k2) for half in (low, ~low)])
        probs, lses = [], []
        for (sb, sl), pair in zip(units, scores):
            pr, ls = [], []
            for s in pair:
                s = jnp.where(masks[sb], s * SM_SCALE, NEG)
                mx = jnp.max(s, axis=-1, keepdims=True)
                e = jnp.exp(s - mx)
                den = jnp.sum(e, axis=-1, keepdims=True)
                pr.append((e * (1.0 / den)).astype(BF16))
                ls.append(mx + jnp.log(den))
            probs.append(pr)
            lses.append(ls)
        for (sb, sl), pr, ls in zip(units, probs, lses):
            v2 = _with_previous(vp_ref, vc_ref, sb, sl)
            o_ref[_block_rows(sb), sl] = jnp.where(low, _dot(pr[0], v2), _dot(pr[1], v2)).astype(BF16)
            lse_ref[_block_rows(sb), sl] = jnp.where(low, ls[0], ls[1])
        kp_ref[...] = kc_ref[_block_rows(n_sb - 1), :]
        vp_ref[...] = vc_ref[_block_rows(n_sb - 1), :]

    def blk(f):
        return pl.BlockSpec((None, rows_step, a_w), f)

    return pl.pallas_call(
        body, name=f"attn_fwd_{l}_d{dil}", grid=(dil, seq // rows_step),
        in_specs=[blk(lambda r, b: (r, b, 0)),
                  blk(lambda r, b: (r, b, 1)),
                  blk(lambda r, b: (r, b, v_col))],
        out_specs=[blk(lambda r, b: (r, b, 0)), blk(lambda r, b: (r, b, 0))],
        out_shape=[SDS((dil, seq, a_w), BF16), SDS((dil, seq, a_w), F32)],
        scratch_shapes=[pltpu.VMEM((BLOCK, a_w), BF16), pltpu.VMEM((BLOCK, a_w), BF16)],
        compiler_params=_params(2),
    )(qk, qk, v)


def _attn_combine(outs, lses, ycv, tm, l):
    t_len, c_w = ycv.shape
    a_w = outs[0].shape[-1]
    dils = DILATIONS[1:]
    n_r = len(dils)

    def body(o1, l1, *rest):
        o_res, l_res, y_ref = rest[:n_r], rest[n_r:2 * n_r], rest[2 * n_r]
        cat_ref, lse_ref = rest[2 * n_r + 1], rest[2 * n_r + 2]
        ya_res, lse_res = rest[2 * n_r + 3:3 * n_r + 3], rest[3 * n_r + 3:4 * n_r + 3]
        scr = rest[4 * n_r + 3:]
        so, sl, sya, slse = scr[:n_r], scr[n_r:2 * n_r], scr[2 * n_r], scr[2 * n_r + 1]
        for i, dil in enumerate(dils):
            _merge_residues(o_res[i], so[i], dil, tm)
            _merge_residues(l_res[i], sl[i], dil, tm)
        ls = [l1[...]] + [_from_groups(s) for s in sl]
        ov = [o1[...].astype(F32)] + [_from_groups(s) for s in so]
        mx = jnp.maximum(jnp.maximum(ls[0], ls[1]), ls[2])
        es = [jnp.exp(t - mx) for t in ls]
        den = es[0] + es[1] + es[2]
        ya = (es[0] * ov[0] + es[1] * ov[1] + es[2] * ov[2]) * (1.0 / den)
        lse = mx + jnp.log(den)
        _to_groups(ya, sya)
        _to_groups(lse, slse)
        cat_ref[:, :a_w] = ya.astype(BF16)
        cat_ref[:, a_w:] = y_ref[...]
        lse_ref[...] = lse
        for i, dil in enumerate(dils):
            _split_residues(sya, ya_res[i], dil, tm)
            _split_residues(slse, lse_res[i], dil, tm)

    ta = pl.BlockSpec((tm, a_w), lambda m: (m, 0))
    res = [_res_spec(dil, tm, a_w) for dil in dils]
    return pl.pallas_call(
        body, name=f"attn_combine_{l}", grid=(t_len // tm,),
        in_specs=[ta, ta] + res + res + [pl.BlockSpec((tm, c_w), lambda m: (m, 0))],
        out_specs=[pl.BlockSpec((tm, a_w + c_w), lambda m: (m, 0)), ta] + res + res,
        out_shape=[SDS((t_len, a_w + c_w), BF16), SDS((t_len, a_w), F32)]
        + [SDS((dil, t_len // dil, a_w), BF16) for dil in dils]
        + [SDS((dil, t_len // dil, a_w), F32) for dil in dils],
        scratch_shapes=[_groups(tm, a_w)] * (2 * n_r + 2),
        compiler_params=_params(1),
    )(outs[0].reshape(t_len, a_w), lses[0].reshape(t_len, a_w), *outs[1:], *lses[1:], ycv)


def _mixer_out(cat, x, prm, woutg, l, tm):
    t_len, d_model = x.shape
    r8 = woutg.shape[-2]

    def body(cat_ref, x_ref, prm_ref, w_ref, xo_ref, y_ref):
        w = w_ref[...].reshape(N_DEV * r8, d_model)
        for rows in _row_halves(tm):
            yv = _dot(cat_ref[rows, :], w)
            y_ref[rows, :] = yv.astype(BF16)
            xo_ref[rows, :] = x_ref[rows, :] + prm_ref[3:4, :] * yv

    tile = pl.BlockSpec((tm, d_model), lambda m: (m, 0))
    return pl.pallas_call(
        body, name=f"mixer_out_{l}", grid=(t_len // tm,),
        in_specs=[tile, tile, pl.BlockSpec((8, d_model), lambda m: (0, 0)),
                  pl.BlockSpec((N_DEV, r8, d_model), lambda m: (0, 0, 0))],
        out_specs=[tile, tile],
        out_shape=[SDS((t_len, d_model), F32), SDS((t_len, d_model), BF16)],
        compiler_params=_params(1),
    )(cat, x, prm, woutg)


def _mixer_bwd_out(dxo, ymix, prm, woutg, proj, cprm, l, tm):
    t_len, d_model = dxo.shape
    a_w = d_model // 2
    c_w = d_model - a_w
    r8 = woutg.shape[-2]
    n_m = t_len // tm
    hb = tm // 8

    dils = DILATIONS[1:]

    def body(dxo_ref, ym_ref, prm_ref, w_ref, pc_ref, halo_ref, cprm_ref,
             dya_ref, dpc_ref, dy_ref, st_ref, cs_ref, *rest):
        dya_res, (carry, dsc) = rest[:len(dils)], rest[len(dils):]
        step = pl.program_id(0)
        tile_i = n_m - 1 - step

        @pl.when(step == 0)
        def _():
            st_ref[...] = jnp.zeros_like(st_ref)
            cs_ref[...] = jnp.zeros_like(cs_ref)
            carry[...] = jnp.zeros_like(carry)

        dxo_v = dxo_ref[...]
        dy = (prm_ref[3:4, :] * dxo_v).astype(BF16)
        dy_ref[...] = dy
        st_ref[3:4, :] += jnp.sum(dxo_v * ym_ref[...].astype(F32), axis=0, keepdims=True)
        w = w_ref[...].reshape(N_DEV * r8, d_model)
        dcat = jnp.concatenate([_dot_nt(dy[rows, :], w) for rows in _row_halves(tm)], axis=0)
        _to_groups(dcat[:, :a_w], dsc)
        dya_ref[...] = dcat[:, :a_w].astype(BF16)
        for i, dil in enumerate(dils):
            _split_residues(dsc, dya_res[i], dil, tm)
        dyc = dcat[:, a_w:]
        gb = pc_ref[:, :c_w].astype(F32)
        gc = pc_ref[:, c_w:2 * c_w].astype(F32)
        u = pc_ref[:, 2 * c_w:].astype(F32)
        z = gc * u
        before = halo_ref[:, c_w:2 * c_w].astype(F32) * halo_ref[:, 2 * c_w:].astype(F32)
        before = jnp.where(tile_i > 0, before, jnp.zeros_like(before))
        z1, z2 = _shift_rows(z, before, tm)
        w0, w1, w2 = cprm_ref[0:1, :], cprm_ref[1:2, :], cprm_ref[2:3, :]
        cv = w0 * z2 + w1 * z1 + w2 * z + cprm_ref[3:4, :]
        dcv = dyc * gb
        cs_ref[0:1, :] += jnp.sum(dcv * z2, axis=0, keepdims=True)
        cs_ref[1:2, :] += jnp.sum(dcv * z1, axis=0, keepdims=True)
        cs_ref[2:3, :] += jnp.sum(dcv * z, axis=0, keepdims=True)
        cs_ref[3:4, :] += jnp.sum(dcv, axis=0, keepdims=True)
        row = lax.broadcasted_iota(jnp.int32, dcv.shape, 0)
        after = carry[...]
        d1 = jnp.where(row == tm - 1, after[0:1, :], pltpu.roll(dcv, tm - 1, 0))
        d2 = jnp.where(row == tm - 2, after[0:1, :],
                       jnp.where(row == tm - 1, after[1:2, :], pltpu.roll(dcv, tm - 2, 0)))
        dz = w2 * dcv + w1 * d1 + w0 * d2
        dpc_ref[:, :c_w] = (dyc * cv).astype(BF16)
        dpc_ref[:, c_w:2 * c_w] = (dz * u).astype(BF16)
        dpc_ref[:, 2 * c_w:] = (dz * gc).astype(BF16)
        carry[...] = dcv[0:8, :]

    def rev(width, col=0):
        return pl.BlockSpec((tm, width), lambda s: (n_m - 1 - s, col))

    fixed_d = pl.BlockSpec((8, d_model), lambda s: (0, 0))
    fixed_c = pl.BlockSpec((8, c_w), lambda s: (0, 0))
    return pl.pallas_call(
        body, name=f"mixer_bwd_out_{l}", grid=(n_m,),
        in_specs=[rev(d_model), rev(d_model), fixed_d,
                  pl.BlockSpec((N_DEV, r8, d_model), lambda s: (0, 0, 0)),
                  rev(3 * c_w, 1),
                  pl.BlockSpec((8, 3 * c_w), lambda s: (jnp.maximum((n_m - 1 - s) * hb - 1, 0), 1)),
                  fixed_c],
        out_specs=[rev(a_w), rev(3 * c_w), rev(d_model), fixed_d, fixed_c]
        + [_res_spec(dil, tm, a_w, lambda s: n_m - 1 - s) for dil in dils],
        out_shape=[SDS((t_len, a_w), BF16), SDS((t_len, 3 * c_w), BF16), SDS((t_len, d_model), BF16),
                   SDS((8, d_model), F32), SDS((8, c_w), F32)]
        + [SDS((dil, t_len // dil, a_w), BF16) for dil in dils],
        scratch_shapes=[pltpu.VMEM((8, c_w), F32), _groups(tm, a_w)],
        compiler_params=_params(1),
    )(dxo, ymix, prm, woutg, proj, proj, cprm)


def _attn_bwd(qk, v, v_col, dya, ya, lse, l):
    dil, seq, a2 = qk.shape
    a_w = a2 // 2
    n_sb = _blocks_per_step(seq)
    rows_step = n_sb * BLOCK
    n_step = seq // rows_step
    last = _block_rows(n_sb - 1)

    def body(q_ref, kc_ref, vc_ref, do_ref, ya_ref, lse_ref,
             dq_ref, dk_ref, dv_ref, hold_k, hold_v, acc_k, acc_v, kp_ref, vp_ref):
        j = pl.program_id(1)

        @pl.when(j == 0)
        def _():
            kp_ref[...] = jnp.zeros_like(kp_ref)
            vp_ref[...] = jnp.zeros_like(vp_ref)

        @pl.when((pl.program_id(0) == 0) & (j == 0))
        def _():
            hold_k[...] = jnp.zeros_like(hold_k)
            hold_v[...] = jnp.zeros_like(hold_v)

        @pl.when(j < n_step)
        def _():
            masks = [_band_mask(j == 0)] + [_band_mask(False)] * (n_sb - 1)
            low = lax.broadcasted_iota(jnp.int32, (BLOCK, LANES), 1) < HEAD_DIM
            units = [(sb, slice(p * LANES, (p + 1) * LANES)) for sb in range(n_sb) for p in range(a_w // LANES)]
            acc_k[...] = jnp.zeros_like(acc_k)
            acc_v[...] = jnp.zeros_like(acc_v)
            heads, scores, dps = [], [], []
            for sb, sl in units:
                qp = q_ref[_block_rows(sb), sl]
                k2 = _with_previous(kp_ref, kc_ref, sb, sl)
                v2 = _with_previous(vp_ref, vc_ref, sb, sl)
                dop = do_ref[_block_rows(sb), sl]
                prod = dop.astype(F32) * ya_ref[_block_rows(sb), sl].astype(F32)
                lsep = lse_ref[_block_rows(sb), sl]
                for hh, half in enumerate((low, ~low)):
                    qh = jnp.where(half, qp, jnp.zeros_like(qp))
                    doh = jnp.where(half, dop, jnp.zeros_like(dop))
                    dsum = jnp.sum(jnp.where(half, prod, 0.0), axis=-1, keepdims=True)
                    heads.append((qh, doh, dsum, lsep[:, hh * HEAD_DIM:hh * HEAD_DIM + 1], masks[sb]))
                    scores.append(_dot_nt(qh, k2))
                    dps.append(_dot_nt(doh, v2))
            dss, prs = [], []
            for (qh, doh, dsum, lse_h, valid), s, dp in zip(heads, scores, dps):
                pr = jnp.exp(jnp.where(valid, s * SM_SCALE, NEG) - lse_h)
                dss.append((pr * (dp - dsum) * SM_SCALE).astype(BF16))
                prs.append(pr.astype(BF16))
            for i, (sb, sl) in enumerate(units):
                k2 = _with_previous(kp_ref, kc_ref, sb, sl)
                h0, h1 = heads[2 * i], heads[2 * i + 1]
                dq_ref[_block_rows(sb), sl] = jnp.where(low, _dot(dss[2 * i], k2),
                                                        _dot(dss[2 * i + 1], k2)).astype(BF16)
                two = slice(sb * BLOCK, (sb + 2) * BLOCK)
                acc_k[two, sl] += _dot_tn(dss[2 * i], h0[0]) + _dot_tn(dss[2 * i + 1], h1[0])
                acc_v[two, sl] += _dot_tn(prs[2 * i], h0[1]) + _dot_tn(prs[2 * i + 1], h1[1])
            for out_ref, hold, acc in ((dk_ref, hold_k, acc_k), (dv_ref, hold_v, acc_v)):
                if n_sb > 1:
                    out_ref[:rows_step - BLOCK, :] = hold[:rows_step - BLOCK, :].astype(BF16)
                out_ref[last, :] = (hold[last, :] + acc[:BLOCK, :]).astype(BF16)
                hold[...] = acc[BLOCK:, :]
            kp_ref[...] = kc_ref[last, :]
            vp_ref[...] = vc_ref[last, :]

        @pl.when(j == n_step)
        def _():
            dk_ref[...] = hold_k[...].astype(BF16)
            dv_ref[...] = hold_v[...].astype(BF16)

    def blk(f):
        return pl.BlockSpec((None, rows_step, a_w), f)

    def cur(j):
        return jnp.minimum(j, n_step - 1)

    def late(j):
        return jnp.maximum(j - 1, 0)

    return pl.pallas_call(
        body, name=f"attn_bwd_{l}_d{dil}", grid=(dil, n_step + 1),
        in_specs=[blk(lambda r, j: (r, cur(j), 0)),
                  blk(lambda r, j: (r, cur(j), 1)),
                  blk(lambda r, j: (r, cur(j), v_col)),
                  blk(lambda r, j: (r, cur(j), 0)),
                  blk(lambda r, j: (r, cur(j), 0)),
                  blk(lambda r, j: (r, cur(j), 0))],
        out_specs=[blk(lambda r, j: (r, cur(j), 0)), blk(lambda r, j: (r, late(j), 0)),
                   blk(lambda r, j: (r, late(j), 0))],
        out_shape=[SDS((dil, seq, a_w), BF16)] * 3,
        scratch_shapes=[pltpu.VMEM((rows_step, a_w), F32), pltpu.VMEM((rows_step, a_w), F32),
                        pltpu.VMEM((rows_step + BLOCK, a_w), F32), pltpu.VMEM((rows_step + BLOCK, a_w), F32),
                        pltpu.VMEM((BLOCK, a_w), BF16), pltpu.VMEM((BLOCK, a_w), BF16)],
        compiler_params=_params(2),
    )(qk, qk, v, dya, ya, lse)


def _mixer_bwd_in(parts, proj, dpc, dxo, x_in, prm, wing, qkg, bd, l, tm):
    t_len, d_model = x_in.shape
    a_w = d_model // 2
    c_w = d_model - a_w
    c8 = wing.shape[-1]
    p_c = N_DEV * c8

    dils = DILATIONS[1:]
    n_in = 3 * len(DILATIONS)

    def body(*refs):
        nat, res = refs[:3], refs[3:n_in]
        (pa_ref, dpc_ref, dxo_ref, x_ref, prm_ref, w_ref, qkg_ref, bd_ref,
         dx_ref, dproj_ref, st_ref, qs_ref) = refs[n_in:n_in + 12]
        sums = refs[n_in + 12:]

        @pl.when(pl.program_id(0) == 0)
        def _():
            st_ref[...] = jnp.zeros_like(st_ref)
            qs_ref[...] = jnp.zeros_like(qs_ref)

        for t in range(3):
            _to_groups(nat[t][...].astype(F32), sums[t])
            for i, dil in enumerate(dils):
                _merge_residues(res[3 * i + t], sums[t], dil, tm, add=True)
        bdm = bd_ref[...]
        grads = [_from_groups(sums[0]), _from_groups(sums[1])]
        for i in range(2):
            t = pa_ref[:, i * a_w:(i + 1) * a_w].astype(F32)
            rstd = lax.rsqrt(_head_mean(t * t, bdm) + EPS)
            that = t * rstd
            qs_ref[i:i + 1, :] += jnp.sum(grads[i] * that, axis=0, keepdims=True)
            dhat = grads[i] * qkg_ref[i:i + 1, :]
            dt = rstd * (dhat - that * _head_mean(dhat * that, bdm))
            dproj_ref[:, i * a_w:(i + 1) * a_w] = dt.astype(BF16)
        dproj_ref[:, 2 * a_w:3 * a_w] = _from_groups(sums[2]).astype(BF16)
        dproj_ref[:, 3 * a_w:] = dpc_ref[...]
        dh = _dot_nt(dproj_ref[:, 0:c8], w_ref[0])
        for j in range(1, N_DEV):
            dh = dh + _dot_nt(dproj_ref[:, j * c8:(j + 1) * c8], w_ref[j])
        dx, dshift, dscale, dng = _ada_bwd(dh, x_ref[...], prm_ref)
        dx_ref[...] = dxo_ref[...] + dx
        st_ref[0:1, :] += dshift
        st_ref[1:2, :] += dscale
        st_ref[2:3, :] += dng

    ta = pl.BlockSpec((tm, a_w), lambda m: (m, 0))
    tile = pl.BlockSpec((tm, d_model), lambda m: (m, 0))
    return pl.pallas_call(
        body, name=f"mixer_bwd_in_{l}", grid=(t_len // tm,),
        in_specs=[ta] * 3 + [_res_spec(dil, tm, a_w) for dil in dils for _ in range(3)]
                 + [pl.BlockSpec((tm, 3 * a_w), lambda m: (m, 0)),
                             pl.BlockSpec((tm, 3 * c_w), lambda m: (m, 0)),
                             tile, tile, pl.BlockSpec((8, d_model), lambda m: (0, 0)),
                             pl.BlockSpec((N_DEV, d_model, c8), lambda m: (0, 0, 0)),
                             pl.BlockSpec((8, a_w), lambda m: (0, 0)),
                             pl.BlockSpec(bd.shape, lambda m: (0, 0))],
        out_specs=[tile, pl.BlockSpec((tm, p_c), lambda m: (m, 0)),
                   pl.BlockSpec((8, d_model), lambda m: (0, 0)), pl.BlockSpec((8, a_w), lambda m: (0, 0))],
        out_shape=[SDS((t_len, d_model), F32), SDS((t_len, p_c), BF16), SDS((8, d_model), F32), SDS((8, a_w), F32)],
        scratch_shapes=[_groups(tm, a_w)] * 3,
        compiler_params=_params(1),
    )(*[t.reshape(t_len, a_w) for t in parts[0]], *[t for p in parts[1:] for t in p],
      proj, dpc, dxo, x_in, prm, wing, qkg, bd)


def _loss_head(x, target, tm):
    t_len, d_model = x.shape

    def body(x_ref, t_ref, dx_ref, loss_ref):
        @pl.when(pl.program_id(0) == 0)
        def _():
            loss_ref[...] = jnp.zeros_like(loss_ref)

        diff = x_ref[...] - t_ref[...]
        dx_ref[...] = diff * (1.0 / d_model)
        per_token = jnp.sum(diff * diff, axis=-1, keepdims=True) * (1.0 / d_model)
        loss_ref[...] += 0.5 * jnp.sum(per_token)

    tile = pl.BlockSpec((tm, d_model), lambda m: (m, 0))
    return pl.pallas_call(
        body, name="loss_head", grid=(t_len // tm,),
        in_specs=[tile, tile],
        out_specs=[tile, pl.BlockSpec((8, LANES), lambda m: (0, 0))],
        out_shape=[SDS((t_len, d_model), F32), SDS((8, LANES), F32)],
        compiler_params=_params(1),
    )(x, target)


def _row_tile(rows, cols):
    best = None
    for t in range(16, rows + 1, 16):
        if rows % t == 0 and t * cols * 4 <= (1 << 20):
            best = t
    return best if best is not None else rows


def _adamw(pieces, w, m, v, name):
    n_q = len(pieces)
    n_p, rows, cols = pieces[0].shape
    tr = _row_tile(rows, cols)
    n_i = rows // tr
    c1 = 1.0 - ADAM_B1 ** ADAM_STEP
    c2 = 1.0 - ADAM_B2 ** ADAM_STEP

    def body(*refs):
        p_refs = refs[:n_q]
        w_ref, m_ref, v_ref, g_ref, d_ref, nm_ref, nv_ref = refs[n_q:]
        for q in range(n_q):
            @pl.when(pl.program_id(0) == q)
            def _():
                g = p_refs[q][0].astype(F32)
                for i in range(1, n_p):
                    g = g + p_refs[q][i].astype(F32)
                g_ref[...] = g
                nm = ADAM_B1 * m_ref[...] + (1.0 - ADAM_B1) * g
                nv = ADAM_B2 * v_ref[...] + (1.0 - ADAM_B2) * (g * g)
                nm_ref[...] = nm
                nv_ref[...] = nv
                d_ref[...] = -ADAM_LR * ((nm / c1) / (jnp.sqrt(nv / c2) + ADAM_EPS) + ADAM_WD * w_ref[...])

    def piece_spec(q):
        return pl.BlockSpec((n_p, tr, cols), lambda a, i: (0, jnp.where(a == q, i, 0), 0))

    tile = pl.BlockSpec((tr, cols), lambda a, i: (a * n_i + i, 0))
    return pl.pallas_call(
        body, name=name, grid=(n_q, n_i),
        in_specs=[piece_spec(q) for q in range(n_q)] + [tile, tile, tile],
        out_specs=[tile] * 4,
        out_shape=[SDS((n_q * rows, cols), F32)] * 4,
        compiler_params=_params(2),
    )(*pieces, w, m, v)


def _pack(vecs):
    flat = jnp.concatenate([v.reshape(-1).astype(F32) for v in vecs])
    rows = -(-flat.shape[0] // (8 * LANES)) * 8
    return jnp.pad(flat, (0, rows * LANES - flat.shape[0])).reshape(rows, LANES)


def _unpack(packed, shapes):
    lead = packed.shape[:-2]
    flat = packed.reshape(lead + (-1,))
    out, off = [], 0
    for s in shapes:
        size = 1
        for d in s:
            size *= d
        out.append(flat[..., off:off + size].reshape(lead + tuple(s)))
        off += size
    return out


def kernel(x, c, w_ada, b_ada, norm_g, w_in, q_norm_g, k_norm_g, conv_w, conv_b, w_out, ffn_w1, ffn_w2, loss_target, m_w_ada, m_b_ada, m_norm_g, m_w_in, m_q_norm_g, m_k_norm_g, m_conv_w, m_conv_b, m_w_out, m_ffn_w1, m_ffn_w2, v_w_ada, v_b_ada, v_norm_g, v_w_in, v_q_norm_g, v_k_norm_g, v_conv_w, v_conv_b, v_w_out, v_ffn_w1, v_ffn_w2):
    _, t_len, d_model = x.shape
    n_layer = w_ada.shape[0]
    a_w = d_model // 2
    c_w = d_model - a_w
    n_head = a_w // HEAD_DIM
    ada_cols = w_ada.shape[-1]
    tm = min(TOKEN_TILE, t_len)
    tm_ffn = min(FFN_TOKEN_TILE, t_len)
    me = _index(_my_place())
    x2 = x.reshape(t_len, d_model)
    target = loss_target.reshape(t_len, d_model)

    small_shapes = [(d_model,), norm_g.shape, conv_w.shape]
    gathered = _allgather_small(_pack([c, norm_g, conv_w]), "gather_small_inputs")
    c_all, ng_parts, cw_parts = _unpack(gathered, small_shapes)
    norm_g_full = jnp.moveaxis(ng_parts, 0, 2).reshape(n_layer, 3, d_model)
    conv_w_full = jnp.moveaxis(cw_parts, 0, 2).reshape(n_layer, 3, c_w)

    b_loc = lax.dynamic_slice_in_dim(b_ada, me * ada_cols, ada_cols, axis=1).reshape(n_layer, 1, ada_cols)
    mod_part, silu_c = _mod_part(c_all, w_ada, b_loc)
    mod_all = _allgather_small(_pack([mod_part]), "gather_mod")
    mod_all = _unpack(mod_all, [mod_part.shape])[0]
    mod_mine = lax.dynamic_index_in_dim(mod_all, me, axis=2, keepdims=False)
    mod = jnp.moveaxis(mod_mine, 0, 1).reshape(n_layer, 3, 3, d_model)

    def prm_of(l, sub):
        rows = jnp.stack([norm_g_full[l, sub], mod[l, sub, 0], mod[l, sub, 1], mod[l, sub, 2]])
        return jnp.pad(rows, ((0, 4), (0, 0)))

    prm = [[prm_of(l, sub) for sub in range(3)] for l in range(n_layer)]
    qkg = [jnp.pad(jnp.stack([jnp.tile(q_norm_g[l], n_head), jnp.tile(k_norm_g[l], n_head)]), ((0, 6), (0, 0)))
           for l in range(n_layer)]
    cprm = [jnp.pad(jnp.concatenate([conv_w_full[l], conv_b[l][None]]), ((0, 4), (0, 0))) for l in range(n_layer)]
    head_of = jnp.arange(min(MXU_WIDTH, a_w)) // HEAD_DIM
    bd = (head_of[:, None] == head_of[None, :]).astype(BF16)

    f8 = ffn_w1.shape[-1]
    r8 = ffn_w2.shape[-2]
    c8 = w_in.shape[-1]
    o8 = w_out.shape[-2]
    me_arr = jnp.reshape(me, (1,)).astype(jnp.int32)
    win_b = [_place_own(w_in, (l,), me_arr, f"own_w_in_{l}") for l in range(n_layer)]
    wout_b = [_place_own(w_out, (l,), me_arr, f"own_w_out_{l}") for l in range(n_layer)]
    w1_b = [[_place_own(ffn_w1, (l, s), me_arr, f"own_w1_{l}_{s}") for s in range(2)] for l in range(n_layer)]
    w2_b = [[_place_own(ffn_w2, (l, s), me_arr, f"own_w2_{l}_{s}") for s in range(2)] for l in range(n_layer)]
    w1_first, w2_first = _allgather_big([w1_b[0][0], w2_b[0][0]], "gather_first_weights")
    groups = [[win_b[0], wout_b[0]], [w1_b[0][1], w2_b[0][1]]]
    for l in range(1, n_layer):
        groups += [[w1_b[l][0], w2_b[l][0]], [win_b[l], wout_b[l]], [w1_b[l][1], w2_b[l][1]]]
    flat = [a for grp in groups for a in grp]
    w_sends, w_recvs, w_lands, w_token = _send_start(flat, None, w1_first, "weights_start")

    def gathered(gi, after):
        lo = sum(len(grp) for grp in groups[:gi])
        hi = lo + len(groups[gi])
        return _send_wait(w_lands[lo:hi], w_sends[lo:hi], w_recvs[lo:hi], after, f"weights_wait_{gi}")

    def chunked(w1g):
        return w1g.reshape(2, N_CHUNK, d_model, f8)

    saved = []
    xc = x2
    for l in range(n_layer):
        keep = {"x0": xc}
        if l == 0:
            w1g, w2g = w1_first, w2_first
            prm_first = prm[0][0] + w_token[0, 0]
        else:
            w1g, w2g = gathered(3 * l - 1, xc)
            prm_first = prm[l][0]
        keep["w_ffn0"] = (chunked(w1g), w2g)
        keep["act0"], keep["h0"], keep["y0"], xc = _ffn_fwd(xc, prm_first, *keep["w_ffn0"], l, 0, tm_ffn)
        keep["x1"] = xc
        wing, woutg = gathered(0 if l == 0 else 3 * l, xc)
        keep.update(wing=wing, woutg=woutg)
        proj, qkn, keep["h1"], ycv, *res = _mixer_in(xc, prm[l][1], wing, qkg[l], cprm[l], bd, l, tm)
        qkv = [(qkn[None], proj[None], 2)] + [(res[2 * i], res[2 * i + 1], 0) for i in range(len(DILATIONS) - 1)]
        branches = [_attn_fwd(*t, l) for t in qkv]
        cat, lse, *res = _attn_combine([b[0] for b in branches], [b[1] for b in branches], ycv, tm, l)
        n_r = len(DILATIONS) - 1
        keep["ya"] = [cat[None]] + res[:n_r]
        keep["lse"] = [lse[None]] + res[n_r:]
        xc, keep["ymix"] = _mixer_out(cat, xc, prm[l][1], woutg, l, tm)
        keep.update(proj=proj, qkv=qkv, cat=cat)
        keep["x2"] = xc
        w1g, w2g = gathered(1 if l == 0 else 3 * l + 1, xc)
        keep["w_ffn2"] = (chunked(w1g), w2g)
        keep["act2"], keep["h2"], keep["y2"], xc = _ffn_fwd(xc, prm[l][2], *keep["w_ffn2"], l, 1, tm_ffn)
        saved.append(keep)

    dx, loss_blk = _loss_head(xc, target, tm)
    loss = lax.psum(loss_blk[0, 0], ("x", "y", "c"))

    tk = min(GRAD_TOKEN_TILE, t_len)
    stats =[[None] * 3 for _ in range(n_layer)]
    conv_stats = [None] * n_layer
    qk_stats = [None] * n_layer

    started = {}
    dummy = jnp.zeros((8, LANES), F32)

    def start_exchange(g, name):
        sends, recvs, thru, token = _send_start(
            [_place_own(g, None, me_arr, f"own_grad_{name}")], [g], dummy, f"grads_start_{name}")
        started[name] = (thru[0], thru[1], sends[0], recvs[0])
        return token

    def wait_exchange(names, after, name):
        ent = [started[k] for k in names]
        done = _send_wait([e[0] for e in ent] + [e[1] for e in ent], [e[2] for e in ent], [e[3] for e in ent],
                          after, name)
        return dict(zip(names, done))

    def ffn_backward(dxo, keep, l, sub, which, prm_dep):
        dx_in, da, dy, st = _ffn_bwd(dxo, keep[f"y{which}"], keep[f"x{which}"], prm_dep,
                                     keep[f"act{which}"], *keep[f"w_ffn{which}"], l, sub, tm_ffn)
        g_w2 = _matmul_tn(
            keep[f"act{which}"], dy,
            pl.BlockSpec((None, None, tk, f8), lambda p, k: (2, p, k, 0)),
            pl.BlockSpec((tk, d_model), lambda p, k: (k, 0)),
            SDS((N_CHUNK, f8, d_model), BF16), pl.BlockSpec((None, f8, d_model), lambda p, k: (p, 0, 0)),
            N_CHUNK, t_len, tk, f"grad_w2_{l}_{sub}", dummy)
        token = start_exchange(g_w2.reshape(N_DEV, r8, d_model), f"w2_{l}_{sub}")
        g_w1 = _matmul_tn(
            keep[f"h{which}"], da.reshape(N_DEV, t_len, f8),
            pl.BlockSpec((tk, d_model), lambda p, k: (k, 0)),
            pl.BlockSpec((None, tk, f8), lambda p, k: (p, k, 0)),
            SDS((N_DEV, d_model, f8), BF16), pl.BlockSpec((None, d_model, f8), lambda p, k: (p, 0, 0)),
            N_DEV, t_len, tk, f"grad_w1_{l}_{sub}", token)
        token = start_exchange(g_w1, f"w1_{l}_{sub}")
        return dx_in, st, token[0, 0]

    token = 0.0
    for l in reversed(range(n_layer)):
        keep = saved[l]
        wing, woutg = keep["wing"], keep["woutg"]
        dx, stats[l][2], token = ffn_backward(dx, keep, l, 1, 2, prm[l][2] + token)
        dya, dpc, dy, st_gate, conv_stats[l], *dya_res = _mixer_bwd_out(
            dx, keep["ymix"], prm[l][1] + token, woutg, keep["proj"], cprm[l], l, tm)
        dyas = [dya[None]] + dya_res
        g_wout = _matmul_tn(
            keep["cat"], dy,
            pl.BlockSpec((tk, d_model), lambda p, k: (k, 0)),
            pl.BlockSpec((tk, d_model), lambda p, k: (k, 0)),
            SDS((d_model, d_model), BF16), pl.BlockSpec((d_model, d_model), lambda p, k: (0, 0)),
            1, t_len, tk, f"grad_wout_{l}", dummy)
        token = start_exchange(g_wout.reshape(N_DEV, o8, d_model), f"wout_{l}")[0, 0]
        parts = [_attn_bwd(*keep["qkv"][i], dyas[i], keep["ya"][i], keep["lse"][i], l)
                 for i in range(len(DILATIONS))]
        dx, dproj, st_norm, qk_stats[l] = _mixer_bwd_in(
            parts, keep["proj"], dpc, dx, keep["x1"], prm[l][1] + token, wing, qkg[l], bd, l, tm)
        stats[l][1] = st_norm + st_gate
        g_win = _matmul_tn(
            keep["h1"], dproj,
            pl.BlockSpec((tk, d_model), lambda p, k: (k, 0)),
            pl.BlockSpec((tk, c8), lambda p, k: (k, p)),
            SDS((N_DEV, d_model, c8), BF16), pl.BlockSpec((None, d_model, c8), lambda p, k: (p, 0, 0)),
            N_DEV, t_len, tk, f"grad_win_{l}", dummy)
        token = start_exchange(g_win, f"win_{l}")[0, 0]
        dx, stats[l][0], token = ffn_backward(dx, keep, l, 0, 0, prm[l][0] + token)
    grad_x = dx.reshape(x.shape)

    def update(pieces, w, m, v, name):
        shape = w.shape
        cols = shape[-1]
        outs = _adamw([p.reshape(p.shape[0], -1, cols) for p in pieces], w.reshape(-1, cols),
                      m.reshape(-1, cols), v.reshape(-1, cols), name)
        return [o.reshape(shape) for o in outs]

    last = "w1_0_0"
    lands = wait_exchange([k for k in started if k != last], dx, "grads_wait_early")
    res = {
        "w_in": update([lands[f"win_{l}"] for l in range(n_layer)], w_in, m_w_in, v_w_in, "adamw_w_in"),
        "w_out": update([lands[f"wout_{l}"] for l in range(n_layer)], w_out, m_w_out, v_w_out, "adamw_w_out"),
        "ffn_w2": update([lands[f"w2_{l}_{s}"] for l in range(n_layer) for s in range(2)],
                         ffn_w2, m_ffn_w2, v_ffn_w2, "adamw_ffn_w2"),
    }

    dmod = jnp.stack([jnp.stack([stats[l][sub][r] for sub in range(3) for r in (0, 1, 3)]) for l in range(n_layer)])
    dng = jnp.stack([jnp.stack([stats[l][sub][2] for sub in range(3)]) for l in range(n_layer)])
    dcw = jnp.stack([conv_stats[l][0:3] for l in range(n_layer)])
    dcb = jnp.stack([conv_stats[l][3] for l in range(n_layer)])
    dqk = jnp.stack([qk_stats[l][0:2].reshape(2, n_head, HEAD_DIM).sum(axis=1) for l in range(n_layer)])
    part_shapes = [dmod.shape, dng.shape, dcw.shape, dcb.shape, dqk.shape]
    parts_all = _allgather_small(_pack([dmod, dng, dcw, dcb, dqk]), "gather_small_grads")
    dmod_all = _unpack(parts_all, part_shapes)[0].reshape(N_DEV, n_layer, 9 * d_model)
    summed = _unpack(_sum_devices(parts_all), part_shapes)
    g_b_ada = summed[0].reshape(n_layer, 9 * d_model)
    ng_cols = norm_g.shape[-1]
    g_norm_g = lax.dynamic_slice_in_dim(summed[1], me * ng_cols, ng_cols, axis=2)
    cw_cols = conv_w.shape[-1]
    g_conv_w = lax.dynamic_slice_in_dim(summed[2], me * cw_cols, cw_cols, axis=2)
    g_conv_b = summed[3]
    g_q, g_k = summed[4][:, 0], summed[4][:, 1]
    dmod_loc = lax.dynamic_slice_in_dim(dmod_all, me * ada_cols, ada_cols, axis=2)
    g_w_ada = _w_ada_grad(silu_c.T, jnp.moveaxis(dmod_loc, 0, 1))
    res["w_ada"] = update([g_w_ada[None]], w_ada, m_w_ada, v_w_ada, "adamw_w_ada")

    names = ["b_ada", "norm_g", "q_norm_g", "k_norm_g", "conv_w", "conv_b"]
    g_small = [g_b_ada, g_norm_g, g_q, g_k, g_conv_w, g_conv_b]
    w_small = [b_ada, norm_g, q_norm_g, k_norm_g, conv_w, conv_b]
    m_small = [m_b_ada, m_norm_g, m_q_norm_g, m_k_norm_g, m_conv_w, m_conv_b]
    v_small = [v_b_ada, v_norm_g, v_q_norm_g, v_k_norm_g, v_conv_w, v_conv_b]
    outs = _adamw([_pack(g_small)[None]], _pack(w_small), _pack(m_small), _pack(v_small), "adamw_small")
    shapes = [w.shape for w in w_small]
    unpacked = [_unpack(o, shapes) for o in outs]
    for i, nme in enumerate(names):
        res[nme] = [unpacked[k][i] for k in range(4)]

    follows = (res["w_in"][1][0, :1, :1] + res["w_out"][1][0, :1, :1] + res["ffn_w2"][1][0, 0, :1, :1]
               + res["w_ada"][1][0, :1, :1] + outs[1][:1, :1])
    lands.update(wait_exchange([last], follows, "grads_wait_last"))
    res["ffn_w1"] = update([lands[f"w1_{l}_{s}"] for l in range(n_layer) for s in range(2)],
                           ffn_w1, m_ffn_w1, v_ffn_w1, "adamw_ffn_w1")

    order = ["w_ada", "b_ada", "norm_g", "w_in", "q_norm_g", "k_norm_g", "conv_w", "conv_b", "w_out", "ffn_w1", "ffn_w2"]
    return (loss, grad_x, *[res[n][0] for n in order], *[res[n][1] for n in order],
            *[res[n][2] for n in order], *[res[n][3] for n in order])
```

```python
import jax
import jax.numpy as jnp
from jax import lax
from jax.experimental import pallas as pl
from jax.experimental.pallas import tpu as pltpu

F32 = jnp.float32
BF16 = jnp.bfloat16
SDS = jax.ShapeDtypeStruct

N_DEV = 8
HEAD_DIM = 64
BLOCK = 128
DILATIONS = (1, 4, 16)
ATTN_BLOCKS_PER_STEP = 4
N_CHUNK = 4
EPS = 1e-6
NEG = -1e30
SM_SCALE = HEAD_DIM ** -0.5
LANES = 128
MXU_WIDTH = 256
TOKEN_TILE = 512
FFN_TOKEN_TILE = 512
GRAD_TOKEN_TILE = 2048
VMEM_LIMIT_BYTES = 56 * 1024 * 1024

ADAM_LR = 0.001
ADAM_B1 = 0.9
ADAM_B2 = 0.999
ADAM_EPS = 1e-08
ADAM_WD = 0.01
ADAM_STEP = 10

MESH_ID = pl.DeviceIdType.MESH
ANY = pl.BlockSpec(memory_space=pl.ANY)
HBM_SPEC = pl.BlockSpec(memory_space=pltpu.HBM)
SEM_SPEC = pl.BlockSpec(memory_space=pltpu.SEMAPHORE)
SIDE_EFFECT = pltpu.SideEffectType.DATAFLOW_SIDE_EFFECTING


def _params(n_axes):
    return pltpu.CompilerParams(dimension_semantics=("arbitrary",) * n_axes,
                                vmem_limit_bytes=VMEM_LIMIT_BYTES)


def _dot(a, b):
    return jnp.dot(a, b, preferred_element_type=F32)


def _dot_nt(a, b):
    return lax.dot_general(a, b, (((1,), (1,)), ((), ())), preferred_element_type=F32)


def _dot_tn(a, b):
    return lax.dot_general(a, b, (((0,), (0,)), ((), ())), preferred_element_type=F32)


def _row_halves(n):
    return (slice(0, n // 2), slice(n // 2, n))


def _split_bf16(t):
    hi = t.astype(BF16)
    return hi, (t - hi.astype(F32)).astype(BF16)


def _my_place():
    x, y, c = lax.axis_index("x"), lax.axis_index("y"), lax.axis_index("c")
    return x, y, c


def _peer(place, k):
    x, y, c = place
    return ((1 - x) if k & 4 else x, (1 - y) if k & 2 else y, (1 - c) if k & 1 else c)


def _index(place):
    return 4 * place[0] + 2 * place[1] + place[2]


def _allgather_small(v, name):
    rows, cols = v.shape

    def body(x_ref, out_ref, send_sems, recv_sems, local_sem):
        me = _my_place()
        mine = pltpu.make_async_copy(x_ref, out_ref.at[_index(me)], local_sem)
        mine.start()
        sends = []
        for k in range(1, N_DEV):
            cp = pltpu.make_async_remote_copy(
                src_ref=x_ref, dst_ref=out_ref.at[_index(me)], send_sem=send_sems.at[k - 1],
                recv_sem=recv_sems.at[k - 1], device_id=_peer(me, k), device_id_type=MESH_ID)
            cp.start()
            sends.append(cp)
        for k in range(1, N_DEV):
            pltpu.make_async_remote_copy(
                src_ref=x_ref, dst_ref=out_ref.at[_index(_peer(me, k))], send_sem=send_sems.at[k - 1],
                recv_sem=recv_sems.at[k - 1], device_id=_peer(me, k), device_id_type=MESH_ID).wait_recv()
        for cp in sends:
            cp.wait_send()
        mine.wait()

    return pl.pallas_call(
        body, name=name,
        out_shape=SDS((N_DEV, rows, cols), v.dtype),
        in_specs=[pl.BlockSpec(memory_space=pltpu.VMEM)],
        out_specs=pl.BlockSpec(memory_space=pltpu.VMEM),
        scratch_shapes=[pltpu.SemaphoreType.DMA((N_DEV - 1,)), pltpu.SemaphoreType.DMA((N_DEV - 1,)),
                        pltpu.SemaphoreType.DMA],
    )(v)


def _allgather_big(lands, name):
    n = len(lands)

    def body(*refs):
        outs = refs[n:2 * n]
        send_sems, recv_sems = refs[2 * n:]
        x, y, c = _my_place()
        me, sib = (x, y, c), (x, y, 1 - c)
        chips = [(1 - x, y), (x, 1 - y), (1 - x, 1 - y)]

        def copy(i, k, block, to):
            dst = outs[i].at[_index(block)]
            return pltpu.make_async_remote_copy(
                src_ref=dst, dst_ref=dst, send_sem=send_sems.at[7 * i + k],
                recv_sem=recv_sems.at[7 * i + k], device_id=to, device_id_type=MESH_ID)

        first = []
        for i in range(n):
            first.append(copy(i, 0, me, sib))
            for j, chip in enumerate(chips):
                first.append(copy(i, 1 + j, me, (*chip, c)))
        for cp in first:
            cp.start()
        passed = []
        for i in range(n):
            for j, chip in enumerate(chips):
                copy(i, 1 + j, (*chip, c), me).wait_recv()
                fwd = copy(i, 4 + j, (*chip, c), sib)
                fwd.start()
                passed.append(fwd)
        for i in range(n):
            copy(i, 0, sib, me).wait_recv()
            for j, chip in enumerate(chips):
                copy(i, 4 + j, (*chip, 1 - c), me).wait_recv()
        for cp in first + passed:
            cp.wait_send()

    return pl.pallas_call(
        body, name=name,
        out_shape=[SDS(a.shape, a.dtype) for a in lands],
        in_specs=[ANY] * n, out_specs=[ANY] * n,
        input_output_aliases={i: i for i in range(n)},
        scratch_shapes=[pltpu.SemaphoreType.DMA((7 * n,)), pltpu.SemaphoreType.DMA((7 * n,))],
    )(*lands)


def _place_own(src, lead, me_arr, name):
    rows, cols = src.shape[-2:]
    tr = _row_tile(rows, cols)

    def body(me_ref, s_ref, o_ref):
        o_ref[...] = s_ref[...].astype(BF16)

    if lead is None:
        in_spec = pl.BlockSpec((None, tr, cols), lambda i, me_ref: (me_ref[0], i, 0))
    else:
        in_spec = pl.BlockSpec((None,) * len(lead) + (tr, cols), lambda i, me_ref: (*lead, i, 0))
    return pl.pallas_call(
        body, name=name,
        grid_spec=pltpu.PrefetchScalarGridSpec(
            num_scalar_prefetch=1, grid=(rows // tr,), in_specs=[in_spec],
            out_specs=pl.BlockSpec((None, tr, cols), lambda i, me_ref: (me_ref[0], i, 0))),
        out_shape=SDS((N_DEV, rows, cols), BF16),
        compiler_params=_params(1),
    )(me_arr, src)


def _send_start(lands, pieces, after, name):
    n = len(lands)
    arrs = list(lands) + list(pieces or [])
    n_a = len(arrs)

    def body(*refs):
        land_r, piece_r = refs[:n], refs[n:n_a]
        send, recv = refs[n_a + 1:n_a + 1 + n], refs[n_a + 1 + n:n_a + 1 + 2 * n]
        token = refs[-1]
        me = _my_place()
        for i in range(n):
            for k in range(1, N_DEV):
                peer = _peer(me, k)
                src = piece_r[i].at[_index(peer)] if pieces else land_r[i].at[_index(me)]
                pltpu.make_async_remote_copy(
                    src_ref=src, dst_ref=land_r[i].at[_index(me)], send_sem=send[i], recv_sem=recv[i],
                    device_id=peer, device_id_type=MESH_ID).start()
        token[...] = jnp.zeros_like(token)

    outs = pl.pallas_call(
        body, name=name,
        out_shape=[pltpu.SemaphoreType.DMA(())] * (2 * n) + [pltpu.HBM(a.shape, a.dtype) for a in arrs]
        + [SDS((8, LANES), F32)],
        in_specs=[HBM_SPEC] * n_a + [ANY],
        out_specs=[SEM_SPEC] * (2 * n) + [HBM_SPEC] * n_a + [pl.BlockSpec(memory_space=pltpu.VMEM)],
        input_output_aliases={i: 2 * n + i for i in range(n_a)},
        compiler_params=pltpu.CompilerParams(has_side_effects=SIDE_EFFECT),
    )(*[pltpu.with_memory_space_constraint(a, pltpu.HBM) for a in arrs], after)
    return outs[:n], outs[n:2 * n], outs[2 * n:2 * n + n_a], outs[-1]


def _send_wait(thru, sends, recvs, after, name):
    n = len(sends)
    n_a = len(thru)

    def body(*refs):
        land_r = refs[:n]
        send, recv = refs[n_a:n_a + n], refs[n_a + n:n_a + 2 * n]
        me = _my_place()
        for i in range(n):
            seven = land_r[i].at[pl.ds(0, N_DEV - 1)]
            cp = pltpu.make_async_remote_copy(src_ref=seven, dst_ref=seven, send_sem=send[i], recv_sem=recv[i],
                                              device_id=_peer(me, 1), device_id_type=MESH_ID)
            cp.wait_send()
            cp.wait_recv()

    outs = pl.pallas_call(
        body, name=name,
        out_shape=[pltpu.HBM(a.shape, a.dtype) for a in thru],
        in_specs=[HBM_SPEC] * n_a + [SEM_SPEC] * (2 * n) + [ANY],
        out_specs=[HBM_SPEC] * n_a,
        input_output_aliases={i: i for i in range(n_a)},
        compiler_params=pltpu.CompilerParams(has_side_effects=SIDE_EFFECT),
    )(*thru, *sends, *recvs, after)
    return outs[:n]


def _ada_fwd(x, prm_ref):
    rstd = lax.rsqrt(jnp.mean(x * x, axis=-1, keepdims=True) + EPS)
    return (x * rstd * prm_ref[0:1, :]) * (1.0 + prm_ref[2:3, :]) + prm_ref[1:2, :]


def _ada_bwd(dh, x, prm_ref):
    ng, ops = prm_ref[0:1, :], 1.0 + prm_ref[2:3, :]
    rstd = lax.rsqrt(jnp.mean(x * x, axis=-1, keepdims=True) + EPS)
    xhat = x * rstd
    dxhat = dh * (ops * ng)
    dx = rstd * (dxhat - xhat * jnp.mean(dxhat * xhat, axis=-1, keepdims=True))
    dshift = jnp.sum(dh, axis=0, keepdims=True)
    dscale = jnp.sum(dh * (xhat * ng), axis=0, keepdims=True)
    dng = jnp.sum(dh * (xhat * ops), axis=0, keepdims=True)
    return dx, dshift, dscale, dng


def _head_mean(t, bd):
    hi, lo = _split_bf16(t)
    g = bd.shape[0]
    parts = [_dot(hi[:, i:i + g], bd) + _dot(lo[:, i:i + g], bd) for i in range(0, t.shape[1], g)]
    return jnp.concatenate(parts, axis=1) * (1.0 / HEAD_DIM)


def _shift_rows(z, before, tm):
    row = lax.broadcasted_iota(jnp.int32, z.shape, 0)
    z1 = jnp.where(row == 0, before[7:8, :], pltpu.roll(z, 1, 0))
    z2 = jnp.where(row == 0, before[6:7, :], jnp.where(row == 1, before[7:8, :], pltpu.roll(z, 2, 0)))
    return z1, z2


def _mod_part(c_all, w_ada, b_loc):
    n_layer, d_model, cols = w_ada.shape

    def body(c_ref, w_ref, b_ref, out_ref, sc_ref):
        cv = c_ref[...]
        sc = cv * jax.nn.sigmoid(cv)
        sc_ref[...] = sc
        a_hi, a_lo = _split_bf16(sc)
        w_hi, w_lo = _split_bf16(w_ref[...])
        out_ref[...] = _dot(a_hi, w_hi) + _dot(a_hi, w_lo) + _dot(a_lo, w_hi) + b_ref[...]

    return pl.pallas_call(
        body, name="mod_part", grid=(n_layer,),
        in_specs=[pl.BlockSpec((N_DEV, d_model), lambda l: (0, 0)),
                  pl.BlockSpec((None, d_model, cols), lambda l: (l, 0, 0)),
                  pl.BlockSpec((None, 1, cols), lambda l: (l, 0, 0))],
        out_specs=[pl.BlockSpec((None, N_DEV, cols), lambda l: (l, 0, 0)),
                   pl.BlockSpec((N_DEV, d_model), lambda l: (0, 0))],
        out_shape=[SDS((n_layer, N_DEV, cols), F32), SDS((N_DEV, d_model), F32)],
        compiler_params=_params(1),
    )(c_all, w_ada, b_loc)


def _w_ada_grad(sc_t, dmod_loc):
    d_model = sc_t.shape[0]
    n_layer, _, cols = dmod_loc.shape

    def body(s_ref, d_ref, out_ref):
        acc = s_ref[:, 0:1] * d_ref[0:1, :]
        for b in range(1, N_DEV):
            acc = acc + s_ref[:, b:b + 1] * d_ref[b:b + 1, :]
        out_ref[...] = acc

    return pl.pallas_call(
        body, name="w_ada_grad", grid=(n_layer,),
        in_specs=[pl.BlockSpec((d_model, N_DEV), lambda l: (0, 0)),
                  pl.BlockSpec((None, N_DEV, cols), lambda l: (l, 0, 0))],
        out_specs=pl.BlockSpec((None, d_model, cols), lambda l: (l, 0, 0)),
        out_shape=SDS((n_layer, d_model, cols), F32),
        compiler_params=_params(1),
    )(sc_t, dmod_loc)


def _sum_devices(g):
    _, rows, cols = g.shape

    def body(g_ref, out_ref):
        acc = g_ref[0]
        for i in range(1, N_DEV):
            acc = acc + g_ref[i]
        out_ref[...] = acc

    return pl.pallas_call(body, name="sum_devices", out_shape=SDS((rows, cols), F32))(g)


def _ffn_fwd(x, prm, w1r, w2g, l, sub, tm):
    t_len, d_model = x.shape
    f8, r8 = w1r.shape[-1], w2g.shape[-2]

    def body(x_ref, prm_ref, w1_ref, w2_ref, act_ref, h_ref, y_ref, xo_ref, h_sc, acc):
        j = pl.program_id(1)

        @pl.when(j == 0)
        def _():
            h = _ada_fwd(x_ref[...], prm_ref).astype(BF16)
            h_sc[...] = h
            h_ref[...] = h
            acc[...] = jnp.zeros_like(acc)

        w2 = w2_ref[...].reshape(2 * r8, d_model)
        for rows in _row_halves(tm):
            hb = h_sc[rows, :]
            g = _dot(hb, w1_ref[0])
            u = _dot(hb, w1_ref[1])
            sig = jax.nn.sigmoid(g)
            sil = g * sig
            s = (sil * u).astype(BF16)
            act_ref[0, rows, :] = (u * (sig * (1.0 + g * (1.0 - sig)))).astype(BF16)
            act_ref[1, rows, :] = sil.astype(BF16)
            act_ref[2, rows, :] = s
            acc[rows, :] += _dot(s, w2)

        @pl.when(j == N_CHUNK - 1)
        def _():
            yv = acc[...]
            y_ref[...] = yv.astype(BF16)
            xo_ref[...] = x_ref[...] + (0.5 * prm_ref[3:4, :]) * yv

    tile = pl.BlockSpec((tm, d_model), lambda m, j: (m, 0))
    return pl.pallas_call(
        body, name=f"ffn_fwd_{l}_{sub}", grid=(t_len // tm, N_CHUNK),
        in_specs=[tile, pl.BlockSpec((8, d_model), lambda m, j: (0, 0)),
                  pl.BlockSpec((2, None, d_model, f8), lambda m, j: (0, j, 0, 0)),
                  pl.BlockSpec((2, r8, d_model), lambda m, j: (j, 0, 0))],
        out_specs=[pl.BlockSpec((3, None, tm, f8), lambda m, j: (0, j, m, 0)), tile, tile, tile],
        out_shape=[SDS((3, N_CHUNK, t_len, f8), BF16), SDS((t_len, d_model), BF16),
                   SDS((t_len, d_model), BF16), SDS((t_len, d_model), F32)],
        scratch_shapes=[pltpu.VMEM((tm, d_model), BF16), pltpu.VMEM((tm, d_model), F32)],
        compiler_params=_params(2),
    )(x, prm, w1r, w2g)


def _ffn_bwd(dxo, y, x_in, prm, act, w1r, w2g, l, sub, tm):
    t_len, d_model = x_in.shape
    f8, r8 = w1r.shape[-1], w2g.shape[-2]
    n_m = t_len // tm

    def body(dxo_ref, y_ref, x_ref, prm_ref, act_ref, w1_ref, w2_ref,
             dx_ref, da_ref, dy_ref, st_ref, dy_sc, acc):
        m, j = pl.program_id(0), pl.program_id(1)

        @pl.when((m == 0) & (j == 0))
        def _():
            st_ref[...] = jnp.zeros_like(st_ref)

        @pl.when(j == 0)
        def _():
            dxo_v = dxo_ref[...]
            dy = ((0.5 * prm_ref[3:4, :]) * dxo_v).astype(BF16)
            dy_sc[...] = dy
            dy_ref[...] = dy
            st_ref[3:4, :] += jnp.sum(0.5 * dxo_v * y_ref[...].astype(F32), axis=0, keepdims=True)
            acc[...] = jnp.zeros_like(acc)

        w2 = w2_ref[...].reshape(2 * r8, d_model)
        for rows in _row_halves(tm):
            ds = _dot_nt(dy_sc[rows, :], w2)
            dg = (ds * act_ref[0, rows, :].astype(F32)).astype(BF16)
            du = (ds * act_ref[1, rows, :].astype(F32)).astype(BF16)
            da_ref[0, rows, :] = dg
            da_ref[1, rows, :] = du
            acc[rows, :] += _dot_nt(dg, w1_ref[0]) + _dot_nt(du, w1_ref[1])

        @pl.when(j == N_CHUNK - 1)
        def _():
            dx, dshift, dscale, dng = _ada_bwd(acc[...], x_ref[...], prm_ref)
            dx_ref[...] = dxo_ref[...] + dx
            st_ref[0:1, :] += dshift
            st_ref[1:2, :] += dscale
            st_ref[2:3, :] += dng

    tile = pl.BlockSpec((tm, d_model), lambda m, j: (m, 0))
    fixed = pl.BlockSpec((8, d_model), lambda m, j: (0, 0))
    return pl.pallas_call(
        body, name=f"ffn_bwd_{l}_{sub}", grid=(n_m, N_CHUNK),
        in_specs=[tile, tile, tile, fixed,
                  pl.BlockSpec((2, None, tm, f8), lambda m, j: (0, j, m, 0)),
                  pl.BlockSpec((2, None, d_model, f8), lambda m, j: (0, j, 0, 0)),
                  pl.BlockSpec((2, r8, d_model), lambda m, j: (j, 0, 0))],
        out_specs=[tile, pl.BlockSpec((2, None, tm, f8), lambda m, j: (0, j, m, 0)), tile, fixed],
        out_shape=[SDS((t_len, d_model), F32), SDS((2, N_CHUNK, t_len, f8), BF16),
                   SDS((t_len, d_model), BF16), SDS((8, d_model), F32)],
        scratch_shapes=[pltpu.VMEM((tm, d_model), BF16), pltpu.VMEM((tm, d_model), F32)],
        compiler_params=_params(2),
    )(dxo, y, x_in, prm, act, w1r, w2g)


def _matmul_tn(a, b, a_spec, b_spec, out_shape, out_spec, n_piece, t_len, tk, name, after):
    def body(a_ref, b_ref, after_ref, o_ref, acc):
        k = pl.program_id(1)

        @pl.when(k == 0)
        def _():
            acc[...] = jnp.zeros_like(acc)

        m_out, n_out = acc.shape
        if (m_out // 2) % LANES == 0:
            for rows in _row_halves(m_out):
                acc[rows, :] += _dot_tn(a_ref[:, rows], b_ref[...])
        else:
            for cols in _row_halves(n_out):
                acc[:, cols] += _dot_tn(a_ref[...], b_ref[:, cols])

        @pl.when(k == pl.num_programs(1) - 1)
        def _():
            o_ref[...] = acc[...].astype(o_ref.dtype)

    acc_shape = tuple(d for d in out_spec.block_shape if d is not None)
    return pl.pallas_call(
        body, name=name, grid=(n_piece, t_len // tk),
        in_specs=[a_spec, b_spec, ANY], out_specs=out_spec, out_shape=out_shape,
        scratch_shapes=[pltpu.VMEM(acc_shape, F32)],
        compiler_params=_params(2),
    )(a, b, after)


def _groups(tm, width):
    return pltpu.VMEM((width // LANES, tm, LANES), F32)


def _to_groups(val, dst_ref, first=0):
    for g in range(val.shape[1] // LANES):
        dst_ref[first + g] = val[:, g * LANES:(g + 1) * LANES]


def _from_groups(src_ref):
    return jnp.concatenate([src_ref[g] for g in range(src_ref.shape[0])], axis=1)


def _split_residues(src_ref, out_ref, dil, tm):
    for g in range(src_ref.shape[0]):
        for r in range(dil):
            out_ref[r, :, g * LANES:(g + 1) * LANES] = (
                src_ref.at[g][pl.ds(r, tm // dil, stride=dil), :].astype(out_ref.dtype))


def _merge_residues(in_ref, dst_ref, dil, tm, add=False):
    for g in range(dst_ref.shape[0]):
        for r in range(dil):
            rows = pl.ds(r, tm // dil, stride=dil)
            val = in_ref[r, :, g * LANES:(g + 1) * LANES].astype(F32)
            dst = dst_ref.at[g]
            dst[rows, :] = (dst[rows, :] + val) if add else val


def _res_spec(dil, tm, width, index=lambda m: m):
    return pl.BlockSpec((dil, tm // dil, width), lambda m: (0, index(m), 0))


def _mixer_in(x, prm, wing, qkg, cprm, bd, l, tm):
    t_len, d_model = x.shape
    a_w = d_model // 2
    c_w = d_model - a_w
    c8 = wing.shape[-1]
    p_c = N_DEV * c8
    dils = DILATIONS[1:]

    def body(x_ref, prm_ref, w_ref, qkg_ref, cprm_ref, bd_ref, proj_ref, qkn_ref, h_ref, ycv_ref, *rest):
        res_refs, (pf, qsc, vsc, carry) = rest[:2 * len(dils)], rest[2 * len(dils):]

        @pl.when(pl.program_id(0) == 0)
        def _():
            carry[...] = jnp.zeros_like(carry)

        h = _ada_fwd(x_ref[...], prm_ref).astype(BF16)
        h_ref[...] = h
        for j in range(N_DEV):
            pf[:, j * c8:(j + 1) * c8] = _dot(h, w_ref[j])
        proj_ref[...] = pf[...].astype(BF16)
        bdm = bd_ref[...]
        for i in range(2):
            t = pf[:, i * a_w:(i + 1) * a_w]
            rstd = lax.rsqrt(_head_mean(t * t, bdm) + EPS)
            tn = t * rstd * qkg_ref[i:i + 1, :]
            qkn_ref[:, i * a_w:(i + 1) * a_w] = tn.astype(BF16)
            _to_groups(tn, qsc, i * (a_w // LANES))
        _to_groups(pf[:, 2 * a_w:3 * a_w], vsc)
        for i, dil in enumerate(dils):
            _split_residues(qsc, res_refs[2 * i], dil, tm)
            _split_residues(vsc, res_refs[2 * i + 1], dil, tm)
        gb = pf[:, 3 * a_w:3 * a_w + c_w]
        z = pf[:, 3 * a_w + c_w:3 * a_w + 2 * c_w] * pf[:, 3 * a_w + 2 * c_w:3 * a_w + 3 * c_w]
        z1, z2 = _shift_rows(z, carry[...], tm)
        cv = cprm_ref[0:1, :] * z2 + cprm_ref[1:2, :] * z1 + cprm_ref[2:3, :] * z + cprm_ref[3:4, :]
        ycv_ref[...] = (gb * cv).astype(BF16)
        carry[...] = z[tm - 8:tm, :]

    res_specs, res_shapes = [], []
    for dil in dils:
        res_specs += [_res_spec(dil, tm, 2 * a_w), _res_spec(dil, tm, a_w)]
        res_shapes += [SDS((dil, t_len // dil, 2 * a_w), BF16), SDS((dil, t_len // dil, a_w), BF16)]
    return pl.pallas_call(
        body, name=f"mixer_in_{l}", grid=(t_len // tm,),
        in_specs=[pl.BlockSpec((tm, d_model), lambda m: (m, 0)),
                  pl.BlockSpec((8, d_model), lambda m: (0, 0)),
                  pl.BlockSpec((N_DEV, d_model, c8), lambda m: (0, 0, 0)),
                  pl.BlockSpec((8, a_w), lambda m: (0, 0)),
                  pl.BlockSpec((8, c_w), lambda m: (0, 0)),
                  pl.BlockSpec(bd.shape, lambda m: (0, 0))],
        out_specs=[pl.BlockSpec((tm, p_c), lambda m: (m, 0)),
                   pl.BlockSpec((tm, 2 * a_w), lambda m: (m, 0)),
                   pl.BlockSpec((tm, d_model), lambda m: (m, 0)),
                   pl.BlockSpec((tm, c_w), lambda m: (m, 0))] + res_specs,
        out_shape=[SDS((t_len, p_c), BF16), SDS((t_len, 2 * a_w), BF16),
                   SDS((t_len, d_model), BF16), SDS((t_len, c_w), BF16)] + res_shapes,
        scratch_shapes=[pltpu.VMEM((tm, p_c), F32), _groups(tm, 2 * a_w), _groups(tm, a_w),
                        pltpu.VMEM((8, c_w), F32)],
        compiler_params=_params(1),
    )(x, prm, wing, qkg, cprm, bd)


def _band_mask(first_block):
    qi = lax.broadcasted_iota(jnp.int32, (BLOCK, 2 * BLOCK), 0)
    kj = lax.broadcasted_iota(jnp.int32, (BLOCK, 2 * BLOCK), 1)
    lowest = jnp.where(first_block, BLOCK, 0)
    return (kj >= qi) & (kj <= qi + BLOCK) & (kj >= lowest)


def _blocks_per_step(seq):
    n_sb = min(ATTN_BLOCKS_PER_STEP, seq // BLOCK)
    assert seq % (n_sb * BLOCK) == 0
    return n_sb


def _block_rows(sb):
    return slice(sb * BLOCK, (sb + 1) * BLOCK)


def _with_previous(prev_ref, cur_ref, sb, sl):
    before = prev_ref[:, sl] if sb == 0 else cur_ref[_block_rows(sb - 1), sl]
    return jnp.concatenate([before, cur_ref[_block_rows(sb), sl]], axis=0)


def _heads_on_lanes(cols):
    width = LANES // len(cols)
    lane = lax.broadcasted_iota(jnp.int32, (BLOCK, LANES), 1)
    out = jnp.broadcast_to(cols[-1], (BLOCK, LANES))
    for h in range(len(cols) - 2, -1, -1):
        out = jnp.where(lane < (h + 1) * width, cols[h], out)
    return out


def _attn_fwd(qk, v, v_col, l):
    dil, seq, a2 = qk.shape
    a_w = a2 // 2
    n_sb = _blocks_per_step(seq)
    rows_step = n_sb * BLOCK
    n_pair = a_w // LANES

    def body(q_ref, kc_ref, vc_ref, o_ref, lse_ref, kp_ref, vp_ref):
        @pl.when(pl.program_id(1) == 0)
        def _():
            kp_ref[...] = jnp.zeros_like(kp_ref)
            vp_ref[...] = jnp.zeros_like(vp_ref)

        masks = [_band_mask(pl.program_id(1) == 0)] + [_band_mask(False)] * (n_sb - 1)
        low = lax.broadcasted_iota(jnp.int32, (BLOCK, LANES), 1) < HEAD_DIM
        units = [(sb, slice(p * LANES, (p + 1) * LANES)) for sb in range(n_sb) for p in range(a_w // LANES)]
        scores = []
        for sb, sl in units:
            qp = q_ref[_block_rows(sb), sl]
            k2 = _with_previous(kp_ref, kc_ref, sb, sl)
            scores.append([_dot_nt(jnp.where(half, qp, jnp.zeros_like(qp)), k2) for half in (low, ~low)])
        probs, lses = [], []
        for (sb, sl), pair in zip(units, scores):
            pr, ls = [], []
            for s in pair:
                s = jnp.where(masks[sb], s * SM_SCALE, NEG)
                mx = jnp.max(s, axis=-1, keepdims=True)
                e = jnp.exp(s - mx)
                den = jnp.sum(e, axis=-1, keepdims=True)
                pr.append((e * (1.0 / den)).astype(BF16))
                ls.append(mx + jnp.log(den))
            probs.append(pr)
            lses.append(ls)
        for (sb, sl), pr, ls in zip(units, probs, lses):
            v2 = _with_previous(vp_ref, vc_ref, sb, sl)
            o_ref[_block_rows(sb), sl] = jnp.where(low, _dot(pr[0], v2), _dot(pr[1], v2)).astype(BF16)
        for sb in range(n_sb):
            cols = [c for ls in lses[sb * n_pair:(sb + 1) * n_pair] for c in ls]
            lse_ref[_block_rows(sb), :] = _heads_on_lanes(cols)
        kp_ref[...] = kc_ref[_block_rows(n_sb - 1), :]
        vp_ref[...] = vc_ref[_block_rows(n_sb - 1), :]

    def blk(f):
        return pl.BlockSpec((None, rows_step, a_w), f)

    return pl.pallas_call(
        body, name=f"attn_fwd_{l}_d{dil}", grid=(dil, seq // rows_step),
        in_specs=[blk(lambda r, b: (r, b, 0)),
                  blk(lambda r, b: (r, b, 1)),
                  blk(lambda r, b: (r, b, v_col))],
        out_specs=[blk(lambda r, b: (r, b, 0)), pl.BlockSpec((None, rows_step, LANES), lambda r, b: (r, b, 0))],
        out_shape=[SDS((dil, seq, a_w), BF16), SDS((dil, seq, LANES), F32)],
        scratch_shapes=[pltpu.VMEM((BLOCK, a_w), BF16), pltpu.VMEM((BLOCK, a_w), BF16)],
        compiler_params=_params(2),
    )(qk, qk, v)


def _attn_combine(outs, lses, ycv, expand, tm, l):
    t_len, c_w = ycv.shape
    a_w = outs[0].shape[-1]
    dils = DILATIONS[1:]
    n_r = len(dils)

    def body(o1, l1, *rest):
        o_res, l_res, y_ref, ex_ref = rest[:n_r], rest[n_r:2 * n_r], rest[2 * n_r], rest[2 * n_r + 1]
        cat_ref, lse_ref = rest[2 * n_r + 2], rest[2 * n_r + 3]
        ya_res, lse_res = rest[2 * n_r + 4:3 * n_r + 4], rest[3 * n_r + 4:4 * n_r + 4]
        scr = rest[4 * n_r + 4:]
        so, sl, sya, slse = scr[:n_r], scr[n_r:2 * n_r], scr[2 * n_r], scr[2 * n_r + 1]
        for i, dil in enumerate(dils):
            _merge_residues(o_res[i], so[i], dil, tm)
            _merge_residues(l_res[i], sl[i], dil, tm)
        ls = [l1[...]] + [s[0] for s in sl]
        mx = jnp.maximum(jnp.maximum(ls[0], ls[1]), ls[2])
        es = [jnp.exp(t - mx) for t in ls]
        den = es[0] + es[1] + es[2]
        inv = 1.0 / den
        slse[0] = mx + jnp.log(den)
        lse_ref[...] = slse[0]
        ex = ex_ref[...]
        ov = [o1[...].astype(F32)] + [_from_groups(s) for s in so]
        ya = jnp.zeros((tm, a_w), F32)
        for e, o in zip(es, ov):
            hi, lo = _split_bf16(e * inv)
            ya = ya + (_dot(hi, ex) + _dot(lo, ex)) * o
        _to_groups(ya, sya)
        cat_ref[:, :a_w] = ya.astype(BF16)
        cat_ref[:, a_w:] = y_ref[...]
        for i, dil in enumerate(dils):
            _split_residues(sya, ya_res[i], dil, tm)
            _split_residues(slse, lse_res[i], dil, tm)

    ta = pl.BlockSpec((tm, a_w), lambda m: (m, 0))
    tl = pl.BlockSpec((tm, LANES), lambda m: (m, 0))
    res_a = [_res_spec(dil, tm, a_w) for dil in dils]
    res_l = [_res_spec(dil, tm, LANES) for dil in dils]
    return pl.pallas_call(
        body, name=f"attn_combine_{l}", grid=(t_len // tm,),
        in_specs=[ta, tl] + res_a + res_l + [pl.BlockSpec((tm, c_w), lambda m: (m, 0)),
                                            pl.BlockSpec((LANES, a_w), lambda m: (0, 0))],
        out_specs=[pl.BlockSpec((tm, a_w + c_w), lambda m: (m, 0)), tl] + res_a + res_l,
        out_shape=[SDS((t_len, a_w + c_w), BF16), SDS((t_len, LANES), F32)]
        + [SDS((dil, t_len // dil, a_w), BF16) for dil in dils]
        + [SDS((dil, t_len // dil, LANES), F32) for dil in dils],
        scratch_shapes=[_groups(tm, a_w)] * n_r + [_groups(tm, LANES)] * n_r + [_groups(tm, a_w), _groups(tm, LANES)],
        compiler_params=_params(1),
    )(outs[0].reshape(t_len, a_w), lses[0].reshape(t_len, LANES), *outs[1:], *lses[1:], ycv, expand)


def _mixer_out(cat, x, prm, woutg, l, tm):
    t_len, d_model = x.shape
    r8 = woutg.shape[-2]

    def body(cat_ref, x_ref, prm_ref, w_ref, xo_ref, y_ref):
        w = w_ref[...].reshape(N_DEV * r8, d_model)
        for rows in _row_halves(tm):
            yv = _dot(cat_ref[rows, :], w)
            y_ref[rows, :] = yv.astype(BF16)
            xo_ref[rows, :] = x_ref[rows, :] + prm_ref[3:4, :] * yv

    tile = pl.BlockSpec((tm, d_model), lambda m: (m, 0))
    return pl.pallas_call(
        body, name=f"mixer_out_{l}", grid=(t_len // tm,),
        in_specs=[tile, tile, pl.BlockSpec((8, d_model), lambda m: (0, 0)),
                  pl.BlockSpec((N_DEV, r8, d_model), lambda m: (0, 0, 0))],
        out_specs=[tile, tile],
        out_shape=[SDS((t_len, d_model), F32), SDS((t_len, d_model), BF16)],
        compiler_params=_params(1),
    )(cat, x, prm, woutg)


def _mixer_bwd_out(dxo, ymix, prm, woutg, proj, cprm, l, tm):
    t_len, d_model = dxo.shape
    a_w = d_model // 2
    c_w = d_model - a_w
    r8 = woutg.shape[-2]
    n_m = t_len // tm
    hb = tm // 8

    dils = DILATIONS[1:]

    def body(dxo_ref, ym_ref, prm_ref, w_ref, pc_ref, halo_ref, cprm_ref,
             dya_ref, dpc_ref, dy_ref, st_ref, cs_ref, *rest):
        dya_res, (carry, dsc) = rest[:len(dils)], rest[len(dils):]
        step = pl.program_id(0)
        tile_i = n_m - 1 - step

        @pl.when(step == 0)
        def _():
            st_ref[...] = jnp.zeros_like(st_ref)
            cs_ref[...] = jnp.zeros_like(cs_ref)
            carry[...] = jnp.zeros_like(carry)

        dxo_v = dxo_ref[...]
        dy = (prm_ref[3:4, :] * dxo_v).astype(BF16)
        dy_ref[...] = dy
        st_ref[3:4, :] += jnp.sum(dxo_v * ym_ref[...].astype(F32), axis=0, keepdims=True)
        w = w_ref[...].reshape(N_DEV * r8, d_model)
        dcat = jnp.concatenate([_dot_nt(dy[rows, :], w) for rows in _row_halves(tm)], axis=0)
        _to_groups(dcat[:, :a_w], dsc)
        dya_ref[...] = dcat[:, :a_w].astype(BF16)
        for i, dil in enumerate(dils):
            _split_residues(dsc, dya_res[i], dil, tm)
        dyc = dcat[:, a_w:]
        gb = pc_ref[:, :c_w].astype(F32)
        gc = pc_ref[:, c_w:2 * c_w].astype(F32)
        u = pc_ref[:, 2 * c_w:].astype(F32)
        z = gc * u
        before = halo_ref[:, c_w:2 * c_w].astype(F32) * halo_ref[:, 2 * c_w:].astype(F32)
        before = jnp.where(tile_i > 0, before, jnp.zeros_like(before))
        z1, z2 = _shift_rows(z, before, tm)
        w0, w1, w2 = cprm_ref[0:1, :], cprm_ref[1:2, :], cprm_ref[2:3, :]
        cv = w0 * z2 + w1 * z1 + w2 * z + cprm_ref[3:4, :]
        dcv = dyc * gb
        cs_ref[0:1, :] += jnp.sum(dcv * z2, axis=0, keepdims=True)
        cs_ref[1:2, :] += jnp.sum(dcv * z1, axis=0, keepdims=True)
        cs_ref[2:3, :] += jnp.sum(dcv * z, axis=0, keepdims=True)
        cs_ref[3:4, :] += jnp.sum(dcv, axis=0, keepdims=True)
        row = lax.broadcasted_iota(jnp.int32, dcv.shape, 0)
        after = carry[...]
        d1 = jnp.where(row == tm - 1, after[0:1, :], pltpu.roll(dcv, tm - 1, 0))
        d2 = jnp.where(row == tm - 2, after[0:1, :],
                       jnp.where(row == tm - 1, after[1:2, :], pltpu.roll(dcv, tm - 2, 0)))
        dz = w2 * dcv + w1 * d1 + w0 * d2
        dpc_ref[:, :c_w] = (dyc * cv).astype(BF16)
        dpc_ref[:, c_w:2 * c_w] = (dz * u).astype(BF16)
        dpc_ref[:, 2 * c_w:] = (dz * gc).astype(BF16)
        carry[...] = dcv[0:8, :]

    def rev(width, col=0):
        return pl.BlockSpec((tm, width), lambda s: (n_m - 1 - s, col))

    fixed_d = pl.BlockSpec((8, d_model), lambda s: (0, 0))
    fixed_c = pl.BlockSpec((8, c_w), lambda s: (0, 0))
    return pl.pallas_call(
        body, name=f"mixer_bwd_out_{l}", grid=(n_m,),
        in_specs=[rev(d_model), rev(d_model), fixed_d,
                  pl.BlockSpec((N_DEV, r8, d_model), lambda s: (0, 0, 0)),
                  rev(3 * c_w, 1),
                  pl.BlockSpec((8, 3 * c_w), lambda s: (jnp.maximum((n_m - 1 - s) * hb - 1, 0), 1)),
                  fixed_c],
        out_specs=[rev(a_w), rev(3 * c_w), rev(d_model), fixed_d, fixed_c]
        + [_res_spec(dil, tm, a_w, lambda s: n_m - 1 - s) for dil in dils],
        out_shape=[SDS((t_len, a_w), BF16), SDS((t_len, 3 * c_w), BF16), SDS((t_len, d_model), BF16),
                   SDS((8, d_model), F32), SDS((8, c_w), F32)]
        + [SDS((dil, t_len // dil, a_w), BF16) for dil in dils],
        scratch_shapes=[pltpu.VMEM((8, c_w), F32), _groups(tm, a_w)],
        compiler_params=_params(1),
    )(dxo, ymix, prm, woutg, proj, proj, cprm)


def _attn_bwd(qk, v, v_col, dya, ya, lse, l):
    dil, seq, a2 = qk.shape
    a_w = a2 // 2
    n_sb = _blocks_per_step(seq)
    rows_step = n_sb * BLOCK
    n_step = seq // rows_step
    last = _block_rows(n_sb - 1)
    lse_w = LANES // (a_w // HEAD_DIM)
    single = n_step == 1

    def body(q_ref, kc_ref, vc_ref, do_ref, ya_ref, lse_ref,
             dq_ref, dk_ref, dv_ref, hold_k, hold_v, acc_k, acc_v, kp_ref, vp_ref):
        j = pl.program_id(1)

        @pl.when(j == 0)
        def _():
            kp_ref[...] = jnp.zeros_like(kp_ref)
            vp_ref[...] = jnp.zeros_like(vp_ref)

        @pl.when((pl.program_id(0) == 0) & (j == 0))
        def _():
            hold_k[...] = jnp.zeros_like(hold_k)
            hold_v[...] = jnp.zeros_like(hold_v)

        @pl.when(j < n_step)
        def _():
            masks = [_band_mask(j == 0)] + [_band_mask(False)] * (n_sb - 1)
            low = lax.broadcasted_iota(jnp.int32, (BLOCK, LANES), 1) < HEAD_DIM
            units = [(sb, slice(p * LANES, (p + 1) * LANES)) for sb in range(n_sb) for p in range(a_w // LANES)]
            acc_k[...] = jnp.zeros_like(acc_k)
            acc_v[...] = jnp.zeros_like(acc_v)
            heads, scores, dps = [], [], []
            for sb, sl in units:
                qp = q_ref[_block_rows(sb), sl]
                k2 = _with_previous(kp_ref, kc_ref, sb, sl)
                v2 = _with_previous(vp_ref, vc_ref, sb, sl)
                dop = do_ref[_block_rows(sb), sl]
                prod = dop.astype(F32) * ya_ref[_block_rows(sb), sl].astype(F32)
                lsep = lse_ref[_block_rows(sb), :]
                first = 2 * (sl.start // LANES) * lse_w
                for hh, half in enumerate((low, ~low)):
                    qh = jnp.where(half, qp, jnp.zeros_like(qp))
                    doh = jnp.where(half, dop, jnp.zeros_like(dop))
                    dsum = jnp.sum(jnp.where(half, prod, 0.0), axis=-1, keepdims=True)
                    heads.append((qh, doh, dsum, lsep[:, first + hh * lse_w:first + hh * lse_w + 1], masks[sb]))
                    scores.append(_dot_nt(qh, k2))
                    dps.append(_dot_nt(doh, v2))
            dss, prs = [], []
            for (qh, doh, dsum, lse_h, valid), s, dp in zip(heads, scores, dps):
                pr = jnp.exp(jnp.where(valid, s * SM_SCALE, NEG) - lse_h)
                dss.append((pr * (dp - dsum) * SM_SCALE).astype(BF16))
                prs.append(pr.astype(BF16))
            for i, (sb, sl) in enumerate(units):
                k2 = _with_previous(kp_ref, kc_ref, sb, sl)
                h0, h1 = heads[2 * i], heads[2 * i + 1]
                dq_ref[_block_rows(sb), sl] = jnp.where(low, _dot(dss[2 * i], k2),
                                                        _dot(dss[2 * i + 1], k2)).astype(BF16)
                two = slice(sb * BLOCK, (sb + 2) * BLOCK)
                acc_k[two, sl] += _dot_tn(dss[2 * i], h0[0]) + _dot_tn(dss[2 * i + 1], h1[0])
                acc_v[two, sl] += _dot_tn(prs[2 * i], h0[1]) + _dot_tn(prs[2 * i + 1], h1[1])
            for out_ref, hold, acc in ((dk_ref, hold_k, acc_k), (dv_ref, hold_v, acc_v)):
                if single:
                    out_ref[...] = acc[BLOCK:, :].astype(BF16)
                    continue
                if n_sb > 1:
                    out_ref[:rows_step - BLOCK, :] = hold[:rows_step - BLOCK, :].astype(BF16)
                out_ref[last, :] = (hold[last, :] + acc[:BLOCK, :]).astype(BF16)
                hold[...] = acc[BLOCK:, :]
            kp_ref[...] = kc_ref[last, :]
            vp_ref[...] = vc_ref[last, :]

        @pl.when(j == n_step)
        def _():
            dk_ref[...] = hold_k[...].astype(BF16)
            dv_ref[...] = hold_v[...].astype(BF16)

    def blk(f):
        return pl.BlockSpec((None, rows_step, a_w), f)

    def cur(j):
        return jnp.minimum(j, n_step - 1)

    def late(j):
        return jnp.maximum(j - 1, 0)

    return pl.pallas_call(
        body, name=f"attn_bwd_{l}_d{dil}", grid=(dil, 1 if single else n_step + 1),
        in_specs=[blk(lambda r, j: (r, cur(j), 0)),
                  blk(lambda r, j: (r, cur(j), 1)),
                  blk(lambda r, j: (r, cur(j), v_col)),
                  blk(lambda r, j: (r, cur(j), 0)),
                  blk(lambda r, j: (r, cur(j), 0)),
                  pl.BlockSpec((None, rows_step, LANES), lambda r, j: (r, cur(j), 0))],
        out_specs=[blk(lambda r, j: (r, cur(j), 0)), blk(lambda r, j: (r, late(j), 0)),
                   blk(lambda r, j: (r, late(j), 0))],
        out_shape=[SDS((dil, seq, a_w), BF16)] * 3,
        scratch_shapes=[pltpu.VMEM((rows_step, a_w), F32), pltpu.VMEM((rows_step, a_w), F32),
                        pltpu.VMEM((rows_step + BLOCK, a_w), F32), pltpu.VMEM((rows_step + BLOCK, a_w), F32),
                        pltpu.VMEM((BLOCK, a_w), BF16), pltpu.VMEM((BLOCK, a_w), BF16)],
        compiler_params=_params(2),
    )(qk, qk, v, dya, ya, lse)


def _mixer_bwd_in(parts, proj, dpc, dxo, x_in, prm, wing, qkg, bd, l, tm):
    t_len, d_model = x_in.shape
    a_w = d_model // 2
    c_w = d_model - a_w
    c8 = wing.shape[-1]
    p_c = N_DEV * c8

    dils = DILATIONS[1:]
    n_in = 3 * len(DILATIONS)

    def body(*refs):
        nat, res = refs[:3], refs[3:n_in]
        (pa_ref, dpc_ref, dxo_ref, x_ref, prm_ref, w_ref, qkg_ref, bd_ref,
         dx_ref, dproj_ref, st_ref, qs_ref) = refs[n_in:n_in + 12]
        sums = refs[n_in + 12:]

        @pl.when(pl.program_id(0) == 0)
        def _():
            st_ref[...] = jnp.zeros_like(st_ref)
            qs_ref[...] = jnp.zeros_like(qs_ref)

        for t in range(3):
            _to_groups(nat[t][...].astype(F32), sums[t])
            for i, dil in enumerate(dils):
                _merge_residues(res[3 * i + t], sums[t], dil, tm, add=True)
        bdm = bd_ref[...]
        grads = [_from_groups(sums[0]), _from_groups(sums[1])]
        for i in range(2):
            t = pa_ref[:, i * a_w:(i + 1) * a_w].astype(F32)
            rstd = lax.rsqrt(_head_mean(t * t, bdm) + EPS)
            that = t * rstd
            qs_ref[i:i + 1, :] += jnp.sum(grads[i] * that, axis=0, keepdims=True)
            dhat = grads[i] * qkg_ref[i:i + 1, :]
            dt = rstd * (dhat - that * _head_mean(dhat * that, bdm))
            dproj_ref[:, i * a_w:(i + 1) * a_w] = dt.astype(BF16)
        dproj_ref[:, 2 * a_w:3 * a_w] = _from_groups(sums[2]).astype(BF16)
        dproj_ref[:, 3 * a_w:] = dpc_ref[...]
        dh = _dot_nt(dproj_ref[:, 0:c8], w_ref[0])
        for j in range(1, N_DEV):
            dh = dh + _dot_nt(dproj_ref[:, j * c8:(j + 1) * c8], w_ref[j])
        dx, dshift, dscale, dng = _ada_bwd(dh, x_ref[...], prm_ref)
        dx_ref[...] = dxo_ref[...] + dx
        st_ref[0:1, :] += dshift
        st_ref[1:2, :] += dscale
        st_ref[2:3, :] += dng

    ta = pl.BlockSpec((tm, a_w), lambda m: (m, 0))
    tile = pl.BlockSpec((tm, d_model), lambda m: (m, 0))
    return pl.pallas_call(
        body, name=f"mixer_bwd_in_{l}", grid=(t_len // tm,),
        in_specs=[ta] * 3 + [_res_spec(dil, tm, a_w) for dil in dils for _ in range(3)]
                 + [pl.BlockSpec((tm, 3 * a_w), lambda m: (m, 0)),
                             pl.BlockSpec((tm, 3 * c_w), lambda m: (m, 0)),
                             tile, tile, pl.BlockSpec((8, d_model), lambda m: (0, 0)),
                             pl.BlockSpec((N_DEV, d_model, c8), lambda m: (0, 0, 0)),
                             pl.BlockSpec((8, a_w), lambda m: (0, 0)),
                             pl.BlockSpec(bd.shape, lambda m: (0, 0))],
        out_specs=[tile, pl.BlockSpec((tm, p_c), lambda m: (m, 0)),
                   pl.BlockSpec((8, d_model), lambda m: (0, 0)), pl.BlockSpec((8, a_w), lambda m: (0, 0))],
        out_shape=[SDS((t_len, d_model), F32), SDS((t_len, p_c), BF16), SDS((8, d_model), F32), SDS((8, a_w), F32)],
        scratch_shapes=[_groups(tm, a_w)] * 3,
        compiler_params=_params(1),
    )(*[t.reshape(t_len, a_w) for t in parts[0]], *[t for p in parts[1:] for t in p],
      proj, dpc, dxo, x_in, prm, wing, qkg, bd)


def _loss_head(x, target, tm):
    t_len, d_model = x.shape

    def body(x_ref, t_ref, dx_ref, loss_ref):
        @pl.when(pl.program_id(0) == 0)
        def _():
            loss_ref[...] = jnp.zeros_like(loss_ref)

        diff = x_ref[...] - t_ref[...]
        dx_ref[...] = diff * (1.0 / d_model)
        per_token = jnp.sum(diff * diff, axis=-1, keepdims=True) * (1.0 / d_model)
        loss_ref[...] += 0.5 * jnp.sum(per_token)

    tile = pl.BlockSpec((tm, d_model), lambda m: (m, 0))
    return pl.pallas_call(
        body, name="loss_head", grid=(t_len // tm,),
        in_specs=[tile, tile],
        out_specs=[tile, pl.BlockSpec((8, LANES), lambda m: (0, 0))],
        out_shape=[SDS((t_len, d_model), F32), SDS((8, LANES), F32)],
        compiler_params=_params(1),
    )(x, target)


def _row_tile(rows, cols):
    best = None
    for t in range(16, rows + 1, 16):
        if rows % t == 0 and t * cols * 4 <= (1 << 20):
            best = t
    return best if best is not None else rows


def _adamw(pieces, w, m, v, name):
    n_q = len(pieces)
    n_p, rows, cols = pieces[0].shape
    tr = _row_tile(rows, cols)
    n_i = rows // tr
    c1 = 1.0 - ADAM_B1 ** ADAM_STEP
    c2 = 1.0 - ADAM_B2 ** ADAM_STEP

    def body(*refs):
        p_refs = refs[:n_q]
        w_ref, m_ref, v_ref, g_ref, d_ref, nm_ref, nv_ref = refs[n_q:]
        for q in range(n_q):
            @pl.when(pl.program_id(0) == q)
            def _():
                g = p_refs[q][0].astype(F32)
                for i in range(1, n_p):
                    g = g + p_refs[q][i].astype(F32)
                g_ref[...] = g
                nm = ADAM_B1 * m_ref[...] + (1.0 - ADAM_B1) * g
                nv = ADAM_B2 * v_ref[...] + (1.0 - ADAM_B2) * (g * g)
                nm_ref[...] = nm
                nv_ref[...] = nv
                d_ref[...] = -ADAM_LR * ((nm / c1) / (jnp.sqrt(nv / c2) + ADAM_EPS) + ADAM_WD * w_ref[...])

    def piece_spec(q):
        return pl.BlockSpec((n_p, tr, cols), lambda a, i: (0, jnp.where(a == q, i, 0), 0))

    tile = pl.BlockSpec((tr, cols), lambda a, i: (a * n_i + i, 0))
    return pl.pallas_call(
        body, name=name, grid=(n_q, n_i),
        in_specs=[piece_spec(q) for q in range(n_q)] + [tile, tile, tile],
        out_specs=[tile] * 4,
        out_shape=[SDS((n_q * rows, cols), F32)] * 4,
        compiler_params=_params(2),
    )(*pieces, w, m, v)


def _pack(vecs):
    flat = jnp.concatenate([v.reshape(-1).astype(F32) for v in vecs])
    rows = -(-flat.shape[0] // (8 * LANES)) * 8
    return jnp.pad(flat, (0, rows * LANES - flat.shape[0])).reshape(rows, LANES)


def _unpack(packed, shapes):
    lead = packed.shape[:-2]
    flat = packed.reshape(lead + (-1,))
    out, off = [], 0
    for s in shapes:
        size = 1
        for d in s:
            size *= d
        out.append(flat[..., off:off + size].reshape(lead + tuple(s)))
        off += size
    return out


def kernel(x, c, w_ada, b_ada, norm_g, w_in, q_norm_g, k_norm_g, conv_w, conv_b, w_out, ffn_w1, ffn_w2, loss_target, m_w_ada, m_b_ada, m_norm_g, m_w_in, m_q_norm_g, m_k_norm_g, m_conv_w, m_conv_b, m_w_out, m_ffn_w1, m_ffn_w2, v_w_ada, v_b_ada, v_norm_g, v_w_in, v_q_norm_g, v_k_norm_g, v_conv_w, v_conv_b, v_w_out, v_ffn_w1, v_ffn_w2):
    _, t_len, d_model = x.shape
    n_layer = w_ada.shape[0]
    a_w = d_model // 2
    c_w = d_model - a_w
    n_head = a_w // HEAD_DIM
    ada_cols = w_ada.shape[-1]
    tm = min(TOKEN_TILE, t_len)
    tm_ffn = min(FFN_TOKEN_TILE, t_len)
    me = _index(_my_place())
    x2 = x.reshape(t_len, d_model)
    target = loss_target.reshape(t_len, d_model)

    small_shapes = [(d_model,), norm_g.shape, conv_w.shape]
    gathered = _allgather_small(_pack([c, norm_g, conv_w]), "gather_small_inputs")
    c_all, ng_parts, cw_parts = _unpack(gathered, small_shapes)
    norm_g_full = jnp.moveaxis(ng_parts, 0, 2).reshape(n_layer, 3, d_model)
    conv_w_full = jnp.moveaxis(cw_parts, 0, 2).reshape(n_layer, 3, c_w)

    b_loc = lax.dynamic_slice_in_dim(b_ada, me * ada_cols, ada_cols, axis=1).reshape(n_layer, 1, ada_cols)
    mod_part, silu_c = _mod_part(c_all, w_ada, b_loc)
    mod_all = _allgather_small(_pack([mod_part]), "gather_mod")
    mod_all = _unpack(mod_all, [mod_part.shape])[0]
    mod_mine = lax.dynamic_index_in_dim(mod_all, me, axis=2, keepdims=False)
    mod = jnp.moveaxis(mod_mine, 0, 1).reshape(n_layer, 3, 3, d_model)

    def prm_of(l, sub):
        rows = jnp.stack([norm_g_full[l, sub], mod[l, sub, 0], mod[l, sub, 1], mod[l, sub, 2]])
        return jnp.pad(rows, ((0, 4), (0, 0)))

    prm = [[prm_of(l, sub) for sub in range(3)] for l in range(n_layer)]
    qkg = [jnp.pad(jnp.stack([jnp.tile(q_norm_g[l], n_head), jnp.tile(k_norm_g[l], n_head)]), ((0, 6), (0, 0)))
           for l in range(n_layer)]
    cprm = [jnp.pad(jnp.concatenate([conv_w_full[l], conv_b[l][None]]), ((0, 4), (0, 0))) for l in range(n_layer)]
    head_of = jnp.arange(min(MXU_WIDTH, a_w)) // HEAD_DIM
    bd = (head_of[:, None] == head_of[None, :]).astype(BF16)
    expand = (jnp.arange(LANES)[:, None] == (jnp.arange(a_w)[None, :] // HEAD_DIM) * (LANES // n_head)).astype(BF16)

    f8 = ffn_w1.shape[-1]
    r8 = ffn_w2.shape[-2]
    c8 = w_in.shape[-1]
    o8 = w_out.shape[-2]
    me_arr = jnp.reshape(me, (1,)).astype(jnp.int32)
    win_b = [_place_own(w_in, (l,), me_arr, f"own_w_in_{l}") for l in range(n_layer)]
    wout_b = [_place_own(w_out, (l,), me_arr, f"own_w_out_{l}") for l in range(n_layer)]
    w1_b = [[_place_own(ffn_w1, (l, s), me_arr, f"own_w1_{l}_{s}") for s in range(2)] for l in range(n_layer)]
    w2_b = [[_place_own(ffn_w2, (l, s), me_arr, f"own_w2_{l}_{s}") for s in range(2)] for l in range(n_layer)]
    w1_first, w2_first = _allgather_big([w1_b[0][0], w2_b[0][0]], "gather_first_weights")
    groups = [[win_b[0], wout_b[0]], [w1_b[0][1], w2_b[0][1]]]
    for l in range(1, n_layer):
        groups += [[w1_b[l][0], w2_b[l][0]], [win_b[l], wout_b[l]], [w1_b[l][1], w2_b[l][1]]]
    flat = [a for grp in groups for a in grp]
    w_sends, w_recvs, w_lands, w_token = _send_start(flat, None, w1_first, "weights_start")

    def gathered(gi, after):
        lo = sum(len(grp) for grp in groups[:gi])
        hi = lo + len(groups[gi])
        return _send_wait(w_lands[lo:hi], w_sends[lo:hi], w_recvs[lo:hi], after, f"weights_wait_{gi}")

    def chunked(w1g):
        return w1g.reshape(2, N_CHUNK, d_model, f8)

    saved = []
    xc = x2
    for l in range(n_layer):
        keep = {"x0": xc}
        if l == 0:
            w1g, w2g = w1_first, w2_first
            prm_first = prm[0][0] + w_token[0, 0]
        else:
            w1g, w2g = gathered(3 * l - 1, xc)
            prm_first = prm[l][0]
        keep["w_ffn0"] = (chunked(w1g), w2g)
        keep["act0"], keep["h0"], keep["y0"], xc = _ffn_fwd(xc, prm_first, *keep["w_ffn0"], l, 0, tm_ffn)
        keep["x1"] = xc
        wing, woutg = gathered(0 if l == 0 else 3 * l, xc)
        keep.update(wing=wing, woutg=woutg)
        proj, qkn, keep["h1"], ycv, *res = _mixer_in(xc, prm[l][1], wing, qkg[l], cprm[l], bd, l, tm)
        qkv = [(qkn[None], proj[None], 2)] + [(res[2 * i], res[2 * i + 1], 0) for i in range(len(DILATIONS) - 1)]
        branches = [_attn_fwd(*t, l) for t in qkv]
        cat, lse, *res = _attn_combine([b[0] for b in branches], [b[1] for b in branches], ycv, expand, tm, l)
        n_r = len(DILATIONS) - 1
        keep["ya"] = [cat[None]] + res[:n_r]
        keep["lse"] = [lse[None]] + res[n_r:]
        xc, keep["ymix"] = _mixer_out(cat, xc, prm[l][1], woutg, l, tm)
        keep.update(proj=proj, qkv=qkv, cat=cat)
        keep["x2"] = xc
        w1g, w2g = gathered(1 if l == 0 else 3 * l + 1, xc)
        keep["w_ffn2"] = (chunked(w1g), w2g)
        keep["act2"], keep["h2"], keep["y2"], xc = _ffn_fwd(xc, prm[l][2], *keep["w_ffn2"], l, 1, tm_ffn)
        saved.append(keep)

    dx, loss_blk = _loss_head(xc, target, tm)
    loss = lax.psum(loss_blk[0, 0], ("x", "y", "c"))

    tk = min(GRAD_TOKEN_TILE, t_len)
    stats =[[None] * 3 for _ in range(n_layer)]
    conv_stats = [None] * n_layer
    qk_stats = [None] * n_layer

    started = {}
    dummy = jnp.zeros((8, LANES), F32)

    def start_exchange(g, name):
        sends, recvs, thru, token = _send_start(
            [_place_own(g, None, me_arr, f"own_grad_{name}")], [g], dummy, f"grads_start_{name}")
        started[name] = (thru[0], thru[1], sends[0], recvs[0])
        return token

    def wait_exchange(names, after, name):
        ent = [started[k] for k in names]
        done = _send_wait([e[0] for e in ent] + [e[1] for e in ent], [e[2] for e in ent], [e[3] for e in ent],
                          after, name)
        return dict(zip(names, done))

    def ffn_backward(dxo, keep, l, sub, which, prm_dep):
        dx_in, da, dy, st = _ffn_bwd(dxo, keep[f"y{which}"], keep[f"x{which}"], prm_dep,
                                     keep[f"act{which}"], *keep[f"w_ffn{which}"], l, sub, tm_ffn)
        g_w2 = _matmul_tn(
            keep[f"act{which}"], dy,
            pl.BlockSpec((None, None, tk, f8), lambda p, k: (2, p, k, 0)),
            pl.BlockSpec((tk, d_model), lambda p, k: (k, 0)),
            SDS((N_CHUNK, f8, d_model), BF16), pl.BlockSpec((None, f8, d_model), lambda p, k: (p, 0, 0)),
            N_CHUNK, t_len, tk, f"grad_w2_{l}_{sub}", dummy)
        token = start_exchange(g_w2.reshape(N_DEV, r8, d_model), f"w2_{l}_{sub}")
        g_w1 = _matmul_tn(
            keep[f"h{which}"], da.reshape(N_DEV, t_len, f8),
            pl.BlockSpec((tk, d_model), lambda p, k: (k, 0)),
            pl.BlockSpec((None, tk, f8), lambda p, k: (p, k, 0)),
            SDS((N_DEV, d_model, f8), BF16), pl.BlockSpec((None, d_model, f8), lambda p, k: (p, 0, 0)),
            N_DEV, t_len, tk, f"grad_w1_{l}_{sub}", token)
        token = start_exchange(g_w1, f"w1_{l}_{sub}")
        return dx_in, st, token[0, 0]

    token = 0.0
    for l in reversed(range(n_layer)):
        keep = saved[l]
        wing, woutg = keep["wing"], keep["woutg"]
        dx, stats[l][2], token = ffn_backward(dx, keep, l, 1, 2, prm[l][2] + token)
        dya, dpc, dy, st_gate, conv_stats[l], *dya_res = _mixer_bwd_out(
            dx, keep["ymix"], prm[l][1] + token, woutg, keep["proj"], cprm[l], l, tm)
        dyas = [dya[None]] + dya_res
        g_wout = _matmul_tn(
            keep["cat"], dy,
            pl.BlockSpec((tk, d_model), lambda p, k: (k, 0)),
            pl.BlockSpec((tk, d_model), lambda p, k: (k, 0)),
            SDS((d_model, d_model), BF16), pl.BlockSpec((d_model, d_model), lambda p, k: (0, 0)),
            1, t_len, tk, f"grad_wout_{l}", dummy)
        token = start_exchange(g_wout.reshape(N_DEV, o8, d_model), f"wout_{l}")[0, 0]
        parts = [_attn_bwd(*keep["qkv"][i], dyas[i], keep["ya"][i], keep["lse"][i], l)
                 for i in range(len(DILATIONS))]
        dx, dproj, st_norm, qk_stats[l] = _mixer_bwd_in(
            parts, keep["proj"], dpc, dx, keep["x1"], prm[l][1] + token, wing, qkg[l], bd, l, tm)
        stats[l][1] = st_norm + st_gate
        g_win = _matmul_tn(
            keep["h1"], dproj,
            pl.BlockSpec((tk, d_model), lambda p, k: (k, 0)),
            pl.BlockSpec((tk, c8), lambda p, k: (k, p)),
            SDS((N_DEV, d_model, c8), BF16), pl.BlockSpec((None, d_model, c8), lambda p, k: (p, 0, 0)),
            N_DEV, t_len, tk, f"grad_win_{l}", dummy)
        token = start_exchange(g_win, f"win_{l}")[0, 0]
        dx, stats[l][0], token = ffn_backward(dx, keep, l, 0, 0, prm[l][0] + token)
    grad_x = dx.reshape(x.shape)

    def update(pieces, w, m, v, name):
        shape = w.shape
        cols = shape[-1]
        outs = _adamw([p.reshape(p.shape[0], -1, cols) for p in pieces], w.reshape(-1, cols),
                      m.reshape(-1, cols), v.reshape(-1, cols), name)
        return [o.reshape(shape) for o in outs]

    last = "w1_0_0"
    lands = wait_exchange([k for k in started if k != last], dx, "grads_wait_early")
    res = {
        "w_in": update([lands[f"win_{l}"] for l in range(n_layer)], w_in, m_w_in, v_w_in, "adamw_w_in"),
        "w_out": update([lands[f"wout_{l}"] for l in range(n_layer)], w_out, m_w_out, v_w_out, "adamw_w_out"),
        "ffn_w2": update([lands[f"w2_{l}_{s}"] for l in range(n_layer) for s in range(2)],
                         ffn_w2, m_ffn_w2, v_ffn_w2, "adamw_ffn_w2"),
    }

    dmod = jnp.stack([jnp.stack([stats[l][sub][r] for sub in range(3) for r in (0, 1, 3)]) for l in range(n_layer)])
    dng = jnp.stack([jnp.stack([stats[l][sub][2] for sub in range(3)]) for l in range(n_layer)])
    dcw = jnp.stack([conv_stats[l][0:3] for l in range(n_layer)])
    dcb = jnp.stack([conv_stats[l][3] for l in range(n_layer)])
    dqk = jnp.stack([qk_stats[l][0:2].reshape(2, n_head, HEAD_DIM).sum(axis=1) for l in range(n_layer)])
    part_shapes = [dmod.shape, dng.shape, dcw.shape, dcb.shape, dqk.shape]
    parts_all = _allgather_small(_pack([dmod, dng, dcw, dcb, dqk]), "gather_small_grads")
    dmod_all = _unpack(parts_all, part_shapes)[0].reshape(N_DEV, n_layer, 9 * d_model)
    summed = _unpack(_sum_devices(parts_all), part_shapes)
    g_b_ada = summed[0].reshape(n_layer, 9 * d_model)
    ng_cols = norm_g.shape[-1]
    g_norm_g = lax.dynamic_slice_in_dim(summed[1], me * ng_cols, ng_cols, axis=2)
    cw_cols = conv_w.shape[-1]
    g_conv_w = lax.dynamic_slice_in_dim(summed[2], me * cw_cols, cw_cols, axis=2)
    g_conv_b = summed[3]
    g_q, g_k = summed[4][:, 0], summed[4][:, 1]
    dmod_loc = lax.dynamic_slice_in_dim(dmod_all, me * ada_cols, ada_cols, axis=2)
    g_w_ada = _w_ada_grad(silu_c.T, jnp.moveaxis(dmod_loc, 0, 1))
    res["w_ada"] = update([g_w_ada[None]], w_ada, m_w_ada, v_w_ada, "adamw_w_ada")

    names = ["b_ada", "norm_g", "q_norm_g", "k_norm_g", "conv_w", "conv_b"]
    g_small = [g_b_ada, g_norm_g, g_q, g_k, g_conv_w, g_conv_b]
    w_small = [b_ada, norm_g, q_norm_g, k_norm_g, conv_w, conv_b]
    m_small = [m_b_ada, m_norm_g, m_q_norm_g, m_k_norm_g, m_conv_w, m_conv_b]
    v_small = [v_b_ada, v_norm_g, v_q_norm_g, v_k_norm_g, v_conv_w, v_conv_b]
    outs = _adamw([_pack(g_small)[None]], _pack(w_small), _pack(m_small), _pack(v_small), "adamw_small")
    shapes = [w.shape for w in w_small]
    unpacked = [_unpack(o, shapes) for o in outs]
    for i, nme in enumerate(names):
        res[nme] = [unpacked[k][i] for k in range(4)]

    follows = (res["w_in"][1][0, :1, :1] + res["w_out"][1][0, :1, :1] + res["ffn_w2"][1][0, 0, :1, :1]
               + res["w_ada"][1][0, :1, :1] + outs[1][:1, :1])
    lands.update(wait_exchange([last], follows, "grads_wait_last"))
    res["ffn_w1"] = update([lands[f"w1_{l}_{s}"] for l in range(n_layer) for s in range(2)],
                           ffn_w1, m_ffn_w1, v_ffn_w1, "adamw_ffn_w1")

    order = ["w_ada", "b_ada", "norm_g", "w_in", "q_norm_g", "k_norm_g", "conv_w", "conv_b", "w_out", "ffn_w1", "ffn_w2"]
    return (loss, grad_x, *[res[n][0] for n in order], *[res[n][1] for n in order],
            *[res[n][2] for n in order], *[res[n][3] for n in order])
```

```python
import jax
import jax.numpy as jnp
from jax import lax
from jax.experimental import pallas as pl
from jax.experimental.pallas import tpu as pltpu

F32 = jnp.float32
BF16 = jnp.bfloat16
SDS = jax.ShapeDtypeStruct

N_DEV = 8
HEAD_DIM = 64
BLOCK = 128
DILATIONS = (1, 4, 16)
ATTN_BLOCKS_PER_STEP = 4
N_CHUNK = 4
EPS = 1e-6
NEG = -1e30
SM_SCALE = HEAD_DIM ** -0.5
LANES = 128
MXU_WIDTH = 256
TOKEN_TILE = 512
FFN_TOKEN_TILE = 512
GRAD_TOKEN_TILE = 2048
VMEM_LIMIT_BYTES = 56 * 1024 * 1024

ADAM_LR = 0.001
ADAM_B1 = 0.9
ADAM_B2 = 0.999
ADAM_EPS = 1e-08
ADAM_WD = 0.01
ADAM_STEP = 10

MESH_ID = pl.DeviceIdType.MESH
ANY = pl.BlockSpec(memory_space=pl.ANY)
HBM_SPEC = pl.BlockSpec(memory_space=pltpu.HBM)
SEM_SPEC = pl.BlockSpec(memory_space=pltpu.SEMAPHORE)
SIDE_EFFECT = pltpu.SideEffectType.DATAFLOW_SIDE_EFFECTING


def _params(n_axes):
    return pltpu.CompilerParams(dimension_semantics=("arbitrary",) * n_axes,
                                vmem_limit_bytes=VMEM_LIMIT_BYTES)


def _dot(a, b):
    return jnp.dot(a, b, preferred_element_type=F32)


def _dot_nt(a, b):
    return lax.dot_general(a, b, (((1,), (1,)), ((), ())), preferred_element_type=F32)


def _dot_tn(a, b):
    return lax.dot_general(a, b, (((0,), (0,)), ((), ())), preferred_element_type=F32)


def _row_halves(n):
    return (slice(0, n // 2), slice(n // 2, n))


def _split_bf16(t):
    hi = t.astype(BF16)
    return hi, (t - hi.astype(F32)).astype(BF16)


def _my_place():
    x, y, c = lax.axis_index("x"), lax.axis_index("y"), lax.axis_index("c")
    return x, y, c


def _peer(place, k):
    x, y, c = place
    return ((1 - x) if k & 4 else x, (1 - y) if k & 2 else y, (1 - c) if k & 1 else c)


def _index(place):
    return 4 * place[0] + 2 * place[1] + place[2]


def _allgather_small(v, name):
    rows, cols = v.shape

    def body(x_ref, out_ref, send_sems, recv_sems, local_sem):
        me = _my_place()
        mine = pltpu.make_async_copy(x_ref, out_ref.at[_index(me)], local_sem)
        mine.start()
        sends = []
        for k in range(1, N_DEV):
            cp = pltpu.make_async_remote_copy(
                src_ref=x_ref, dst_ref=out_ref.at[_index(me)], send_sem=send_sems.at[k - 1],
                recv_sem=recv_sems.at[k - 1], device_id=_peer(me, k), device_id_type=MESH_ID)
            cp.start()
            sends.append(cp)
        for k in range(1, N_DEV):
            pltpu.make_async_remote_copy(
                src_ref=x_ref, dst_ref=out_ref.at[_index(_peer(me, k))], send_sem=send_sems.at[k - 1],
                recv_sem=recv_sems.at[k - 1], device_id=_peer(me, k), device_id_type=MESH_ID).wait_recv()
        for cp in sends:
            cp.wait_send()
        mine.wait()

    return pl.pallas_call(
        body, name=name,
        out_shape=SDS((N_DEV, rows, cols), v.dtype),
        in_specs=[pl.BlockSpec(memory_space=pltpu.VMEM)],
        out_specs=pl.BlockSpec(memory_space=pltpu.VMEM),
        scratch_shapes=[pltpu.SemaphoreType.DMA((N_DEV - 1,)), pltpu.SemaphoreType.DMA((N_DEV - 1,)),
                        pltpu.SemaphoreType.DMA],
    )(v)


def _allgather_big(lands, name):
    n = len(lands)

    def body(*refs):
        outs = refs[n:2 * n]
        send_sems, recv_sems = refs[2 * n:]
        x, y, c = _my_place()
        me, sib = (x, y, c), (x, y, 1 - c)
        chips = [(1 - x, y), (x, 1 - y), (1 - x, 1 - y)]

        def copy(i, k, block, to):
            dst = outs[i].at[_index(block)]
            return pltpu.make_async_remote_copy(
                src_ref=dst, dst_ref=dst, send_sem=send_sems.at[7 * i + k],
                recv_sem=recv_sems.at[7 * i + k], device_id=to, device_id_type=MESH_ID)

        first = []
        for i in range(n):
            first.append(copy(i, 0, me, sib))
            for j, chip in enumerate(chips):
                first.append(copy(i, 1 + j, me, (*chip, c)))
        for cp in first:
            cp.start()
        passed = []
        for i in range(n):
            for j, chip in enumerate(chips):
                copy(i, 1 + j, (*chip, c), me).wait_recv()
                fwd = copy(i, 4 + j, (*chip, c), sib)
                fwd.start()
                passed.append(fwd)
        for i in range(n):
            copy(i, 0, sib, me).wait_recv()
            for j, chip in enumerate(chips):
                copy(i, 4 + j, (*chip, 1 - c), me).wait_recv()
        for cp in first + passed:
            cp.wait_send()

    return pl.pallas_call(
        body, name=name,
        out_shape=[SDS(a.shape, a.dtype) for a in lands],
        in_specs=[ANY] * n, out_specs=[ANY] * n,
        input_output_aliases={i: i for i in range(n)},
        scratch_shapes=[pltpu.SemaphoreType.DMA((7 * n,)), pltpu.SemaphoreType.DMA((7 * n,))],
    )(*lands)


def _place_own(src, lead, me_arr, name):
    rows, cols = src.shape[-2:]
    tr = _row_tile(rows, cols)

    def body(me_ref, s_ref, o_ref):
        o_ref[...] = s_ref[...].astype(BF16)

    if lead is None:
        in_spec = pl.BlockSpec((None, tr, cols), lambda i, me_ref: (me_ref[0], i, 0))
    else:
        in_spec = pl.BlockSpec((None,) * len(lead) + (tr, cols), lambda i, me_ref: (*lead, i, 0))
    return pl.pallas_call(
        body, name=name,
        grid_spec=pltpu.PrefetchScalarGridSpec(
            num_scalar_prefetch=1, grid=(rows // tr,), in_specs=[in_spec],
            out_specs=pl.BlockSpec((None, tr, cols), lambda i, me_ref: (me_ref[0], i, 0))),
        out_shape=SDS((N_DEV, rows, cols), BF16),
        compiler_params=_params(1),
    )(me_arr, src)


def _send_start(lands, pieces, after, name):
    n = len(lands)
    arrs = list(lands) + list(pieces or [])
    n_a = len(arrs)

    def body(*refs):
        land_r, piece_r = refs[:n], refs[n:n_a]
        send, recv = refs[n_a + 1:n_a + 1 + n], refs[n_a + 1 + n:n_a + 1 + 2 * n]
        token = refs[-1]
        me = _my_place()
        for i in range(n):
            for k in range(1, N_DEV):
                peer = _peer(me, k)
                src = piece_r[i].at[_index(peer)] if pieces else land_r[i].at[_index(me)]
                pltpu.make_async_remote_copy(
                    src_ref=src, dst_ref=land_r[i].at[_index(me)], send_sem=send[i], recv_sem=recv[i],
                    device_id=peer, device_id_type=MESH_ID).start()
        token[...] = jnp.zeros_like(token)

    outs = pl.pallas_call(
        body, name=name,
        out_shape=[pltpu.SemaphoreType.DMA(())] * (2 * n) + [pltpu.HBM(a.shape, a.dtype) for a in arrs]
        + [SDS((8, LANES), F32)],
        in_specs=[HBM_SPEC] * n_a + [ANY],
        out_specs=[SEM_SPEC] * (2 * n) + [HBM_SPEC] * n_a + [pl.BlockSpec(memory_space=pltpu.VMEM)],
        input_output_aliases={i: 2 * n + i for i in range(n_a)},
        compiler_params=pltpu.CompilerParams(has_side_effects=SIDE_EFFECT),
    )(*[pltpu.with_memory_space_constraint(a, pltpu.HBM) for a in arrs], after)
    return outs[:n], outs[n:2 * n], outs[2 * n:2 * n + n_a], outs[-1]


def _send_wait(thru, sends, recvs, after, name):
    n = len(sends)
    n_a = len(thru)

    def body(*refs):
        land_r = refs[:n]
        send, recv = refs[n_a:n_a + n], refs[n_a + n:n_a + 2 * n]
        me = _my_place()
        for i in range(n):
            seven = land_r[i].at[pl.ds(0, N_DEV - 1)]
            cp = pltpu.make_async_remote_copy(src_ref=seven, dst_ref=seven, send_sem=send[i], recv_sem=recv[i],
                                              device_id=_peer(me, 1), device_id_type=MESH_ID)
            cp.wait_send()
            cp.wait_recv()

    outs = pl.pallas_call(
        body, name=name,
        out_shape=[pltpu.HBM(a.shape, a.dtype) for a in thru],
        in_specs=[HBM_SPEC] * n_a + [SEM_SPEC] * (2 * n) + [ANY],
        out_specs=[HBM_SPEC] * n_a,
        input_output_aliases={i: i for i in range(n_a)},
        compiler_params=pltpu.CompilerParams(has_side_effects=SIDE_EFFECT),
    )(*thru, *sends, *recvs, after)
    return outs[:n]


def _ada_fwd(x, prm_ref):
    rstd = lax.rsqrt(jnp.mean(x * x, axis=-1, keepdims=True) + EPS)
    return (x * rstd * prm_ref[0:1, :]) * (1.0 + prm_ref[2:3, :]) + prm_ref[1:2, :]


def _ada_bwd(dh, x, prm_ref):
    ng, ops = prm_ref[0:1, :], 1.0 + prm_ref[2:3, :]
    rstd = lax.rsqrt(jnp.mean(x * x, axis=-1, keepdims=True) + EPS)
    xhat = x * rstd
    dxhat = dh * (ops * ng)
    dx = rstd * (dxhat - xhat * jnp.mean(dxhat * xhat, axis=-1, keepdims=True))
    dshift = jnp.sum(dh, axis=0, keepdims=True)
    dscale = jnp.sum(dh * (xhat * ng), axis=0, keepdims=True)
    dng = jnp.sum(dh * (xhat * ops), axis=0, keepdims=True)
    return dx, dshift, dscale, dng


def _head_mean(t, bd):
    hi = t.astype(BF16)
    g = bd.shape[0]
    parts = [_dot(hi[:, i:i + g], bd) for i in range(0, t.shape[1], g)]
    return jnp.concatenate(parts, axis=1) * (1.0 / HEAD_DIM)


def _shift_rows(z, before, tm):
    row = lax.broadcasted_iota(jnp.int32, z.shape, 0)
    z1 = jnp.where(row == 0, before[7:8, :], pltpu.roll(z, 1, 0))
    z2 = jnp.where(row == 0, before[6:7, :], jnp.where(row == 1, before[7:8, :], pltpu.roll(z, 2, 0)))
    return z1, z2


def _mod_part(c_all, w_ada, b_loc):
    n_layer, d_model, cols = w_ada.shape

    def body(c_ref, w_ref, b_ref, out_ref, sc_ref):
        cv = c_ref[...]
        sc = cv * jax.nn.sigmoid(cv)
        sc_ref[...] = sc
        a_hi, a_lo = _split_bf16(sc)
        w_hi, w_lo = _split_bf16(w_ref[...])
        out_ref[...] = _dot(a_hi, w_hi) + _dot(a_hi, w_lo) + _dot(a_lo, w_hi) + b_ref[...]

    return pl.pallas_call(
        body, name="mod_part", grid=(n_layer,),
        in_specs=[pl.BlockSpec((N_DEV, d_model), lambda l: (0, 0)),
                  pl.BlockSpec((None, d_model, cols), lambda l: (l, 0, 0)),
                  pl.BlockSpec((None, 1, cols), lambda l: (l, 0, 0))],
        out_specs=[pl.BlockSpec((None, N_DEV, cols), lambda l: (l, 0, 0)),
                   pl.BlockSpec((N_DEV, d_model), lambda l: (0, 0))],
        out_shape=[SDS((n_layer, N_DEV, cols), F32), SDS((N_DEV, d_model), F32)],
        compiler_params=_params(1),
    )(c_all, w_ada, b_loc)


def _w_ada_grad(sc_t, dmod_loc):
    d_model = sc_t.shape[0]
    n_layer, _, cols = dmod_loc.shape

    def body(s_ref, d_ref, out_ref):
        acc = s_ref[:, 0:1] * d_ref[0:1, :]
        for b in range(1, N_DEV):
            acc = acc + s_ref[:, b:b + 1] * d_ref[b:b + 1, :]
        out_ref[...] = acc

    return pl.pallas_call(
        body, name="w_ada_grad", grid=(n_layer,),
        in_specs=[pl.BlockSpec((d_model, N_DEV), lambda l: (0, 0)),
                  pl.BlockSpec((None, N_DEV, cols), lambda l: (l, 0, 0))],
        out_specs=pl.BlockSpec((None, d_model, cols), lambda l: (l, 0, 0)),
        out_shape=SDS((n_layer, d_model, cols), F32),
        compiler_params=_params(1),
    )(sc_t, dmod_loc)


def _sum_devices(g):
    _, rows, cols = g.shape

    def body(g_ref, out_ref):
        acc = g_ref[0]
        for i in range(1, N_DEV):
            acc = acc + g_ref[i]
        out_ref[...] = acc

    return pl.pallas_call(body, name="sum_devices", out_shape=SDS((rows, cols), F32))(g)


def _ffn_fwd(x, prm, w1r, w2g, l, sub, tm):
    t_len, d_model = x.shape
    f8, r8 = w1r.shape[-1], w2g.shape[-2]

    def body(x_ref, prm_ref, w1_ref, w2_ref, act_ref, h_ref, y_ref, xo_ref, h_sc, acc):
        j = pl.program_id(1)

        @pl.when(j == 0)
        def _():
            h = _ada_fwd(x_ref[...], prm_ref).astype(BF16)
            h_sc[...] = h
            h_ref[...] = h
            acc[...] = jnp.zeros_like(acc)

        w2 = w2_ref[...].reshape(2 * r8, d_model)
        parts = _row_halves(tm)
        gus = [(_dot(h_sc[rows, :], w1_ref[0]), _dot(h_sc[rows, :], w1_ref[1])) for rows in parts]
        ss = []
        for rows, (g, u) in zip(parts, gus):
            sig = jax.nn.sigmoid(g)
            sil = g * sig
            s = (sil * u).astype(BF16)
            act_ref[0, rows, :] = (u * (sig * (1.0 + g * (1.0 - sig)))).astype(BF16)
            act_ref[1, rows, :] = sil.astype(BF16)
            act_ref[2, rows, :] = s
            ss.append(s)
        for rows, s in zip(parts, ss):
            acc[rows, :] += _dot(s, w2)

        @pl.when(j == N_CHUNK - 1)
        def _():
            yv = acc[...]
            y_ref[...] = yv.astype(BF16)
            xo_ref[...] = x_ref[...] + (0.5 * prm_ref[3:4, :]) * yv

    tile = pl.BlockSpec((tm, d_model), lambda m, j: (m, 0))
    return pl.pallas_call(
        body, name=f"ffn_fwd_{l}_{sub}", grid=(t_len // tm, N_CHUNK),
        in_specs=[tile, pl.BlockSpec((8, d_model), lambda m, j: (0, 0)),
                  pl.BlockSpec((2, None, d_model, f8), lambda m, j: (0, j, 0, 0)),
                  pl.BlockSpec((2, r8, d_model), lambda m, j: (j, 0, 0))],
        out_specs=[pl.BlockSpec((3, None, tm, f8), lambda m, j: (0, j, m, 0)), tile, tile, tile],
        out_shape=[SDS((3, N_CHUNK, t_len, f8), BF16), SDS((t_len, d_model), BF16),
                   SDS((t_len, d_model), BF16), SDS((t_len, d_model), F32)],
        scratch_shapes=[pltpu.VMEM((tm, d_model), BF16), pltpu.VMEM((tm, d_model), F32)],
        compiler_params=_params(2),
    )(x, prm, w1r, w2g)


def _ffn_bwd(dxo, y, x_in, prm, act, w1r, w2g, l, sub, tm):
    t_len, d_model = x_in.shape
    f8, r8 = w1r.shape[-1], w2g.shape[-2]
    n_m = t_len // tm

    def body(dxo_ref, y_ref, x_ref, prm_ref, act_ref, w1_ref, w2_ref,
             dx_ref, da_ref, dy_ref, st_ref, dy_sc, acc):
        m, j = pl.program_id(0), pl.program_id(1)

        @pl.when((m == 0) & (j == 0))
        def _():
            st_ref[...] = jnp.zeros_like(st_ref)

        @pl.when(j == 0)
        def _():
            dxo_v = dxo_ref[...]
            dy = ((0.5 * prm_ref[3:4, :]) * dxo_v).astype(BF16)
            dy_sc[...] = dy
            dy_ref[...] = dy
            st_ref[3:4, :] += jnp.sum(0.5 * dxo_v * y_ref[...].astype(F32), axis=0, keepdims=True)
            acc[...] = jnp.zeros_like(acc)

        w2 = w2_ref[...].reshape(2 * r8, d_model)
        parts = _row_halves(tm)
        dss = [_dot_nt(dy_sc[rows, :], w2) for rows in parts]
        das = []
        for rows, ds in zip(parts, dss):
            dg = (ds * act_ref[0, rows, :].astype(F32)).astype(BF16)
            du = (ds * act_ref[1, rows, :].astype(F32)).astype(BF16)
            da_ref[0, rows, :] = dg
            da_ref[1, rows, :] = du
            das.append((dg, du))
        for rows, (dg, du) in zip(parts, das):
            acc[rows, :] += _dot_nt(dg, w1_ref[0]) + _dot_nt(du, w1_ref[1])

        @pl.when(j == N_CHUNK - 1)
        def _():
            dx, dshift, dscale, dng = _ada_bwd(acc[...], x_ref[...], prm_ref)
            dx_ref[...] = dxo_ref[...] + dx
            st_ref[0:1, :] += dshift
            st_ref[1:2, :] += dscale
            st_ref[2:3, :] += dng

    tile = pl.BlockSpec((tm, d_model), lambda m, j: (m, 0))
    fixed = pl.BlockSpec((8, d_model), lambda m, j: (0, 0))
    return pl.pallas_call(
        body, name=f"ffn_bwd_{l}_{sub}", grid=(n_m, N_CHUNK),
        in_specs=[tile, tile, tile, fixed,
                  pl.BlockSpec((2, None, tm, f8), lambda m, j: (0, j, m, 0)),
                  pl.BlockSpec((2, None, d_model, f8), lambda m, j: (0, j, 0, 0)),
                  pl.BlockSpec((2, r8, d_model), lambda m, j: (j, 0, 0))],
        out_specs=[tile, pl.BlockSpec((2, None, tm, f8), lambda m, j: (0, j, m, 0)), tile, fixed],
        out_shape=[SDS((t_len, d_model), F32), SDS((2, N_CHUNK, t_len, f8), BF16),
                   SDS((t_len, d_model), BF16), SDS((8, d_model), F32)],
        scratch_shapes=[pltpu.VMEM((tm, d_model), BF16), pltpu.VMEM((tm, d_model), F32)],
        compiler_params=_params(2),
    )(dxo, y, x_in, prm, act, w1r, w2g)


def _matmul_tn(a, b, a_spec, b_spec, out_shape, out_spec, n_piece, t_len, tk, name, after):
    def body(a_ref, b_ref, after_ref, o_ref, acc):
        k = pl.program_id(1)

        @pl.when(k == 0)
        def _():
            acc[...] = jnp.zeros_like(acc)

        m_out, n_out = acc.shape
        if (m_out // 2) % LANES == 0:
            for rows in _row_halves(m_out):
                acc[rows, :] += _dot_tn(a_ref[:, rows], b_ref[...])
        else:
            for cols in _row_halves(n_out):
                acc[:, cols] += _dot_tn(a_ref[...], b_ref[:, cols])

        @pl.when(k == pl.num_programs(1) - 1)
        def _():
            if len(o_ref.shape) == 3:
                cols = o_ref.shape[2]
                for q in range(o_ref.shape[0]):
                    o_ref[q] = acc[:, q * cols:(q + 1) * cols].astype(o_ref.dtype)
            else:
                o_ref[...] = acc[...].astype(o_ref.dtype)

    blk = tuple(d for d in out_spec.block_shape if d is not None)
    acc_shape = blk if len(blk) == 2 else (blk[1], blk[0] * blk[2])
    return pl.pallas_call(
        body, name=name, grid=(n_piece, t_len // tk),
        in_specs=[a_spec, b_spec, ANY], out_specs=out_spec, out_shape=out_shape,
        scratch_shapes=[pltpu.VMEM(acc_shape, F32)],
        compiler_params=_params(2),
    )(a, b, after)


def _groups(tm, width):
    return pltpu.VMEM((width // LANES, tm, LANES), F32)


def _to_groups(val, dst_ref, first=0):
    for g in range(val.shape[1] // LANES):
        dst_ref[first + g] = val[:, g * LANES:(g + 1) * LANES]


def _from_groups(src_ref):
    return jnp.concatenate([src_ref[g] for g in range(src_ref.shape[0])], axis=1)


def _split_residues(src_ref, out_ref, dil, tm):
    for g in range(src_ref.shape[0]):
        for r in range(dil):
            out_ref[r, :, g * LANES:(g + 1) * LANES] = (
                src_ref.at[g][pl.ds(r, tm // dil, stride=dil), :].astype(out_ref.dtype))


def _merge_residues(in_ref, dst_ref, dil, tm, add=False):
    for g in range(dst_ref.shape[0]):
        for r in range(dil):
            rows = pl.ds(r, tm // dil, stride=dil)
            val = in_ref[r, :, g * LANES:(g + 1) * LANES].astype(F32)
            dst = dst_ref.at[g]
            dst[rows, :] = (dst[rows, :] + val) if add else val


def _res_spec(dil, tm, width, index=lambda m: m):
    return pl.BlockSpec((dil, tm // dil, width), lambda m: (0, index(m), 0))


def _mixer_in(x, prm, wing, qkg, cprm, bd, l, tm):
    t_len, d_model = x.shape
    a_w = d_model // 2
    c_w = d_model - a_w
    c8 = wing.shape[-1]
    p_c = N_DEV * c8
    dils = DILATIONS[1:]

    def body(x_ref, prm_ref, w_ref, qkg_ref, cprm_ref, bd_ref, proj_ref, qkn_ref, h_ref, ycv_ref, *rest):
        res_refs, (pf, qsc, vsc, carry) = rest[:2 * len(dils)], rest[2 * len(dils):]

        @pl.when(pl.program_id(0) == 0)
        def _():
            carry[...] = jnp.zeros_like(carry)

        h = _ada_fwd(x_ref[...], prm_ref).astype(BF16)
        h_ref[...] = h
        for j in range(N_DEV):
            pf[:, j * c8:(j + 1) * c8] = _dot(h, w_ref[j])
        proj_ref[...] = pf[...].astype(BF16)
        bdm = bd_ref[...]
        for i in range(2):
            t = pf[:, i * a_w:(i + 1) * a_w]
            rstd = lax.rsqrt(_head_mean(t * t, bdm) + EPS)
            tn = t * rstd * qkg_ref[i:i + 1, :]
            qkn_ref[:, i * a_w:(i + 1) * a_w] = tn.astype(BF16)
            _to_groups(tn, qsc, i * (a_w // LANES))
        _to_groups(pf[:, 2 * a_w:3 * a_w], vsc)
        for i, dil in enumerate(dils):
            _split_residues(qsc, res_refs[2 * i], dil, tm)
            _split_residues(vsc, res_refs[2 * i + 1], dil, tm)
        gb = pf[:, 3 * a_w:3 * a_w + c_w]
        z = pf[:, 3 * a_w + c_w:3 * a_w + 2 * c_w] * pf[:, 3 * a_w + 2 * c_w:3 * a_w + 3 * c_w]
        z1, z2 = _shift_rows(z, carry[...], tm)
        cv = cprm_ref[0:1, :] * z2 + cprm_ref[1:2, :] * z1 + cprm_ref[2:3, :] * z + cprm_ref[3:4, :]
        ycv_ref[...] = (gb * cv).astype(BF16)
        carry[...] = z[tm - 8:tm, :]

    res_specs, res_shapes = [], []
    for dil in dils:
        res_specs += [_res_spec(dil, tm, 2 * a_w), _res_spec(dil, tm, a_w)]
        res_shapes += [SDS((dil, t_len // dil, 2 * a_w), BF16), SDS((dil, t_len // dil, a_w), BF16)]
    return pl.pallas_call(
        body, name=f"mixer_in_{l}", grid=(t_len // tm,),
        in_specs=[pl.BlockSpec((tm, d_model), lambda m: (m, 0)),
                  pl.BlockSpec((8, d_model), lambda m: (0, 0)),
                  pl.BlockSpec((N_DEV, d_model, c8), lambda m: (0, 0, 0)),
                  pl.BlockSpec((8, a_w), lambda m: (0, 0)),
                  pl.BlockSpec((8, c_w), lambda m: (0, 0)),
                  pl.BlockSpec(bd.shape, lambda m: (0, 0))],
        out_specs=[pl.BlockSpec((tm, p_c), lambda m: (m, 0)),
                   pl.BlockSpec((tm, 2 * a_w), lambda m: (m, 0)),
                   pl.BlockSpec((tm, d_model), lambda m: (m, 0)),
                   pl.BlockSpec((tm, c_w), lambda m: (m, 0))] + res_specs,
        out_shape=[SDS((t_len, p_c), BF16), SDS((t_len, 2 * a_w), BF16),
                   SDS((t_len, d_model), BF16), SDS((t_len, c_w), BF16)] + res_shapes,
        scratch_shapes=[pltpu.VMEM((tm, p_c), F32), _groups(tm, 2 * a_w), _groups(tm, a_w),
                        pltpu.VMEM((8, c_w), F32)],
        compiler_params=_params(1),
    )(x, prm, wing, qkg, cprm, bd)


def _band_mask(first_block):
    qi = lax.broadcasted_iota(jnp.int32, (BLOCK, 2 * BLOCK), 0)
    kj = lax.broadcasted_iota(jnp.int32, (BLOCK, 2 * BLOCK), 1)
    lowest = jnp.where(first_block, BLOCK, 0)
    return (kj >= qi) & (kj <= qi + BLOCK) & (kj >= lowest)


def _blocks_per_step(seq):
    n_sb = min(ATTN_BLOCKS_PER_STEP, seq // BLOCK)
    assert seq % (n_sb * BLOCK) == 0
    return n_sb


def _block_rows(sb):
    return slice(sb * BLOCK, (sb + 1) * BLOCK)


def _with_previous(prev_ref, cur_ref, sb, sl):
    before = prev_ref[:, sl] if sb == 0 else cur_ref[_block_rows(sb - 1), sl]
    return jnp.concatenate([before, cur_ref[_block_rows(sb), sl]], axis=0)


def _heads_on_lanes(cols):
    width = LANES // len(cols)
    lane = lax.broadcasted_iota(jnp.int32, (BLOCK, LANES), 1)
    out = jnp.broadcast_to(cols[-1], (BLOCK, LANES))
    for h in range(len(cols) - 2, -1, -1):
        out = jnp.where(lane < (h + 1) * width, cols[h], out)
    return out


def _attn_fwd(qk, v, v_col, l):
    dil, seq, a2 = qk.shape
    a_w = a2 // 2
    n_sb = _blocks_per_step(seq)
    rows_step = n_sb * BLOCK
    n_pair = a_w // LANES

    def body(q_ref, kc_ref, vc_ref, o_ref, lse_ref, kp_ref, vp_ref):
        @pl.when(pl.program_id(1) == 0)
        def _():
            kp_ref[...] = jnp.zeros_like(kp_ref)
            vp_ref[...] = jnp.zeros_like(vp_ref)

        masks = [_band_mask(pl.program_id(1) == 0)] + [_band_mask(False)] * (n_sb - 1)
        low = lax.broadcasted_iota(jnp.int32, (BLOCK, LANES), 1) < HEAD_DIM
        units = [(sb, slice(p * LANES, (p + 1) * LANES)) for sb in range(n_sb) for p in range(a_w // LANES)]
        scores = []
        for sb, sl in units:
            qp = q_ref[_block_rows(sb), sl]
            k2 = _with_previous(kp_ref, kc_ref, sb, sl)
            scores.append([_dot_nt(jnp.where(half, qp, jnp.zeros_like(qp)), k2) for half in (low, ~low)])
        probs, lses = [], []
        for (sb, sl), pair in zip(units, scores):
            pr, ls = [], []
            for s in pair:
                s = jnp.where(masks[sb], s * SM_SCALE, NEG)
                mx = jnp.max(s, axis=-1, keepdims=True)
                e = jnp.exp(s - mx)
                den = jnp.sum(e, axis=-1, keepdims=True)
                pr.append((e * (1.0 / den)).astype(BF16))
                ls.append(mx + jnp.log(den))
            probs.append(pr)
            lses.append(ls)
        for (sb, sl), pr, ls in zip(units, probs, lses):
            v2 = _with_previous(vp_ref, vc_ref, sb, sl)
            o_ref[_block_rows(sb), sl] = jnp.where(low, _dot(pr[0], v2), _dot(pr[1], v2)).astype(BF16)
        for sb in range(n_sb):
            cols = [c for ls in lses[sb * n_pair:(sb + 1) * n_pair] for c in ls]
            lse_ref[_block_rows(sb), :] = _heads_on_lanes(cols)
        kp_ref[...] = kc_ref[_block_rows(n_sb - 1), :]
        vp_ref[...] = vc_ref[_block_rows(n_sb - 1), :]

    def blk(f):
        return pl.BlockSpec((None, rows_step, a_w), f)

    return pl.pallas_call(
        body, name=f"attn_fwd_{l}_d{dil}", grid=(dil, seq // rows_step),
        in_specs=[blk(lambda r, b: (r, b, 0)),
                  blk(lambda r, b: (r, b, 1)),
                  blk(lambda r, b: (r, b, v_col))],
        out_specs=[blk(lambda r, b: (r, b, 0)), pl.BlockSpec((None, rows_step, LANES), lambda r, b: (r, b, 0))],
        out_shape=[SDS((dil, seq, a_w), BF16), SDS((dil, seq, LANES), F32)],
        scratch_shapes=[pltpu.VMEM((BLOCK, a_w), BF16), pltpu.VMEM((BLOCK, a_w), BF16)],
        compiler_params=_params(2),
    )(qk, qk, v)


def _attn_combine(outs, lses, ycv, expand, tm, l):
    t_len, c_w = ycv.shape
    a_w = outs[0].shape[-1]
    dils = DILATIONS[1:]
    n_r = len(dils)

    def body(o1, l1, *rest):
        o_res, l_res, y_ref, ex_ref = rest[:n_r], rest[n_r:2 * n_r], rest[2 * n_r], rest[2 * n_r + 1]
        cat_ref, lse_ref = rest[2 * n_r + 2], rest[2 * n_r + 3]
        ya_res, lse_res = rest[2 * n_r + 4:3 * n_r + 4], rest[3 * n_r + 4:4 * n_r + 4]
        scr = rest[4 * n_r + 4:]
        so, sl, sya, slse = scr[:n_r], scr[n_r:2 * n_r], scr[2 * n_r], scr[2 * n_r + 1]
        for i, dil in enumerate(dils):
            _merge_residues(o_res[i], so[i], dil, tm)
            _merge_residues(l_res[i], sl[i], dil, tm)
        ls = [l1[...]] + [s[0] for s in sl]
        mx = jnp.maximum(jnp.maximum(ls[0], ls[1]), ls[2])
        es = [jnp.exp(t - mx) for t in ls]
        den = es[0] + es[1] + es[2]
        inv = 1.0 / den
        slse[0] = mx + jnp.log(den)
        lse_ref[...] = slse[0]
        ex = ex_ref[...]
        ov = [o1[...].astype(F32)] + [_from_groups(s) for s in so]
        ya = jnp.zeros((tm, a_w), F32)
        for e, o in zip(es, ov):
            hi, lo = _split_bf16(e * inv)
            ya = ya + (_dot(hi, ex) + _dot(lo, ex)) * o
        _to_groups(ya, sya)
        cat_ref[:, :a_w] = ya.astype(BF16)
        cat_ref[:, a_w:] = y_ref[...]
        for i, dil in enumerate(dils):
            _split_residues(sya, ya_res[i], dil, tm)
            _split_residues(slse, lse_res[i], dil, tm)

    ta = pl.BlockSpec((tm, a_w), lambda m: (m, 0))
    tl = pl.BlockSpec((tm, LANES), lambda m: (m, 0))
    res_a = [_res_spec(dil, tm, a_w) for dil in dils]
    res_l = [_res_spec(dil, tm, LANES) for dil in dils]
    return pl.pallas_call(
        body, name=f"attn_combine_{l}", grid=(t_len // tm,),
        in_specs=[ta, tl] + res_a + res_l + [pl.BlockSpec((tm, c_w), lambda m: (m, 0)),
                                            pl.BlockSpec((LANES, a_w), lambda m: (0, 0))],
        out_specs=[pl.BlockSpec((tm, a_w + c_w), lambda m: (m, 0)), tl] + res_a + res_l,
        out_shape=[SDS((t_len, a_w + c_w), BF16), SDS((t_len, LANES), F32)]
        + [SDS((dil, t_len // dil, a_w), BF16) for dil in dils]
        + [SDS((dil, t_len // dil, LANES), F32) for dil in dils],
        scratch_shapes=[_groups(tm, a_w)] * n_r + [_groups(tm, LANES)] * n_r + [_groups(tm, a_w), _groups(tm, LANES)],
        compiler_params=_params(1),
    )(outs[0].reshape(t_len, a_w), lses[0].reshape(t_len, LANES), *outs[1:], *lses[1:], ycv, expand)


def _mixer_out(cat, x, prm, woutg, l, tm):
    t_len, d_model = x.shape
    r8 = woutg.shape[-2]

    def body(cat_ref, x_ref, prm_ref, w_ref, xo_ref, y_ref):
        w = w_ref[...].reshape(N_DEV * r8, d_model)
        for rows in _row_halves(tm):
            yv = _dot(cat_ref[rows, :], w)
            y_ref[rows, :] = yv.astype(BF16)
            xo_ref[rows, :] = x_ref[rows, :] + prm_ref[3:4, :] * yv

    tile = pl.BlockSpec((tm, d_model), lambda m: (m, 0))
    return pl.pallas_call(
        body, name=f"mixer_out_{l}", grid=(t_len // tm,),
        in_specs=[tile, tile, pl.BlockSpec((8, d_model), lambda m: (0, 0)),
                  pl.BlockSpec((N_DEV, r8, d_model), lambda m: (0, 0, 0))],
        out_specs=[tile, tile],
        out_shape=[SDS((t_len, d_model), F32), SDS((t_len, d_model), BF16)],
        compiler_params=_params(1),
    )(cat, x, prm, woutg)


def _mixer_bwd_out(dxo, ymix, prm, woutg, proj, cprm, l, tm):
    t_len, d_model = dxo.shape
    a_w = d_model // 2
    c_w = d_model - a_w
    r8 = woutg.shape[-2]
    n_m = t_len // tm
    hb = tm // 8

    dils = DILATIONS[1:]

    def body(dxo_ref, ym_ref, prm_ref, w_ref, pc_ref, halo_ref, cprm_ref,
             dya_ref, dpc_ref, dy_ref, st_ref, cs_ref, *rest):
        dya_res, (carry, dsc) = rest[:len(dils)], rest[len(dils):]
        step = pl.program_id(0)
        tile_i = n_m - 1 - step

        @pl.when(step == 0)
        def _():
            st_ref[...] = jnp.zeros_like(st_ref)
            cs_ref[...] = jnp.zeros_like(cs_ref)
            carry[...] = jnp.zeros_like(carry)

        dxo_v = dxo_ref[...]
        dy = (prm_ref[3:4, :] * dxo_v).astype(BF16)
        dy_ref[...] = dy
        st_ref[3:4, :] += jnp.sum(dxo_v * ym_ref[...].astype(F32), axis=0, keepdims=True)
        w = w_ref[...].reshape(N_DEV * r8, d_model)
        dcat = jnp.concatenate([_dot_nt(dy[rows, :], w) for rows in _row_halves(tm)], axis=0)
        _to_groups(dcat[:, :a_w], dsc)
        dya_ref[...] = dcat[:, :a_w].astype(BF16)
        for i, dil in enumerate(dils):
            _split_residues(dsc, dya_res[i], dil, tm)
        dyc = dcat[:, a_w:]
        gb = pc_ref[:, :c_w].astype(F32)
        gc = pc_ref[:, c_w:2 * c_w].astype(F32)
        u = pc_ref[:, 2 * c_w:].astype(F32)
        z = gc * u
        before = halo_ref[:, c_w:2 * c_w].astype(F32) * halo_ref[:, 2 * c_w:].astype(F32)
        before = jnp.where(tile_i > 0, before, jnp.zeros_like(before))
        z1, z2 = _shift_rows(z, before, tm)
        w0, w1, w2 = cprm_ref[0:1, :], cprm_ref[1:2, :], cprm_ref[2:3, :]
        cv = w0 * z2 + w1 * z1 + w2 * z + cprm_ref[3:4, :]
        dcv = dyc * gb
        cs_ref[0:1, :] += jnp.sum(dcv * z2, axis=0, keepdims=True)
        cs_ref[1:2, :] += jnp.sum(dcv * z1, axis=0, keepdims=True)
        cs_ref[2:3, :] += jnp.sum(dcv * z, axis=0, keepdims=True)
        cs_ref[3:4, :] += jnp.sum(dcv, axis=0, keepdims=True)
        row = lax.broadcasted_iota(jnp.int32, dcv.shape, 0)
        after = carry[...]
        d1 = jnp.where(row == tm - 1, after[0:1, :], pltpu.roll(dcv, tm - 1, 0))
        d2 = jnp.where(row == tm - 2, after[0:1, :],
                       jnp.where(row == tm - 1, after[1:2, :], pltpu.roll(dcv, tm - 2, 0)))
        dz = w2 * dcv + w1 * d1 + w0 * d2
        dpc_ref[:, :c_w] = (dyc * cv).astype(BF16)
        dpc_ref[:, c_w:2 * c_w] = (dz * u).astype(BF16)
        dpc_ref[:, 2 * c_w:] = (dz * gc).astype(BF16)
        carry[...] = dcv[0:8, :]

    def rev(width, col=0):
        return pl.BlockSpec((tm, width), lambda s: (n_m - 1 - s, col))

    fixed_d = pl.BlockSpec((8, d_model), lambda s: (0, 0))
    fixed_c = pl.BlockSpec((8, c_w), lambda s: (0, 0))
    return pl.pallas_call(
        body, name=f"mixer_bwd_out_{l}", grid=(n_m,),
        in_specs=[rev(d_model), rev(d_model), fixed_d,
                  pl.BlockSpec((N_DEV, r8, d_model), lambda s: (0, 0, 0)),
                  rev(3 * c_w, 1),
                  pl.BlockSpec((8, 3 * c_w), lambda s: (jnp.maximum((n_m - 1 - s) * hb - 1, 0), 1)),
                  fixed_c],
        out_specs=[rev(a_w), rev(3 * c_w), rev(d_model), fixed_d, fixed_c]
        + [_res_spec(dil, tm, a_w, lambda s: n_m - 1 - s) for dil in dils],
        out_shape=[SDS((t_len, a_w), BF16), SDS((t_len, 3 * c_w), BF16), SDS((t_len, d_model), BF16),
                   SDS((8, d_model), F32), SDS((8, c_w), F32)]
        + [SDS((dil, t_len // dil, a_w), BF16) for dil in dils],
        scratch_shapes=[pltpu.VMEM((8, c_w), F32), _groups(tm, a_w)],
        compiler_params=_params(1),
    )(dxo, ymix, prm, woutg, proj, proj, cprm)


def _attn_bwd(qk, v, v_col, dya, ya, lse, l):
    dil, seq, a2 = qk.shape
    a_w = a2 // 2
    n_sb = _blocks_per_step(seq)
    rows_step = n_sb * BLOCK
    n_step = seq // rows_step
    last = _block_rows(n_sb - 1)
    lse_w = LANES // (a_w // HEAD_DIM)
    single = n_step == 1

    def body(q_ref, kc_ref, vc_ref, do_ref, ya_ref, lse_ref,
             dq_ref, dk_ref, dv_ref, hold_k, hold_v, acc_k, acc_v, kp_ref, vp_ref):
        j = pl.program_id(1)

        @pl.when(j == 0)
        def _():
            kp_ref[...] = jnp.zeros_like(kp_ref)
            vp_ref[...] = jnp.zeros_like(vp_ref)

        @pl.when((pl.program_id(0) == 0) & (j == 0))
        def _():
            hold_k[...] = jnp.zeros_like(hold_k)
            hold_v[...] = jnp.zeros_like(hold_v)

        @pl.when(j < n_step)
        def _():
            masks = [_band_mask(j == 0)] + [_band_mask(False)] * (n_sb - 1)
            low = lax.broadcasted_iota(jnp.int32, (BLOCK, LANES), 1) < HEAD_DIM
            units = [(sb, slice(p * LANES, (p + 1) * LANES)) for sb in range(n_sb) for p in range(a_w // LANES)]
            acc_k[...] = jnp.zeros_like(acc_k)
            acc_v[...] = jnp.zeros_like(acc_v)
            heads, scores, dps = [], [], []
            for sb, sl in units:
                qp = q_ref[_block_rows(sb), sl]
                k2 = _with_previous(kp_ref, kc_ref, sb, sl)
                v2 = _with_previous(vp_ref, vc_ref, sb, sl)
                dop = do_ref[_block_rows(sb), sl]
                prod = dop.astype(F32) * ya_ref[_block_rows(sb), sl].astype(F32)
                lsep = lse_ref[_block_rows(sb), :]
                first = 2 * (sl.start // LANES) * lse_w
                for hh, half in enumerate((low, ~low)):
                    qh = jnp.where(half, qp, jnp.zeros_like(qp))
                    doh = jnp.where(half, dop, jnp.zeros_like(dop))
                    dsum = jnp.sum(jnp.where(half, prod, 0.0), axis=-1, keepdims=True)
                    heads.append((qh, doh, dsum, lsep[:, first + hh * lse_w:first + hh * lse_w + 1], masks[sb]))
                    scores.append(_dot_nt(qh, k2))
                    dps.append(_dot_nt(doh, v2))
            dss, prs = [], []
            for (qh, doh, dsum, lse_h, valid), s, dp in zip(heads, scores, dps):
                pr = jnp.exp(jnp.where(valid, s * SM_SCALE, NEG) - lse_h)
                dss.append((pr * (dp - dsum) * SM_SCALE).astype(BF16))
                prs.append(pr.astype(BF16))
            for i, (sb, sl) in enumerate(units):
                k2 = _with_previous(kp_ref, kc_ref, sb, sl)
                h0, h1 = heads[2 * i], heads[2 * i + 1]
                dq_ref[_block_rows(sb), sl] = jnp.where(low, _dot(dss[2 * i], k2),
                                                        _dot(dss[2 * i + 1], k2)).astype(BF16)
                two = slice(sb * BLOCK, (sb + 2) * BLOCK)
                acc_k[two, sl] += _dot_tn(dss[2 * i], h0[0]) + _dot_tn(dss[2 * i + 1], h1[0])
                acc_v[two, sl] += _dot_tn(prs[2 * i], h0[1]) + _dot_tn(prs[2 * i + 1], h1[1])
            for out_ref, hold, acc in ((dk_ref, hold_k, acc_k), (dv_ref, hold_v, acc_v)):
                if single:
                    out_ref[...] = acc[BLOCK:, :].astype(BF16)
                    continue
                if n_sb > 1:
                    out_ref[:rows_step - BLOCK, :] = hold[:rows_step - BLOCK, :].astype(BF16)
                out_ref[last, :] = (hold[last, :] + acc[:BLOCK, :]).astype(BF16)
                hold[...] = acc[BLOCK:, :]
            kp_ref[...] = kc_ref[last, :]
            vp_ref[...] = vc_ref[last, :]

        @pl.when(j == n_step)
        def _():
            dk_ref[...] = hold_k[...].astype(BF16)
            dv_ref[...] = hold_v[...].astype(BF16)

    def blk(f):
        return pl.BlockSpec((None, rows_step, a_w), f)

    def cur(j):
        return jnp.minimum(j, n_step - 1)

    def late(j):
        return jnp.maximum(j - 1, 0)

    return pl.pallas_call(
        body, name=f"attn_bwd_{l}_d{dil}", grid=(dil, 1 if single else n_step + 1),
        in_specs=[blk(lambda r, j: (r, cur(j), 0)),
                  blk(lambda r, j: (r, cur(j), 1)),
                  blk(lambda r, j: (r, cur(j), v_col)),
                  blk(lambda r, j: (r, cur(j), 0)),
                  blk(lambda r, j: (r, cur(j), 0)),
                  pl.BlockSpec((None, rows_step, LANES), lambda r, j: (r, cur(j), 0))],
        out_specs=[blk(lambda r, j: (r, cur(j), 0)), blk(lambda r, j: (r, late(j), 0)),
                   blk(lambda r, j: (r, late(j), 0))],
        out_shape=[SDS((dil, seq, a_w), BF16)] * 3,
        scratch_shapes=[pltpu.VMEM((rows_step, a_w), F32), pltpu.VMEM((rows_step, a_w), F32),
                        pltpu.VMEM((rows_step + BLOCK, a_w), F32), pltpu.VMEM((rows_step + BLOCK, a_w), F32),
                        pltpu.VMEM((BLOCK, a_w), BF16), pltpu.VMEM((BLOCK, a_w), BF16)],
        compiler_params=_params(2),
    )(qk, qk, v, dya, ya, lse)


def _mixer_bwd_in(parts, proj, dpc, dxo, x_in, prm, wing, qkg, bd, l, tm):
    t_len, d_model = x_in.shape
    a_w = d_model // 2
    c_w = d_model - a_w
    c8 = wing.shape[-1]
    p_c = N_DEV * c8

    dils = DILATIONS[1:]
    n_in = 3 * len(DILATIONS)

    def body(*refs):
        nat, res = refs[:3], refs[3:n_in]
        (pa_ref, dpc_ref, dxo_ref, x_ref, prm_ref, w_ref, qkg_ref, bd_ref,
         dx_ref, dproj_ref, st_ref, qs_ref) = refs[n_in:n_in + 12]
        sums = refs[n_in + 12:]

        @pl.when(pl.program_id(0) == 0)
        def _():
            st_ref[...] = jnp.zeros_like(st_ref)
            qs_ref[...] = jnp.zeros_like(qs_ref)

        for t in range(3):
            _to_groups(nat[t][...].astype(F32), sums[t])
            for i, dil in enumerate(dils):
                _merge_residues(res[3 * i + t], sums[t], dil, tm, add=True)
        bdm = bd_ref[...]
        grads = [_from_groups(sums[0]), _from_groups(sums[1])]
        for i in range(2):
            t = pa_ref[:, i * a_w:(i + 1) * a_w].astype(F32)
            rstd = lax.rsqrt(_head_mean(t * t, bdm) + EPS)
            that = t * rstd
            qs_ref[i:i + 1, :] += jnp.sum(grads[i] * that, axis=0, keepdims=True)
            dhat = grads[i] * qkg_ref[i:i + 1, :]
            dt = rstd * (dhat - that * _head_mean(dhat * that, bdm))
            dproj_ref[:, i * a_w:(i + 1) * a_w] = dt.astype(BF16)
        dproj_ref[:, 2 * a_w:3 * a_w] = _from_groups(sums[2]).astype(BF16)
        dproj_ref[:, 3 * a_w:] = dpc_ref[...]
        dh = _dot_nt(dproj_ref[:, 0:c8], w_ref[0])
        for j in range(1, N_DEV):
            dh = dh + _dot_nt(dproj_ref[:, j * c8:(j + 1) * c8], w_ref[j])
        dx, dshift, dscale, dng = _ada_bwd(dh, x_ref[...], prm_ref)
        dx_ref[...] = dxo_ref[...] + dx
        st_ref[0:1, :] += dshift
        st_ref[1:2, :] += dscale
        st_ref[2:3, :] += dng

    ta = pl.BlockSpec((tm, a_w), lambda m: (m, 0))
    tile = pl.BlockSpec((tm, d_model), lambda m: (m, 0))
    return pl.pallas_call(
        body, name=f"mixer_bwd_in_{l}", grid=(t_len // tm,),
        in_specs=[ta] * 3 + [_res_spec(dil, tm, a_w) for dil in dils for _ in range(3)]
                 + [pl.BlockSpec((tm, 3 * a_w), lambda m: (m, 0)),
                             pl.BlockSpec((tm, 3 * c_w), lambda m: (m, 0)),
                             tile, tile, pl.BlockSpec((8, d_model), lambda m: (0, 0)),
                             pl.BlockSpec((N_DEV, d_model, c8), lambda m: (0, 0, 0)),
                             pl.BlockSpec((8, a_w), lambda m: (0, 0)),
                             pl.BlockSpec(bd.shape, lambda m: (0, 0))],
        out_specs=[tile, pl.BlockSpec((tm, p_c), lambda m: (m, 0)),
                   pl.BlockSpec((8, d_model), lambda m: (0, 0)), pl.BlockSpec((8, a_w), lambda m: (0, 0))],
        out_shape=[SDS((t_len, d_model), F32), SDS((t_len, p_c), BF16), SDS((8, d_model), F32), SDS((8, a_w), F32)],
        scratch_shapes=[_groups(tm, a_w)] * 3,
        compiler_params=_params(1),
    )(*[t.reshape(t_len, a_w) for t in parts[0]], *[t for p in parts[1:] for t in p],
      proj, dpc, dxo, x_in, prm, wing, qkg, bd)


def _loss_head(x, target, tm):
    t_len, d_model = x.shape

    def body(x_ref, t_ref, dx_ref, loss_ref):
        @pl.when(pl.program_id(0) == 0)
        def _():
            loss_ref[...] = jnp.zeros_like(loss_ref)

        diff = x_ref[...] - t_ref[...]
        dx_ref[...] = diff * (1.0 / d_model)
        per_token = jnp.sum(diff * diff, axis=-1, keepdims=True) * (1.0 / d_model)
        loss_ref[...] += 0.5 * jnp.sum(per_token)

    tile = pl.BlockSpec((tm, d_model), lambda m: (m, 0))
    return pl.pallas_call(
        body, name="loss_head", grid=(t_len // tm,),
        in_specs=[tile, tile],
        out_specs=[tile, pl.BlockSpec((8, LANES), lambda m: (0, 0))],
        out_shape=[SDS((t_len, d_model), F32), SDS((8, LANES), F32)],
        compiler_params=_params(1),
    )(x, target)


def _row_tile(rows, cols):
    best = None
    for t in range(16, rows + 1, 16):
        if rows % t == 0 and t * cols * 4 <= (1 << 20):
            best = t
    return best if best is not None else rows


def _adamw(pieces, w, m, v, name):
    n_q = len(pieces)
    n_p, rows, cols = pieces[0].shape
    tr = _row_tile(rows, cols)
    n_i = rows // tr
    c1 = 1.0 - ADAM_B1 ** ADAM_STEP
    c2 = 1.0 - ADAM_B2 ** ADAM_STEP

    def body(*refs):
        p_refs = refs[:n_q]
        w_ref, m_ref, v_ref, g_ref, d_ref, nm_ref, nv_ref = refs[n_q:]
        for q in range(n_q):
            @pl.when(pl.program_id(0) == q)
            def _():
                g = p_refs[q][0].astype(F32)
                for i in range(1, n_p):
                    g = g + p_refs[q][i].astype(F32)
                g_ref[...] = g
                nm = ADAM_B1 * m_ref[...] + (1.0 - ADAM_B1) * g
                nv = ADAM_B2 * v_ref[...] + (1.0 - ADAM_B2) * (g * g)
                nm_ref[...] = nm
                nv_ref[...] = nv
                d_ref[...] = -ADAM_LR * ((nm / c1) / (jnp.sqrt(nv / c2) + ADAM_EPS) + ADAM_WD * w_ref[...])

    def piece_spec(q):
        return pl.BlockSpec((n_p, tr, cols), lambda a, i: (0, jnp.where(a == q, i, 0), 0))

    tile = pl.BlockSpec((tr, cols), lambda a, i: (a * n_i + i, 0))
    return pl.pallas_call(
        body, name=name, grid=(n_q, n_i),
        in_specs=[piece_spec(q) for q in range(n_q)] + [tile, tile, tile],
        out_specs=[tile] * 4,
        out_shape=[SDS((n_q * rows, cols), F32)] * 4,
        compiler_params=_params(2),
    )(*pieces, w, m, v)


def _pack(vecs):
    flat = jnp.concatenate([v.reshape(-1).astype(F32) for v in vecs])
    rows = -(-flat.shape[0] // (8 * LANES)) * 8
    return jnp.pad(flat, (0, rows * LANES - flat.shape[0])).reshape(rows, LANES)


def _unpack(packed, shapes):
    lead = packed.shape[:-2]
    flat = packed.reshape(lead + (-1,))
    out, off = [], 0
    for s in shapes:
        size = 1
        for d in s:
            size *= d
        out.append(flat[..., off:off + size].reshape(lead + tuple(s)))
        off += size
    return out


def kernel(x, c, w_ada, b_ada, norm_g, w_in, q_norm_g, k_norm_g, conv_w, conv_b, w_out, ffn_w1, ffn_w2, loss_target, m_w_ada, m_b_ada, m_norm_g, m_w_in, m_q_norm_g, m_k_norm_g, m_conv_w, m_conv_b, m_w_out, m_ffn_w1, m_ffn_w2, v_w_ada, v_b_ada, v_norm_g, v_w_in, v_q_norm_g, v_k_norm_g, v_conv_w, v_conv_b, v_w_out, v_ffn_w1, v_ffn_w2):
    _, t_len, d_model = x.shape
    n_layer = w_ada.shape[0]
    a_w = d_model // 2
    c_w = d_model - a_w
    n_head = a_w // HEAD_DIM
    ada_cols = w_ada.shape[-1]
    tm = min(TOKEN_TILE, t_len)
    tm_ffn = min(FFN_TOKEN_TILE, t_len)
    me = _index(_my_place())
    x2 = x.reshape(t_len, d_model)
    target = loss_target.reshape(t_len, d_model)

    small_shapes = [(d_model,), norm_g.shape, conv_w.shape]
    gathered = _allgather_small(_pack([c, norm_g, conv_w]), "gather_small_inputs")
    c_all, ng_parts, cw_parts = _unpack(gathered, small_shapes)
    norm_g_full = jnp.moveaxis(ng_parts, 0, 2).reshape(n_layer, 3, d_model)
    conv_w_full = jnp.moveaxis(cw_parts, 0, 2).reshape(n_layer, 3, c_w)

    b_loc = lax.dynamic_slice_in_dim(b_ada, me * ada_cols, ada_cols, axis=1).reshape(n_layer, 1, ada_cols)
    mod_part, silu_c = _mod_part(c_all, w_ada, b_loc)
    mod_all = _allgather_small(_pack([mod_part]), "gather_mod")
    mod_all = _unpack(mod_all, [mod_part.shape])[0]
    mod_mine = lax.dynamic_index_in_dim(mod_all, me, axis=2, keepdims=False)
    mod = jnp.moveaxis(mod_mine, 0, 1).reshape(n_layer, 3, 3, d_model)

    def prm_of(l, sub):
        rows = jnp.stack([norm_g_full[l, sub], mod[l, sub, 0], mod[l, sub, 1], mod[l, sub, 2]])
        return jnp.pad(rows, ((0, 4), (0, 0)))

    prm = [[prm_of(l, sub) for sub in range(3)] for l in range(n_layer)]
    qkg = [jnp.pad(jnp.stack([jnp.tile(q_norm_g[l], n_head), jnp.tile(k_norm_g[l], n_head)]), ((0, 6), (0, 0)))
           for l in range(n_layer)]
    cprm = [jnp.pad(jnp.concatenate([conv_w_full[l], conv_b[l][None]]), ((0, 4), (0, 0))) for l in range(n_layer)]
    head_of = jnp.arange(min(MXU_WIDTH, a_w)) // HEAD_DIM
    bd = (head_of[:, None] == head_of[None, :]).astype(BF16)
    expand = (jnp.arange(LANES)[:, None] == (jnp.arange(a_w)[None, :] // HEAD_DIM) * (LANES // n_head)).astype(BF16)

    f8 = ffn_w1.shape[-1]
    r8 = ffn_w2.shape[-2]
    c8 = w_in.shape[-1]
    o8 = w_out.shape[-2]
    me_arr = jnp.reshape(me, (1,)).astype(jnp.int32)
    win_b = [_place_own(w_in, (l,), me_arr, f"own_w_in_{l}") for l in range(n_layer)]
    wout_b = [_place_own(w_out, (l,), me_arr, f"own_w_out_{l}") for l in range(n_layer)]
    w1_b = [[_place_own(ffn_w1, (l, s), me_arr, f"own_w1_{l}_{s}") for s in range(2)] for l in range(n_layer)]
    w2_b = [[_place_own(ffn_w2, (l, s), me_arr, f"own_w2_{l}_{s}") for s in range(2)] for l in range(n_layer)]
    w1_first, w2_first = _allgather_big([w1_b[0][0], w2_b[0][0]], "gather_first_weights")
    groups = [[win_b[0], wout_b[0]], [w1_b[0][1], w2_b[0][1]]]
    for l in range(1, n_layer):
        groups += [[w1_b[l][0], w2_b[l][0]], [win_b[l], wout_b[l]], [w1_b[l][1], w2_b[l][1]]]
    flat = [a for grp in groups for a in grp]
    w_sends, w_recvs, w_lands, w_token = _send_start(flat, None, w1_first, "weights_start")

    def gathered(gi, after):
        lo = sum(len(grp) for grp in groups[:gi])
        hi = lo + len(groups[gi])
        return _send_wait(w_lands[lo:hi], w_sends[lo:hi], w_recvs[lo:hi], after, f"weights_wait_{gi}")

    def chunked(w1g):
        return w1g.reshape(2, N_CHUNK, d_model, f8)

    saved = []
    xc = x2
    for l in range(n_layer):
        keep = {"x0": xc}
        if l == 0:
            w1g, w2g = w1_first, w2_first
            prm_first = prm[0][0] + w_token[0, 0]
        else:
            w1g, w2g = gathered(3 * l - 1, xc)
            prm_first = prm[l][0]
        keep["w_ffn0"] = (chunked(w1g), w2g)
        keep["act0"], keep["h0"], keep["y0"], xc = _ffn_fwd(xc, prm_first, *keep["w_ffn0"], l, 0, tm_ffn)
        keep["x1"] = xc
        wing, woutg = gathered(0 if l == 0 else 3 * l, xc)
        keep.update(wing=wing, woutg=woutg)
        proj, qkn, keep["h1"], ycv, *res = _mixer_in(xc, prm[l][1], wing, qkg[l], cprm[l], bd, l, tm)
        qkv = [(qkn[None], proj[None], 2)] + [(res[2 * i], res[2 * i + 1], 0) for i in range(len(DILATIONS) - 1)]
        branches = [_attn_fwd(*t, l) for t in qkv]
        cat, lse, *res = _attn_combine([b[0] for b in branches], [b[1] for b in branches], ycv, expand, tm, l)
        n_r = len(DILATIONS) - 1
        keep["ya"] = [cat[None]] + res[:n_r]
        keep["lse"] = [lse[None]] + res[n_r:]
        xc, keep["ymix"] = _mixer_out(cat, xc, prm[l][1], woutg, l, tm)
        keep.update(proj=proj, qkv=qkv, cat=cat)
        keep["x2"] = xc
        w1g, w2g = gathered(1 if l == 0 else 3 * l + 1, xc)
        keep["w_ffn2"] = (chunked(w1g), w2g)
        keep["act2"], keep["h2"], keep["y2"], xc = _ffn_fwd(xc, prm[l][2], *keep["w_ffn2"], l, 1, tm_ffn)
        saved.append(keep)

    dx, loss_blk = _loss_head(xc, target, tm)
    loss = lax.psum(loss_blk[0, 0], ("x", "y", "c"))

    tk = min(GRAD_TOKEN_TILE, t_len)
    stats =[[None] * 3 for _ in range(n_layer)]
    conv_stats = [None] * n_layer
    qk_stats = [None] * n_layer

    started = {}
    dummy = jnp.zeros((8, LANES), F32)

    def start_exchange(g, name):
        sends, recvs, thru, token = _send_start(
            [_place_own(g, None, me_arr, f"own_grad_{name}")], [g], dummy, f"grads_start_{name}")
        started[name] = (thru[0], thru[1], sends[0], recvs[0])
        return token

    def wait_exchange(names, after, name):
        ent = [started[k] for k in names]
        done = _send_wait([e[0] for e in ent] + [e[1] for e in ent], [e[2] for e in ent], [e[3] for e in ent],
                          after, name)
        return dict(zip(names, done))

    def ffn_backward(dxo, keep, l, sub, which, prm_dep):
        dx_in, da, dy, st = _ffn_bwd(dxo, keep[f"y{which}"], keep[f"x{which}"], prm_dep,
                                     keep[f"act{which}"], *keep[f"w_ffn{which}"], l, sub, tm_ffn)
        g_w2 = _matmul_tn(
            keep[f"act{which}"], dy,
            pl.BlockSpec((None, None, tk, f8), lambda p, k: (2, p, k, 0)),
            pl.BlockSpec((tk, d_model), lambda p, k: (k, 0)),
            SDS((N_CHUNK, f8, d_model), BF16), pl.BlockSpec((None, f8, d_model), lambda p, k: (p, 0, 0)),
            N_CHUNK, t_len, tk, f"grad_w2_{l}_{sub}", dummy)
        token = start_exchange(g_w2.reshape(N_DEV, r8, d_model), f"w2_{l}_{sub}")
        g_w1 = _matmul_tn(
            keep[f"h{which}"], da.reshape(N_DEV, t_len, f8),
            pl.BlockSpec((tk, d_model), lambda p, k: (k, 0)),
            pl.BlockSpec((None, tk, f8), lambda p, k: (p, k, 0)),
            SDS((N_DEV, d_model, f8), BF16), pl.BlockSpec((None, d_model, f8), lambda p, k: (p, 0, 0)),
            N_DEV, t_len, tk, f"grad_w1_{l}_{sub}", token)
        token = start_exchange(g_w1, f"w1_{l}_{sub}")
        return dx_in, st, token[0, 0]

    token = 0.0
    for l in reversed(range(n_layer)):
        keep = saved[l]
        wing, woutg = keep["wing"], keep["woutg"]
        dx, stats[l][2], token = ffn_backward(dx, keep, l, 1, 2, prm[l][2] + token)
        dya, dpc, dy, st_gate, conv_stats[l], *dya_res = _mixer_bwd_out(
            dx, keep["ymix"], prm[l][1] + token, woutg, keep["proj"], cprm[l], l, tm)
        dyas = [dya[None]] + dya_res
        g_wout = _matmul_tn(
            keep["cat"], dy,
            pl.BlockSpec((tk, d_model), lambda p, k: (k, 0)),
            pl.BlockSpec((tk, d_model), lambda p, k: (k, 0)),
            SDS((d_model, d_model), BF16), pl.BlockSpec((d_model, d_model), lambda p, k: (0, 0)),
            1, t_len, tk, f"grad_wout_{l}", dummy)
        token = start_exchange(g_wout.reshape(N_DEV, o8, d_model), f"wout_{l}")[0, 0]
        parts = [_attn_bwd(*keep["qkv"][i], dyas[i], keep["ya"][i], keep["lse"][i], l)
                 for i in range(len(DILATIONS))]
        dx, dproj, st_norm, qk_stats[l] = _mixer_bwd_in(
            parts, keep["proj"], dpc, dx, keep["x1"], prm[l][1] + token, wing, qkg[l], bd, l, tm)
        stats[l][1] = st_norm + st_gate
        g_win = _matmul_tn(
            keep["h1"], dproj,
            pl.BlockSpec((tk, d_model), lambda p, k: (k, 0)),
            pl.BlockSpec((tk, N_DEV // 2 * c8), lambda p, k: (k, p)),
            SDS((N_DEV, d_model, c8), BF16), pl.BlockSpec((N_DEV // 2, d_model, c8), lambda p, k: (p, 0, 0)),
            2, t_len, tk, f"grad_win_{l}", dummy)
        token = start_exchange(g_win, f"win_{l}")[0, 0]
        dx, stats[l][0], token = ffn_backward(dx, keep, l, 0, 0, prm[l][0] + token)
    grad_x = dx.reshape(x.shape)

    def update(pieces, w, m, v, name):
        shape = w.shape
        cols = shape[-1]
        outs = _adamw([p.reshape(p.shape[0], -1, cols) for p in pieces], w.reshape(-1, cols),
                      m.reshape(-1, cols), v.reshape(-1, cols), name)
        return [o.reshape(shape) for o in outs]

    last = "w1_0_0"
    lands = wait_exchange([k for k in started if k != last], dx, "grads_wait_early")
    res = {
        "w_in": update([lands[f"win_{l}"] for l in range(n_layer)], w_in, m_w_in, v_w_in, "adamw_w_in"),
        "w_out": update([lands[f"wout_{l}"] for l in range(n_layer)], w_out, m_w_out, v_w_out, "adamw_w_out"),
        "ffn_w2": update([lands[f"w2_{l}_{s}"] for l in range(n_layer) for s in range(2)],
                         ffn_w2, m_ffn_w2, v_ffn_w2, "adamw_ffn_w2"),
    }

    dmod = jnp.stack([jnp.stack([stats[l][sub][r] for sub in range(3) for r in (0, 1, 3)]) for l in range(n_layer)])
    dng = jnp.stack([jnp.stack([stats[l][sub][2] for sub in range(3)]) for l in range(n_layer)])
    dcw = jnp.stack([conv_stats[l][0:3] for l in range(n_layer)])
    dcb = jnp.stack([conv_stats[l][3] for l in range(n_layer)])
    dqk = jnp.stack([qk_stats[l][0:2].reshape(2, n_head, HEAD_DIM).sum(axis=1) for l in range(n_layer)])
    part_shapes = [dmod.shape, dng.shape, dcw.shape, dcb.shape, dqk.shape]
    parts_all = _allgather_small(_pack([dmod, dng, dcw, dcb, dqk]), "gather_small_grads")
    dmod_all = _unpack(parts_all, part_shapes)[0].reshape(N_DEV, n_layer, 9 * d_model)
    summed = _unpack(_sum_devices(parts_all), part_shapes)
    g_b_ada = summed[0].reshape(n_layer, 9 * d_model)
    ng_cols = norm_g.shape[-1]
    g_norm_g = lax.dynamic_slice_in_dim(summed[1], me * ng_cols, ng_cols, axis=2)
    cw_cols = conv_w.shape[-1]
    g_conv_w = lax.dynamic_slice_in_dim(summed[2], me * cw_cols, cw_cols, axis=2)
    g_conv_b = summed[3]
    g_q, g_k = summed[4][:, 0], summed[4][:, 1]
    dmod_loc = lax.dynamic_slice_in_dim(dmod_all, me * ada_cols, ada_cols, axis=2)
    g_w_ada = _w_ada_grad(silu_c.T, jnp.moveaxis(dmod_loc, 0, 1))
    res["w_ada"] = update([g_w_ada[None]], w_ada, m_w_ada, v_w_ada, "adamw_w_ada")

    names = ["b_ada", "norm_g", "q_norm_g", "k_norm_g", "conv_w", "conv_b"]
    g_small = [g_b_ada, g_norm_g, g_q, g_k, g_conv_w, g_conv_b]
    w_small = [b_ada, norm_g, q_norm_g, k_norm_g, conv_w, conv_b]
    m_small = [m_b_ada, m_norm_g, m_q_norm_g, m_k_norm_g, m_conv_w, m_conv_b]
    v_small = [v_b_ada, v_norm_g, v_q_norm_g, v_k_norm_g, v_conv_w, v_conv_b]
    outs = _adamw([_pack(g_small)[None]], _pack(w_small), _pack(m_small), _pack(v_small), "adamw_small")
    shapes = [w.shape for w in w_small]
    unpacked = [_unpack(o, shapes) for o in outs]
    for i, nme in enumerate(names):
        res[nme] = [unpacked[k][i] for k in range(4)]

    follows = (res["w_in"][1][0, :1, :1] + res["w_out"][1][0, :1, :1] + res["ffn_w2"][1][0, 0, :1, :1]
               + res["w_ada"][1][0, :1, :1] + outs[1][:1, :1])
    lands.update(wait_exchange([last], follows, "grads_wait_last"))
    res["ffn_w1"] = update([lands[f"w1_{l}_{s}"] for l in range(n_layer) for s in range(2)],
                           ffn_w1, m_ffn_w1, v_ffn_w1, "adamw_ffn_w1")

    order = ["w_ada", "b_ada", "norm_g", "w_in", "q_norm_g", "k_norm_g", "conv_w", "conv_b", "w_out", "ffn_w1", "ffn_w2"]
    return (loss, grad_x, *[res[n][0] for n in order], *[res[n][1] for n in order],
            *[res[n][2] for n in order], *[res[n][3] for n in order])
```

```python
import jax
import jax.numpy as jnp
from jax import lax
from jax.experimental import pallas as pl
from jax.experimental.pallas import tpu as pltpu

F32 = jnp.float32
BF16 = jnp.bfloat16
SDS = jax.ShapeDtypeStruct

N_DEV = 8
HEAD_DIM = 64
BLOCK = 128
DILATIONS = (1, 4, 16)
ATTN_BLOCKS_PER_STEP = 4
N_CHUNK = 4
EPS = 1e-6
NEG = -1e30
SM_SCALE = HEAD_DIM ** -0.5
LANES = 128
MXU_WIDTH = 256
TOKEN_TILE = 512
FFN_FWD_TOKEN_TILE = 1024
FFN_BWD_TOKEN_TILE = 512
GRAD_TOKEN_TILE = 2048
VMEM_LIMIT_BYTES = 56 * 1024 * 1024

ADAM_LR = 0.001
ADAM_B1 = 0.9
ADAM_B2 = 0.999
ADAM_EPS = 1e-08
ADAM_WD = 0.01
ADAM_STEP = 10

MESH_ID = pl.DeviceIdType.MESH
ANY = pl.BlockSpec(memory_space=pl.ANY)
HBM_SPEC = pl.BlockSpec(memory_space=pltpu.HBM)
SEM_SPEC = pl.BlockSpec(memory_space=pltpu.SEMAPHORE)
SIDE_EFFECT = pltpu.SideEffectType.DATAFLOW_SIDE_EFFECTING


def _params(n_axes):
    return pltpu.CompilerParams(dimension_semantics=("arbitrary",) * n_axes,
                                vmem_limit_bytes=VMEM_LIMIT_BYTES)


def _dot(a, b):
    return jnp.dot(a, b, preferred_element_type=F32)


def _dot_nt(a, b):
    return lax.dot_general(a, b, (((1,), (1,)), ((), ())), preferred_element_type=F32)


def _dot_tn(a, b):
    return lax.dot_general(a, b, (((0,), (0,)), ((), ())), preferred_element_type=F32)


def _row_halves(n):
    return (slice(0, n // 2), slice(n // 2, n))


def _split_bf16(t):
    hi = t.astype(BF16)
    return hi, (t - hi.astype(F32)).astype(BF16)


def _my_place():
    x, y, c = lax.axis_index("x"), lax.axis_index("y"), lax.axis_index("c")
    return x, y, c


def _peer(place, k):
    x, y, c = place
    return ((1 - x) if k & 4 else x, (1 - y) if k & 2 else y, (1 - c) if k & 1 else c)


def _index(place):
    return 4 * place[0] + 2 * place[1] + place[2]


def _allgather_small(v, name):
    rows, cols = v.shape

    def body(x_ref, out_ref, send_sems, recv_sems, local_sem):
        me = _my_place()
        mine = pltpu.make_async_copy(x_ref, out_ref.at[_index(me)], local_sem)
        mine.start()
        sends = []
        for k in range(1, N_DEV):
            cp = pltpu.make_async_remote_copy(
                src_ref=x_ref, dst_ref=out_ref.at[_index(me)], send_sem=send_sems.at[k - 1],
                recv_sem=recv_sems.at[k - 1], device_id=_peer(me, k), device_id_type=MESH_ID)
            cp.start()
            sends.append(cp)
        for k in range(1, N_DEV):
            pltpu.make_async_remote_copy(
                src_ref=x_ref, dst_ref=out_ref.at[_index(_peer(me, k))], send_sem=send_sems.at[k - 1],
                recv_sem=recv_sems.at[k - 1], device_id=_peer(me, k), device_id_type=MESH_ID).wait_recv()
        for cp in sends:
            cp.wait_send()
        mine.wait()

    return pl.pallas_call(
        body, name=name,
        out_shape=SDS((N_DEV, rows, cols), v.dtype),
        in_specs=[pl.BlockSpec(memory_space=pltpu.VMEM)],
        out_specs=pl.BlockSpec(memory_space=pltpu.VMEM),
        scratch_shapes=[pltpu.SemaphoreType.DMA((N_DEV - 1,)), pltpu.SemaphoreType.DMA((N_DEV - 1,)),
                        pltpu.SemaphoreType.DMA],
    )(v)


def _allgather_big(lands, name):
    n = len(lands)

    def body(*refs):
        outs = refs[n:2 * n]
        send_sems, recv_sems = refs[2 * n:]
        x, y, c = _my_place()
        me, sib = (x, y, c), (x, y, 1 - c)
        chips = [(1 - x, y), (x, 1 - y), (1 - x, 1 - y)]

        def copy(i, k, block, to):
            dst = outs[i].at[_index(block)]
            return pltpu.make_async_remote_copy(
                src_ref=dst, dst_ref=dst, send_sem=send_sems.at[7 * i + k],
                recv_sem=recv_sems.at[7 * i + k], device_id=to, device_id_type=MESH_ID)

        first = []
        for i in range(n):
            first.append(copy(i, 0, me, sib))
            for j, chip in enumerate(chips):
                first.append(copy(i, 1 + j, me, (*chip, c)))
        for cp in first:
            cp.start()
        passed = []
        for i in range(n):
            for j, chip in enumerate(chips):
                copy(i, 1 + j, (*chip, c), me).wait_recv()
                fwd = copy(i, 4 + j, (*chip, c), sib)
                fwd.start()
                passed.append(fwd)
        for i in range(n):
            copy(i, 0, sib, me).wait_recv()
            for j, chip in enumerate(chips):
                copy(i, 4 + j, (*chip, 1 - c), me).wait_recv()
        for cp in first + passed:
            cp.wait_send()

    return pl.pallas_call(
        body, name=name,
        out_shape=[SDS(a.shape, a.dtype) for a in lands],
        in_specs=[ANY] * n, out_specs=[ANY] * n,
        input_output_aliases={i: i for i in range(n)},
        scratch_shapes=[pltpu.SemaphoreType.DMA((7 * n,)), pltpu.SemaphoreType.DMA((7 * n,))],
    )(*lands)


def _place_own(src, lead, me_arr, name):
    rows, cols = src.shape[-2:]
    tr = _row_tile(rows, cols)

    def body(me_ref, s_ref, o_ref):
        o_ref[...] = s_ref[...].astype(BF16)

    if lead is None:
        in_spec = pl.BlockSpec((None, tr, cols), lambda i, me_ref: (me_ref[0], i, 0))
    else:
        in_spec = pl.BlockSpec((None,) * len(lead) + (tr, cols), lambda i, me_ref: (*lead, i, 0))
    return pl.pallas_call(
        body, name=name,
        grid_spec=pltpu.PrefetchScalarGridSpec(
            num_scalar_prefetch=1, grid=(rows // tr,), in_specs=[in_spec],
            out_specs=pl.BlockSpec((None, tr, cols), lambda i, me_ref: (me_ref[0], i, 0))),
        out_shape=SDS((N_DEV, rows, cols), BF16),
        compiler_params=_params(1),
    )(me_arr, src)


def _send_start(lands, pieces, after, name):
    n = len(lands)
    arrs = list(lands) + list(pieces or [])
    n_a = len(arrs)

    def body(*refs):
        land_r, piece_r = refs[:n], refs[n:n_a]
        send, recv = refs[n_a + 1:n_a + 1 + n], refs[n_a + 1 + n:n_a + 1 + 2 * n]
        token = refs[-1]
        me = _my_place()
        for i in range(n):
            for k in range(1, N_DEV):
                peer = _peer(me, k)
                src = piece_r[i].at[_index(peer)] if pieces else land_r[i].at[_index(me)]
                pltpu.make_async_remote_copy(
                    src_ref=src, dst_ref=land_r[i].at[_index(me)], send_sem=send[i], recv_sem=recv[i],
                    device_id=peer, device_id_type=MESH_ID).start()
        token[...] = jnp.zeros_like(token)

    outs = pl.pallas_call(
        body, name=name,
        out_shape=[pltpu.SemaphoreType.DMA(())] * (2 * n) + [pltpu.HBM(a.shape, a.dtype) for a in arrs]
        + [SDS((8, LANES), F32)],
        in_specs=[HBM_SPEC] * n_a + [ANY],
        out_specs=[SEM_SPEC] * (2 * n) + [HBM_SPEC] * n_a + [pl.BlockSpec(memory_space=pltpu.VMEM)],
        input_output_aliases={i: 2 * n + i for i in range(n_a)},
        compiler_params=pltpu.CompilerParams(has_side_effects=SIDE_EFFECT),
    )(*[pltpu.with_memory_space_constraint(a, pltpu.HBM) for a in arrs], after)
    return outs[:n], outs[n:2 * n], outs[2 * n:2 * n + n_a], outs[-1]


def _send_wait(thru, sends, recvs, after, name):
    n = len(sends)
    n_a = len(thru)

    def body(*refs):
        land_r = refs[:n]
        send, recv = refs[n_a:n_a + n], refs[n_a + n:n_a + 2 * n]
        me = _my_place()
        for i in range(n):
            seven = land_r[i].at[pl.ds(0, N_DEV - 1)]
            cp = pltpu.make_async_remote_copy(src_ref=seven, dst_ref=seven, send_sem=send[i], recv_sem=recv[i],
                                              device_id=_peer(me, 1), device_id_type=MESH_ID)
            cp.wait_send()
            cp.wait_recv()

    outs = pl.pallas_call(
        body, name=name,
        out_shape=[pltpu.HBM(a.shape, a.dtype) for a in thru],
        in_specs=[HBM_SPEC] * n_a + [SEM_SPEC] * (2 * n) + [ANY],
        out_specs=[HBM_SPEC] * n_a,
        input_output_aliases={i: i for i in range(n_a)},
        compiler_params=pltpu.CompilerParams(has_side_effects=SIDE_EFFECT),
    )(*thru, *sends, *recvs, after)
    return outs[:n]


def _ada_fwd(x, prm_ref):
    rstd = lax.rsqrt(jnp.mean(x * x, axis=-1, keepdims=True) + EPS)
    return (x * rstd * prm_ref[0:1, :]) * (1.0 + prm_ref[2:3, :]) + prm_ref[1:2, :]


def _ada_bwd(dh, x, prm_ref):
    ng, ops = prm_ref[0:1, :], 1.0 + prm_ref[2:3, :]
    rstd = lax.rsqrt(jnp.mean(x * x, axis=-1, keepdims=True) + EPS)
    xhat = x * rstd
    dxhat = dh * (ops * ng)
    dx = rstd * (dxhat - xhat * jnp.mean(dxhat * xhat, axis=-1, keepdims=True))
    dshift = jnp.sum(dh, axis=0, keepdims=True)
    dscale = jnp.sum(dh * (xhat * ng), axis=0, keepdims=True)
    dng = jnp.sum(dh * (xhat * ops), axis=0, keepdims=True)
    return dx, dshift, dscale, dng


def _head_mean(t, bd):
    hi = t.astype(BF16)
    g = bd.shape[0]
    parts = [_dot(hi[:, i:i + g], bd) for i in range(0, t.shape[1], g)]
    return jnp.concatenate(parts, axis=1) * (1.0 / HEAD_DIM)


def _join_columns(w_hbm, dst_ref, sems):
    cols = w_hbm.shape[2]
    copies = [pltpu.make_async_copy(w_hbm.at[j], dst_ref.at[:, pl.ds(j * cols, cols)], sems.at[j])
              for j in range(w_hbm.shape[0])]
    for cp in copies:
        cp.start()
    for cp in copies:
        cp.wait()


def _shift_rows(z, before, tm):
    row = lax.broadcasted_iota(jnp.int32, z.shape, 0)
    z1 = jnp.where(row == 0, before[7:8, :], pltpu.roll(z, 1, 0))
    z2 = jnp.where(row == 0, before[6:7, :], jnp.where(row == 1, before[7:8, :], pltpu.roll(z, 2, 0)))
    return z1, z2


def _mod_part(c_all, w_ada, b_loc):
    n_layer, d_model, cols = w_ada.shape

    def body(c_ref, w_ref, b_ref, out_ref, sc_ref):
        cv = c_ref[...]
        sc = cv * jax.nn.sigmoid(cv)
        sc_ref[...] = sc
        a_hi, a_lo = _split_bf16(sc)
        w_hi, w_lo = _split_bf16(w_ref[...])
        out_ref[...] = _dot(a_hi, w_hi) + _dot(a_hi, w_lo) + _dot(a_lo, w_hi) + b_ref[...]

    return pl.pallas_call(
        body, name="mod_part", grid=(n_layer,),
        in_specs=[pl.BlockSpec((N_DEV, d_model), lambda l: (0, 0)),
                  pl.BlockSpec((None, d_model, cols), lambda l: (l, 0, 0)),
                  pl.BlockSpec((None, 1, cols), lambda l: (l, 0, 0))],
        out_specs=[pl.BlockSpec((None, N_DEV, cols), lambda l: (l, 0, 0)),
                   pl.BlockSpec((N_DEV, d_model), lambda l: (0, 0))],
        out_shape=[SDS((n_layer, N_DEV, cols), F32), SDS((N_DEV, d_model), F32)],
        compiler_params=_params(1),
    )(c_all, w_ada, b_loc)


def _w_ada_grad(sc_t, dmod_loc):
    d_model = sc_t.shape[0]
    n_layer, _, cols = dmod_loc.shape

    def body(s_ref, d_ref, out_ref):
        acc = s_ref[:, 0:1] * d_ref[0:1, :]
        for b in range(1, N_DEV):
            acc = acc + s_ref[:, b:b + 1] * d_ref[b:b + 1, :]
        out_ref[...] = acc

    return pl.pallas_call(
        body, name="w_ada_grad", grid=(n_layer,),
        in_specs=[pl.BlockSpec((d_model, N_DEV), lambda l: (0, 0)),
                  pl.BlockSpec((None, N_DEV, cols), lambda l: (l, 0, 0))],
        out_specs=pl.BlockSpec((None, d_model, cols), lambda l: (l, 0, 0)),
        out_shape=SDS((n_layer, d_model, cols), F32),
        compiler_params=_params(1),
    )(sc_t, dmod_loc)


def _sum_devices(g):
    _, rows, cols = g.shape

    def body(g_ref, out_ref):
        acc = g_ref[0]
        for i in range(1, N_DEV):
            acc = acc + g_ref[i]
        out_ref[...] = acc

    return pl.pallas_call(body, name="sum_devices", out_shape=SDS((rows, cols), F32))(g)


def _ffn_fwd(x, prm, w1r, w2g, l, sub, tm):
    t_len, d_model = x.shape
    f8, r8 = w1r.shape[-1], w2g.shape[-2]

    def body(x_ref, prm_ref, w1_ref, w2_ref, act_ref, h_ref, y_ref, xo_ref, h_sc, acc):
        j = pl.program_id(1)

        @pl.when(j == 0)
        def _():
            h = _ada_fwd(x_ref[...], prm_ref).astype(BF16)
            h_sc[...] = h
            h_ref[...] = h
            acc[...] = jnp.zeros_like(acc)

        w2 = w2_ref[...].reshape(2 * r8, d_model)
        parts = _row_halves(tm)
        gus = [(_dot(h_sc[rows, :], w1_ref[0]), _dot(h_sc[rows, :], w1_ref[1])) for rows in parts]
        ss = []
        for rows, (g, u) in zip(parts, gus):
            sig = jax.nn.sigmoid(g)
            sil = g * sig
            s = (sil * u).astype(BF16)
            act_ref[0, rows, :] = (u * (sig * (1.0 + g * (1.0 - sig)))).astype(BF16)
            act_ref[1, rows, :] = sil.astype(BF16)
            act_ref[2, rows, :] = s
            ss.append(s)
        for rows, s in zip(parts, ss):
            acc[rows, :] += _dot(s, w2)

        @pl.when(j == N_CHUNK - 1)
        def _():
            yv = acc[...]
            y_ref[...] = yv.astype(BF16)
            xo_ref[...] = x_ref[...] + (0.5 * prm_ref[3:4, :]) * yv

    tile = pl.BlockSpec((tm, d_model), lambda m, j: (m, 0))
    once = pl.BlockSpec((tm, d_model), lambda m, j: (m, 0), pipeline_mode=pl.Buffered(1))
    return pl.pallas_call(
        body, name=f"ffn_fwd_{l}_{sub}", grid=(t_len // tm, N_CHUNK),
        in_specs=[tile, pl.BlockSpec((8, d_model), lambda m, j: (0, 0)),
                  pl.BlockSpec((2, None, d_model, f8), lambda m, j: (0, j, 0, 0)),
                  pl.BlockSpec((2, r8, d_model), lambda m, j: (j, 0, 0))],
        out_specs=[pl.BlockSpec((3, None, tm, f8), lambda m, j: (0, j, m, 0)), once, once, tile],
        out_shape=[SDS((3, N_CHUNK, t_len, f8), BF16), SDS((t_len, d_model), BF16),
                   SDS((t_len, d_model), BF16), SDS((t_len, d_model), F32)],
        scratch_shapes=[pltpu.VMEM((tm, d_model), BF16), pltpu.VMEM((tm, d_model), F32)],
        compiler_params=_params(2),
    )(x, prm, w1r, w2g)


def _ffn_bwd(dxo, y, x_in, prm, act, w1r, w2g, l, sub, tm):
    t_len, d_model = x_in.shape
    f8, r8 = w1r.shape[-1], w2g.shape[-2]
    n_m = t_len // tm

    def body(dxo_ref, y_ref, x_ref, prm_ref, act_ref, w1_ref, w2_ref,
             dx_ref, da_ref, dy_ref, st_ref, dy_sc, acc):
        m, j = pl.program_id(0), pl.program_id(1)

        @pl.when((m == 0) & (j == 0))
        def _():
            st_ref[...] = jnp.zeros_like(st_ref)

        @pl.when(j == 0)
        def _():
            dxo_v = dxo_ref[...]
            dy = ((0.5 * prm_ref[3:4, :]) * dxo_v).astype(BF16)
            dy_sc[...] = dy
            dy_ref[...] = dy
            st_ref[3:4, :] += jnp.sum(0.5 * dxo_v * y_ref[...].astype(F32), axis=0, keepdims=True)
            acc[...] = jnp.zeros_like(acc)

        w2 = w2_ref[...].reshape(2 * r8, d_model)
        parts = _row_halves(tm)
        dss = [_dot_nt(dy_sc[rows, :], w2) for rows in parts]
        das = []
        for rows, ds in zip(parts, dss):
            dg = (ds * act_ref[0, rows, :].astype(F32)).astype(BF16)
            du = (ds * act_ref[1, rows, :].astype(F32)).astype(BF16)
            da_ref[0, rows, :] = dg
            da_ref[1, rows, :] = du
            das.append((dg, du))
        for rows, (dg, du) in zip(parts, das):
            acc[rows, :] += _dot_nt(dg, w1_ref[0]) + _dot_nt(du, w1_ref[1])

        @pl.when(j == N_CHUNK - 1)
        def _():
            dx, dshift, dscale, dng = _ada_bwd(acc[...], x_ref[...], prm_ref)
            dx_ref[...] = dxo_ref[...] + dx
            st_ref[0:1, :] += dshift
            st_ref[1:2, :] += dscale
            st_ref[2:3, :] += dng

    tile = pl.BlockSpec((tm, d_model), lambda m, j: (m, 0))
    fixed = pl.BlockSpec((8, d_model), lambda m, j: (0, 0))
    return pl.pallas_call(
        body, name=f"ffn_bwd_{l}_{sub}", grid=(n_m, N_CHUNK),
        in_specs=[tile, tile, tile, fixed,
                  pl.BlockSpec((2, None, tm, f8), lambda m, j: (0, j, m, 0)),
                  pl.BlockSpec((2, None, d_model, f8), lambda m, j: (0, j, 0, 0)),
                  pl.BlockSpec((2, r8, d_model), lambda m, j: (j, 0, 0))],
        out_specs=[tile, pl.BlockSpec((2, None, tm, f8), lambda m, j: (0, j, m, 0)), tile, fixed],
        out_shape=[SDS((t_len, d_model), F32), SDS((2, N_CHUNK, t_len, f8), BF16),
                   SDS((t_len, d_model), BF16), SDS((8, d_model), F32)],
        scratch_shapes=[pltpu.VMEM((tm, d_model), BF16), pltpu.VMEM((tm, d_model), F32)],
        compiler_params=_params(2),
    )(dxo, y, x_in, prm, act, w1r, w2g)


def _matmul_tn(a, b, a_spec, b_spec, out_shape, out_spec, n_piece, t_len, tk, name, after):
    def body(a_ref, b_ref, after_ref, o_ref, acc):
        k = pl.program_id(1)

        @pl.when(k == 0)
        def _():
            acc[...] = jnp.zeros_like(acc)

        m_out, n_out = acc.shape
        if (m_out // 2) % LANES == 0:
            for rows in _row_halves(m_out):
                acc[rows, :] += _dot_tn(a_ref[:, rows], b_ref[...])
        else:
            for cols in _row_halves(n_out):
                acc[:, cols] += _dot_tn(a_ref[...], b_ref[:, cols])

        @pl.when(k == pl.num_programs(1) - 1)
        def _():
            if len(o_ref.shape) == 3:
                cols = o_ref.shape[2]
                for q in range(o_ref.shape[0]):
                    o_ref[q] = acc[:, q * cols:(q + 1) * cols].astype(o_ref.dtype)
            else:
                o_ref[...] = acc[...].astype(o_ref.dtype)

    blk = tuple(d for d in out_spec.block_shape if d is not None)
    acc_shape = blk if len(blk) == 2 else (blk[1], blk[0] * blk[2])
    return pl.pallas_call(
        body, name=name, grid=(n_piece, t_len // tk),
        in_specs=[a_spec, b_spec, ANY], out_specs=out_spec, out_shape=out_shape,
        scratch_shapes=[pltpu.VMEM(acc_shape, F32)],
        compiler_params=_params(2),
    )(a, b, after)


def _groups(tm, width):
    return pltpu.VMEM((width // LANES, tm, LANES), F32)


def _to_groups(val, dst_ref, first=0):
    for g in range(val.shape[1] // LANES):
        dst_ref[first + g] = val[:, g * LANES:(g + 1) * LANES]


def _from_groups(src_ref):
    return jnp.concatenate([src_ref[g] for g in range(src_ref.shape[0])], axis=1)


def _split_residues(src_ref, out_ref, dil, tm):
    for g in range(src_ref.shape[0]):
        for r in range(dil):
            out_ref[r, :, g * LANES:(g + 1) * LANES] = (
                src_ref.at[g][pl.ds(r, tm // dil, stride=dil), :].astype(out_ref.dtype))


def _merge_residues(in_ref, dst_ref, dil, tm, add=False):
    for g in range(dst_ref.shape[0]):
        for r in range(dil):
            rows = pl.ds(r, tm // dil, stride=dil)
            val = in_ref[r, :, g * LANES:(g + 1) * LANES].astype(F32)
            dst = dst_ref.at[g]
            dst[rows, :] = (dst[rows, :] + val) if add else val


def _res_spec(dil, tm, width, index=lambda m: m):
    return pl.BlockSpec((dil, tm // dil, width), lambda m: (0, index(m), 0))


def _mixer_in(x, prm, wing, qkg, cprm, bd, l, tm):
    t_len, d_model = x.shape
    a_w = d_model // 2
    c_w = d_model - a_w
    c8 = wing.shape[-1]
    p_c = N_DEV * c8
    dils = DILATIONS[1:]

    def body(x_ref, prm_ref, w_ref, qkg_ref, cprm_ref, bd_ref, proj_ref, qkn_ref, h_ref, ycv_ref, *rest):
        res_refs, (pf, qsc, vsc, carry, wcat, wsems) = rest[:2 * len(dils)], rest[2 * len(dils):]

        @pl.when(pl.program_id(0) == 0)
        def _():
            carry[...] = jnp.zeros_like(carry)
            _join_columns(w_ref, wcat, wsems)

        h = _ada_fwd(x_ref[...], prm_ref).astype(BF16)
        h_ref[...] = h
        for cols in _row_halves(p_c):
            pf[:, cols] = _dot(h, wcat[:, cols])
        proj_ref[...] = pf[...].astype(BF16)
        bdm = bd_ref[...]
        for i in range(2):
            t = pf[:, i * a_w:(i + 1) * a_w]
            rstd = lax.rsqrt(_head_mean(t * t, bdm) + EPS)
            tn = t * rstd * qkg_ref[i:i + 1, :]
            qkn_ref[:, i * a_w:(i + 1) * a_w] = tn.astype(BF16)
            _to_groups(tn, qsc, i * (a_w // LANES))
        _to_groups(pf[:, 2 * a_w:3 * a_w], vsc)
        for i, dil in enumerate(dils):
            _split_residues(qsc, res_refs[2 * i], dil, tm)
            _split_residues(vsc, res_refs[2 * i + 1], dil, tm)
        gb = pf[:, 3 * a_w:3 * a_w + c_w]
        z = pf[:, 3 * a_w + c_w:3 * a_w + 2 * c_w] * pf[:, 3 * a_w + 2 * c_w:3 * a_w + 3 * c_w]
        z1, z2 = _shift_rows(z, carry[...], tm)
        cv = cprm_ref[0:1, :] * z2 + cprm_ref[1:2, :] * z1 + cprm_ref[2:3, :] * z + cprm_ref[3:4, :]
        ycv_ref[...] = (gb * cv).astype(BF16)
        carry[...] = z[tm - 8:tm, :]

    res_specs, res_shapes = [], []
    for dil in dils:
        res_specs += [_res_spec(dil, tm, 2 * a_w), _res_spec(dil, tm, a_w)]
        res_shapes += [SDS((dil, t_len // dil, 2 * a_w), BF16), SDS((dil, t_len // dil, a_w), BF16)]
    return pl.pallas_call(
        body, name=f"mixer_in_{l}", grid=(t_len // tm,),
        in_specs=[pl.BlockSpec((tm, d_model), lambda m: (m, 0)),
                  pl.BlockSpec((8, d_model), lambda m: (0, 0)),
                  ANY,
                  pl.BlockSpec((8, a_w), lambda m: (0, 0)),
                  pl.BlockSpec((8, c_w), lambda m: (0, 0)),
                  pl.BlockSpec(bd.shape, lambda m: (0, 0))],
        out_specs=[pl.BlockSpec((tm, p_c), lambda m: (m, 0)),
                   pl.BlockSpec((tm, 2 * a_w), lambda m: (m, 0)),
                   pl.BlockSpec((tm, d_model), lambda m: (m, 0)),
                   pl.BlockSpec((tm, c_w), lambda m: (m, 0))] + res_specs,
        out_shape=[SDS((t_len, p_c), BF16), SDS((t_len, 2 * a_w), BF16),
                   SDS((t_len, d_model), BF16), SDS((t_len, c_w), BF16)] + res_shapes,
        scratch_shapes=[pltpu.VMEM((tm, p_c), F32), _groups(tm, 2 * a_w), _groups(tm, a_w),
                        pltpu.VMEM((8, c_w), F32), pltpu.VMEM((d_model, p_c), BF16),
                        pltpu.SemaphoreType.DMA((N_DEV,))],
        compiler_params=_params(1),
    )(x, prm, wing, qkg, cprm, bd)


def _band_mask(first_block):
    qi = lax.broadcasted_iota(jnp.int32, (BLOCK, 2 * BLOCK), 0)
    kj = lax.broadcasted_iota(jnp.int32, (BLOCK, 2 * BLOCK), 1)
    lowest = jnp.where(first_block, BLOCK, 0)
    return (kj >= qi) & (kj <= qi + BLOCK) & (kj >= lowest)


def _blocks_per_step(seq):
    n_sb = min(ATTN_BLOCKS_PER_STEP, seq // BLOCK)
    assert seq % (n_sb * BLOCK) == 0
    return n_sb


def _block_rows(sb):
    return slice(sb * BLOCK, (sb + 1) * BLOCK)


def _with_previous(prev_ref, cur_ref, sb, sl):
    before = prev_ref[:, sl] if sb == 0 else cur_ref[_block_rows(sb - 1), sl]
    return jnp.concatenate([before, cur_ref[_block_rows(sb), sl]], axis=0)


def _heads_on_lanes(cols):
    width = LANES // len(cols)
    lane = lax.broadcasted_iota(jnp.int32, (BLOCK, LANES), 1)
    out = jnp.broadcast_to(cols[-1], (BLOCK, LANES))
    for h in range(len(cols) - 2, -1, -1):
        out = jnp.where(lane < (h + 1) * width, cols[h], out)
    return out


def _attn_fwd(qk, v, v_col, l):
    dil, seq, a2 = qk.shape
    a_w = a2 // 2
    n_sb = _blocks_per_step(seq)
    rows_step = n_sb * BLOCK
    n_pair = a_w // LANES

    def body(q_ref, kc_ref, vc_ref, o_ref, lse_ref, kp_ref, vp_ref):
        @pl.when(pl.program_id(1) == 0)
        def _():
            kp_ref[...] = jnp.zeros_like(kp_ref)
            vp_ref[...] = jnp.zeros_like(vp_ref)

        masks = [_band_mask(pl.program_id(1) == 0)] + [_band_mask(False)] * (n_sb - 1)
        low = lax.broadcasted_iota(jnp.int32, (BLOCK, LANES), 1) < HEAD_DIM
        units = [(sb, slice(p * LANES, (p + 1) * LANES)) for sb in range(n_sb) for p in range(a_w // LANES)]
        scores = []
        for sb, sl in units:
            qp = q_ref[_block_rows(sb), sl]
            k2 = _with_previous(kp_ref, kc_ref, sb, sl)
            scores.append([_dot_nt(jnp.where(half, qp, jnp.zeros_like(qp)), k2) for half in (low, ~low)])
        probs, lses = [], []
        for (sb, sl), pair in zip(units, scores):
            pr, ls = [], []
            for s in pair:
                s = jnp.where(masks[sb], s * SM_SCALE, NEG)
                mx = jnp.max(s, axis=-1, keepdims=True)
                e = jnp.exp(s - mx)
                den = jnp.sum(e, axis=-1, keepdims=True)
                pr.append((e * (1.0 / den)).astype(BF16))
                ls.append(mx + jnp.log(den))
            probs.append(pr)
            lses.append(ls)
        for (sb, sl), pr, ls in zip(units, probs, lses):
            v2 = _with_previous(vp_ref, vc_ref, sb, sl)
            o_ref[_block_rows(sb), sl] = jnp.where(low, _dot(pr[0], v2), _dot(pr[1], v2)).astype(BF16)
        for sb in range(n_sb):
            cols = [c for ls in lses[sb * n_pair:(sb + 1) * n_pair] for c in ls]
            lse_ref[_block_rows(sb), :] = _heads_on_lanes(cols)
        kp_ref[...] = kc_ref[_block_rows(n_sb - 1), :]
        vp_ref[...] = vc_ref[_block_rows(n_sb - 1), :]

    def blk(f):
        return pl.BlockSpec((None, rows_step, a_w), f)

    return pl.pallas_call(
        body, name=f"attn_fwd_{l}_d{dil}", grid=(dil, seq // rows_step),
        in_specs=[blk(lambda r, b: (r, b, 0)),
                  blk(lambda r, b: (r, b, 1)),
                  blk(lambda r, b: (r, b, v_col))],
        out_specs=[blk(lambda r, b: (r, b, 0)), pl.BlockSpec((None, rows_step, LANES), lambda r, b: (r, b, 0))],
        out_shape=[SDS((dil, seq, a_w), BF16), SDS((dil, seq, LANES), F32)],
        scratch_shapes=[pltpu.VMEM((BLOCK, a_w), BF16), pltpu.VMEM((BLOCK, a_w), BF16)],
        compiler_params=_params(2),
    )(qk, qk, v)


def _attn_combine(outs, lses, ycv, expand, tm, l):
    t_len, c_w = ycv.shape
    a_w = outs[0].shape[-1]
    dils = DILATIONS[1:]
    n_r = len(dils)

    def body(o1, l1, *rest):
        o_res, l_res, y_ref, ex_ref = rest[:n_r], rest[n_r:2 * n_r], rest[2 * n_r], rest[2 * n_r + 1]
        cat_ref, lse_ref = rest[2 * n_r + 2], rest[2 * n_r + 3]
        ya_res, lse_res = rest[2 * n_r + 4:3 * n_r + 4], rest[3 * n_r + 4:4 * n_r + 4]
        scr = rest[4 * n_r + 4:]
        so, sl, sya, slse = scr[:n_r], scr[n_r:2 * n_r], scr[2 * n_r], scr[2 * n_r + 1]
        for i, dil in enumerate(dils):
            _merge_residues(o_res[i], so[i], dil, tm)
            _merge_residues(l_res[i], sl[i], dil, tm)
        ls = [l1[...]] + [s[0] for s in sl]
        mx = jnp.maximum(jnp.maximum(ls[0], ls[1]), ls[2])
        es = [jnp.exp(t - mx) for t in ls]
        den = es[0] + es[1] + es[2]
        inv = 1.0 / den
        slse[0] = mx + jnp.log(den)
        lse_ref[...] = slse[0]
        ex = ex_ref[...]
        ov = [o1[...].astype(F32)] + [_from_groups(s) for s in so]
        ya = jnp.zeros((tm, a_w), F32)
        for e, o in zip(es, ov):
            hi, lo = _split_bf16(e * inv)
            ya = ya + (_dot(hi, ex) + _dot(lo, ex)) * o
        _to_groups(ya, sya)
        cat_ref[:, :a_w] = ya.astype(BF16)
        cat_ref[:, a_w:] = y_ref[...]
        for i, dil in enumerate(dils):
            _split_residues(sya, ya_res[i], dil, tm)
            _split_residues(slse, lse_res[i], dil, tm)

    ta = pl.BlockSpec((tm, a_w), lambda m: (m, 0))
    tl = pl.BlockSpec((tm, LANES), lambda m: (m, 0))
    res_a = [_res_spec(dil, tm, a_w) for dil in dils]
    res_l = [_res_spec(dil, tm, LANES) for dil in dils]
    return pl.pallas_call(
        body, name=f"attn_combine_{l}", grid=(t_len // tm,),
        in_specs=[ta, tl] + res_a + res_l + [pl.BlockSpec((tm, c_w), lambda m: (m, 0)),
                                            pl.BlockSpec((LANES, a_w), lambda m: (0, 0))],
        out_specs=[pl.BlockSpec((tm, a_w + c_w), lambda m: (m, 0)), tl] + res_a + res_l,
        out_shape=[SDS((t_len, a_w + c_w), BF16), SDS((t_len, LANES), F32)]
        + [SDS((dil, t_len // dil, a_w), BF16) for dil in dils]
        + [SDS((dil, t_len // dil, LANES), F32) for dil in dils],
        scratch_shapes=[_groups(tm, a_w)] * n_r + [_groups(tm, LANES)] * n_r + [_groups(tm, a_w), _groups(tm, LANES)],
        compiler_params=_params(1),
    )(outs[0].reshape(t_len, a_w), lses[0].reshape(t_len, LANES), *outs[1:], *lses[1:], ycv, expand)


def _mixer_out(cat, x, prm, woutg, l, tm):
    t_len, d_model = x.shape
    r8 = woutg.shape[-2]

    def body(cat_ref, x_ref, prm_ref, w_ref, xo_ref, y_ref):
        w = w_ref[...].reshape(N_DEV * r8, d_model)
        for rows in _row_halves(tm):
            yv = _dot(cat_ref[rows, :], w)
            y_ref[rows, :] = yv.astype(BF16)
            xo_ref[rows, :] = x_ref[rows, :] + prm_ref[3:4, :] * yv

    tile = pl.BlockSpec((tm, d_model), lambda m: (m, 0))
    return pl.pallas_call(
        body, name=f"mixer_out_{l}", grid=(t_len // tm,),
        in_specs=[tile, tile, pl.BlockSpec((8, d_model), lambda m: (0, 0)),
                  pl.BlockSpec((N_DEV, r8, d_model), lambda m: (0, 0, 0))],
        out_specs=[tile, tile],
        out_shape=[SDS((t_len, d_model), F32), SDS((t_len, d_model), BF16)],
        compiler_params=_params(1),
    )(cat, x, prm, woutg)


def _mixer_bwd_out(dxo, ymix, prm, woutg, proj, cprm, l, tm):
    t_len, d_model = dxo.shape
    a_w = d_model // 2
    c_w = d_model - a_w
    r8 = woutg.shape[-2]
    n_m = t_len // tm
    hb = tm // 8

    dils = DILATIONS[1:]

    def body(dxo_ref, ym_ref, prm_ref, w_ref, pc_ref, halo_ref, cprm_ref,
             dya_ref, dpc_ref, dy_ref, st_ref, cs_ref, *rest):
        dya_res, (carry, dsc) = rest[:len(dils)], rest[len(dils):]
        step = pl.program_id(0)
        tile_i = n_m - 1 - step

        @pl.when(step == 0)
        def _():
            st_ref[...] = jnp.zeros_like(st_ref)
            cs_ref[...] = jnp.zeros_like(cs_ref)
            carry[...] = jnp.zeros_like(carry)

        dxo_v = dxo_ref[...]
        dy = (prm_ref[3:4, :] * dxo_v).astype(BF16)
        dy_ref[...] = dy
        st_ref[3:4, :] += jnp.sum(dxo_v * ym_ref[...].astype(F32), axis=0, keepdims=True)
        w = w_ref[...].reshape(N_DEV * r8, d_model)
        dcat = jnp.concatenate([_dot_nt(dy[rows, :], w) for rows in _row_halves(tm)], axis=0)
        _to_groups(dcat[:, :a_w], dsc)
        dya_ref[...] = dcat[:, :a_w].astype(BF16)
        for i, dil in enumerate(dils):
            _split_residues(dsc, dya_res[i], dil, tm)
        dyc = dcat[:, a_w:]
        gb = pc_ref[:, :c_w].astype(F32)
        gc = pc_ref[:, c_w:2 * c_w].astype(F32)
        u = pc_ref[:, 2 * c_w:].astype(F32)
        z = gc * u
        before = halo_ref[:, c_w:2 * c_w].astype(F32) * halo_ref[:, 2 * c_w:].astype(F32)
        before = jnp.where(tile_i > 0, before, jnp.zeros_like(before))
        z1, z2 = _shift_rows(z, before, tm)
        w0, w1, w2 = cprm_ref[0:1, :], cprm_ref[1:2, :], cprm_ref[2:3, :]
        cv = w0 * z2 + w1 * z1 + w2 * z + cprm_ref[3:4, :]
        dcv = dyc * gb
        cs_ref[0:1, :] += jnp.sum(dcv * z2, axis=0, keepdims=True)
        cs_ref[1:2, :] += jnp.sum(dcv * z1, axis=0, keepdims=True)
        cs_ref[2:3, :] += jnp.sum(dcv * z, axis=0, keepdims=True)
        cs_ref[3:4, :] += jnp.sum(dcv, axis=0, keepdims=True)
        row = lax.broadcasted_iota(jnp.int32, dcv.shape, 0)
        after = carry[...]
        d1 = jnp.where(row == tm - 1, after[0:1, :], pltpu.roll(dcv, tm - 1, 0))
        d2 = jnp.where(row == tm - 2, after[0:1, :],
                       jnp.where(row == tm - 1, after[1:2, :], pltpu.roll(dcv, tm - 2, 0)))
        dz = w2 * dcv + w1 * d1 + w0 * d2
        dpc_ref[:, :c_w] = (dyc * cv).astype(BF16)
        dpc_ref[:, c_w:2 * c_w] = (dz * u).astype(BF16)
        dpc_ref[:, 2 * c_w:] = (dz * gc).astype(BF16)
        carry[...] = dcv[0:8, :]

    def rev(width, col=0):
        return pl.BlockSpec((tm, width), lambda s: (n_m - 1 - s, col))

    fixed_d = pl.BlockSpec((8, d_model), lambda s: (0, 0))
    fixed_c = pl.BlockSpec((8, c_w), lambda s: (0, 0))
    return pl.pallas_call(
        body, name=f"mixer_bwd_out_{l}", grid=(n_m,),
        in_specs=[rev(d_model), rev(d_model), fixed_d,
                  pl.BlockSpec((N_DEV, r8, d_model), lambda s: (0, 0, 0)),
                  rev(3 * c_w, 1),
                  pl.BlockSpec((8, 3 * c_w), lambda s: (jnp.maximum((n_m - 1 - s) * hb - 1, 0), 1)),
                  fixed_c],
        out_specs=[rev(a_w), rev(3 * c_w), rev(d_model), fixed_d, fixed_c]
        + [_res_spec(dil, tm, a_w, lambda s: n_m - 1 - s) for dil in dils],
        out_shape=[SDS((t_len, a_w), BF16), SDS((t_len, 3 * c_w), BF16), SDS((t_len, d_model), BF16),
                   SDS((8, d_model), F32), SDS((8, c_w), F32)]
        + [SDS((dil, t_len // dil, a_w), BF16) for dil in dils],
        scratch_shapes=[pltpu.VMEM((8, c_w), F32), _groups(tm, a_w)],
        compiler_params=_params(1),
    )(dxo, ymix, prm, woutg, proj, proj, cprm)


def _attn_bwd(qk, v, v_col, dya, ya, lse, l):
    dil, seq, a2 = qk.shape
    a_w = a2 // 2
    n_sb = _blocks_per_step(seq)
    rows_step = n_sb * BLOCK
    n_step = seq // rows_step
    last = _block_rows(n_sb - 1)
    lse_w = LANES // (a_w // HEAD_DIM)
    single = n_step == 1

    def body(q_ref, kc_ref, vc_ref, do_ref, ya_ref, lse_ref,
             dq_ref, dk_ref, dv_ref, hold_k, hold_v, acc_k, acc_v, kp_ref, vp_ref):
        j = pl.program_id(1)

        @pl.when(j == 0)
        def _():
            kp_ref[...] = jnp.zeros_like(kp_ref)
            vp_ref[...] = jnp.zeros_like(vp_ref)

        @pl.when((pl.program_id(0) == 0) & (j == 0))
        def _():
            hold_k[...] = jnp.zeros_like(hold_k)
            hold_v[...] = jnp.zeros_like(hold_v)

        @pl.when(j < n_step)
        def _():
            masks = [_band_mask(j == 0)] + [_band_mask(False)] * (n_sb - 1)
            low = lax.broadcasted_iota(jnp.int32, (BLOCK, LANES), 1) < HEAD_DIM
            units = [(sb, slice(p * LANES, (p + 1) * LANES)) for sb in range(n_sb) for p in range(a_w // LANES)]
            acc_k[...] = jnp.zeros_like(acc_k)
            acc_v[...] = jnp.zeros_like(acc_v)
            heads, scores, dps = [], [], []
            for sb, sl in units:
                qp = q_ref[_block_rows(sb), sl]
                k2 = _with_previous(kp_ref, kc_ref, sb, sl)
                v2 = _with_previous(vp_ref, vc_ref, sb, sl)
                dop = do_ref[_block_rows(sb), sl]
                prod = dop.astype(F32) * ya_ref[_block_rows(sb), sl].astype(F32)
                lsep = lse_ref[_block_rows(sb), :]
                first = 2 * (sl.start // LANES) * lse_w
                for hh, half in enumerate((low, ~low)):
                    qh = jnp.where(half, qp, jnp.zeros_like(qp))
                    doh = jnp.where(half, dop, jnp.zeros_like(dop))
                    dsum = jnp.sum(jnp.where(half, prod, 0.0), axis=-1, keepdims=True)
                    heads.append((qh, doh, dsum, lsep[:, first + hh * lse_w:first + hh * lse_w + 1], masks[sb]))
                    scores.append(_dot_nt(qh, k2))
                    dps.append(_dot_nt(doh, v2))
            dss, prs = [], []
            for (qh, doh, dsum, lse_h, valid), s, dp in zip(heads, scores, dps):
                pr = jnp.exp(jnp.where(valid, s * SM_SCALE, NEG) - lse_h)
                dss.append((pr * (dp - dsum) * SM_SCALE).astype(BF16))
                prs.append(pr.astype(BF16))
            for i, (sb, sl) in enumerate(units):
                k2 = _with_previous(kp_ref, kc_ref, sb, sl)
                h0, h1 = heads[2 * i], heads[2 * i + 1]
                dq_ref[_block_rows(sb), sl] = jnp.where(low, _dot(dss[2 * i], k2),
                                                        _dot(dss[2 * i + 1], k2)).astype(BF16)
                two = slice(sb * BLOCK, (sb + 2) * BLOCK)
                acc_k[two, sl] += _dot_tn(dss[2 * i], h0[0]) + _dot_tn(dss[2 * i + 1], h1[0])
                acc_v[two, sl] += _dot_tn(prs[2 * i], h0[1]) + _dot_tn(prs[2 * i + 1], h1[1])
            for out_ref, hold, acc in ((dk_ref, hold_k, acc_k), (dv_ref, hold_v, acc_v)):
                if single:
                    out_ref[...] = acc[BLOCK:, :].astype(BF16)
                    continue
                if n_sb > 1:
                    out_ref[:rows_step - BLOCK, :] = hold[:rows_step - BLOCK, :].astype(BF16)
                out_ref[last, :] = (hold[last, :] + acc[:BLOCK, :]).astype(BF16)
                hold[...] = acc[BLOCK:, :]
            kp_ref[...] = kc_ref[last, :]
            vp_ref[...] = vc_ref[last, :]

        @pl.when(j == n_step)
        def _():
            dk_ref[...] = hold_k[...].astype(BF16)
            dv_ref[...] = hold_v[...].astype(BF16)

    def blk(f):
        return pl.BlockSpec((None, rows_step, a_w), f)

    def cur(j):
        return jnp.minimum(j, n_step - 1)

    def late(j):
        return jnp.maximum(j - 1, 0)

    return pl.pallas_call(
        body, name=f"attn_bwd_{l}_d{dil}", grid=(dil, 1 if single else n_step + 1),
        in_specs=[blk(lambda r, j: (r, cur(j), 0)),
                  blk(lambda r, j: (r, cur(j), 1)),
                  blk(lambda r, j: (r, cur(j), v_col)),
                  blk(lambda r, j: (r, cur(j), 0)),
                  blk(lambda r, j: (r, cur(j), 0)),
                  pl.BlockSpec((None, rows_step, LANES), lambda r, j: (r, cur(j), 0))],
        out_specs=[blk(lambda r, j: (r, cur(j), 0)), blk(lambda r, j: (r, late(j), 0)),
                   blk(lambda r, j: (r, late(j), 0))],
        out_shape=[SDS((dil, seq, a_w), BF16)] * 3,
        scratch_shapes=[pltpu.VMEM((rows_step, a_w), F32), pltpu.VMEM((rows_step, a_w), F32),
                        pltpu.VMEM((rows_step + BLOCK, a_w), F32), pltpu.VMEM((rows_step + BLOCK, a_w), F32),
                        pltpu.VMEM((BLOCK, a_w), BF16), pltpu.VMEM((BLOCK, a_w), BF16)],
        compiler_params=_params(2),
    )(qk, qk, v, dya, ya, lse)


def _mixer_bwd_in(parts, proj, dpc, dxo, x_in, prm, wing, qkg, bd, l, tm):
    t_len, d_model = x_in.shape
    a_w = d_model // 2
    c_w = d_model - a_w
    c8 = wing.shape[-1]
    p_c = N_DEV * c8

    dils = DILATIONS[1:]
    n_in = 3 * len(DILATIONS)

    def body(*refs):
        nat, res = refs[:3], refs[3:n_in]
        (pa_ref, dpc_ref, dxo_ref, x_ref, prm_ref, w_ref, qkg_ref, bd_ref,
         dx_ref, dproj_ref, st_ref, qs_ref) = refs[n_in:n_in + 12]
        sums, wcat, wsems = refs[n_in + 12:n_in + 15], refs[n_in + 15], refs[n_in + 16]

        @pl.when(pl.program_id(0) == 0)
        def _():
            st_ref[...] = jnp.zeros_like(st_ref)
            qs_ref[...] = jnp.zeros_like(qs_ref)
            _join_columns(w_ref, wcat, wsems)

        for t in range(3):
            _to_groups(nat[t][...].astype(F32), sums[t])
            for i, dil in enumerate(dils):
                _merge_residues(res[3 * i + t], sums[t], dil, tm, add=True)
        bdm = bd_ref[...]
        grads = [_from_groups(sums[0]), _from_groups(sums[1])]
        for i in range(2):
            t = pa_ref[:, i * a_w:(i + 1) * a_w].astype(F32)
            rstd = lax.rsqrt(_head_mean(t * t, bdm) + EPS)
            that = t * rstd
            qs_ref[i:i + 1, :] += jnp.sum(grads[i] * that, axis=0, keepdims=True)
            dhat = grads[i] * qkg_ref[i:i + 1, :]
            dt = rstd * (dhat - that * _head_mean(dhat * that, bdm))
            dproj_ref[:, i * a_w:(i + 1) * a_w] = dt.astype(BF16)
        dproj_ref[:, 2 * a_w:3 * a_w] = _from_groups(sums[2]).astype(BF16)
        dproj_ref[:, 3 * a_w:] = dpc_ref[...]
        dh = jnp.concatenate([_dot_nt(dproj_ref[rows, :], wcat[...]) for rows in _row_halves(tm)], axis=0)
        dx, dshift, dscale, dng = _ada_bwd(dh, x_ref[...], prm_ref)
        dx_ref[...] = dxo_ref[...] + dx
        st_ref[0:1, :] += dshift
        st_ref[1:2, :] += dscale
        st_ref[2:3, :] += dng

    ta = pl.BlockSpec((tm, a_w), lambda m: (m, 0))
    tile = pl.BlockSpec((tm, d_model), lambda m: (m, 0))
    return pl.pallas_call(
        body, name=f"mixer_bwd_in_{l}", grid=(t_len // tm,),
        in_specs=[ta] * 3 + [_res_spec(dil, tm, a_w) for dil in dils for _ in range(3)]
                 + [pl.BlockSpec((tm, 3 * a_w), lambda m: (m, 0)),
                             pl.BlockSpec((tm, 3 * c_w), lambda m: (m, 0)),
                             tile, tile, pl.BlockSpec((8, d_model), lambda m: (0, 0)),
                             ANY,
                             pl.BlockSpec((8, a_w), lambda m: (0, 0)),
                             pl.BlockSpec(bd.shape, lambda m: (0, 0))],
        out_specs=[tile, pl.BlockSpec((tm, p_c), lambda m: (m, 0)),
                   pl.BlockSpec((8, d_model), lambda m: (0, 0)), pl.BlockSpec((8, a_w), lambda m: (0, 0))],
        out_shape=[SDS((t_len, d_model), F32), SDS((t_len, p_c), BF16), SDS((8, d_model), F32), SDS((8, a_w), F32)],
        scratch_shapes=[_groups(tm, a_w)] * 3 + [pltpu.VMEM((d_model, p_c), BF16), pltpu.SemaphoreType.DMA((N_DEV,))],
        compiler_params=_params(1),
    )(*[t.reshape(t_len, a_w) for t in parts[0]], *[t for p in parts[1:] for t in p],
      proj, dpc, dxo, x_in, prm, wing, qkg, bd)


def _loss_head(x, target, tm):
    t_len, d_model = x.shape

    def body(x_ref, t_ref, dx_ref, loss_ref):
        @pl.when(pl.program_id(0) == 0)
        def _():
            loss_ref[...] = jnp.zeros_like(loss_ref)

        diff = x_ref[...] - t_ref[...]
        dx_ref[...] = diff * (1.0 / d_model)
        per_token = jnp.sum(diff * diff, axis=-1, keepdims=True) * (1.0 / d_model)
        loss_ref[...] += 0.5 * jnp.sum(per_token)

    tile = pl.BlockSpec((tm, d_model), lambda m: (m, 0))
    return pl.pallas_call(
        body, name="loss_head", grid=(t_len // tm,),
        in_specs=[tile, tile],
        out_specs=[tile, pl.BlockSpec((8, LANES), lambda m: (0, 0))],
        out_shape=[SDS((t_len, d_model), F32), SDS((8, LANES), F32)],
        compiler_params=_params(1),
    )(x, target)


def _row_tile(rows, cols):
    best = None
    for t in range(16, rows + 1, 16):
        if rows % t == 0 and t * cols * 4 <= (1 << 20):
            best = t
    return best if best is not None else rows


def _adamw(pieces, w, m, v, name):
    n_q = len(pieces)
    n_p, rows, cols = pieces[0].shape
    tr = _row_tile(rows, cols)
    n_i = rows // tr
    c1 = 1.0 - ADAM_B1 ** ADAM_STEP
    c2 = 1.0 - ADAM_B2 ** ADAM_STEP

    def body(*refs):
        p_refs = refs[:n_q]
        w_ref, m_ref, v_ref, g_ref, d_ref, nm_ref, nv_ref = refs[n_q:]
        for q in range(n_q):
            @pl.when(pl.program_id(0) == q)
            def _():
                g = p_refs[q][0].astype(F32)
                for i in range(1, n_p):
                    g = g + p_refs[q][i].astype(F32)
                g_ref[...] = g
                nm = ADAM_B1 * m_ref[...] + (1.0 - ADAM_B1) * g
                nv = ADAM_B2 * v_ref[...] + (1.0 - ADAM_B2) * (g * g)
                nm_ref[...] = nm
                nv_ref[...] = nv
                d_ref[...] = -ADAM_LR * ((nm / c1) / (jnp.sqrt(nv / c2) + ADAM_EPS) + ADAM_WD * w_ref[...])

    def piece_spec(q):
        return pl.BlockSpec((n_p, tr, cols), lambda a, i: (0, jnp.where(a == q, i, 0), 0))

    tile = pl.BlockSpec((tr, cols), lambda a, i: (a * n_i + i, 0))
    return pl.pallas_call(
        body, name=name, grid=(n_q, n_i),
        in_specs=[piece_spec(q) for q in range(n_q)] + [tile, tile, tile],
        out_specs=[tile] * 4,
        out_shape=[SDS((n_q * rows, cols), F32)] * 4,
        compiler_params=_params(2),
    )(*pieces, w, m, v)


def _pack(vecs):
    flat = jnp.concatenate([v.reshape(-1).astype(F32) for v in vecs])
    rows = -(-flat.shape[0] // (8 * LANES)) * 8
    return jnp.pad(flat, (0, rows * LANES - flat.shape[0])).reshape(rows, LANES)


def _unpack(packed, shapes):
    lead = packed.shape[:-2]
    flat = packed.reshape(lead + (-1,))
    out, off = [], 0
    for s in shapes:
        size = 1
        for d in s:
            size *= d
        out.append(flat[..., off:off + size].reshape(lead + tuple(s)))
        off += size
    return out


def kernel(x, c, w_ada, b_ada, norm_g, w_in, q_norm_g, k_norm_g, conv_w, conv_b, w_out, ffn_w1, ffn_w2, loss_target, m_w_ada, m_b_ada, m_norm_g, m_w_in, m_q_norm_g, m_k_norm_g, m_conv_w, m_conv_b, m_w_out, m_ffn_w1, m_ffn_w2, v_w_ada, v_b_ada, v_norm_g, v_w_in, v_q_norm_g, v_k_norm_g, v_conv_w, v_conv_b, v_w_out, v_ffn_w1, v_ffn_w2):
    _, t_len, d_model = x.shape
    n_layer = w_ada.shape[0]
    a_w = d_model // 2
    c_w = d_model - a_w
    n_head = a_w // HEAD_DIM
    ada_cols = w_ada.shape[-1]
    tm = min(TOKEN_TILE, t_len)
    tm_ffn = min(FFN_FWD_TOKEN_TILE, t_len)
    tm_ffn_bwd = min(FFN_BWD_TOKEN_TILE, t_len)
    me = _index(_my_place())
    x2 = x.reshape(t_len, d_model)
    target = loss_target.reshape(t_len, d_model)

    small_shapes = [(d_model,), norm_g.shape, conv_w.shape]
    gathered = _allgather_small(_pack([c, norm_g, conv_w]), "gather_small_inputs")
    c_all, ng_parts, cw_parts = _unpack(gathered, small_shapes)
    norm_g_full = jnp.moveaxis(ng_parts, 0, 2).reshape(n_layer, 3, d_model)
    conv_w_full = jnp.moveaxis(cw_parts, 0, 2).reshape(n_layer, 3, c_w)

    b_loc = lax.dynamic_slice_in_dim(b_ada, me * ada_cols, ada_cols, axis=1).reshape(n_layer, 1, ada_cols)
    mod_part, silu_c = _mod_part(c_all, w_ada, b_loc)
    mod_all = _allgather_small(_pack([mod_part]), "gather_mod")
    mod_all = _unpack(mod_all, [mod_part.shape])[0]
    mod_mine = lax.dynamic_index_in_dim(mod_all, me, axis=2, keepdims=False)
    mod = jnp.moveaxis(mod_mine, 0, 1).reshape(n_layer, 3, 3, d_model)

    def prm_of(l, sub):
        rows = jnp.stack([norm_g_full[l, sub], mod[l, sub, 0], mod[l, sub, 1], mod[l, sub, 2]])
        return jnp.pad(rows, ((0, 4), (0, 0)))

    prm = [[prm_of(l, sub) for sub in range(3)] for l in range(n_layer)]
    qkg = [jnp.pad(jnp.stack([jnp.tile(q_norm_g[l], n_head), jnp.tile(k_norm_g[l], n_head)]), ((0, 6), (0, 0)))
           for l in range(n_layer)]
    cprm = [jnp.pad(jnp.concatenate([conv_w_full[l], conv_b[l][None]]), ((0, 4), (0, 0))) for l in range(n_layer)]
    head_of = jnp.arange(min(MXU_WIDTH, a_w)) // HEAD_DIM
    bd = (head_of[:, None] == head_of[None, :]).astype(BF16)
    expand = (jnp.arange(LANES)[:, None] == (jnp.arange(a_w)[None, :] // HEAD_DIM) * (LANES // n_head)).astype(BF16)

    f8 = ffn_w1.shape[-1]
    r8 = ffn_w2.shape[-2]
    c8 = w_in.shape[-1]
    o8 = w_out.shape[-2]
    me_arr = jnp.reshape(me, (1,)).astype(jnp.int32)
    win_b = [_place_own(w_in, (l,), me_arr, f"own_w_in_{l}") for l in range(n_layer)]
    wout_b = [_place_own(w_out, (l,), me_arr, f"own_w_out_{l}") for l in range(n_layer)]
    w1_b = [[_place_own(ffn_w1, (l, s), me_arr, f"own_w1_{l}_{s}") for s in range(2)] for l in range(n_layer)]
    w2_b = [[_place_own(ffn_w2, (l, s), me_arr, f"own_w2_{l}_{s}") for s in range(2)] for l in range(n_layer)]
    w1_first, w2_first = _allgather_big([w1_b[0][0], w2_b[0][0]], "gather_first_weights")
    groups = [[win_b[0], wout_b[0]], [w1_b[0][1], w2_b[0][1]]]
    for l in range(1, n_layer):
        groups += [[w1_b[l][0], w2_b[l][0]], [win_b[l], wout_b[l]], [w1_b[l][1], w2_b[l][1]]]
    flat = [a for grp in groups for a in grp]
    w_sends, w_recvs, w_lands, w_token = _send_start(flat, None, w1_first, "weights_start")

    def gathered(gi, after):
        lo = sum(len(grp) for grp in groups[:gi])
        hi = lo + len(groups[gi])
        return _send_wait(w_lands[lo:hi], w_sends[lo:hi], w_recvs[lo:hi], after, f"weights_wait_{gi}")

    def chunked(w1g):
        return w1g.reshape(2, N_CHUNK, d_model, f8)

    saved = []
    xc = x2
    for l in range(n_layer):
        keep = {"x0": xc}
        if l == 0:
            w1g, w2g = w1_first, w2_first
            prm_first = prm[0][0] + w_token[0, 0]
        else:
            w1g, w2g = gathered(3 * l - 1, xc)
            prm_first = prm[l][0]
        keep["w_ffn0"] = (chunked(w1g), w2g)
        keep["act0"], keep["h0"], keep["y0"], xc = _ffn_fwd(xc, prm_first, *keep["w_ffn0"], l, 0, tm_ffn)
        keep["x1"] = xc
        wing, woutg = gathered(0 if l == 0 else 3 * l, xc)
        keep.update(wing=wing, woutg=woutg)
        proj, qkn, keep["h1"], ycv, *res = _mixer_in(xc, prm[l][1], wing, qkg[l], cprm[l], bd, l, tm)
        qkv = [(qkn[None], proj[None], 2)] + [(res[2 * i], res[2 * i + 1], 0) for i in range(len(DILATIONS) - 1)]
        branches = [_attn_fwd(*t, l) for t in qkv]
        cat, lse, *res = _attn_combine([b[0] for b in branches], [b[1] for b in branches], ycv, expand, tm, l)
        n_r = len(DILATIONS) - 1
        keep["ya"] = [cat[None]] + res[:n_r]
        keep["lse"] = [lse[None]] + res[n_r:]
        xc, keep["ymix"] = _mixer_out(cat, xc, prm[l][1], woutg, l, tm)
        keep.update(proj=proj, qkv=qkv, cat=cat)
        keep["x2"] = xc
        w1g, w2g = gathered(1 if l == 0 else 3 * l + 1, xc)
        keep["w_ffn2"] = (chunked(w1g), w2g)
        keep["act2"], keep["h2"], keep["y2"], xc = _ffn_fwd(xc, prm[l][2], *keep["w_ffn2"], l, 1, tm_ffn)
        saved.append(keep)

    dx, loss_blk = _loss_head(xc, target, tm)
    loss = lax.psum(loss_blk[0, 0], ("x", "y", "c"))

    tk = min(GRAD_TOKEN_TILE, t_len)
    stats =[[None] * 3 for _ in range(n_layer)]
    conv_stats = [None] * n_layer
    qk_stats = [None] * n_layer

    started = {}
    dummy = jnp.zeros((8, LANES), F32)

    def start_exchange(g, name):
        sends, recvs, thru, token = _send_start(
            [_place_own(g, None, me_arr, f"own_grad_{name}")], [g], dummy, f"grads_start_{name}")
        started[name] = (thru[0], thru[1], sends[0], recvs[0])
        return token

    def wait_exchange(names, after, name):
        ent = [started[k] for k in names]
        done = _send_wait([e[0] for e in ent] + [e[1] for e in ent], [e[2] for e in ent], [e[3] for e in ent],
                          after, name)
        return dict(zip(names, done))

    def ffn_backward(dxo, keep, l, sub, which, prm_dep):
        dx_in, da, dy, st = _ffn_bwd(dxo, keep[f"y{which}"], keep[f"x{which}"], prm_dep,
                                     keep[f"act{which}"], *keep[f"w_ffn{which}"], l, sub, tm_ffn_bwd)
        g_w2 = _matmul_tn(
            keep[f"act{which}"], dy,
            pl.BlockSpec((None, None, tk, f8), lambda p, k: (2, p, k, 0)),
            pl.BlockSpec((tk, d_model), lambda p, k: (k, 0)),
            SDS((N_CHUNK, f8, d_model), BF16), pl.BlockSpec((None, f8, d_model), lambda p, k: (p, 0, 0)),
            N_CHUNK, t_len, tk, f"grad_w2_{l}_{sub}", dummy)
        token = start_exchange(g_w2.reshape(N_DEV, r8, d_model), f"w2_{l}_{sub}")
        g_w1 = _matmul_tn(
            keep[f"h{which}"], da.reshape(N_DEV, t_len, f8),
            pl.BlockSpec((tk, d_model), lambda p, k: (k, 0)),
            pl.BlockSpec((None, tk, f8), lambda p, k: (p, k, 0)),
            SDS((N_DEV, d_model, f8), BF16), pl.BlockSpec((None, d_model, f8), lambda p, k: (p, 0, 0)),
            N_DEV, t_len, tk, f"grad_w1_{l}_{sub}", token)
        token = start_exchange(g_w1, f"w1_{l}_{sub}")
        return dx_in, st, token[0, 0]

    token = 0.0
    for l in reversed(range(n_layer)):
        keep = saved[l]
        wing, woutg = keep["wing"], keep["woutg"]
        dx, stats[l][2], token = ffn_backward(dx, keep, l, 1, 2, prm[l][2] + token)
        dya, dpc, dy, st_gate, conv_stats[l], *dya_res = _mixer_bwd_out(
            dx, keep["ymix"], prm[l][1] + token, woutg, keep["proj"], cprm[l], l, tm)
        dyas = [dya[None]] + dya_res
        g_wout = _matmul_tn(
            keep["cat"], dy,
            pl.BlockSpec((tk, d_model), lambda p, k: (k, 0)),
            pl.BlockSpec((tk, d_model), lambda p, k: (k, 0)),
            SDS((d_model, d_model), BF16), pl.BlockSpec((d_model, d_model), lambda p, k: (0, 0)),
            1, t_len, tk, f"grad_wout_{l}", dummy)
        token = start_exchange(g_wout.reshape(N_DEV, o8, d_model), f"wout_{l}")[0, 0]
        parts = [_attn_bwd(*keep["qkv"][i], dyas[i], keep["ya"][i], keep["lse"][i], l)
                 for i in range(len(DILATIONS))]
        dx, dproj, st_norm, qk_stats[l] = _mixer_bwd_in(
            parts, keep["proj"], dpc, dx, keep["x1"], prm[l][1] + token, wing, qkg[l], bd, l, tm)
        stats[l][1] = st_norm + st_gate
        g_win = _matmul_tn(
            keep["h1"], dproj,
            pl.BlockSpec((tk, d_model), lambda p, k: (k, 0)),
            pl.BlockSpec((tk, N_DEV // 2 * c8), lambda p, k: (k, p)),
            SDS((N_DEV, d_model, c8), BF16), pl.BlockSpec((N_DEV // 2, d_model, c8), lambda p, k: (p, 0, 0)),
            2, t_len, tk, f"grad_win_{l}", dummy)
        token = start_exchange(g_win, f"win_{l}")[0, 0]
        dx, stats[l][0], token = ffn_backward(dx, keep, l, 0, 0, prm[l][0] + token)
    grad_x = dx.reshape(x.shape)

    def update(pieces, w, m, v, name):
        shape = w.shape
        cols = shape[-1]
        outs = _adamw([p.reshape(p.shape[0], -1, cols) for p in pieces], w.reshape(-1, cols),
                      m.reshape(-1, cols), v.reshape(-1, cols), name)
        return [o.reshape(shape) for o in outs]

    last = "w1_0_0"
    lands = wait_exchange([k for k in started if k != last], dx, "grads_wait_early")
    res = {
        "w_in": update([lands[f"win_{l}"] for l in range(n_layer)], w_in, m_w_in, v_w_in, "adamw_w_in"),
        "w_out": update([lands[f"wout_{l}"] for l in range(n_layer)], w_out, m_w_out, v_w_out, "adamw_w_out"),
        "ffn_w2": update([lands[f"w2_{l}_{s}"] for l in range(n_layer) for s in range(2)],
                         ffn_w2, m_ffn_w2, v_ffn_w2, "adamw_ffn_w2"),
    }

    dmod = jnp.stack([jnp.stack([stats[l][sub][r] for sub in range(3) for r in (0, 1, 3)]) for l in range(n_layer)])
    dng = jnp.stack([jnp.stack([stats[l][sub][2] for sub in range(3)]) for l in range(n_layer)])
    dcw = jnp.stack([conv_stats[l][0:3] for l in range(n_layer)])
    dcb = jnp.stack([conv_stats[l][3] for l in range(n_layer)])
    dqk = jnp.stack([qk_stats[l][0:2].reshape(2, n_head, HEAD_DIM).sum(axis=1) for l in range(n_layer)])
    part_shapes = [dmod.shape, dng.shape, dcw.shape, dcb.shape, dqk.shape]
    parts_all = _allgather_small(_pack([dmod, dng, dcw, dcb, dqk]), "gather_small_grads")
    dmod_all = _unpack(parts_all, part_shapes)[0].reshape(N_DEV, n_layer, 9 * d_model)
    summed = _unpack(_sum_devices(parts_all), part_shapes)
    g_b_ada = summed[0].reshape(n_layer, 9 * d_model)
    ng_cols = norm_g.shape[-1]
    g_norm_g = lax.dynamic_slice_in_dim(summed[1], me * ng_cols, ng_cols, axis=2)
    cw_cols = conv_w.shape[-1]
    g_conv_w = lax.dynamic_slice_in_dim(summed[2], me * cw_cols, cw_cols, axis=2)
    g_conv_b = summed[3]
    g_q, g_k = summed[4][:, 0], summed[4][:, 1]
    dmod_loc = lax.dynamic_slice_in_dim(dmod_all, me * ada_cols, ada_cols, axis=2)
    g_w_ada = _w_ada_grad(silu_c.T, jnp.moveaxis(dmod_loc, 0, 1))
    res["w_ada"] = update([g_w_ada[None]], w_ada, m_w_ada, v_w_ada, "adamw_w_ada")

    names = ["b_ada", "norm_g", "q_norm_g", "k_norm_g", "conv_w", "conv_b"]
    g_small = [g_b_ada, g_norm_g, g_q, g_k, g_conv_w, g_conv_b]
    w_small = [b_ada, norm_g, q_norm_g, k_norm_g, conv_w, conv_b]
    m_small = [m_b_ada, m_norm_g, m_q_norm_g, m_k_norm_g, m_conv_w, m_conv_b]
    v_small = [v_b_ada, v_norm_g, v_q_norm_g, v_k_norm_g, v_conv_w, v_conv_b]
    outs = _adamw([_pack(g_small)[None]], _pack(w_small), _pack(m_small), _pack(v_small), "adamw_small")
    shapes = [w.shape for w in w_small]
    unpacked = [_unpack(o, shapes) for o in outs]
    for i, nme in enumerate(names):
        res[nme] = [unpacked[k][i] for k in range(4)]

    follows = (res["w_in"][1][0, :1, :1] + res["w_out"][1][0, :1, :1] + res["ffn_w2"][1][0, 0, :1, :1]
               + res["w_ada"][1][0, :1, :1] + outs[1][:1, :1])
    lands.update(wait_exchange([last], follows, "grads_wait_last"))
    res["ffn_w1"] = update([lands[f"w1_{l}_{s}"] for l in range(n_layer) for s in range(2)],
                           ffn_w1, m_ffn_w1, v_ffn_w1, "adamw_ffn_w1")

    order = ["w_ada", "b_ada", "norm_g", "w_in", "q_norm_g", "k_norm_g", "conv_w", "conv_b", "w_out", "ffn_w1", "ffn_w2"]
    return (loss, grad_x, *[res[n][0] for n in order], *[res[n][1] for n in order],
            *[res[n][2] for n in order], *[res[n][3] for n in order])
```

```python
import jax
import jax.numpy as jnp
from jax import lax
from jax.experimental import pallas as pl
from jax.experimental.pallas import tpu as pltpu

F32 = jnp.float32
BF16 = jnp.bfloat16
SDS = jax.ShapeDtypeStruct

N_DEV = 8
HEAD_DIM = 64
BLOCK = 128
DILATIONS = (1, 4, 16)
ATTN_BLOCKS_PER_STEP = 4
N_CHUNK = 4
EPS = 1e-6
NEG = -1e30
SM_SCALE = HEAD_DIM ** -0.5
LANES = 128
MXU_WIDTH = 256
TOKEN_TILE = 512
FFN_ROW_PART = 256
FFN_FWD_TOKEN_TILE = 1024
FFN_BWD_TOKEN_TILE = 512
GRAD_TOKEN_TILE = 2048
VMEM_LIMIT_BYTES = 56 * 1024 * 1024

ADAM_LR = 0.001
ADAM_B1 = 0.9
ADAM_B2 = 0.999
ADAM_EPS = 1e-08
ADAM_WD = 0.01
ADAM_STEP = 10

MESH_ID = pl.DeviceIdType.MESH
ANY = pl.BlockSpec(memory_space=pl.ANY)
HBM_SPEC = pl.BlockSpec(memory_space=pltpu.HBM)
SEM_SPEC = pl.BlockSpec(memory_space=pltpu.SEMAPHORE)
SIDE_EFFECT = pltpu.SideEffectType.DATAFLOW_SIDE_EFFECTING


def _params(n_axes):
    return pltpu.CompilerParams(dimension_semantics=("arbitrary",) * n_axes,
                                vmem_limit_bytes=VMEM_LIMIT_BYTES)


def _dot(a, b):
    return jnp.dot(a, b, preferred_element_type=F32)


def _dot_nt(a, b):
    return lax.dot_general(a, b, (((1,), (1,)), ((), ())), preferred_element_type=F32)


def _dot_tn(a, b):
    return lax.dot_general(a, b, (((0,), (0,)), ((), ())), preferred_element_type=F32)


def _row_halves(n):
    return (slice(0, n // 2), slice(n // 2, n))


def _split_bf16(t):
    hi = t.astype(BF16)
    return hi, (t - hi.astype(F32)).astype(BF16)


def _my_place():
    x, y, c = lax.axis_index("x"), lax.axis_index("y"), lax.axis_index("c")
    return x, y, c


def _peer(place, k):
    x, y, c = place
    return ((1 - x) if k & 4 else x, (1 - y) if k & 2 else y, (1 - c) if k & 1 else c)


def _index(place):
    return 4 * place[0] + 2 * place[1] + place[2]


def _allgather_small(v, name):
    rows, cols = v.shape

    def body(x_ref, out_ref, send_sems, recv_sems, local_sem):
        me = _my_place()
        mine = pltpu.make_async_copy(x_ref, out_ref.at[_index(me)], local_sem)
        mine.start()
        sends = []
        for k in range(1, N_DEV):
            cp = pltpu.make_async_remote_copy(
                src_ref=x_ref, dst_ref=out_ref.at[_index(me)], send_sem=send_sems.at[k - 1],
                recv_sem=recv_sems.at[k - 1], device_id=_peer(me, k), device_id_type=MESH_ID)
            cp.start()
            sends.append(cp)
        for k in range(1, N_DEV):
            pltpu.make_async_remote_copy(
                src_ref=x_ref, dst_ref=out_ref.at[_index(_peer(me, k))], send_sem=send_sems.at[k - 1],
                recv_sem=recv_sems.at[k - 1], device_id=_peer(me, k), device_id_type=MESH_ID).wait_recv()
        for cp in sends:
            cp.wait_send()
        mine.wait()

    return pl.pallas_call(
        body, name=name,
        out_shape=SDS((N_DEV, rows, cols), v.dtype),
        in_specs=[pl.BlockSpec(memory_space=pltpu.VMEM)],
        out_specs=pl.BlockSpec(memory_space=pltpu.VMEM),
        scratch_shapes=[pltpu.SemaphoreType.DMA((N_DEV - 1,)), pltpu.SemaphoreType.DMA((N_DEV - 1,)),
                        pltpu.SemaphoreType.DMA],
    )(v)


def _allgather_big(lands, name):
    n = len(lands)

    def body(*refs):
        outs = refs[n:2 * n]
        send_sems, recv_sems = refs[2 * n:]
        x, y, c = _my_place()
        me, sib = (x, y, c), (x, y, 1 - c)
        chips = [(1 - x, y), (x, 1 - y), (1 - x, 1 - y)]

        def copy(i, k, block, to):
            dst = outs[i].at[_index(block)]
            return pltpu.make_async_remote_copy(
                src_ref=dst, dst_ref=dst, send_sem=send_sems.at[7 * i + k],
                recv_sem=recv_sems.at[7 * i + k], device_id=to, device_id_type=MESH_ID)

        first = []
        for i in range(n):
            first.append(copy(i, 0, me, sib))
            for j, chip in enumerate(chips):
                first.append(copy(i, 1 + j, me, (*chip, c)))
        for cp in first:
            cp.start()
        passed = []
        for i in range(n):
            for j, chip in enumerate(chips):
                copy(i, 1 + j, (*chip, c), me).wait_recv()
                fwd = copy(i, 4 + j, (*chip, c), sib)
                fwd.start()
                passed.append(fwd)
        for i in range(n):
            copy(i, 0, sib, me).wait_recv()
            for j, chip in enumerate(chips):
                copy(i, 4 + j, (*chip, 1 - c), me).wait_recv()
        for cp in first + passed:
            cp.wait_send()

    return pl.pallas_call(
        body, name=name,
        out_shape=[SDS(a.shape, a.dtype) for a in lands],
        in_specs=[ANY] * n, out_specs=[ANY] * n,
        input_output_aliases={i: i for i in range(n)},
        scratch_shapes=[pltpu.SemaphoreType.DMA((7 * n,)), pltpu.SemaphoreType.DMA((7 * n,))],
    )(*lands)


def _place_own(src, lead, me_arr, name):
    rows, cols = src.shape[-2:]
    tr = _row_tile(rows, cols)

    def body(me_ref, s_ref, o_ref):
        o_ref[...] = s_ref[...].astype(BF16)

    if lead is None:
        in_spec = pl.BlockSpec((None, tr, cols), lambda i, me_ref: (me_ref[0], i, 0))
    else:
        in_spec = pl.BlockSpec((None,) * len(lead) + (tr, cols), lambda i, me_ref: (*lead, i, 0))
    return pl.pallas_call(
        body, name=name,
        grid_spec=pltpu.PrefetchScalarGridSpec(
            num_scalar_prefetch=1, grid=(rows // tr,), in_specs=[in_spec],
            out_specs=pl.BlockSpec((None, tr, cols), lambda i, me_ref: (me_ref[0], i, 0))),
        out_shape=SDS((N_DEV, rows, cols), BF16),
        compiler_params=_params(1),
    )(me_arr, src)


def _send_start(lands, pieces, after, name):
    n = len(lands)
    arrs = list(lands) + list(pieces or [])
    n_a = len(arrs)

    def body(*refs):
        land_r, piece_r = refs[:n], refs[n:n_a]
        send, recv = refs[n_a + 1:n_a + 1 + n], refs[n_a + 1 + n:n_a + 1 + 2 * n]
        token = refs[-1]
        me = _my_place()
        for i in range(n):
            for k in range(1, N_DEV):
                peer = _peer(me, k)
                src = piece_r[i].at[_index(peer)] if pieces else land_r[i].at[_index(me)]
                pltpu.make_async_remote_copy(
                    src_ref=src, dst_ref=land_r[i].at[_index(me)], send_sem=send[i], recv_sem=recv[i],
                    device_id=peer, device_id_type=MESH_ID).start()
        token[...] = jnp.zeros_like(token)

    outs = pl.pallas_call(
        body, name=name,
        out_shape=[pltpu.SemaphoreType.DMA(())] * (2 * n) + [pltpu.HBM(a.shape, a.dtype) for a in arrs]
        + [SDS((8, LANES), F32)],
        in_specs=[HBM_SPEC] * n_a + [ANY],
        out_specs=[SEM_SPEC] * (2 * n) + [HBM_SPEC] * n_a + [pl.BlockSpec(memory_space=pltpu.VMEM)],
        input_output_aliases={i: 2 * n + i for i in range(n_a)},
        compiler_params=pltpu.CompilerParams(has_side_effects=SIDE_EFFECT),
    )(*[pltpu.with_memory_space_constraint(a, pltpu.HBM) for a in arrs], after)
    return outs[:n], outs[n:2 * n], outs[2 * n:2 * n + n_a], outs[-1]


def _send_wait(thru, sends, recvs, after, name):
    n = len(sends)
    n_a = len(thru)

    def body(*refs):
        land_r = refs[:n]
        send, recv = refs[n_a:n_a + n], refs[n_a + n:n_a + 2 * n]
        me = _my_place()
        for i in range(n):
            seven = land_r[i].at[pl.ds(0, N_DEV - 1)]
            cp = pltpu.make_async_remote_copy(src_ref=seven, dst_ref=seven, send_sem=send[i], recv_sem=recv[i],
                                              device_id=_peer(me, 1), device_id_type=MESH_ID)
            cp.wait_send()
            cp.wait_recv()

    outs = pl.pallas_call(
        body, name=name,
        out_shape=[pltpu.HBM(a.shape, a.dtype) for a in thru],
        in_specs=[HBM_SPEC] * n_a + [SEM_SPEC] * (2 * n) + [ANY],
        out_specs=[HBM_SPEC] * n_a,
        input_output_aliases={i: i for i in range(n_a)},
        compiler_params=pltpu.CompilerParams(has_side_effects=SIDE_EFFECT),
    )(*thru, *sends, *recvs, after)
    return outs[:n]


def _ada_fwd(x, prm_ref):
    rstd = lax.rsqrt(jnp.mean(x * x, axis=-1, keepdims=True) + EPS)
    return (x * rstd * prm_ref[0:1, :]) * (1.0 + prm_ref[2:3, :]) + prm_ref[1:2, :]


def _ada_bwd(dh, x, prm_ref):
    ng, ops = prm_ref[0:1, :], 1.0 + prm_ref[2:3, :]
    rstd = lax.rsqrt(jnp.mean(x * x, axis=-1, keepdims=True) + EPS)
    xhat = x * rstd
    dxhat = dh * (ops * ng)
    dx = rstd * (dxhat - xhat * jnp.mean(dxhat * xhat, axis=-1, keepdims=True))
    dshift = jnp.sum(dh, axis=0, keepdims=True)
    dscale = jnp.sum(dh * (xhat * ng), axis=0, keepdims=True)
    dng = jnp.sum(dh * (xhat * ops), axis=0, keepdims=True)
    return dx, dshift, dscale, dng


def _head_mean(t, bd):
    hi = t.astype(BF16)
    g = bd.shape[0]
    parts = [_dot(hi[:, i:i + g], bd) for i in range(0, t.shape[1], g)]
    return jnp.concatenate(parts, axis=1) * (1.0 / HEAD_DIM)


def _join_columns(w_hbm, dst_ref, sems):
    cols = w_hbm.shape[2]
    copies = [pltpu.make_async_copy(w_hbm.at[j], dst_ref.at[:, pl.ds(j * cols, cols)], sems.at[j])
              for j in range(w_hbm.shape[0])]
    for cp in copies:
        cp.start()
    for cp in copies:
        cp.wait()


def _shift_rows(z, before, tm):
    row = lax.broadcasted_iota(jnp.int32, z.shape, 0)
    z1 = jnp.where(row == 0, before[7:8, :], pltpu.roll(z, 1, 0))
    z2 = jnp.where(row == 0, before[6:7, :], jnp.where(row == 1, before[7:8, :], pltpu.roll(z, 2, 0)))
    return z1, z2


def _mod_part(c_all, w_ada, b_loc):
    n_layer, d_model, cols = w_ada.shape

    def body(c_ref, w_ref, b_ref, out_ref, sc_ref):
        cv = c_ref[...]
        sc = cv * jax.nn.sigmoid(cv)
        sc_ref[...] = sc
        a_hi, a_lo = _split_bf16(sc)
        w_hi, w_lo = _split_bf16(w_ref[...])
        out_ref[...] = _dot(a_hi, w_hi) + _dot(a_hi, w_lo) + _dot(a_lo, w_hi) + b_ref[...]

    return pl.pallas_call(
        body, name="mod_part", grid=(n_layer,),
        in_specs=[pl.BlockSpec((N_DEV, d_model), lambda l: (0, 0)),
                  pl.BlockSpec((None, d_model, cols), lambda l: (l, 0, 0)),
                  pl.BlockSpec((None, 1, cols), lambda l: (l, 0, 0))],
        out_specs=[pl.BlockSpec((None, N_DEV, cols), lambda l: (l, 0, 0)),
                   pl.BlockSpec((N_DEV, d_model), lambda l: (0, 0))],
        out_shape=[SDS((n_layer, N_DEV, cols), F32), SDS((N_DEV, d_model), F32)],
        compiler_params=_params(1),
    )(c_all, w_ada, b_loc)


def _w_ada_grad(sc_t, dmod_loc):
    d_model = sc_t.shape[0]
    n_layer, _, cols = dmod_loc.shape

    def body(s_ref, d_ref, out_ref):
        acc = s_ref[:, 0:1] * d_ref[0:1, :]
        for b in range(1, N_DEV):
            acc = acc + s_ref[:, b:b + 1] * d_ref[b:b + 1, :]
        out_ref[...] = acc

    return pl.pallas_call(
        body, name="w_ada_grad", grid=(n_layer,),
        in_specs=[pl.BlockSpec((d_model, N_DEV), lambda l: (0, 0)),
                  pl.BlockSpec((None, N_DEV, cols), lambda l: (l, 0, 0))],
        out_specs=pl.BlockSpec((None, d_model, cols), lambda l: (l, 0, 0)),
        out_shape=SDS((n_layer, d_model, cols), F32),
        compiler_params=_params(1),
    )(sc_t, dmod_loc)


def _sum_devices(g):
    _, rows, cols = g.shape

    def body(g_ref, out_ref):
        acc = g_ref[0]
        for i in range(1, N_DEV):
            acc = acc + g_ref[i]
        out_ref[...] = acc

    return pl.pallas_call(body, name="sum_devices", out_shape=SDS((rows, cols), F32))(g)


def _ffn_fwd(x, prm, w1r, w2g, l, sub, tm):
    t_len, d_model = x.shape
    f8, r8 = w1r.shape[-1], w2g.shape[-2]

    def body(x_ref, prm_ref, w1_ref, w2_ref, act_ref, h_ref, y_ref, xo_ref, h_sc, acc):
        j = pl.program_id(1)

        @pl.when(j == 0)
        def _():
            h = _ada_fwd(x_ref[...], prm_ref).astype(BF16)
            h_sc[...] = h
            h_ref[...] = h
            acc[...] = jnp.zeros_like(acc)

        w2 = w2_ref[...].reshape(2 * r8, d_model)
        parts = [slice(i, i + FFN_ROW_PART) for i in range(0, tm, FFN_ROW_PART)]

        def first(rows):
            return _dot(h_sc[rows, :], w1_ref[0]), _dot(h_sc[rows, :], w1_ref[1])

        gu = first(parts[0])
        for i, rows in enumerate(parts):
            g, u = gu
            if i + 1 < len(parts):
                gu = first(parts[i + 1])
            sig = jax.nn.sigmoid(g)
            sil = g * sig
            s = (sil * u).astype(BF16)
            act_ref[0, rows, :] = (u * (sig * (1.0 + g * (1.0 - sig)))).astype(BF16)
            act_ref[1, rows, :] = sil.astype(BF16)
            act_ref[2, rows, :] = s
            acc[rows, :] += _dot(s, w2)

        @pl.when(j == N_CHUNK - 1)
        def _():
            yv = acc[...]
            y_ref[...] = yv.astype(BF16)
            xo_ref[...] = x_ref[...] + (0.5 * prm_ref[3:4, :]) * yv

    tile = pl.BlockSpec((tm, d_model), lambda m, j: (m, 0))
    once = pl.BlockSpec((tm, d_model), lambda m, j: (m, 0), pipeline_mode=pl.Buffered(1))
    return pl.pallas_call(
        body, name=f"ffn_fwd_{l}_{sub}", grid=(t_len // tm, N_CHUNK),
        in_specs=[tile, pl.BlockSpec((8, d_model), lambda m, j: (0, 0)),
                  pl.BlockSpec((2, None, d_model, f8), lambda m, j: (0, j, 0, 0)),
                  pl.BlockSpec((2, r8, d_model), lambda m, j: (j, 0, 0))],
        out_specs=[pl.BlockSpec((3, None, tm, f8), lambda m, j: (0, j, m, 0)), once, once, tile],
        out_shape=[SDS((3, N_CHUNK, t_len, f8), BF16), SDS((t_len, d_model), BF16),
                   SDS((t_len, d_model), BF16), SDS((t_len, d_model), F32)],
        scratch_shapes=[pltpu.VMEM((tm, d_model), BF16), pltpu.VMEM((tm, d_model), F32)],
        compiler_params=_params(2),
    )(x, prm, w1r, w2g)


def _ffn_bwd(dxo, y, x_in, prm, act, w1r, w2g, l, sub, tm):
    t_len, d_model = x_in.shape
    f8, r8 = w1r.shape[-1], w2g.shape[-2]
    n_m = t_len // tm

    def body(dxo_ref, y_ref, x_ref, prm_ref, act_ref, w1_ref, w2_ref,
             dx_ref, da_ref, dy_ref, st_ref, dy_sc, acc):
        m, j = pl.program_id(0), pl.program_id(1)

        @pl.when((m == 0) & (j == 0))
        def _():
            st_ref[...] = jnp.zeros_like(st_ref)

        @pl.when(j == 0)
        def _():
            dxo_v = dxo_ref[...]
            dy = ((0.5 * prm_ref[3:4, :]) * dxo_v).astype(BF16)
            dy_sc[...] = dy
            dy_ref[...] = dy
            st_ref[3:4, :] += jnp.sum(0.5 * dxo_v * y_ref[...].astype(F32), axis=0, keepdims=True)
            acc[...] = jnp.zeros_like(acc)

        w2 = w2_ref[...].reshape(2 * r8, d_model)
        parts = [slice(i, i + FFN_ROW_PART) for i in range(0, tm, FFN_ROW_PART)]
        ds = _dot_nt(dy_sc[parts[0], :], w2)
        for i, rows in enumerate(parts):
            ds_now = ds
            if i + 1 < len(parts):
                ds = _dot_nt(dy_sc[parts[i + 1], :], w2)
            dg = (ds_now * act_ref[0, rows, :].astype(F32)).astype(BF16)
            du = (ds_now * act_ref[1, rows, :].astype(F32)).astype(BF16)
            da_ref[0, rows, :] = dg
            da_ref[1, rows, :] = du
            acc[rows, :] += _dot_nt(dg, w1_ref[0]) + _dot_nt(du, w1_ref[1])

        @pl.when(j == N_CHUNK - 1)
        def _():
            dx, dshift, dscale, dng = _ada_bwd(acc[...], x_ref[...], prm_ref)
            dx_ref[...] = dxo_ref[...] + dx
            st_ref[0:1, :] += dshift
            st_ref[1:2, :] += dscale
            st_ref[2:3, :] += dng

    tile = pl.BlockSpec((tm, d_model), lambda m, j: (m, 0))
    fixed = pl.BlockSpec((8, d_model), lambda m, j: (0, 0))
    return pl.pallas_call(
        body, name=f"ffn_bwd_{l}_{sub}", grid=(n_m, N_CHUNK),
        in_specs=[tile, tile, tile, fixed,
                  pl.BlockSpec((2, None, tm, f8), lambda m, j: (0, j, m, 0)),
                  pl.BlockSpec((2, None, d_model, f8), lambda m, j: (0, j, 0, 0)),
                  pl.BlockSpec((2, r8, d_model), lambda m, j: (j, 0, 0))],
        out_specs=[tile, pl.BlockSpec((2, None, tm, f8), lambda m, j: (0, j, m, 0)), tile, fixed],
        out_shape=[SDS((t_len, d_model), F32), SDS((2, N_CHUNK, t_len, f8), BF16),
                   SDS((t_len, d_model), BF16), SDS((8, d_model), F32)],
        scratch_shapes=[pltpu.VMEM((tm, d_model), BF16), pltpu.VMEM((tm, d_model), F32)],
        compiler_params=_params(2),
    )(dxo, y, x_in, prm, act, w1r, w2g)


def _matmul_tn(a, b, a_spec, b_spec, out_shape, out_spec, n_piece, t_len, tk, name, after):
    def body(a_ref, b_ref, after_ref, o_ref, acc):
        k = pl.program_id(1)

        @pl.when(k == 0)
        def _():
            acc[...] = jnp.zeros_like(acc)

        m_out, n_out = acc.shape
        if (m_out // 2) % LANES == 0:
            for rows in _row_halves(m_out):
                acc[rows, :] += _dot_tn(a_ref[:, rows], b_ref[...])
        else:
            for cols in _row_halves(n_out):
                acc[:, cols] += _dot_tn(a_ref[...], b_ref[:, cols])

        @pl.when(k == pl.num_programs(1) - 1)
        def _():
            if len(o_ref.shape) == 3:
                cols = o_ref.shape[2]
                for q in range(o_ref.shape[0]):
                    o_ref[q] = acc[:, q * cols:(q + 1) * cols].astype(o_ref.dtype)
            else:
                o_ref[...] = acc[...].astype(o_ref.dtype)

    blk = tuple(d for d in out_spec.block_shape if d is not None)
    acc_shape = blk if len(blk) == 2 else (blk[1], blk[0] * blk[2])
    return pl.pallas_call(
        body, name=name, grid=(n_piece, t_len // tk),
        in_specs=[a_spec, b_spec, ANY], out_specs=out_spec, out_shape=out_shape,
        scratch_shapes=[pltpu.VMEM(acc_shape, F32)],
        compiler_params=_params(2),
    )(a, b, after)


def _groups(tm, width):
    return pltpu.VMEM((width // LANES, tm, LANES), F32)


def _to_groups(val, dst_ref, first=0):
    for g in range(val.shape[1] // LANES):
        dst_ref[first + g] = val[:, g * LANES:(g + 1) * LANES]


def _from_groups(src_ref):
    return jnp.concatenate([src_ref[g] for g in range(src_ref.shape[0])], axis=1)


def _split_residues(src_ref, out_ref, dil, tm):
    for g in range(src_ref.shape[0]):
        for r in range(dil):
            out_ref[r, :, g * LANES:(g + 1) * LANES] = (
                src_ref.at[g][pl.ds(r, tm // dil, stride=dil), :].astype(out_ref.dtype))


def _merge_residues(in_ref, dst_ref, dil, tm, add=False):
    for g in range(dst_ref.shape[0]):
        for r in range(dil):
            rows = pl.ds(r, tm // dil, stride=dil)
            val = in_ref[r, :, g * LANES:(g + 1) * LANES].astype(F32)
            dst = dst_ref.at[g]
            dst[rows, :] = (dst[rows, :] + val) if add else val


def _res_spec(dil, tm, width, index=lambda m: m):
    return pl.BlockSpec((dil, tm // dil, width), lambda m: (0, index(m), 0))


def _mixer_in(x, prm, wing, qkg, cprm, bd, l, tm):
    t_len, d_model = x.shape
    a_w = d_model // 2
    c_w = d_model - a_w
    c8 = wing.shape[-1]
    p_c = N_DEV * c8
    dils = DILATIONS[1:]

    def body(x_ref, prm_ref, w_ref, qkg_ref, cprm_ref, bd_ref, proj_ref, qkn_ref, h_ref, ycv_ref, *rest):
        res_refs, (pf, qsc, vsc, carry, wcat, wsems) = rest[:2 * len(dils)], rest[2 * len(dils):]

        @pl.when(pl.program_id(0) == 0)
        def _():
            carry[...] = jnp.zeros_like(carry)
            _join_columns(w_ref, wcat, wsems)

        h = _ada_fwd(x_ref[...], prm_ref).astype(BF16)
        h_ref[...] = h
        for cols in _row_halves(p_c):
            pf[:, cols] = _dot(h, wcat[:, cols])
        proj_ref[...] = pf[...].astype(BF16)
        bdm = bd_ref[...]
        for i in range(2):
            t = pf[:, i * a_w:(i + 1) * a_w]
            rstd = lax.rsqrt(_head_mean(t * t, bdm) + EPS)
            tn = t * rstd * qkg_ref[i:i + 1, :]
            qkn_ref[:, i * a_w:(i + 1) * a_w] = tn.astype(BF16)
            _to_groups(tn, qsc, i * (a_w // LANES))
        _to_groups(pf[:, 2 * a_w:3 * a_w], vsc)
        for i, dil in enumerate(dils):
            _split_residues(qsc, res_refs[2 * i], dil, tm)
            _split_residues(vsc, res_refs[2 * i + 1], dil, tm)
        gb = pf[:, 3 * a_w:3 * a_w + c_w]
        z = pf[:, 3 * a_w + c_w:3 * a_w + 2 * c_w] * pf[:, 3 * a_w + 2 * c_w:3 * a_w + 3 * c_w]
        z1, z2 = _shift_rows(z, carry[...], tm)
        cv = cprm_ref[0:1, :] * z2 + cprm_ref[1:2, :] * z1 + cprm_ref[2:3, :] * z + cprm_ref[3:4, :]
        ycv_ref[...] = (gb * cv).astype(BF16)
        carry[...] = z[tm - 8:tm, :]

    res_specs, res_shapes = [], []
    for dil in dils:
        res_specs += [_res_spec(dil, tm, 2 * a_w), _res_spec(dil, tm, a_w)]
        res_shapes += [SDS((dil, t_len // dil, 2 * a_w), BF16), SDS((dil, t_len // dil, a_w), BF16)]
    return pl.pallas_call(
        body, name=f"mixer_in_{l}", grid=(t_len // tm,),
        in_specs=[pl.BlockSpec((tm, d_model), lambda m: (m, 0)),
                  pl.BlockSpec((8, d_model), lambda m: (0, 0)),
                  ANY,
                  pl.BlockSpec((8, a_w), lambda m: (0, 0)),
                  pl.BlockSpec((8, c_w), lambda m: (0, 0)),
                  pl.BlockSpec(bd.shape, lambda m: (0, 0))],
        out_specs=[pl.BlockSpec((tm, p_c), lambda m: (m, 0)),
                   pl.BlockSpec((tm, 2 * a_w), lambda m: (m, 0)),
                   pl.BlockSpec((tm, d_model), lambda m: (m, 0)),
                   pl.BlockSpec((tm, c_w), lambda m: (m, 0))] + res_specs,
        out_shape=[SDS((t_len, p_c), BF16), SDS((t_len, 2 * a_w), BF16),
                   SDS((t_len, d_model), BF16), SDS((t_len, c_w), BF16)] + res_shapes,
        scratch_shapes=[pltpu.VMEM((tm, p_c), F32), _groups(tm, 2 * a_w), _groups(tm, a_w),
                        pltpu.VMEM((8, c_w), F32), pltpu.VMEM((d_model, p_c), BF16),
                        pltpu.SemaphoreType.DMA((N_DEV,))],
        compiler_params=_params(1),
    )(x, prm, wing, qkg, cprm, bd)


def _band_mask(first_block):
    qi = lax.broadcasted_iota(jnp.int32, (BLOCK, 2 * BLOCK), 0)
    kj = lax.broadcasted_iota(jnp.int32, (BLOCK, 2 * BLOCK), 1)
    lowest = jnp.where(first_block, BLOCK, 0)
    return (kj >= qi) & (kj <= qi + BLOCK) & (kj >= lowest)


def _blocks_per_step(seq):
    n_sb = min(ATTN_BLOCKS_PER_STEP, seq // BLOCK)
    assert seq % (n_sb * BLOCK) == 0
    return n_sb


def _block_rows(sb):
    return slice(sb * BLOCK, (sb + 1) * BLOCK)


def _with_previous(prev_ref, cur_ref, sb, sl):
    before = prev_ref[:, sl] if sb == 0 else cur_ref[_block_rows(sb - 1), sl]
    return jnp.concatenate([before, cur_ref[_block_rows(sb), sl]], axis=0)


def _heads_on_lanes(cols):
    width = LANES // len(cols)
    lane = lax.broadcasted_iota(jnp.int32, (BLOCK, LANES), 1)
    out = jnp.broadcast_to(cols[-1], (BLOCK, LANES))
    for h in range(len(cols) - 2, -1, -1):
        out = jnp.where(lane < (h + 1) * width, cols[h], out)
    return out


def _attn_fwd(qk, v, v_col, l):
    dil, seq, a2 = qk.shape
    a_w = a2 // 2
    n_sb = _blocks_per_step(seq)
    rows_step = n_sb * BLOCK
    n_pair = a_w // LANES

    def body(q_ref, kc_ref, vc_ref, o_ref, lse_ref, kp_ref, vp_ref):
        @pl.when(pl.program_id(1) == 0)
        def _():
            kp_ref[...] = jnp.zeros_like(kp_ref)
            vp_ref[...] = jnp.zeros_like(vp_ref)

        masks = [_band_mask(pl.program_id(1) == 0)] + [_band_mask(False)] * (n_sb - 1)
        low = lax.broadcasted_iota(jnp.int32, (BLOCK, LANES), 1) < HEAD_DIM
        units = [(sb, slice(p * LANES, (p + 1) * LANES)) for sb in range(n_sb) for p in range(a_w // LANES)]
        scores = []
        for sb, sl in units:
            qp = q_ref[_block_rows(sb), sl]
            k2 = _with_previous(kp_ref, kc_ref, sb, sl)
            scores.append([_dot_nt(jnp.where(half, qp, jnp.zeros_like(qp)), k2) for half in (low, ~low)])
        probs, lses = [], []
        for (sb, sl), pair in zip(units, scores):
            pr, ls = [], []
            for s in pair:
                s = jnp.where(masks[sb], s * SM_SCALE, NEG)
                mx = jnp.max(s, axis=-1, keepdims=True)
                e = jnp.exp(s - mx)
                den = jnp.sum(e, axis=-1, keepdims=True)
                pr.append((e * (1.0 / den)).astype(BF16))
                ls.append(mx + jnp.log(den))
            probs.append(pr)
            lses.append(ls)
        for (sb, sl), pr, ls in zip(units, probs, lses):
            v2 = _with_previous(vp_ref, vc_ref, sb, sl)
            o_ref[_block_rows(sb), sl] = jnp.where(low, _dot(pr[0], v2), _dot(pr[1], v2)).astype(BF16)
        for sb in range(n_sb):
            cols = [c for ls in lses[sb * n_pair:(sb + 1) * n_pair] for c in ls]
            lse_ref[_block_rows(sb), :] = _heads_on_lanes(cols)
        kp_ref[...] = kc_ref[_block_rows(n_sb - 1), :]
        vp_ref[...] = vc_ref[_block_rows(n_sb - 1), :]

    def blk(f):
        return pl.BlockSpec((None, rows_step, a_w), f)

    return pl.pallas_call(
        body, name=f"attn_fwd_{l}_d{dil}", grid=(dil, seq // rows_step),
        in_specs=[blk(lambda r, b: (r, b, 0)),
                  blk(lambda r, b: (r, b, 1)),
                  blk(lambda r, b: (r, b, v_col))],
        out_specs=[blk(lambda r, b: (r, b, 0)), pl.BlockSpec((None, rows_step, LANES), lambda r, b: (r, b, 0))],
        out_shape=[SDS((dil, seq, a_w), BF16), SDS((dil, seq, LANES), F32)],
        scratch_shapes=[pltpu.VMEM((BLOCK, a_w), BF16), pltpu.VMEM((BLOCK, a_w), BF16)],
        compiler_params=_params(2),
    )(qk, qk, v)


def _attn_combine(outs, lses, ycv, expand, tm, l):
    t_len, c_w = ycv.shape
    a_w = outs[0].shape[-1]
    dils = DILATIONS[1:]
    n_r = len(dils)

    def body(o1, l1, *rest):
        o_res, l_res, y_ref, ex_ref = rest[:n_r], rest[n_r:2 * n_r], rest[2 * n_r], rest[2 * n_r + 1]
        cat_ref, lse_ref = rest[2 * n_r + 2], rest[2 * n_r + 3]
        ya_res, lse_res = rest[2 * n_r + 4:3 * n_r + 4], rest[3 * n_r + 4:4 * n_r + 4]
        scr = rest[4 * n_r + 4:]
        so, sl, sya, slse = scr[:n_r], scr[n_r:2 * n_r], scr[2 * n_r], scr[2 * n_r + 1]
        for i, dil in enumerate(dils):
            _merge_residues(o_res[i], so[i], dil, tm)
            _merge_residues(l_res[i], sl[i], dil, tm)
        ls = [l1[...]] + [s[0] for s in sl]
        mx = jnp.maximum(jnp.maximum(ls[0], ls[1]), ls[2])
        es = [jnp.exp(t - mx) for t in ls]
        den = es[0] + es[1] + es[2]
        inv = 1.0 / den
        slse[0] = mx + jnp.log(den)
        lse_ref[...] = slse[0]
        ex = ex_ref[...]
        ov = [o1[...].astype(F32)] + [_from_groups(s) for s in so]
        ya = jnp.zeros((tm, a_w), F32)
        for e, o in zip(es, ov):
            hi, lo = _split_bf16(e * inv)
            ya = ya + (_dot(hi, ex) + _dot(lo, ex)) * o
        _to_groups(ya, sya)
        cat_ref[:, :a_w] = ya.astype(BF16)
        cat_ref[:, a_w:] = y_ref[...]
        for i, dil in enumerate(dils):
            _split_residues(sya, ya_res[i], dil, tm)
            _split_residues(slse, lse_res[i], dil, tm)

    ta = pl.BlockSpec((tm, a_w), lambda m: (m, 0))
    tl = pl.BlockSpec((tm, LANES), lambda m: (m, 0))
    res_a = [_res_spec(dil, tm, a_w) for dil in dils]
    res_l = [_res_spec(dil, tm, LANES) for dil in dils]
    return pl.pallas_call(
        body, name=f"attn_combine_{l}", grid=(t_len // tm,),
        in_specs=[ta, tl] + res_a + res_l + [pl.BlockSpec((tm, c_w), lambda m: (m, 0)),
                                            pl.BlockSpec((LANES, a_w), lambda m: (0, 0))],
        out_specs=[pl.BlockSpec((tm, a_w + c_w), lambda m: (m, 0)), tl] + res_a + res_l,
        out_shape=[SDS((t_len, a_w + c_w), BF16), SDS((t_len, LANES), F32)]
        + [SDS((dil, t_len // dil, a_w), BF16) for dil in dils]
        + [SDS((dil, t_len // dil, LANES), F32) for dil in dils],
        scratch_shapes=[_groups(tm, a_w)] * n_r + [_groups(tm, LANES)] * n_r + [_groups(tm, a_w), _groups(tm, LANES)],
        compiler_params=_params(1),
    )(outs[0].reshape(t_len, a_w), lses[0].reshape(t_len, LANES), *outs[1:], *lses[1:], ycv, expand)


def _mixer_out(cat, x, prm, woutg, l, tm):
    t_len, d_model = x.shape
    r8 = woutg.shape[-2]

    def body(cat_ref, x_ref, prm_ref, w_ref, xo_ref, y_ref):
        w = w_ref[...].reshape(N_DEV * r8, d_model)
        for rows in _row_halves(tm):
            yv = _dot(cat_ref[rows, :], w)
            y_ref[rows, :] = yv.astype(BF16)
            xo_ref[rows, :] = x_ref[rows, :] + prm_ref[3:4, :] * yv

    tile = pl.BlockSpec((tm, d_model), lambda m: (m, 0))
    return pl.pallas_call(
        body, name=f"mixer_out_{l}", grid=(t_len // tm,),
        in_specs=[tile, tile, pl.BlockSpec((8, d_model), lambda m: (0, 0)),
                  pl.BlockSpec((N_DEV, r8, d_model), lambda m: (0, 0, 0))],
        out_specs=[tile, tile],
        out_shape=[SDS((t_len, d_model), F32), SDS((t_len, d_model), BF16)],
        compiler_params=_params(1),
    )(cat, x, prm, woutg)


def _mixer_bwd_out(dxo, ymix, prm, woutg, proj, cprm, l, tm):
    t_len, d_model = dxo.shape
    a_w = d_model // 2
    c_w = d_model - a_w
    r8 = woutg.shape[-2]
    n_m = t_len // tm
    hb = tm // 8

    dils = DILATIONS[1:]

    def body(dxo_ref, ym_ref, prm_ref, w_ref, pc_ref, halo_ref, cprm_ref,
             dya_ref, dpc_ref, dy_ref, st_ref, cs_ref, *rest):
        dya_res, (carry, dsc) = rest[:len(dils)], rest[len(dils):]
        step = pl.program_id(0)
        tile_i = n_m - 1 - step

        @pl.when(step == 0)
        def _():
            st_ref[...] = jnp.zeros_like(st_ref)
            cs_ref[...] = jnp.zeros_like(cs_ref)
            carry[...] = jnp.zeros_like(carry)

        dxo_v = dxo_ref[...]
        dy = (prm_ref[3:4, :] * dxo_v).astype(BF16)
        dy_ref[...] = dy
        st_ref[3:4, :] += jnp.sum(dxo_v * ym_ref[...].astype(F32), axis=0, keepdims=True)
        w = w_ref[...].reshape(N_DEV * r8, d_model)
        dcat = jnp.concatenate([_dot_nt(dy[rows, :], w) for rows in _row_halves(tm)], axis=0)
        _to_groups(dcat[:, :a_w], dsc)
        dya_ref[...] = dcat[:, :a_w].astype(BF16)
        for i, dil in enumerate(dils):
            _split_residues(dsc, dya_res[i], dil, tm)
        dyc = dcat[:, a_w:]
        gb = pc_ref[:, :c_w].astype(F32)
        gc = pc_ref[:, c_w:2 * c_w].astype(F32)
        u = pc_ref[:, 2 * c_w:].astype(F32)
        z = gc * u
        before = halo_ref[:, c_w:2 * c_w].astype(F32) * halo_ref[:, 2 * c_w:].astype(F32)
        before = jnp.where(tile_i > 0, before, jnp.zeros_like(before))
        z1, z2 = _shift_rows(z, before, tm)
        w0, w1, w2 = cprm_ref[0:1, :], cprm_ref[1:2, :], cprm_ref[2:3, :]
        cv = w0 * z2 + w1 * z1 + w2 * z + cprm_ref[3:4, :]
        dcv = dyc * gb
        cs_ref[0:1, :] += jnp.sum(dcv * z2, axis=0, keepdims=True)
        cs_ref[1:2, :] += jnp.sum(dcv * z1, axis=0, keepdims=True)
        cs_ref[2:3, :] += jnp.sum(dcv * z, axis=0, keepdims=True)
        cs_ref[3:4, :] += jnp.sum(dcv, axis=0, keepdims=True)
        row = lax.broadcasted_iota(jnp.int32, dcv.shape, 0)
        after = carry[...]
        d1 = jnp.where(row == tm - 1, after[0:1, :], pltpu.roll(dcv, tm - 1, 0))
        d2 = jnp.where(row == tm - 2, after[0:1, :],
                       jnp.where(row == tm - 1, after[1:2, :], pltpu.roll(dcv, tm - 2, 0)))
        dz = w2 * dcv + w1 * d1 + w0 * d2
        dpc_ref[:, :c_w] = (dyc * cv).astype(BF16)
        dpc_ref[:, c_w:2 * c_w] = (dz * u).astype(BF16)
        dpc_ref[:, 2 * c_w:] = (dz * gc).astype(BF16)
        carry[...] = dcv[0:8, :]

    def rev(width, col=0):
        return pl.BlockSpec((tm, width), lambda s: (n_m - 1 - s, col))

    fixed_d = pl.BlockSpec((8, d_model), lambda s: (0, 0))
    fixed_c = pl.BlockSpec((8, c_w), lambda s: (0, 0))
    return pl.pallas_call(
        body, name=f"mixer_bwd_out_{l}", grid=(n_m,),
        in_specs=[rev(d_model), rev(d_model), fixed_d,
                  pl.BlockSpec((N_DEV, r8, d_model), lambda s: (0, 0, 0)),
                  rev(3 * c_w, 1),
                  pl.BlockSpec((8, 3 * c_w), lambda s: (jnp.maximum((n_m - 1 - s) * hb - 1, 0), 1)),
                  fixed_c],
        out_specs=[rev(a_w), rev(3 * c_w), rev(d_model), fixed_d, fixed_c]
        + [_res_spec(dil, tm, a_w, lambda s: n_m - 1 - s) for dil in dils],
        out_shape=[SDS((t_len, a_w), BF16), SDS((t_len, 3 * c_w), BF16), SDS((t_len, d_model), BF16),
                   SDS((8, d_model), F32), SDS((8, c_w), F32)]
        + [SDS((dil, t_len // dil, a_w), BF16) for dil in dils],
        scratch_shapes=[pltpu.VMEM((8, c_w), F32), _groups(tm, a_w)],
        compiler_params=_params(1),
    )(dxo, ymix, prm, woutg, proj, proj, cprm)


def _attn_bwd(qk, v, v_col, dya, ya, lse, l):
    dil, seq, a2 = qk.shape
    a_w = a2 // 2
    n_sb = _blocks_per_step(seq)
    rows_step = n_sb * BLOCK
    n_step = seq // rows_step
    last = _block_rows(n_sb - 1)
    lse_w = LANES // (a_w // HEAD_DIM)
    single = n_step == 1

    def body(q_ref, kc_ref, vc_ref, do_ref, ya_ref, lse_ref,
             dq_ref, dk_ref, dv_ref, hold_k, hold_v, acc_k, acc_v, kp_ref, vp_ref):
        j = pl.program_id(1)

        @pl.when(j == 0)
        def _():
            kp_ref[...] = jnp.zeros_like(kp_ref)
            vp_ref[...] = jnp.zeros_like(vp_ref)

        @pl.when((pl.program_id(0) == 0) & (j == 0))
        def _():
            hold_k[...] = jnp.zeros_like(hold_k)
            hold_v[...] = jnp.zeros_like(hold_v)

        @pl.when(j < n_step)
        def _():
            masks = [_band_mask(j == 0)] + [_band_mask(False)] * (n_sb - 1)
            low = lax.broadcasted_iota(jnp.int32, (BLOCK, LANES), 1) < HEAD_DIM
            units = [(sb, slice(p * LANES, (p + 1) * LANES)) for sb in range(n_sb) for p in range(a_w // LANES)]
            acc_k[...] = jnp.zeros_like(acc_k)
            acc_v[...] = jnp.zeros_like(acc_v)
            heads, scores, dps = [], [], []
            for sb, sl in units:
                qp = q_ref[_block_rows(sb), sl]
                k2 = _with_previous(kp_ref, kc_ref, sb, sl)
                v2 = _with_previous(vp_ref, vc_ref, sb, sl)
                dop = do_ref[_block_rows(sb), sl]
                prod = dop.astype(F32) * ya_ref[_block_rows(sb), sl].astype(F32)
                lsep = lse_ref[_block_rows(sb), :]
                first = 2 * (sl.start // LANES) * lse_w
                for hh, half in enumerate((low, ~low)):
                    qh = jnp.where(half, qp, jnp.zeros_like(qp))
                    doh = jnp.where(half, dop, jnp.zeros_like(dop))
                    dsum = jnp.sum(jnp.where(half, prod, 0.0), axis=-1, keepdims=True)
                    heads.append((qh, doh, dsum, lsep[:, first + hh * lse_w:first + hh * lse_w + 1], masks[sb]))
                    scores.append(_dot_nt(qh, k2))
                    dps.append(_dot_nt(doh, v2))
            dss, prs = [], []
            for (qh, doh, dsum, lse_h, valid), s, dp in zip(heads, scores, dps):
                pr = jnp.exp(jnp.where(valid, s * SM_SCALE, NEG) - lse_h)
                dss.append((pr * (dp - dsum) * SM_SCALE).astype(BF16))
                prs.append(pr.astype(BF16))
            for i, (sb, sl) in enumerate(units):
                k2 = _with_previous(kp_ref, kc_ref, sb, sl)
                h0, h1 = heads[2 * i], heads[2 * i + 1]
                dq_ref[_block_rows(sb), sl] = jnp.where(low, _dot(dss[2 * i], k2),
                                                        _dot(dss[2 * i + 1], k2)).astype(BF16)
                two = slice(sb * BLOCK, (sb + 2) * BLOCK)
                acc_k[two, sl] += _dot_tn(dss[2 * i], h0[0]) + _dot_tn(dss[2 * i + 1], h1[0])
                acc_v[two, sl] += _dot_tn(prs[2 * i], h0[1]) + _dot_tn(prs[2 * i + 1], h1[1])
            for out_ref, hold, acc in ((dk_ref, hold_k, acc_k), (dv_ref, hold_v, acc_v)):
                if single:
                    out_ref[...] = acc[BLOCK:, :].astype(BF16)
                    continue
                if n_sb > 1:
                    out_ref[:rows_step - BLOCK, :] = hold[:rows_step - BLOCK, :].astype(BF16)
                out_ref[last, :] = (hold[last, :] + acc[:BLOCK, :]).astype(BF16)
                hold[...] = acc[BLOCK:, :]
            kp_ref[...] = kc_ref[last, :]
            vp_ref[...] = vc_ref[last, :]

        @pl.when(j == n_step)
        def _():
            dk_ref[...] = hold_k[...].astype(BF16)
            dv_ref[...] = hold_v[...].astype(BF16)

    def blk(f):
        return pl.BlockSpec((None, rows_step, a_w), f)

    def cur(j):
        return jnp.minimum(j, n_step - 1)

    def late(j):
        return jnp.maximum(j - 1, 0)

    return pl.pallas_call(
        body, name=f"attn_bwd_{l}_d{dil}", grid=(dil, 1 if single else n_step + 1),
        in_specs=[blk(lambda r, j: (r, cur(j), 0)),
                  blk(lambda r, j: (r, cur(j), 1)),
                  blk(lambda r, j: (r, cur(j), v_col)),
                  blk(lambda r, j: (r, cur(j), 0)),
                  blk(lambda r, j: (r, cur(j), 0)),
                  pl.BlockSpec((None, rows_step, LANES), lambda r, j: (r, cur(j), 0))],
        out_specs=[blk(lambda r, j: (r, cur(j), 0)), blk(lambda r, j: (r, late(j), 0)),
                   blk(lambda r, j: (r, late(j), 0))],
        out_shape=[SDS((dil, seq, a_w), BF16)] * 3,
        scratch_shapes=[pltpu.VMEM((rows_step, a_w), F32), pltpu.VMEM((rows_step, a_w), F32),
                        pltpu.VMEM((rows_step + BLOCK, a_w), F32), pltpu.VMEM((rows_step + BLOCK, a_w), F32),
                        pltpu.VMEM((BLOCK, a_w), BF16), pltpu.VMEM((BLOCK, a_w), BF16)],
        compiler_params=_params(2),
    )(qk, qk, v, dya, ya, lse)


def _mixer_bwd_in(parts, proj, dpc, dxo, x_in, prm, wing, qkg, bd, l, tm):
    t_len, d_model = x_in.shape
    a_w = d_model // 2
    c_w = d_model - a_w
    c8 = wing.shape[-1]
    p_c = N_DEV * c8

    dils = DILATIONS[1:]
    n_in = 3 * len(DILATIONS)

    def body(*refs):
        nat, res = refs[:3], refs[3:n_in]
        (pa_ref, dpc_ref, dxo_ref, x_ref, prm_ref, w_ref, qkg_ref, bd_ref,
         dx_ref, dproj_ref, st_ref, qs_ref) = refs[n_in:n_in + 12]
        sums, wcat, wsems = refs[n_in + 12:n_in + 15], refs[n_in + 15], refs[n_in + 16]

        @pl.when(pl.program_id(0) == 0)
        def _():
            st_ref[...] = jnp.zeros_like(st_ref)
            qs_ref[...] = jnp.zeros_like(qs_ref)
            _join_columns(w_ref, wcat, wsems)

        for t in range(3):
            _to_groups(nat[t][...].astype(F32), sums[t])
            for i, dil in enumerate(dils):
                _merge_residues(res[3 * i + t], sums[t], dil, tm, add=True)
        bdm = bd_ref[...]
        grads = [_from_groups(sums[0]), _from_groups(sums[1])]
        for i in range(2):
            t = pa_ref[:, i * a_w:(i + 1) * a_w].astype(F32)
            rstd = lax.rsqrt(_head_mean(t * t, bdm) + EPS)
            that = t * rstd
            qs_ref[i:i + 1, :] += jnp.sum(grads[i] * that, axis=0, keepdims=True)
            dhat = grads[i] * qkg_ref[i:i + 1, :]
            dt = rstd * (dhat - that * _head_mean(dhat * that, bdm))
            dproj_ref[:, i * a_w:(i + 1) * a_w] = dt.astype(BF16)
        dproj_ref[:, 2 * a_w:3 * a_w] = _from_groups(sums[2]).astype(BF16)
        dproj_ref[:, 3 * a_w:] = dpc_ref[...]
        dh = jnp.concatenate([_dot_nt(dproj_ref[rows, :], wcat[...]) for rows in _row_halves(tm)], axis=0)
        dx, dshift, dscale, dng = _ada_bwd(dh, x_ref[...], prm_ref)
        dx_ref[...] = dxo_ref[...] + dx
        st_ref[0:1, :] += dshift
        st_ref[1:2, :] += dscale
        st_ref[2:3, :] += dng

    ta = pl.BlockSpec((tm, a_w), lambda m: (m, 0))
    tile = pl.BlockSpec((tm, d_model), lambda m: (m, 0))
    return pl.pallas_call(
        body, name=f"mixer_bwd_in_{l}", grid=(t_len // tm,),
        in_specs=[ta] * 3 + [_res_spec(dil, tm, a_w) for dil in dils for _ in range(3)]
                 + [pl.BlockSpec((tm, 3 * a_w), lambda m: (m, 0)),
                             pl.BlockSpec((tm, 3 * c_w), lambda m: (m, 0)),
                             tile, tile, pl.BlockSpec((8, d_model), lambda m: (0, 0)),
                             ANY,
                             pl.BlockSpec((8, a_w), lambda m: (0, 0)),
                             pl.BlockSpec(bd.shape, lambda m: (0, 0))],
        out_specs=[tile, pl.BlockSpec((tm, p_c), lambda m: (m, 0)),
                   pl.BlockSpec((8, d_model), lambda m: (0, 0)), pl.BlockSpec((8, a_w), lambda m: (0, 0))],
        out_shape=[SDS((t_len, d_model), F32), SDS((t_len, p_c), BF16), SDS((8, d_model), F32), SDS((8, a_w), F32)],
        scratch_shapes=[_groups(tm, a_w)] * 3 + [pltpu.VMEM((d_model, p_c), BF16), pltpu.SemaphoreType.DMA((N_DEV,))],
        compiler_params=_params(1),
    )(*[t.reshape(t_len, a_w) for t in parts[0]], *[t for p in parts[1:] for t in p],
      proj, dpc, dxo, x_in, prm, wing, qkg, bd)


def _loss_head(x, target, tm):
    t_len, d_model = x.shape

    def body(x_ref, t_ref, dx_ref, loss_ref):
        @pl.when(pl.program_id(0) == 0)
        def _():
            loss_ref[...] = jnp.zeros_like(loss_ref)

        diff = x_ref[...] - t_ref[...]
        dx_ref[...] = diff * (1.0 / d_model)
        per_token = jnp.sum(diff * diff, axis=-1, keepdims=True) * (1.0 / d_model)
        loss_ref[...] += 0.5 * jnp.sum(per_token)

    tile = pl.BlockSpec((tm, d_model), lambda m: (m, 0))
    return pl.pallas_call(
        body, name="loss_head", grid=(t_len // tm,),
        in_specs=[tile, tile],
        out_specs=[tile, pl.BlockSpec((8, LANES), lambda m: (0, 0))],
        out_shape=[SDS((t_len, d_model), F32), SDS((8, LANES), F32)],
        compiler_params=_params(1),
    )(x, target)


def _row_tile(rows, cols):
    best = None
    for t in range(16, rows + 1, 16):
        if rows % t == 0 and t * cols * 4 <= (1 << 20):
            best = t
    return best if best is not None else rows


def _adamw(pieces, w, m, v, name):
    n_q = len(pieces)
    n_p, rows, cols = pieces[0].shape
    tr = _row_tile(rows, cols)
    n_i = rows // tr
    c1 = 1.0 - ADAM_B1 ** ADAM_STEP
    c2 = 1.0 - ADAM_B2 ** ADAM_STEP

    def body(*refs):
        p_refs = refs[:n_q]
        w_ref, m_ref, v_ref, g_ref, d_ref, nm_ref, nv_ref = refs[n_q:]
        for q in range(n_q):
            @pl.when(pl.program_id(0) == q)
            def _():
                g = p_refs[q][0].astype(F32)
                for i in range(1, n_p):
                    g = g + p_refs[q][i].astype(F32)
                g_ref[...] = g
                nm = ADAM_B1 * m_ref[...] + (1.0 - ADAM_B1) * g
                nv = ADAM_B2 * v_ref[...] + (1.0 - ADAM_B2) * (g * g)
                nm_ref[...] = nm
                nv_ref[...] = nv
                d_ref[...] = -ADAM_LR * ((nm / c1) / (jnp.sqrt(nv / c2) + ADAM_EPS) + ADAM_WD * w_ref[...])

    def piece_spec(q):
        return pl.BlockSpec((n_p, tr, cols), lambda a, i: (0, jnp.where(a == q, i, 0), 0))

    tile = pl.BlockSpec((tr, cols), lambda a, i: (a * n_i + i, 0))
    return pl.pallas_call(
        body, name=name, grid=(n_q, n_i),
        in_specs=[piece_spec(q) for q in range(n_q)] + [tile, tile, tile],
        out_specs=[tile] * 4,
        out_shape=[SDS((n_q * rows, cols), F32)] * 4,
        compiler_params=_params(2),
    )(*pieces, w, m, v)


def _pack(vecs):
    flat = jnp.concatenate([v.reshape(-1).astype(F32) for v in vecs])
    rows = -(-flat.shape[0] // (8 * LANES)) * 8
    return jnp.pad(flat, (0, rows * LANES - flat.shape[0])).reshape(rows, LANES)


def _unpack(packed, shapes):
    lead = packed.shape[:-2]
    flat = packed.reshape(lead + (-1,))
    out, off = [], 0
    for s in shapes:
        size = 1
        for d in s:
            size *= d
        out.append(flat[..., off:off + size].reshape(lead + tuple(s)))
        off += size
    return out


def kernel(x, c, w_ada, b_ada, norm_g, w_in, q_norm_g, k_norm_g, conv_w, conv_b, w_out, ffn_w1, ffn_w2, loss_target, m_w_ada, m_b_ada, m_norm_g, m_w_in, m_q_norm_g, m_k_norm_g, m_conv_w, m_conv_b, m_w_out, m_ffn_w1, m_ffn_w2, v_w_ada, v_b_ada, v_norm_g, v_w_in, v_q_norm_g, v_k_norm_g, v_conv_w, v_conv_b, v_w_out, v_ffn_w1, v_ffn_w2):
    _, t_len, d_model = x.shape
    n_layer = w_ada.shape[0]
    a_w = d_model // 2
    c_w = d_model - a_w
    n_head = a_w // HEAD_DIM
    ada_cols = w_ada.shape[-1]
    tm = min(TOKEN_TILE, t_len)
    tm_ffn = min(FFN_FWD_TOKEN_TILE, t_len)
    tm_ffn_bwd = min(FFN_BWD_TOKEN_TILE, t_len)
    me = _index(_my_place())
    x2 = x.reshape(t_len, d_model)
    target = loss_target.reshape(t_len, d_model)

    small_shapes = [(d_model,), norm_g.shape, conv_w.shape]
    gathered = _allgather_small(_pack([c, norm_g, conv_w]), "gather_small_inputs")
    c_all, ng_parts, cw_parts = _unpack(gathered, small_shapes)
    norm_g_full = jnp.moveaxis(ng_parts, 0, 2).reshape(n_layer, 3, d_model)
    conv_w_full = jnp.moveaxis(cw_parts, 0, 2).reshape(n_layer, 3, c_w)

    b_loc = lax.dynamic_slice_in_dim(b_ada, me * ada_cols, ada_cols, axis=1).reshape(n_layer, 1, ada_cols)
    mod_part, silu_c = _mod_part(c_all, w_ada, b_loc)
    mod_all = _allgather_small(_pack([mod_part]), "gather_mod")
    mod_all = _unpack(mod_all, [mod_part.shape])[0]
    mod_mine = lax.dynamic_index_in_dim(mod_all, me, axis=2, keepdims=False)
    mod = jnp.moveaxis(mod_mine, 0, 1).reshape(n_layer, 3, 3, d_model)

    def prm_of(l, sub):
        rows = jnp.stack([norm_g_full[l, sub], mod[l, sub, 0], mod[l, sub, 1], mod[l, sub, 2]])
        return jnp.pad(rows, ((0, 4), (0, 0)))

    prm = [[prm_of(l, sub) for sub in range(3)] for l in range(n_layer)]
    qkg = [jnp.pad(jnp.stack([jnp.tile(q_norm_g[l], n_head), jnp.tile(k_norm_g[l], n_head)]), ((0, 6), (0, 0)))
           for l in range(n_layer)]
    cprm = [jnp.pad(jnp.concatenate([conv_w_full[l], conv_b[l][None]]), ((0, 4), (0, 0))) for l in range(n_layer)]
    head_of = jnp.arange(min(MXU_WIDTH, a_w)) // HEAD_DIM
    bd = (head_of[:, None] == head_of[None, :]).astype(BF16)
    expand = (jnp.arange(LANES)[:, None] == (jnp.arange(a_w)[None, :] // HEAD_DIM) * (LANES // n_head)).astype(BF16)

    f8 = ffn_w1.shape[-1]
    r8 = ffn_w2.shape[-2]
    c8 = w_in.shape[-1]
    o8 = w_out.shape[-2]
    me_arr = jnp.reshape(me, (1,)).astype(jnp.int32)
    win_b = [_place_own(w_in, (l,), me_arr, f"own_w_in_{l}") for l in range(n_layer)]
    wout_b = [_place_own(w_out, (l,), me_arr, f"own_w_out_{l}") for l in range(n_layer)]
    w1_b = [[_place_own(ffn_w1, (l, s), me_arr, f"own_w1_{l}_{s}") for s in range(2)] for l in range(n_layer)]
    w2_b = [[_place_own(ffn_w2, (l, s), me_arr, f"own_w2_{l}_{s}") for s in range(2)] for l in range(n_layer)]
    w1_first, w2_first = _allgather_big([w1_b[0][0], w2_b[0][0]], "gather_first_weights")
    groups = [[win_b[0], wout_b[0]], [w1_b[0][1], w2_b[0][1]]]
    for l in range(1, n_layer):
        groups += [[w1_b[l][0], w2_b[l][0]], [win_b[l], wout_b[l]], [w1_b[l][1], w2_b[l][1]]]
    flat = [a for grp in groups for a in grp]
    w_sends, w_recvs, w_lands, w_token = _send_start(flat, None, w1_first, "weights_start")

    def gathered(gi, after):
        lo = sum(len(grp) for grp in groups[:gi])
        hi = lo + len(groups[gi])
        return _send_wait(w_lands[lo:hi], w_sends[lo:hi], w_recvs[lo:hi], after, f"weights_wait_{gi}")

    def chunked(w1g):
        return w1g.reshape(2, N_CHUNK, d_model, f8)

    saved = []
    xc = x2
    for l in range(n_layer):
        keep = {"x0": xc}
        if l == 0:
            w1g, w2g = w1_first, w2_first
            prm_first = prm[0][0] + w_token[0, 0]
        else:
            w1g, w2g = gathered(3 * l - 1, xc)
            prm_first = prm[l][0]
        keep["w_ffn0"] = (chunked(w1g), w2g)
        keep["act0"], keep["h0"], keep["y0"], xc = _ffn_fwd(xc, prm_first, *keep["w_ffn0"], l, 0, tm_ffn)
        keep["x1"] = xc
        wing, woutg = gathered(0 if l == 0 else 3 * l, xc)
        keep.update(wing=wing, woutg=woutg)
        proj, qkn, keep["h1"], ycv, *res = _mixer_in(xc, prm[l][1], wing, qkg[l], cprm[l], bd, l, tm)
        qkv = [(qkn[None], proj[None], 2)] + [(res[2 * i], res[2 * i + 1], 0) for i in range(len(DILATIONS) - 1)]
        branches = [_attn_fwd(*t, l) for t in qkv]
        cat, lse, *res = _attn_combine([b[0] for b in branches], [b[1] for b in branches], ycv, expand, tm, l)
        n_r = len(DILATIONS) - 1
        keep["ya"] = [cat[None]] + res[:n_r]
        keep["lse"] = [lse[None]] + res[n_r:]
        xc, keep["ymix"] = _mixer_out(cat, xc, prm[l][1], woutg, l, tm)
        keep.update(proj=proj, qkv=qkv, cat=cat)
        keep["x2"] = xc
        w1g, w2g = gathered(1 if l == 0 else 3 * l + 1, xc)
        keep["w_ffn2"] = (chunked(w1g), w2g)
        keep["act2"], keep["h2"], keep["y2"], xc = _ffn_fwd(xc, prm[l][2], *keep["w_ffn2"], l, 1, tm_ffn)
        saved.append(keep)

    dx, loss_blk = _loss_head(xc, target, tm)
    loss = lax.psum(loss_blk[0, 0], ("x", "y", "c"))

    tk = min(GRAD_TOKEN_TILE, t_len)
    stats =[[None] * 3 for _ in range(n_layer)]
    conv_stats = [None] * n_layer
    qk_stats = [None] * n_layer

    started = {}
    dummy = jnp.zeros((8, LANES), F32)

    def start_exchange(g, name):
        sends, recvs, thru, token = _send_start(
            [_place_own(g, None, me_arr, f"own_grad_{name}")], [g], dummy, f"grads_start_{name}")
        started[name] = (thru[0], thru[1], sends[0], recvs[0])
        return token

    def wait_exchange(names, after, name):
        ent = [started[k] for k in names]
        done = _send_wait([e[0] for e in ent] + [e[1] for e in ent], [e[2] for e in ent], [e[3] for e in ent],
                          after, name)
        return dict(zip(names, done))

    def ffn_backward(dxo, keep, l, sub, which, prm_dep):
        dx_in, da, dy, st = _ffn_bwd(dxo, keep[f"y{which}"], keep[f"x{which}"], prm_dep,
                                     keep[f"act{which}"], *keep[f"w_ffn{which}"], l, sub, tm_ffn_bwd)
        g_w2 = _matmul_tn(
            keep[f"act{which}"], dy,
            pl.BlockSpec((None, None, tk, f8), lambda p, k: (2, p, k, 0)),
            pl.BlockSpec((tk, d_model), lambda p, k: (k, 0)),
            SDS((N_CHUNK, f8, d_model), BF16), pl.BlockSpec((None, f8, d_model), lambda p, k: (p, 0, 0)),
            N_CHUNK, t_len, tk, f"grad_w2_{l}_{sub}", dummy)
        token = start_exchange(g_w2.reshape(N_DEV, r8, d_model), f"w2_{l}_{sub}")
        g_w1 = _matmul_tn(
            keep[f"h{which}"], da.reshape(N_DEV, t_len, f8),
            pl.BlockSpec((tk, d_model), lambda p, k: (k, 0)),
            pl.BlockSpec((None, tk, f8), lambda p, k: (p, k, 0)),
            SDS((N_DEV, d_model, f8), BF16), pl.BlockSpec((None, d_model, f8), lambda p, k: (p, 0, 0)),
            N_DEV, t_len, tk, f"grad_w1_{l}_{sub}", token)
        token = start_exchange(g_w1, f"w1_{l}_{sub}")
        return dx_in, st, token[0, 0]

    token = 0.0
    for l in reversed(range(n_layer)):
        keep = saved[l]
        wing, woutg = keep["wing"], keep["woutg"]
        dx, stats[l][2], token = ffn_backward(dx, keep, l, 1, 2, prm[l][2] + token)
        dya, dpc, dy, st_gate, conv_stats[l], *dya_res = _mixer_bwd_out(
            dx, keep["ymix"], prm[l][1] + token, woutg, keep["proj"], cprm[l], l, tm)
        dyas = [dya[None]] + dya_res
        g_wout = _matmul_tn(
            keep["cat"], dy,
            pl.BlockSpec((tk, d_model), lambda p, k: (k, 0)),
            pl.BlockSpec((tk, d_model), lambda p, k: (k, 0)),
            SDS((d_model, d_model), BF16), pl.BlockSpec((d_model, d_model), lambda p, k: (0, 0)),
            1, t_len, tk, f"grad_wout_{l}", dummy)
        token = start_exchange(g_wout.reshape(N_DEV, o8, d_model), f"wout_{l}")[0, 0]
        parts = [_attn_bwd(*keep["qkv"][i], dyas[i], keep["ya"][i], keep["lse"][i], l)
                 for i in range(len(DILATIONS))]
        dx, dproj, st_norm, qk_stats[l] = _mixer_bwd_in(
            parts, keep["proj"], dpc, dx, keep["x1"], prm[l][1] + token, wing, qkg[l], bd, l, tm)
        stats[l][1] = st_norm + st_gate
        g_win = _matmul_tn(
            keep["h1"], dproj,
            pl.BlockSpec((tk, d_model), lambda p, k: (k, 0)),
            pl.BlockSpec((tk, N_DEV // 2 * c8), lambda p, k: (k, p)),
            SDS((N_DEV, d_model, c8), BF16), pl.BlockSpec((N_DEV // 2, d_model, c8), lambda p, k: (p, 0, 0)),
            2, t_len, tk, f"grad_win_{l}", dummy)
        token = start_exchange(g_win, f"win_{l}")[0, 0]
        dx, stats[l][0], token = ffn_backward(dx, keep, l, 0, 0, prm[l][0] + token)
    grad_x = dx.reshape(x.shape)

    def update(pieces, w, m, v, name):
        shape = w.shape
        cols = shape[-1]
        outs = _adamw([p.reshape(p.shape[0], -1, cols) for p in pieces], w.reshape(-1, cols),
                      m.reshape(-1, cols), v.reshape(-1, cols), name)
        return [o.reshape(shape) for o in outs]

    last = "w1_0_0"
    lands = wait_exchange([k for k in started if k != last], dx, "grads_wait_early")
    res = {
        "w_in": update([lands[f"win_{l}"] for l in range(n_layer)], w_in, m_w_in, v_w_in, "adamw_w_in"),
        "w_out": update([lands[f"wout_{l}"] for l in range(n_layer)], w_out, m_w_out, v_w_out, "adamw_w_out"),
        "ffn_w2": update([lands[f"w2_{l}_{s}"] for l in range(n_layer) for s in range(2)],
                         ffn_w2, m_ffn_w2, v_ffn_w2, "adamw_ffn_w2"),
    }

    dmod = jnp.stack([jnp.stack([stats[l][sub][r] for sub in range(3) for r in (0, 1, 3)]) for l in range(n_layer)])
    dng = jnp.stack([jnp.stack([stats[l][sub][2] for sub in range(3)]) for l in range(n_layer)])
    dcw = jnp.stack([conv_stats[l][0:3] for l in range(n_layer)])
    dcb = jnp.stack([conv_stats[l][3] for l in range(n_layer)])
    dqk = jnp.stack([qk_stats[l][0:2].reshape(2, n_head, HEAD_DIM).sum(axis=1) for l in range(n_layer)])
    part_shapes = [dmod.shape, dng.shape, dcw.shape, dcb.shape, dqk.shape]
    parts_all = _allgather_small(_pack([dmod, dng, dcw, dcb, dqk]), "gather_small_grads")
    dmod_all = _unpack(parts_all, part_shapes)[0].reshape(N_DEV, n_layer, 9 * d_model)
    summed = _unpack(_sum_devices(parts_all), part_shapes)
    g_b_ada = summed[0].reshape(n_layer, 9 * d_model)
    ng_cols = norm_g.shape[-1]
    g_norm_g = lax.dynamic_slice_in_dim(summed[1], me * ng_cols, ng_cols, axis=2)
    cw_cols = conv_w.shape[-1]
    g_conv_w = lax.dynamic_slice_in_dim(summed[2], me * cw_cols, cw_cols, axis=2)
    g_conv_b = summed[3]
    g_q, g_k = summed[4][:, 0], summed[4][:, 1]
    dmod_loc = lax.dynamic_slice_in_dim(dmod_all, me * ada_cols, ada_cols, axis=2)
    g_w_ada = _w_ada_grad(silu_c.T, jnp.moveaxis(dmod_loc, 0, 1))
    res["w_ada"] = update([g_w_ada[None]], w_ada, m_w_ada, v_w_ada, "adamw_w_ada")

    names = ["b_ada", "norm_g", "q_norm_g", "k_norm_g", "conv_w", "conv_b"]
    g_small = [g_b_ada, g_norm_g, g_q, g_k, g_conv_w, g_conv_b]
    w_small = [b_ada, norm_g, q_norm_g, k_norm_g, conv_w, conv_b]
    m_small = [m_b_ada, m_norm_g, m_q_norm_g, m_k_norm_g, m_conv_w, m_conv_b]
    v_small = [v_b_ada, v_norm_g, v_q_norm_g, v_k_norm_g, v_conv_w, v_conv_b]
    outs = _adamw([_pack(g_small)[None]], _pack(w_small), _pack(m_small), _pack(v_small), "adamw_small")
    shapes = [w.shape for w in w_small]
    unpacked = [_unpack(o, shapes) for o in outs]
    for i, nme in enumerate(names):
        res[nme] = [unpacked[k][i] for k in range(4)]

    follows = (res["w_in"][1][0, :1, :1] + res["w_out"][1][0, :1, :1] + res["ffn_w2"][1][0, 0, :1, :1]
               + res["w_ada"][1][0, :1, :1] + outs[1][:1, :1])
    lands.update(wait_exchange([last], follows, "grads_wait_last"))
    res["ffn_w1"] = update([lands[f"w1_{l}_{s}"] for l in range(n_layer) for s in range(2)],
                           ffn_w1, m_ffn_w1, v_ffn_w1, "adamw_ffn_w1")

    order = ["w_ada", "b_ada", "norm_g", "w_in", "q_norm_g", "k_norm_g", "conv_w", "conv_b", "w_out", "ffn_w1", "ffn_w2"]
    return (loss, grad_x, *[res[n][0] for n in order], *[res[n][1] for n in order],
            *[res[n][2] for n in order], *[res[n][3] for n in order])
```

```python
import jax
import jax.numpy as jnp
from jax import lax
from jax.experimental import pallas as pl
from jax.experimental.pallas import tpu as pltpu

F32 = jnp.float32
BF16 = jnp.bfloat16
SDS = jax.ShapeDtypeStruct

N_DEV = 8
HEAD_DIM = 64
BLOCK = 128
DILATIONS = (1, 4, 16)
ATTN_BLOCKS_PER_STEP = 4
N_CHUNK = 4
EPS = 1e-6
NEG = -1e30
SM_SCALE = HEAD_DIM ** -0.5
LANES = 128
MXU_WIDTH = 256
TOKEN_TILE = 512
FFN_ROW_PART = 256
FFN_FWD_TOKEN_TILE = 256
FFN_BWD_TOKEN_TILE = 256
GRAD_TOKEN_TILE = 2048
VMEM_LIMIT_BYTES = 56 * 1024 * 1024

ADAM_LR = 0.001
ADAM_B1 = 0.9
ADAM_B2 = 0.999
ADAM_EPS = 1e-08
ADAM_WD = 0.01
ADAM_STEP = 10

MESH_ID = pl.DeviceIdType.MESH
ANY = pl.BlockSpec(memory_space=pl.ANY)
HBM_SPEC = pl.BlockSpec(memory_space=pltpu.HBM)
SEM_SPEC = pl.BlockSpec(memory_space=pltpu.SEMAPHORE)
SIDE_EFFECT = pltpu.SideEffectType.DATAFLOW_SIDE_EFFECTING


def _params(n_axes):
    return pltpu.CompilerParams(dimension_semantics=("arbitrary",) * n_axes,
                                vmem_limit_bytes=VMEM_LIMIT_BYTES)


def _dot(a, b):
    return jnp.dot(a, b, preferred_element_type=F32)


def _dot_nt(a, b):
    return lax.dot_general(a, b, (((1,), (1,)), ((), ())), preferred_element_type=F32)


def _dot_tn(a, b):
    return lax.dot_general(a, b, (((0,), (0,)), ((), ())), preferred_element_type=F32)


def _row_halves(n):
    return (slice(0, n // 2), slice(n // 2, n))


def _split_bf16(t):
    hi = t.astype(BF16)
    return hi, (t - hi.astype(F32)).astype(BF16)


def _my_place():
    x, y, c = lax.axis_index("x"), lax.axis_index("y"), lax.axis_index("c")
    return x, y, c


def _peer(place, k):
    x, y, c = place
    return ((1 - x) if k & 4 else x, (1 - y) if k & 2 else y, (1 - c) if k & 1 else c)


def _index(place):
    return 4 * place[0] + 2 * place[1] + place[2]


def _allgather_small(v, name):
    rows, cols = v.shape

    def body(x_ref, out_ref, send_sems, recv_sems, local_sem):
        me = _my_place()
        mine = pltpu.make_async_copy(x_ref, out_ref.at[_index(me)], local_sem)
        mine.start()
        sends = []
        for k in range(1, N_DEV):
            cp = pltpu.make_async_remote_copy(
                src_ref=x_ref, dst_ref=out_ref.at[_index(me)], send_sem=send_sems.at[k - 1],
                recv_sem=recv_sems.at[k - 1], device_id=_peer(me, k), device_id_type=MESH_ID)
            cp.start()
            sends.append(cp)
        for k in range(1, N_DEV):
            pltpu.make_async_remote_copy(
                src_ref=x_ref, dst_ref=out_ref.at[_index(_peer(me, k))], send_sem=send_sems.at[k - 1],
                recv_sem=recv_sems.at[k - 1], device_id=_peer(me, k), device_id_type=MESH_ID).wait_recv()
        for cp in sends:
            cp.wait_send()
        mine.wait()

    return pl.pallas_call(
        body, name=name,
        out_shape=SDS((N_DEV, rows, cols), v.dtype),
        in_specs=[pl.BlockSpec(memory_space=pltpu.VMEM)],
        out_specs=pl.BlockSpec(memory_space=pltpu.VMEM),
        scratch_shapes=[pltpu.SemaphoreType.DMA((N_DEV - 1,)), pltpu.SemaphoreType.DMA((N_DEV - 1,)),
                        pltpu.SemaphoreType.DMA],
    )(v)


def _allgather_big(lands, name):
    n = len(lands)

    def body(*refs):
        outs = refs[n:2 * n]
        send_sems, recv_sems = refs[2 * n:]
        x, y, c = _my_place()
        me, sib = (x, y, c), (x, y, 1 - c)
        chips = [(1 - x, y), (x, 1 - y), (1 - x, 1 - y)]

        def copy(i, k, block, to):
            dst = outs[i].at[_index(block)]
            return pltpu.make_async_remote_copy(
                src_ref=dst, dst_ref=dst, send_sem=send_sems.at[7 * i + k],
                recv_sem=recv_sems.at[7 * i + k], device_id=to, device_id_type=MESH_ID)

        first = []
        for i in range(n):
            first.append(copy(i, 0, me, sib))
            for j, chip in enumerate(chips):
                first.append(copy(i, 1 + j, me, (*chip, c)))
        for cp in first:
            cp.start()
        passed = []
        for i in range(n):
            for j, chip in enumerate(chips):
                copy(i, 1 + j, (*chip, c), me).wait_recv()
                fwd = copy(i, 4 + j, (*chip, c), sib)
                fwd.start()
                passed.append(fwd)
        for i in range(n):
            copy(i, 0, sib, me).wait_recv()
            for j, chip in enumerate(chips):
                copy(i, 4 + j, (*chip, 1 - c), me).wait_recv()
        for cp in first + passed:
            cp.wait_send()

    return pl.pallas_call(
        body, name=name,
        out_shape=[SDS(a.shape, a.dtype) for a in lands],
        in_specs=[ANY] * n, out_specs=[ANY] * n,
        input_output_aliases={i: i for i in range(n)},
        scratch_shapes=[pltpu.SemaphoreType.DMA((7 * n,)), pltpu.SemaphoreType.DMA((7 * n,))],
    )(*lands)


def _place_own(src, lead, me_arr, name):
    rows, cols = src.shape[-2:]
    tr = _row_tile(rows, cols)

    def body(me_ref, s_ref, o_ref):
        o_ref[...] = s_ref[...].astype(BF16)

    if lead is None:
        in_spec = pl.BlockSpec((None, tr, cols), lambda i, me_ref: (me_ref[0], i, 0))
    else:
        in_spec = pl.BlockSpec((None,) * len(lead) + (tr, cols), lambda i, me_ref: (*lead, i, 0))
    return pl.pallas_call(
        body, name=name,
        grid_spec=pltpu.PrefetchScalarGridSpec(
            num_scalar_prefetch=1, grid=(rows // tr,), in_specs=[in_spec],
            out_specs=pl.BlockSpec((None, tr, cols), lambda i, me_ref: (me_ref[0], i, 0))),
        out_shape=SDS((N_DEV, rows, cols), BF16),
        compiler_params=_params(1),
    )(me_arr, src)


def _send_start(lands, pieces, after, name):
    n = len(lands)
    arrs = list(lands) + list(pieces or [])
    n_a = len(arrs)

    def body(*refs):
        land_r, piece_r = refs[:n], refs[n:n_a]
        send, recv = refs[n_a + 1:n_a + 1 + n], refs[n_a + 1 + n:n_a + 1 + 2 * n]
        token = refs[-1]
        me = _my_place()
        for i in range(n):
            for k in range(1, N_DEV):
                peer = _peer(me, k)
                src = piece_r[i].at[_index(peer)] if pieces else land_r[i].at[_index(me)]
                pltpu.make_async_remote_copy(
                    src_ref=src, dst_ref=land_r[i].at[_index(me)], send_sem=send[i], recv_sem=recv[i],
                    device_id=peer, device_id_type=MESH_ID).start()
        token[...] = jnp.zeros_like(token)

    outs = pl.pallas_call(
        body, name=name,
        out_shape=[pltpu.SemaphoreType.DMA(())] * (2 * n) + [pltpu.HBM(a.shape, a.dtype) for a in arrs]
        + [SDS((8, LANES), F32)],
        in_specs=[HBM_SPEC] * n_a + [ANY],
        out_specs=[SEM_SPEC] * (2 * n) + [HBM_SPEC] * n_a + [pl.BlockSpec(memory_space=pltpu.VMEM)],
        input_output_aliases={i: 2 * n + i for i in range(n_a)},
        compiler_params=pltpu.CompilerParams(has_side_effects=SIDE_EFFECT),
    )(*[pltpu.with_memory_space_constraint(a, pltpu.HBM) for a in arrs], after)
    return outs[:n], outs[n:2 * n], outs[2 * n:2 * n + n_a], outs[-1]


def _send_wait(thru, sends, recvs, after, name):
    n = len(sends)
    n_a = len(thru)

    def body(*refs):
        land_r = refs[:n]
        send, recv = refs[n_a:n_a + n], refs[n_a + n:n_a + 2 * n]
        me = _my_place()
        for i in range(n):
            seven = land_r[i].at[pl.ds(0, N_DEV - 1)]
            cp = pltpu.make_async_remote_copy(src_ref=seven, dst_ref=seven, send_sem=send[i], recv_sem=recv[i],
                                              device_id=_peer(me, 1), device_id_type=MESH_ID)
            cp.wait_send()
            cp.wait_recv()

    outs = pl.pallas_call(
        body, name=name,
        out_shape=[pltpu.HBM(a.shape, a.dtype) for a in thru],
        in_specs=[HBM_SPEC] * n_a + [SEM_SPEC] * (2 * n) + [ANY],
        out_specs=[HBM_SPEC] * n_a,
        input_output_aliases={i: i for i in range(n_a)},
        compiler_params=pltpu.CompilerParams(has_side_effects=SIDE_EFFECT),
    )(*thru, *sends, *recvs, after)
    return outs[:n]


def _ada_fwd(x, prm_ref):
    rstd = lax.rsqrt(jnp.mean(x * x, axis=-1, keepdims=True) + EPS)
    return (x * rstd * prm_ref[0:1, :]) * (1.0 + prm_ref[2:3, :]) + prm_ref[1:2, :]


def _ada_bwd(dh, x, prm_ref):
    ng, ops = prm_ref[0:1, :], 1.0 + prm_ref[2:3, :]
    rstd = lax.rsqrt(jnp.mean(x * x, axis=-1, keepdims=True) + EPS)
    xhat = x * rstd
    dxhat = dh * (ops * ng)
    dx = rstd * (dxhat - xhat * jnp.mean(dxhat * xhat, axis=-1, keepdims=True))
    dshift = jnp.sum(dh, axis=0, keepdims=True)
    dscale = jnp.sum(dh * (xhat * ng), axis=0, keepdims=True)
    dng = jnp.sum(dh * (xhat * ops), axis=0, keepdims=True)
    return dx, dshift, dscale, dng


def _head_mean(t, bd):
    hi = t.astype(BF16)
    g = bd.shape[0]
    parts = [_dot(hi[:, i:i + g], bd) for i in range(0, t.shape[1], g)]
    return jnp.concatenate(parts, axis=1) * (1.0 / HEAD_DIM)


def _join_columns(w_hbm, dst_ref, sems):
    cols = w_hbm.shape[2]
    copies = [pltpu.make_async_copy(w_hbm.at[j], dst_ref.at[:, pl.ds(j * cols, cols)], sems.at[j])
              for j in range(w_hbm.shape[0])]
    for cp in copies:
        cp.start()
    for cp in copies:
        cp.wait()


def _shift_rows(z, before, tm):
    row = lax.broadcasted_iota(jnp.int32, z.shape, 0)
    z1 = jnp.where(row == 0, before[7:8, :], pltpu.roll(z, 1, 0))
    z2 = jnp.where(row == 0, before[6:7, :], jnp.where(row == 1, before[7:8, :], pltpu.roll(z, 2, 0)))
    return z1, z2


def _mod_part(c_all, w_ada, b_loc):
    n_layer, d_model, cols = w_ada.shape

    def body(c_ref, w_ref, b_ref, out_ref, sc_ref):
        cv = c_ref[...]
        sc = cv * jax.nn.sigmoid(cv)
        sc_ref[...] = sc
        a_hi, a_lo = _split_bf16(sc)
        w_hi, w_lo = _split_bf16(w_ref[...])
        out_ref[...] = _dot(a_hi, w_hi) + _dot(a_hi, w_lo) + _dot(a_lo, w_hi) + b_ref[...]

    return pl.pallas_call(
        body, name="mod_part", grid=(n_layer,),
        in_specs=[pl.BlockSpec((N_DEV, d_model), lambda l: (0, 0)),
                  pl.BlockSpec((None, d_model, cols), lambda l: (l, 0, 0)),
                  pl.BlockSpec((None, 1, cols), lambda l: (l, 0, 0))],
        out_specs=[pl.BlockSpec((None, N_DEV, cols), lambda l: (l, 0, 0)),
                   pl.BlockSpec((N_DEV, d_model), lambda l: (0, 0))],
        out_shape=[SDS((n_layer, N_DEV, cols), F32), SDS((N_DEV, d_model), F32)],
        compiler_params=_params(1),
    )(c_all, w_ada, b_loc)


def _w_ada_grad(sc_t, dmod_loc):
    d_model = sc_t.shape[0]
    n_layer, _, cols = dmod_loc.shape

    def body(s_ref, d_ref, out_ref):
        acc = s_ref[:, 0:1] * d_ref[0:1, :]
        for b in range(1, N_DEV):
            acc = acc + s_ref[:, b:b + 1] * d_ref[b:b + 1, :]
        out_ref[...] = acc

    return pl.pallas_call(
        body, name="w_ada_grad", grid=(n_layer,),
        in_specs=[pl.BlockSpec((d_model, N_DEV), lambda l: (0, 0)),
                  pl.BlockSpec((None, N_DEV, cols), lambda l: (l, 0, 0))],
        out_specs=pl.BlockSpec((None, d_model, cols), lambda l: (l, 0, 0)),
        out_shape=SDS((n_layer, d_model, cols), F32),
        compiler_params=_params(1),
    )(sc_t, dmod_loc)


def _sum_devices(g):
    _, rows, cols = g.shape

    def body(g_ref, out_ref):
        acc = g_ref[0]
        for i in range(1, N_DEV):
            acc = acc + g_ref[i]
        out_ref[...] = acc

    return pl.pallas_call(body, name="sum_devices", out_shape=SDS((rows, cols), F32))(g)


def _ffn_fwd(x, prm, w1r, w2g, l, sub, tm):
    t_len, d_model = x.shape
    f8, r8 = w1r.shape[-1], w2g.shape[-2]
    n_m = t_len // tm

    def body(x_ref, xn_ref, prm_ref, w1_ref, w2_ref, act_ref, h_ref, y_ref, xo_ref, h_sc):
        @pl.when(pl.program_id(0) == 0)
        def _():
            h_sc[...] = _ada_fwd(x_ref[...], prm_ref).astype(BF16)

        h = h_sc[...]
        h_ref[...] = h

        def first(j):
            return _dot(h, w1_ref[0, j]), _dot(h, w1_ref[1, j])

        gu = first(0)
        yv = None
        for j in range(N_CHUNK):
            g, u = gu
            if j + 1 < N_CHUNK:
                gu = first(j + 1)
            else:
                h_next = _ada_fwd(xn_ref[...], prm_ref).astype(BF16)
            sig = jax.nn.sigmoid(g)
            sil = g * sig
            s = (sil * u).astype(BF16)
            act_ref[0, j] = (u * (sig * (1.0 + g * (1.0 - sig)))).astype(BF16)
            act_ref[1, j] = sil.astype(BF16)
            act_ref[2, j] = s
            part = _dot(s, w2_ref[2 * j:2 * j + 2].reshape(2 * r8, d_model))
            yv = part if yv is None else yv + part
        y_ref[...] = yv.astype(BF16)
        xo_ref[...] = x_ref[...] + (0.5 * prm_ref[3:4, :]) * yv
        h_sc[...] = h_next

    tile = pl.BlockSpec((tm, d_model), lambda m: (m, 0))
    return pl.pallas_call(
        body, name=f"ffn_fwd_{l}_{sub}", grid=(n_m,),
        in_specs=[tile, pl.BlockSpec((tm, d_model), lambda m: (jnp.minimum(m + 1, n_m - 1), 0)),
                  pl.BlockSpec((8, d_model), lambda m: (0, 0)),
                  pl.BlockSpec((2, N_CHUNK, d_model, f8), lambda m: (0, 0, 0, 0), pipeline_mode=pl.Buffered(1)),
                  pl.BlockSpec((N_DEV, r8, d_model), lambda m: (0, 0, 0), pipeline_mode=pl.Buffered(1))],
        out_specs=[pl.BlockSpec((3, N_CHUNK, tm, f8), lambda m: (0, 0, m, 0)), tile, tile, tile],
        out_shape=[SDS((3, N_CHUNK, t_len, f8), BF16), SDS((t_len, d_model), BF16),
                   SDS((t_len, d_model), BF16), SDS((t_len, d_model), F32)],
        scratch_shapes=[pltpu.VMEM((tm, d_model), BF16)],
        compiler_params=_params(1),
    )(x, x, prm, w1r, w2g)


def _ffn_bwd(dxo, y, x_in, prm, act, w1r, w2g, l, sub, tm):
    t_len, d_model = x_in.shape
    f8, r8 = w1r.shape[-1], w2g.shape[-2]
    n_m = t_len // tm

    def body(dxo_ref, y_ref, x_ref, prm_ref, act_ref, w1_ref, w2_ref, dx_ref, da_ref, dy_ref, st_ref):
        @pl.when(pl.program_id(0) == 0)
        def _():
            st_ref[...] = jnp.zeros_like(st_ref)

        dxo_v = dxo_ref[...]
        dy = ((0.5 * prm_ref[3:4, :]) * dxo_v).astype(BF16)
        dy_ref[...] = dy
        st_ref[3:4, :] += jnp.sum(0.5 * dxo_v * y_ref[...].astype(F32), axis=0, keepdims=True)

        def first(j):
            return _dot_nt(dy, w2_ref[2 * j:2 * j + 2].reshape(2 * r8, d_model))

        ds = first(0)
        dh = None
        for j in range(N_CHUNK):
            ds_now = ds
            if j + 1 < N_CHUNK:
                ds = first(j + 1)
            dg = (ds_now * act_ref[0, j].astype(F32)).astype(BF16)
            du = (ds_now * act_ref[1, j].astype(F32)).astype(BF16)
            da_ref[0, j] = dg
            da_ref[1, j] = du
            part = _dot_nt(dg, w1_ref[0, j]) + _dot_nt(du, w1_ref[1, j])
            dh = part if dh is None else dh + part
        dx, dshift, dscale, dng = _ada_bwd(dh, x_ref[...], prm_ref)
        dx_ref[...] = dxo_v + dx
        st_ref[0:1, :] += dshift
        st_ref[1:2, :] += dscale
        st_ref[2:3, :] += dng

    tile = pl.BlockSpec((tm, d_model), lambda m: (m, 0))
    fixed = pl.BlockSpec((8, d_model), lambda m: (0, 0))
    return pl.pallas_call(
        body, name=f"ffn_bwd_{l}_{sub}", grid=(n_m,),
        in_specs=[tile, tile, tile, fixed,
                  pl.BlockSpec((2, N_CHUNK, tm, f8), lambda m: (0, 0, m, 0)),
                  pl.BlockSpec((2, N_CHUNK, d_model, f8), lambda m: (0, 0, 0, 0), pipeline_mode=pl.Buffered(1)),
                  pl.BlockSpec((N_DEV, r8, d_model), lambda m: (0, 0, 0), pipeline_mode=pl.Buffered(1))],
        out_specs=[tile, pl.BlockSpec((2, N_CHUNK, tm, f8), lambda m: (0, 0, m, 0)), tile, fixed],
        out_shape=[SDS((t_len, d_model), F32), SDS((2, N_CHUNK, t_len, f8), BF16),
                   SDS((t_len, d_model), BF16), SDS((8, d_model), F32)],
        compiler_params=_params(1),
    )(dxo, y, x_in, prm, act, w1r, w2g)


def _matmul_tn(a, b, a_spec, b_spec, out_shape, out_spec, n_piece, t_len, tk, name, after):
    def body(a_ref, b_ref, after_ref, o_ref, acc):
        k = pl.program_id(1)

        @pl.when(k == 0)
        def _():
            acc[...] = jnp.zeros_like(acc)

        m_out, n_out = acc.shape
        if (m_out // 2) % LANES == 0:
            for rows in _row_halves(m_out):
                acc[rows, :] += _dot_tn(a_ref[:, rows], b_ref[...])
        else:
            for cols in _row_halves(n_out):
                acc[:, cols] += _dot_tn(a_ref[...], b_ref[:, cols])

        @pl.when(k == pl.num_programs(1) - 1)
        def _():
            if len(o_ref.shape) == 3:
                cols = o_ref.shape[2]
                for q in range(o_ref.shape[0]):
                    o_ref[q] = acc[:, q * cols:(q + 1) * cols].astype(o_ref.dtype)
            else:
                o_ref[...] = acc[...].astype(o_ref.dtype)

    blk = tuple(d for d in out_spec.block_shape if d is not None)
    acc_shape = blk if len(blk) == 2 else (blk[1], blk[0] * blk[2])
    return pl.pallas_call(
        body, name=name, grid=(n_piece, t_len // tk),
        in_specs=[a_spec, b_spec, ANY], out_specs=out_spec, out_shape=out_shape,
        scratch_shapes=[pltpu.VMEM(acc_shape, F32)],
        compiler_params=_params(2),
    )(a, b, after)


def _groups(tm, width):
    return pltpu.VMEM((width // LANES, tm, LANES), F32)


def _to_groups(val, dst_ref, first=0):
    for g in range(val.shape[1] // LANES):
        dst_ref[first + g] = val[:, g * LANES:(g + 1) * LANES]


def _from_groups(src_ref):
    return jnp.concatenate([src_ref[g] for g in range(src_ref.shape[0])], axis=1)


def _split_residues(src_ref, out_ref, dil, tm):
    for g in range(src_ref.shape[0]):
        for r in range(dil):
            out_ref[r, :, g * LANES:(g + 1) * LANES] = (
                src_ref.at[g][pl.ds(r, tm // dil, stride=dil), :].astype(out_ref.dtype))


def _merge_residues(in_ref, dst_ref, dil, tm, add=False):
    for g in range(dst_ref.shape[0]):
        for r in range(dil):
            rows = pl.ds(r, tm // dil, stride=dil)
            val = in_ref[r, :, g * LANES:(g + 1) * LANES].astype(F32)
            dst = dst_ref.at[g]
            dst[rows, :] = (dst[rows, :] + val) if add else val


def _res_spec(dil, tm, width, index=lambda m: m):
    return pl.BlockSpec((dil, tm // dil, width), lambda m: (0, index(m), 0))


def _mixer_in(x, prm, wing, qkg, cprm, bd, l, tm):
    t_len, d_model = x.shape
    a_w = d_model // 2
    c_w = d_model - a_w
    c8 = wing.shape[-1]
    p_c = N_DEV * c8
    dils = DILATIONS[1:]

    def body(x_ref, prm_ref, w_ref, qkg_ref, cprm_ref, bd_ref, proj_ref, qkn_ref, h_ref, ycv_ref, *rest):
        res_refs, (pf, qsc, vsc, carry, wcat, wsems) = rest[:2 * len(dils)], rest[2 * len(dils):]

        @pl.when(pl.program_id(0) == 0)
        def _():
            carry[...] = jnp.zeros_like(carry)
            _join_columns(w_ref, wcat, wsems)

        h = _ada_fwd(x_ref[...], prm_ref).astype(BF16)
        h_ref[...] = h
        for cols in _row_halves(p_c):
            pf[:, cols] = _dot(h, wcat[:, cols])
        proj_ref[...] = pf[...].astype(BF16)
        bdm = bd_ref[...]
        for i in range(2):
            t = pf[:, i * a_w:(i + 1) * a_w]
            rstd = lax.rsqrt(_head_mean(t * t, bdm) + EPS)
            tn = t * rstd * qkg_ref[i:i + 1, :]
            qkn_ref[:, i * a_w:(i + 1) * a_w] = tn.astype(BF16)
            _to_groups(tn, qsc, i * (a_w // LANES))
        _to_groups(pf[:, 2 * a_w:3 * a_w], vsc)
        for i, dil in enumerate(dils):
            _split_residues(qsc, res_refs[2 * i], dil, tm)
            _split_residues(vsc, res_refs[2 * i + 1], dil, tm)
        gb = pf[:, 3 * a_w:3 * a_w + c_w]
        z = pf[:, 3 * a_w + c_w:3 * a_w + 2 * c_w] * pf[:, 3 * a_w + 2 * c_w:3 * a_w + 3 * c_w]
        z1, z2 = _shift_rows(z, carry[...], tm)
        cv = cprm_ref[0:1, :] * z2 + cprm_ref[1:2, :] * z1 + cprm_ref[2:3, :] * z + cprm_ref[3:4, :]
        ycv_ref[...] = (gb * cv).astype(BF16)
        carry[...] = z[tm - 8:tm, :]

    res_specs, res_shapes = [], []
    for dil in dils:
        res_specs += [_res_spec(dil, tm, 2 * a_w), _res_spec(dil, tm, a_w)]
        res_shapes += [SDS((dil, t_len // dil, 2 * a_w), BF16), SDS((dil, t_len // dil, a_w), BF16)]
    return pl.pallas_call(
        body, name=f"mixer_in_{l}", grid=(t_len // tm,),
        in_specs=[pl.BlockSpec((tm, d_model), lambda m: (m, 0)),
                  pl.BlockSpec((8, d_model), lambda m: (0, 0)),
                  ANY,
                  pl.BlockSpec((8, a_w), lambda m: (0, 0)),
                  pl.BlockSpec((8, c_w), lambda m: (0, 0)),
                  pl.BlockSpec(bd.shape, lambda m: (0, 0))],
        out_specs=[pl.BlockSpec((tm, p_c), lambda m: (m, 0)),
                   pl.BlockSpec((tm, 2 * a_w), lambda m: (m, 0)),
                   pl.BlockSpec((tm, d_model), lambda m: (m, 0)),
                   pl.BlockSpec((tm, c_w), lambda m: (m, 0))] + res_specs,
        out_shape=[SDS((t_len, p_c), BF16), SDS((t_len, 2 * a_w), BF16),
                   SDS((t_len, d_model), BF16), SDS((t_len, c_w), BF16)] + res_shapes,
        scratch_shapes=[pltpu.VMEM((tm, p_c), F32), _groups(tm, 2 * a_w), _groups(tm, a_w),
                        pltpu.VMEM((8, c_w), F32), pltpu.VMEM((d_model, p_c), BF16),
                        pltpu.SemaphoreType.DMA((N_DEV,))],
        compiler_params=_params(1),
    )(x, prm, wing, qkg, cprm, bd)


def _band_mask(first_block):
    qi = lax.broadcasted_iota(jnp.int32, (BLOCK, 2 * BLOCK), 0)
    kj = lax.broadcasted_iota(jnp.int32, (BLOCK, 2 * BLOCK), 1)
    lowest = jnp.where(first_block, BLOCK, 0)
    return (kj >= qi) & (kj <= qi + BLOCK) & (kj >= lowest)


def _blocks_per_step(seq):
    n_sb = min(ATTN_BLOCKS_PER_STEP, seq // BLOCK)
    assert seq % (n_sb * BLOCK) == 0
    return n_sb


def _block_rows(sb):
    return slice(sb * BLOCK, (sb + 1) * BLOCK)


def _with_previous(prev_ref, cur_ref, sb, sl):
    before = prev_ref[:, sl] if sb == 0 else cur_ref[_block_rows(sb - 1), sl]
    return jnp.concatenate([before, cur_ref[_block_rows(sb), sl]], axis=0)


def _heads_on_lanes(cols):
    width = LANES // len(cols)
    lane = lax.broadcasted_iota(jnp.int32, (BLOCK, LANES), 1)
    out = jnp.broadcast_to(cols[-1], (BLOCK, LANES))
    for h in range(len(cols) - 2, -1, -1):
        out = jnp.where(lane < (h + 1) * width, cols[h], out)
    return out


def _attn_fwd(qk, v, v_col, l):
    dil, seq, a2 = qk.shape
    a_w = a2 // 2
    n_sb = _blocks_per_step(seq)
    rows_step = n_sb * BLOCK
    n_pair = a_w // LANES

    def body(q_ref, kc_ref, vc_ref, o_ref, lse_ref, kp_ref, vp_ref):
        @pl.when(pl.program_id(1) == 0)
        def _():
            kp_ref[...] = jnp.zeros_like(kp_ref)
            vp_ref[...] = jnp.zeros_like(vp_ref)

        masks = [_band_mask(pl.program_id(1) == 0)] + [_band_mask(False)] * (n_sb - 1)
        low = lax.broadcasted_iota(jnp.int32, (BLOCK, LANES), 1) < HEAD_DIM
        units = [(sb, slice(p * LANES, (p + 1) * LANES)) for sb in range(n_sb) for p in range(a_w // LANES)]
        scores = []
        for sb, sl in units:
            qp = q_ref[_block_rows(sb), sl]
            k2 = _with_previous(kp_ref, kc_ref, sb, sl)
            scores.append([_dot_nt(jnp.where(half, qp, jnp.zeros_like(qp)), k2) for half in (low, ~low)])
        probs, lses = [], []
        for (sb, sl), pair in zip(units, scores):
            pr, ls = [], []
            for s in pair:
                s = jnp.where(masks[sb], s * SM_SCALE, NEG)
                mx = jnp.max(s, axis=-1, keepdims=True)
                e = jnp.exp(s - mx)
                den = jnp.sum(e, axis=-1, keepdims=True)
                pr.append((e * (1.0 / den)).astype(BF16))
                ls.append(mx + jnp.log(den))
            probs.append(pr)
            lses.append(ls)
        for (sb, sl), pr, ls in zip(units, probs, lses):
            v2 = _with_previous(vp_ref, vc_ref, sb, sl)
            o_ref[_block_rows(sb), sl] = jnp.where(low, _dot(pr[0], v2), _dot(pr[1], v2)).astype(BF16)
        for sb in range(n_sb):
            cols = [c for ls in lses[sb * n_pair:(sb + 1) * n_pair] for c in ls]
            lse_ref[_block_rows(sb), :] = _heads_on_lanes(cols)
        kp_ref[...] = kc_ref[_block_rows(n_sb - 1), :]
        vp_ref[...] = vc_ref[_block_rows(n_sb - 1), :]

    def blk(f):
        return pl.BlockSpec((None, rows_step, a_w), f)

    return pl.pallas_call(
        body, name=f"attn_fwd_{l}_d{dil}", grid=(dil, seq // rows_step),
        in_specs=[blk(lambda r, b: (r, b, 0)),
                  blk(lambda r, b: (r, b, 1)),
                  blk(lambda r, b: (r, b, v_col))],
        out_specs=[blk(lambda r, b: (r, b, 0)), pl.BlockSpec((None, rows_step, LANES), lambda r, b: (r, b, 0))],
        out_shape=[SDS((dil, seq, a_w), BF16), SDS((dil, seq, LANES), F32)],
        scratch_shapes=[pltpu.VMEM((BLOCK, a_w), BF16), pltpu.VMEM((BLOCK, a_w), BF16)],
        compiler_params=_params(2),
    )(qk, qk, v)


def _attn_combine(outs, lses, ycv, expand, tm, l):
    t_len, c_w = ycv.shape
    a_w = outs[0].shape[-1]
    dils = DILATIONS[1:]
    n_r = len(dils)

    def body(o1, l1, *rest):
        o_res, l_res, y_ref, ex_ref = rest[:n_r], rest[n_r:2 * n_r], rest[2 * n_r], rest[2 * n_r + 1]
        cat_ref, lse_ref = rest[2 * n_r + 2], rest[2 * n_r + 3]
        ya_res, lse_res = rest[2 * n_r + 4:3 * n_r + 4], rest[3 * n_r + 4:4 * n_r + 4]
        scr = rest[4 * n_r + 4:]
        so, sl, sya, slse = scr[:n_r], scr[n_r:2 * n_r], scr[2 * n_r], scr[2 * n_r + 1]
        for i, dil in enumerate(dils):
            _merge_residues(o_res[i], so[i], dil, tm)
            _merge_residues(l_res[i], sl[i], dil, tm)
        ls = [l1[...]] + [s[0] for s in sl]
        mx = jnp.maximum(jnp.maximum(ls[0], ls[1]), ls[2])
        es = [jnp.exp(t - mx) for t in ls]
        den = es[0] + es[1] + es[2]
        inv = 1.0 / den
        slse[0] = mx + jnp.log(den)
        lse_ref[...] = slse[0]
        ex = ex_ref[...]
        ov = [o1[...].astype(F32)] + [_from_groups(s) for s in so]
        ya = jnp.zeros((tm, a_w), F32)
        for e, o in zip(es, ov):
            hi, lo = _split_bf16(e * inv)
            ya = ya + (_dot(hi, ex) + _dot(lo, ex)) * o
        _to_groups(ya, sya)
        cat_ref[:, :a_w] = ya.astype(BF16)
        cat_ref[:, a_w:] = y_ref[...]
        for i, dil in enumerate(dils):
            _split_residues(sya, ya_res[i], dil, tm)
            _split_residues(slse, lse_res[i], dil, tm)

    ta = pl.BlockSpec((tm, a_w), lambda m: (m, 0))
    tl = pl.BlockSpec((tm, LANES), lambda m: (m, 0))
    res_a = [_res_spec(dil, tm, a_w) for dil in dils]
    res_l = [_res_spec(dil, tm, LANES) for dil in dils]
    return pl.pallas_call(
        body, name=f"attn_combine_{l}", grid=(t_len // tm,),
        in_specs=[ta, tl] + res_a + res_l + [pl.BlockSpec((tm, c_w), lambda m: (m, 0)),
                                            pl.BlockSpec((LANES, a_w), lambda m: (0, 0))],
        out_specs=[pl.BlockSpec((tm, a_w + c_w), lambda m: (m, 0)), tl] + res_a + res_l,
        out_shape=[SDS((t_len, a_w + c_w), BF16), SDS((t_len, LANES), F32)]
        + [SDS((dil, t_len // dil, a_w), BF16) for dil in dils]
        + [SDS((dil, t_len // dil, LANES), F32) for dil in dils],
        scratch_shapes=[_groups(tm, a_w)] * n_r + [_groups(tm, LANES)] * n_r + [_groups(tm, a_w), _groups(tm, LANES)],
        compiler_params=_params(1),
    )(outs[0].reshape(t_len, a_w), lses[0].reshape(t_len, LANES), *outs[1:], *lses[1:], ycv, expand)


def _mixer_out(cat, x, prm, woutg, l, tm):
    t_len, d_model = x.shape
    r8 = woutg.shape[-2]

    def body(cat_ref, x_ref, prm_ref, w_ref, xo_ref, y_ref):
        w = w_ref[...].reshape(N_DEV * r8, d_model)
        for rows in _row_halves(tm):
            yv = _dot(cat_ref[rows, :], w)
            y_ref[rows, :] = yv.astype(BF16)
            xo_ref[rows, :] = x_ref[rows, :] + prm_ref[3:4, :] * yv

    tile = pl.BlockSpec((tm, d_model), lambda m: (m, 0))
    return pl.pallas_call(
        body, name=f"mixer_out_{l}", grid=(t_len // tm,),
        in_specs=[tile, tile, pl.BlockSpec((8, d_model), lambda m: (0, 0)),
                  pl.BlockSpec((N_DEV, r8, d_model), lambda m: (0, 0, 0))],
        out_specs=[tile, tile],
        out_shape=[SDS((t_len, d_model), F32), SDS((t_len, d_model), BF16)],
        compiler_params=_params(1),
    )(cat, x, prm, woutg)


def _mixer_bwd_out(dxo, ymix, prm, woutg, proj, cprm, l, tm):
    t_len, d_model = dxo.shape
    a_w = d_model // 2
    c_w = d_model - a_w
    r8 = woutg.shape[-2]
    n_m = t_len // tm
    hb = tm // 8

    dils = DILATIONS[1:]

    def body(dxo_ref, ym_ref, prm_ref, w_ref, pc_ref, halo_ref, cprm_ref,
             dya_ref, dpc_ref, dy_ref, st_ref, cs_ref, *rest):
        dya_res, (carry, dsc) = rest[:len(dils)], rest[len(dils):]
        step = pl.program_id(0)
        tile_i = n_m - 1 - step

        @pl.when(step == 0)
        def _():
            st_ref[...] = jnp.zeros_like(st_ref)
            cs_ref[...] = jnp.zeros_like(cs_ref)
            carry[...] = jnp.zeros_like(carry)

        dxo_v = dxo_ref[...]
        dy = (prm_ref[3:4, :] * dxo_v).astype(BF16)
        dy_ref[...] = dy
        st_ref[3:4, :] += jnp.sum(dxo_v * ym_ref[...].astype(F32), axis=0, keepdims=True)
        w = w_ref[...].reshape(N_DEV * r8, d_model)
        dcat = jnp.concatenate([_dot_nt(dy[rows, :], w) for rows in _row_halves(tm)], axis=0)
        _to_groups(dcat[:, :a_w], dsc)
        dya_ref[...] = dcat[:, :a_w].astype(BF16)
        for i, dil in enumerate(dils):
            _split_residues(dsc, dya_res[i], dil, tm)
        dyc = dcat[:, a_w:]
        gb = pc_ref[:, :c_w].astype(F32)
        gc = pc_ref[:, c_w:2 * c_w].astype(F32)
        u = pc_ref[:, 2 * c_w:].astype(F32)
        z = gc * u
        before = halo_ref[:, c_w:2 * c_w].astype(F32) * halo_ref[:, 2 * c_w:].astype(F32)
        before = jnp.where(tile_i > 0, before, jnp.zeros_like(before))
        z1, z2 = _shift_rows(z, before, tm)
        w0, w1, w2 = cprm_ref[0:1, :], cprm_ref[1:2, :], cprm_ref[2:3, :]
        cv = w0 * z2 + w1 * z1 + w2 * z + cprm_ref[3:4, :]
        dcv = dyc * gb
        cs_ref[0:1, :] += jnp.sum(dcv * z2, axis=0, keepdims=True)
        cs_ref[1:2, :] += jnp.sum(dcv * z1, axis=0, keepdims=True)
        cs_ref[2:3, :] += jnp.sum(dcv * z, axis=0, keepdims=True)
        cs_ref[3:4, :] += jnp.sum(dcv, axis=0, keepdims=True)
        row = lax.broadcasted_iota(jnp.int32, dcv.shape, 0)
        after = carry[...]
        d1 = jnp.where(row == tm - 1, after[0:1, :], pltpu.roll(dcv, tm - 1, 0))
        d2 = jnp.where(row == tm - 2, after[0:1, :],
                       jnp.where(row == tm - 1, after[1:2, :], pltpu.roll(dcv, tm - 2, 0)))
        dz = w2 * dcv + w1 * d1 + w0 * d2
        dpc_ref[:, :c_w] = (dyc * cv).astype(BF16)
        dpc_ref[:, c_w:2 * c_w] = (dz * u).astype(BF16)
        dpc_ref[:, 2 * c_w:] = (dz * gc).astype(BF16)
        carry[...] = dcv[0:8, :]

    def rev(width, col=0):
        return pl.BlockSpec((tm, width), lambda s: (n_m - 1 - s, col))

    fixed_d = pl.BlockSpec((8, d_model), lambda s: (0, 0))
    fixed_c = pl.BlockSpec((8, c_w), lambda s: (0, 0))
    return pl.pallas_call(
        body, name=f"mixer_bwd_out_{l}", grid=(n_m,),
        in_specs=[rev(d_model), rev(d_model), fixed_d,
                  pl.BlockSpec((N_DEV, r8, d_model), lambda s: (0, 0, 0)),
                  rev(3 * c_w, 1),
                  pl.BlockSpec((8, 3 * c_w), lambda s: (jnp.maximum((n_m - 1 - s) * hb - 1, 0), 1)),
                  fixed_c],
        out_specs=[rev(a_w), rev(3 * c_w), rev(d_model), fixed_d, fixed_c]
        + [_res_spec(dil, tm, a_w, lambda s: n_m - 1 - s) for dil in dils],
        out_shape=[SDS((t_len, a_w), BF16), SDS((t_len, 3 * c_w), BF16), SDS((t_len, d_model), BF16),
                   SDS((8, d_model), F32), SDS((8, c_w), F32)]
        + [SDS((dil, t_len // dil, a_w), BF16) for dil in dils],
        scratch_shapes=[pltpu.VMEM((8, c_w), F32), _groups(tm, a_w)],
        compiler_params=_params(1),
    )(dxo, ymix, prm, woutg, proj, proj, cprm)


def _attn_bwd(qk, v, v_col, dya, ya, lse, l):
    dil, seq, a2 = qk.shape
    a_w = a2 // 2
    n_sb = _blocks_per_step(seq)
    rows_step = n_sb * BLOCK
    n_step = seq // rows_step
    last = _block_rows(n_sb - 1)
    lse_w = LANES // (a_w // HEAD_DIM)
    single = n_step == 1

    def body(q_ref, kc_ref, vc_ref, do_ref, ya_ref, lse_ref,
             dq_ref, dk_ref, dv_ref, hold_k, hold_v, acc_k, acc_v, kp_ref, vp_ref):
        j = pl.program_id(1)

        @pl.when(j == 0)
        def _():
            kp_ref[...] = jnp.zeros_like(kp_ref)
            vp_ref[...] = jnp.zeros_like(vp_ref)

        @pl.when((pl.program_id(0) == 0) & (j == 0))
        def _():
            hold_k[...] = jnp.zeros_like(hold_k)
            hold_v[...] = jnp.zeros_like(hold_v)

        @pl.when(j < n_step)
        def _():
            masks = [_band_mask(j == 0)] + [_band_mask(False)] * (n_sb - 1)
            low = lax.broadcasted_iota(jnp.int32, (BLOCK, LANES), 1) < HEAD_DIM
            units = [(sb, slice(p * LANES, (p + 1) * LANES)) for sb in range(n_sb) for p in range(a_w // LANES)]
            acc_k[...] = jnp.zeros_like(acc_k)
            acc_v[...] = jnp.zeros_like(acc_v)
            heads, scores, dps = [], [], []
            for sb, sl in units:
                qp = q_ref[_block_rows(sb), sl]
                k2 = _with_previous(kp_ref, kc_ref, sb, sl)
                v2 = _with_previous(vp_ref, vc_ref, sb, sl)
                dop = do_ref[_block_rows(sb), sl]
                prod = dop.astype(F32) * ya_ref[_block_rows(sb), sl].astype(F32)
                lsep = lse_ref[_block_rows(sb), :]
                first = 2 * (sl.start // LANES) * lse_w
                for hh, half in enumerate((low, ~low)):
                    qh = jnp.where(half, qp, jnp.zeros_like(qp))
                    doh = jnp.where(half, dop, jnp.zeros_like(dop))
                    dsum = jnp.sum(jnp.where(half, prod, 0.0), axis=-1, keepdims=True)
                    heads.append((qh, doh, dsum, lsep[:, first + hh * lse_w:first + hh * lse_w + 1], masks[sb]))
                    scores.append(_dot_nt(qh, k2))
                    dps.append(_dot_nt(doh, v2))
            dss, prs = [], []
            for (qh, doh, dsum, lse_h, valid), s, dp in zip(heads, scores, dps):
                pr = jnp.exp(jnp.where(valid, s * SM_SCALE, NEG) - lse_h)
                dss.append((pr * (dp - dsum) * SM_SCALE).astype(BF16))
                prs.append(pr.astype(BF16))
            for i, (sb, sl) in enumerate(units):
                k2 = _with_previous(kp_ref, kc_ref, sb, sl)
                h0, h1 = heads[2 * i], heads[2 * i + 1]
                dq_ref[_block_rows(sb), sl] = jnp.where(low, _dot(dss[2 * i], k2),
                                                        _dot(dss[2 * i + 1], k2)).astype(BF16)
                two = slice(sb * BLOCK, (sb + 2) * BLOCK)
                acc_k[two, sl] += _dot_tn(dss[2 * i], h0[0]) + _dot_tn(dss[2 * i + 1], h1[0])
                acc_v[two, sl] += _dot_tn(prs[2 * i], h0[1]) + _dot_tn(prs[2 * i + 1], h1[1])
            for out_ref, hold, acc in ((dk_ref, hold_k, acc_k), (dv_ref, hold_v, acc_v)):
                if single:
                    out_ref[...] = acc[BLOCK:, :].astype(BF16)
                    continue
                if n_sb > 1:
                    out_ref[:rows_step - BLOCK, :] = hold[:rows_step - BLOCK, :].astype(BF16)
                out_ref[last, :] = (hold[last, :] + acc[:BLOCK, :]).astype(BF16)
                hold[...] = acc[BLOCK:, :]
            kp_ref[...] = kc_ref[last, :]
            vp_ref[...] = vc_ref[last, :]

        @pl.when(j == n_step)
        def _():
            dk_ref[...] = hold_k[...].astype(BF16)
            dv_ref[...] = hold_v[...].astype(BF16)

    def blk(f):
        return pl.BlockSpec((None, rows_step, a_w), f)

    def cur(j):
        return jnp.minimum(j, n_step - 1)

    def late(j):
        return jnp.maximum(j - 1, 0)

    return pl.pallas_call(
        body, name=f"attn_bwd_{l}_d{dil}", grid=(dil, 1 if single else n_step + 1),
        in_specs=[blk(lambda r, j: (r, cur(j), 0)),
                  blk(lambda r, j: (r, cur(j), 1)),
                  blk(lambda r, j: (r, cur(j), v_col)),
                  blk(lambda r, j: (r, cur(j), 0)),
                  blk(lambda r, j: (r, cur(j), 0)),
                  pl.BlockSpec((None, rows_step, LANES), lambda r, j: (r, cur(j), 0))],
        out_specs=[blk(lambda r, j: (r, cur(j), 0)), blk(lambda r, j: (r, late(j), 0)),
                   blk(lambda r, j: (r, late(j), 0))],
        out_shape=[SDS((dil, seq, a_w), BF16)] * 3,
        scratch_shapes=[pltpu.VMEM((rows_step, a_w), F32), pltpu.VMEM((rows_step, a_w), F32),
                        pltpu.VMEM((rows_step + BLOCK, a_w), F32), pltpu.VMEM((rows_step + BLOCK, a_w), F32),
                        pltpu.VMEM((BLOCK, a_w), BF16), pltpu.VMEM((BLOCK, a_w), BF16)],
        compiler_params=_params(2),
    )(qk, qk, v, dya, ya, lse)


def _mixer_bwd_in(parts, proj, dpc, dxo, x_in, prm, wing, qkg, bd, l, tm):
    t_len, d_model = x_in.shape
    a_w = d_model // 2
    c_w = d_model - a_w
    c8 = wing.shape[-1]
    p_c = N_DEV * c8

    dils = DILATIONS[1:]
    n_in = 3 * len(DILATIONS)

    def body(*refs):
        nat, res = refs[:3], refs[3:n_in]
        (pa_ref, dpc_ref, dxo_ref, x_ref, prm_ref, w_ref, qkg_ref, bd_ref,
         dx_ref, dproj_ref, st_ref, qs_ref) = refs[n_in:n_in + 12]
        sums, wcat, wsems = refs[n_in + 12:n_in + 15], refs[n_in + 15], refs[n_in + 16]

        @pl.when(pl.program_id(0) == 0)
        def _():
            st_ref[...] = jnp.zeros_like(st_ref)
            qs_ref[...] = jnp.zeros_like(qs_ref)
            _join_columns(w_ref, wcat, wsems)

        for t in range(3):
            _to_groups(nat[t][...].astype(F32), sums[t])
            for i, dil in enumerate(dils):
                _merge_residues(res[3 * i + t], sums[t], dil, tm, add=True)
        bdm = bd_ref[...]
        grads = [_from_groups(sums[0]), _from_groups(sums[1])]
        for i in range(2):
            t = pa_ref[:, i * a_w:(i + 1) * a_w].astype(F32)
            rstd = lax.rsqrt(_head_mean(t * t, bdm) + EPS)
            that = t * rstd
            qs_ref[i:i + 1, :] += jnp.sum(grads[i] * that, axis=0, keepdims=True)
            dhat = grads[i] * qkg_ref[i:i + 1, :]
            dt = rstd * (dhat - that * _head_mean(dhat * that, bdm))
            dproj_ref[:, i * a_w:(i + 1) * a_w] = dt.astype(BF16)
        dproj_ref[:, 2 * a_w:3 * a_w] = _from_groups(sums[2]).astype(BF16)
        dproj_ref[:, 3 * a_w:] = dpc_ref[...]
        dh = jnp.concatenate([_dot_nt(dproj_ref[rows, :], wcat[...]) for rows in _row_halves(tm)], axis=0)
        dx, dshift, dscale, dng = _ada_bwd(dh, x_ref[...], prm_ref)
        dx_ref[...] = dxo_ref[...] + dx
        st_ref[0:1, :] += dshift
        st_ref[1:2, :] += dscale
        st_ref[2:3, :] += dng

    ta = pl.BlockSpec((tm, a_w), lambda m: (m, 0))
    tile = pl.BlockSpec((tm, d_model), lambda m: (m, 0))
    return pl.pallas_call(
        body, name=f"mixer_bwd_in_{l}", grid=(t_len // tm,),
        in_specs=[ta] * 3 + [_res_spec(dil, tm, a_w) for dil in dils for _ in range(3)]
                 + [pl.BlockSpec((tm, 3 * a_w), lambda m: (m, 0)),
                             pl.BlockSpec((tm, 3 * c_w), lambda m: (m, 0)),
                             tile, tile, pl.BlockSpec((8, d_model), lambda m: (0, 0)),
                             ANY,
                             pl.BlockSpec((8, a_w), lambda m: (0, 0)),
                             pl.BlockSpec(bd.shape, lambda m: (0, 0))],
        out_specs=[tile, pl.BlockSpec((tm, p_c), lambda m: (m, 0)),
                   pl.BlockSpec((8, d_model), lambda m: (0, 0)), pl.BlockSpec((8, a_w), lambda m: (0, 0))],
        out_shape=[SDS((t_len, d_model), F32), SDS((t_len, p_c), BF16), SDS((8, d_model), F32), SDS((8, a_w), F32)],
        scratch_shapes=[_groups(tm, a_w)] * 3 + [pltpu.VMEM((d_model, p_c), BF16), pltpu.SemaphoreType.DMA((N_DEV,))],
        compiler_params=_params(1),
    )(*[t.reshape(t_len, a_w) for t in parts[0]], *[t for p in parts[1:] for t in p],
      proj, dpc, dxo, x_in, prm, wing, qkg, bd)


def _loss_head(x, target, tm):
    t_len, d_model = x.shape

    def body(x_ref, t_ref, dx_ref, loss_ref):
        @pl.when(pl.program_id(0) == 0)
        def _():
            loss_ref[...] = jnp.zeros_like(loss_ref)

        diff = x_ref[...] - t_ref[...]
        dx_ref[...] = diff * (1.0 / d_model)
        per_token = jnp.sum(diff * diff, axis=-1, keepdims=True) * (1.0 / d_model)
        loss_ref[...] += 0.5 * jnp.sum(per_token)

    tile = pl.BlockSpec((tm, d_model), lambda m: (m, 0))
    return pl.pallas_call(
        body, name="loss_head", grid=(t_len // tm,),
        in_specs=[tile, tile],
        out_specs=[tile, pl.BlockSpec((8, LANES), lambda m: (0, 0))],
        out_shape=[SDS((t_len, d_model), F32), SDS((8, LANES), F32)],
        compiler_params=_params(1),
    )(x, target)


def _row_tile(rows, cols):
    best = None
    for t in range(16, rows + 1, 16):
        if rows % t == 0 and t * cols * 4 <= (1 << 20):
            best = t
    return best if best is not None else rows


def _adamw(pieces, w, m, v, name):
    n_q = len(pieces)
    n_p, rows, cols = pieces[0].shape
    tr = _row_tile(rows, cols)
    n_i = rows // tr
    c1 = 1.0 - ADAM_B1 ** ADAM_STEP
    c2 = 1.0 - ADAM_B2 ** ADAM_STEP

    def body(*refs):
        p_refs = refs[:n_q]
        w_ref, m_ref, v_ref, g_ref, d_ref, nm_ref, nv_ref = refs[n_q:]
        for q in range(n_q):
            @pl.when(pl.program_id(0) == q)
            def _():
                g = p_refs[q][0].astype(F32)
                for i in range(1, n_p):
                    g = g + p_refs[q][i].astype(F32)
                g_ref[...] = g
                nm = ADAM_B1 * m_ref[...] + (1.0 - ADAM_B1) * g
                nv = ADAM_B2 * v_ref[...] + (1.0 - ADAM_B2) * (g * g)
                nm_ref[...] = nm
                nv_ref[...] = nv
                d_ref[...] = -ADAM_LR * ((nm / c1) / (jnp.sqrt(nv / c2) + ADAM_EPS) + ADAM_WD * w_ref[...])

    def piece_spec(q):
        return pl.BlockSpec((n_p, tr, cols), lambda a, i: (0, jnp.where(a == q, i, 0), 0))

    tile = pl.BlockSpec((tr, cols), lambda a, i: (a * n_i + i, 0))
    return pl.pallas_call(
        body, name=name, grid=(n_q, n_i),
        in_specs=[piece_spec(q) for q in range(n_q)] + [tile, tile, tile],
        out_specs=[tile] * 4,
        out_shape=[SDS((n_q * rows, cols), F32)] * 4,
        compiler_params=_params(2),
    )(*pieces, w, m, v)


def _pack(vecs):
    flat = jnp.concatenate([v.reshape(-1).astype(F32) for v in vecs])
    rows = -(-flat.shape[0] // (8 * LANES)) * 8
    return jnp.pad(flat, (0, rows * LANES - flat.shape[0])).reshape(rows, LANES)


def _unpack(packed, shapes):
    lead = packed.shape[:-2]
    flat = packed.reshape(lead + (-1,))
    out, off = [], 0
    for s in shapes:
        size = 1
        for d in s:
            size *= d
        out.append(flat[..., off:off + size].reshape(lead + tuple(s)))
        off += size
    return out


def kernel(x, c, w_ada, b_ada, norm_g, w_in, q_norm_g, k_norm_g, conv_w, conv_b, w_out, ffn_w1, ffn_w2, loss_target, m_w_ada, m_b_ada, m_norm_g, m_w_in, m_q_norm_g, m_k_norm_g, m_conv_w, m_conv_b, m_w_out, m_ffn_w1, m_ffn_w2, v_w_ada, v_b_ada, v_norm_g, v_w_in, v_q_norm_g, v_k_norm_g, v_conv_w, v_conv_b, v_w_out, v_ffn_w1, v_ffn_w2):
    _, t_len, d_model = x.shape
    n_layer = w_ada.shape[0]
    a_w = d_model // 2
    c_w = d_model - a_w
    n_head = a_w // HEAD_DIM
    ada_cols = w_ada.shape[-1]
    tm = min(TOKEN_TILE, t_len)
    tm_ffn = min(FFN_FWD_TOKEN_TILE, t_len)
    tm_ffn_bwd = min(FFN_BWD_TOKEN_TILE, t_len)
    me = _index(_my_place())
    x2 = x.reshape(t_len, d_model)
    target = loss_target.reshape(t_len, d_model)

    small_shapes = [(d_model,), norm_g.shape, conv_w.shape]
    gathered = _allgather_small(_pack([c, norm_g, conv_w]), "gather_small_inputs")
    c_all, ng_parts, cw_parts = _unpack(gathered, small_shapes)
    norm_g_full = jnp.moveaxis(ng_parts, 0, 2).reshape(n_layer, 3, d_model)
    conv_w_full = jnp.moveaxis(cw_parts, 0, 2).reshape(n_layer, 3, c_w)

    b_loc = lax.dynamic_slice_in_dim(b_ada, me * ada_cols, ada_cols, axis=1).reshape(n_layer, 1, ada_cols)
    mod_part, silu_c = _mod_part(c_all, w_ada, b_loc)
    mod_all = _allgather_small(_pack([mod_part]), "gather_mod")
    mod_all = _unpack(mod_all, [mod_part.shape])[0]
    mod_mine = lax.dynamic_index_in_dim(mod_all, me, axis=2, keepdims=False)
    mod = jnp.moveaxis(mod_mine, 0, 1).reshape(n_layer, 3, 3, d_model)

    def prm_of(l, sub):
        rows = jnp.stack([norm_g_full[l, sub], mod[l, sub, 0], mod[l, sub, 1], mod[l, sub, 2]])
        return jnp.pad(rows, ((0, 4), (0, 0)))

    prm = [[prm_of(l, sub) for sub in range(3)] for l in range(n_layer)]
    qkg = [jnp.pad(jnp.stack([jnp.tile(q_norm_g[l], n_head), jnp.tile(k_norm_g[l], n_head)]), ((0, 6), (0, 0)))
           for l in range(n_layer)]
    cprm = [jnp.pad(jnp.concatenate([conv_w_full[l], conv_b[l][None]]), ((0, 4), (0, 0))) for l in range(n_layer)]
    head_of = jnp.arange(min(MXU_WIDTH, a_w)) // HEAD_DIM
    bd = (head_of[:, None] == head_of[None, :]).astype(BF16)
    expand = (jnp.arange(LANES)[:, None] == (jnp.arange(a_w)[None, :] // HEAD_DIM) * (LANES // n_head)).astype(BF16)

    f8 = ffn_w1.shape[-1]
    r8 = ffn_w2.shape[-2]
    c8 = w_in.shape[-1]
    o8 = w_out.shape[-2]
    me_arr = jnp.reshape(me, (1,)).astype(jnp.int32)
    win_b = [_place_own(w_in, (l,), me_arr, f"own_w_in_{l}") for l in range(n_layer)]
    wout_b = [_place_own(w_out, (l,), me_arr, f"own_w_out_{l}") for l in range(n_layer)]
    w1_b = [[_place_own(ffn_w1, (l, s), me_arr, f"own_w1_{l}_{s}") for s in range(2)] for l in range(n_layer)]
    w2_b = [[_place_own(ffn_w2, (l, s), me_arr, f"own_w2_{l}_{s}") for s in range(2)] for l in range(n_layer)]
    w1_first, w2_first = _allgather_big([w1_b[0][0], w2_b[0][0]], "gather_first_weights")
    groups = [[win_b[0], wout_b[0]], [w1_b[0][1], w2_b[0][1]]]
    for l in range(1, n_layer):
        groups += [[w1_b[l][0], w2_b[l][0]], [win_b[l], wout_b[l]], [w1_b[l][1], w2_b[l][1]]]
    flat = [a for grp in groups for a in grp]
    w_sends, w_recvs, w_lands, w_token = _send_start(flat, None, w1_first, "weights_start")

    def gathered(gi, after):
        lo = sum(len(grp) for grp in groups[:gi])
        hi = lo + len(groups[gi])
        return _send_wait(w_lands[lo:hi], w_sends[lo:hi], w_recvs[lo:hi], after, f"weights_wait_{gi}")

    def chunked(w1g):
        return w1g.reshape(2, N_CHUNK, d_model, f8)

    saved = []
    xc = x2
    for l in range(n_layer):
        keep = {"x0": xc}
        if l == 0:
            w1g, w2g = w1_first, w2_first
            prm_first = prm[0][0] + w_token[0, 0]
        else:
            w1g, w2g = gathered(3 * l - 1, xc)
            prm_first = prm[l][0]
        keep["w_ffn0"] = (chunked(w1g), w2g)
        keep["act0"], keep["h0"], keep["y0"], xc = _ffn_fwd(xc, prm_first, *keep["w_ffn0"], l, 0, tm_ffn)
        keep["x1"] = xc
        wing, woutg = gathered(0 if l == 0 else 3 * l, xc)
        keep.update(wing=wing, woutg=woutg)
        proj, qkn, keep["h1"], ycv, *res = _mixer_in(xc, prm[l][1], wing, qkg[l], cprm[l], bd, l, tm)
        qkv = [(qkn[None], proj[None], 2)] + [(res[2 * i], res[2 * i + 1], 0) for i in range(len(DILATIONS) - 1)]
        branches = [_attn_fwd(*t, l) for t in qkv]
        cat, lse, *res = _attn_combine([b[0] for b in branches], [b[1] for b in branches], ycv, expand, tm, l)
        n_r = len(DILATIONS) - 1
        keep["ya"] = [cat[None]] + res[:n_r]
        keep["lse"] = [lse[None]] + res[n_r:]
        xc, keep["ymix"] = _mixer_out(cat, xc, prm[l][1], woutg, l, tm)
        keep.update(proj=proj, qkv=qkv, cat=cat)
        keep["x2"] = xc
        w1g, w2g = gathered(1 if l == 0 else 3 * l + 1, xc)
        keep["w_ffn2"] = (chunked(w1g), w2g)
        keep["act2"], keep["h2"], keep["y2"], xc = _ffn_fwd(xc, prm[l][2], *keep["w_ffn2"], l, 1, tm_ffn)
        saved.append(keep)

    dx, loss_blk = _loss_head(xc, target, tm)
    loss = lax.psum(loss_blk[0, 0], ("x", "y", "c"))

    tk = min(GRAD_TOKEN_TILE, t_len)
    stats =[[None] * 3 for _ in range(n_layer)]
    conv_stats = [None] * n_layer
    qk_stats = [None] * n_layer

    started = {}
    dummy = jnp.zeros((8, LANES), F32)

    def start_exchange(g, name):
        sends, recvs, thru, token = _send_start(
            [_place_own(g, None, me_arr, f"own_grad_{name}")], [g], dummy, f"grads_start_{name}")
        started[name] = (thru[0], thru[1], sends[0], recvs[0])
        return token

    def wait_exchange(names, after, name):
        ent = [started[k] for k in names]
        done = _send_wait([e[0] for e in ent] + [e[1] for e in ent], [e[2] for e in ent], [e[3] for e in ent],
                          after, name)
        return dict(zip(names, done))

    def ffn_backward(dxo, keep, l, sub, which, prm_dep):
        dx_in, da, dy, st = _ffn_bwd(dxo, keep[f"y{which}"], keep[f"x{which}"], prm_dep,
                                     keep[f"act{which}"], *keep[f"w_ffn{which}"], l, sub, tm_ffn_bwd)
        g_w2 = _matmul_tn(
            keep[f"act{which}"], dy,
            pl.BlockSpec((None, None, tk, f8), lambda p, k: (2, p, k, 0)),
            pl.BlockSpec((tk, d_model), lambda p, k: (k, 0)),
            SDS((N_CHUNK, f8, d_model), BF16), pl.BlockSpec((None, f8, d_model), lambda p, k: (p, 0, 0)),
            N_CHUNK, t_len, tk, f"grad_w2_{l}_{sub}", dummy)
        token = start_exchange(g_w2.reshape(N_DEV, r8, d_model), f"w2_{l}_{sub}")
        g_w1 = _matmul_tn(
            keep[f"h{which}"], da.reshape(N_DEV, t_len, f8),
            pl.BlockSpec((tk, d_model), lambda p, k: (k, 0)),
            pl.BlockSpec((None, tk, f8), lambda p, k: (p, k, 0)),
            SDS((N_DEV, d_model, f8), BF16), pl.BlockSpec((None, d_model, f8), lambda p, k: (p, 0, 0)),
            N_DEV, t_len, tk, f"grad_w1_{l}_{sub}", token)
        token = start_exchange(g_w1, f"w1_{l}_{sub}")
        return dx_in, st, token[0, 0]

    token = 0.0
    for l in reversed(range(n_layer)):
        keep = saved[l]
        wing, woutg = keep["wing"], keep["woutg"]
        dx, stats[l][2], token = ffn_backward(dx, keep, l, 1, 2, prm[l][2] + token)
        dya, dpc, dy, st_gate, conv_stats[l], *dya_res = _mixer_bwd_out(
            dx, keep["ymix"], prm[l][1] + token, woutg, keep["proj"], cprm[l], l, tm)
        dyas = [dya[None]] + dya_res
        g_wout = _matmul_tn(
            keep["cat"], dy,
            pl.BlockSpec((tk, d_model), lambda p, k: (k, 0)),
            pl.BlockSpec((tk, d_model), lambda p, k: (k, 0)),
            SDS((d_model, d_model), BF16), pl.BlockSpec((d_model, d_model), lambda p, k: (0, 0)),
            1, t_len, tk, f"grad_wout_{l}", dummy)
        token = start_exchange(g_wout.reshape(N_DEV, o8, d_model), f"wout_{l}")[0, 0]
        parts = [_attn_bwd(*keep["qkv"][i], dyas[i], keep["ya"][i], keep["lse"][i], l)
                 for i in range(len(DILATIONS))]
        dx, dproj, st_norm, qk_stats[l] = _mixer_bwd_in(
            parts, keep["proj"], dpc, dx, keep["x1"], prm[l][1] + token, wing, qkg[l], bd, l, tm)
        stats[l][1] = st_norm + st_gate
        g_win = _matmul_tn(
            keep["h1"], dproj,
            pl.BlockSpec((tk, d_model), lambda p, k: (k, 0)),
            pl.BlockSpec((tk, N_DEV // 2 * c8), lambda p, k: (k, p)),
            SDS((N_DEV, d_model, c8), BF16), pl.BlockSpec((N_DEV // 2, d_model, c8), lambda p, k: (p, 0, 0)),
            2, t_len, tk, f"grad_win_{l}", dummy)
        token = start_exchange(g_win, f"win_{l}")[0, 0]
        dx, stats[l][0], token = ffn_backward(dx, keep, l, 0, 0, prm[l][0] + token)
    grad_x = dx.reshape(x.shape)

    def update(pieces, w, m, v, name):
        shape = w.shape
        cols = shape[-1]
        outs = _adamw([p.reshape(p.shape[0], -1, cols) for p in pieces], w.reshape(-1, cols),
                      m.reshape(-1, cols), v.reshape(-1, cols), name)
        return [o.reshape(shape) for o in outs]

    last = "w1_0_0"
    lands = wait_exchange([k for k in started if k != last], dx, "grads_wait_early")
    res = {
        "w_in": update([lands[f"win_{l}"] for l in range(n_layer)], w_in, m_w_in, v_w_in, "adamw_w_in"),
        "w_out": update([lands[f"wout_{l}"] for l in range(n_layer)], w_out, m_w_out, v_w_out, "adamw_w_out"),
        "ffn_w2": update([lands[f"w2_{l}_{s}"] for l in range(n_layer) for s in range(2)],
                         ffn_w2, m_ffn_w2, v_ffn_w2, "adamw_ffn_w2"),
    }

    dmod = jnp.stack([jnp.stack([stats[l][sub][r] for sub in range(3) for r in (0, 1, 3)]) for l in range(n_layer)])
    dng = jnp.stack([jnp.stack([stats[l][sub][2] for sub in range(3)]) for l in range(n_layer)])
    dcw = jnp.stack([conv_stats[l][0:3] for l in range(n_layer)])
    dcb = jnp.stack([conv_stats[l][3] for l in range(n_layer)])
    dqk = jnp.stack([qk_stats[l][0:2].reshape(2, n_head, HEAD_DIM).sum(axis=1) for l in range(n_layer)])
    part_shapes = [dmod.shape, dng.shape, dcw.shape, dcb.shape, dqk.shape]
    parts_all = _allgather_small(_pack([dmod, dng, dcw, dcb, dqk]), "gather_small_grads")
    dmod_all = _unpack(parts_all, part_shapes)[0].reshape(N_DEV, n_layer, 9 * d_model)
    summed = _unpack(_sum_devices(parts_all), part_shapes)
    g_b_ada = summed[0].reshape(n_layer, 9 * d_model)
    ng_cols = norm_g.shape[-1]
    g_norm_g = lax.dynamic_slice_in_dim(summed[1], me * ng_cols, ng_cols, axis=2)
    cw_cols = conv_w.shape[-1]
    g_conv_w = lax.dynamic_slice_in_dim(summed[2], me * cw_cols, cw_cols, axis=2)
    g_conv_b = summed[3]
    g_q, g_k = summed[4][:, 0], summed[4][:, 1]
    dmod_loc = lax.dynamic_slice_in_dim(dmod_all, me * ada_cols, ada_cols, axis=2)
    g_w_ada = _w_ada_grad(silu_c.T, jnp.moveaxis(dmod_loc, 0, 1))
    res["w_ada"] = update([g_w_ada[None]], w_ada, m_w_ada, v_w_ada, "adamw_w_ada")

    names = ["b_ada", "norm_g", "q_norm_g", "k_norm_g", "conv_w", "conv_b"]
    g_small = [g_b_ada, g_norm_g, g_q, g_k, g_conv_w, g_conv_b]
    w_small = [b_ada, norm_g, q_norm_g, k_norm_g, conv_w, conv_b]
    m_small = [m_b_ada, m_norm_g, m_q_norm_g, m_k_norm_g, m_conv_w, m_conv_b]
    v_small = [v_b_ada, v_norm_g, v_q_norm_g, v_k_norm_g, v_conv_w, v_conv_b]
    outs = _adamw([_pack(g_small)[None]], _pack(w_small), _pack(m_small), _pack(v_small), "adamw_small")
    shapes = [w.shape for w in w_small]
    unpacked = [_unpack(o, shapes) for o in outs]
    for i, nme in enumerate(names):
        res[nme] = [unpacked[k][i] for k in range(4)]

    follows = (res["w_in"][1][0, :1, :1] + res["w_out"][1][0, :1, :1] + res["ffn_w2"][1][0, 0, :1, :1]
               + res["w_ada"][1][0, :1, :1] + outs[1][:1, :1])
    lands.update(wait_exchange([last], follows, "grads_wait_last"))
    res["ffn_w1"] = update([lands[f"w1_{l}_{s}"] for l in range(n_layer) for s in range(2)],
                           ffn_w1, m_ffn_w1, v_ffn_w1, "adamw_ffn_w1")

    order = ["w_ada", "b_ada", "norm_g", "w_in", "q_norm_g", "k_norm_g", "conv_w", "conv_b", "w_out", "ffn_w1", "ffn_w2"]
    return (loss, grad_x, *[res[n][0] for n in order], *[res[n][1] for n in order],
            *[res[n][2] for n in order], *[res[n][3] for n in order])
```

```python
import jax
import jax.numpy as jnp
from jax import lax
from jax.experimental import pallas as pl
from jax.experimental.pallas import tpu as pltpu

F32 = jnp.float32
BF16 = jnp.bfloat16
SDS = jax.ShapeDtypeStruct

N_DEV = 8
HEAD_DIM = 64
BLOCK = 128
DILATIONS = (1, 4, 16)
ATTN_BLOCKS_PER_STEP = 4
N_CHUNK = 4
EPS = 1e-6
NEG = -1e30
SM_SCALE = HEAD_DIM ** -0.5
LANES = 128
MXU_WIDTH = 256
TOKEN_TILE = 512
FFN_ROW_PART = 256
FFN_FWD_TOKEN_TILE = 256
FFN_BWD_TOKEN_TILE = 256
GRAD_TOKEN_TILE = 2048
VMEM_LIMIT_BYTES = 56 * 1024 * 1024

ADAM_LR = 0.001
ADAM_B1 = 0.9
ADAM_B2 = 0.999
ADAM_EPS = 1e-08
ADAM_WD = 0.01
ADAM_STEP = 10

MESH_ID = pl.DeviceIdType.MESH
ANY = pl.BlockSpec(memory_space=pl.ANY)
HBM_SPEC = pl.BlockSpec(memory_space=pltpu.HBM)
SEM_SPEC = pl.BlockSpec(memory_space=pltpu.SEMAPHORE)
SIDE_EFFECT = pltpu.SideEffectType.DATAFLOW_SIDE_EFFECTING


def _params(n_axes):
    return pltpu.CompilerParams(dimension_semantics=("arbitrary",) * n_axes,
                                vmem_limit_bytes=VMEM_LIMIT_BYTES)


def _dot(a, b):
    return jnp.dot(a, b, preferred_element_type=F32)


def _dot_nt(a, b):
    return lax.dot_general(a, b, (((1,), (1,)), ((), ())), preferred_element_type=F32)


def _dot_tn(a, b):
    return lax.dot_general(a, b, (((0,), (0,)), ((), ())), preferred_element_type=F32)


def _row_halves(n):
    return (slice(0, n // 2), slice(n // 2, n))


def _split_bf16(t):
    hi = t.astype(BF16)
    return hi, (t - hi.astype(F32)).astype(BF16)


def _my_place():
    x, y, c = lax.axis_index("x"), lax.axis_index("y"), lax.axis_index("c")
    return x, y, c


def _peer(place, k):
    x, y, c = place
    return ((1 - x) if k & 4 else x, (1 - y) if k & 2 else y, (1 - c) if k & 1 else c)


def _index(place):
    return 4 * place[0] + 2 * place[1] + place[2]


def _allgather_small(v, name):
    rows, cols = v.shape

    def body(x_ref, out_ref, send_sems, recv_sems, local_sem):
        me = _my_place()
        mine = pltpu.make_async_copy(x_ref, out_ref.at[_index(me)], local_sem)
        mine.start()
        sends = []
        for k in range(1, N_DEV):
            cp = pltpu.make_async_remote_copy(
                src_ref=x_ref, dst_ref=out_ref.at[_index(me)], send_sem=send_sems.at[k - 1],
                recv_sem=recv_sems.at[k - 1], device_id=_peer(me, k), device_id_type=MESH_ID)
            cp.start()
            sends.append(cp)
        for k in range(1, N_DEV):
            pltpu.make_async_remote_copy(
                src_ref=x_ref, dst_ref=out_ref.at[_index(_peer(me, k))], send_sem=send_sems.at[k - 1],
                recv_sem=recv_sems.at[k - 1], device_id=_peer(me, k), device_id_type=MESH_ID).wait_recv()
        for cp in sends:
            cp.wait_send()
        mine.wait()

    return pl.pallas_call(
        body, name=name,
        out_shape=SDS((N_DEV, rows, cols), v.dtype),
        in_specs=[pl.BlockSpec(memory_space=pltpu.VMEM)],
        out_specs=pl.BlockSpec(memory_space=pltpu.VMEM),
        scratch_shapes=[pltpu.SemaphoreType.DMA((N_DEV - 1,)), pltpu.SemaphoreType.DMA((N_DEV - 1,)),
                        pltpu.SemaphoreType.DMA],
    )(v)


def _allgather_big(lands, name):
    n = len(lands)

    def body(*refs):
        outs = refs[n:2 * n]
        send_sems, recv_sems = refs[2 * n:]
        x, y, c = _my_place()
        me, sib = (x, y, c), (x, y, 1 - c)
        chips = [(1 - x, y), (x, 1 - y), (1 - x, 1 - y)]

        def copy(i, k, block, to):
            dst = outs[i].at[_index(block)]
            return pltpu.make_async_remote_copy(
                src_ref=dst, dst_ref=dst, send_sem=send_sems.at[7 * i + k],
                recv_sem=recv_sems.at[7 * i + k], device_id=to, device_id_type=MESH_ID)

        first = []
        for i in range(n):
            first.append(copy(i, 0, me, sib))
            for j, chip in enumerate(chips):
                first.append(copy(i, 1 + j, me, (*chip, c)))
        for cp in first:
            cp.start()
        passed = []
        for i in range(n):
            for j, chip in enumerate(chips):
                copy(i, 1 + j, (*chip, c), me).wait_recv()
                fwd = copy(i, 4 + j, (*chip, c), sib)
                fwd.start()
                passed.append(fwd)
        for i in range(n):
            copy(i, 0, sib, me).wait_recv()
            for j, chip in enumerate(chips):
                copy(i, 4 + j, (*chip, 1 - c), me).wait_recv()
        for cp in first + passed:
            cp.wait_send()

    return pl.pallas_call(
        body, name=name,
        out_shape=[SDS(a.shape, a.dtype) for a in lands],
        in_specs=[ANY] * n, out_specs=[ANY] * n,
        input_output_aliases={i: i for i in range(n)},
        scratch_shapes=[pltpu.SemaphoreType.DMA((7 * n,)), pltpu.SemaphoreType.DMA((7 * n,))],
    )(*lands)


def _place_own(src, lead, me_arr, name):
    rows, cols = src.shape[-2:]
    tr = _row_tile(rows, cols)

    def body(me_ref, s_ref, o_ref):
        o_ref[...] = s_ref[...].astype(BF16)

    if lead is None:
        in_spec = pl.BlockSpec((None, tr, cols), lambda i, me_ref: (me_ref[0], i, 0))
    else:
        in_spec = pl.BlockSpec((None,) * len(lead) + (tr, cols), lambda i, me_ref: (*lead, i, 0))
    return pl.pallas_call(
        body, name=name,
        grid_spec=pltpu.PrefetchScalarGridSpec(
            num_scalar_prefetch=1, grid=(rows // tr,), in_specs=[in_spec],
            out_specs=pl.BlockSpec((None, tr, cols), lambda i, me_ref: (me_ref[0], i, 0))),
        out_shape=SDS((N_DEV, rows, cols), BF16),
        compiler_params=_params(1),
    )(me_arr, src)


def _send_start(lands, pieces, after, name):
    n = len(lands)
    arrs = list(lands) + list(pieces or [])
    n_a = len(arrs)

    def body(*refs):
        land_r, piece_r = refs[:n], refs[n:n_a]
        send, recv = refs[n_a + 1:n_a + 1 + n], refs[n_a + 1 + n:n_a + 1 + 2 * n]
        token = refs[-1]
        me = _my_place()
        for i in range(n):
            for k in range(1, N_DEV):
                peer = _peer(me, k)
                src = piece_r[i].at[_index(peer)] if pieces else land_r[i].at[_index(me)]
                pltpu.make_async_remote_copy(
                    src_ref=src, dst_ref=land_r[i].at[_index(me)], send_sem=send[i], recv_sem=recv[i],
                    device_id=peer, device_id_type=MESH_ID).start()
        token[...] = jnp.zeros_like(token)

    outs = pl.pallas_call(
        body, name=name,
        out_shape=[pltpu.SemaphoreType.DMA(())] * (2 * n) + [pltpu.HBM(a.shape, a.dtype) for a in arrs]
        + [SDS((8, LANES), F32)],
        in_specs=[HBM_SPEC] * n_a + [ANY],
        out_specs=[SEM_SPEC] * (2 * n) + [HBM_SPEC] * n_a + [pl.BlockSpec(memory_space=pltpu.VMEM)],
        input_output_aliases={i: 2 * n + i for i in range(n_a)},
        compiler_params=pltpu.CompilerParams(has_side_effects=SIDE_EFFECT),
    )(*[pltpu.with_memory_space_constraint(a, pltpu.HBM) for a in arrs], after)
    return outs[:n], outs[n:2 * n], outs[2 * n:2 * n + n_a], outs[-1]


def _send_wait(thru, sends, recvs, after, name):
    n = len(sends)
    n_a = len(thru)

    def body(*refs):
        land_r = refs[:n]
        send, recv = refs[n_a:n_a + n], refs[n_a + n:n_a + 2 * n]
        me = _my_place()
        for i in range(n):
            seven = land_r[i].at[pl.ds(0, N_DEV - 1)]
            cp = pltpu.make_async_remote_copy(src_ref=seven, dst_ref=seven, send_sem=send[i], recv_sem=recv[i],
                                              device_id=_peer(me, 1), device_id_type=MESH_ID)
            cp.wait_send()
            cp.wait_recv()

    outs = pl.pallas_call(
        body, name=name,
        out_shape=[pltpu.HBM(a.shape, a.dtype) for a in thru],
        in_specs=[HBM_SPEC] * n_a + [SEM_SPEC] * (2 * n) + [ANY],
        out_specs=[HBM_SPEC] * n_a,
        input_output_aliases={i: i for i in range(n_a)},
        compiler_params=pltpu.CompilerParams(has_side_effects=SIDE_EFFECT),
    )(*thru, *sends, *recvs, after)
    return outs[:n]


def _ada_fwd(x, prm_ref):
    rstd = lax.rsqrt(jnp.mean(x * x, axis=-1, keepdims=True) + EPS)
    return (x * rstd * prm_ref[0:1, :]) * (1.0 + prm_ref[2:3, :]) + prm_ref[1:2, :]


def _ada_bwd(dh, x, prm_ref):
    ng, ops = prm_ref[0:1, :], 1.0 + prm_ref[2:3, :]
    rstd = lax.rsqrt(jnp.mean(x * x, axis=-1, keepdims=True) + EPS)
    xhat = x * rstd
    dxhat = dh * (ops * ng)
    dx = rstd * (dxhat - xhat * jnp.mean(dxhat * xhat, axis=-1, keepdims=True))
    dshift = jnp.sum(dh, axis=0, keepdims=True)
    dscale = jnp.sum(dh * (xhat * ng), axis=0, keepdims=True)
    dng = jnp.sum(dh * (xhat * ops), axis=0, keepdims=True)
    return dx, dshift, dscale, dng


def _head_mean(t, bd):
    hi = t.astype(BF16)
    g = bd.shape[0]
    parts = [_dot(hi[:, i:i + g], bd) for i in range(0, t.shape[1], g)]
    return jnp.concatenate(parts, axis=1) * (1.0 / HEAD_DIM)


def _join_columns(w_hbm, dst_ref, sems):
    cols = w_hbm.shape[2]
    copies = [pltpu.make_async_copy(w_hbm.at[j], dst_ref.at[:, pl.ds(j * cols, cols)], sems.at[j])
              for j in range(w_hbm.shape[0])]
    for cp in copies:
        cp.start()
    for cp in copies:
        cp.wait()


def _shift_rows(z, before, tm):
    row = lax.broadcasted_iota(jnp.int32, z.shape, 0)
    z1 = jnp.where(row == 0, before[7:8, :], pltpu.roll(z, 1, 0))
    z2 = jnp.where(row == 0, before[6:7, :], jnp.where(row == 1, before[7:8, :], pltpu.roll(z, 2, 0)))
    return z1, z2


def _mod_part(c_all, w_ada, b_loc):
    n_layer, d_model, cols = w_ada.shape

    def body(c_ref, w_ref, b_ref, out_ref, sc_ref):
        cv = c_ref[...]
        sc = cv * jax.nn.sigmoid(cv)
        sc_ref[...] = sc
        a_hi, a_lo = _split_bf16(sc)
        w_hi, w_lo = _split_bf16(w_ref[...])
        out_ref[...] = _dot(a_hi, w_hi) + _dot(a_hi, w_lo) + _dot(a_lo, w_hi) + b_ref[...]

    return pl.pallas_call(
        body, name="mod_part", grid=(n_layer,),
        in_specs=[pl.BlockSpec((N_DEV, d_model), lambda l: (0, 0)),
                  pl.BlockSpec((None, d_model, cols), lambda l: (l, 0, 0)),
                  pl.BlockSpec((None, 1, cols), lambda l: (l, 0, 0))],
        out_specs=[pl.BlockSpec((None, N_DEV, cols), lambda l: (l, 0, 0)),
                   pl.BlockSpec((N_DEV, d_model), lambda l: (0, 0))],
        out_shape=[SDS((n_layer, N_DEV, cols), F32), SDS((N_DEV, d_model), F32)],
        compiler_params=_params(1),
    )(c_all, w_ada, b_loc)


def _w_ada_grad(sc_t, dmod_loc):
    d_model = sc_t.shape[0]
    n_layer, _, cols = dmod_loc.shape

    def body(s_ref, d_ref, out_ref):
        acc = s_ref[:, 0:1] * d_ref[0:1, :]
        for b in range(1, N_DEV):
            acc = acc + s_ref[:, b:b + 1] * d_ref[b:b + 1, :]
        out_ref[...] = acc

    return pl.pallas_call(
        body, name="w_ada_grad", grid=(n_layer,),
        in_specs=[pl.BlockSpec((d_model, N_DEV), lambda l: (0, 0)),
                  pl.BlockSpec((None, N_DEV, cols), lambda l: (l, 0, 0))],
        out_specs=pl.BlockSpec((None, d_model, cols), lambda l: (l, 0, 0)),
        out_shape=SDS((n_layer, d_model, cols), F32),
        compiler_params=_params(1),
    )(sc_t, dmod_loc)


def _sum_devices(g):
    _, rows, cols = g.shape

    def body(g_ref, out_ref):
        acc = g_ref[0]
        for i in range(1, N_DEV):
            acc = acc + g_ref[i]
        out_ref[...] = acc

    return pl.pallas_call(body, name="sum_devices", out_shape=SDS((rows, cols), F32))(g)


def _ffn_fwd(x, prm, w1r, w2g, l, sub, tm):
    t_len, d_model = x.shape
    f8, r8 = w1r.shape[-1], w2g.shape[-2]
    n_m = t_len // tm

    def body(x_ref, xn_ref, prm_ref, w1_ref, w2_ref, act_ref, h_ref, y_ref, xo_ref, h_sc):
        @pl.when(pl.program_id(0) == 0)
        def _():
            h_sc[...] = _ada_fwd(x_ref[...], prm_ref).astype(BF16)

        h = h_sc[...]
        h_ref[...] = h

        def first(j):
            return _dot(h, w1_ref[0, j]), _dot(h, w1_ref[1, j])

        gu = first(0)
        yv = None
        for j in range(N_CHUNK):
            g, u = gu
            if j + 1 < N_CHUNK:
                gu = first(j + 1)
            else:
                h_next = _ada_fwd(xn_ref[...], prm_ref).astype(BF16)
            sig = jax.nn.sigmoid(g)
            sil = g * sig
            s = (sil * u).astype(BF16)
            act_ref[0, j] = (u * (sig * (1.0 + g * (1.0 - sig)))).astype(BF16)
            act_ref[1, j] = sil.astype(BF16)
            act_ref[2, j] = s
            part = _dot(s, w2_ref[2 * j:2 * j + 2].reshape(2 * r8, d_model))
            yv = part if yv is None else yv + part
        y_ref[...] = yv.astype(BF16)
        xo_ref[...] = x_ref[...] + (0.5 * prm_ref[3:4, :]) * yv
        h_sc[...] = h_next

    tile = pl.BlockSpec((tm, d_model), lambda m: (m, 0))
    return pl.pallas_call(
        body, name=f"ffn_fwd_{l}_{sub}", grid=(n_m,),
        in_specs=[tile, pl.BlockSpec((tm, d_model), lambda m: (jnp.minimum(m + 1, n_m - 1), 0)),
                  pl.BlockSpec((8, d_model), lambda m: (0, 0)),
                  pl.BlockSpec((2, N_CHUNK, d_model, f8), lambda m: (0, 0, 0, 0), pipeline_mode=pl.Buffered(1)),
                  pl.BlockSpec((N_DEV, r8, d_model), lambda m: (0, 0, 0), pipeline_mode=pl.Buffered(1))],
        out_specs=[pl.BlockSpec((3, N_CHUNK, tm, f8), lambda m: (0, 0, m, 0)), tile, tile, tile],
        out_shape=[SDS((3, N_CHUNK, t_len, f8), BF16), SDS((t_len, d_model), BF16),
                   SDS((t_len, d_model), BF16), SDS((t_len, d_model), F32)],
        scratch_shapes=[pltpu.VMEM((tm, d_model), BF16)],
        compiler_params=_params(1),
    )(x, x, prm, w1r, w2g)


def _ffn_bwd(dxo, y, x_in, prm, act, w1r, w2g, l, sub, tm):
    t_len, d_model = x_in.shape
    f8, r8 = w1r.shape[-1], w2g.shape[-2]
    n_m = t_len // tm

    def body(dxo_ref, y_ref, x_ref, prm_ref, act_ref, w1_ref, w2_ref, dx_ref, da_ref, dy_ref, st_ref):
        @pl.when(pl.program_id(0) == 0)
        def _():
            st_ref[...] = jnp.zeros_like(st_ref)

        dxo_v = dxo_ref[...]
        dy = ((0.5 * prm_ref[3:4, :]) * dxo_v).astype(BF16)
        dy_ref[...] = dy
        st_ref[3:4, :] += jnp.sum(0.5 * dxo_v * y_ref[...].astype(F32), axis=0, keepdims=True)

        def first(j):
            return _dot_nt(dy, w2_ref[2 * j:2 * j + 2].reshape(2 * r8, d_model))

        ds = first(0)
        dh = None
        for j in range(N_CHUNK):
            ds_now = ds
            if j + 1 < N_CHUNK:
                ds = first(j + 1)
            dg = (ds_now * act_ref[0, j].astype(F32)).astype(BF16)
            du = (ds_now * act_ref[1, j].astype(F32)).astype(BF16)
            da_ref[0, j] = dg
            da_ref[1, j] = du
            part = _dot_nt(dg, w1_ref[0, j]) + _dot_nt(du, w1_ref[1, j])
            dh = part if dh is None else dh + part
        dx, dshift, dscale, dng = _ada_bwd(dh, x_ref[...], prm_ref)
        dx_ref[...] = dxo_v + dx
        st_ref[0:1, :] += dshift
        st_ref[1:2, :] += dscale
        st_ref[2:3, :] += dng

    tile = pl.BlockSpec((tm, d_model), lambda m: (m, 0))
    fixed = pl.BlockSpec((8, d_model), lambda m: (0, 0))
    return pl.pallas_call(
        body, name=f"ffn_bwd_{l}_{sub}", grid=(n_m,),
        in_specs=[tile, tile, tile, fixed,
                  pl.BlockSpec((2, N_CHUNK, tm, f8), lambda m: (0, 0, m, 0)),
                  pl.BlockSpec((2, N_CHUNK, d_model, f8), lambda m: (0, 0, 0, 0), pipeline_mode=pl.Buffered(1)),
                  pl.BlockSpec((N_DEV, r8, d_model), lambda m: (0, 0, 0), pipeline_mode=pl.Buffered(1))],
        out_specs=[tile, pl.BlockSpec((2, N_CHUNK, tm, f8), lambda m: (0, 0, m, 0)), tile, fixed],
        out_shape=[SDS((t_len, d_model), F32), SDS((2, N_CHUNK, t_len, f8), BF16),
                   SDS((t_len, d_model), BF16), SDS((8, d_model), F32)],
        compiler_params=_params(1),
    )(dxo, y, x_in, prm, act, w1r, w2g)


def _matmul_tn(a, b, a_spec, b_spec, out_shape, out_spec, n_piece, t_len, tk, name, after, resident=None):
    def body(a_ref, b_ref, after_ref, o_ref, acc):
        k = pl.program_id(1)

        @pl.when(k == 0)
        def _():
            acc[...] = jnp.zeros_like(acc)

        rows_k = pl.ds(pl.multiple_of(k * tk, tk), tk)
        a_t = a_ref.at[rows_k] if resident == "a" else a_ref
        b_t = b_ref.at[rows_k] if resident == "b" else b_ref
        m_out, n_out = acc.shape
        if (m_out // 2) % LANES == 0:
            for rows in _row_halves(m_out):
                acc[rows, :] += _dot_tn(a_t[:, rows], b_t[...])
        else:
            for cols in _row_halves(n_out):
                acc[:, cols] += _dot_tn(a_t[...], b_t[:, cols])

        @pl.when(k == pl.num_programs(1) - 1)
        def _():
            if len(o_ref.shape) == 3:
                cols = o_ref.shape[2]
                for q in range(o_ref.shape[0]):
                    o_ref[q] = acc[:, q * cols:(q + 1) * cols].astype(o_ref.dtype)
            else:
                o_ref[...] = acc[...].astype(o_ref.dtype)

    blk = tuple(d for d in out_spec.block_shape if d is not None)
    acc_shape = blk if len(blk) == 2 else (blk[1], blk[0] * blk[2])
    if resident == "a":
        a_spec = pl.BlockSpec(a.shape, lambda p, k: (0, 0), pipeline_mode=pl.Buffered(1))
    if resident == "b":
        b_spec = pl.BlockSpec(b.shape, lambda p, k: (0, 0), pipeline_mode=pl.Buffered(1))
    return pl.pallas_call(
        body, name=name, grid=(n_piece, t_len // tk),
        in_specs=[a_spec, b_spec, ANY], out_specs=out_spec, out_shape=out_shape,
        scratch_shapes=[pltpu.VMEM(acc_shape, F32)],
        compiler_params=_params(2),
    )(a, b, after)


def _groups(tm, width):
    return pltpu.VMEM((width // LANES, tm, LANES), F32)


def _to_groups(val, dst_ref, first=0):
    for g in range(val.shape[1] // LANES):
        dst_ref[first + g] = val[:, g * LANES:(g + 1) * LANES]


def _from_groups(src_ref):
    return jnp.concatenate([src_ref[g] for g in range(src_ref.shape[0])], axis=1)


def _split_residues(src_ref, out_ref, dil, tm):
    for g in range(src_ref.shape[0]):
        for r in range(dil):
            out_ref[r, :, g * LANES:(g + 1) * LANES] = (
                src_ref.at[g][pl.ds(r, tm // dil, stride=dil), :].astype(out_ref.dtype))


def _merge_residues(in_ref, dst_ref, dil, tm, add=False):
    for g in range(dst_ref.shape[0]):
        for r in range(dil):
            rows = pl.ds(r, tm // dil, stride=dil)
            val = in_ref[r, :, g * LANES:(g + 1) * LANES].astype(F32)
            dst = dst_ref.at[g]
            dst[rows, :] = (dst[rows, :] + val) if add else val


def _res_spec(dil, tm, width, index=lambda m: m):
    return pl.BlockSpec((dil, tm // dil, width), lambda m: (0, index(m), 0))


def _mixer_in(x, prm, wing, qkg, cprm, bd, l, tm):
    t_len, d_model = x.shape
    a_w = d_model // 2
    c_w = d_model - a_w
    c8 = wing.shape[-1]
    p_c = N_DEV * c8
    dils = DILATIONS[1:]

    def body(x_ref, prm_ref, w_ref, qkg_ref, cprm_ref, bd_ref, proj_ref, qkn_ref, h_ref, ycv_ref, *rest):
        res_refs, (pf, qsc, vsc, carry, wcat, wsems) = rest[:2 * len(dils)], rest[2 * len(dils):]

        @pl.when(pl.program_id(0) == 0)
        def _():
            carry[...] = jnp.zeros_like(carry)
            _join_columns(w_ref, wcat, wsems)

        h = _ada_fwd(x_ref[...], prm_ref).astype(BF16)
        h_ref[...] = h
        for cols in _row_halves(p_c):
            pf[:, cols] = _dot(h, wcat[:, cols])
        proj_ref[...] = pf[...].astype(BF16)
        bdm = bd_ref[...]
        for i in range(2):
            t = pf[:, i * a_w:(i + 1) * a_w]
            rstd = lax.rsqrt(_head_mean(t * t, bdm) + EPS)
            tn = t * rstd * qkg_ref[i:i + 1, :]
            qkn_ref[:, i * a_w:(i + 1) * a_w] = tn.astype(BF16)
            _to_groups(tn, qsc, i * (a_w // LANES))
        _to_groups(pf[:, 2 * a_w:3 * a_w], vsc)
        for i, dil in enumerate(dils):
            _split_residues(qsc, res_refs[2 * i], dil, tm)
            _split_residues(vsc, res_refs[2 * i + 1], dil, tm)
        gb = pf[:, 3 * a_w:3 * a_w + c_w]
        z = pf[:, 3 * a_w + c_w:3 * a_w + 2 * c_w] * pf[:, 3 * a_w + 2 * c_w:3 * a_w + 3 * c_w]
        z1, z2 = _shift_rows(z, carry[...], tm)
        cv = cprm_ref[0:1, :] * z2 + cprm_ref[1:2, :] * z1 + cprm_ref[2:3, :] * z + cprm_ref[3:4, :]
        ycv_ref[...] = (gb * cv).astype(BF16)
        carry[...] = z[tm - 8:tm, :]

    res_specs, res_shapes = [], []
    for dil in dils:
        res_specs += [_res_spec(dil, tm, 2 * a_w), _res_spec(dil, tm, a_w)]
        res_shapes += [SDS((dil, t_len // dil, 2 * a_w), BF16), SDS((dil, t_len // dil, a_w), BF16)]
    return pl.pallas_call(
        body, name=f"mixer_in_{l}", grid=(t_len // tm,),
        in_specs=[pl.BlockSpec((tm, d_model), lambda m: (m, 0)),
                  pl.BlockSpec((8, d_model), lambda m: (0, 0)),
                  ANY,
                  pl.BlockSpec((8, a_w), lambda m: (0, 0)),
                  pl.BlockSpec((8, c_w), lambda m: (0, 0)),
                  pl.BlockSpec(bd.shape, lambda m: (0, 0))],
        out_specs=[pl.BlockSpec((tm, p_c), lambda m: (m, 0)),
                   pl.BlockSpec((tm, 2 * a_w), lambda m: (m, 0)),
                   pl.BlockSpec((tm, d_model), lambda m: (m, 0)),
                   pl.BlockSpec((tm, c_w), lambda m: (m, 0))] + res_specs,
        out_shape=[SDS((t_len, p_c), BF16), SDS((t_len, 2 * a_w), BF16),
                   SDS((t_len, d_model), BF16), SDS((t_len, c_w), BF16)] + res_shapes,
        scratch_shapes=[pltpu.VMEM((tm, p_c), F32), _groups(tm, 2 * a_w), _groups(tm, a_w),
                        pltpu.VMEM((8, c_w), F32), pltpu.VMEM((d_model, p_c), BF16),
                        pltpu.SemaphoreType.DMA((N_DEV,))],
        compiler_params=_params(1),
    )(x, prm, wing, qkg, cprm, bd)


def _band_mask(first_block):
    qi = lax.broadcasted_iota(jnp.int32, (BLOCK, 2 * BLOCK), 0)
    kj = lax.broadcasted_iota(jnp.int32, (BLOCK, 2 * BLOCK), 1)
    lowest = jnp.where(first_block, BLOCK, 0)
    return (kj >= qi) & (kj <= qi + BLOCK) & (kj >= lowest)


def _blocks_per_step(seq):
    n_sb = min(ATTN_BLOCKS_PER_STEP, seq // BLOCK)
    assert seq % (n_sb * BLOCK) == 0
    return n_sb


def _block_rows(sb):
    return slice(sb * BLOCK, (sb + 1) * BLOCK)


def _with_previous(prev_ref, cur_ref, sb, sl):
    before = prev_ref[:, sl] if sb == 0 else cur_ref[_block_rows(sb - 1), sl]
    return jnp.concatenate([before, cur_ref[_block_rows(sb), sl]], axis=0)


def _heads_on_lanes(cols):
    width = LANES // len(cols)
    lane = lax.broadcasted_iota(jnp.int32, (BLOCK, LANES), 1)
    out = jnp.broadcast_to(cols[-1], (BLOCK, LANES))
    for h in range(len(cols) - 2, -1, -1):
        out = jnp.where(lane < (h + 1) * width, cols[h], out)
    return out


def _attn_fwd(qk, v, v_col, l):
    dil, seq, a2 = qk.shape
    a_w = a2 // 2
    n_sb = _blocks_per_step(seq)
    rows_step = n_sb * BLOCK
    n_pair = a_w // LANES

    def body(q_ref, kc_ref, vc_ref, o_ref, lse_ref, kp_ref, vp_ref):
        @pl.when(pl.program_id(1) == 0)
        def _():
            kp_ref[...] = jnp.zeros_like(kp_ref)
            vp_ref[...] = jnp.zeros_like(vp_ref)

        masks = [_band_mask(pl.program_id(1) == 0)] + [_band_mask(False)] * (n_sb - 1)
        low = lax.broadcasted_iota(jnp.int32, (BLOCK, LANES), 1) < HEAD_DIM
        units = [(sb, slice(p * LANES, (p + 1) * LANES)) for sb in range(n_sb) for p in range(a_w // LANES)]
        scores = []
        for sb, sl in units:
            qp = q_ref[_block_rows(sb), sl]
            k2 = _with_previous(kp_ref, kc_ref, sb, sl)
            scores.append([_dot_nt(jnp.where(half, qp, jnp.zeros_like(qp)), k2) for half in (low, ~low)])
        probs, lses = [], []
        for (sb, sl), pair in zip(units, scores):
            pr, ls = [], []
            for s in pair:
                s = jnp.where(masks[sb], s * SM_SCALE, NEG)
                mx = jnp.max(s, axis=-1, keepdims=True)
                e = jnp.exp(s - mx)
                den = jnp.sum(e, axis=-1, keepdims=True)
                pr.append((e * (1.0 / den)).astype(BF16))
                ls.append(mx + jnp.log(den))
            probs.append(pr)
            lses.append(ls)
        for (sb, sl), pr, ls in zip(units, probs, lses):
            v2 = _with_previous(vp_ref, vc_ref, sb, sl)
            o_ref[_block_rows(sb), sl] = jnp.where(low, _dot(pr[0], v2), _dot(pr[1], v2)).astype(BF16)
        for sb in range(n_sb):
            cols = [c for ls in lses[sb * n_pair:(sb + 1) * n_pair] for c in ls]
            lse_ref[_block_rows(sb), :] = _heads_on_lanes(cols)
        kp_ref[...] = kc_ref[_block_rows(n_sb - 1), :]
        vp_ref[...] = vc_ref[_block_rows(n_sb - 1), :]

    def blk(f):
        return pl.BlockSpec((None, rows_step, a_w), f)

    return pl.pallas_call(
        body, name=f"attn_fwd_{l}_d{dil}", grid=(dil, seq // rows_step),
        in_specs=[blk(lambda r, b: (r, b, 0)),
                  blk(lambda r, b: (r, b, 1)),
                  blk(lambda r, b: (r, b, v_col))],
        out_specs=[blk(lambda r, b: (r, b, 0)), pl.BlockSpec((None, rows_step, LANES), lambda r, b: (r, b, 0))],
        out_shape=[SDS((dil, seq, a_w), BF16), SDS((dil, seq, LANES), F32)],
        scratch_shapes=[pltpu.VMEM((BLOCK, a_w), BF16), pltpu.VMEM((BLOCK, a_w), BF16)],
        compiler_params=_params(2),
    )(qk, qk, v)


def _attn_combine(outs, lses, ycv, expand, tm, l):
    t_len, c_w = ycv.shape
    a_w = outs[0].shape[-1]
    dils = DILATIONS[1:]
    n_r = len(dils)

    def body(o1, l1, *rest):
        o_res, l_res, y_ref, ex_ref = rest[:n_r], rest[n_r:2 * n_r], rest[2 * n_r], rest[2 * n_r + 1]
        cat_ref, lse_ref = rest[2 * n_r + 2], rest[2 * n_r + 3]
        ya_res, lse_res = rest[2 * n_r + 4:3 * n_r + 4], rest[3 * n_r + 4:4 * n_r + 4]
        scr = rest[4 * n_r + 4:]
        so, sl, sya, slse = scr[:n_r], scr[n_r:2 * n_r], scr[2 * n_r], scr[2 * n_r + 1]
        for i, dil in enumerate(dils):
            _merge_residues(o_res[i], so[i], dil, tm)
            _merge_residues(l_res[i], sl[i], dil, tm)
        ls = [l1[...]] + [s[0] for s in sl]
        mx = jnp.maximum(jnp.maximum(ls[0], ls[1]), ls[2])
        es = [jnp.exp(t - mx) for t in ls]
        den = es[0] + es[1] + es[2]
        inv = 1.0 / den
        slse[0] = mx + jnp.log(den)
        lse_ref[...] = slse[0]
        ex = ex_ref[...]
        ov = [o1[...].astype(F32)] + [_from_groups(s) for s in so]
        ya = jnp.zeros((tm, a_w), F32)
        for e, o in zip(es, ov):
            hi, lo = _split_bf16(e * inv)
            ya = ya + (_dot(hi, ex) + _dot(lo, ex)) * o
        _to_groups(ya, sya)
        cat_ref[:, :a_w] = ya.astype(BF16)
        cat_ref[:, a_w:] = y_ref[...]
        for i, dil in enumerate(dils):
            _split_residues(sya, ya_res[i], dil, tm)
            _split_residues(slse, lse_res[i], dil, tm)

    ta = pl.BlockSpec((tm, a_w), lambda m: (m, 0))
    tl = pl.BlockSpec((tm, LANES), lambda m: (m, 0))
    res_a = [_res_spec(dil, tm, a_w) for dil in dils]
    res_l = [_res_spec(dil, tm, LANES) for dil in dils]
    return pl.pallas_call(
        body, name=f"attn_combine_{l}", grid=(t_len // tm,),
        in_specs=[ta, tl] + res_a + res_l + [pl.BlockSpec((tm, c_w), lambda m: (m, 0)),
                                            pl.BlockSpec((LANES, a_w), lambda m: (0, 0))],
        out_specs=[pl.BlockSpec((tm, a_w + c_w), lambda m: (m, 0)), tl] + res_a + res_l,
        out_shape=[SDS((t_len, a_w + c_w), BF16), SDS((t_len, LANES), F32)]
        + [SDS((dil, t_len // dil, a_w), BF16) for dil in dils]
        + [SDS((dil, t_len // dil, LANES), F32) for dil in dils],
        scratch_shapes=[_groups(tm, a_w)] * n_r + [_groups(tm, LANES)] * n_r + [_groups(tm, a_w), _groups(tm, LANES)],
        compiler_params=_params(1),
    )(outs[0].reshape(t_len, a_w), lses[0].reshape(t_len, LANES), *outs[1:], *lses[1:], ycv, expand)


def _mixer_out(cat, x, prm, woutg, l, tm):
    t_len, d_model = x.shape
    r8 = woutg.shape[-2]

    def body(cat_ref, x_ref, prm_ref, w_ref, xo_ref, y_ref):
        w = w_ref[...].reshape(N_DEV * r8, d_model)
        for rows in _row_halves(tm):
            yv = _dot(cat_ref[rows, :], w)
            y_ref[rows, :] = yv.astype(BF16)
            xo_ref[rows, :] = x_ref[rows, :] + prm_ref[3:4, :] * yv

    tile = pl.BlockSpec((tm, d_model), lambda m: (m, 0))
    return pl.pallas_call(
        body, name=f"mixer_out_{l}", grid=(t_len // tm,),
        in_specs=[tile, tile, pl.BlockSpec((8, d_model), lambda m: (0, 0)),
                  pl.BlockSpec((N_DEV, r8, d_model), lambda m: (0, 0, 0))],
        out_specs=[tile, tile],
        out_shape=[SDS((t_len, d_model), F32), SDS((t_len, d_model), BF16)],
        compiler_params=_params(1),
    )(cat, x, prm, woutg)


def _mixer_bwd_out(dxo, ymix, prm, woutg, proj, cprm, l, tm):
    t_len, d_model = dxo.shape
    a_w = d_model // 2
    c_w = d_model - a_w
    r8 = woutg.shape[-2]
    n_m = t_len // tm
    hb = tm // 8

    dils = DILATIONS[1:]

    def body(dxo_ref, ym_ref, prm_ref, w_ref, pc_ref, halo_ref, cprm_ref,
             dya_ref, dpc_ref, dy_ref, st_ref, cs_ref, *rest):
        dya_res, (carry, dsc) = rest[:len(dils)], rest[len(dils):]
        step = pl.program_id(0)
        tile_i = n_m - 1 - step

        @pl.when(step == 0)
        def _():
            st_ref[...] = jnp.zeros_like(st_ref)
            cs_ref[...] = jnp.zeros_like(cs_ref)
            carry[...] = jnp.zeros_like(carry)

        dxo_v = dxo_ref[...]
        dy = (prm_ref[3:4, :] * dxo_v).astype(BF16)
        dy_ref[...] = dy
        st_ref[3:4, :] += jnp.sum(dxo_v * ym_ref[...].astype(F32), axis=0, keepdims=True)
        w = w_ref[...].reshape(N_DEV * r8, d_model)
        dcat = jnp.concatenate([_dot_nt(dy[rows, :], w) for rows in _row_halves(tm)], axis=0)
        _to_groups(dcat[:, :a_w], dsc)
        dya_ref[...] = dcat[:, :a_w].astype(BF16)
        for i, dil in enumerate(dils):
            _split_residues(dsc, dya_res[i], dil, tm)
        dyc = dcat[:, a_w:]
        gb = pc_ref[:, :c_w].astype(F32)
        gc = pc_ref[:, c_w:2 * c_w].astype(F32)
        u = pc_ref[:, 2 * c_w:].astype(F32)
        z = gc * u
        before = halo_ref[:, c_w:2 * c_w].astype(F32) * halo_ref[:, 2 * c_w:].astype(F32)
        before = jnp.where(tile_i > 0, before, jnp.zeros_like(before))
        z1, z2 = _shift_rows(z, before, tm)
        w0, w1, w2 = cprm_ref[0:1, :], cprm_ref[1:2, :], cprm_ref[2:3, :]
        cv = w0 * z2 + w1 * z1 + w2 * z + cprm_ref[3:4, :]
        dcv = dyc * gb
        cs_ref[0:1, :] += jnp.sum(dcv * z2, axis=0, keepdims=True)
        cs_ref[1:2, :] += jnp.sum(dcv * z1, axis=0, keepdims=True)
        cs_ref[2:3, :] += jnp.sum(dcv * z, axis=0, keepdims=True)
        cs_ref[3:4, :] += jnp.sum(dcv, axis=0, keepdims=True)
        row = lax.broadcasted_iota(jnp.int32, dcv.shape, 0)
        after = carry[...]
        d1 = jnp.where(row == tm - 1, after[0:1, :], pltpu.roll(dcv, tm - 1, 0))
        d2 = jnp.where(row == tm - 2, after[0:1, :],
                       jnp.where(row == tm - 1, after[1:2, :], pltpu.roll(dcv, tm - 2, 0)))
        dz = w2 * dcv + w1 * d1 + w0 * d2
        dpc_ref[:, :c_w] = (dyc * cv).astype(BF16)
        dpc_ref[:, c_w:2 * c_w] = (dz * u).astype(BF16)
        dpc_ref[:, 2 * c_w:] = (dz * gc).astype(BF16)
        carry[...] = dcv[0:8, :]

    def rev(width, col=0):
        return pl.BlockSpec((tm, width), lambda s: (n_m - 1 - s, col))

    fixed_d = pl.BlockSpec((8, d_model), lambda s: (0, 0))
    fixed_c = pl.BlockSpec((8, c_w), lambda s: (0, 0))
    return pl.pallas_call(
        body, name=f"mixer_bwd_out_{l}", grid=(n_m,),
        in_specs=[rev(d_model), rev(d_model), fixed_d,
                  pl.BlockSpec((N_DEV, r8, d_model), lambda s: (0, 0, 0)),
                  rev(3 * c_w, 1),
                  pl.BlockSpec((8, 3 * c_w), lambda s: (jnp.maximum((n_m - 1 - s) * hb - 1, 0), 1)),
                  fixed_c],
        out_specs=[rev(a_w), rev(3 * c_w), rev(d_model), fixed_d, fixed_c]
        + [_res_spec(dil, tm, a_w, lambda s: n_m - 1 - s) for dil in dils],
        out_shape=[SDS((t_len, a_w), BF16), SDS((t_len, 3 * c_w), BF16), SDS((t_len, d_model), BF16),
                   SDS((8, d_model), F32), SDS((8, c_w), F32)]
        + [SDS((dil, t_len // dil, a_w), BF16) for dil in dils],
        scratch_shapes=[pltpu.VMEM((8, c_w), F32), _groups(tm, a_w)],
        compiler_params=_params(1),
    )(dxo, ymix, prm, woutg, proj, proj, cprm)


def _attn_bwd(qk, v, v_col, dya, ya, lse, l):
    dil, seq, a2 = qk.shape
    a_w = a2 // 2
    n_sb = _blocks_per_step(seq)
    rows_step = n_sb * BLOCK
    n_step = seq // rows_step
    last = _block_rows(n_sb - 1)
    lse_w = LANES // (a_w // HEAD_DIM)
    single = n_step == 1

    def body(q_ref, kc_ref, vc_ref, do_ref, ya_ref, lse_ref,
             dq_ref, dk_ref, dv_ref, hold_k, hold_v, acc_k, acc_v, kp_ref, vp_ref):
        j = pl.program_id(1)

        @pl.when(j == 0)
        def _():
            kp_ref[...] = jnp.zeros_like(kp_ref)
            vp_ref[...] = jnp.zeros_like(vp_ref)

        @pl.when((pl.program_id(0) == 0) & (j == 0))
        def _():
            hold_k[...] = jnp.zeros_like(hold_k)
            hold_v[...] = jnp.zeros_like(hold_v)

        @pl.when(j < n_step)
        def _():
            masks = [_band_mask(j == 0)] + [_band_mask(False)] * (n_sb - 1)
            low = lax.broadcasted_iota(jnp.int32, (BLOCK, LANES), 1) < HEAD_DIM
            units = [(sb, slice(p * LANES, (p + 1) * LANES)) for sb in range(n_sb) for p in range(a_w // LANES)]
            acc_k[...] = jnp.zeros_like(acc_k)
            acc_v[...] = jnp.zeros_like(acc_v)
            heads, scores, dps = [], [], []
            for sb, sl in units:
                qp = q_ref[_block_rows(sb), sl]
                k2 = _with_previous(kp_ref, kc_ref, sb, sl)
                v2 = _with_previous(vp_ref, vc_ref, sb, sl)
                dop = do_ref[_block_rows(sb), sl]
                prod = dop.astype(F32) * ya_ref[_block_rows(sb), sl].astype(F32)
                lsep = lse_ref[_block_rows(sb), :]
                first = 2 * (sl.start // LANES) * lse_w
                for hh, half in enumerate((low, ~low)):
                    qh = jnp.where(half, qp, jnp.zeros_like(qp))
                    doh = jnp.where(half, dop, jnp.zeros_like(dop))
                    dsum = jnp.sum(jnp.where(half, prod, 0.0), axis=-1, keepdims=True)
                    heads.append((qh, doh, dsum, lsep[:, first + hh * lse_w:first + hh * lse_w + 1], masks[sb]))
                    scores.append(_dot_nt(qh, k2))
                    dps.append(_dot_nt(doh, v2))
            dss, prs = [], []
            for (qh, doh, dsum, lse_h, valid), s, dp in zip(heads, scores, dps):
                pr = jnp.exp(jnp.where(valid, s * SM_SCALE, NEG) - lse_h)
                dss.append((pr * (dp - dsum) * SM_SCALE).astype(BF16))
                prs.append(pr.astype(BF16))
            for i, (sb, sl) in enumerate(units):
                k2 = _with_previous(kp_ref, kc_ref, sb, sl)
                h0, h1 = heads[2 * i], heads[2 * i + 1]
                dq_ref[_block_rows(sb), sl] = jnp.where(low, _dot(dss[2 * i], k2),
                                                        _dot(dss[2 * i + 1], k2)).astype(BF16)
                two = slice(sb * BLOCK, (sb + 2) * BLOCK)
                acc_k[two, sl] += _dot_tn(dss[2 * i], h0[0]) + _dot_tn(dss[2 * i + 1], h1[0])
                acc_v[two, sl] += _dot_tn(prs[2 * i], h0[1]) + _dot_tn(prs[2 * i + 1], h1[1])
            for out_ref, hold, acc in ((dk_ref, hold_k, acc_k), (dv_ref, hold_v, acc_v)):
                if single:
                    out_ref[...] = acc[BLOCK:, :].astype(BF16)
                    continue
                if n_sb > 1:
                    out_ref[:rows_step - BLOCK, :] = hold[:rows_step - BLOCK, :].astype(BF16)
                out_ref[last, :] = (hold[last, :] + acc[:BLOCK, :]).astype(BF16)
                hold[...] = acc[BLOCK:, :]
            kp_ref[...] = kc_ref[last, :]
            vp_ref[...] = vc_ref[last, :]

        @pl.when(j == n_step)
        def _():
            dk_ref[...] = hold_k[...].astype(BF16)
            dv_ref[...] = hold_v[...].astype(BF16)

    def blk(f):
        return pl.BlockSpec((None, rows_step, a_w), f)

    def cur(j):
        return jnp.minimum(j, n_step - 1)

    def late(j):
        return jnp.maximum(j - 1, 0)

    return pl.pallas_call(
        body, name=f"attn_bwd_{l}_d{dil}", grid=(dil, 1 if single else n_step + 1),
        in_specs=[blk(lambda r, j: (r, cur(j), 0)),
                  blk(lambda r, j: (r, cur(j), 1)),
                  blk(lambda r, j: (r, cur(j), v_col)),
                  blk(lambda r, j: (r, cur(j), 0)),
                  blk(lambda r, j: (r, cur(j), 0)),
                  pl.BlockSpec((None, rows_step, LANES), lambda r, j: (r, cur(j), 0))],
        out_specs=[blk(lambda r, j: (r, cur(j), 0)), blk(lambda r, j: (r, late(j), 0)),
                   blk(lambda r, j: (r, late(j), 0))],
        out_shape=[SDS((dil, seq, a_w), BF16)] * 3,
        scratch_shapes=[pltpu.VMEM((rows_step, a_w), F32), pltpu.VMEM((rows_step, a_w), F32),
                        pltpu.VMEM((rows_step + BLOCK, a_w), F32), pltpu.VMEM((rows_step + BLOCK, a_w), F32),
                        pltpu.VMEM((BLOCK, a_w), BF16), pltpu.VMEM((BLOCK, a_w), BF16)],
        compiler_params=_params(2),
    )(qk, qk, v, dya, ya, lse)


def _mixer_bwd_in(parts, proj, dpc, dxo, x_in, prm, wing, qkg, bd, l, tm):
    t_len, d_model = x_in.shape
    a_w = d_model // 2
    c_w = d_model - a_w
    c8 = wing.shape[-1]
    p_c = N_DEV * c8

    dils = DILATIONS[1:]
    n_in = 3 * len(DILATIONS)

    def body(*refs):
        nat, res = refs[:3], refs[3:n_in]
        (pa_ref, dpc_ref, dxo_ref, x_ref, prm_ref, w_ref, qkg_ref, bd_ref,
         dx_ref, dproj_ref, st_ref, qs_ref) = refs[n_in:n_in + 12]
        sums, wcat, wsems = refs[n_in + 12:n_in + 15], refs[n_in + 15], refs[n_in + 16]

        @pl.when(pl.program_id(0) == 0)
        def _():
            st_ref[...] = jnp.zeros_like(st_ref)
            qs_ref[...] = jnp.zeros_like(qs_ref)
            _join_columns(w_ref, wcat, wsems)

        for t in range(3):
            _to_groups(nat[t][...].astype(F32), sums[t])
            for i, dil in enumerate(dils):
                _merge_residues(res[3 * i + t], sums[t], dil, tm, add=True)
        bdm = bd_ref[...]
        grads = [_from_groups(sums[0]), _from_groups(sums[1])]
        for i in range(2):
            t = pa_ref[:, i * a_w:(i + 1) * a_w].astype(F32)
            rstd = lax.rsqrt(_head_mean(t * t, bdm) + EPS)
            that = t * rstd
            qs_ref[i:i + 1, :] += jnp.sum(grads[i] * that, axis=0, keepdims=True)
            dhat = grads[i] * qkg_ref[i:i + 1, :]
            dt = rstd * (dhat - that * _head_mean(dhat * that, bdm))
            dproj_ref[:, i * a_w:(i + 1) * a_w] = dt.astype(BF16)
        dproj_ref[:, 2 * a_w:3 * a_w] = _from_groups(sums[2]).astype(BF16)
        dproj_ref[:, 3 * a_w:] = dpc_ref[...]
        dh = jnp.concatenate([_dot_nt(dproj_ref[rows, :], wcat[...]) for rows in _row_halves(tm)], axis=0)
        dx, dshift, dscale, dng = _ada_bwd(dh, x_ref[...], prm_ref)
        dx_ref[...] = dxo_ref[...] + dx
        st_ref[0:1, :] += dshift
        st_ref[1:2, :] += dscale
        st_ref[2:3, :] += dng

    ta = pl.BlockSpec((tm, a_w), lambda m: (m, 0))
    tile = pl.BlockSpec((tm, d_model), lambda m: (m, 0))
    return pl.pallas_call(
        body, name=f"mixer_bwd_in_{l}", grid=(t_len // tm,),
        in_specs=[ta] * 3 + [_res_spec(dil, tm, a_w) for dil in dils for _ in range(3)]
                 + [pl.BlockSpec((tm, 3 * a_w), lambda m: (m, 0)),
                             pl.BlockSpec((tm, 3 * c_w), lambda m: (m, 0)),
                             tile, tile, pl.BlockSpec((8, d_model), lambda m: (0, 0)),
                             ANY,
                             pl.BlockSpec((8, a_w), lambda m: (0, 0)),
                             pl.BlockSpec(bd.shape, lambda m: (0, 0))],
        out_specs=[tile, pl.BlockSpec((tm, p_c), lambda m: (m, 0)),
                   pl.BlockSpec((8, d_model), lambda m: (0, 0)), pl.BlockSpec((8, a_w), lambda m: (0, 0))],
        out_shape=[SDS((t_len, d_model), F32), SDS((t_len, p_c), BF16), SDS((8, d_model), F32), SDS((8, a_w), F32)],
        scratch_shapes=[_groups(tm, a_w)] * 3 + [pltpu.VMEM((d_model, p_c), BF16), pltpu.SemaphoreType.DMA((N_DEV,))],
        compiler_params=_params(1),
    )(*[t.reshape(t_len, a_w) for t in parts[0]], *[t for p in parts[1:] for t in p],
      proj, dpc, dxo, x_in, prm, wing, qkg, bd)


def _loss_head(x, target, tm):
    t_len, d_model = x.shape

    def body(x_ref, t_ref, dx_ref, loss_ref):
        @pl.when(pl.program_id(0) == 0)
        def _():
            loss_ref[...] = jnp.zeros_like(loss_ref)

        diff = x_ref[...] - t_ref[...]
        dx_ref[...] = diff * (1.0 / d_model)
        per_token = jnp.sum(diff * diff, axis=-1, keepdims=True) * (1.0 / d_model)
        loss_ref[...] += 0.5 * jnp.sum(per_token)

    tile = pl.BlockSpec((tm, d_model), lambda m: (m, 0))
    return pl.pallas_call(
        body, name="loss_head", grid=(t_len // tm,),
        in_specs=[tile, tile],
        out_specs=[tile, pl.BlockSpec((8, LANES), lambda m: (0, 0))],
        out_shape=[SDS((t_len, d_model), F32), SDS((8, LANES), F32)],
        compiler_params=_params(1),
    )(x, target)


def _row_tile(rows, cols):
    best = None
    for t in range(16, rows + 1, 16):
        if rows % t == 0 and t * cols * 4 <= (1 << 20):
            best = t
    return best if best is not None else rows


def _adamw(pieces, w, m, v, name):
    n_q = len(pieces)
    n_p, rows, cols = pieces[0].shape
    tr = _row_tile(rows, cols)
    n_i = rows // tr
    c1 = 1.0 - ADAM_B1 ** ADAM_STEP
    c2 = 1.0 - ADAM_B2 ** ADAM_STEP

    def body(*refs):
        p_refs = refs[:n_q]
        w_ref, m_ref, v_ref, g_ref, d_ref, nm_ref, nv_ref = refs[n_q:]
        for q in range(n_q):
            @pl.when(pl.program_id(0) == q)
            def _():
                g = p_refs[q][0].astype(F32)
                for i in range(1, n_p):
                    g = g + p_refs[q][i].astype(F32)
                g_ref[...] = g
                nm = ADAM_B1 * m_ref[...] + (1.0 - ADAM_B1) * g
                nv = ADAM_B2 * v_ref[...] + (1.0 - ADAM_B2) * (g * g)
                nm_ref[...] = nm
                nv_ref[...] = nv
                d_ref[...] = -ADAM_LR * ((nm / c1) / (jnp.sqrt(nv / c2) + ADAM_EPS) + ADAM_WD * w_ref[...])

    def piece_spec(q):
        return pl.BlockSpec((n_p, tr, cols), lambda a, i: (0, jnp.where(a == q, i, 0), 0))

    tile = pl.BlockSpec((tr, cols), lambda a, i: (a * n_i + i, 0))
    return pl.pallas_call(
        body, name=name, grid=(n_q, n_i),
        in_specs=[piece_spec(q) for q in range(n_q)] + [tile, tile, tile],
        out_specs=[tile] * 4,
        out_shape=[SDS((n_q * rows, cols), F32)] * 4,
        compiler_params=_params(2),
    )(*pieces, w, m, v)


def _pack(vecs):
    flat = jnp.concatenate([v.reshape(-1).astype(F32) for v in vecs])
    rows = -(-flat.shape[0] // (8 * LANES)) * 8
    return jnp.pad(flat, (0, rows * LANES - flat.shape[0])).reshape(rows, LANES)


def _unpack(packed, shapes):
    lead = packed.shape[:-2]
    flat = packed.reshape(lead + (-1,))
    out, off = [], 0
    for s in shapes:
        size = 1
        for d in s:
            size *= d
        out.append(flat[..., off:off + size].reshape(lead + tuple(s)))
        off += size
    return out


def kernel(x, c, w_ada, b_ada, norm_g, w_in, q_norm_g, k_norm_g, conv_w, conv_b, w_out, ffn_w1, ffn_w2, loss_target, m_w_ada, m_b_ada, m_norm_g, m_w_in, m_q_norm_g, m_k_norm_g, m_conv_w, m_conv_b, m_w_out, m_ffn_w1, m_ffn_w2, v_w_ada, v_b_ada, v_norm_g, v_w_in, v_q_norm_g, v_k_norm_g, v_conv_w, v_conv_b, v_w_out, v_ffn_w1, v_ffn_w2):
    _, t_len, d_model = x.shape
    n_layer = w_ada.shape[0]
    a_w = d_model // 2
    c_w = d_model - a_w
    n_head = a_w // HEAD_DIM
    ada_cols = w_ada.shape[-1]
    tm = min(TOKEN_TILE, t_len)
    tm_ffn = min(FFN_FWD_TOKEN_TILE, t_len)
    tm_ffn_bwd = min(FFN_BWD_TOKEN_TILE, t_len)
    me = _index(_my_place())
    x2 = x.reshape(t_len, d_model)
    target = loss_target.reshape(t_len, d_model)

    small_shapes = [(d_model,), norm_g.shape, conv_w.shape]
    gathered = _allgather_small(_pack([c, norm_g, conv_w]), "gather_small_inputs")
    c_all, ng_parts, cw_parts = _unpack(gathered, small_shapes)
    norm_g_full = jnp.moveaxis(ng_parts, 0, 2).reshape(n_layer, 3, d_model)
    conv_w_full = jnp.moveaxis(cw_parts, 0, 2).reshape(n_layer, 3, c_w)

    b_loc = lax.dynamic_slice_in_dim(b_ada, me * ada_cols, ada_cols, axis=1).reshape(n_layer, 1, ada_cols)
    mod_part, silu_c = _mod_part(c_all, w_ada, b_loc)
    mod_all = _allgather_small(_pack([mod_part]), "gather_mod")
    mod_all = _unpack(mod_all, [mod_part.shape])[0]
    mod_mine = lax.dynamic_index_in_dim(mod_all, me, axis=2, keepdims=False)
    mod = jnp.moveaxis(mod_mine, 0, 1).reshape(n_layer, 3, 3, d_model)

    def prm_of(l, sub):
        rows = jnp.stack([norm_g_full[l, sub], mod[l, sub, 0], mod[l, sub, 1], mod[l, sub, 2]])
        return jnp.pad(rows, ((0, 4), (0, 0)))

    prm = [[prm_of(l, sub) for sub in range(3)] for l in range(n_layer)]
    qkg = [jnp.pad(jnp.stack([jnp.tile(q_norm_g[l], n_head), jnp.tile(k_norm_g[l], n_head)]), ((0, 6), (0, 0)))
           for l in range(n_layer)]
    cprm = [jnp.pad(jnp.concatenate([conv_w_full[l], conv_b[l][None]]), ((0, 4), (0, 0))) for l in range(n_layer)]
    head_of = jnp.arange(min(MXU_WIDTH, a_w)) // HEAD_DIM
    bd = (head_of[:, None] == head_of[None, :]).astype(BF16)
    expand = (jnp.arange(LANES)[:, None] == (jnp.arange(a_w)[None, :] // HEAD_DIM) * (LANES // n_head)).astype(BF16)

    f8 = ffn_w1.shape[-1]
    r8 = ffn_w2.shape[-2]
    c8 = w_in.shape[-1]
    o8 = w_out.shape[-2]
    me_arr = jnp.reshape(me, (1,)).astype(jnp.int32)
    win_b = [_place_own(w_in, (l,), me_arr, f"own_w_in_{l}") for l in range(n_layer)]
    wout_b = [_place_own(w_out, (l,), me_arr, f"own_w_out_{l}") for l in range(n_layer)]
    w1_b = [[_place_own(ffn_w1, (l, s), me_arr, f"own_w1_{l}_{s}") for s in range(2)] for l in range(n_layer)]
    w2_b = [[_place_own(ffn_w2, (l, s), me_arr, f"own_w2_{l}_{s}") for s in range(2)] for l in range(n_layer)]
    w1_first, w2_first = _allgather_big([w1_b[0][0], w2_b[0][0]], "gather_first_weights")
    groups = [[win_b[0], wout_b[0]], [w1_b[0][1], w2_b[0][1]]]
    for l in range(1, n_layer):
        groups += [[w1_b[l][0], w2_b[l][0]], [win_b[l], wout_b[l]], [w1_b[l][1], w2_b[l][1]]]
    flat = [a for grp in groups for a in grp]
    w_sends, w_recvs, w_lands, w_token = _send_start(flat, None, w1_first, "weights_start")

    def gathered(gi, after):
        lo = sum(len(grp) for grp in groups[:gi])
        hi = lo + len(groups[gi])
        return _send_wait(w_lands[lo:hi], w_sends[lo:hi], w_recvs[lo:hi], after, f"weights_wait_{gi}")

    def chunked(w1g):
        return w1g.reshape(2, N_CHUNK, d_model, f8)

    saved = []
    xc = x2
    for l in range(n_layer):
        keep = {"x0": xc}
        if l == 0:
            w1g, w2g = w1_first, w2_first
            prm_first = prm[0][0] + w_token[0, 0]
        else:
            w1g, w2g = gathered(3 * l - 1, xc)
            prm_first = prm[l][0]
        keep["w_ffn0"] = (chunked(w1g), w2g)
        keep["act0"], keep["h0"], keep["y0"], xc = _ffn_fwd(xc, prm_first, *keep["w_ffn0"], l, 0, tm_ffn)
        keep["x1"] = xc
        wing, woutg = gathered(0 if l == 0 else 3 * l, xc)
        keep.update(wing=wing, woutg=woutg)
        proj, qkn, keep["h1"], ycv, *res = _mixer_in(xc, prm[l][1], wing, qkg[l], cprm[l], bd, l, tm)
        qkv = [(qkn[None], proj[None], 2)] + [(res[2 * i], res[2 * i + 1], 0) for i in range(len(DILATIONS) - 1)]
        branches = [_attn_fwd(*t, l) for t in qkv]
        cat, lse, *res = _attn_combine([b[0] for b in branches], [b[1] for b in branches], ycv, expand, tm, l)
        n_r = len(DILATIONS) - 1
        keep["ya"] = [cat[None]] + res[:n_r]
        keep["lse"] = [lse[None]] + res[n_r:]
        xc, keep["ymix"] = _mixer_out(cat, xc, prm[l][1], woutg, l, tm)
        keep.update(proj=proj, qkv=qkv, cat=cat)
        keep["x2"] = xc
        w1g, w2g = gathered(1 if l == 0 else 3 * l + 1, xc)
        keep["w_ffn2"] = (chunked(w1g), w2g)
        keep["act2"], keep["h2"], keep["y2"], xc = _ffn_fwd(xc, prm[l][2], *keep["w_ffn2"], l, 1, tm_ffn)
        saved.append(keep)

    dx, loss_blk = _loss_head(xc, target, tm)
    loss = lax.psum(loss_blk[0, 0], ("x", "y", "c"))

    tk = min(GRAD_TOKEN_TILE, t_len)
    stats =[[None] * 3 for _ in range(n_layer)]
    conv_stats = [None] * n_layer
    qk_stats = [None] * n_layer

    started = {}
    dummy = jnp.zeros((8, LANES), F32)

    def start_exchange(g, name):
        sends, recvs, thru, token = _send_start(
            [_place_own(g, None, me_arr, f"own_grad_{name}")], [g], dummy, f"grads_start_{name}")
        started[name] = (thru[0], thru[1], sends[0], recvs[0])
        return token

    def wait_exchange(names, after, name):
        ent = [started[k] for k in names]
        done = _send_wait([e[0] for e in ent] + [e[1] for e in ent], [e[2] for e in ent], [e[3] for e in ent],
                          after, name)
        return dict(zip(names, done))

    def ffn_backward(dxo, keep, l, sub, which, prm_dep):
        dx_in, da, dy, st = _ffn_bwd(dxo, keep[f"y{which}"], keep[f"x{which}"], prm_dep,
                                     keep[f"act{which}"], *keep[f"w_ffn{which}"], l, sub, tm_ffn_bwd)
        g_w2 = _matmul_tn(
            keep[f"act{which}"], dy,
            pl.BlockSpec((None, None, tk, f8), lambda p, k: (2, p, k, 0)),
            pl.BlockSpec((tk, d_model), lambda p, k: (k, 0)),
            SDS((N_CHUNK, f8, d_model), BF16), pl.BlockSpec((None, f8, d_model), lambda p, k: (p, 0, 0)),
            N_CHUNK, t_len, tk, f"grad_w2_{l}_{sub}", dummy, resident="b")
        token = start_exchange(g_w2.reshape(N_DEV, r8, d_model), f"w2_{l}_{sub}")
        g_w1 = _matmul_tn(
            keep[f"h{which}"], da.reshape(N_DEV, t_len, f8),
            pl.BlockSpec((tk, d_model), lambda p, k: (k, 0)),
            pl.BlockSpec((None, tk, f8), lambda p, k: (p, k, 0)),
            SDS((N_DEV, d_model, f8), BF16), pl.BlockSpec((None, d_model, f8), lambda p, k: (p, 0, 0)),
            N_DEV, t_len, tk, f"grad_w1_{l}_{sub}", token, resident="a")
        token = start_exchange(g_w1, f"w1_{l}_{sub}")
        return dx_in, st, token[0, 0]

    token = 0.0
    for l in reversed(range(n_layer)):
        keep = saved[l]
        wing, woutg = keep["wing"], keep["woutg"]
        dx, stats[l][2], token = ffn_backward(dx, keep, l, 1, 2, prm[l][2] + token)
        dya, dpc, dy, st_gate, conv_stats[l], *dya_res = _mixer_bwd_out(
            dx, keep["ymix"], prm[l][1] + token, woutg, keep["proj"], cprm[l], l, tm)
        dyas = [dya[None]] + dya_res
        g_wout = _matmul_tn(
            keep["cat"], dy,
            pl.BlockSpec((tk, d_model), lambda p, k: (k, 0)),
            pl.BlockSpec((tk, d_model), lambda p, k: (k, 0)),
            SDS((d_model, d_model), BF16), pl.BlockSpec((d_model, d_model), lambda p, k: (0, 0)),
            1, t_len, tk, f"grad_wout_{l}", dummy)
        token = start_exchange(g_wout.reshape(N_DEV, o8, d_model), f"wout_{l}")[0, 0]
        parts = [_attn_bwd(*keep["qkv"][i], dyas[i], keep["ya"][i], keep["lse"][i], l)
                 for i in range(len(DILATIONS))]
        dx, dproj, st_norm, qk_stats[l] = _mixer_bwd_in(
            parts, keep["proj"], dpc, dx, keep["x1"], prm[l][1] + token, wing, qkg[l], bd, l, tm)
        stats[l][1] = st_norm + st_gate
        g_win = _matmul_tn(
            keep["h1"], dproj,
            pl.BlockSpec((tk, d_model), lambda p, k: (k, 0)),
            pl.BlockSpec((tk, N_DEV // 2 * c8), lambda p, k: (k, p)),
            SDS((N_DEV, d_model, c8), BF16), pl.BlockSpec((N_DEV // 2, d_model, c8), lambda p, k: (p, 0, 0)),
            2, t_len, tk, f"grad_win_{l}", dummy)
        token = start_exchange(g_win, f"win_{l}")[0, 0]
        dx, stats[l][0], token = ffn_backward(dx, keep, l, 0, 0, prm[l][0] + token)
    grad_x = dx.reshape(x.shape)

    def update(pieces, w, m, v, name):
        shape = w.shape
        cols = shape[-1]
        outs = _adamw([p.reshape(p.shape[0], -1, cols) for p in pieces], w.reshape(-1, cols),
                      m.reshape(-1, cols), v.reshape(-1, cols), name)
        return [o.reshape(shape) for o in outs]

    last = "w1_0_0"
    lands = wait_exchange([k for k in started if k != last], dx, "grads_wait_early")
    res = {
        "w_in": update([lands[f"win_{l}"] for l in range(n_layer)], w_in, m_w_in, v_w_in, "adamw_w_in"),
        "w_out": update([lands[f"wout_{l}"] for l in range(n_layer)], w_out, m_w_out, v_w_out, "adamw_w_out"),
        "ffn_w2": update([lands[f"w2_{l}_{s}"] for l in range(n_layer) for s in range(2)],
                         ffn_w2, m_ffn_w2, v_ffn_w2, "adamw_ffn_w2"),
    }

    dmod = jnp.stack([jnp.stack([stats[l][sub][r] for sub in range(3) for r in (0, 1, 3)]) for l in range(n_layer)])
    dng = jnp.stack([jnp.stack([stats[l][sub][2] for sub in range(3)]) for l in range(n_layer)])
    dcw = jnp.stack([conv_stats[l][0:3] for l in range(n_layer)])
    dcb = jnp.stack([conv_stats[l][3] for l in range(n_layer)])
    dqk = jnp.stack([qk_stats[l][0:2].reshape(2, n_head, HEAD_DIM).sum(axis=1) for l in range(n_layer)])
    part_shapes = [dmod.shape, dng.shape, dcw.shape, dcb.shape, dqk.shape]
    parts_all = _allgather_small(_pack([dmod, dng, dcw, dcb, dqk]), "gather_small_grads")
    dmod_all = _unpack(parts_all, part_shapes)[0].reshape(N_DEV, n_layer, 9 * d_model)
    summed = _unpack(_sum_devices(parts_all), part_shapes)
    g_b_ada = summed[0].reshape(n_layer, 9 * d_model)
    ng_cols = norm_g.shape[-1]
    g_norm_g = lax.dynamic_slice_in_dim(summed[1], me * ng_cols, ng_cols, axis=2)
    cw_cols = conv_w.shape[-1]
    g_conv_w = lax.dynamic_slice_in_dim(summed[2], me * cw_cols, cw_cols, axis=2)
    g_conv_b = summed[3]
    g_q, g_k = summed[4][:, 0], summed[4][:, 1]
    dmod_loc = lax.dynamic_slice_in_dim(dmod_all, me * ada_cols, ada_cols, axis=2)
    g_w_ada = _w_ada_grad(silu_c.T, jnp.moveaxis(dmod_loc, 0, 1))
    res["w_ada"] = update([g_w_ada[None]], w_ada, m_w_ada, v_w_ada, "adamw_w_ada")

    names = ["b_ada", "norm_g", "q_norm_g", "k_norm_g", "conv_w", "conv_b"]
    g_small = [g_b_ada, g_norm_g, g_q, g_k, g_conv_w, g_conv_b]
    w_small = [b_ada, norm_g, q_norm_g, k_norm_g, conv_w, conv_b]
    m_small = [m_b_ada, m_norm_g, m_q_norm_g, m_k_norm_g, m_conv_w, m_conv_b]
    v_small = [v_b_ada, v_norm_g, v_q_norm_g, v_k_norm_g, v_conv_w, v_conv_b]
    outs = _adamw([_pack(g_small)[None]], _pack(w_small), _pack(m_small), _pack(v_small), "adamw_small")
    shapes = [w.shape for w in w_small]
    unpacked = [_unpack(o, shapes) for o in outs]
    for i, nme in enumerate(names):
        res[nme] = [unpacked[k][i] for k in range(4)]

    follows = (res["w_in"][1][0, :1, :1] + res["w_out"][1][0, :1, :1] + res["ffn_w2"][1][0, 0, :1, :1]
               + res["w_ada"][1][0, :1, :1] + outs[1][:1, :1])
    lands.update(wait_exchange([last], follows, "grads_wait_last"))
    res["ffn_w1"] = update([lands[f"w1_{l}_{s}"] for l in range(n_layer) for s in range(2)],
                           ffn_w1, m_ffn_w1, v_ffn_w1, "adamw_ffn_w1")

    order = ["w_ada", "b_ada", "norm_g", "w_in", "q_norm_g", "k_norm_g", "conv_w", "conv_b", "w_out", "ffn_w1", "ffn_w2"]
    return (loss, grad_x, *[res[n][0] for n in order], *[res[n][1] for n in order],
            *[res[n][2] for n in order], *[res[n][3] for n in order])
```

```python
import jax
import jax.numpy as jnp
from jax import lax
from jax.experimental import pallas as pl
from jax.experimental.pallas import tpu as pltpu

F32 = jnp.float32
BF16 = jnp.bfloat16
SDS = jax.ShapeDtypeStruct

N_DEV = 8
HEAD_DIM = 64
BLOCK = 128
DILATIONS = (1, 4, 16)
ATTN_BLOCKS_PER_STEP = 4
N_CHUNK = 4
EPS = 1e-6
NEG = -1e30
SM_SCALE = HEAD_DIM ** -0.5
LANES = 128
MXU_WIDTH = 256
TOKEN_TILE = 512
FFN_ROW_PART = 256
FFN_FWD_TOKEN_TILE = 256
FFN_BWD_TOKEN_TILE = 256
GRAD_TOKEN_TILE = 2048
VMEM_LIMIT_BYTES = 56 * 1024 * 1024

ADAM_LR = 0.001
ADAM_B1 = 0.9
ADAM_B2 = 0.999
ADAM_EPS = 1e-08
ADAM_WD = 0.01
ADAM_STEP = 10

MESH_ID = pl.DeviceIdType.MESH
ANY = pl.BlockSpec(memory_space=pl.ANY)
HBM_SPEC = pl.BlockSpec(memory_space=pltpu.HBM)
SEM_SPEC = pl.BlockSpec(memory_space=pltpu.SEMAPHORE)
SIDE_EFFECT = pltpu.SideEffectType.DATAFLOW_SIDE_EFFECTING


def _params(n_axes):
    return pltpu.CompilerParams(dimension_semantics=("arbitrary",) * n_axes,
                                vmem_limit_bytes=VMEM_LIMIT_BYTES)


def _dot(a, b):
    return jnp.dot(a, b, preferred_element_type=F32)


def _dot_nt(a, b):
    return lax.dot_general(a, b, (((1,), (1,)), ((), ())), preferred_element_type=F32)


def _dot_tn(a, b):
    return lax.dot_general(a, b, (((0,), (0,)), ((), ())), preferred_element_type=F32)


def _row_halves(n):
    return (slice(0, n // 2), slice(n // 2, n))


def _split_bf16(t):
    hi = t.astype(BF16)
    return hi, (t - hi.astype(F32)).astype(BF16)


def _my_place():
    x, y, c = lax.axis_index("x"), lax.axis_index("y"), lax.axis_index("c")
    return x, y, c


def _peer(place, k):
    x, y, c = place
    return ((1 - x) if k & 4 else x, (1 - y) if k & 2 else y, (1 - c) if k & 1 else c)


def _index(place):
    return 4 * place[0] + 2 * place[1] + place[2]


def _allgather_small(v, name):
    rows, cols = v.shape

    def body(x_ref, out_ref, send_sems, recv_sems, local_sem):
        me = _my_place()
        mine = pltpu.make_async_copy(x_ref, out_ref.at[_index(me)], local_sem)
        mine.start()
        sends = []
        for k in range(1, N_DEV):
            cp = pltpu.make_async_remote_copy(
                src_ref=x_ref, dst_ref=out_ref.at[_index(me)], send_sem=send_sems.at[k - 1],
                recv_sem=recv_sems.at[k - 1], device_id=_peer(me, k), device_id_type=MESH_ID)
            cp.start()
            sends.append(cp)
        for k in range(1, N_DEV):
            pltpu.make_async_remote_copy(
                src_ref=x_ref, dst_ref=out_ref.at[_index(_peer(me, k))], send_sem=send_sems.at[k - 1],
                recv_sem=recv_sems.at[k - 1], device_id=_peer(me, k), device_id_type=MESH_ID).wait_recv()
        for cp in sends:
            cp.wait_send()
        mine.wait()

    return pl.pallas_call(
        body, name=name,
        out_shape=SDS((N_DEV, rows, cols), v.dtype),
        in_specs=[pl.BlockSpec(memory_space=pltpu.VMEM)],
        out_specs=pl.BlockSpec(memory_space=pltpu.VMEM),
        scratch_shapes=[pltpu.SemaphoreType.DMA((N_DEV - 1,)), pltpu.SemaphoreType.DMA((N_DEV - 1,)),
                        pltpu.SemaphoreType.DMA],
    )(v)


def _allgather_big(lands, name):
    n = len(lands)

    def body(*refs):
        outs = refs[n:2 * n]
        send_sems, recv_sems = refs[2 * n:]
        x, y, c = _my_place()
        me, sib = (x, y, c), (x, y, 1 - c)
        chips = [(1 - x, y), (x, 1 - y), (1 - x, 1 - y)]

        def copy(i, k, block, to):
            dst = outs[i].at[_index(block)]
            return pltpu.make_async_remote_copy(
                src_ref=dst, dst_ref=dst, send_sem=send_sems.at[7 * i + k],
                recv_sem=recv_sems.at[7 * i + k], device_id=to, device_id_type=MESH_ID)

        first = []
        for i in range(n):
            first.append(copy(i, 0, me, sib))
            for j, chip in enumerate(chips):
                first.append(copy(i, 1 + j, me, (*chip, c)))
        for cp in first:
            cp.start()
        passed = []
        for i in range(n):
            for j, chip in enumerate(chips):
                copy(i, 1 + j, (*chip, c), me).wait_recv()
                fwd = copy(i, 4 + j, (*chip, c), sib)
                fwd.start()
                passed.append(fwd)
        for i in range(n):
            copy(i, 0, sib, me).wait_recv()
            for j, chip in enumerate(chips):
                copy(i, 4 + j, (*chip, 1 - c), me).wait_recv()
        for cp in first + passed:
            cp.wait_send()

    return pl.pallas_call(
        body, name=name,
        out_shape=[SDS(a.shape, a.dtype) for a in lands],
        in_specs=[ANY] * n, out_specs=[ANY] * n,
        input_output_aliases={i: i for i in range(n)},
        scratch_shapes=[pltpu.SemaphoreType.DMA((7 * n,)), pltpu.SemaphoreType.DMA((7 * n,))],
    )(*lands)


def _place_own(src, lead, me_arr, name):
    rows, cols = src.shape[-2:]
    tr = _row_tile(rows, cols)

    def body(me_ref, s_ref, o_ref):
        o_ref[...] = s_ref[...].astype(BF16)

    if lead is None:
        in_spec = pl.BlockSpec((None, tr, cols), lambda i, me_ref: (me_ref[0], i, 0))
    else:
        in_spec = pl.BlockSpec((None,) * len(lead) + (tr, cols), lambda i, me_ref: (*lead, i, 0))
    return pl.pallas_call(
        body, name=name,
        grid_spec=pltpu.PrefetchScalarGridSpec(
            num_scalar_prefetch=1, grid=(rows // tr,), in_specs=[in_spec],
            out_specs=pl.BlockSpec((None, tr, cols), lambda i, me_ref: (me_ref[0], i, 0))),
        out_shape=SDS((N_DEV, rows, cols), BF16),
        compiler_params=_params(1),
    )(me_arr, src)


def _send_start(lands, pieces, after, name):
    n = len(lands)
    arrs = list(lands) + list(pieces or [])
    n_a = len(arrs)

    def body(*refs):
        land_r, piece_r = refs[:n], refs[n:n_a]
        send, recv = refs[n_a + 1:n_a + 1 + n], refs[n_a + 1 + n:n_a + 1 + 2 * n]
        token = refs[-1]
        me = _my_place()
        for i in range(n):
            for k in range(1, N_DEV):
                peer = _peer(me, k)
                src = piece_r[i].at[_index(peer)] if pieces else land_r[i].at[_index(me)]
                pltpu.make_async_remote_copy(
                    src_ref=src, dst_ref=land_r[i].at[_index(me)], send_sem=send[i], recv_sem=recv[i],
                    device_id=peer, device_id_type=MESH_ID).start()
        token[...] = jnp.zeros_like(token)

    outs = pl.pallas_call(
        body, name=name,
        out_shape=[pltpu.SemaphoreType.DMA(())] * (2 * n) + [pltpu.HBM(a.shape, a.dtype) for a in arrs]
        + [SDS((8, LANES), F32)],
        in_specs=[HBM_SPEC] * n_a + [ANY],
        out_specs=[SEM_SPEC] * (2 * n) + [HBM_SPEC] * n_a + [pl.BlockSpec(memory_space=pltpu.VMEM)],
        input_output_aliases={i: 2 * n + i for i in range(n_a)},
        compiler_params=pltpu.CompilerParams(has_side_effects=SIDE_EFFECT),
    )(*[pltpu.with_memory_space_constraint(a, pltpu.HBM) for a in arrs], after)
    return outs[:n], outs[n:2 * n], outs[2 * n:2 * n + n_a], outs[-1]


def _send_wait(thru, sends, recvs, after, name):
    n = len(sends)
    n_a = len(thru)

    def body(*refs):
        land_r = refs[:n]
        send, recv = refs[n_a:n_a + n], refs[n_a + n:n_a + 2 * n]
        me = _my_place()
        for i in range(n):
            seven = land_r[i].at[pl.ds(0, N_DEV - 1)]
            cp = pltpu.make_async_remote_copy(src_ref=seven, dst_ref=seven, send_sem=send[i], recv_sem=recv[i],
                                              device_id=_peer(me, 1), device_id_type=MESH_ID)
            cp.wait_send()
            cp.wait_recv()

    outs = pl.pallas_call(
        body, name=name,
        out_shape=[pltpu.HBM(a.shape, a.dtype) for a in thru],
        in_specs=[HBM_SPEC] * n_a + [SEM_SPEC] * (2 * n) + [ANY],
        out_specs=[HBM_SPEC] * n_a,
        input_output_aliases={i: i for i in range(n_a)},
        compiler_params=pltpu.CompilerParams(has_side_effects=SIDE_EFFECT),
    )(*thru, *sends, *recvs, after)
    return outs[:n]


def _ada_fwd(x, prm_ref):
    rstd = lax.rsqrt(jnp.mean(x * x, axis=-1, keepdims=True) + EPS)
    return (x * rstd * prm_ref[0:1, :]) * (1.0 + prm_ref[2:3, :]) + prm_ref[1:2, :]


def _ada_bwd(dh, x, prm_ref):
    ng, ops = prm_ref[0:1, :], 1.0 + prm_ref[2:3, :]
    rstd = lax.rsqrt(jnp.mean(x * x, axis=-1, keepdims=True) + EPS)
    xhat = x * rstd
    dxhat = dh * (ops * ng)
    dx = rstd * (dxhat - xhat * jnp.mean(dxhat * xhat, axis=-1, keepdims=True))
    dshift = jnp.sum(dh, axis=0, keepdims=True)
    dscale = jnp.sum(dh * (xhat * ng), axis=0, keepdims=True)
    dng = jnp.sum(dh * (xhat * ops), axis=0, keepdims=True)
    return dx, dshift, dscale, dng


def _head_mean(t, bd):
    hi = t.astype(BF16)
    g = bd.shape[0]
    parts = [_dot(hi[:, i:i + g], bd) for i in range(0, t.shape[1], g)]
    return jnp.concatenate(parts, axis=1) * (1.0 / HEAD_DIM)


def _join_columns(w_hbm, dst_ref, sems):
    cols = w_hbm.shape[2]
    copies = [pltpu.make_async_copy(w_hbm.at[j], dst_ref.at[:, pl.ds(j * cols, cols)], sems.at[j])
              for j in range(w_hbm.shape[0])]
    for cp in copies:
        cp.start()
    for cp in copies:
        cp.wait()


def _shift_rows(z, before, tm):
    row = lax.broadcasted_iota(jnp.int32, z.shape, 0)
    z1 = jnp.where(row == 0, before[7:8, :], pltpu.roll(z, 1, 0))
    z2 = jnp.where(row == 0, before[6:7, :], jnp.where(row == 1, before[7:8, :], pltpu.roll(z, 2, 0)))
    return z1, z2


def _mod_part(c_all, w_ada, b_loc):
    n_layer, d_model, cols = w_ada.shape

    def body(c_ref, w_ref, b_ref, out_ref, sc_ref):
        cv = c_ref[...]
        sc = cv * jax.nn.sigmoid(cv)
        sc_ref[...] = sc
        a_hi, a_lo = _split_bf16(sc)
        w_hi, w_lo = _split_bf16(w_ref[...])
        out_ref[...] = _dot(a_hi, w_hi) + _dot(a_hi, w_lo) + _dot(a_lo, w_hi) + b_ref[...]

    return pl.pallas_call(
        body, name="mod_part", grid=(n_layer,),
        in_specs=[pl.BlockSpec((N_DEV, d_model), lambda l: (0, 0)),
                  pl.BlockSpec((None, d_model, cols), lambda l: (l, 0, 0)),
                  pl.BlockSpec((None, 1, cols), lambda l: (l, 0, 0))],
        out_specs=[pl.BlockSpec((None, N_DEV, cols), lambda l: (l, 0, 0)),
                   pl.BlockSpec((N_DEV, d_model), lambda l: (0, 0))],
        out_shape=[SDS((n_layer, N_DEV, cols), F32), SDS((N_DEV, d_model), F32)],
        compiler_params=_params(1),
    )(c_all, w_ada, b_loc)


def _w_ada_grad(sc_t, dmod_loc):
    d_model = sc_t.shape[0]
    n_layer, _, cols = dmod_loc.shape

    def body(s_ref, d_ref, out_ref):
        acc = s_ref[:, 0:1] * d_ref[0:1, :]
        for b in range(1, N_DEV):
            acc = acc + s_ref[:, b:b + 1] * d_ref[b:b + 1, :]
        out_ref[...] = acc

    return pl.pallas_call(
        body, name="w_ada_grad", grid=(n_layer,),
        in_specs=[pl.BlockSpec((d_model, N_DEV), lambda l: (0, 0)),
                  pl.BlockSpec((None, N_DEV, cols), lambda l: (l, 0, 0))],
        out_specs=pl.BlockSpec((None, d_model, cols), lambda l: (l, 0, 0)),
        out_shape=SDS((n_layer, d_model, cols), F32),
        compiler_params=_params(1),
    )(sc_t, dmod_loc)


def _sum_devices(g):
    _, rows, cols = g.shape

    def body(g_ref, out_ref):
        acc = g_ref[0]
        for i in range(1, N_DEV):
            acc = acc + g_ref[i]
        out_ref[...] = acc

    return pl.pallas_call(body, name="sum_devices", out_shape=SDS((rows, cols), F32))(g)


def _ffn_fwd(x, prm, w1r, w2g, l, sub, tm):
    t_len, d_model = x.shape
    f8, r8 = w1r.shape[-1], w2g.shape[-2]
    n_m = t_len // tm

    def body(x_ref, xn_ref, prm_ref, w1_ref, w2_ref, act_ref, h_ref, y_ref, xo_ref, h_sc):
        @pl.when(pl.program_id(0) == 0)
        def _():
            h_sc[...] = _ada_fwd(x_ref[...], prm_ref).astype(BF16)

        h = h_sc[...]
        h_ref[...] = h

        def first(j):
            return _dot(h, w1_ref[0, j]), _dot(h, w1_ref[1, j])

        gu = first(0)
        yv = None
        for j in range(N_CHUNK):
            g, u = gu
            if j + 1 < N_CHUNK:
                gu = first(j + 1)
            else:
                h_next = _ada_fwd(xn_ref[...], prm_ref).astype(BF16)
            sig = jax.nn.sigmoid(g)
            sil = g * sig
            s = (sil * u).astype(BF16)
            act_ref[0, j] = (u * (sig * (1.0 + g * (1.0 - sig)))).astype(BF16)
            act_ref[1, j] = sil.astype(BF16)
            act_ref[2, j] = s
            part = _dot(s, w2_ref[2 * j:2 * j + 2].reshape(2 * r8, d_model))
            yv = part if yv is None else yv + part
        y_ref[...] = yv.astype(BF16)
        xo_ref[...] = x_ref[...] + (0.5 * prm_ref[3:4, :]) * yv
        h_sc[...] = h_next

    tile = pl.BlockSpec((tm, d_model), lambda m: (m, 0))
    return pl.pallas_call(
        body, name=f"ffn_fwd_{l}_{sub}", grid=(n_m,),
        in_specs=[tile, pl.BlockSpec((tm, d_model), lambda m: (jnp.minimum(m + 1, n_m - 1), 0)),
                  pl.BlockSpec((8, d_model), lambda m: (0, 0)),
                  pl.BlockSpec((2, N_CHUNK, d_model, f8), lambda m: (0, 0, 0, 0), pipeline_mode=pl.Buffered(1)),
                  pl.BlockSpec((N_DEV, r8, d_model), lambda m: (0, 0, 0), pipeline_mode=pl.Buffered(1))],
        out_specs=[pl.BlockSpec((3, N_CHUNK, tm, f8), lambda m: (0, 0, m, 0)), tile, tile, tile],
        out_shape=[SDS((3, N_CHUNK, t_len, f8), BF16), SDS((t_len, d_model), BF16),
                   SDS((t_len, d_model), BF16), SDS((t_len, d_model), F32)],
        scratch_shapes=[pltpu.VMEM((tm, d_model), BF16)],
        compiler_params=_params(1),
    )(x, x, prm, w1r, w2g)


def _ffn_bwd(dxo, y, x_in, prm, act, w1r, w2g, l, sub, tm):
    t_len, d_model = x_in.shape
    f8, r8 = w1r.shape[-1], w2g.shape[-2]
    n_m = t_len // tm

    def body(dxo_ref, y_ref, x_ref, prm_ref, act_ref, w1_ref, w2_ref, dx_ref, da_ref, dy_ref, st_ref):
        @pl.when(pl.program_id(0) == 0)
        def _():
            st_ref[...] = jnp.zeros_like(st_ref)

        dxo_v = dxo_ref[...]
        dy = ((0.5 * prm_ref[3:4, :]) * dxo_v).astype(BF16)
        dy_ref[...] = dy
        st_ref[3:4, :] += jnp.sum(0.5 * dxo_v * y_ref[...].astype(F32), axis=0, keepdims=True)

        def first(j):
            return _dot_nt(dy, w2_ref[2 * j:2 * j + 2].reshape(2 * r8, d_model))

        ds = first(0)
        dh = None
        for j in range(N_CHUNK):
            ds_now = ds
            if j + 1 < N_CHUNK:
                ds = first(j + 1)
            dg = (ds_now * act_ref[0, j].astype(F32)).astype(BF16)
            du = (ds_now * act_ref[1, j].astype(F32)).astype(BF16)
            da_ref[0, j] = dg
            da_ref[1, j] = du
            part = _dot_nt(dg, w1_ref[0, j]) + _dot_nt(du, w1_ref[1, j])
            dh = part if dh is None else dh + part
        dx, dshift, dscale, dng = _ada_bwd(dh, x_ref[...], prm_ref)
        dx_ref[...] = dxo_v + dx
        st_ref[0:1, :] += dshift
        st_ref[1:2, :] += dscale
        st_ref[2:3, :] += dng

    tile = pl.BlockSpec((tm, d_model), lambda m: (m, 0))
    fixed = pl.BlockSpec((8, d_model), lambda m: (0, 0))
    return pl.pallas_call(
        body, name=f"ffn_bwd_{l}_{sub}", grid=(n_m,),
        in_specs=[tile, tile, tile, fixed,
                  pl.BlockSpec((2, N_CHUNK, tm, f8), lambda m: (0, 0, m, 0)),
                  pl.BlockSpec((2, N_CHUNK, d_model, f8), lambda m: (0, 0, 0, 0), pipeline_mode=pl.Buffered(1)),
                  pl.BlockSpec((N_DEV, r8, d_model), lambda m: (0, 0, 0), pipeline_mode=pl.Buffered(1))],
        out_specs=[tile, pl.BlockSpec((2, N_CHUNK, tm, f8), lambda m: (0, 0, m, 0)), tile, fixed],
        out_shape=[SDS((t_len, d_model), F32), SDS((2, N_CHUNK, t_len, f8), BF16),
                   SDS((t_len, d_model), BF16), SDS((8, d_model), F32)],
        compiler_params=_params(1),
    )(dxo, y, x_in, prm, act, w1r, w2g)


def _matmul_tn(a, b, a_spec, b_spec, out_shape, out_spec, n_piece, t_len, tk, name, after):
    def body(a_ref, b_ref, after_ref, o_ref, acc):
        k = pl.program_id(1)

        @pl.when(k == 0)
        def _():
            acc[...] = jnp.zeros_like(acc)

        m_out, n_out = acc.shape
        if (m_out // 2) % LANES == 0:
            for rows in _row_halves(m_out):
                acc[rows, :] += _dot_tn(a_ref[:, rows], b_ref[...])
        else:
            for cols in _row_halves(n_out):
                acc[:, cols] += _dot_tn(a_ref[...], b_ref[:, cols])

        @pl.when(k == pl.num_programs(1) - 1)
        def _():
            if len(o_ref.shape) == 3:
                cols = o_ref.shape[2]
                for q in range(o_ref.shape[0]):
                    o_ref[q] = acc[:, q * cols:(q + 1) * cols].astype(o_ref.dtype)
            else:
                o_ref[...] = acc[...].astype(o_ref.dtype)

    blk = tuple(d for d in out_spec.block_shape if d is not None)
    acc_shape = blk if len(blk) == 2 else (blk[1], blk[0] * blk[2])
    return pl.pallas_call(
        body, name=name, grid=(n_piece, t_len // tk),
        in_specs=[a_spec, b_spec, ANY], out_specs=out_spec, out_shape=out_shape,
        scratch_shapes=[pltpu.VMEM(acc_shape, F32)],
        compiler_params=_params(2),
    )(a, b, after)


def _groups(tm, width):
    return pltpu.VMEM((width // LANES, tm, LANES), F32)


def _to_groups(val, dst_ref, first=0):
    for g in range(val.shape[1] // LANES):
        dst_ref[first + g] = val[:, g * LANES:(g + 1) * LANES]


def _from_groups(src_ref):
    return jnp.concatenate([src_ref[g] for g in range(src_ref.shape[0])], axis=1)


def _split_residues(src_ref, out_ref, dil, tm):
    for g in range(src_ref.shape[0]):
        for r in range(dil):
            out_ref[r, :, g * LANES:(g + 1) * LANES] = (
                src_ref.at[g][pl.ds(r, tm // dil, stride=dil), :].astype(out_ref.dtype))


def _merge_residues(in_ref, dst_ref, dil, tm, add=False):
    for g in range(dst_ref.shape[0]):
        for r in range(dil):
            rows = pl.ds(r, tm // dil, stride=dil)
            val = in_ref[r, :, g * LANES:(g + 1) * LANES].astype(F32)
            dst = dst_ref.at[g]
            dst[rows, :] = (dst[rows, :] + val) if add else val


def _res_spec(dil, tm, width, index=lambda m: m):
    return pl.BlockSpec((dil, tm // dil, width), lambda m: (0, index(m), 0))


def _mixer_in(x, prm, wing, qkg, cprm, bd, l, tm):
    t_len, d_model = x.shape
    a_w = d_model // 2
    c_w = d_model - a_w
    c8 = wing.shape[-1]
    p_c = N_DEV * c8
    dils = DILATIONS[1:]

    def body(x_ref, prm_ref, w_ref, qkg_ref, cprm_ref, bd_ref, proj_ref, qkn_ref, h_ref, ycv_ref, *rest):
        res_refs, (pf, qsc, vsc, carry, wcat, wsems) = rest[:2 * len(dils)], rest[2 * len(dils):]

        @pl.when(pl.program_id(0) == 0)
        def _():
            carry[...] = jnp.zeros_like(carry)
            _join_columns(w_ref, wcat, wsems)

        h = _ada_fwd(x_ref[...], prm_ref).astype(BF16)
        h_ref[...] = h
        for cols in _row_halves(p_c):
            pf[:, cols] = _dot(h, wcat[:, cols])
        proj_ref[...] = pf[...].astype(BF16)
        bdm = bd_ref[...]
        for i in range(2):
            t = pf[:, i * a_w:(i + 1) * a_w]
            rstd = lax.rsqrt(_head_mean(t * t, bdm) + EPS)
            tn = t * rstd * qkg_ref[i:i + 1, :]
            qkn_ref[:, i * a_w:(i + 1) * a_w] = tn.astype(BF16)
            _to_groups(tn, qsc, i * (a_w // LANES))
        _to_groups(pf[:, 2 * a_w:3 * a_w], vsc)
        for i, dil in enumerate(dils):
            _split_residues(qsc, res_refs[2 * i], dil, tm)
            _split_residues(vsc, res_refs[2 * i + 1], dil, tm)
        gb = pf[:, 3 * a_w:3 * a_w + c_w]
        z = pf[:, 3 * a_w + c_w:3 * a_w + 2 * c_w] * pf[:, 3 * a_w + 2 * c_w:3 * a_w + 3 * c_w]
        z1, z2 = _shift_rows(z, carry[...], tm)
        cv = cprm_ref[0:1, :] * z2 + cprm_ref[1:2, :] * z1 + cprm_ref[2:3, :] * z + cprm_ref[3:4, :]
        ycv_ref[...] = (gb * cv).astype(BF16)
        carry[...] = z[tm - 8:tm, :]

    res_specs, res_shapes = [], []
    for dil in dils:
        res_specs += [_res_spec(dil, tm, 2 * a_w), _res_spec(dil, tm, a_w)]
        res_shapes += [SDS((dil, t_len // dil, 2 * a_w), BF16), SDS((dil, t_len // dil, a_w), BF16)]
    return pl.pallas_call(
        body, name=f"mixer_in_{l}", grid=(t_len // tm,),
        in_specs=[pl.BlockSpec((tm, d_model), lambda m: (m, 0)),
                  pl.BlockSpec((8, d_model), lambda m: (0, 0)),
                  ANY,
                  pl.BlockSpec((8, a_w), lambda m: (0, 0)),
                  pl.BlockSpec((8, c_w), lambda m: (0, 0)),
                  pl.BlockSpec(bd.shape, lambda m: (0, 0))],
        out_specs=[pl.BlockSpec((tm, p_c), lambda m: (m, 0)),
                   pl.BlockSpec((tm, 2 * a_w), lambda m: (m, 0)),
                   pl.BlockSpec((tm, d_model), lambda m: (m, 0)),
                   pl.BlockSpec((tm, c_w), lambda m: (m, 0))] + res_specs,
        out_shape=[SDS((t_len, p_c), BF16), SDS((t_len, 2 * a_w), BF16),
                   SDS((t_len, d_model), BF16), SDS((t_len, c_w), BF16)] + res_shapes,
        scratch_shapes=[pltpu.VMEM((tm, p_c), F32), _groups(tm, 2 * a_w), _groups(tm, a_w),
                        pltpu.VMEM((8, c_w), F32), pltpu.VMEM((d_model, p_c), BF16),
                        pltpu.SemaphoreType.DMA((N_DEV,))],
        compiler_params=_params(1),
    )(x, prm, wing, qkg, cprm, bd)


def _band_mask(first_block):
    qi = lax.broadcasted_iota(jnp.int32, (BLOCK, 2 * BLOCK), 0)
    kj = lax.broadcasted_iota(jnp.int32, (BLOCK, 2 * BLOCK), 1)
    lowest = jnp.where(first_block, BLOCK, 0)
    return (kj >= qi) & (kj <= qi + BLOCK) & (kj >= lowest)


def _blocks_per_step(seq):
    n_sb = min(ATTN_BLOCKS_PER_STEP, seq // BLOCK)
    assert seq % (n_sb * BLOCK) == 0
    return n_sb


def _block_rows(sb):
    return slice(sb * BLOCK, (sb + 1) * BLOCK)


def _with_previous(prev_ref, cur_ref, sb, sl):
    before = prev_ref[:, sl] if sb == 0 else cur_ref[_block_rows(sb - 1), sl]
    return jnp.concatenate([before, cur_ref[_block_rows(sb), sl]], axis=0)


def _heads_on_lanes(cols):
    width = LANES // len(cols)
    lane = lax.broadcasted_iota(jnp.int32, (BLOCK, LANES), 1)
    out = jnp.broadcast_to(cols[-1], (BLOCK, LANES))
    for h in range(len(cols) - 2, -1, -1):
        out = jnp.where(lane < (h + 1) * width, cols[h], out)
    return out


def _attn_fwd(qk, v, v_col, l):
    dil, seq, a2 = qk.shape
    a_w = a2 // 2
    n_sb = _blocks_per_step(seq)
    rows_step = n_sb * BLOCK
    n_pair = a_w // LANES

    def body(q_ref, kc_ref, vc_ref, o_ref, lse_ref, kp_ref, vp_ref):
        @pl.when(pl.program_id(1) == 0)
        def _():
            kp_ref[...] = jnp.zeros_like(kp_ref)
            vp_ref[...] = jnp.zeros_like(vp_ref)

        masks = [_band_mask(pl.program_id(1) == 0)] + [_band_mask(False)] * (n_sb - 1)
        low = lax.broadcasted_iota(jnp.int32, (BLOCK, LANES), 1) < HEAD_DIM
        units = [(sb, slice(p * LANES, (p + 1) * LANES)) for sb in range(n_sb) for p in range(a_w // LANES)]
        scores = []
        for sb, sl in units:
            qp = q_ref[_block_rows(sb), sl]
            k2 = _with_previous(kp_ref, kc_ref, sb, sl)
            scores.append([_dot_nt(jnp.where(half, qp, jnp.zeros_like(qp)), k2) for half in (low, ~low)])
        probs, lses = [], []
        for (sb, sl), pair in zip(units, scores):
            pr, ls = [], []
            for s in pair:
                s = jnp.where(masks[sb], s * SM_SCALE, NEG)
                mx = jnp.max(s, axis=-1, keepdims=True)
                e = jnp.exp(s - mx)
                den = jnp.sum(e, axis=-1, keepdims=True)
                pr.append((e * (1.0 / den)).astype(BF16))
                ls.append(mx + jnp.log(den))
            probs.append(pr)
            lses.append(ls)
        for (sb, sl), pr, ls in zip(units, probs, lses):
            v2 = _with_previous(vp_ref, vc_ref, sb, sl)
            o_ref[_block_rows(sb), sl] = jnp.where(low, _dot(pr[0], v2), _dot(pr[1], v2)).astype(BF16)
        for sb in range(n_sb):
            cols = [c for ls in lses[sb * n_pair:(sb + 1) * n_pair] for c in ls]
            lse_ref[_block_rows(sb), :] = _heads_on_lanes(cols)
        kp_ref[...] = kc_ref[_block_rows(n_sb - 1), :]
        vp_ref[...] = vc_ref[_block_rows(n_sb - 1), :]

    def blk(f):
        return pl.BlockSpec((None, rows_step, a_w), f)

    return pl.pallas_call(
        body, name=f"attn_fwd_{l}_d{dil}", grid=(dil, seq // rows_step),
        in_specs=[blk(lambda r, b: (r, b, 0)),
                  blk(lambda r, b: (r, b, 1)),
                  blk(lambda r, b: (r, b, v_col))],
        out_specs=[blk(lambda r, b: (r, b, 0)), pl.BlockSpec((None, rows_step, LANES), lambda r, b: (r, b, 0))],
        out_shape=[SDS((dil, seq, a_w), BF16), SDS((dil, seq, LANES), F32)],
        scratch_shapes=[pltpu.VMEM((BLOCK, a_w), BF16), pltpu.VMEM((BLOCK, a_w), BF16)],
        compiler_params=_params(2),
    )(qk, qk, v)


def _attn_combine(outs, lses, ycv, expand, tm, l):
    t_len, c_w = ycv.shape
    a_w = outs[0].shape[-1]
    dils = DILATIONS[1:]
    n_r = len(dils)

    def body(o1, l1, *rest):
        o_res, l_res, y_ref, ex_ref = rest[:n_r], rest[n_r:2 * n_r], rest[2 * n_r], rest[2 * n_r + 1]
        cat_ref, lse_ref = rest[2 * n_r + 2], rest[2 * n_r + 3]
        ya_res, lse_res = rest[2 * n_r + 4:3 * n_r + 4], rest[3 * n_r + 4:4 * n_r + 4]
        scr = rest[4 * n_r + 4:]
        so, sl, sya, slse = scr[:n_r], scr[n_r:2 * n_r], scr[2 * n_r], scr[2 * n_r + 1]
        for i, dil in enumerate(dils):
            _merge_residues(o_res[i], so[i], dil, tm)
            _merge_residues(l_res[i], sl[i], dil, tm)
        ls = [l1[...]] + [s[0] for s in sl]
        mx = jnp.maximum(jnp.maximum(ls[0], ls[1]), ls[2])
        es = [jnp.exp(t - mx) for t in ls]
        den = es[0] + es[1] + es[2]
        inv = 1.0 / den
        slse[0] = mx + jnp.log(den)
        lse_ref[...] = slse[0]
        ex = ex_ref[...]
        ov = [o1[...].astype(F32)] + [_from_groups(s) for s in so]
        ya = jnp.zeros((tm, a_w), F32)
        for e, o in zip(es, ov):
            hi, lo = _split_bf16(e * inv)
            ya = ya + (_dot(hi, ex) + _dot(lo, ex)) * o
        _to_groups(ya, sya)
        cat_ref[:, :a_w] = ya.astype(BF16)
        cat_ref[:, a_w:] = y_ref[...]
        for i, dil in enumerate(dils):
            _split_residues(sya, ya_res[i], dil, tm)
            _split_residues(slse, lse_res[i], dil, tm)

    ta = pl.BlockSpec((tm, a_w), lambda m: (m, 0))
    tl = pl.BlockSpec((tm, LANES), lambda m: (m, 0))
    res_a = [_res_spec(dil, tm, a_w) for dil in dils]
    res_l = [_res_spec(dil, tm, LANES) for dil in dils]
    return pl.pallas_call(
        body, name=f"attn_combine_{l}", grid=(t_len // tm,),
        in_specs=[ta, tl] + res_a + res_l + [pl.BlockSpec((tm, c_w), lambda m: (m, 0)),
                                            pl.BlockSpec((LANES, a_w), lambda m: (0, 0))],
        out_specs=[pl.BlockSpec((tm, a_w + c_w), lambda m: (m, 0)), tl] + res_a + res_l,
        out_shape=[SDS((t_len, a_w + c_w), BF16), SDS((t_len, LANES), F32)]
        + [SDS((dil, t_len // dil, a_w), BF16) for dil in dils]
        + [SDS((dil, t_len // dil, LANES), F32) for dil in dils],
        scratch_shapes=[_groups(tm, a_w)] * n_r + [_groups(tm, LANES)] * n_r + [_groups(tm, a_w), _groups(tm, LANES)],
        compiler_params=_params(1),
    )(outs[0].reshape(t_len, a_w), lses[0].reshape(t_len, LANES), *outs[1:], *lses[1:], ycv, expand)


def _mixer_out(cat, x, prm, woutg, l, tm):
    t_len, d_model = x.shape
    r8 = woutg.shape[-2]

    def body(cat_ref, x_ref, prm_ref, w_ref, xo_ref, y_ref):
        w = w_ref[...].reshape(N_DEV * r8, d_model)
        for rows in _row_halves(tm):
            yv = _dot(cat_ref[rows, :], w)
            y_ref[rows, :] = yv.astype(BF16)
            xo_ref[rows, :] = x_ref[rows, :] + prm_ref[3:4, :] * yv

    tile = pl.BlockSpec((tm, d_model), lambda m: (m, 0))
    return pl.pallas_call(
        body, name=f"mixer_out_{l}", grid=(t_len // tm,),
        in_specs=[tile, tile, pl.BlockSpec((8, d_model), lambda m: (0, 0)),
                  pl.BlockSpec((N_DEV, r8, d_model), lambda m: (0, 0, 0))],
        out_specs=[tile, tile],
        out_shape=[SDS((t_len, d_model), F32), SDS((t_len, d_model), BF16)],
        compiler_params=_params(1),
    )(cat, x, prm, woutg)


def _mixer_bwd_out(dxo, ymix, prm, woutg, proj, cprm, l, tm):
    t_len, d_model = dxo.shape
    a_w = d_model // 2
    c_w = d_model - a_w
    r8 = woutg.shape[-2]
    n_m = t_len // tm
    hb = tm // 8

    dils = DILATIONS[1:]

    def body(dxo_ref, ym_ref, prm_ref, w_ref, pc_ref, halo_ref, cprm_ref,
             dya_ref, dpc_ref, dy_ref, st_ref, cs_ref, *rest):
        dya_res, (carry, dsc) = rest[:len(dils)], rest[len(dils):]
        step = pl.program_id(0)
        tile_i = n_m - 1 - step

        @pl.when(step == 0)
        def _():
            st_ref[...] = jnp.zeros_like(st_ref)
            cs_ref[...] = jnp.zeros_like(cs_ref)
            carry[...] = jnp.zeros_like(carry)

        dxo_v = dxo_ref[...]
        dy = (prm_ref[3:4, :] * dxo_v).astype(BF16)
        dy_ref[...] = dy
        st_ref[3:4, :] += jnp.sum(dxo_v * ym_ref[...].astype(F32), axis=0, keepdims=True)
        w = w_ref[...].reshape(N_DEV * r8, d_model)
        dcat = jnp.concatenate([_dot_nt(dy[rows, :], w) for rows in _row_halves(tm)], axis=0)
        _to_groups(dcat[:, :a_w], dsc)
        dya_ref[...] = dcat[:, :a_w].astype(BF16)
        for i, dil in enumerate(dils):
            _split_residues(dsc, dya_res[i], dil, tm)
        dyc = dcat[:, a_w:]
        gb = pc_ref[:, :c_w].astype(F32)
        gc = pc_ref[:, c_w:2 * c_w].astype(F32)
        u = pc_ref[:, 2 * c_w:].astype(F32)
        z = gc * u
        before = halo_ref[:, c_w:2 * c_w].astype(F32) * halo_ref[:, 2 * c_w:].astype(F32)
        before = jnp.where(tile_i > 0, before, jnp.zeros_like(before))
        z1, z2 = _shift_rows(z, before, tm)
        w0, w1, w2 = cprm_ref[0:1, :], cprm_ref[1:2, :], cprm_ref[2:3, :]
        cv = w0 * z2 + w1 * z1 + w2 * z + cprm_ref[3:4, :]
        dcv = dyc * gb
        cs_ref[0:1, :] += jnp.sum(dcv * z2, axis=0, keepdims=True)
        cs_ref[1:2, :] += jnp.sum(dcv * z1, axis=0, keepdims=True)
        cs_ref[2:3, :] += jnp.sum(dcv * z, axis=0, keepdims=True)
        cs_ref[3:4, :] += jnp.sum(dcv, axis=0, keepdims=True)
        row = lax.broadcasted_iota(jnp.int32, dcv.shape, 0)
        after = carry[...]
        d1 = jnp.where(row == tm - 1, after[0:1, :], pltpu.roll(dcv, tm - 1, 0))
        d2 = jnp.where(row == tm - 2, after[0:1, :],
                       jnp.where(row == tm - 1, after[1:2, :], pltpu.roll(dcv, tm - 2, 0)))
        dz = w2 * dcv + w1 * d1 + w0 * d2
        dpc_ref[:, :c_w] = (dyc * cv).astype(BF16)
        dpc_ref[:, c_w:2 * c_w] = (dz * u).astype(BF16)
        dpc_ref[:, 2 * c_w:] = (dz * gc).astype(BF16)
        carry[...] = dcv[0:8, :]

    def rev(width, col=0):
        return pl.BlockSpec((tm, width), lambda s: (n_m - 1 - s, col))

    fixed_d = pl.BlockSpec((8, d_model), lambda s: (0, 0))
    fixed_c = pl.BlockSpec((8, c_w), lambda s: (0, 0))
    return pl.pallas_call(
        body, name=f"mixer_bwd_out_{l}", grid=(n_m,),
        in_specs=[rev(d_model), rev(d_model), fixed_d,
                  pl.BlockSpec((N_DEV, r8, d_model), lambda s: (0, 0, 0)),
                  rev(3 * c_w, 1),
                  pl.BlockSpec((8, 3 * c_w), lambda s: (jnp.maximum((n_m - 1 - s) * hb - 1, 0), 1)),
                  fixed_c],
        out_specs=[rev(a_w), rev(3 * c_w), rev(d_model), fixed_d, fixed_c]
        + [_res_spec(dil, tm, a_w, lambda s: n_m - 1 - s) for dil in dils],
        out_shape=[SDS((t_len, a_w), BF16), SDS((t_len, 3 * c_w), BF16), SDS((t_len, d_model), BF16),
                   SDS((8, d_model), F32), SDS((8, c_w), F32)]
        + [SDS((dil, t_len // dil, a_w), BF16) for dil in dils],
        scratch_shapes=[pltpu.VMEM((8, c_w), F32), _groups(tm, a_w)],
        compiler_params=_params(1),
    )(dxo, ymix, prm, woutg, proj, proj, cprm)


def _attn_bwd(qk, v, v_col, dya, ya, lse, l):
    dil, seq, a2 = qk.shape
    a_w = a2 // 2
    n_sb = _blocks_per_step(seq)
    rows_step = n_sb * BLOCK
    n_step = seq // rows_step
    last = _block_rows(n_sb - 1)
    lse_w = LANES // (a_w // HEAD_DIM)
    single = n_step == 1

    def body(q_ref, kc_ref, vc_ref, do_ref, ya_ref, lse_ref,
             dq_ref, dk_ref, dv_ref, hold_k, hold_v, acc_k, acc_v, kp_ref, vp_ref):
        j = pl.program_id(1)

        @pl.when(j == 0)
        def _():
            kp_ref[...] = jnp.zeros_like(kp_ref)
            vp_ref[...] = jnp.zeros_like(vp_ref)

        @pl.when((pl.program_id(0) == 0) & (j == 0))
        def _():
            hold_k[...] = jnp.zeros_like(hold_k)
            hold_v[...] = jnp.zeros_like(hold_v)

        @pl.when(j < n_step)
        def _():
            masks = [_band_mask(j == 0)] + [_band_mask(False)] * (n_sb - 1)
            low = lax.broadcasted_iota(jnp.int32, (BLOCK, LANES), 1) < HEAD_DIM
            units = [(sb, slice(p * LANES, (p + 1) * LANES)) for sb in range(n_sb) for p in range(a_w // LANES)]
            acc_k[...] = jnp.zeros_like(acc_k)
            acc_v[...] = jnp.zeros_like(acc_v)
            heads, scores, dps = [], [], []
            for sb, sl in units:
                qp = q_ref[_block_rows(sb), sl]
                k2 = _with_previous(kp_ref, kc_ref, sb, sl)
                v2 = _with_previous(vp_ref, vc_ref, sb, sl)
                dop = do_ref[_block_rows(sb), sl]
                prod = dop.astype(F32) * ya_ref[_block_rows(sb), sl].astype(F32)
                lsep = lse_ref[_block_rows(sb), :]
                first = 2 * (sl.start // LANES) * lse_w
                for hh, half in enumerate((low, ~low)):
                    qh = jnp.where(half, qp, jnp.zeros_like(qp))
                    doh = jnp.where(half, dop, jnp.zeros_like(dop))
                    dsum = jnp.sum(jnp.where(half, prod, 0.0), axis=-1, keepdims=True)
                    heads.append((qh, doh, dsum, lsep[:, first + hh * lse_w:first + hh * lse_w + 1], masks[sb]))
                    scores.append(_dot_nt(qh, k2))
                    dps.append(_dot_nt(doh, v2))
            dss, prs = [], []
            for (qh, doh, dsum, lse_h, valid), s, dp in zip(heads, scores, dps):
                pr = jnp.exp(jnp.where(valid, s * SM_SCALE, NEG) - lse_h)
                dss.append((pr * (dp - dsum) * SM_SCALE).astype(BF16))
                prs.append(pr.astype(BF16))
            for i, (sb, sl) in enumerate(units):
                k2 = _with_previous(kp_ref, kc_ref, sb, sl)
                h0, h1 = heads[2 * i], heads[2 * i + 1]
                dq_ref[_block_rows(sb), sl] = jnp.where(low, _dot(dss[2 * i], k2),
                                                        _dot(dss[2 * i + 1], k2)).astype(BF16)
                two = slice(sb * BLOCK, (sb + 2) * BLOCK)
                acc_k[two, sl] += _dot_tn(dss[2 * i], h0[0]) + _dot_tn(dss[2 * i + 1], h1[0])
                acc_v[two, sl] += _dot_tn(prs[2 * i], h0[1]) + _dot_tn(prs[2 * i + 1], h1[1])
            for out_ref, hold, acc in ((dk_ref, hold_k, acc_k), (dv_ref, hold_v, acc_v)):
                if single:
                    out_ref[...] = acc[BLOCK:, :].astype(BF16)
                    continue
                if n_sb > 1:
                    out_ref[:rows_step - BLOCK, :] = hold[:rows_step - BLOCK, :].astype(BF16)
                out_ref[last, :] = (hold[last, :] + acc[:BLOCK, :]).astype(BF16)
                hold[...] = acc[BLOCK:, :]
            kp_ref[...] = kc_ref[last, :]
            vp_ref[...] = vc_ref[last, :]

        @pl.when(j == n_step)
        def _():
            dk_ref[...] = hold_k[...].astype(BF16)
            dv_ref[...] = hold_v[...].astype(BF16)

    def blk(f):
        return pl.BlockSpec((None, rows_step, a_w), f)

    def cur(j):
        return jnp.minimum(j, n_step - 1)

    def late(j):
        return jnp.maximum(j - 1, 0)

    return pl.pallas_call(
        body, name=f"attn_bwd_{l}_d{dil}", grid=(dil, 1 if single else n_step + 1),
        in_specs=[blk(lambda r, j: (r, cur(j), 0)),
                  blk(lambda r, j: (r, cur(j), 1)),
                  blk(lambda r, j: (r, cur(j), v_col)),
                  blk(lambda r, j: (r, cur(j), 0)),
                  blk(lambda r, j: (r, cur(j), 0)),
                  pl.BlockSpec((None, rows_step, LANES), lambda r, j: (r, cur(j), 0))],
        out_specs=[blk(lambda r, j: (r, cur(j), 0)), blk(lambda r, j: (r, late(j), 0)),
                   blk(lambda r, j: (r, late(j), 0))],
        out_shape=[SDS((dil, seq, a_w), BF16)] * 3,
        scratch_shapes=[pltpu.VMEM((rows_step, a_w), F32), pltpu.VMEM((rows_step, a_w), F32),
                        pltpu.VMEM((rows_step + BLOCK, a_w), F32), pltpu.VMEM((rows_step + BLOCK, a_w), F32),
                        pltpu.VMEM((BLOCK, a_w), BF16), pltpu.VMEM((BLOCK, a_w), BF16)],
        compiler_params=_params(2),
    )(qk, qk, v, dya, ya, lse)


def _mixer_bwd_in(parts, proj, dpc, dxo, x_in, prm, wing, qkg, bd, l, tm):
    t_len, d_model = x_in.shape
    a_w = d_model // 2
    c_w = d_model - a_w
    c8 = wing.shape[-1]
    p_c = N_DEV * c8

    dils = DILATIONS[1:]
    n_in = 3 * len(DILATIONS)

    def body(*refs):
        nat, res = refs[:3], refs[3:n_in]
        (pa_ref, dpc_ref, dxo_ref, x_ref, prm_ref, w_ref, qkg_ref, bd_ref,
         dx_ref, dproj_ref, st_ref, qs_ref) = refs[n_in:n_in + 12]
        sums, wcat, wsems = refs[n_in + 12:n_in + 15], refs[n_in + 15], refs[n_in + 16]

        @pl.when(pl.program_id(0) == 0)
        def _():
            st_ref[...] = jnp.zeros_like(st_ref)
            qs_ref[...] = jnp.zeros_like(qs_ref)
            _join_columns(w_ref, wcat, wsems)

        for t in range(3):
            _to_groups(nat[t][...].astype(F32), sums[t])
            for i, dil in enumerate(dils):
                _merge_residues(res[3 * i + t], sums[t], dil, tm, add=True)
        bdm = bd_ref[...]
        grads = [_from_groups(sums[0]), _from_groups(sums[1])]
        for i in range(2):
            t = pa_ref[:, i * a_w:(i + 1) * a_w].astype(F32)
            rstd = lax.rsqrt(_head_mean(t * t, bdm) + EPS)
            that = t * rstd
            qs_ref[i:i + 1, :] += jnp.sum(grads[i] * that, axis=0, keepdims=True)
            dhat = grads[i] * qkg_ref[i:i + 1, :]
            dt = rstd * (dhat - that * _head_mean(dhat * that, bdm))
            dproj_ref[:, i * a_w:(i + 1) * a_w] = dt.astype(BF16)
        dproj_ref[:, 2 * a_w:3 * a_w] = _from_groups(sums[2]).astype(BF16)
        dproj_ref[:, 3 * a_w:] = dpc_ref[...]
        dh = jnp.concatenate([_dot_nt(dproj_ref[rows, :], wcat[...]) for rows in _row_halves(tm)], axis=0)
        dx, dshift, dscale, dng = _ada_bwd(dh, x_ref[...], prm_ref)
        dx_ref[...] = dxo_ref[...] + dx
        st_ref[0:1, :] += dshift
        st_ref[1:2, :] += dscale
        st_ref[2:3, :] += dng

    ta = pl.BlockSpec((tm, a_w), lambda m: (m, 0))
    tile = pl.BlockSpec((tm, d_model), lambda m: (m, 0))
    return pl.pallas_call(
        body, name=f"mixer_bwd_in_{l}", grid=(t_len // tm,),
        in_specs=[ta] * 3 + [_res_spec(dil, tm, a_w) for dil in dils for _ in range(3)]
                 + [pl.BlockSpec((tm, 3 * a_w), lambda m: (m, 0)),
                             pl.BlockSpec((tm, 3 * c_w), lambda m: (m, 0)),
                             tile, tile, pl.BlockSpec((8, d_model), lambda m: (0, 0)),
                             ANY,
                             pl.BlockSpec((8, a_w), lambda m: (0, 0)),
                             pl.BlockSpec(bd.shape, lambda m: (0, 0))],
        out_specs=[tile, pl.BlockSpec((tm, p_c), lambda m: (m, 0)),
                   pl.BlockSpec((8, d_model), lambda m: (0, 0)), pl.BlockSpec((8, a_w), lambda m: (0, 0))],
        out_shape=[SDS((t_len, d_model), F32), SDS((t_len, p_c), BF16), SDS((8, d_model), F32), SDS((8, a_w), F32)],
        scratch_shapes=[_groups(tm, a_w)] * 3 + [pltpu.VMEM((d_model, p_c), BF16), pltpu.SemaphoreType.DMA((N_DEV,))],
        compiler_params=_params(1),
    )(*[t.reshape(t_len, a_w) for t in parts[0]], *[t for p in parts[1:] for t in p],
      proj, dpc, dxo, x_in, prm, wing, qkg, bd)


def _loss_head(x, target, tm):
    t_len, d_model = x.shape

    def body(x_ref, t_ref, dx_ref, loss_ref):
        @pl.when(pl.program_id(0) == 0)
        def _():
            loss_ref[...] = jnp.zeros_like(loss_ref)

        diff = x_ref[...] - t_ref[...]
        dx_ref[...] = diff * (1.0 / d_model)
        per_token = jnp.sum(diff * diff, axis=-1, keepdims=True) * (1.0 / d_model)
        loss_ref[...] += 0.5 * jnp.sum(per_token)

    tile = pl.BlockSpec((tm, d_model), lambda m: (m, 0))
    return pl.pallas_call(
        body, name="loss_head", grid=(t_len // tm,),
        in_specs=[tile, tile],
        out_specs=[tile, pl.BlockSpec((8, LANES), lambda m: (0, 0))],
        out_shape=[SDS((t_len, d_model), F32), SDS((8, LANES), F32)],
        compiler_params=_params(1),
    )(x, target)


def _row_tile(rows, cols):
    best = None
    for t in range(16, rows + 1, 16):
        if rows % t == 0 and t * cols * 4 <= (1 << 20):
            best = t
    return best if best is not None else rows


def _adamw(pieces, w, m, v, name):
    n_q = len(pieces)
    n_p, rows, cols = pieces[0].shape
    tr = _row_tile(rows, cols)
    n_i = rows // tr
    c1 = 1.0 - ADAM_B1 ** ADAM_STEP
    c2 = 1.0 - ADAM_B2 ** ADAM_STEP

    def body(*refs):
        p_refs = refs[:n_q]
        w_ref, m_ref, v_ref, g_ref, d_ref, nm_ref, nv_ref = refs[n_q:]
        for q in range(n_q):
            @pl.when(pl.program_id(0) == q)
            def _():
                g = p_refs[q][0].astype(F32)
                for i in range(1, n_p):
                    g = g + p_refs[q][i].astype(F32)
                g_ref[...] = g
                nm = ADAM_B1 * m_ref[...] + (1.0 - ADAM_B1) * g
                nv = ADAM_B2 * v_ref[...] + (1.0 - ADAM_B2) * (g * g)
                nm_ref[...] = nm
                nv_ref[...] = nv
                d_ref[...] = -ADAM_LR * ((nm / c1) / (jnp.sqrt(nv / c2) + ADAM_EPS) + ADAM_WD * w_ref[...])

    def piece_spec(q):
        return pl.BlockSpec((n_p, tr, cols), lambda a, i: (0, jnp.where(a == q, i, 0), 0))

    tile = pl.BlockSpec((tr, cols), lambda a, i: (a * n_i + i, 0))
    return pl.pallas_call(
        body, name=name, grid=(n_q, n_i),
        in_specs=[piece_spec(q) for q in range(n_q)] + [tile, tile, tile],
        out_specs=[tile] * 4,
        out_shape=[SDS((n_q * rows, cols), F32)] * 4,
        compiler_params=_params(2),
    )(*pieces, w, m, v)


def _pack(vecs):
    flat = jnp.concatenate([v.reshape(-1).astype(F32) for v in vecs])
    rows = -(-flat.shape[0] // (8 * LANES)) * 8
    return jnp.pad(flat, (0, rows * LANES - flat.shape[0])).reshape(rows, LANES)


def _unpack(packed, shapes):
    lead = packed.shape[:-2]
    flat = packed.reshape(lead + (-1,))
    out, off = [], 0
    for s in shapes:
        size = 1
        for d in s:
            size *= d
        out.append(flat[..., off:off + size].reshape(lead + tuple(s)))
        off += size
    return out


def kernel(x, c, w_ada, b_ada, norm_g, w_in, q_norm_g, k_norm_g, conv_w, conv_b, w_out, ffn_w1, ffn_w2, loss_target, m_w_ada, m_b_ada, m_norm_g, m_w_in, m_q_norm_g, m_k_norm_g, m_conv_w, m_conv_b, m_w_out, m_ffn_w1, m_ffn_w2, v_w_ada, v_b_ada, v_norm_g, v_w_in, v_q_norm_g, v_k_norm_g, v_conv_w, v_conv_b, v_w_out, v_ffn_w1, v_ffn_w2):
    _, t_len, d_model = x.shape
    n_layer = w_ada.shape[0]
    a_w = d_model // 2
    c_w = d_model - a_w
    n_head = a_w // HEAD_DIM
    ada_cols = w_ada.shape[-1]
    tm = min(TOKEN_TILE, t_len)
    tm_ffn = min(FFN_FWD_TOKEN_TILE, t_len)
    tm_ffn_bwd = min(FFN_BWD_TOKEN_TILE, t_len)
    me = _index(_my_place())
    x2 = x.reshape(t_len, d_model)
    target = loss_target.reshape(t_len, d_model)

    small_shapes = [(d_model,), norm_g.shape, conv_w.shape]
    gathered = _allgather_small(_pack([c, norm_g, conv_w]), "gather_small_inputs")
    c_all, ng_parts, cw_parts = _unpack(gathered, small_shapes)
    norm_g_full = jnp.moveaxis(ng_parts, 0, 2).reshape(n_layer, 3, d_model)
    conv_w_full = jnp.moveaxis(cw_parts, 0, 2).reshape(n_layer, 3, c_w)

    b_loc = lax.dynamic_slice_in_dim(b_ada, me * ada_cols, ada_cols, axis=1).reshape(n_layer, 1, ada_cols)
    mod_part, silu_c = _mod_part(c_all, w_ada, b_loc)
    mod_all = _allgather_small(_pack([mod_part]), "gather_mod")
    mod_all = _unpack(mod_all, [mod_part.shape])[0]
    mod_mine = lax.dynamic_index_in_dim(mod_all, me, axis=2, keepdims=False)
    mod = jnp.moveaxis(mod_mine, 0, 1).reshape(n_layer, 3, 3, d_model)

    def prm_of(l, sub):
        rows = jnp.stack([norm_g_full[l, sub], mod[l, sub, 0], mod[l, sub, 1], mod[l, sub, 2]])
        return jnp.pad(rows, ((0, 4), (0, 0)))

    prm = [[prm_of(l, sub) for sub in range(3)] for l in range(n_layer)]
    qkg = [jnp.pad(jnp.stack([jnp.tile(q_norm_g[l], n_head), jnp.tile(k_norm_g[l], n_head)]), ((0, 6), (0, 0)))
           for l in range(n_layer)]
    cprm = [jnp.pad(jnp.concatenate([conv_w_full[l], conv_b[l][None]]), ((0, 4), (0, 0))) for l in range(n_layer)]
    head_of = jnp.arange(min(MXU_WIDTH, a_w)) // HEAD_DIM
    bd = (head_of[:, None] == head_of[None, :]).astype(BF16)
    expand = (jnp.arange(LANES)[:, None] == (jnp.arange(a_w)[None, :] // HEAD_DIM) * (LANES // n_head)).astype(BF16)

    f8 = ffn_w1.shape[-1]
    r8 = ffn_w2.shape[-2]
    c8 = w_in.shape[-1]
    o8 = w_out.shape[-2]
    me_arr = jnp.reshape(me, (1,)).astype(jnp.int32)
    win_b = [_place_own(w_in, (l,), me_arr, f"own_w_in_{l}") for l in range(n_layer)]
    wout_b = [_place_own(w_out, (l,), me_arr, f"own_w_out_{l}") for l in range(n_layer)]
    w1_b = [[_place_own(ffn_w1, (l, s), me_arr, f"own_w1_{l}_{s}") for s in range(2)] for l in range(n_layer)]
    w2_b = [[_place_own(ffn_w2, (l, s), me_arr, f"own_w2_{l}_{s}") for s in range(2)] for l in range(n_layer)]
    w1_first, w2_first = _allgather_big([w1_b[0][0], w2_b[0][0]], "gather_first_weights")
    groups = [[win_b[0], wout_b[0]], [w1_b[0][1], w2_b[0][1]]]
    for l in range(1, n_layer):
        groups += [[w1_b[l][0], w2_b[l][0]], [win_b[l], wout_b[l]], [w1_b[l][1], w2_b[l][1]]]
    flat = [a for grp in groups for a in grp]
    w_sends, w_recvs, w_lands, w_token = _send_start(flat, None, w1_first, "weights_start")

    def gathered(gi, after):
        lo = sum(len(grp) for grp in groups[:gi])
        hi = lo + len(groups[gi])
        return _send_wait(w_lands[lo:hi], w_sends[lo:hi], w_recvs[lo:hi], after, f"weights_wait_{gi}")

    def chunked(w1g):
        return w1g.reshape(2, N_CHUNK, d_model, f8)

    saved = []
    xc = x2
    for l in range(n_layer):
        keep = {"x0": xc}
        if l == 0:
            w1g, w2g = w1_first, w2_first
            prm_first = prm[0][0] + w_token[0, 0]
        else:
            w1g, w2g = gathered(3 * l - 1, xc)
            prm_first = prm[l][0]
        keep["w_ffn0"] = (chunked(w1g), w2g)
        keep["act0"], keep["h0"], keep["y0"], xc = _ffn_fwd(xc, prm_first, *keep["w_ffn0"], l, 0, tm_ffn)
        keep["x1"] = xc
        wing, woutg = gathered(0 if l == 0 else 3 * l, xc)
        keep.update(wing=wing, woutg=woutg)
        proj, qkn, keep["h1"], ycv, *res = _mixer_in(xc, prm[l][1], wing, qkg[l], cprm[l], bd, l, tm)
        qkv = [(qkn[None], proj[None], 2)] + [(res[2 * i], res[2 * i + 1], 0) for i in range(len(DILATIONS) - 1)]
        branches = [_attn_fwd(*t, l) for t in qkv]
        cat, lse, *res = _attn_combine([b[0] for b in branches], [b[1] for b in branches], ycv, expand, tm, l)
        n_r = len(DILATIONS) - 1
        keep["ya"] = [cat[None]] + res[:n_r]
        keep["lse"] = [lse[None]] + res[n_r:]
        xc, keep["ymix"] = _mixer_out(cat, xc, prm[l][1], woutg, l, tm)
        keep.update(proj=proj, qkv=qkv, cat=cat)
        keep["x2"] = xc
        w1g, w2g = gathered(1 if l == 0 else 3 * l + 1, xc)
        keep["w_ffn2"] = (chunked(w1g), w2g)
        keep["act2"], keep["h2"], keep["y2"], xc = _ffn_fwd(xc, prm[l][2], *keep["w_ffn2"], l, 1, tm_ffn)
        saved.append(keep)

    dx, loss_blk = _loss_head(xc, target, tm)
    loss = lax.psum(loss_blk[0, 0], ("x", "y", "c"))

    tk = min(GRAD_TOKEN_TILE, t_len)
    stats =[[None] * 3 for _ in range(n_layer)]
    conv_stats = [None] * n_layer
    qk_stats = [None] * n_layer

    started = {}
    dummy = jnp.zeros((8, LANES), F32)

    def start_exchange(g, name):
        sends, recvs, thru, token = _send_start(
            [_place_own(g, None, me_arr, f"own_grad_{name}")], [g], dummy, f"grads_start_{name}")
        started[name] = (thru[0], thru[1], sends[0], recvs[0])
        return token

    def wait_exchange(names, after, name):
        ent = [started[k] for k in names]
        done = _send_wait([e[0] for e in ent] + [e[1] for e in ent], [e[2] for e in ent], [e[3] for e in ent],
                          after, name)
        return dict(zip(names, done))

    def ffn_backward(dxo, keep, l, sub, which, prm_dep):
        dx_in, da, dy, st = _ffn_bwd(dxo, keep[f"y{which}"], keep[f"x{which}"], prm_dep,
                                     keep[f"act{which}"], *keep[f"w_ffn{which}"], l, sub, tm_ffn_bwd)
        g_w2 = _matmul_tn(
            keep[f"act{which}"], dy,
            pl.BlockSpec((None, None, tk, f8), lambda p, k: (2, p, k, 0)),
            pl.BlockSpec((tk, d_model), lambda p, k: (k, 0)),
            SDS((N_CHUNK, f8, d_model), BF16), pl.BlockSpec((None, f8, d_model), lambda p, k: (p, 0, 0)),
            N_CHUNK, t_len, tk, f"grad_w2_{l}_{sub}", dummy)
        token = start_exchange(g_w2.reshape(N_DEV, r8, d_model), f"w2_{l}_{sub}")
        g_w1 = _matmul_tn(
            da.reshape(N_DEV, t_len, f8), keep[f"h{which}"],
            pl.BlockSpec((None, tk, f8), lambda p, k: (p, k, 0)),
            pl.BlockSpec((tk, d_model), lambda p, k: (k, 0)),
            SDS((N_DEV, f8, d_model), BF16), pl.BlockSpec((None, f8, d_model), lambda p, k: (p, 0, 0)),
            N_DEV, t_len, tk, f"grad_w1_{l}_{sub}", token)
        token = start_exchange(g_w1, f"w1_{l}_{sub}")
        return dx_in, st, token[0, 0]

    token = 0.0
    for l in reversed(range(n_layer)):
        keep = saved[l]
        wing, woutg = keep["wing"], keep["woutg"]
        dx, stats[l][2], token = ffn_backward(dx, keep, l, 1, 2, prm[l][2] + token)
        dya, dpc, dy, st_gate, conv_stats[l], *dya_res = _mixer_bwd_out(
            dx, keep["ymix"], prm[l][1] + token, woutg, keep["proj"], cprm[l], l, tm)
        dyas = [dya[None]] + dya_res
        g_wout = _matmul_tn(
            keep["cat"], dy,
            pl.BlockSpec((tk, d_model), lambda p, k: (k, 0)),
            pl.BlockSpec((tk, d_model), lambda p, k: (k, 0)),
            SDS((d_model, d_model), BF16), pl.BlockSpec((d_model, d_model), lambda p, k: (0, 0)),
            1, t_len, tk, f"grad_wout_{l}", dummy)
        token = start_exchange(g_wout.reshape(N_DEV, o8, d_model), f"wout_{l}")[0, 0]
        parts = [_attn_bwd(*keep["qkv"][i], dyas[i], keep["ya"][i], keep["lse"][i], l)
                 for i in range(len(DILATIONS))]
        dx, dproj, st_norm, qk_stats[l] = _mixer_bwd_in(
            parts, keep["proj"], dpc, dx, keep["x1"], prm[l][1] + token, wing, qkg[l], bd, l, tm)
        stats[l][1] = st_norm + st_gate
        g_win = _matmul_tn(
            keep["h1"], dproj,
            pl.BlockSpec((tk, d_model), lambda p, k: (k, 0)),
            pl.BlockSpec((tk, N_DEV // 2 * c8), lambda p, k: (k, p)),
            SDS((N_DEV, d_model, c8), BF16), pl.BlockSpec((N_DEV // 2, d_model, c8), lambda p, k: (p, 0, 0)),
            2, t_len, tk, f"grad_win_{l}", dummy)
        token = start_exchange(g_win, f"win_{l}")[0, 0]
        dx, stats[l][0], token = ffn_backward(dx, keep, l, 0, 0, prm[l][0] + token)
    grad_x = dx.reshape(x.shape)

    def update(pieces, w, m, v, name):
        shape = w.shape
        cols = shape[-1]
        outs = _adamw([p.reshape(p.shape[0], -1, cols) for p in pieces], w.reshape(-1, cols),
                      m.reshape(-1, cols), v.reshape(-1, cols), name)
        return [o.reshape(shape) for o in outs]

    last = "w1_0_0"
    lands = wait_exchange([k for k in started if k != last], dx, "grads_wait_early")
    res = {
        "w_in": update([lands[f"win_{l}"] for l in range(n_layer)], w_in, m_w_in, v_w_in, "adamw_w_in"),
        "w_out": update([lands[f"wout_{l}"] for l in range(n_layer)], w_out, m_w_out, v_w_out, "adamw_w_out"),
        "ffn_w2": update([lands[f"w2_{l}_{s}"] for l in range(n_layer) for s in range(2)],
                         ffn_w2, m_ffn_w2, v_ffn_w2, "adamw_ffn_w2"),
    }

    dmod = jnp.stack([jnp.stack([stats[l][sub][r] for sub in range(3) for r in (0, 1, 3)]) for l in range(n_layer)])
    dng = jnp.stack([jnp.stack([stats[l][sub][2] for sub in range(3)]) for l in range(n_layer)])
    dcw = jnp.stack([conv_stats[l][0:3] for l in range(n_layer)])
    dcb = jnp.stack([conv_stats[l][3] for l in range(n_layer)])
    dqk = jnp.stack([qk_stats[l][0:2].reshape(2, n_head, HEAD_DIM).sum(axis=1) for l in range(n_layer)])
    part_shapes = [dmod.shape, dng.shape, dcw.shape, dcb.shape, dqk.shape]
    parts_all = _allgather_small(_pack([dmod, dng, dcw, dcb, dqk]), "gather_small_grads")
    dmod_all = _unpack(parts_all, part_shapes)[0].reshape(N_DEV, n_layer, 9 * d_model)
    summed = _unpack(_sum_devices(parts_all), part_shapes)
    g_b_ada = summed[0].reshape(n_layer, 9 * d_model)
    ng_cols = norm_g.shape[-1]
    g_norm_g = lax.dynamic_slice_in_dim(summed[1], me * ng_cols, ng_cols, axis=2)
    cw_cols = conv_w.shape[-1]
    g_conv_w = lax.dynamic_slice_in_dim(summed[2], me * cw_cols, cw_cols, axis=2)
    g_conv_b = summed[3]
    g_q, g_k = summed[4][:, 0], summed[4][:, 1]
    dmod_loc = lax.dynamic_slice_in_dim(dmod_all, me * ada_cols, ada_cols, axis=2)
    g_w_ada = _w_ada_grad(silu_c.T, jnp.moveaxis(dmod_loc, 0, 1))
    res["w_ada"] = update([g_w_ada[None]], w_ada, m_w_ada, v_w_ada, "adamw_w_ada")

    names = ["b_ada", "norm_g", "q_norm_g", "k_norm_g", "conv_w", "conv_b"]
    g_small = [g_b_ada, g_norm_g, g_q, g_k, g_conv_w, g_conv_b]
    w_small = [b_ada, norm_g, q_norm_g, k_norm_g, conv_w, conv_b]
    m_small = [m_b_ada, m_norm_g, m_q_norm_g, m_k_norm_g, m_conv_w, m_conv_b]
    v_small = [v_b_ada, v_norm_g, v_q_norm_g, v_k_norm_g, v_conv_w, v_conv_b]
    outs = _adamw([_pack(g_small)[None]], _pack(w_small), _pack(m_small), _pack(v_small), "adamw_small")
    shapes = [w.shape for w in w_small]
    unpacked = [_unpack(o, shapes) for o in outs]
    for i, nme in enumerate(names):
        res[nme] = [unpacked[k][i] for k in range(4)]

    follows = (res["w_in"][1][0, :1, :1] + res["w_out"][1][0, :1, :1] + res["ffn_w2"][1][0, 0, :1, :1]
               + res["w_ada"][1][0, :1, :1] + outs[1][:1, :1])
    lands.update(wait_exchange([last], follows, "grads_wait_last"))
    transposed = update([lands[f"w1_{l}_{s}"] for l in range(n_layer) for s in range(2)],
                        *[jnp.swapaxes(t, -1, -2) for t in (ffn_w1, m_ffn_w1, v_ffn_w1)], "adamw_ffn_w1")
    res["ffn_w1"] = [jnp.swapaxes(t, -1, -2) for t in transposed]

    order = ["w_ada", "b_ada", "norm_g", "w_in", "q_norm_g", "k_norm_g", "conv_w", "conv_b", "w_out", "ffn_w1", "ffn_w2"]
    return (loss, grad_x, *[res[n][0] for n in order], *[res[n][1] for n in order],
            *[res[n][2] for n in order], *[res[n][3] for n in order])
```

```python
import jax
import jax.numpy as jnp
from jax import lax
from jax.experimental import pallas as pl
from jax.experimental.pallas import tpu as pltpu

F32 = jnp.float32
BF16 = jnp.bfloat16
SDS = jax.ShapeDtypeStruct

N_DEV = 8
HEAD_DIM = 64
BLOCK = 128
DILATIONS = (1, 4, 16)
ATTN_BLOCKS_PER_STEP = 4
N_CHUNK = 4
EPS = 1e-6
NEG = -1e30
SM_SCALE = HEAD_DIM ** -0.5
LANES = 128
MXU_WIDTH = 256
TOKEN_TILE = 512
FFN_ROW_PART = 256
FFN_FWD_TOKEN_TILE = 256
FFN_BWD_TOKEN_TILE = 256
GRAD_TOKEN_TILE = 2048
VMEM_LIMIT_BYTES = 56 * 1024 * 1024

ADAM_LR = 0.001
ADAM_B1 = 0.9
ADAM_B2 = 0.999
ADAM_EPS = 1e-08
ADAM_WD = 0.01
ADAM_STEP = 10

MESH_ID = pl.DeviceIdType.MESH
ANY = pl.BlockSpec(memory_space=pl.ANY)
HBM_SPEC = pl.BlockSpec(memory_space=pltpu.HBM)
SEM_SPEC = pl.BlockSpec(memory_space=pltpu.SEMAPHORE)
SIDE_EFFECT = pltpu.SideEffectType.DATAFLOW_SIDE_EFFECTING


def _params(n_axes):
    return pltpu.CompilerParams(dimension_semantics=("arbitrary",) * n_axes,
                                vmem_limit_bytes=VMEM_LIMIT_BYTES)


def _dot(a, b):
    return jnp.dot(a, b, preferred_element_type=F32)


def _dot_nt(a, b):
    return lax.dot_general(a, b, (((1,), (1,)), ((), ())), preferred_element_type=F32)


def _dot_tn(a, b):
    return lax.dot_general(a, b, (((0,), (0,)), ((), ())), preferred_element_type=F32)


def _row_halves(n):
    return (slice(0, n // 2), slice(n // 2, n))


def _split_bf16(t):
    hi = t.astype(BF16)
    return hi, (t - hi.astype(F32)).astype(BF16)


def _my_place():
    x, y, c = lax.axis_index("x"), lax.axis_index("y"), lax.axis_index("c")
    return x, y, c


def _peer(place, k):
    x, y, c = place
    return ((1 - x) if k & 4 else x, (1 - y) if k & 2 else y, (1 - c) if k & 1 else c)


def _index(place):
    return 4 * place[0] + 2 * place[1] + place[2]


def _allgather_small(v, name):
    rows, cols = v.shape

    def body(x_ref, out_ref, send_sems, recv_sems, local_sem):
        me = _my_place()
        mine = pltpu.make_async_copy(x_ref, out_ref.at[_index(me)], local_sem)
        mine.start()
        sends = []
        for k in range(1, N_DEV):
            cp = pltpu.make_async_remote_copy(
                src_ref=x_ref, dst_ref=out_ref.at[_index(me)], send_sem=send_sems.at[k - 1],
                recv_sem=recv_sems.at[k - 1], device_id=_peer(me, k), device_id_type=MESH_ID)
            cp.start()
            sends.append(cp)
        for k in range(1, N_DEV):
            pltpu.make_async_remote_copy(
                src_ref=x_ref, dst_ref=out_ref.at[_index(_peer(me, k))], send_sem=send_sems.at[k - 1],
                recv_sem=recv_sems.at[k - 1], device_id=_peer(me, k), device_id_type=MESH_ID).wait_recv()
        for cp in sends:
            cp.wait_send()
        mine.wait()

    return pl.pallas_call(
        body, name=name,
        out_shape=SDS((N_DEV, rows, cols), v.dtype),
        in_specs=[pl.BlockSpec(memory_space=pltpu.VMEM)],
        out_specs=pl.BlockSpec(memory_space=pltpu.VMEM),
        scratch_shapes=[pltpu.SemaphoreType.DMA((N_DEV - 1,)), pltpu.SemaphoreType.DMA((N_DEV - 1,)),
                        pltpu.SemaphoreType.DMA],
    )(v)


def _allgather_big(lands, name):
    n = len(lands)

    def body(*refs):
        outs = refs[n:2 * n]
        send_sems, recv_sems = refs[2 * n:]
        x, y, c = _my_place()
        me, sib = (x, y, c), (x, y, 1 - c)
        chips = [(1 - x, y), (x, 1 - y), (1 - x, 1 - y)]

        def copy(i, k, block, to):
            dst = outs[i].at[_index(block)]
            return pltpu.make_async_remote_copy(
                src_ref=dst, dst_ref=dst, send_sem=send_sems.at[7 * i + k],
                recv_sem=recv_sems.at[7 * i + k], device_id=to, device_id_type=MESH_ID)

        first = []
        for i in range(n):
            first.append(copy(i, 0, me, sib))
            for j, chip in enumerate(chips):
                first.append(copy(i, 1 + j, me, (*chip, c)))
        for cp in first:
            cp.start()
        passed = []
        for i in range(n):
            for j, chip in enumerate(chips):
                copy(i, 1 + j, (*chip, c), me).wait_recv()
                fwd = copy(i, 4 + j, (*chip, c), sib)
                fwd.start()
                passed.append(fwd)
        for i in range(n):
            copy(i, 0, sib, me).wait_recv()
            for j, chip in enumerate(chips):
                copy(i, 4 + j, (*chip, 1 - c), me).wait_recv()
        for cp in first + passed:
            cp.wait_send()

    return pl.pallas_call(
        body, name=name,
        out_shape=[SDS(a.shape, a.dtype) for a in lands],
        in_specs=[ANY] * n, out_specs=[ANY] * n,
        input_output_aliases={i: i for i in range(n)},
        scratch_shapes=[pltpu.SemaphoreType.DMA((7 * n,)), pltpu.SemaphoreType.DMA((7 * n,))],
    )(*lands)


def _place_own(src, lead, me_arr, name):
    rows, cols = src.shape[-2:]
    tr = _row_tile(rows, cols)

    def body(me_ref, s_ref, o_ref):
        o_ref[...] = s_ref[...].astype(BF16)

    if lead is None:
        in_spec = pl.BlockSpec((None, tr, cols), lambda i, me_ref: (me_ref[0], i, 0))
    else:
        in_spec = pl.BlockSpec((None,) * len(lead) + (tr, cols), lambda i, me_ref: (*lead, i, 0))
    return pl.pallas_call(
        body, name=name,
        grid_spec=pltpu.PrefetchScalarGridSpec(
            num_scalar_prefetch=1, grid=(rows // tr,), in_specs=[in_spec],
            out_specs=pl.BlockSpec((None, tr, cols), lambda i, me_ref: (me_ref[0], i, 0))),
        out_shape=SDS((N_DEV, rows, cols), BF16),
        compiler_params=_params(1),
    )(me_arr, src)


def _send_start(lands, pieces, after, name):
    n = len(lands)
    arrs = list(lands) + list(pieces or [])
    n_a = len(arrs)

    def body(*refs):
        land_r, piece_r = refs[:n], refs[n:n_a]
        send, recv = refs[n_a + 1:n_a + 1 + n], refs[n_a + 1 + n:n_a + 1 + 2 * n]
        token = refs[-1]
        me = _my_place()
        for i in range(n):
            for k in range(1, N_DEV):
                peer = _peer(me, k)
                src = piece_r[i].at[_index(peer)] if pieces else land_r[i].at[_index(me)]
                pltpu.make_async_remote_copy(
                    src_ref=src, dst_ref=land_r[i].at[_index(me)], send_sem=send[i], recv_sem=recv[i],
                    device_id=peer, device_id_type=MESH_ID).start()
        token[...] = jnp.zeros_like(token)

    outs = pl.pallas_call(
        body, name=name,
        out_shape=[pltpu.SemaphoreType.DMA(())] * (2 * n) + [pltpu.HBM(a.shape, a.dtype) for a in arrs]
        + [SDS((8, LANES), F32)],
        in_specs=[HBM_SPEC] * n_a + [ANY],
        out_specs=[SEM_SPEC] * (2 * n) + [HBM_SPEC] * n_a + [pl.BlockSpec(memory_space=pltpu.VMEM)],
        input_output_aliases={i: 2 * n + i for i in range(n_a)},
        compiler_params=pltpu.CompilerParams(has_side_effects=SIDE_EFFECT),
    )(*[pltpu.with_memory_space_constraint(a, pltpu.HBM) for a in arrs], after)
    return outs[:n], outs[n:2 * n], outs[2 * n:2 * n + n_a], outs[-1]


def _send_wait(thru, sends, recvs, after, name):
    n = len(sends)
    n_a = len(thru)

    def body(*refs):
        land_r = refs[:n]
        send, recv = refs[n_a:n_a + n], refs[n_a + n:n_a + 2 * n]
        me = _my_place()
        for i in range(n):
            seven = land_r[i].at[pl.ds(0, N_DEV - 1)]
            cp = pltpu.make_async_remote_copy(src_ref=seven, dst_ref=seven, send_sem=send[i], recv_sem=recv[i],
                                              device_id=_peer(me, 1), device_id_type=MESH_ID)
            cp.wait_send()
            cp.wait_recv()

    outs = pl.pallas_call(
        body, name=name,
        out_shape=[pltpu.HBM(a.shape, a.dtype) for a in thru],
        in_specs=[HBM_SPEC] * n_a + [SEM_SPEC] * (2 * n) + [ANY],
        out_specs=[HBM_SPEC] * n_a,
        input_output_aliases={i: i for i in range(n_a)},
        compiler_params=pltpu.CompilerParams(has_side_effects=SIDE_EFFECT),
    )(*thru, *sends, *recvs, after)
    return outs[:n]


def _ada_fwd(x, prm_ref):
    rstd = lax.rsqrt(jnp.mean(x * x, axis=-1, keepdims=True) + EPS)
    return (x * rstd * prm_ref[0:1, :]) * (1.0 + prm_ref[2:3, :]) + prm_ref[1:2, :]


def _ada_bwd(dh, x, prm_ref):
    ng, ops = prm_ref[0:1, :], 1.0 + prm_ref[2:3, :]
    rstd = lax.rsqrt(jnp.mean(x * x, axis=-1, keepdims=True) + EPS)
    xhat = x * rstd
    dxhat = dh * (ops * ng)
    dx = rstd * (dxhat - xhat * jnp.mean(dxhat * xhat, axis=-1, keepdims=True))
    dshift = jnp.sum(dh, axis=0, keepdims=True)
    dscale = jnp.sum(dh * (xhat * ng), axis=0, keepdims=True)
    dng = jnp.sum(dh * (xhat * ops), axis=0, keepdims=True)
    return dx, dshift, dscale, dng


def _head_mean(t, bd):
    hi = t.astype(BF16)
    g = bd.shape[0]
    parts = [_dot(hi[:, i:i + g], bd) for i in range(0, t.shape[1], g)]
    return jnp.concatenate(parts, axis=1) * (1.0 / HEAD_DIM)


def _join_columns(w_hbm, dst_ref, sems):
    cols = w_hbm.shape[2]
    copies = [pltpu.make_async_copy(w_hbm.at[j], dst_ref.at[:, pl.ds(j * cols, cols)], sems.at[j])
              for j in range(w_hbm.shape[0])]
    for cp in copies:
        cp.start()
    for cp in copies:
        cp.wait()


def _shift_rows(z, before, tm):
    row = lax.broadcasted_iota(jnp.int32, z.shape, 0)
    z1 = jnp.where(row == 0, before[7:8, :], pltpu.roll(z, 1, 0))
    z2 = jnp.where(row == 0, before[6:7, :], jnp.where(row == 1, before[7:8, :], pltpu.roll(z, 2, 0)))
    return z1, z2


def _mod_part(c_all, w_ada, b_loc):
    n_layer, d_model, cols = w_ada.shape

    def body(c_ref, w_ref, b_ref, out_ref, sc_ref):
        cv = c_ref[...]
        sc = cv * jax.nn.sigmoid(cv)
        sc_ref[...] = sc
        a_hi, a_lo = _split_bf16(sc)
        w_hi, w_lo = _split_bf16(w_ref[...])
        out_ref[...] = _dot(a_hi, w_hi) + _dot(a_hi, w_lo) + _dot(a_lo, w_hi) + b_ref[...]

    return pl.pallas_call(
        body, name="mod_part", grid=(n_layer,),
        in_specs=[pl.BlockSpec((N_DEV, d_model), lambda l: (0, 0)),
                  pl.BlockSpec((None, d_model, cols), lambda l: (l, 0, 0)),
                  pl.BlockSpec((None, 1, cols), lambda l: (l, 0, 0))],
        out_specs=[pl.BlockSpec((None, N_DEV, cols), lambda l: (l, 0, 0)),
                   pl.BlockSpec((N_DEV, d_model), lambda l: (0, 0))],
        out_shape=[SDS((n_layer, N_DEV, cols), F32), SDS((N_DEV, d_model), F32)],
        compiler_params=_params(1),
    )(c_all, w_ada, b_loc)


def _w_ada_grad(sc_t, dmod_loc):
    d_model = sc_t.shape[0]
    n_layer, _, cols = dmod_loc.shape

    def body(s_ref, d_ref, out_ref):
        acc = s_ref[:, 0:1] * d_ref[0:1, :]
        for b in range(1, N_DEV):
            acc = acc + s_ref[:, b:b + 1] * d_ref[b:b + 1, :]
        out_ref[...] = acc

    return pl.pallas_call(
        body, name="w_ada_grad", grid=(n_layer,),
        in_specs=[pl.BlockSpec((d_model, N_DEV), lambda l: (0, 0)),
                  pl.BlockSpec((None, N_DEV, cols), lambda l: (l, 0, 0))],
        out_specs=pl.BlockSpec((None, d_model, cols), lambda l: (l, 0, 0)),
        out_shape=SDS((n_layer, d_model, cols), F32),
        compiler_params=_params(1),
    )(sc_t, dmod_loc)


def _sum_devices(g):
    _, rows, cols = g.shape

    def body(g_ref, out_ref):
        acc = g_ref[0]
        for i in range(1, N_DEV):
            acc = acc + g_ref[i]
        out_ref[...] = acc

    return pl.pallas_call(body, name="sum_devices", out_shape=SDS((rows, cols), F32))(g)


def _ffn_fwd(x, prm, w1r, w2g, l, sub, tm):
    t_len, d_model = x.shape
    f8, r8 = w1r.shape[-1], w2g.shape[-2]
    n_m = t_len // tm

    def body(x_ref, xn_ref, prm_ref, w1_ref, w2_ref, act_ref, h_ref, y_ref, xo_ref, h_sc):
        @pl.when(pl.program_id(0) == 0)
        def _():
            h_sc[...] = _ada_fwd(x_ref[...], prm_ref).astype(BF16)

        h = h_sc[...]
        h_ref[...] = h

        def first(j):
            return _dot(h, w1_ref[0, j]), _dot(h, w1_ref[1, j])

        gu = first(0)
        yv = None
        for j in range(N_CHUNK):
            g, u = gu
            if j + 1 < N_CHUNK:
                gu = first(j + 1)
            else:
                h_next = _ada_fwd(xn_ref[...], prm_ref).astype(BF16)
            sig = jax.nn.sigmoid(g)
            sil = g * sig
            s = (sil * u).astype(BF16)
            act_ref[0, j] = (u * (sig * (1.0 + g * (1.0 - sig)))).astype(BF16)
            act_ref[1, j] = sil.astype(BF16)
            act_ref[2, j] = s
            part = _dot(s, w2_ref[2 * j:2 * j + 2].reshape(2 * r8, d_model))
            yv = part if yv is None else yv + part
        y_ref[...] = yv.astype(BF16)
        xo_ref[...] = x_ref[...] + (0.5 * prm_ref[3:4, :]) * yv
        h_sc[...] = h_next

    tile = pl.BlockSpec((tm, d_model), lambda m: (m, 0))
    return pl.pallas_call(
        body, name=f"ffn_fwd_{l}_{sub}", grid=(n_m,),
        in_specs=[tile, pl.BlockSpec((tm, d_model), lambda m: (jnp.minimum(m + 1, n_m - 1), 0)),
                  pl.BlockSpec((8, d_model), lambda m: (0, 0)),
                  pl.BlockSpec((2, N_CHUNK, d_model, f8), lambda m: (0, 0, 0, 0), pipeline_mode=pl.Buffered(1)),
                  pl.BlockSpec((N_DEV, r8, d_model), lambda m: (0, 0, 0), pipeline_mode=pl.Buffered(1))],
        out_specs=[pl.BlockSpec((3, N_CHUNK, tm, f8), lambda m: (0, 0, m, 0)), tile, tile, tile],
        out_shape=[SDS((3, N_CHUNK, t_len, f8), BF16), SDS((t_len, d_model), BF16),
                   SDS((t_len, d_model), BF16), SDS((t_len, d_model), F32)],
        scratch_shapes=[pltpu.VMEM((tm, d_model), BF16)],
        compiler_params=_params(1),
    )(x, x, prm, w1r, w2g)


def _ffn_bwd(dxo, y, x_in, prm, act, w1r, w2g, l, sub, tm):
    t_len, d_model = x_in.shape
    f8, r8 = w1r.shape[-1], w2g.shape[-2]
    n_m = t_len // tm

    def body(dxo_ref, y_ref, x_ref, prm_ref, act_ref, w1_ref, w2_ref, dx_ref, da_ref, dy_ref, st_ref):
        @pl.when(pl.program_id(0) == 0)
        def _():
            st_ref[...] = jnp.zeros_like(st_ref)

        dxo_v = dxo_ref[...]
        dy = ((0.5 * prm_ref[3:4, :]) * dxo_v).astype(BF16)
        dy_ref[...] = dy
        st_ref[3:4, :] += jnp.sum(0.5 * dxo_v * y_ref[...].astype(F32), axis=0, keepdims=True)

        def first(j):
            return _dot_nt(dy, w2_ref[2 * j:2 * j + 2].reshape(2 * r8, d_model))

        ds = first(0)
        dh = None
        for j in range(N_CHUNK):
            ds_now = ds
            if j + 1 < N_CHUNK:
                ds = first(j + 1)
            dg = (ds_now * act_ref[0, j].astype(F32)).astype(BF16)
            du = (ds_now * act_ref[1, j].astype(F32)).astype(BF16)
            da_ref[0, j] = dg
            da_ref[1, j] = du
            part = _dot_nt(dg, w1_ref[0, j]) + _dot_nt(du, w1_ref[1, j])
            dh = part if dh is None else dh + part
        dx, dshift, dscale, dng = _ada_bwd(dh, x_ref[...], prm_ref)
        dx_ref[...] = dxo_v + dx
        st_ref[0:1, :] += dshift
        st_ref[1:2, :] += dscale
        st_ref[2:3, :] += dng

    tile = pl.BlockSpec((tm, d_model), lambda m: (m, 0))
    fixed = pl.BlockSpec((8, d_model), lambda m: (0, 0))
    return pl.pallas_call(
        body, name=f"ffn_bwd_{l}_{sub}", grid=(n_m,),
        in_specs=[tile, tile, tile, fixed,
                  pl.BlockSpec((2, N_CHUNK, tm, f8), lambda m: (0, 0, m, 0)),
                  pl.BlockSpec((2, N_CHUNK, d_model, f8), lambda m: (0, 0, 0, 0), pipeline_mode=pl.Buffered(1)),
                  pl.BlockSpec((N_DEV, r8, d_model), lambda m: (0, 0, 0), pipeline_mode=pl.Buffered(1))],
        out_specs=[tile, pl.BlockSpec((2, N_CHUNK, tm, f8), lambda m: (0, 0, m, 0)), tile, fixed],
        out_shape=[SDS((t_len, d_model), F32), SDS((2, N_CHUNK, t_len, f8), BF16),
                   SDS((t_len, d_model), BF16), SDS((8, d_model), F32)],
        compiler_params=_params(1),
    )(dxo, y, x_in, prm, act, w1r, w2g)


def _matmul_tn(a, b, a_spec, b_spec, out_shape, out_spec, n_piece, t_len, tk, name, after):
    def body(a_ref, b_ref, after_ref, o_ref, acc):
        k = pl.program_id(1)

        @pl.when(k == 0)
        def _():
            acc[...] = jnp.zeros_like(acc)

        m_out, n_out = acc.shape
        if (m_out // 2) % LANES == 0:
            for rows in _row_halves(m_out):
                acc[rows, :] += _dot_tn(a_ref[:, rows], b_ref[...])
        else:
            for cols in _row_halves(n_out):
                acc[:, cols] += _dot_tn(a_ref[...], b_ref[:, cols])

        @pl.when(k == pl.num_programs(1) - 1)
        def _():
            if len(o_ref.shape) == 3:
                cols = o_ref.shape[2]
                for q in range(o_ref.shape[0]):
                    o_ref[q] = acc[:, q * cols:(q + 1) * cols].astype(o_ref.dtype)
            else:
                o_ref[...] = acc[...].astype(o_ref.dtype)

    blk = tuple(d for d in out_spec.block_shape if d is not None)
    acc_shape = blk if len(blk) == 2 else (blk[1], blk[0] * blk[2])
    return pl.pallas_call(
        body, name=name, grid=(n_piece, t_len // tk),
        in_specs=[a_spec, b_spec, ANY], out_specs=out_spec, out_shape=out_shape,
        scratch_shapes=[pltpu.VMEM(acc_shape, F32)],
        compiler_params=_params(2),
    )(a, b, after)


def _groups(tm, width):
    return pltpu.VMEM((width // LANES, tm, LANES), F32)


def _to_groups(val, dst_ref, first=0):
    for g in range(val.shape[1] // LANES):
        dst_ref[first + g] = val[:, g * LANES:(g + 1) * LANES]


def _from_groups(src_ref):
    return jnp.concatenate([src_ref[g] for g in range(src_ref.shape[0])], axis=1)


def _split_residues(src_ref, out_ref, dil, tm):
    for g in range(src_ref.shape[0]):
        for r in range(dil):
            out_ref[r, :, g * LANES:(g + 1) * LANES] = (
                src_ref.at[g][pl.ds(r, tm // dil, stride=dil), :].astype(out_ref.dtype))


def _merge_residues(in_ref, dst_ref, dil, tm, add=False):
    for g in range(dst_ref.shape[0]):
        for r in range(dil):
            rows = pl.ds(r, tm // dil, stride=dil)
            val = in_ref[r, :, g * LANES:(g + 1) * LANES].astype(F32)
            dst = dst_ref.at[g]
            dst[rows, :] = (dst[rows, :] + val) if add else val


def _res_spec(dil, tm, width, index=lambda m: m):
    return pl.BlockSpec((dil, tm // dil, width), lambda m: (0, index(m), 0))


def _mixer_in(x, prm, wing, qkg, cprm, bd, l, tm):
    t_len, d_model = x.shape
    a_w = d_model // 2
    c_w = d_model - a_w
    c8 = wing.shape[-1]
    p_c = N_DEV * c8
    dils = DILATIONS[1:]

    def body(x_ref, prm_ref, w_ref, qkg_ref, cprm_ref, bd_ref, proj_ref, qkn_ref, h_ref, ycv_ref, *rest):
        res_refs, (pf, qsc, vsc, carry, wcat, wsems) = rest[:2 * len(dils)], rest[2 * len(dils):]

        @pl.when(pl.program_id(0) == 0)
        def _():
            carry[...] = jnp.zeros_like(carry)
            _join_columns(w_ref, wcat, wsems)

        h = _ada_fwd(x_ref[...], prm_ref).astype(BF16)
        h_ref[...] = h
        for cols in _row_halves(p_c):
            pf[:, cols] = _dot(h, wcat[:, cols])
        proj_ref[...] = pf[...].astype(BF16)
        bdm = bd_ref[...]
        for i in range(2):
            t = pf[:, i * a_w:(i + 1) * a_w]
            rstd = lax.rsqrt(_head_mean(t * t, bdm) + EPS)
            tn = t * rstd * qkg_ref[i:i + 1, :]
            qkn_ref[:, i * a_w:(i + 1) * a_w] = tn.astype(BF16)
            _to_groups(tn, qsc, i * (a_w // LANES))
        _to_groups(pf[:, 2 * a_w:3 * a_w], vsc)
        for i, dil in enumerate(dils):
            _split_residues(qsc, res_refs[2 * i], dil, tm)
            _split_residues(vsc, res_refs[2 * i + 1], dil, tm)
        gb = pf[:, 3 * a_w:3 * a_w + c_w]
        z = pf[:, 3 * a_w + c_w:3 * a_w + 2 * c_w] * pf[:, 3 * a_w + 2 * c_w:3 * a_w + 3 * c_w]
        z1, z2 = _shift_rows(z, carry[...], tm)
        cv = cprm_ref[0:1, :] * z2 + cprm_ref[1:2, :] * z1 + cprm_ref[2:3, :] * z + cprm_ref[3:4, :]
        ycv_ref[...] = (gb * cv).astype(BF16)
        carry[...] = z[tm - 8:tm, :]

    res_specs, res_shapes = [], []
    for dil in dils:
        res_specs += [_res_spec(dil, tm, 2 * a_w), _res_spec(dil, tm, a_w)]
        res_shapes += [SDS((dil, t_len // dil, 2 * a_w), BF16), SDS((dil, t_len // dil, a_w), BF16)]
    return pl.pallas_call(
        body, name=f"mixer_in_{l}", grid=(t_len // tm,),
        in_specs=[pl.BlockSpec((tm, d_model), lambda m: (m, 0)),
                  pl.BlockSpec((8, d_model), lambda m: (0, 0)),
                  ANY,
                  pl.BlockSpec((8, a_w), lambda m: (0, 0)),
                  pl.BlockSpec((8, c_w), lambda m: (0, 0)),
                  pl.BlockSpec(bd.shape, lambda m: (0, 0))],
        out_specs=[pl.BlockSpec((tm, p_c), lambda m: (m, 0)),
                   pl.BlockSpec((tm, 2 * a_w), lambda m: (m, 0)),
                   pl.BlockSpec((tm, d_model), lambda m: (m, 0)),
                   pl.BlockSpec((tm, c_w), lambda m: (m, 0))] + res_specs,
        out_shape=[SDS((t_len, p_c), BF16), SDS((t_len, 2 * a_w), BF16),
                   SDS((t_len, d_model), BF16), SDS((t_len, c_w), BF16)] + res_shapes,
        scratch_shapes=[pltpu.VMEM((tm, p_c), F32), _groups(tm, 2 * a_w), _groups(tm, a_w),
                        pltpu.VMEM((8, c_w), F32), pltpu.VMEM((d_model, p_c), BF16),
                        pltpu.SemaphoreType.DMA((N_DEV,))],
        compiler_params=_params(1),
    )(x, prm, wing, qkg, cprm, bd)


def _band_mask(first_block):
    qi = lax.broadcasted_iota(jnp.int32, (BLOCK, 2 * BLOCK), 0)
    kj = lax.broadcasted_iota(jnp.int32, (BLOCK, 2 * BLOCK), 1)
    lowest = jnp.where(first_block, BLOCK, 0)
    return (kj >= qi) & (kj <= qi + BLOCK) & (kj >= lowest)


def _blocks_per_step(seq):
    n_sb = min(ATTN_BLOCKS_PER_STEP, seq // BLOCK)
    assert seq % (n_sb * BLOCK) == 0
    return n_sb


def _block_rows(sb):
    return slice(sb * BLOCK, (sb + 1) * BLOCK)


def _with_previous(prev_ref, cur_ref, sb, sl):
    before = prev_ref[:, sl] if sb == 0 else cur_ref[_block_rows(sb - 1), sl]
    return jnp.concatenate([before, cur_ref[_block_rows(sb), sl]], axis=0)


def _heads_on_lanes(cols):
    width = LANES // len(cols)
    lane = lax.broadcasted_iota(jnp.int32, (BLOCK, LANES), 1)
    out = jnp.broadcast_to(cols[-1], (BLOCK, LANES))
    for h in range(len(cols) - 2, -1, -1):
        out = jnp.where(lane < (h + 1) * width, cols[h], out)
    return out


def _attn_fwd(qk, v, v_col, l):
    dil, seq, a2 = qk.shape
    a_w = a2 // 2
    n_sb = _blocks_per_step(seq)
    rows_step = n_sb * BLOCK
    n_pair = a_w // LANES

    def body(q_ref, kc_ref, vc_ref, o_ref, lse_ref, kp_ref, vp_ref):
        @pl.when(pl.program_id(1) == 0)
        def _():
            kp_ref[...] = jnp.zeros_like(kp_ref)
            vp_ref[...] = jnp.zeros_like(vp_ref)

        masks = [_band_mask(pl.program_id(1) == 0)] + [_band_mask(False)] * (n_sb - 1)
        low = lax.broadcasted_iota(jnp.int32, (BLOCK, LANES), 1) < HEAD_DIM
        units = [(sb, slice(p * LANES, (p + 1) * LANES)) for sb in range(n_sb) for p in range(a_w // LANES)]
        scores = []
        for sb, sl in units:
            qp = q_ref[_block_rows(sb), sl]
            k2 = _with_previous(kp_ref, kc_ref, sb, sl)
            scores.append([_dot_nt(jnp.where(half, qp, jnp.zeros_like(qp)), k2) for half in (low, ~low)])
        probs, lses = [], []
        for (sb, sl), pair in zip(units, scores):
            pr, ls = [], []
            for s in pair:
                s = jnp.where(masks[sb], s * SM_SCALE, NEG)
                mx = jnp.max(s, axis=-1, keepdims=True)
                e = jnp.exp(s - mx)
                den = jnp.sum(e, axis=-1, keepdims=True)
                pr.append((e * (1.0 / den)).astype(BF16))
                ls.append(mx + jnp.log(den))
            probs.append(pr)
            lses.append(ls)
        for (sb, sl), pr, ls in zip(units, probs, lses):
            v2 = _with_previous(vp_ref, vc_ref, sb, sl)
            o_ref[_block_rows(sb), sl] = jnp.where(low, _dot(pr[0], v2), _dot(pr[1], v2)).astype(BF16)
        for sb in range(n_sb):
            cols = [c for ls in lses[sb * n_pair:(sb + 1) * n_pair] for c in ls]
            lse_ref[_block_rows(sb), :] = _heads_on_lanes(cols)
        kp_ref[...] = kc_ref[_block_rows(n_sb - 1), :]
        vp_ref[...] = vc_ref[_block_rows(n_sb - 1), :]

    def blk(f):
        return pl.BlockSpec((None, rows_step, a_w), f)

    return pl.pallas_call(
        body, name=f"attn_fwd_{l}_d{dil}", grid=(dil, seq // rows_step),
        in_specs=[blk(lambda r, b: (r, b, 0)),
                  blk(lambda r, b: (r, b, 1)),
                  blk(lambda r, b: (r, b, v_col))],
        out_specs=[blk(lambda r, b: (r, b, 0)), pl.BlockSpec((None, rows_step, LANES), lambda r, b: (r, b, 0))],
        out_shape=[SDS((dil, seq, a_w), BF16), SDS((dil, seq, LANES), F32)],
        scratch_shapes=[pltpu.VMEM((BLOCK, a_w), BF16), pltpu.VMEM((BLOCK, a_w), BF16)],
        compiler_params=_params(2),
    )(qk, qk, v)


def _attn_combine(outs, lses, ycv, expand, tm, l):
    t_len, c_w = ycv.shape
    a_w = outs[0].shape[-1]
    dils = DILATIONS[1:]
    n_r = len(dils)

    def body(o1, l1, *rest):
        o_res, l_res, y_ref, ex_ref = rest[:n_r], rest[n_r:2 * n_r], rest[2 * n_r], rest[2 * n_r + 1]
        cat_ref, lse_ref = rest[2 * n_r + 2], rest[2 * n_r + 3]
        ya_res, lse_res = rest[2 * n_r + 4:3 * n_r + 4], rest[3 * n_r + 4:4 * n_r + 4]
        scr = rest[4 * n_r + 4:]
        so, sl, sya, slse = scr[:n_r], scr[n_r:2 * n_r], scr[2 * n_r], scr[2 * n_r + 1]
        for i, dil in enumerate(dils):
            _merge_residues(o_res[i], so[i], dil, tm)
            _merge_residues(l_res[i], sl[i], dil, tm)
        ls = [l1[...]] + [s[0] for s in sl]
        mx = jnp.maximum(jnp.maximum(ls[0], ls[1]), ls[2])
        es = [jnp.exp(t - mx) for t in ls]
        den = es[0] + es[1] + es[2]
        inv = 1.0 / den
        slse[0] = mx + jnp.log(den)
        lse_ref[...] = slse[0]
        ex = ex_ref[...]
        ov = [o1[...].astype(F32)] + [_from_groups(s) for s in so]
        ya = jnp.zeros((tm, a_w), F32)
        for e, o in zip(es, ov):
            hi, lo = _split_bf16(e * inv)
            ya = ya + (_dot(hi, ex) + _dot(lo, ex)) * o
        _to_groups(ya, sya)
        cat_ref[:, :a_w] = ya.astype(BF16)
        cat_ref[:, a_w:] = y_ref[...]
        for i, dil in enumerate(dils):
            _split_residues(sya, ya_res[i], dil, tm)
            _split_residues(slse, lse_res[i], dil, tm)

    ta = pl.BlockSpec((tm, a_w), lambda m: (m, 0))
    tl = pl.BlockSpec((tm, LANES), lambda m: (m, 0))
    res_a = [_res_spec(dil, tm, a_w) for dil in dils]
    res_l = [_res_spec(dil, tm, LANES) for dil in dils]
    return pl.pallas_call(
        body, name=f"attn_combine_{l}", grid=(t_len // tm,),
        in_specs=[ta, tl] + res_a + res_l + [pl.BlockSpec((tm, c_w), lambda m: (m, 0)),
                                            pl.BlockSpec((LANES, a_w), lambda m: (0, 0))],
        out_specs=[pl.BlockSpec((tm, a_w + c_w), lambda m: (m, 0)), tl] + res_a + res_l,
        out_shape=[SDS((t_len, a_w + c_w), BF16), SDS((t_len, LANES), F32)]
        + [SDS((dil, t_len // dil, a_w), BF16) for dil in dils]
        + [SDS((dil, t_len // dil, LANES), F32) for dil in dils],
        scratch_shapes=[_groups(tm, a_w)] * n_r + [_groups(tm, LANES)] * n_r + [_groups(tm, a_w), _groups(tm, LANES)],
        compiler_params=_params(1),
    )(outs[0].reshape(t_len, a_w), lses[0].reshape(t_len, LANES), *outs[1:], *lses[1:], ycv, expand)


def _mixer_out(cat, x, prm, woutg, l, tm):
    t_len, d_model = x.shape
    r8 = woutg.shape[-2]

    def body(cat_ref, x_ref, prm_ref, w_ref, xo_ref, y_ref):
        w = w_ref[...].reshape(N_DEV * r8, d_model)
        for rows in _row_halves(tm):
            yv = _dot(cat_ref[rows, :], w)
            y_ref[rows, :] = yv.astype(BF16)
            xo_ref[rows, :] = x_ref[rows, :] + prm_ref[3:4, :] * yv

    tile = pl.BlockSpec((tm, d_model), lambda m: (m, 0))
    return pl.pallas_call(
        body, name=f"mixer_out_{l}", grid=(t_len // tm,),
        in_specs=[tile, tile, pl.BlockSpec((8, d_model), lambda m: (0, 0)),
                  pl.BlockSpec((N_DEV, r8, d_model), lambda m: (0, 0, 0))],
        out_specs=[tile, tile],
        out_shape=[SDS((t_len, d_model), F32), SDS((t_len, d_model), BF16)],
        compiler_params=_params(1),
    )(cat, x, prm, woutg)


def _mixer_bwd_out(dxo, ymix, prm, woutg, proj, cprm, l, tm):
    t_len, d_model = dxo.shape
    a_w = d_model // 2
    c_w = d_model - a_w
    r8 = woutg.shape[-2]
    n_m = t_len // tm
    hb = tm // 8

    dils = DILATIONS[1:]

    def body(dxo_ref, ym_ref, prm_ref, w_ref, pc_ref, halo_ref, cprm_ref,
             dya_ref, dpc_ref, dy_ref, st_ref, cs_ref, *rest):
        dya_res, (carry, dsc) = rest[:len(dils)], rest[len(dils):]
        step = pl.program_id(0)
        tile_i = n_m - 1 - step

        @pl.when(step == 0)
        def _():
            st_ref[...] = jnp.zeros_like(st_ref)
            cs_ref[...] = jnp.zeros_like(cs_ref)
            carry[...] = jnp.zeros_like(carry)

        dxo_v = dxo_ref[...]
        dy = (prm_ref[3:4, :] * dxo_v).astype(BF16)
        dy_ref[...] = dy
        st_ref[3:4, :] += jnp.sum(dxo_v * ym_ref[...].astype(F32), axis=0, keepdims=True)
        w = w_ref[...].reshape(N_DEV * r8, d_model)
        dcat = jnp.concatenate([_dot_nt(dy[rows, :], w) for rows in _row_halves(tm)], axis=0)
        _to_groups(dcat[:, :a_w], dsc)
        dya_ref[...] = dcat[:, :a_w].astype(BF16)
        for i, dil in enumerate(dils):
            _split_residues(dsc, dya_res[i], dil, tm)
        dyc = dcat[:, a_w:]
        gb = pc_ref[:, :c_w].astype(F32)
        gc = pc_ref[:, c_w:2 * c_w].astype(F32)
        u = pc_ref[:, 2 * c_w:].astype(F32)
        z = gc * u
        before = halo_ref[:, c_w:2 * c_w].astype(F32) * halo_ref[:, 2 * c_w:].astype(F32)
        before = jnp.where(tile_i > 0, before, jnp.zeros_like(before))
        z1, z2 = _shift_rows(z, before, tm)
        w0, w1, w2 = cprm_ref[0:1, :], cprm_ref[1:2, :], cprm_ref[2:3, :]
        cv = w0 * z2 + w1 * z1 + w2 * z + cprm_ref[3:4, :]
        dcv = dyc * gb
        cs_ref[0:1, :] += jnp.sum(dcv * z2, axis=0, keepdims=True)
        cs_ref[1:2, :] += jnp.sum(dcv * z1, axis=0, keepdims=True)
        cs_ref[2:3, :] += jnp.sum(dcv * z, axis=0, keepdims=True)
        cs_ref[3:4, :] += jnp.sum(dcv, axis=0, keepdims=True)
        row = lax.broadcasted_iota(jnp.int32, dcv.shape, 0)
        after = carry[...]
        d1 = jnp.where(row == tm - 1, after[0:1, :], pltpu.roll(dcv, tm - 1, 0))
        d2 = jnp.where(row == tm - 2, after[0:1, :],
                       jnp.where(row == tm - 1, after[1:2, :], pltpu.roll(dcv, tm - 2, 0)))
        dz = w2 * dcv + w1 * d1 + w0 * d2
        dpc_ref[:, :c_w] = (dyc * cv).astype(BF16)
        dpc_ref[:, c_w:2 * c_w] = (dz * u).astype(BF16)
        dpc_ref[:, 2 * c_w:] = (dz * gc).astype(BF16)
        carry[...] = dcv[0:8, :]

    def rev(width, col=0):
        return pl.BlockSpec((tm, width), lambda s: (n_m - 1 - s, col))

    fixed_d = pl.BlockSpec((8, d_model), lambda s: (0, 0))
    fixed_c = pl.BlockSpec((8, c_w), lambda s: (0, 0))
    return pl.pallas_call(
        body, name=f"mixer_bwd_out_{l}", grid=(n_m,),
        in_specs=[rev(d_model), rev(d_model), fixed_d,
                  pl.BlockSpec((N_DEV, r8, d_model), lambda s: (0, 0, 0)),
                  rev(3 * c_w, 1),
                  pl.BlockSpec((8, 3 * c_w), lambda s: (jnp.maximum((n_m - 1 - s) * hb - 1, 0), 1)),
                  fixed_c],
        out_specs=[rev(a_w), rev(3 * c_w), rev(d_model), fixed_d, fixed_c]
        + [_res_spec(dil, tm, a_w, lambda s: n_m - 1 - s) for dil in dils],
        out_shape=[SDS((t_len, a_w), BF16), SDS((t_len, 3 * c_w), BF16), SDS((t_len, d_model), BF16),
                   SDS((8, d_model), F32), SDS((8, c_w), F32)]
        + [SDS((dil, t_len // dil, a_w), BF16) for dil in dils],
        scratch_shapes=[pltpu.VMEM((8, c_w), F32), _groups(tm, a_w)],
        compiler_params=_params(1),
    )(dxo, ymix, prm, woutg, proj, proj, cprm)


def _attn_bwd(qk, v, v_col, dya, ya, lse, l):
    dil, seq, a2 = qk.shape
    a_w = a2 // 2
    n_sb = _blocks_per_step(seq)
    rows_step = n_sb * BLOCK
    n_step = seq // rows_step
    last = _block_rows(n_sb - 1)
    lse_w = LANES // (a_w // HEAD_DIM)
    single = n_step == 1

    def body(q_ref, kc_ref, vc_ref, do_ref, ya_ref, lse_ref,
             dq_ref, dk_ref, dv_ref, hold_k, hold_v, acc_k, acc_v, kp_ref, vp_ref):
        j = pl.program_id(1)

        @pl.when(j == 0)
        def _():
            kp_ref[...] = jnp.zeros_like(kp_ref)
            vp_ref[...] = jnp.zeros_like(vp_ref)

        @pl.when((pl.program_id(0) == 0) & (j == 0))
        def _():
            hold_k[...] = jnp.zeros_like(hold_k)
            hold_v[...] = jnp.zeros_like(hold_v)

        @pl.when(j < n_step)
        def _():
            masks = [_band_mask(j == 0)] + [_band_mask(False)] * (n_sb - 1)
            low = lax.broadcasted_iota(jnp.int32, (BLOCK, LANES), 1) < HEAD_DIM
            units = [(sb, slice(p * LANES, (p + 1) * LANES)) for sb in range(n_sb) for p in range(a_w // LANES)]
            acc_k[...] = jnp.zeros_like(acc_k)
            acc_v[...] = jnp.zeros_like(acc_v)
            heads, scores, dps = [], [], []
            for sb, sl in units:
                qp = q_ref[_block_rows(sb), sl]
                k2 = _with_previous(kp_ref, kc_ref, sb, sl)
                v2 = _with_previous(vp_ref, vc_ref, sb, sl)
                dop = do_ref[_block_rows(sb), sl]
                prod = dop.astype(F32) * ya_ref[_block_rows(sb), sl].astype(F32)
                lsep = lse_ref[_block_rows(sb), :]
                first = 2 * (sl.start // LANES) * lse_w
                for hh, half in enumerate((low, ~low)):
                    qh = jnp.where(half, qp, jnp.zeros_like(qp))
                    doh = jnp.where(half, dop, jnp.zeros_like(dop))
                    dsum = jnp.sum(jnp.where(half, prod, 0.0), axis=-1, keepdims=True)
                    heads.append((qh, doh, dsum, lsep[:, first + hh * lse_w:first + hh * lse_w + 1], masks[sb]))
                    scores.append(_dot_nt(qh, k2))
                    dps.append(_dot_nt(doh, v2))
            dss, prs = [], []
            for (qh, doh, dsum, lse_h, valid), s, dp in zip(heads, scores, dps):
                pr = jnp.exp(jnp.where(valid, s * SM_SCALE, NEG) - lse_h)
                dss.append((pr * (dp - dsum) * SM_SCALE).astype(BF16))
                prs.append(pr.astype(BF16))
            for i, (sb, sl) in enumerate(units):
                k2 = _with_previous(kp_ref, kc_ref, sb, sl)
                h0, h1 = heads[2 * i], heads[2 * i + 1]
                dq_ref[_block_rows(sb), sl] = jnp.where(low, _dot(dss[2 * i], k2),
                                                        _dot(dss[2 * i + 1], k2)).astype(BF16)
                two = slice(sb * BLOCK, (sb + 2) * BLOCK)
                acc_k[two, sl] += _dot_tn(dss[2 * i], h0[0]) + _dot_tn(dss[2 * i + 1], h1[0])
                acc_v[two, sl] += _dot_tn(prs[2 * i], h0[1]) + _dot_tn(prs[2 * i + 1], h1[1])
            for out_ref, hold, acc in ((dk_ref, hold_k, acc_k), (dv_ref, hold_v, acc_v)):
                if single:
                    out_ref[...] = acc[BLOCK:, :].astype(BF16)
                    continue
                if n_sb > 1:
                    out_ref[:rows_step - BLOCK, :] = hold[:rows_step - BLOCK, :].astype(BF16)
                out_ref[last, :] = (hold[last, :] + acc[:BLOCK, :]).astype(BF16)
                hold[...] = acc[BLOCK:, :]
            kp_ref[...] = kc_ref[last, :]
            vp_ref[...] = vc_ref[last, :]

        @pl.when(j == n_step)
        def _():
            dk_ref[...] = hold_k[...].astype(BF16)
            dv_ref[...] = hold_v[...].astype(BF16)

    def blk(f):
        return pl.BlockSpec((None, rows_step, a_w), f)

    def cur(j):
        return jnp.minimum(j, n_step - 1)

    def late(j):
        return jnp.maximum(j - 1, 0)

    return pl.pallas_call(
        body, name=f"attn_bwd_{l}_d{dil}", grid=(dil, 1 if single else n_step + 1),
        in_specs=[blk(lambda r, j: (r, cur(j), 0)),
                  blk(lambda r, j: (r, cur(j), 1)),
                  blk(lambda r, j: (r, cur(j), v_col)),
                  blk(lambda r, j: (r, cur(j), 0)),
                  blk(lambda r, j: (r, cur(j), 0)),
                  pl.BlockSpec((None, rows_step, LANES), lambda r, j: (r, cur(j), 0))],
        out_specs=[blk(lambda r, j: (r, cur(j), 0)), blk(lambda r, j: (r, late(j), 0)),
                   blk(lambda r, j: (r, late(j), 0))],
        out_shape=[SDS((dil, seq, a_w), BF16)] * 3,
        scratch_shapes=[pltpu.VMEM((rows_step, a_w), F32), pltpu.VMEM((rows_step, a_w), F32),
                        pltpu.VMEM((rows_step + BLOCK, a_w), F32), pltpu.VMEM((rows_step + BLOCK, a_w), F32),
                        pltpu.VMEM((BLOCK, a_w), BF16), pltpu.VMEM((BLOCK, a_w), BF16)],
        compiler_params=_params(2),
    )(qk, qk, v, dya, ya, lse)


def _mixer_bwd_in(parts, proj, dpc, dxo, x_in, prm, wing, qkg, bd, l, tm):
    t_len, d_model = x_in.shape
    a_w = d_model // 2
    c_w = d_model - a_w
    c8 = wing.shape[-1]
    p_c = N_DEV * c8

    dils = DILATIONS[1:]
    n_in = 3 * len(DILATIONS)

    def body(*refs):
        nat, res = refs[:3], refs[3:n_in]
        (pa_ref, dpc_ref, dxo_ref, x_ref, prm_ref, w_ref, qkg_ref, bd_ref,
         dx_ref, dproj_ref, st_ref, qs_ref) = refs[n_in:n_in + 12]
        sums, wcat, wsems = refs[n_in + 12:n_in + 15], refs[n_in + 15], refs[n_in + 16]

        @pl.when(pl.program_id(0) == 0)
        def _():
            st_ref[...] = jnp.zeros_like(st_ref)
            qs_ref[...] = jnp.zeros_like(qs_ref)
            _join_columns(w_ref, wcat, wsems)

        for t in range(3):
            _to_groups(nat[t][...].astype(F32), sums[t])
            for i, dil in enumerate(dils):
                _merge_residues(res[3 * i + t], sums[t], dil, tm, add=True)
        bdm = bd_ref[...]
        grads = [_from_groups(sums[0]), _from_groups(sums[1])]
        for i in range(2):
            t = pa_ref[:, i * a_w:(i + 1) * a_w].astype(F32)
            rstd = lax.rsqrt(_head_mean(t * t, bdm) + EPS)
            that = t * rstd
            qs_ref[i:i + 1, :] += jnp.sum(grads[i] * that, axis=0, keepdims=True)
            dhat = grads[i] * qkg_ref[i:i + 1, :]
            dt = rstd * (dhat - that * _head_mean(dhat * that, bdm))
            dproj_ref[:, i * a_w:(i + 1) * a_w] = dt.astype(BF16)
        dproj_ref[:, 2 * a_w:3 * a_w] = _from_groups(sums[2]).astype(BF16)
        dproj_ref[:, 3 * a_w:] = dpc_ref[...]
        dh = jnp.concatenate([_dot_nt(dproj_ref[rows, :], wcat[...]) for rows in _row_halves(tm)], axis=0)
        dx, dshift, dscale, dng = _ada_bwd(dh, x_ref[...], prm_ref)
        dx_ref[...] = dxo_ref[...] + dx
        st_ref[0:1, :] += dshift
        st_ref[1:2, :] += dscale
        st_ref[2:3, :] += dng

    ta = pl.BlockSpec((tm, a_w), lambda m: (m, 0))
    tile = pl.BlockSpec((tm, d_model), lambda m: (m, 0))
    return pl.pallas_call(
        body, name=f"mixer_bwd_in_{l}", grid=(t_len // tm,),
        in_specs=[ta] * 3 + [_res_spec(dil, tm, a_w) for dil in dils for _ in range(3)]
                 + [pl.BlockSpec((tm, 3 * a_w), lambda m: (m, 0)),
                             pl.BlockSpec((tm, 3 * c_w), lambda m: (m, 0)),
                             tile, tile, pl.BlockSpec((8, d_model), lambda m: (0, 0)),
                             ANY,
                             pl.BlockSpec((8, a_w), lambda m: (0, 0)),
                             pl.BlockSpec(bd.shape, lambda m: (0, 0))],
        out_specs=[tile, pl.BlockSpec((tm, p_c), lambda m: (m, 0)),
                   pl.BlockSpec((8, d_model), lambda m: (0, 0)), pl.BlockSpec((8, a_w), lambda m: (0, 0))],
        out_shape=[SDS((t_len, d_model), F32), SDS((t_len, p_c), BF16), SDS((8, d_model), F32), SDS((8, a_w), F32)],
        scratch_shapes=[_groups(tm, a_w)] * 3 + [pltpu.VMEM((d_model, p_c), BF16), pltpu.SemaphoreType.DMA((N_DEV,))],
        compiler_params=_params(1),
    )(*[t.reshape(t_len, a_w) for t in parts[0]], *[t for p in parts[1:] for t in p],
      proj, dpc, dxo, x_in, prm, wing, qkg, bd)


def _loss_head(x, target, tm):
    t_len, d_model = x.shape

    def body(x_ref, t_ref, dx_ref, loss_ref):
        @pl.when(pl.program_id(0) == 0)
        def _():
            loss_ref[...] = jnp.zeros_like(loss_ref)

        diff = x_ref[...] - t_ref[...]
        dx_ref[...] = diff * (1.0 / d_model)
        per_token = jnp.sum(diff * diff, axis=-1, keepdims=True) * (1.0 / d_model)
        loss_ref[...] += 0.5 * jnp.sum(per_token)

    tile = pl.BlockSpec((tm, d_model), lambda m: (m, 0))
    return pl.pallas_call(
        body, name="loss_head", grid=(t_len // tm,),
        in_specs=[tile, tile],
        out_specs=[tile, pl.BlockSpec((8, LANES), lambda m: (0, 0))],
        out_shape=[SDS((t_len, d_model), F32), SDS((8, LANES), F32)],
        compiler_params=_params(1),
    )(x, target)


def _row_tile(rows, cols):
    best = None
    for t in range(16, rows + 1, 16):
        if rows % t == 0 and t * cols * 4 <= (1 << 20):
            best = t
    return best if best is not None else rows


def _adamw(pieces, w, m, v, name):
    n_q = len(pieces)
    n_p, rows, cols = pieces[0].shape
    tr = _row_tile(rows, cols)
    n_i = rows // tr
    c1 = 1.0 - ADAM_B1 ** ADAM_STEP
    c2 = 1.0 - ADAM_B2 ** ADAM_STEP

    def body(*refs):
        p_refs = refs[:n_q]
        w_ref, m_ref, v_ref, g_ref, d_ref, nm_ref, nv_ref = refs[n_q:]
        for q in range(n_q):
            @pl.when(pl.program_id(0) == q)
            def _():
                g = p_refs[q][0].astype(F32)
                for i in range(1, n_p):
                    g = g + p_refs[q][i].astype(F32)
                g_ref[...] = g
                nm = ADAM_B1 * m_ref[...] + (1.0 - ADAM_B1) * g
                nv = ADAM_B2 * v_ref[...] + (1.0 - ADAM_B2) * (g * g)
                nm_ref[...] = nm
                nv_ref[...] = nv
                d_ref[...] = -ADAM_LR * ((nm / c1) / (jnp.sqrt(nv / c2) + ADAM_EPS) + ADAM_WD * w_ref[...])

    def piece_spec(q):
        return pl.BlockSpec((n_p, tr, cols), lambda a, i: (0, jnp.where(a == q, i, 0), 0))

    tile = pl.BlockSpec((tr, cols), lambda a, i: (a * n_i + i, 0))
    return pl.pallas_call(
        body, name=name, grid=(n_q, n_i),
        in_specs=[piece_spec(q) for q in range(n_q)] + [tile, tile, tile],
        out_specs=[tile] * 4,
        out_shape=[SDS((n_q * rows, cols), F32)] * 4,
        compiler_params=_params(2),
    )(*pieces, w, m, v)


def _pack(vecs):
    flat = jnp.concatenate([v.reshape(-1).astype(F32) for v in vecs])
    rows = -(-flat.shape[0] // (8 * LANES)) * 8
    return jnp.pad(flat, (0, rows * LANES - flat.shape[0])).reshape(rows, LANES)


def _unpack(packed, shapes):
    lead = packed.shape[:-2]
    flat = packed.reshape(lead + (-1,))
    out, off = [], 0
    for s in shapes:
        size = 1
        for d in s:
            size *= d
        out.append(flat[..., off:off + size].reshape(lead + tuple(s)))
        off += size
    return out


def kernel(x, c, w_ada, b_ada, norm_g, w_in, q_norm_g, k_norm_g, conv_w, conv_b, w_out, ffn_w1, ffn_w2, loss_target, m_w_ada, m_b_ada, m_norm_g, m_w_in, m_q_norm_g, m_k_norm_g, m_conv_w, m_conv_b, m_w_out, m_ffn_w1, m_ffn_w2, v_w_ada, v_b_ada, v_norm_g, v_w_in, v_q_norm_g, v_k_norm_g, v_conv_w, v_conv_b, v_w_out, v_ffn_w1, v_ffn_w2):
    _, t_len, d_model = x.shape
    n_layer = w_ada.shape[0]
    a_w = d_model // 2
    c_w = d_model - a_w
    n_head = a_w // HEAD_DIM
    ada_cols = w_ada.shape[-1]
    tm = min(TOKEN_TILE, t_len)
    tm_ffn = min(FFN_FWD_TOKEN_TILE, t_len)
    tm_ffn_bwd = min(FFN_BWD_TOKEN_TILE, t_len)
    me = _index(_my_place())
    x2 = x.reshape(t_len, d_model)
    target = loss_target.reshape(t_len, d_model)

    small_shapes = [(d_model,), norm_g.shape, conv_w.shape]
    gathered = _allgather_small(_pack([c, norm_g, conv_w]), "gather_small_inputs")
    c_all, ng_parts, cw_parts = _unpack(gathered, small_shapes)
    norm_g_full = jnp.moveaxis(ng_parts, 0, 2).reshape(n_layer, 3, d_model)
    conv_w_full = jnp.moveaxis(cw_parts, 0, 2).reshape(n_layer, 3, c_w)

    b_loc = lax.dynamic_slice_in_dim(b_ada, me * ada_cols, ada_cols, axis=1).reshape(n_layer, 1, ada_cols)
    mod_part, silu_c = _mod_part(c_all, w_ada, b_loc)
    mod_all = _allgather_small(_pack([mod_part]), "gather_mod")
    mod_all = _unpack(mod_all, [mod_part.shape])[0]
    mod_mine = lax.dynamic_index_in_dim(mod_all, me, axis=2, keepdims=False)
    mod = jnp.moveaxis(mod_mine, 0, 1).reshape(n_layer, 3, 3, d_model)

    def prm_of(l, sub):
        rows = jnp.stack([norm_g_full[l, sub], mod[l, sub, 0], mod[l, sub, 1], mod[l, sub, 2]])
        return jnp.pad(rows, ((0, 4), (0, 0)))

    prm = [[prm_of(l, sub) for sub in range(3)] for l in range(n_layer)]
    qkg = [jnp.pad(jnp.stack([jnp.tile(q_norm_g[l], n_head), jnp.tile(k_norm_g[l], n_head)]), ((0, 6), (0, 0)))
           for l in range(n_layer)]
    cprm = [jnp.pad(jnp.concatenate([conv_w_full[l], conv_b[l][None]]), ((0, 4), (0, 0))) for l in range(n_layer)]
    head_of = jnp.arange(min(MXU_WIDTH, a_w)) // HEAD_DIM
    bd = (head_of[:, None] == head_of[None, :]).astype(BF16)
    expand = (jnp.arange(LANES)[:, None] == (jnp.arange(a_w)[None, :] // HEAD_DIM) * (LANES // n_head)).astype(BF16)

    f8 = ffn_w1.shape[-1]
    r8 = ffn_w2.shape[-2]
    c8 = w_in.shape[-1]
    o8 = w_out.shape[-2]
    me_arr = jnp.reshape(me, (1,)).astype(jnp.int32)
    win_b = [_place_own(w_in, (l,), me_arr, f"own_w_in_{l}") for l in range(n_layer)]
    wout_b = [_place_own(w_out, (l,), me_arr, f"own_w_out_{l}") for l in range(n_layer)]
    w1_b = [[_place_own(ffn_w1, (l, s), me_arr, f"own_w1_{l}_{s}") for s in range(2)] for l in range(n_layer)]
    w2_b = [[_place_own(ffn_w2, (l, s), me_arr, f"own_w2_{l}_{s}") for s in range(2)] for l in range(n_layer)]
    w1_first, w2_first = _allgather_big([w1_b[0][0], w2_b[0][0]], "gather_first_weights")
    groups = [[win_b[0], wout_b[0]], [w1_b[0][1], w2_b[0][1]]]
    for l in range(1, n_layer):
        groups += [[w1_b[l][0], w2_b[l][0]], [win_b[l], wout_b[l]], [w1_b[l][1], w2_b[l][1]]]
    flat = [a for grp in groups for a in grp]
    w_sends, w_recvs, w_lands, w_token = _send_start(flat, None, w1_first, "weights_start")

    def gathered(gi, after):
        lo = sum(len(grp) for grp in groups[:gi])
        hi = lo + len(groups[gi])
        return _send_wait(w_lands[lo:hi], w_sends[lo:hi], w_recvs[lo:hi], after, f"weights_wait_{gi}")

    def chunked(w1g):
        return w1g.reshape(2, N_CHUNK, d_model, f8)

    saved = []
    xc = x2
    for l in range(n_layer):
        keep = {"x0": xc}
        if l == 0:
            w1g, w2g = w1_first, w2_first
            prm_first = prm[0][0] + w_token[0, 0]
        else:
            w1g, w2g = gathered(3 * l - 1, xc)
            prm_first = prm[l][0]
        keep["w_ffn0"] = (chunked(w1g), w2g)
        keep["act0"], keep["h0"], keep["y0"], xc = _ffn_fwd(xc, prm_first, *keep["w_ffn0"], l, 0, tm_ffn)
        keep["x1"] = xc
        wing, woutg = gathered(0 if l == 0 else 3 * l, xc)
        keep.update(wing=wing, woutg=woutg)
        proj, qkn, keep["h1"], ycv, *res = _mixer_in(xc, prm[l][1], wing, qkg[l], cprm[l], bd, l, tm)
        qkv = [(qkn[None], proj[None], 2)] + [(res[2 * i], res[2 * i + 1], 0) for i in range(len(DILATIONS) - 1)]
        branches = [_attn_fwd(*t, l) for t in qkv]
        cat, lse, *res = _attn_combine([b[0] for b in branches], [b[1] for b in branches], ycv, expand, tm, l)
        n_r = len(DILATIONS) - 1
        keep["ya"] = [cat[None]] + res[:n_r]
        keep["lse"] = [lse[None]] + res[n_r:]
        xc, keep["ymix"] = _mixer_out(cat, xc, prm[l][1], woutg, l, tm)
        keep.update(proj=proj, qkv=qkv, cat=cat)
        keep["x2"] = xc
        w1g, w2g = gathered(1 if l == 0 else 3 * l + 1, xc)
        keep["w_ffn2"] = (chunked(w1g), w2g)
        keep["act2"], keep["h2"], keep["y2"], xc = _ffn_fwd(xc, prm[l][2], *keep["w_ffn2"], l, 1, tm_ffn)
        saved.append(keep)

    dx, loss_blk = _loss_head(xc, target, tm)
    loss = lax.psum(loss_blk[0, 0], ("x", "y", "c"))

    tk = min(GRAD_TOKEN_TILE, t_len)
    stats =[[None] * 3 for _ in range(n_layer)]
    conv_stats = [None] * n_layer
    qk_stats = [None] * n_layer

    started = {}
    dummy = jnp.zeros((8, LANES), F32)

    def start_exchange(g, name):
        sends, recvs, thru, token = _send_start(
            [_place_own(g, None, me_arr, f"own_grad_{name}")], [g], dummy, f"grads_start_{name}")
        started[name] = (thru[0], thru[1], sends[0], recvs[0], token)
        return token

    def wait_exchange(names, after, name):
        ent = [started[k] for k in names]
        done = _send_wait([e[0] for e in ent] + [e[1] for e in ent], [e[2] for e in ent], [e[3] for e in ent],
                          after, name)
        return dict(zip(names, done))

    def ffn_backward(dxo, keep, l, sub, which, prm_dep):
        dx_in, da, dy, st = _ffn_bwd(dxo, keep[f"y{which}"], keep[f"x{which}"], prm_dep,
                                     keep[f"act{which}"], *keep[f"w_ffn{which}"], l, sub, tm_ffn_bwd)
        g_w2 = _matmul_tn(
            keep[f"act{which}"], dy,
            pl.BlockSpec((None, None, tk, f8), lambda p, k: (2, p, k, 0)),
            pl.BlockSpec((tk, d_model), lambda p, k: (k, 0)),
            SDS((N_CHUNK, f8, d_model), BF16), pl.BlockSpec((None, f8, d_model), lambda p, k: (p, 0, 0)),
            N_CHUNK, t_len, tk, f"grad_w2_{l}_{sub}", dummy)
        token = start_exchange(g_w2.reshape(N_DEV, r8, d_model), f"w2_{l}_{sub}")
        g_w1 = _matmul_tn(
            da.reshape(N_DEV, t_len, f8), keep[f"h{which}"],
            pl.BlockSpec((None, tk, f8), lambda p, k: (p, k, 0)),
            pl.BlockSpec((tk, d_model), lambda p, k: (k, 0)),
            SDS((N_DEV, f8, d_model), BF16), pl.BlockSpec((None, f8, d_model), lambda p, k: (p, 0, 0)),
            N_DEV, t_len, tk, f"grad_w1_{l}_{sub}", token)
        token = start_exchange(g_w1, f"w1_{l}_{sub}")
        return dx_in, st, token[0, 0]

    token = 0.0
    for l in reversed(range(n_layer)):
        keep = saved[l]
        wing, woutg = keep["wing"], keep["woutg"]
        dx, stats[l][2], token = ffn_backward(dx, keep, l, 1, 2, prm[l][2] + token)
        dya, dpc, dy, st_gate, conv_stats[l], *dya_res = _mixer_bwd_out(
            dx, keep["ymix"], prm[l][1] + token, woutg, keep["proj"], cprm[l], l, tm)
        dyas = [dya[None]] + dya_res
        g_wout = _matmul_tn(
            keep["cat"], dy,
            pl.BlockSpec((tk, d_model), lambda p, k: (k, 0)),
            pl.BlockSpec((tk, d_model), lambda p, k: (k, 0)),
            SDS((d_model, d_model), BF16), pl.BlockSpec((d_model, d_model), lambda p, k: (0, 0)),
            1, t_len, tk, f"grad_wout_{l}", dummy)
        token = start_exchange(g_wout.reshape(N_DEV, o8, d_model), f"wout_{l}")[0, 0]
        parts = [_attn_bwd(*keep["qkv"][i], dyas[i], keep["ya"][i], keep["lse"][i], l)
                 for i in range(len(DILATIONS))]
        dx, dproj, st_norm, qk_stats[l] = _mixer_bwd_in(
            parts, keep["proj"], dpc, dx, keep["x1"], prm[l][1] + token, wing, qkg[l], bd, l, tm)
        stats[l][1] = st_norm + st_gate
        g_win = _matmul_tn(
            keep["h1"], dproj,
            pl.BlockSpec((tk, d_model), lambda p, k: (k, 0)),
            pl.BlockSpec((tk, N_DEV // 2 * c8), lambda p, k: (k, p)),
            SDS((N_DEV, d_model, c8), BF16), pl.BlockSpec((N_DEV // 2, d_model, c8), lambda p, k: (p, 0, 0)),
            2, t_len, tk, f"grad_win_{l}", dummy)
        token = start_exchange(g_win, f"win_{l}")[0, 0]
        dx, stats[l][0], token = ffn_backward(dx, keep, l, 0, 0, prm[l][0] + token)
    grad_x = dx.reshape(x.shape)

    def update(pieces, w, m, v, name):
        shape = w.shape
        cols = shape[-1]
        outs = _adamw([p.reshape(p.shape[0], -1, cols) for p in pieces], w.reshape(-1, cols),
                      m.reshape(-1, cols), v.reshape(-1, cols), name)
        return [o.reshape(shape) for o in outs]

    last = "w1_0_0"
    lands = wait_exchange([k for k in started if k != last], started[last][4], "grads_wait_early")
    res = {
        "w_in": update([lands[f"win_{l}"] for l in range(n_layer)], w_in, m_w_in, v_w_in, "adamw_w_in"),
        "w_out": update([lands[f"wout_{l}"] for l in range(n_layer)], w_out, m_w_out, v_w_out, "adamw_w_out"),
        "ffn_w2": update([lands[f"w2_{l}_{s}"] for l in range(n_layer) for s in range(2)],
                         ffn_w2, m_ffn_w2, v_ffn_w2, "adamw_ffn_w2"),
    }

    dmod = jnp.stack([jnp.stack([stats[l][sub][r] for sub in range(3) for r in (0, 1, 3)]) for l in range(n_layer)])
    dng = jnp.stack([jnp.stack([stats[l][sub][2] for sub in range(3)]) for l in range(n_layer)])
    dcw = jnp.stack([conv_stats[l][0:3] for l in range(n_layer)])
    dcb = jnp.stack([conv_stats[l][3] for l in range(n_layer)])
    dqk = jnp.stack([qk_stats[l][0:2].reshape(2, n_head, HEAD_DIM).sum(axis=1) for l in range(n_layer)])
    part_shapes = [dmod.shape, dng.shape, dcw.shape, dcb.shape, dqk.shape]
    parts_all = _allgather_small(_pack([dmod, dng, dcw, dcb, dqk]), "gather_small_grads")
    dmod_all = _unpack(parts_all, part_shapes)[0].reshape(N_DEV, n_layer, 9 * d_model)
    summed = _unpack(_sum_devices(parts_all), part_shapes)
    g_b_ada = summed[0].reshape(n_layer, 9 * d_model)
    ng_cols = norm_g.shape[-1]
    g_norm_g = lax.dynamic_slice_in_dim(summed[1], me * ng_cols, ng_cols, axis=2)
    cw_cols = conv_w.shape[-1]
    g_conv_w = lax.dynamic_slice_in_dim(summed[2], me * cw_cols, cw_cols, axis=2)
    g_conv_b = summed[3]
    g_q, g_k = summed[4][:, 0], summed[4][:, 1]
    dmod_loc = lax.dynamic_slice_in_dim(dmod_all, me * ada_cols, ada_cols, axis=2)
    g_w_ada = _w_ada_grad(silu_c.T, jnp.moveaxis(dmod_loc, 0, 1))
    res["w_ada"] = update([g_w_ada[None]], w_ada, m_w_ada, v_w_ada, "adamw_w_ada")

    names = ["b_ada", "norm_g", "q_norm_g", "k_norm_g", "conv_w", "conv_b"]
    g_small = [g_b_ada, g_norm_g, g_q, g_k, g_conv_w, g_conv_b]
    w_small = [b_ada, norm_g, q_norm_g, k_norm_g, conv_w, conv_b]
    m_small = [m_b_ada, m_norm_g, m_q_norm_g, m_k_norm_g, m_conv_w, m_conv_b]
    v_small = [v_b_ada, v_norm_g, v_q_norm_g, v_k_norm_g, v_conv_w, v_conv_b]
    outs = _adamw([_pack(g_small)[None]], _pack(w_small), _pack(m_small), _pack(v_small), "adamw_small")
    shapes = [w.shape for w in w_small]
    unpacked = [_unpack(o, shapes) for o in outs]
    for i, nme in enumerate(names):
        res[nme] = [unpacked[k][i] for k in range(4)]

    follows = (res["w_in"][1][0, :1, :1] + res["w_out"][1][0, :1, :1] + res["ffn_w2"][1][0, 0, :1, :1]
               + res["w_ada"][1][0, :1, :1] + outs[1][:1, :1])
    lands.update(wait_exchange([last], follows, "grads_wait_last"))
    transposed = update([lands[f"w1_{l}_{s}"] for l in range(n_layer) for s in range(2)],
                        *[jnp.swapaxes(t, -1, -2) for t in (ffn_w1, m_ffn_w1, v_ffn_w1)], "adamw_ffn_w1")
    res["ffn_w1"] = [jnp.swapaxes(t, -1, -2) for t in transposed]

    order = ["w_ada", "b_ada", "norm_g", "w_in", "q_norm_g", "k_norm_g", "conv_w", "conv_b", "w_out", "ffn_w1", "ffn_w2"]
    return (loss, grad_x, *[res[n][0] for n in order], *[res[n][1] for n in order],
            *[res[n][2] for n in order], *[res[n][3] for n in order])
```

```python
import jax
import jax.numpy as jnp
from jax import lax
from jax.experimental import pallas as pl
from jax.experimental.pallas import tpu as pltpu

F32 = jnp.float32
BF16 = jnp.bfloat16
SDS = jax.ShapeDtypeStruct

N_DEV = 8
HEAD_DIM = 64
BLOCK = 128
DILATIONS = (1, 4, 16)
ATTN_BLOCKS_PER_STEP = 4
N_CHUNK = 4
EPS = 1e-6
NEG = -1e30
SM_SCALE = HEAD_DIM ** -0.5
LANES = 128
MXU_WIDTH = 256
TOKEN_TILE = 512
FFN_ROW_PART = 256
FFN_FWD_TOKEN_TILE = 256
FFN_BWD_TOKEN_TILE = 256
GRAD_TOKEN_TILE = 2048
VMEM_LIMIT_BYTES = 56 * 1024 * 1024

ADAM_LR = 0.001
ADAM_B1 = 0.9
ADAM_B2 = 0.999
ADAM_EPS = 1e-08
ADAM_WD = 0.01
ADAM_STEP = 10

MESH_ID = pl.DeviceIdType.MESH
ANY = pl.BlockSpec(memory_space=pl.ANY)
HBM_SPEC = pl.BlockSpec(memory_space=pltpu.HBM)
SEM_SPEC = pl.BlockSpec(memory_space=pltpu.SEMAPHORE)
SIDE_EFFECT = pltpu.SideEffectType.DATAFLOW_SIDE_EFFECTING


def _params(n_axes):
    return pltpu.CompilerParams(dimension_semantics=("arbitrary",) * n_axes,
                                vmem_limit_bytes=VMEM_LIMIT_BYTES)


def _dot(a, b):
    return jnp.dot(a, b, preferred_element_type=F32)


def _dot_nt(a, b):
    return lax.dot_general(a, b, (((1,), (1,)), ((), ())), preferred_element_type=F32)


def _dot_tn(a, b):
    return lax.dot_general(a, b, (((0,), (0,)), ((), ())), preferred_element_type=F32)


def _row_halves(n):
    return (slice(0, n // 2), slice(n // 2, n))


def _split_bf16(t):
    hi = t.astype(BF16)
    return hi, (t - hi.astype(F32)).astype(BF16)


def _my_place():
    x, y, c = lax.axis_index("x"), lax.axis_index("y"), lax.axis_index("c")
    return x, y, c


def _peer(place, k):
    x, y, c = place
    return ((1 - x) if k & 4 else x, (1 - y) if k & 2 else y, (1 - c) if k & 1 else c)


def _index(place):
    return 4 * place[0] + 2 * place[1] + place[2]


def _allgather_small(v, name):
    rows, cols = v.shape

    def body(x_ref, out_ref, send_sems, recv_sems, local_sem):
        me = _my_place()
        mine = pltpu.make_async_copy(x_ref, out_ref.at[_index(me)], local_sem)
        mine.start()
        sends = []
        for k in range(1, N_DEV):
            cp = pltpu.make_async_remote_copy(
                src_ref=x_ref, dst_ref=out_ref.at[_index(me)], send_sem=send_sems.at[k - 1],
                recv_sem=recv_sems.at[k - 1], device_id=_peer(me, k), device_id_type=MESH_ID)
            cp.start()
            sends.append(cp)
        for k in range(1, N_DEV):
            pltpu.make_async_remote_copy(
                src_ref=x_ref, dst_ref=out_ref.at[_index(_peer(me, k))], send_sem=send_sems.at[k - 1],
                recv_sem=recv_sems.at[k - 1], device_id=_peer(me, k), device_id_type=MESH_ID).wait_recv()
        for cp in sends:
            cp.wait_send()
        mine.wait()

    return pl.pallas_call(
        body, name=name,
        out_shape=SDS((N_DEV, rows, cols), v.dtype),
        in_specs=[pl.BlockSpec(memory_space=pltpu.VMEM)],
        out_specs=pl.BlockSpec(memory_space=pltpu.VMEM),
        scratch_shapes=[pltpu.SemaphoreType.DMA((N_DEV - 1,)), pltpu.SemaphoreType.DMA((N_DEV - 1,)),
                        pltpu.SemaphoreType.DMA],
    )(v)


def _allgather_big(lands, name):
    n = len(lands)

    def body(*refs):
        outs = refs[n:2 * n]
        send_sems, recv_sems = refs[2 * n:]
        x, y, c = _my_place()
        me, sib = (x, y, c), (x, y, 1 - c)
        chips = [(1 - x, y), (x, 1 - y), (1 - x, 1 - y)]

        def copy(i, k, block, to):
            dst = outs[i].at[_index(block)]
            return pltpu.make_async_remote_copy(
                src_ref=dst, dst_ref=dst, send_sem=send_sems.at[7 * i + k],
                recv_sem=recv_sems.at[7 * i + k], device_id=to, device_id_type=MESH_ID)

        first = []
        for i in range(n):
            first.append(copy(i, 0, me, sib))
            for j, chip in enumerate(chips):
                first.append(copy(i, 1 + j, me, (*chip, c)))
        for cp in first:
            cp.start()
        passed = []
        for i in range(n):
            for j, chip in enumerate(chips):
                copy(i, 1 + j, (*chip, c), me).wait_recv()
                fwd = copy(i, 4 + j, (*chip, c), sib)
                fwd.start()
                passed.append(fwd)
        for i in range(n):
            copy(i, 0, sib, me).wait_recv()
            for j, chip in enumerate(chips):
                copy(i, 4 + j, (*chip, 1 - c), me).wait_recv()
        for cp in first + passed:
            cp.wait_send()

    return pl.pallas_call(
        body, name=name,
        out_shape=[SDS(a.shape, a.dtype) for a in lands],
        in_specs=[ANY] * n, out_specs=[ANY] * n,
        input_output_aliases={i: i for i in range(n)},
        scratch_shapes=[pltpu.SemaphoreType.DMA((7 * n,)), pltpu.SemaphoreType.DMA((7 * n,))],
    )(*lands)


def _place_own(src, lead, me_arr, name):
    rows, cols = src.shape[-2:]
    tr = _row_tile(rows, cols)

    def body(me_ref, s_ref, o_ref):
        o_ref[...] = s_ref[...].astype(BF16)

    if lead is None:
        in_spec = pl.BlockSpec((None, tr, cols), lambda i, me_ref: (me_ref[0], i, 0))
    else:
        in_spec = pl.BlockSpec((None,) * len(lead) + (tr, cols), lambda i, me_ref: (*lead, i, 0))
    return pl.pallas_call(
        body, name=name,
        grid_spec=pltpu.PrefetchScalarGridSpec(
            num_scalar_prefetch=1, grid=(rows // tr,), in_specs=[in_spec],
            out_specs=pl.BlockSpec((None, tr, cols), lambda i, me_ref: (me_ref[0], i, 0))),
        out_shape=SDS((N_DEV, rows, cols), BF16),
        compiler_params=_params(1),
    )(me_arr, src)


def _send_start(lands, pieces, after, name):
    n = len(lands)
    arrs = list(lands) + list(pieces or [])
    n_a = len(arrs)

    def body(*refs):
        land_r, piece_r = refs[:n], refs[n:n_a]
        send, recv = refs[n_a + 1:n_a + 1 + n], refs[n_a + 1 + n:n_a + 1 + 2 * n]
        token = refs[-1]
        me = _my_place()
        for i in range(n):
            for k in range(1, N_DEV):
                peer = _peer(me, k)
                src = piece_r[i].at[_index(peer)] if pieces else land_r[i].at[_index(me)]
                pltpu.make_async_remote_copy(
                    src_ref=src, dst_ref=land_r[i].at[_index(me)], send_sem=send[i], recv_sem=recv[i],
                    device_id=peer, device_id_type=MESH_ID).start()
        token[...] = jnp.zeros_like(token)

    outs = pl.pallas_call(
        body, name=name,
        out_shape=[pltpu.SemaphoreType.DMA(())] * (2 * n) + [pltpu.HBM(a.shape, a.dtype) for a in arrs]
        + [SDS((8, LANES), F32)],
        in_specs=[HBM_SPEC] * n_a + [ANY],
        out_specs=[SEM_SPEC] * (2 * n) + [HBM_SPEC] * n_a + [pl.BlockSpec(memory_space=pltpu.VMEM)],
        input_output_aliases={i: 2 * n + i for i in range(n_a)},
        compiler_params=pltpu.CompilerParams(has_side_effects=SIDE_EFFECT),
    )(*[pltpu.with_memory_space_constraint(a, pltpu.HBM) for a in arrs], after)
    return outs[:n], outs[n:2 * n], outs[2 * n:2 * n + n_a], outs[-1]


def _send_wait(thru, sends, recvs, after, name):
    n = len(sends)
    n_a = len(thru)

    def body(*refs):
        land_r = refs[:n]
        send, recv = refs[n_a:n_a + n], refs[n_a + n:n_a + 2 * n]
        me = _my_place()
        for i in range(n):
            seven = land_r[i].at[pl.ds(0, N_DEV - 1)]
            cp = pltpu.make_async_remote_copy(src_ref=seven, dst_ref=seven, send_sem=send[i], recv_sem=recv[i],
                                              device_id=_peer(me, 1), device_id_type=MESH_ID)
            cp.wait_send()
            cp.wait_recv()

    outs = pl.pallas_call(
        body, name=name,
        out_shape=[pltpu.HBM(a.shape, a.dtype) for a in thru],
        in_specs=[HBM_SPEC] * n_a + [SEM_SPEC] * (2 * n) + [ANY],
        out_specs=[HBM_SPEC] * n_a,
        input_output_aliases={i: i for i in range(n_a)},
        compiler_params=pltpu.CompilerParams(has_side_effects=SIDE_EFFECT),
    )(*thru, *sends, *recvs, after)
    return outs[:n]


def _ada_fwd(x, prm_ref):
    rstd = lax.rsqrt(jnp.mean(x * x, axis=-1, keepdims=True) + EPS)
    return (x * rstd * prm_ref[0:1, :]) * (1.0 + prm_ref[2:3, :]) + prm_ref[1:2, :]


def _ada_bwd(dh, x, prm_ref):
    ng, ops = prm_ref[0:1, :], 1.0 + prm_ref[2:3, :]
    rstd = lax.rsqrt(jnp.mean(x * x, axis=-1, keepdims=True) + EPS)
    xhat = x * rstd
    dxhat = dh * (ops * ng)
    dx = rstd * (dxhat - xhat * jnp.mean(dxhat * xhat, axis=-1, keepdims=True))
    dshift = jnp.sum(dh, axis=0, keepdims=True)
    dscale = jnp.sum(dh * (xhat * ng), axis=0, keepdims=True)
    dng = jnp.sum(dh * (xhat * ops), axis=0, keepdims=True)
    return dx, dshift, dscale, dng


def _head_mean(t, bd):
    hi = t.astype(BF16)
    g = bd.shape[0]
    parts = [_dot(hi[:, i:i + g], bd) for i in range(0, t.shape[1], g)]
    return jnp.concatenate(parts, axis=1) * (1.0 / HEAD_DIM)


def _join_columns(w_hbm, dst_ref, sems):
    cols = w_hbm.shape[2]
    copies = [pltpu.make_async_copy(w_hbm.at[j], dst_ref.at[:, pl.ds(j * cols, cols)], sems.at[j])
              for j in range(w_hbm.shape[0])]
    for cp in copies:
        cp.start()
    for cp in copies:
        cp.wait()


def _shift_rows(z, before, tm):
    row = lax.broadcasted_iota(jnp.int32, z.shape, 0)
    z1 = jnp.where(row == 0, before[7:8, :], pltpu.roll(z, 1, 0))
    z2 = jnp.where(row == 0, before[6:7, :], jnp.where(row == 1, before[7:8, :], pltpu.roll(z, 2, 0)))
    return z1, z2


def _mod_part(c_all, w_ada, b_loc):
    n_layer, d_model, cols = w_ada.shape

    def body(c_ref, w_ref, b_ref, out_ref, sc_ref):
        cv = c_ref[...]
        sc = cv * jax.nn.sigmoid(cv)
        sc_ref[...] = sc
        a_hi, a_lo = _split_bf16(sc)
        w_hi, w_lo = _split_bf16(w_ref[...])
        out_ref[...] = _dot(a_hi, w_hi) + _dot(a_hi, w_lo) + _dot(a_lo, w_hi) + b_ref[...]

    return pl.pallas_call(
        body, name="mod_part", grid=(n_layer,),
        in_specs=[pl.BlockSpec((N_DEV, d_model), lambda l: (0, 0)),
                  pl.BlockSpec((None, d_model, cols), lambda l: (l, 0, 0)),
                  pl.BlockSpec((None, 1, cols), lambda l: (l, 0, 0))],
        out_specs=[pl.BlockSpec((None, N_DEV, cols), lambda l: (l, 0, 0)),
                   pl.BlockSpec((N_DEV, d_model), lambda l: (0, 0))],
        out_shape=[SDS((n_layer, N_DEV, cols), F32), SDS((N_DEV, d_model), F32)],
        compiler_params=_params(1),
    )(c_all, w_ada, b_loc)


def _w_ada_grad(sc_t, dmod_loc):
    d_model = sc_t.shape[0]
    n_layer, _, cols = dmod_loc.shape

    def body(s_ref, d_ref, out_ref):
        acc = s_ref[:, 0:1] * d_ref[0:1, :]
        for b in range(1, N_DEV):
            acc = acc + s_ref[:, b:b + 1] * d_ref[b:b + 1, :]
        out_ref[...] = acc

    return pl.pallas_call(
        body, name="w_ada_grad", grid=(n_layer,),
        in_specs=[pl.BlockSpec((d_model, N_DEV), lambda l: (0, 0)),
                  pl.BlockSpec((None, N_DEV, cols), lambda l: (l, 0, 0))],
        out_specs=pl.BlockSpec((None, d_model, cols), lambda l: (l, 0, 0)),
        out_shape=SDS((n_layer, d_model, cols), F32),
        compiler_params=_params(1),
    )(sc_t, dmod_loc)


def _sum_devices(g):
    _, rows, cols = g.shape

    def body(g_ref, out_ref):
        acc = g_ref[0]
        for i in range(1, N_DEV):
            acc = acc + g_ref[i]
        out_ref[...] = acc

    return pl.pallas_call(body, name="sum_devices", out_shape=SDS((rows, cols), F32))(g)


def _ffn_fwd(x, prm, w1r, w2g, l, sub, tm):
    t_len, d_model = x.shape
    f8, r8 = w1r.shape[-1], w2g.shape[-2]
    n_m = t_len // tm

    def body(x_ref, xn_ref, prm_ref, w1_ref, w2_ref, act_ref, h_ref, y_ref, xo_ref, h_sc):
        @pl.when(pl.program_id(0) == 0)
        def _():
            h_sc[...] = _ada_fwd(x_ref[...], prm_ref).astype(BF16)

        h = h_sc[...]
        h_ref[...] = h

        def first(j):
            return _dot(h, w1_ref[0, j]), _dot(h, w1_ref[1, j])

        gu = first(0)
        yv = None
        for j in range(N_CHUNK):
            g, u = gu
            if j + 1 < N_CHUNK:
                gu = first(j + 1)
            else:
                h_next = _ada_fwd(xn_ref[...], prm_ref).astype(BF16)
            sig = jax.nn.sigmoid(g)
            sil = g * sig
            s = (sil * u).astype(BF16)
            act_ref[0, j] = (u * (sig * (1.0 + g * (1.0 - sig)))).astype(BF16)
            act_ref[1, j] = sil.astype(BF16)
            act_ref[2, j] = s
            part = _dot(s, w2_ref[2 * j:2 * j + 2].reshape(2 * r8, d_model))
            yv = part if yv is None else yv + part
        y_ref[...] = yv.astype(BF16)
        xo_ref[...] = x_ref[...] + (0.5 * prm_ref[3:4, :]) * yv
        h_sc[...] = h_next

    tile = pl.BlockSpec((tm, d_model), lambda m: (m, 0))
    return pl.pallas_call(
        body, name=f"ffn_fwd_{l}_{sub}", grid=(n_m,),
        in_specs=[tile, pl.BlockSpec((tm, d_model), lambda m: (jnp.minimum(m + 1, n_m - 1), 0)),
                  pl.BlockSpec((8, d_model), lambda m: (0, 0)),
                  pl.BlockSpec((2, N_CHUNK, d_model, f8), lambda m: (0, 0, 0, 0), pipeline_mode=pl.Buffered(1)),
                  pl.BlockSpec((N_DEV, r8, d_model), lambda m: (0, 0, 0), pipeline_mode=pl.Buffered(1))],
        out_specs=[pl.BlockSpec((3, N_CHUNK, tm, f8), lambda m: (0, 0, m, 0)), tile, tile, tile],
        out_shape=[SDS((3, N_CHUNK, t_len, f8), BF16), SDS((t_len, d_model), BF16),
                   SDS((t_len, d_model), BF16), SDS((t_len, d_model), F32)],
        scratch_shapes=[pltpu.VMEM((tm, d_model), BF16)],
        compiler_params=_params(1),
    )(x, x, prm, w1r, w2g)


def _ffn_bwd(dxo, y, x_in, prm, act, w1r, w2g, l, sub, tm):
    t_len, d_model = x_in.shape
    f8, r8 = w1r.shape[-1], w2g.shape[-2]
    n_m = t_len // tm

    def body(dxo_ref, y_ref, x_ref, prm_ref, act_ref, w1_ref, w2_ref, dx_ref, da_ref, dy_ref, st_ref):
        @pl.when(pl.program_id(0) == 0)
        def _():
            st_ref[...] = jnp.zeros_like(st_ref)

        dxo_v = dxo_ref[...]
        dy = ((0.5 * prm_ref[3:4, :]) * dxo_v).astype(BF16)
        dy_ref[...] = dy
        st_ref[3:4, :] += jnp.sum(0.5 * dxo_v * y_ref[...].astype(F32), axis=0, keepdims=True)

        def first(j):
            return _dot_nt(dy, w2_ref[2 * j:2 * j + 2].reshape(2 * r8, d_model))

        ds = first(0)
        dh = None
        for j in range(N_CHUNK):
            ds_now = ds
            if j + 1 < N_CHUNK:
                ds = first(j + 1)
            dg = (ds_now * act_ref[0, j].astype(F32)).astype(BF16)
            du = (ds_now * act_ref[1, j].astype(F32)).astype(BF16)
            da_ref[0, j] = dg
            da_ref[1, j] = du
            part = _dot_nt(dg, w1_ref[0, j]) + _dot_nt(du, w1_ref[1, j])
            dh = part if dh is None else dh + part
        dx, dshift, dscale, dng = _ada_bwd(dh, x_ref[...], prm_ref)
        dx_ref[...] = dxo_v + dx
        st_ref[0:1, :] += dshift
        st_ref[1:2, :] += dscale
        st_ref[2:3, :] += dng

    tile = pl.BlockSpec((tm, d_model), lambda m: (m, 0))
    fixed = pl.BlockSpec((8, d_model), lambda m: (0, 0))
    return pl.pallas_call(
        body, name=f"ffn_bwd_{l}_{sub}", grid=(n_m,),
        in_specs=[tile, tile, tile, fixed,
                  pl.BlockSpec((2, N_CHUNK, tm, f8), lambda m: (0, 0, m, 0)),
                  pl.BlockSpec((2, N_CHUNK, d_model, f8), lambda m: (0, 0, 0, 0), pipeline_mode=pl.Buffered(1)),
                  pl.BlockSpec((N_DEV, r8, d_model), lambda m: (0, 0, 0), pipeline_mode=pl.Buffered(1))],
        out_specs=[tile, pl.BlockSpec((2, N_CHUNK, tm, f8), lambda m: (0, 0, m, 0)), tile, fixed],
        out_shape=[SDS((t_len, d_model), F32), SDS((2, N_CHUNK, t_len, f8), BF16),
                   SDS((t_len, d_model), BF16), SDS((8, d_model), F32)],
        compiler_params=_params(1),
    )(dxo, y, x_in, prm, act, w1r, w2g)


def _matmul_tn(a, b, a_spec, b_spec, out_shape, out_spec, n_piece, t_len, tk, name, after):
    def body(a_ref, b_ref, after_ref, o_ref, acc):
        k = pl.program_id(1)

        @pl.when(k == 0)
        def _():
            acc[...] = jnp.zeros_like(acc)

        m_out, n_out = acc.shape
        if (m_out // 2) % LANES == 0:
            for rows in _row_halves(m_out):
                acc[rows, :] += _dot_tn(a_ref[:, rows], b_ref[...])
        else:
            for cols in _row_halves(n_out):
                acc[:, cols] += _dot_tn(a_ref[...], b_ref[:, cols])

        @pl.when(k == pl.num_programs(1) - 1)
        def _():
            if len(o_ref.shape) == 3:
                cols = o_ref.shape[2]
                for q in range(o_ref.shape[0]):
                    o_ref[q] = acc[:, q * cols:(q + 1) * cols].astype(o_ref.dtype)
            else:
                o_ref[...] = acc[...].astype(o_ref.dtype)

    blk = tuple(d for d in out_spec.block_shape if d is not None)
    acc_shape = blk if len(blk) == 2 else (blk[1], blk[0] * blk[2])
    return pl.pallas_call(
        body, name=name, grid=(n_piece, t_len // tk),
        in_specs=[a_spec, b_spec, ANY], out_specs=out_spec, out_shape=out_shape,
        scratch_shapes=[pltpu.VMEM(acc_shape, F32)],
        compiler_params=_params(2),
    )(a, b, after)


def _groups(tm, width):
    return pltpu.VMEM((width // LANES, tm, LANES), F32)


def _to_groups(val, dst_ref, first=0):
    for g in range(val.shape[1] // LANES):
        dst_ref[first + g] = val[:, g * LANES:(g + 1) * LANES]


def _from_groups(src_ref):
    return jnp.concatenate([src_ref[g] for g in range(src_ref.shape[0])], axis=1)


def _split_residues(src_ref, out_ref, dil, tm):
    for g in range(src_ref.shape[0]):
        for r in range(dil):
            out_ref[r, :, g * LANES:(g + 1) * LANES] = (
                src_ref.at[g][pl.ds(r, tm // dil, stride=dil), :].astype(out_ref.dtype))


def _merge_residues(in_ref, dst_ref, dil, tm, add=False):
    for g in range(dst_ref.shape[0]):
        for r in range(dil):
            rows = pl.ds(r, tm // dil, stride=dil)
            val = in_ref[r, :, g * LANES:(g + 1) * LANES].astype(F32)
            dst = dst_ref.at[g]
            dst[rows, :] = (dst[rows, :] + val) if add else val


def _res_spec(dil, tm, width, index=lambda m: m):
    return pl.BlockSpec((dil, tm // dil, width), lambda m: (0, index(m), 0))


def _mixer_in(x, prm, wing, qkg, cprm, bd, l, tm):
    t_len, d_model = x.shape
    a_w = d_model // 2
    c_w = d_model - a_w
    c8 = wing.shape[-1]
    p_c = N_DEV * c8
    dils = DILATIONS[1:]

    def body(x_ref, prm_ref, w_ref, qkg_ref, cprm_ref, bd_ref, proj_ref, qkn_ref, h_ref, ycv_ref, *rest):
        res_refs, (pf, qsc, vsc, carry, wcat, wsems) = rest[:2 * len(dils)], rest[2 * len(dils):]

        @pl.when(pl.program_id(0) == 0)
        def _():
            carry[...] = jnp.zeros_like(carry)
            _join_columns(w_ref, wcat, wsems)

        h = _ada_fwd(x_ref[...], prm_ref).astype(BF16)
        h_ref[...] = h
        for cols in _row_halves(p_c):
            pf[:, cols] = _dot(h, wcat[:, cols])
        proj_ref[...] = pf[...].astype(BF16)
        bdm = bd_ref[...]
        for i in range(2):
            t = pf[:, i * a_w:(i + 1) * a_w]
            rstd = lax.rsqrt(_head_mean(t * t, bdm) + EPS)
            tn = t * rstd * qkg_ref[i:i + 1, :]
            qkn_ref[:, i * a_w:(i + 1) * a_w] = tn.astype(BF16)
            _to_groups(tn, qsc, i * (a_w // LANES))
        _to_groups(pf[:, 2 * a_w:3 * a_w], vsc)
        for i, dil in enumerate(dils):
            _split_residues(qsc, res_refs[2 * i], dil, tm)
            _split_residues(vsc, res_refs[2 * i + 1], dil, tm)
        gb = pf[:, 3 * a_w:3 * a_w + c_w]
        z = pf[:, 3 * a_w + c_w:3 * a_w + 2 * c_w] * pf[:, 3 * a_w + 2 * c_w:3 * a_w + 3 * c_w]
        z1, z2 = _shift_rows(z, carry[...], tm)
        cv = cprm_ref[0:1, :] * z2 + cprm_ref[1:2, :] * z1 + cprm_ref[2:3, :] * z + cprm_ref[3:4, :]
        ycv_ref[...] = (gb * cv).astype(BF16)
        carry[...] = z[tm - 8:tm, :]

    res_specs, res_shapes = [], []
    for dil in dils:
        res_specs += [_res_spec(dil, tm, 2 * a_w), _res_spec(dil, tm, a_w)]
        res_shapes += [SDS((dil, t_len // dil, 2 * a_w), BF16), SDS((dil, t_len // dil, a_w), BF16)]
    return pl.pallas_call(
        body, name=f"mixer_in_{l}", grid=(t_len // tm,),
        in_specs=[pl.BlockSpec((tm, d_model), lambda m: (m, 0)),
                  pl.BlockSpec((8, d_model), lambda m: (0, 0)),
                  ANY,
                  pl.BlockSpec((8, a_w), lambda m: (0, 0)),
                  pl.BlockSpec((8, c_w), lambda m: (0, 0)),
                  pl.BlockSpec(bd.shape, lambda m: (0, 0))],
        out_specs=[pl.BlockSpec((tm, p_c), lambda m: (m, 0)),
                   pl.BlockSpec((tm, 2 * a_w), lambda m: (m, 0)),
                   pl.BlockSpec((tm, d_model), lambda m: (m, 0)),
                   pl.BlockSpec((tm, c_w), lambda m: (m, 0))] + res_specs,
        out_shape=[SDS((t_len, p_c), BF16), SDS((t_len, 2 * a_w), BF16),
                   SDS((t_len, d_model), BF16), SDS((t_len, c_w), BF16)] + res_shapes,
        scratch_shapes=[pltpu.VMEM((tm, p_c), F32), _groups(tm, 2 * a_w), _groups(tm, a_w),
                        pltpu.VMEM((8, c_w), F32), pltpu.VMEM((d_model, p_c), BF16),
                        pltpu.SemaphoreType.DMA((N_DEV,))],
        compiler_params=_params(1),
    )(x, prm, wing, qkg, cprm, bd)


def _band_mask(first_block):
    qi = lax.broadcasted_iota(jnp.int32, (BLOCK, 2 * BLOCK), 0)
    kj = lax.broadcasted_iota(jnp.int32, (BLOCK, 2 * BLOCK), 1)
    lowest = jnp.where(first_block, BLOCK, 0)
    return (kj >= qi) & (kj <= qi + BLOCK) & (kj >= lowest)


def _blocks_per_step(seq):
    n_sb = min(ATTN_BLOCKS_PER_STEP, seq // BLOCK)
    assert seq % (n_sb * BLOCK) == 0
    return n_sb


def _block_rows(sb):
    return slice(sb * BLOCK, (sb + 1) * BLOCK)


def _with_previous(prev_ref, cur_ref, sb, sl):
    before = prev_ref[:, sl] if sb == 0 else cur_ref[_block_rows(sb - 1), sl]
    return jnp.concatenate([before, cur_ref[_block_rows(sb), sl]], axis=0)


def _heads_on_lanes(cols):
    width = LANES // len(cols)
    lane = lax.broadcasted_iota(jnp.int32, (BLOCK, LANES), 1)
    out = jnp.broadcast_to(cols[-1], (BLOCK, LANES))
    for h in range(len(cols) - 2, -1, -1):
        out = jnp.where(lane < (h + 1) * width, cols[h], out)
    return out


def _attn_fwd(qk, v, v_col, l):
    dil, seq, a2 = qk.shape
    a_w = a2 // 2
    n_sb = _blocks_per_step(seq)
    rows_step = n_sb * BLOCK
    n_pair = a_w // LANES

    def body(q_ref, kc_ref, vc_ref, o_ref, lse_ref, kp_ref, vp_ref):
        @pl.when(pl.program_id(1) == 0)
        def _():
            kp_ref[...] = jnp.zeros_like(kp_ref)
            vp_ref[...] = jnp.zeros_like(vp_ref)

        masks = [_band_mask(pl.program_id(1) == 0)] + [_band_mask(False)] * (n_sb - 1)
        low = lax.broadcasted_iota(jnp.int32, (BLOCK, LANES), 1) < HEAD_DIM
        units = [(sb, slice(p * LANES, (p + 1) * LANES)) for sb in range(n_sb) for p in range(a_w // LANES)]
        scores = []
        for sb, sl in units:
            qp = q_ref[_block_rows(sb), sl]
            k2 = _with_previous(kp_ref, kc_ref, sb, sl)
            scores.append([_dot_nt(jnp.where(half, qp, jnp.zeros_like(qp)), k2) for half in (low, ~low)])
        probs, lses = [], []
        for (sb, sl), pair in zip(units, scores):
            pr, ls = [], []
            for s in pair:
                s = jnp.where(masks[sb], s * SM_SCALE, NEG)
                mx = jnp.max(s, axis=-1, keepdims=True)
                e = jnp.exp(s - mx)
                den = jnp.sum(e, axis=-1, keepdims=True)
                pr.append((e * (1.0 / den)).astype(BF16))
                ls.append(mx + jnp.log(den))
            probs.append(pr)
            lses.append(ls)
        for (sb, sl), pr, ls in zip(units, probs, lses):
            v2 = _with_previous(vp_ref, vc_ref, sb, sl)
            o_ref[_block_rows(sb), sl] = jnp.where(low, _dot(pr[0], v2), _dot(pr[1], v2)).astype(BF16)
        for sb in range(n_sb):
            cols = [c for ls in lses[sb * n_pair:(sb + 1) * n_pair] for c in ls]
            lse_ref[_block_rows(sb), :] = _heads_on_lanes(cols)
        kp_ref[...] = kc_ref[_block_rows(n_sb - 1), :]
        vp_ref[...] = vc_ref[_block_rows(n_sb - 1), :]

    def blk(f):
        return pl.BlockSpec((None, rows_step, a_w), f)

    return pl.pallas_call(
        body, name=f"attn_fwd_{l}_d{dil}", grid=(dil, seq // rows_step),
        in_specs=[blk(lambda r, b: (r, b, 0)),
                  blk(lambda r, b: (r, b, 1)),
                  blk(lambda r, b: (r, b, v_col))],
        out_specs=[blk(lambda r, b: (r, b, 0)), pl.BlockSpec((None, rows_step, LANES), lambda r, b: (r, b, 0))],
        out_shape=[SDS((dil, seq, a_w), BF16), SDS((dil, seq, LANES), F32)],
        scratch_shapes=[pltpu.VMEM((BLOCK, a_w), BF16), pltpu.VMEM((BLOCK, a_w), BF16)],
        compiler_params=_params(2),
    )(qk, qk, v)


def _attn_combine(outs, lses, ycv, expand, tm, l):
    t_len, c_w = ycv.shape
    a_w = outs[0].shape[-1]
    dils = DILATIONS[1:]
    n_r = len(dils)

    def body(o1, l1, *rest):
        o_res, l_res, y_ref, ex_ref = rest[:n_r], rest[n_r:2 * n_r], rest[2 * n_r], rest[2 * n_r + 1]
        cat_ref, lse_ref = rest[2 * n_r + 2], rest[2 * n_r + 3]
        ya_res, lse_res = rest[2 * n_r + 4:3 * n_r + 4], rest[3 * n_r + 4:4 * n_r + 4]
        scr = rest[4 * n_r + 4:]
        so, sl, sya, slse = scr[:n_r], scr[n_r:2 * n_r], scr[2 * n_r], scr[2 * n_r + 1]
        for i, dil in enumerate(dils):
            _merge_residues(o_res[i], so[i], dil, tm)
            _merge_residues(l_res[i], sl[i], dil, tm)
        ls = [l1[...]] + [s[0] for s in sl]
        mx = jnp.maximum(jnp.maximum(ls[0], ls[1]), ls[2])
        es = [jnp.exp(t - mx) for t in ls]
        den = es[0] + es[1] + es[2]
        inv = 1.0 / den
        slse[0] = mx + jnp.log(den)
        lse_ref[...] = slse[0]
        ex = ex_ref[...]
        ov = [o1[...].astype(F32)] + [_from_groups(s) for s in so]
        ya = jnp.zeros((tm, a_w), F32)
        for e, o in zip(es, ov):
            hi, lo = _split_bf16(e * inv)
            ya = ya + (_dot(hi, ex) + _dot(lo, ex)) * o
        _to_groups(ya, sya)
        cat_ref[:, :a_w] = ya.astype(BF16)
        cat_ref[:, a_w:] = y_ref[...]
        for i, dil in enumerate(dils):
            _split_residues(sya, ya_res[i], dil, tm)
            _split_residues(slse, lse_res[i], dil, tm)

    ta = pl.BlockSpec((tm, a_w), lambda m: (m, 0))
    tl = pl.BlockSpec((tm, LANES), lambda m: (m, 0))
    res_a = [_res_spec(dil, tm, a_w) for dil in dils]
    res_l = [_res_spec(dil, tm, LANES) for dil in dils]
    return pl.pallas_call(
        body, name=f"attn_combine_{l}", grid=(t_len // tm,),
        in_specs=[ta, tl] + res_a + res_l + [pl.BlockSpec((tm, c_w), lambda m: (m, 0)),
                                            pl.BlockSpec((LANES, a_w), lambda m: (0, 0))],
        out_specs=[pl.BlockSpec((tm, a_w + c_w), lambda m: (m, 0)), tl] + res_a + res_l,
        out_shape=[SDS((t_len, a_w + c_w), BF16), SDS((t_len, LANES), F32)]
        + [SDS((dil, t_len // dil, a_w), BF16) for dil in dils]
        + [SDS((dil, t_len // dil, LANES), F32) for dil in dils],
        scratch_shapes=[_groups(tm, a_w)] * n_r + [_groups(tm, LANES)] * n_r + [_groups(tm, a_w), _groups(tm, LANES)],
        compiler_params=_params(1),
    )(outs[0].reshape(t_len, a_w), lses[0].reshape(t_len, LANES), *outs[1:], *lses[1:], ycv, expand)


def _mixer_out(cat, x, prm, woutg, l, tm):
    t_len, d_model = x.shape
    r8 = woutg.shape[-2]

    def body(cat_ref, x_ref, prm_ref, w_ref, xo_ref, y_ref):
        w = w_ref[...].reshape(N_DEV * r8, d_model)
        for rows in _row_halves(tm):
            yv = _dot(cat_ref[rows, :], w)
            y_ref[rows, :] = yv.astype(BF16)
            xo_ref[rows, :] = x_ref[rows, :] + prm_ref[3:4, :] * yv

    tile = pl.BlockSpec((tm, d_model), lambda m: (m, 0))
    return pl.pallas_call(
        body, name=f"mixer_out_{l}", grid=(t_len // tm,),
        in_specs=[tile, tile, pl.BlockSpec((8, d_model), lambda m: (0, 0)),
                  pl.BlockSpec((N_DEV, r8, d_model), lambda m: (0, 0, 0))],
        out_specs=[tile, tile],
        out_shape=[SDS((t_len, d_model), F32), SDS((t_len, d_model), BF16)],
        compiler_params=_params(1),
    )(cat, x, prm, woutg)


def _mixer_bwd_out(dxo, ymix, prm, woutg, proj, cprm, l, tm):
    t_len, d_model = dxo.shape
    a_w = d_model // 2
    c_w = d_model - a_w
    r8 = woutg.shape[-2]
    n_m = t_len // tm
    hb = tm // 8

    dils = DILATIONS[1:]

    def body(dxo_ref, ym_ref, prm_ref, w_ref, pc_ref, halo_ref, cprm_ref,
             dya_ref, dpc_ref, dy_ref, st_ref, cs_ref, *rest):
        dya_res, (carry, dsc) = rest[:len(dils)], rest[len(dils):]
        step = pl.program_id(0)
        tile_i = n_m - 1 - step

        @pl.when(step == 0)
        def _():
            st_ref[...] = jnp.zeros_like(st_ref)
            cs_ref[...] = jnp.zeros_like(cs_ref)
            carry[...] = jnp.zeros_like(carry)

        dxo_v = dxo_ref[...]
        dy = (prm_ref[3:4, :] * dxo_v).astype(BF16)
        dy_ref[...] = dy
        st_ref[3:4, :] += jnp.sum(dxo_v * ym_ref[...].astype(F32), axis=0, keepdims=True)
        w = w_ref[...].reshape(N_DEV * r8, d_model)
        dcat = jnp.concatenate([_dot_nt(dy[rows, :], w) for rows in _row_halves(tm)], axis=0)
        _to_groups(dcat[:, :a_w], dsc)
        dya_ref[...] = dcat[:, :a_w].astype(BF16)
        for i, dil in enumerate(dils):
            _split_residues(dsc, dya_res[i], dil, tm)
        dyc = dcat[:, a_w:]
        gb = pc_ref[:, :c_w].astype(F32)
        gc = pc_ref[:, c_w:2 * c_w].astype(F32)
        u = pc_ref[:, 2 * c_w:].astype(F32)
        z = gc * u
        before = halo_ref[:, c_w:2 * c_w].astype(F32) * halo_ref[:, 2 * c_w:].astype(F32)
        before = jnp.where(tile_i > 0, before, jnp.zeros_like(before))
        z1, z2 = _shift_rows(z, before, tm)
        w0, w1, w2 = cprm_ref[0:1, :], cprm_ref[1:2, :], cprm_ref[2:3, :]
        cv = w0 * z2 + w1 * z1 + w2 * z + cprm_ref[3:4, :]
        dcv = dyc * gb
        cs_ref[0:1, :] += jnp.sum(dcv * z2, axis=0, keepdims=True)
        cs_ref[1:2, :] += jnp.sum(dcv * z1, axis=0, keepdims=True)
        cs_ref[2:3, :] += jnp.sum(dcv * z, axis=0, keepdims=True)
        cs_ref[3:4, :] += jnp.sum(dcv, axis=0, keepdims=True)
        row = lax.broadcasted_iota(jnp.int32, dcv.shape, 0)
        after = carry[...]
        d1 = jnp.where(row == tm - 1, after[0:1, :], pltpu.roll(dcv, tm - 1, 0))
        d2 = jnp.where(row == tm - 2, after[0:1, :],
                       jnp.where(row == tm - 1, after[1:2, :], pltpu.roll(dcv, tm - 2, 0)))
        dz = w2 * dcv + w1 * d1 + w0 * d2
        dpc_ref[:, :c_w] = (dyc * cv).astype(BF16)
        dpc_ref[:, c_w:2 * c_w] = (dz * u).astype(BF16)
        dpc_ref[:, 2 * c_w:] = (dz * gc).astype(BF16)
        carry[...] = dcv[0:8, :]

    def rev(width, col=0):
        return pl.BlockSpec((tm, width), lambda s: (n_m - 1 - s, col))

    fixed_d = pl.BlockSpec((8, d_model), lambda s: (0, 0))
    fixed_c = pl.BlockSpec((8, c_w), lambda s: (0, 0))
    return pl.pallas_call(
        body, name=f"mixer_bwd_out_{l}", grid=(n_m,),
        in_specs=[rev(d_model), rev(d_model), fixed_d,
                  pl.BlockSpec((N_DEV, r8, d_model), lambda s: (0, 0, 0)),
                  rev(3 * c_w, 1),
                  pl.BlockSpec((8, 3 * c_w), lambda s: (jnp.maximum((n_m - 1 - s) * hb - 1, 0), 1)),
                  fixed_c],
        out_specs=[rev(a_w), rev(3 * c_w), rev(d_model), fixed_d, fixed_c]
        + [_res_spec(dil, tm, a_w, lambda s: n_m - 1 - s) for dil in dils],
        out_shape=[SDS((t_len, a_w), BF16), SDS((t_len, 3 * c_w), BF16), SDS((t_len, d_model), BF16),
                   SDS((8, d_model), F32), SDS((8, c_w), F32)]
        + [SDS((dil, t_len // dil, a_w), BF16) for dil in dils],
        scratch_shapes=[pltpu.VMEM((8, c_w), F32), _groups(tm, a_w)],
        compiler_params=_params(1),
    )(dxo, ymix, prm, woutg, proj, proj, cprm)


def _attn_bwd(qk, v, v_col, dya, ya, lse, l):
    dil, seq, a2 = qk.shape
    a_w = a2 // 2
    n_sb = _blocks_per_step(seq)
    rows_step = n_sb * BLOCK
    n_step = seq // rows_step
    last = _block_rows(n_sb - 1)
    lse_w = LANES // (a_w // HEAD_DIM)
    single = n_step == 1

    def body(q_ref, kc_ref, vc_ref, do_ref, ya_ref, lse_ref,
             dq_ref, dk_ref, dv_ref, hold_k, hold_v, acc_k, acc_v, kp_ref, vp_ref):
        j = pl.program_id(1)

        @pl.when(j == 0)
        def _():
            kp_ref[...] = jnp.zeros_like(kp_ref)
            vp_ref[...] = jnp.zeros_like(vp_ref)

        @pl.when((pl.program_id(0) == 0) & (j == 0))
        def _():
            hold_k[...] = jnp.zeros_like(hold_k)
            hold_v[...] = jnp.zeros_like(hold_v)

        @pl.when(j < n_step)
        def _():
            masks = [_band_mask(j == 0)] + [_band_mask(False)] * (n_sb - 1)
            low = lax.broadcasted_iota(jnp.int32, (BLOCK, LANES), 1) < HEAD_DIM
            units = [(sb, slice(p * LANES, (p + 1) * LANES)) for sb in range(n_sb) for p in range(a_w // LANES)]
            acc_k[...] = jnp.zeros_like(acc_k)
            acc_v[...] = jnp.zeros_like(acc_v)
            heads, scores, dps = [], [], []
            for sb, sl in units:
                qp = q_ref[_block_rows(sb), sl]
                k2 = _with_previous(kp_ref, kc_ref, sb, sl)
                v2 = _with_previous(vp_ref, vc_ref, sb, sl)
                dop = do_ref[_block_rows(sb), sl]
                prod = dop.astype(F32) * ya_ref[_block_rows(sb), sl].astype(F32)
                lsep = lse_ref[_block_rows(sb), :]
                first = 2 * (sl.start // LANES) * lse_w
                for hh, half in enumerate((low, ~low)):
                    qh = jnp.where(half, qp, jnp.zeros_like(qp))
                    doh = jnp.where(half, dop, jnp.zeros_like(dop))
                    dsum = jnp.sum(jnp.where(half, prod, 0.0), axis=-1, keepdims=True)
                    heads.append((qh, doh, dsum, lsep[:, first + hh * lse_w:first + hh * lse_w + 1], masks[sb]))
                    scores.append(_dot_nt(qh, k2))
                    dps.append(_dot_nt(doh, v2))
            dss, prs = [], []
            for (qh, doh, dsum, lse_h, valid), s, dp in zip(heads, scores, dps):
                pr = jnp.exp(jnp.where(valid, s * SM_SCALE, NEG) - lse_h)
                dss.append((pr * (dp - dsum) * SM_SCALE).astype(BF16))
                prs.append(pr.astype(BF16))
            for i, (sb, sl) in enumerate(units):
                k2 = _with_previous(kp_ref, kc_ref, sb, sl)
                h0, h1 = heads[2 * i], heads[2 * i + 1]
                dq_ref[_block_rows(sb), sl] = jnp.where(low, _dot(dss[2 * i], k2),
                                                        _dot(dss[2 * i + 1], k2)).astype(BF16)
                two = slice(sb * BLOCK, (sb + 2) * BLOCK)
                acc_k[two, sl] += _dot_tn(dss[2 * i], h0[0]) + _dot_tn(dss[2 * i + 1], h1[0])
                acc_v[two, sl] += _dot_tn(prs[2 * i], h0[1]) + _dot_tn(prs[2 * i + 1], h1[1])
            for out_ref, hold, acc in ((dk_ref, hold_k, acc_k), (dv_ref, hold_v, acc_v)):
                if single:
                    out_ref[...] = acc[BLOCK:, :].astype(BF16)
                    continue
                if n_sb > 1:
                    out_ref[:rows_step - BLOCK, :] = hold[:rows_step - BLOCK, :].astype(BF16)
                out_ref[last, :] = (hold[last, :] + acc[:BLOCK, :]).astype(BF16)
                hold[...] = acc[BLOCK:, :]
            kp_ref[...] = kc_ref[last, :]
            vp_ref[...] = vc_ref[last, :]

        @pl.when(j == n_step)
        def _():
            dk_ref[...] = hold_k[...].astype(BF16)
            dv_ref[...] = hold_v[...].astype(BF16)

    def blk(f):
        return pl.BlockSpec((None, rows_step, a_w), f)

    def cur(j):
        return jnp.minimum(j, n_step - 1)

    def late(j):
        return jnp.maximum(j - 1, 0)

    return pl.pallas_call(
        body, name=f"attn_bwd_{l}_d{dil}", grid=(dil, 1 if single else n_step + 1),
        in_specs=[blk(lambda r, j: (r, cur(j), 0)),
                  blk(lambda r, j: (r, cur(j), 1)),
                  blk(lambda r, j: (r, cur(j), v_col)),
                  blk(lambda r, j: (r, cur(j), 0)),
                  blk(lambda r, j: (r, cur(j), 0)),
                  pl.BlockSpec((None, rows_step, LANES), lambda r, j: (r, cur(j), 0))],
        out_specs=[blk(lambda r, j: (r, cur(j), 0)), blk(lambda r, j: (r, late(j), 0)),
                   blk(lambda r, j: (r, late(j), 0))],
        out_shape=[SDS((dil, seq, a_w), BF16)] * 3,
        scratch_shapes=[pltpu.VMEM((rows_step, a_w), F32), pltpu.VMEM((rows_step, a_w), F32),
                        pltpu.VMEM((rows_step + BLOCK, a_w), F32), pltpu.VMEM((rows_step + BLOCK, a_w), F32),
                        pltpu.VMEM((BLOCK, a_w), BF16), pltpu.VMEM((BLOCK, a_w), BF16)],
        compiler_params=_params(2),
    )(qk, qk, v, dya, ya, lse)


def _mixer_bwd_in(parts, proj, dpc, dxo, x_in, prm, wing, qkg, bd, l, tm):
    t_len, d_model = x_in.shape
    a_w = d_model // 2
    c_w = d_model - a_w
    c8 = wing.shape[-1]
    p_c = N_DEV * c8

    dils = DILATIONS[1:]
    n_in = 3 * len(DILATIONS)

    def body(*refs):
        nat, res = refs[:3], refs[3:n_in]
        (pa_ref, dpc_ref, dxo_ref, x_ref, prm_ref, w_ref, qkg_ref, bd_ref,
         dx_ref, dproj_ref, st_ref, qs_ref) = refs[n_in:n_in + 12]
        sums, wcat, wsems = refs[n_in + 12:n_in + 15], refs[n_in + 15], refs[n_in + 16]

        @pl.when(pl.program_id(0) == 0)
        def _():
            st_ref[...] = jnp.zeros_like(st_ref)
            qs_ref[...] = jnp.zeros_like(qs_ref)
            _join_columns(w_ref, wcat, wsems)

        for t in range(3):
            _to_groups(nat[t][...].astype(F32), sums[t])
            for i, dil in enumerate(dils):
                _merge_residues(res[3 * i + t], sums[t], dil, tm, add=True)
        bdm = bd_ref[...]
        grads = [_from_groups(sums[0]), _from_groups(sums[1])]
        for i in range(2):
            t = pa_ref[:, i * a_w:(i + 1) * a_w].astype(F32)
            rstd = lax.rsqrt(_head_mean(t * t, bdm) + EPS)
            that = t * rstd
            qs_ref[i:i + 1, :] += jnp.sum(grads[i] * that, axis=0, keepdims=True)
            dhat = grads[i] * qkg_ref[i:i + 1, :]
            dt = rstd * (dhat - that * _head_mean(dhat * that, bdm))
            dproj_ref[:, i * a_w:(i + 1) * a_w] = dt.astype(BF16)
        dproj_ref[:, 2 * a_w:3 * a_w] = _from_groups(sums[2]).astype(BF16)
        dproj_ref[:, 3 * a_w:] = dpc_ref[...]
        dh = jnp.concatenate([_dot_nt(dproj_ref[rows, :], wcat[...]) for rows in _row_halves(tm)], axis=0)
        dx, dshift, dscale, dng = _ada_bwd(dh, x_ref[...], prm_ref)
        dx_ref[...] = dxo_ref[...] + dx
        st_ref[0:1, :] += dshift
        st_ref[1:2, :] += dscale
        st_ref[2:3, :] += dng

    ta = pl.BlockSpec((tm, a_w), lambda m: (m, 0))
    tile = pl.BlockSpec((tm, d_model), lambda m: (m, 0))
    return pl.pallas_call(
        body, name=f"mixer_bwd_in_{l}", grid=(t_len // tm,),
        in_specs=[ta] * 3 + [_res_spec(dil, tm, a_w) for dil in dils for _ in range(3)]
                 + [pl.BlockSpec((tm, 3 * a_w), lambda m: (m, 0)),
                             pl.BlockSpec((tm, 3 * c_w), lambda m: (m, 0)),
                             tile, tile, pl.BlockSpec((8, d_model), lambda m: (0, 0)),
                             ANY,
                             pl.BlockSpec((8, a_w), lambda m: (0, 0)),
                             pl.BlockSpec(bd.shape, lambda m: (0, 0))],
        out_specs=[tile, pl.BlockSpec((tm, p_c), lambda m: (m, 0)),
                   pl.BlockSpec((8, d_model), lambda m: (0, 0)), pl.BlockSpec((8, a_w), lambda m: (0, 0))],
        out_shape=[SDS((t_len, d_model), F32), SDS((t_len, p_c), BF16), SDS((8, d_model), F32), SDS((8, a_w), F32)],
        scratch_shapes=[_groups(tm, a_w)] * 3 + [pltpu.VMEM((d_model, p_c), BF16), pltpu.SemaphoreType.DMA((N_DEV,))],
        compiler_params=_params(1),
    )(*[t.reshape(t_len, a_w) for t in parts[0]], *[t for p in parts[1:] for t in p],
      proj, dpc, dxo, x_in, prm, wing, qkg, bd)


def _loss_head(x, target, tm):
    t_len, d_model = x.shape

    def body(x_ref, t_ref, dx_ref, loss_ref):
        @pl.when(pl.program_id(0) == 0)
        def _():
            loss_ref[...] = jnp.zeros_like(loss_ref)

        diff = x_ref[...] - t_ref[...]
        dx_ref[...] = diff * (1.0 / d_model)
        per_token = jnp.sum(diff * diff, axis=-1, keepdims=True) * (1.0 / d_model)
        loss_ref[...] += 0.5 * jnp.sum(per_token)

    tile = pl.BlockSpec((tm, d_model), lambda m: (m, 0))
    return pl.pallas_call(
        body, name="loss_head", grid=(t_len // tm,),
        in_specs=[tile, tile],
        out_specs=[tile, pl.BlockSpec((8, LANES), lambda m: (0, 0))],
        out_shape=[SDS((t_len, d_model), F32), SDS((8, LANES), F32)],
        compiler_params=_params(1),
    )(x, target)


def _row_tile(rows, cols):
    best = None
    for t in range(16, rows + 1, 16):
        if rows % t == 0 and t * cols * 4 <= (1 << 20):
            best = t
    return best if best is not None else rows


def _adamw(pieces, w, m, v, name, first_q=0, earlier=None):
    n_q = len(pieces)
    n_p, rows, cols = pieces[0].shape
    tr = _row_tile(rows, cols)
    n_i = rows // tr
    n_e = 0 if earlier is None else 4
    c1 = 1.0 - ADAM_B1 ** ADAM_STEP
    c2 = 1.0 - ADAM_B2 ** ADAM_STEP

    def body(*refs):
        p_refs = refs[:n_q]
        w_ref, m_ref, v_ref = refs[n_q:n_q + 3]
        g_ref, d_ref, nm_ref, nv_ref = refs[n_q + 3 + n_e:]
        for q in range(n_q):
            @pl.when(pl.program_id(0) == q)
            def _():
                g = p_refs[q][0].astype(F32)
                for i in range(1, n_p):
                    g = g + p_refs[q][i].astype(F32)
                g_ref[...] = g
                nm = ADAM_B1 * m_ref[...] + (1.0 - ADAM_B1) * g
                nv = ADAM_B2 * v_ref[...] + (1.0 - ADAM_B2) * (g * g)
                nm_ref[...] = nm
                nv_ref[...] = nv
                d_ref[...] = -ADAM_LR * ((nm / c1) / (jnp.sqrt(nv / c2) + ADAM_EPS) + ADAM_WD * w_ref[...])

    def piece_spec(q):
        return pl.BlockSpec((n_p, tr, cols), lambda a, i: (0, jnp.where(a == q, i, 0), 0))

    tile = pl.BlockSpec((tr, cols), lambda a, i: ((first_q + a) * n_i + i, 0))
    return pl.pallas_call(
        body, name=name, grid=(n_q, n_i),
        in_specs=[piece_spec(q) for q in range(n_q)] + [tile, tile, tile] + [ANY] * n_e,
        out_specs=[tile] * 4,
        out_shape=[SDS(w.shape, F32)] * 4,
        input_output_aliases={n_q + 3 + k: k for k in range(n_e)},
        compiler_params=_params(2),
    )(*pieces, w, m, v, *(earlier or []))


def _pack(vecs):
    flat = jnp.concatenate([v.reshape(-1).astype(F32) for v in vecs])
    rows = -(-flat.shape[0] // (8 * LANES)) * 8
    return jnp.pad(flat, (0, rows * LANES - flat.shape[0])).reshape(rows, LANES)


def _unpack(packed, shapes):
    lead = packed.shape[:-2]
    flat = packed.reshape(lead + (-1,))
    out, off = [], 0
    for s in shapes:
        size = 1
        for d in s:
            size *= d
        out.append(flat[..., off:off + size].reshape(lead + tuple(s)))
        off += size
    return out


def kernel(x, c, w_ada, b_ada, norm_g, w_in, q_norm_g, k_norm_g, conv_w, conv_b, w_out, ffn_w1, ffn_w2, loss_target, m_w_ada, m_b_ada, m_norm_g, m_w_in, m_q_norm_g, m_k_norm_g, m_conv_w, m_conv_b, m_w_out, m_ffn_w1, m_ffn_w2, v_w_ada, v_b_ada, v_norm_g, v_w_in, v_q_norm_g, v_k_norm_g, v_conv_w, v_conv_b, v_w_out, v_ffn_w1, v_ffn_w2):
    _, t_len, d_model = x.shape
    n_layer = w_ada.shape[0]
    a_w = d_model // 2
    c_w = d_model - a_w
    n_head = a_w // HEAD_DIM
    ada_cols = w_ada.shape[-1]
    tm = min(TOKEN_TILE, t_len)
    tm_ffn = min(FFN_FWD_TOKEN_TILE, t_len)
    tm_ffn_bwd = min(FFN_BWD_TOKEN_TILE, t_len)
    me = _index(_my_place())
    x2 = x.reshape(t_len, d_model)
    target = loss_target.reshape(t_len, d_model)

    small_shapes = [(d_model,), norm_g.shape, conv_w.shape]
    gathered = _allgather_small(_pack([c, norm_g, conv_w]), "gather_small_inputs")
    c_all, ng_parts, cw_parts = _unpack(gathered, small_shapes)
    norm_g_full = jnp.moveaxis(ng_parts, 0, 2).reshape(n_layer, 3, d_model)
    conv_w_full = jnp.moveaxis(cw_parts, 0, 2).reshape(n_layer, 3, c_w)

    b_loc = lax.dynamic_slice_in_dim(b_ada, me * ada_cols, ada_cols, axis=1).reshape(n_layer, 1, ada_cols)
    mod_part, silu_c = _mod_part(c_all, w_ada, b_loc)
    mod_all = _allgather_small(_pack([mod_part]), "gather_mod")
    mod_all = _unpack(mod_all, [mod_part.shape])[0]
    mod_mine = lax.dynamic_index_in_dim(mod_all, me, axis=2, keepdims=False)
    mod = jnp.moveaxis(mod_mine, 0, 1).reshape(n_layer, 3, 3, d_model)

    def prm_of(l, sub):
        rows = jnp.stack([norm_g_full[l, sub], mod[l, sub, 0], mod[l, sub, 1], mod[l, sub, 2]])
        return jnp.pad(rows, ((0, 4), (0, 0)))

    prm = [[prm_of(l, sub) for sub in range(3)] for l in range(n_layer)]
    qkg = [jnp.pad(jnp.stack([jnp.tile(q_norm_g[l], n_head), jnp.tile(k_norm_g[l], n_head)]), ((0, 6), (0, 0)))
           for l in range(n_layer)]
    cprm = [jnp.pad(jnp.concatenate([conv_w_full[l], conv_b[l][None]]), ((0, 4), (0, 0))) for l in range(n_layer)]
    head_of = jnp.arange(min(MXU_WIDTH, a_w)) // HEAD_DIM
    bd = (head_of[:, None] == head_of[None, :]).astype(BF16)
    expand = (jnp.arange(LANES)[:, None] == (jnp.arange(a_w)[None, :] // HEAD_DIM) * (LANES // n_head)).astype(BF16)

    f8 = ffn_w1.shape[-1]
    r8 = ffn_w2.shape[-2]
    c8 = w_in.shape[-1]
    o8 = w_out.shape[-2]
    me_arr = jnp.reshape(me, (1,)).astype(jnp.int32)
    win_b = [_place_own(w_in, (l,), me_arr, f"own_w_in_{l}") for l in range(n_layer)]
    wout_b = [_place_own(w_out, (l,), me_arr, f"own_w_out_{l}") for l in range(n_layer)]
    w1_b = [[_place_own(ffn_w1, (l, s), me_arr, f"own_w1_{l}_{s}") for s in range(2)] for l in range(n_layer)]
    w2_b = [[_place_own(ffn_w2, (l, s), me_arr, f"own_w2_{l}_{s}") for s in range(2)] for l in range(n_layer)]
    w1_first, w2_first = _allgather_big([w1_b[0][0], w2_b[0][0]], "gather_first_weights")
    groups = [[win_b[0], wout_b[0]], [w1_b[0][1], w2_b[0][1]]]
    for l in range(1, n_layer):
        groups += [[w1_b[l][0], w2_b[l][0]], [win_b[l], wout_b[l]], [w1_b[l][1], w2_b[l][1]]]
    flat = [a for grp in groups for a in grp]
    w_sends, w_recvs, w_lands, w_token = _send_start(flat, None, w1_first, "weights_start")

    def gathered(gi, after):
        lo = sum(len(grp) for grp in groups[:gi])
        hi = lo + len(groups[gi])
        return _send_wait(w_lands[lo:hi], w_sends[lo:hi], w_recvs[lo:hi], after, f"weights_wait_{gi}")

    def chunked(w1g):
        return w1g.reshape(2, N_CHUNK, d_model, f8)

    saved = []
    xc = x2
    for l in range(n_layer):
        keep = {"x0": xc}
        if l == 0:
            w1g, w2g = w1_first, w2_first
            prm_first = prm[0][0] + w_token[0, 0]
        else:
            w1g, w2g = gathered(3 * l - 1, xc)
            prm_first = prm[l][0]
        keep["w_ffn0"] = (chunked(w1g), w2g)
        keep["act0"], keep["h0"], keep["y0"], xc = _ffn_fwd(xc, prm_first, *keep["w_ffn0"], l, 0, tm_ffn)
        keep["x1"] = xc
        wing, woutg = gathered(0 if l == 0 else 3 * l, xc)
        keep.update(wing=wing, woutg=woutg)
        proj, qkn, keep["h1"], ycv, *res = _mixer_in(xc, prm[l][1], wing, qkg[l], cprm[l], bd, l, tm)
        qkv = [(qkn[None], proj[None], 2)] + [(res[2 * i], res[2 * i + 1], 0) for i in range(len(DILATIONS) - 1)]
        branches = [_attn_fwd(*t, l) for t in qkv]
        cat, lse, *res = _attn_combine([b[0] for b in branches], [b[1] for b in branches], ycv, expand, tm, l)
        n_r = len(DILATIONS) - 1
        keep["ya"] = [cat[None]] + res[:n_r]
        keep["lse"] = [lse[None]] + res[n_r:]
        xc, keep["ymix"] = _mixer_out(cat, xc, prm[l][1], woutg, l, tm)
        keep.update(proj=proj, qkv=qkv, cat=cat)
        keep["x2"] = xc
        w1g, w2g = gathered(1 if l == 0 else 3 * l + 1, xc)
        keep["w_ffn2"] = (chunked(w1g), w2g)
        keep["act2"], keep["h2"], keep["y2"], xc = _ffn_fwd(xc, prm[l][2], *keep["w_ffn2"], l, 1, tm_ffn)
        saved.append(keep)

    dx, loss_blk = _loss_head(xc, target, tm)
    loss = lax.psum(loss_blk[0, 0], ("x", "y", "c"))

    tk = min(GRAD_TOKEN_TILE, t_len)
    stats =[[None] * 3 for _ in range(n_layer)]
    conv_stats = [None] * n_layer
    qk_stats = [None] * n_layer

    started = {}
    dummy = jnp.zeros((8, LANES), F32)

    def start_exchange(g, name):
        sends, recvs, thru, token = _send_start(
            [_place_own(g, None, me_arr, f"own_grad_{name}")], [g], dummy, f"grads_start_{name}")
        started[name] = (thru[0], thru[1], sends[0], recvs[0], token)
        return token

    def wait_exchange(names, after, name):
        ent = [started[k] for k in names]
        done = _send_wait([e[0] for e in ent] + [e[1] for e in ent], [e[2] for e in ent], [e[3] for e in ent],
                          after, name)
        return dict(zip(names, done))

    def ffn_backward(dxo, keep, l, sub, which, prm_dep):
        dx_in, da, dy, st = _ffn_bwd(dxo, keep[f"y{which}"], keep[f"x{which}"], prm_dep,
                                     keep[f"act{which}"], *keep[f"w_ffn{which}"], l, sub, tm_ffn_bwd)
        g_w2 = _matmul_tn(
            keep[f"act{which}"], dy,
            pl.BlockSpec((None, None, tk, f8), lambda p, k: (2, p, k, 0)),
            pl.BlockSpec((tk, d_model), lambda p, k: (k, 0)),
            SDS((N_CHUNK, f8, d_model), BF16), pl.BlockSpec((None, f8, d_model), lambda p, k: (p, 0, 0)),
            N_CHUNK, t_len, tk, f"grad_w2_{l}_{sub}", dummy)
        token = start_exchange(g_w2.reshape(N_DEV, r8, d_model), f"w2_{l}_{sub}")
        g_w1 = _matmul_tn(
            da.reshape(N_DEV, t_len, f8), keep[f"h{which}"],
            pl.BlockSpec((None, tk, f8), lambda p, k: (p, k, 0)),
            pl.BlockSpec((tk, d_model), lambda p, k: (k, 0)),
            SDS((N_DEV, f8, d_model), BF16), pl.BlockSpec((None, f8, d_model), lambda p, k: (p, 0, 0)),
            N_DEV, t_len, tk, f"grad_w1_{l}_{sub}", token)
        token = start_exchange(g_w1, f"w1_{l}_{sub}")
        return dx_in, st, token[0, 0]

    token = 0.0
    for l in reversed(range(n_layer)):
        keep = saved[l]
        wing, woutg = keep["wing"], keep["woutg"]
        dx, stats[l][2], token = ffn_backward(dx, keep, l, 1, 2, prm[l][2] + token)
        dya, dpc, dy, st_gate, conv_stats[l], *dya_res = _mixer_bwd_out(
            dx, keep["ymix"], prm[l][1] + token, woutg, keep["proj"], cprm[l], l, tm)
        dyas = [dya[None]] + dya_res
        g_wout = _matmul_tn(
            keep["cat"], dy,
            pl.BlockSpec((tk, d_model), lambda p, k: (k, 0)),
            pl.BlockSpec((tk, d_model), lambda p, k: (k, 0)),
            SDS((d_model, d_model), BF16), pl.BlockSpec((d_model, d_model), lambda p, k: (0, 0)),
            1, t_len, tk, f"grad_wout_{l}", dummy)
        token = start_exchange(g_wout.reshape(N_DEV, o8, d_model), f"wout_{l}")[0, 0]
        parts = [_attn_bwd(*keep["qkv"][i], dyas[i], keep["ya"][i], keep["lse"][i], l)
                 for i in range(len(DILATIONS))]
        dx, dproj, st_norm, qk_stats[l] = _mixer_bwd_in(
            parts, keep["proj"], dpc, dx, keep["x1"], prm[l][1] + token, wing, qkg[l], bd, l, tm)
        stats[l][1] = st_norm + st_gate
        g_win = _matmul_tn(
            keep["h1"], dproj,
            pl.BlockSpec((tk, d_model), lambda p, k: (k, 0)),
            pl.BlockSpec((tk, N_DEV // 2 * c8), lambda p, k: (k, p)),
            SDS((N_DEV, d_model, c8), BF16), pl.BlockSpec((N_DEV // 2, d_model, c8), lambda p, k: (p, 0, 0)),
            2, t_len, tk, f"grad_win_{l}", dummy)
        token = start_exchange(g_win, f"win_{l}")[0, 0]
        dx, stats[l][0], token = ffn_backward(dx, keep, l, 0, 0, prm[l][0] + token)
    grad_x = dx.reshape(x.shape)

    def update(pieces, w, m, v, name):
        shape = w.shape
        cols = shape[-1]
        outs = _adamw([p.reshape(p.shape[0], -1, cols) for p in pieces], w.reshape(-1, cols),
                      m.reshape(-1, cols), v.reshape(-1, cols), name)
        return [o.reshape(shape) for o in outs]

    last = "w1_0_0"
    lands = wait_exchange([k for k in started if k != last], started[last][4], "grads_wait_early")
    res = {
        "w_in": update([lands[f"win_{l}"] for l in range(n_layer)], w_in, m_w_in, v_w_in, "adamw_w_in"),
        "w_out": update([lands[f"wout_{l}"] for l in range(n_layer)], w_out, m_w_out, v_w_out, "adamw_w_out"),
        "ffn_w2": update([lands[f"w2_{l}_{s}"] for l in range(n_layer) for s in range(2)],
                         ffn_w2, m_ffn_w2, v_ffn_w2, "adamw_ffn_w2"),
    }

    dmod = jnp.stack([jnp.stack([stats[l][sub][r] for sub in range(3) for r in (0, 1, 3)]) for l in range(n_layer)])
    dng = jnp.stack([jnp.stack([stats[l][sub][2] for sub in range(3)]) for l in range(n_layer)])
    dcw = jnp.stack([conv_stats[l][0:3] for l in range(n_layer)])
    dcb = jnp.stack([conv_stats[l][3] for l in range(n_layer)])
    dqk = jnp.stack([qk_stats[l][0:2].reshape(2, n_head, HEAD_DIM).sum(axis=1) for l in range(n_layer)])
    part_shapes = [dmod.shape, dng.shape, dcw.shape, dcb.shape, dqk.shape]
    parts_all = _allgather_small(_pack([dmod, dng, dcw, dcb, dqk]), "gather_small_grads")
    dmod_all = _unpack(parts_all, part_shapes)[0].reshape(N_DEV, n_layer, 9 * d_model)
    summed = _unpack(_sum_devices(parts_all), part_shapes)
    g_b_ada = summed[0].reshape(n_layer, 9 * d_model)
    ng_cols = norm_g.shape[-1]
    g_norm_g = lax.dynamic_slice_in_dim(summed[1], me * ng_cols, ng_cols, axis=2)
    cw_cols = conv_w.shape[-1]
    g_conv_w = lax.dynamic_slice_in_dim(summed[2], me * cw_cols, cw_cols, axis=2)
    g_conv_b = summed[3]
    g_q, g_k = summed[4][:, 0], summed[4][:, 1]
    dmod_loc = lax.dynamic_slice_in_dim(dmod_all, me * ada_cols, ada_cols, axis=2)
    g_w_ada = _w_ada_grad(silu_c.T, jnp.moveaxis(dmod_loc, 0, 1))
    res["w_ada"] = update([g_w_ada[None]], w_ada, m_w_ada, v_w_ada, "adamw_w_ada")

    names = ["b_ada", "norm_g", "q_norm_g", "k_norm_g", "conv_w", "conv_b"]
    g_small = [g_b_ada, g_norm_g, g_q, g_k, g_conv_w, g_conv_b]
    w_small = [b_ada, norm_g, q_norm_g, k_norm_g, conv_w, conv_b]
    m_small = [m_b_ada, m_norm_g, m_q_norm_g, m_k_norm_g, m_conv_w, m_conv_b]
    v_small = [v_b_ada, v_norm_g, v_q_norm_g, v_k_norm_g, v_conv_w, v_conv_b]
    outs = _adamw([_pack(g_small)[None]], _pack(w_small), _pack(m_small), _pack(v_small), "adamw_small")
    shapes = [w.shape for w in w_small]
    unpacked = [_unpack(o, shapes) for o in outs]
    for i, nme in enumerate(names):
        res[nme] = [unpacked[k][i] for k in range(4)]

    w1_t = [jnp.swapaxes(t, -1, -2) for t in (ffn_w1, m_ffn_w1, v_ffn_w1)]
    w1_flat = [t.reshape(-1, d_model) for t in w1_t]
    w1_keys = [f"w1_{l}_{s}" for l in range(n_layer) for s in range(2)]
    assert w1_keys[0] == last
    early = _adamw([lands[k] for k in w1_keys[1:]], *w1_flat, "adamw_ffn_w1_early", first_q=1)
    follows = (res["w_in"][1][0, :1, :1] + res["w_out"][1][0, :1, :1] + res["ffn_w2"][1][0, 0, :1, :1]
               + res["w_ada"][1][0, :1, :1] + outs[1][:1, :1] + early[1][-1:, :1])
    lands.update(wait_exchange([last], follows, "grads_wait_last"))
    done = _adamw([lands[last]], *w1_flat, "adamw_ffn_w1_last", first_q=0, earlier=early)
    res["ffn_w1"] = [jnp.swapaxes(t.reshape(w1_t[0].shape), -1, -2) for t in done]

    order = ["w_ada", "b_ada", "norm_g", "w_in", "q_norm_g", "k_norm_g", "conv_w", "conv_b", "w_out", "ffn_w1", "ffn_w2"]
    return (loss, grad_x, *[res[n][0] for n in order], *[res[n][1] for n in order],
            *[res[n][2] for n in order], *[res[n][3] for n in order])
```

```python
import jax
import jax.numpy as jnp
from jax import lax
from jax.experimental import pallas as pl
from jax.experimental.pallas import tpu as pltpu

F32 = jnp.float32
BF16 = jnp.bfloat16
SDS = jax.ShapeDtypeStruct

N_DEV = 8
HEAD_DIM = 64
BLOCK = 128
DILATIONS = (1, 4, 16)
ATTN_BLOCKS_PER_STEP = 8
N_CHUNK = 4
EPS = 1e-6
NEG = -1e30
SM_SCALE = HEAD_DIM ** -0.5
LANES = 128
MXU_WIDTH = 256
TOKEN_TILE = 512
FFN_ROW_PART = 256
FFN_FWD_TOKEN_TILE = 256
FFN_BWD_TOKEN_TILE = 256
GRAD_TOKEN_TILE = 2048
VMEM_LIMIT_BYTES = 56 * 1024 * 1024

ADAM_LR = 0.001
ADAM_B1 = 0.9
ADAM_B2 = 0.999
ADAM_EPS = 1e-08
ADAM_WD = 0.01
ADAM_STEP = 10

MESH_ID = pl.DeviceIdType.MESH
ANY = pl.BlockSpec(memory_space=pl.ANY)
HBM_SPEC = pl.BlockSpec(memory_space=pltpu.HBM)
SEM_SPEC = pl.BlockSpec(memory_space=pltpu.SEMAPHORE)
SIDE_EFFECT = pltpu.SideEffectType.DATAFLOW_SIDE_EFFECTING


def _params(n_axes):
    return pltpu.CompilerParams(dimension_semantics=("arbitrary",) * n_axes,
                                vmem_limit_bytes=VMEM_LIMIT_BYTES)


def _dot(a, b):
    return jnp.dot(a, b, preferred_element_type=F32)


def _dot_nt(a, b):
    return lax.dot_general(a, b, (((1,), (1,)), ((), ())), preferred_element_type=F32)


def _dot_tn(a, b):
    return lax.dot_general(a, b, (((0,), (0,)), ((), ())), preferred_element_type=F32)


def _row_halves(n):
    return (slice(0, n // 2), slice(n // 2, n))


def _split_bf16(t):
    hi = t.astype(BF16)
    return hi, (t - hi.astype(F32)).astype(BF16)


def _my_place():
    x, y, c = lax.axis_index("x"), lax.axis_index("y"), lax.axis_index("c")
    return x, y, c


def _peer(place, k):
    x, y, c = place
    return ((1 - x) if k & 4 else x, (1 - y) if k & 2 else y, (1 - c) if k & 1 else c)


def _index(place):
    return 4 * place[0] + 2 * place[1] + place[2]


def _allgather_small(v, name):
    rows, cols = v.shape

    def body(x_ref, out_ref, send_sems, recv_sems, local_sem):
        me = _my_place()
        mine = pltpu.make_async_copy(x_ref, out_ref.at[_index(me)], local_sem)
        mine.start()
        sends = []
        for k in range(1, N_DEV):
            cp = pltpu.make_async_remote_copy(
                src_ref=x_ref, dst_ref=out_ref.at[_index(me)], send_sem=send_sems.at[k - 1],
                recv_sem=recv_sems.at[k - 1], device_id=_peer(me, k), device_id_type=MESH_ID)
            cp.start()
            sends.append(cp)
        for k in range(1, N_DEV):
            pltpu.make_async_remote_copy(
                src_ref=x_ref, dst_ref=out_ref.at[_index(_peer(me, k))], send_sem=send_sems.at[k - 1],
                recv_sem=recv_sems.at[k - 1], device_id=_peer(me, k), device_id_type=MESH_ID).wait_recv()
        for cp in sends:
            cp.wait_send()
        mine.wait()

    return pl.pallas_call(
        body, name=name,
        out_shape=SDS((N_DEV, rows, cols), v.dtype),
        in_specs=[pl.BlockSpec(memory_space=pltpu.VMEM)],
        out_specs=pl.BlockSpec(memory_space=pltpu.VMEM),
        scratch_shapes=[pltpu.SemaphoreType.DMA((N_DEV - 1,)), pltpu.SemaphoreType.DMA((N_DEV - 1,)),
                        pltpu.SemaphoreType.DMA],
    )(v)


def _allgather_big(lands, name):
    n = len(lands)

    def body(*refs):
        outs = refs[n:2 * n]
        send_sems, recv_sems = refs[2 * n:]
        x, y, c = _my_place()
        me, sib = (x, y, c), (x, y, 1 - c)
        chips = [(1 - x, y), (x, 1 - y), (1 - x, 1 - y)]

        def copy(i, k, block, to):
            dst = outs[i].at[_index(block)]
            return pltpu.make_async_remote_copy(
                src_ref=dst, dst_ref=dst, send_sem=send_sems.at[7 * i + k],
                recv_sem=recv_sems.at[7 * i + k], device_id=to, device_id_type=MESH_ID)

        first = []
        for i in range(n):
            first.append(copy(i, 0, me, sib))
            for j, chip in enumerate(chips):
                first.append(copy(i, 1 + j, me, (*chip, c)))
        for cp in first:
            cp.start()
        passed = []
        for i in range(n):
            for j, chip in enumerate(chips):
                copy(i, 1 + j, (*chip, c), me).wait_recv()
                fwd = copy(i, 4 + j, (*chip, c), sib)
                fwd.start()
                passed.append(fwd)
        for i in range(n):
            copy(i, 0, sib, me).wait_recv()
            for j, chip in enumerate(chips):
                copy(i, 4 + j, (*chip, 1 - c), me).wait_recv()
        for cp in first + passed:
            cp.wait_send()

    return pl.pallas_call(
        body, name=name,
        out_shape=[SDS(a.shape, a.dtype) for a in lands],
        in_specs=[ANY] * n, out_specs=[ANY] * n,
        input_output_aliases={i: i for i in range(n)},
        scratch_shapes=[pltpu.SemaphoreType.DMA((7 * n,)), pltpu.SemaphoreType.DMA((7 * n,))],
    )(*lands)


def _place_own(src, lead, me_arr, name):
    rows, cols = src.shape[-2:]
    tr = _row_tile(rows, cols)

    def body(me_ref, s_ref, o_ref):
        o_ref[...] = s_ref[...].astype(BF16)

    if lead is None:
        in_spec = pl.BlockSpec((None, tr, cols), lambda i, me_ref: (me_ref[0], i, 0))
    else:
        in_spec = pl.BlockSpec((None,) * len(lead) + (tr, cols), lambda i, me_ref: (*lead, i, 0))
    return pl.pallas_call(
        body, name=name,
        grid_spec=pltpu.PrefetchScalarGridSpec(
            num_scalar_prefetch=1, grid=(rows // tr,), in_specs=[in_spec],
            out_specs=pl.BlockSpec((None, tr, cols), lambda i, me_ref: (me_ref[0], i, 0))),
        out_shape=SDS((N_DEV, rows, cols), BF16),
        compiler_params=_params(1),
    )(me_arr, src)


def _send_start(lands, pieces, after, name):
    n = len(lands)
    arrs = list(lands) + list(pieces or [])
    n_a = len(arrs)

    def body(*refs):
        land_r, piece_r = refs[:n], refs[n:n_a]
        send, recv = refs[n_a + 1:n_a + 1 + n], refs[n_a + 1 + n:n_a + 1 + 2 * n]
        token = refs[-1]
        me = _my_place()
        for i in range(n):
            for k in range(1, N_DEV):
                peer = _peer(me, k)
                src = piece_r[i].at[_index(peer)] if pieces else land_r[i].at[_index(me)]
                pltpu.make_async_remote_copy(
                    src_ref=src, dst_ref=land_r[i].at[_index(me)], send_sem=send[i], recv_sem=recv[i],
                    device_id=peer, device_id_type=MESH_ID).start()
        token[...] = jnp.zeros_like(token)

    outs = pl.pallas_call(
        body, name=name,
        out_shape=[pltpu.SemaphoreType.DMA(())] * (2 * n) + [pltpu.HBM(a.shape, a.dtype) for a in arrs]
        + [SDS((8, LANES), F32)],
        in_specs=[HBM_SPEC] * n_a + [ANY],
        out_specs=[SEM_SPEC] * (2 * n) + [HBM_SPEC] * n_a + [pl.BlockSpec(memory_space=pltpu.VMEM)],
        input_output_aliases={i: 2 * n + i for i in range(n_a)},
        compiler_params=pltpu.CompilerParams(has_side_effects=SIDE_EFFECT),
    )(*[pltpu.with_memory_space_constraint(a, pltpu.HBM) for a in arrs], after)
    return outs[:n], outs[n:2 * n], outs[2 * n:2 * n + n_a], outs[-1]


def _send_wait(thru, sends, recvs, after, name):
    n = len(sends)
    n_a = len(thru)

    def body(*refs):
        land_r = refs[:n]
        send, recv = refs[n_a:n_a + n], refs[n_a + n:n_a + 2 * n]
        me = _my_place()
        for i in range(n):
            seven = land_r[i].at[pl.ds(0, N_DEV - 1)]
            cp = pltpu.make_async_remote_copy(src_ref=seven, dst_ref=seven, send_sem=send[i], recv_sem=recv[i],
                                              device_id=_peer(me, 1), device_id_type=MESH_ID)
            cp.wait_send()
            cp.wait_recv()

    outs = pl.pallas_call(
        body, name=name,
        out_shape=[pltpu.HBM(a.shape, a.dtype) for a in thru],
        in_specs=[HBM_SPEC] * n_a + [SEM_SPEC] * (2 * n) + [ANY],
        out_specs=[HBM_SPEC] * n_a,
        input_output_aliases={i: i for i in range(n_a)},
        compiler_params=pltpu.CompilerParams(has_side_effects=SIDE_EFFECT),
    )(*thru, *sends, *recvs, after)
    return outs[:n]


def _ada_fwd(x, prm_ref):
    rstd = lax.rsqrt(jnp.mean(x * x, axis=-1, keepdims=True) + EPS)
    return (x * rstd * prm_ref[0:1, :]) * (1.0 + prm_ref[2:3, :]) + prm_ref[1:2, :]


def _ada_bwd(dh, x, prm_ref):
    ng, ops = prm_ref[0:1, :], 1.0 + prm_ref[2:3, :]
    rstd = lax.rsqrt(jnp.mean(x * x, axis=-1, keepdims=True) + EPS)
    xhat = x * rstd
    dxhat = dh * (ops * ng)
    dx = rstd * (dxhat - xhat * jnp.mean(dxhat * xhat, axis=-1, keepdims=True))
    dshift = jnp.sum(dh, axis=0, keepdims=True)
    dscale = jnp.sum(dh * (xhat * ng), axis=0, keepdims=True)
    dng = jnp.sum(dh * (xhat * ops), axis=0, keepdims=True)
    return dx, dshift, dscale, dng


def _head_mean(t, bd):
    hi = t.astype(BF16)
    g = bd.shape[0]
    parts = [_dot(hi[:, i:i + g], bd) for i in range(0, t.shape[1], g)]
    return jnp.concatenate(parts, axis=1) * (1.0 / HEAD_DIM)


def _join_columns(w_hbm, dst_ref, sems):
    cols = w_hbm.shape[2]
    copies = [pltpu.make_async_copy(w_hbm.at[j], dst_ref.at[:, pl.ds(j * cols, cols)], sems.at[j])
              for j in range(w_hbm.shape[0])]
    for cp in copies:
        cp.start()
    for cp in copies:
        cp.wait()


def _shift_rows(z, before, tm):
    row = lax.broadcasted_iota(jnp.int32, z.shape, 0)
    z1 = jnp.where(row == 0, before[7:8, :], pltpu.roll(z, 1, 0))
    z2 = jnp.where(row == 0, before[6:7, :], jnp.where(row == 1, before[7:8, :], pltpu.roll(z, 2, 0)))
    return z1, z2


def _mod_part(c_all, w_ada, b_loc):
    n_layer, d_model, cols = w_ada.shape

    def body(c_ref, w_ref, b_ref, out_ref, sc_ref):
        cv = c_ref[...]
        sc = cv * jax.nn.sigmoid(cv)
        sc_ref[...] = sc
        a_hi, a_lo = _split_bf16(sc)
        w_hi, w_lo = _split_bf16(w_ref[...])
        out_ref[...] = _dot(a_hi, w_hi) + _dot(a_hi, w_lo) + _dot(a_lo, w_hi) + b_ref[...]

    return pl.pallas_call(
        body, name="mod_part", grid=(n_layer,),
        in_specs=[pl.BlockSpec((N_DEV, d_model), lambda l: (0, 0)),
                  pl.BlockSpec((None, d_model, cols), lambda l: (l, 0, 0)),
                  pl.BlockSpec((None, 1, cols), lambda l: (l, 0, 0))],
        out_specs=[pl.BlockSpec((None, N_DEV, cols), lambda l: (l, 0, 0)),
                   pl.BlockSpec((N_DEV, d_model), lambda l: (0, 0))],
        out_shape=[SDS((n_layer, N_DEV, cols), F32), SDS((N_DEV, d_model), F32)],
        compiler_params=_params(1),
    )(c_all, w_ada, b_loc)


def _w_ada_grad(sc_t, dmod_loc):
    d_model = sc_t.shape[0]
    n_layer, _, cols = dmod_loc.shape

    def body(s_ref, d_ref, out_ref):
        acc = s_ref[:, 0:1] * d_ref[0:1, :]
        for b in range(1, N_DEV):
            acc = acc + s_ref[:, b:b + 1] * d_ref[b:b + 1, :]
        out_ref[...] = acc

    return pl.pallas_call(
        body, name="w_ada_grad", grid=(n_layer,),
        in_specs=[pl.BlockSpec((d_model, N_DEV), lambda l: (0, 0)),
                  pl.BlockSpec((None, N_DEV, cols), lambda l: (l, 0, 0))],
        out_specs=pl.BlockSpec((None, d_model, cols), lambda l: (l, 0, 0)),
        out_shape=SDS((n_layer, d_model, cols), F32),
        compiler_params=_params(1),
    )(sc_t, dmod_loc)


def _sum_devices(g):
    _, rows, cols = g.shape

    def body(g_ref, out_ref):
        acc = g_ref[0]
        for i in range(1, N_DEV):
            acc = acc + g_ref[i]
        out_ref[...] = acc

    return pl.pallas_call(body, name="sum_devices", out_shape=SDS((rows, cols), F32))(g)


def _ffn_fwd(x, prm, w1r, w2g, l, sub, tm):
    t_len, d_model = x.shape
    f8, r8 = w1r.shape[-1], w2g.shape[-2]
    n_m = t_len // tm

    def body(x_ref, xn_ref, prm_ref, w1_ref, w2_ref, act_ref, h_ref, y_ref, xo_ref, h_sc):
        @pl.when(pl.program_id(0) == 0)
        def _():
            h_sc[...] = _ada_fwd(x_ref[...], prm_ref).astype(BF16)

        h = h_sc[...]
        h_ref[...] = h

        def first(j):
            return _dot(h, w1_ref[0, j]), _dot(h, w1_ref[1, j])

        gu = first(0)
        yv = None
        for j in range(N_CHUNK):
            g, u = gu
            if j + 1 < N_CHUNK:
                gu = first(j + 1)
            else:
                h_next = _ada_fwd(xn_ref[...], prm_ref).astype(BF16)
            sig = jax.nn.sigmoid(g)
            sil = g * sig
            s = (sil * u).astype(BF16)
            act_ref[0, j] = (u * (sig * (1.0 + g * (1.0 - sig)))).astype(BF16)
            act_ref[1, j] = sil.astype(BF16)
            act_ref[2, j] = s
            part = _dot(s, w2_ref[2 * j:2 * j + 2].reshape(2 * r8, d_model))
            yv = part if yv is None else yv + part
        y_ref[...] = yv.astype(BF16)
        xo_ref[...] = x_ref[...] + (0.5 * prm_ref[3:4, :]) * yv
        h_sc[...] = h_next

    tile = pl.BlockSpec((tm, d_model), lambda m: (m, 0))
    return pl.pallas_call(
        body, name=f"ffn_fwd_{l}_{sub}", grid=(n_m,),
        in_specs=[tile, pl.BlockSpec((tm, d_model), lambda m: (jnp.minimum(m + 1, n_m - 1), 0)),
                  pl.BlockSpec((8, d_model), lambda m: (0, 0)),
                  pl.BlockSpec((2, N_CHUNK, d_model, f8), lambda m: (0, 0, 0, 0), pipeline_mode=pl.Buffered(1)),
                  pl.BlockSpec((N_DEV, r8, d_model), lambda m: (0, 0, 0), pipeline_mode=pl.Buffered(1))],
        out_specs=[pl.BlockSpec((3, N_CHUNK, tm, f8), lambda m: (0, 0, m, 0)), tile, tile, tile],
        out_shape=[SDS((3, N_CHUNK, t_len, f8), BF16), SDS((t_len, d_model), BF16),
                   SDS((t_len, d_model), BF16), SDS((t_len, d_model), F32)],
        scratch_shapes=[pltpu.VMEM((tm, d_model), BF16)],
        compiler_params=_params(1),
    )(x, x, prm, w1r, w2g)


def _ffn_bwd(dxo, y, x_in, prm, act, w1r, w2g, l, sub, tm):
    t_len, d_model = x_in.shape
    f8, r8 = w1r.shape[-1], w2g.shape[-2]
    n_m = t_len // tm

    def body(dxo_ref, y_ref, x_ref, prm_ref, act_ref, w1_ref, w2_ref, dx_ref, da_ref, dy_ref, st_ref):
        @pl.when(pl.program_id(0) == 0)
        def _():
            st_ref[...] = jnp.zeros_like(st_ref)

        dxo_v = dxo_ref[...]
        dy = ((0.5 * prm_ref[3:4, :]) * dxo_v).astype(BF16)
        dy_ref[...] = dy
        st_ref[3:4, :] += jnp.sum(0.5 * dxo_v * y_ref[...].astype(F32), axis=0, keepdims=True)

        def first(j):
            return _dot_nt(dy, w2_ref[2 * j:2 * j + 2].reshape(2 * r8, d_model))

        ds = first(0)
        dh = None
        for j in range(N_CHUNK):
            ds_now = ds
            if j + 1 < N_CHUNK:
                ds = first(j + 1)
            dg = (ds_now * act_ref[0, j].astype(F32)).astype(BF16)
            du = (ds_now * act_ref[1, j].astype(F32)).astype(BF16)
            da_ref[0, j] = dg
            da_ref[1, j] = du
            part = _dot_nt(dg, w1_ref[0, j]) + _dot_nt(du, w1_ref[1, j])
            dh = part if dh is None else dh + part
        dx, dshift, dscale, dng = _ada_bwd(dh, x_ref[...], prm_ref)
        dx_ref[...] = dxo_v + dx
        st_ref[0:1, :] += dshift
        st_ref[1:2, :] += dscale
        st_ref[2:3, :] += dng

    tile = pl.BlockSpec((tm, d_model), lambda m: (m, 0))
    fixed = pl.BlockSpec((8, d_model), lambda m: (0, 0))
    return pl.pallas_call(
        body, name=f"ffn_bwd_{l}_{sub}", grid=(n_m,),
        in_specs=[tile, tile, tile, fixed,
                  pl.BlockSpec((2, N_CHUNK, tm, f8), lambda m: (0, 0, m, 0)),
                  pl.BlockSpec((2, N_CHUNK, d_model, f8), lambda m: (0, 0, 0, 0), pipeline_mode=pl.Buffered(1)),
                  pl.BlockSpec((N_DEV, r8, d_model), lambda m: (0, 0, 0), pipeline_mode=pl.Buffered(1))],
        out_specs=[tile, pl.BlockSpec((2, N_CHUNK, tm, f8), lambda m: (0, 0, m, 0)), tile, fixed],
        out_shape=[SDS((t_len, d_model), F32), SDS((2, N_CHUNK, t_len, f8), BF16),
                   SDS((t_len, d_model), BF16), SDS((8, d_model), F32)],
        compiler_params=_params(1),
    )(dxo, y, x_in, prm, act, w1r, w2g)


def _matmul_tn(a, b, a_spec, b_spec, out_shape, out_spec, n_piece, t_len, tk, name, after):
    def body(a_ref, b_ref, after_ref, o_ref, acc):
        k = pl.program_id(1)

        @pl.when(k == 0)
        def _():
            acc[...] = jnp.zeros_like(acc)

        m_out, n_out = acc.shape
        if (m_out // 2) % LANES == 0:
            for rows in _row_halves(m_out):
                acc[rows, :] += _dot_tn(a_ref[:, rows], b_ref[...])
        else:
            for cols in _row_halves(n_out):
                acc[:, cols] += _dot_tn(a_ref[...], b_ref[:, cols])

        @pl.when(k == pl.num_programs(1) - 1)
        def _():
            if len(o_ref.shape) == 3:
                cols = o_ref.shape[2]
                for q in range(o_ref.shape[0]):
                    o_ref[q] = acc[:, q * cols:(q + 1) * cols].astype(o_ref.dtype)
            else:
                o_ref[...] = acc[...].astype(o_ref.dtype)

    blk = tuple(d for d in out_spec.block_shape if d is not None)
    acc_shape = blk if len(blk) == 2 else (blk[1], blk[0] * blk[2])
    return pl.pallas_call(
        body, name=name, grid=(n_piece, t_len // tk),
        in_specs=[a_spec, b_spec, ANY], out_specs=out_spec, out_shape=out_shape,
        scratch_shapes=[pltpu.VMEM(acc_shape, F32)],
        compiler_params=_params(2),
    )(a, b, after)


def _groups(tm, width):
    return pltpu.VMEM((width // LANES, tm, LANES), F32)


def _to_groups(val, dst_ref, first=0):
    for g in range(val.shape[1] // LANES):
        dst_ref[first + g] = val[:, g * LANES:(g + 1) * LANES]


def _from_groups(src_ref):
    return jnp.concatenate([src_ref[g] for g in range(src_ref.shape[0])], axis=1)


def _split_residues(src_ref, out_ref, dil, tm):
    for g in range(src_ref.shape[0]):
        for r in range(dil):
            out_ref[r, :, g * LANES:(g + 1) * LANES] = (
                src_ref.at[g][pl.ds(r, tm // dil, stride=dil), :].astype(out_ref.dtype))


def _merge_residues(in_ref, dst_ref, dil, tm, add=False):
    for g in range(dst_ref.shape[0]):
        for r in range(dil):
            rows = pl.ds(r, tm // dil, stride=dil)
            val = in_ref[r, :, g * LANES:(g + 1) * LANES].astype(F32)
            dst = dst_ref.at[g]
            dst[rows, :] = (dst[rows, :] + val) if add else val


def _res_spec(dil, tm, width, index=lambda m: m):
    return pl.BlockSpec((dil, tm // dil, width), lambda m: (0, index(m), 0))


def _mixer_in(x, prm, wing, qkg, cprm, bd, l, tm):
    t_len, d_model = x.shape
    a_w = d_model // 2
    c_w = d_model - a_w
    c8 = wing.shape[-1]
    p_c = N_DEV * c8
    dils = DILATIONS[1:]

    def body(x_ref, prm_ref, w_ref, qkg_ref, cprm_ref, bd_ref, proj_ref, qkn_ref, h_ref, ycv_ref, *rest):
        res_refs, (pf, qsc, vsc, carry, wcat, wsems) = rest[:2 * len(dils)], rest[2 * len(dils):]

        @pl.when(pl.program_id(0) == 0)
        def _():
            carry[...] = jnp.zeros_like(carry)
            _join_columns(w_ref, wcat, wsems)

        h = _ada_fwd(x_ref[...], prm_ref).astype(BF16)
        h_ref[...] = h
        for cols in _row_halves(p_c):
            pf[:, cols] = _dot(h, wcat[:, cols])
        proj_ref[...] = pf[...].astype(BF16)
        bdm = bd_ref[...]
        for i in range(2):
            t = pf[:, i * a_w:(i + 1) * a_w]
            rstd = lax.rsqrt(_head_mean(t * t, bdm) + EPS)
            tn = t * rstd * qkg_ref[i:i + 1, :]
            qkn_ref[:, i * a_w:(i + 1) * a_w] = tn.astype(BF16)
            _to_groups(tn, qsc, i * (a_w // LANES))
        _to_groups(pf[:, 2 * a_w:3 * a_w], vsc)
        for i, dil in enumerate(dils):
            _split_residues(qsc, res_refs[2 * i], dil, tm)
            _split_residues(vsc, res_refs[2 * i + 1], dil, tm)
        gb = pf[:, 3 * a_w:3 * a_w + c_w]
        z = pf[:, 3 * a_w + c_w:3 * a_w + 2 * c_w] * pf[:, 3 * a_w + 2 * c_w:3 * a_w + 3 * c_w]
        z1, z2 = _shift_rows(z, carry[...], tm)
        cv = cprm_ref[0:1, :] * z2 + cprm_ref[1:2, :] * z1 + cprm_ref[2:3, :] * z + cprm_ref[3:4, :]
        ycv_ref[...] = (gb * cv).astype(BF16)
        carry[...] = z[tm - 8:tm, :]

    res_specs, res_shapes = [], []
    for dil in dils:
        res_specs += [_res_spec(dil, tm, 2 * a_w), _res_spec(dil, tm, a_w)]
        res_shapes += [SDS((dil, t_len // dil, 2 * a_w), BF16), SDS((dil, t_len // dil, a_w), BF16)]
    return pl.pallas_call(
        body, name=f"mixer_in_{l}", grid=(t_len // tm,),
        in_specs=[pl.BlockSpec((tm, d_model), lambda m: (m, 0)),
                  pl.BlockSpec((8, d_model), lambda m: (0, 0)),
                  ANY,
                  pl.BlockSpec((8, a_w), lambda m: (0, 0)),
                  pl.BlockSpec((8, c_w), lambda m: (0, 0)),
                  pl.BlockSpec(bd.shape, lambda m: (0, 0))],
        out_specs=[pl.BlockSpec((tm, p_c), lambda m: (m, 0)),
                   pl.BlockSpec((tm, 2 * a_w), lambda m: (m, 0)),
                   pl.BlockSpec((tm, d_model), lambda m: (m, 0)),
                   pl.BlockSpec((tm, c_w), lambda m: (m, 0))] + res_specs,
        out_shape=[SDS((t_len, p_c), BF16), SDS((t_len, 2 * a_w), BF16),
                   SDS((t_len, d_model), BF16), SDS((t_len, c_w), BF16)] + res_shapes,
        scratch_shapes=[pltpu.VMEM((tm, p_c), F32), _groups(tm, 2 * a_w), _groups(tm, a_w),
                        pltpu.VMEM((8, c_w), F32), pltpu.VMEM((d_model, p_c), BF16),
                        pltpu.SemaphoreType.DMA((N_DEV,))],
        compiler_params=_params(1),
    )(x, prm, wing, qkg, cprm, bd)


def _band_mask(first_block):
    qi = lax.broadcasted_iota(jnp.int32, (BLOCK, 2 * BLOCK), 0)
    kj = lax.broadcasted_iota(jnp.int32, (BLOCK, 2 * BLOCK), 1)
    lowest = jnp.where(first_block, BLOCK, 0)
    return (kj >= qi) & (kj <= qi + BLOCK) & (kj >= lowest)


def _blocks_per_step(seq):
    n_sb = min(ATTN_BLOCKS_PER_STEP, seq // BLOCK)
    assert seq % (n_sb * BLOCK) == 0
    return n_sb


def _block_rows(sb):
    return slice(sb * BLOCK, (sb + 1) * BLOCK)


def _with_previous(prev_ref, cur_ref, sb, sl):
    before = prev_ref[:, sl] if sb == 0 else cur_ref[_block_rows(sb - 1), sl]
    return jnp.concatenate([before, cur_ref[_block_rows(sb), sl]], axis=0)


def _heads_on_lanes(cols):
    width = LANES // len(cols)
    lane = lax.broadcasted_iota(jnp.int32, (BLOCK, LANES), 1)
    out = jnp.broadcast_to(cols[-1], (BLOCK, LANES))
    for h in range(len(cols) - 2, -1, -1):
        out = jnp.where(lane < (h + 1) * width, cols[h], out)
    return out


def _attn_fwd(qk, v, v_col, l):
    dil, seq, a2 = qk.shape
    a_w = a2 // 2
    n_sb = _blocks_per_step(seq)
    rows_step = n_sb * BLOCK
    n_pair = a_w // LANES

    def body(q_ref, kc_ref, vc_ref, o_ref, lse_ref, kp_ref, vp_ref):
        @pl.when(pl.program_id(1) == 0)
        def _():
            kp_ref[...] = jnp.zeros_like(kp_ref)
            vp_ref[...] = jnp.zeros_like(vp_ref)

        masks = [_band_mask(pl.program_id(1) == 0)] + [_band_mask(False)] * (n_sb - 1)
        low = lax.broadcasted_iota(jnp.int32, (BLOCK, LANES), 1) < HEAD_DIM
        units = [(sb, slice(p * LANES, (p + 1) * LANES)) for sb in range(n_sb) for p in range(a_w // LANES)]
        scores = []
        for sb, sl in units:
            qp = q_ref[_block_rows(sb), sl]
            k2 = _with_previous(kp_ref, kc_ref, sb, sl)
            scores.append([_dot_nt(jnp.where(half, qp, jnp.zeros_like(qp)), k2) for half in (low, ~low)])
        probs, lses = [], []
        for (sb, sl), pair in zip(units, scores):
            pr, ls = [], []
            for s in pair:
                s = jnp.where(masks[sb], s * SM_SCALE, NEG)
                mx = jnp.max(s, axis=-1, keepdims=True)
                e = jnp.exp(s - mx)
                den = jnp.sum(e, axis=-1, keepdims=True)
                pr.append((e * (1.0 / den)).astype(BF16))
                ls.append(mx + jnp.log(den))
            probs.append(pr)
            lses.append(ls)
        for (sb, sl), pr, ls in zip(units, probs, lses):
            v2 = _with_previous(vp_ref, vc_ref, sb, sl)
            o_ref[_block_rows(sb), sl] = jnp.where(low, _dot(pr[0], v2), _dot(pr[1], v2)).astype(BF16)
        for sb in range(n_sb):
            cols = [c for ls in lses[sb * n_pair:(sb + 1) * n_pair] for c in ls]
            lse_ref[_block_rows(sb), :] = _heads_on_lanes(cols)
        kp_ref[...] = kc_ref[_block_rows(n_sb - 1), :]
        vp_ref[...] = vc_ref[_block_rows(n_sb - 1), :]

    def blk(f):
        return pl.BlockSpec((None, rows_step, a_w), f)

    return pl.pallas_call(
        body, name=f"attn_fwd_{l}_d{dil}", grid=(dil, seq // rows_step),
        in_specs=[blk(lambda r, b: (r, b, 0)),
                  blk(lambda r, b: (r, b, 1)),
                  blk(lambda r, b: (r, b, v_col))],
        out_specs=[blk(lambda r, b: (r, b, 0)), pl.BlockSpec((None, rows_step, LANES), lambda r, b: (r, b, 0))],
        out_shape=[SDS((dil, seq, a_w), BF16), SDS((dil, seq, LANES), F32)],
        scratch_shapes=[pltpu.VMEM((BLOCK, a_w), BF16), pltpu.VMEM((BLOCK, a_w), BF16)],
        compiler_params=_params(2),
    )(qk, qk, v)


def _attn_combine(outs, lses, ycv, expand, tm, l):
    t_len, c_w = ycv.shape
    a_w = outs[0].shape[-1]
    dils = DILATIONS[1:]
    n_r = len(dils)

    def body(o1, l1, *rest):
        o_res, l_res, y_ref, ex_ref = rest[:n_r], rest[n_r:2 * n_r], rest[2 * n_r], rest[2 * n_r + 1]
        cat_ref, lse_ref = rest[2 * n_r + 2], rest[2 * n_r + 3]
        ya_res, lse_res = rest[2 * n_r + 4:3 * n_r + 4], rest[3 * n_r + 4:4 * n_r + 4]
        scr = rest[4 * n_r + 4:]
        so, sl, sya, slse = scr[:n_r], scr[n_r:2 * n_r], scr[2 * n_r], scr[2 * n_r + 1]
        for i, dil in enumerate(dils):
            _merge_residues(o_res[i], so[i], dil, tm)
            _merge_residues(l_res[i], sl[i], dil, tm)
        ls = [l1[...]] + [s[0] for s in sl]
        mx = jnp.maximum(jnp.maximum(ls[0], ls[1]), ls[2])
        es = [jnp.exp(t - mx) for t in ls]
        den = es[0] + es[1] + es[2]
        inv = 1.0 / den
        slse[0] = mx + jnp.log(den)
        lse_ref[...] = slse[0]
        ex = ex_ref[...]
        ov = [o1[...].astype(F32)] + [_from_groups(s) for s in so]
        ya = jnp.zeros((tm, a_w), F32)
        for e, o in zip(es, ov):
            hi, lo = _split_bf16(e * inv)
            ya = ya + (_dot(hi, ex) + _dot(lo, ex)) * o
        _to_groups(ya, sya)
        cat_ref[:, :a_w] = ya.astype(BF16)
        cat_ref[:, a_w:] = y_ref[...]
        for i, dil in enumerate(dils):
            _split_residues(sya, ya_res[i], dil, tm)
            _split_residues(slse, lse_res[i], dil, tm)

    ta = pl.BlockSpec((tm, a_w), lambda m: (m, 0))
    tl = pl.BlockSpec((tm, LANES), lambda m: (m, 0))
    res_a = [_res_spec(dil, tm, a_w) for dil in dils]
    res_l = [_res_spec(dil, tm, LANES) for dil in dils]
    return pl.pallas_call(
        body, name=f"attn_combine_{l}", grid=(t_len // tm,),
        in_specs=[ta, tl] + res_a + res_l + [pl.BlockSpec((tm, c_w), lambda m: (m, 0)),
                                            pl.BlockSpec((LANES, a_w), lambda m: (0, 0))],
        out_specs=[pl.BlockSpec((tm, a_w + c_w), lambda m: (m, 0)), tl] + res_a + res_l,
        out_shape=[SDS((t_len, a_w + c_w), BF16), SDS((t_len, LANES), F32)]
        + [SDS((dil, t_len // dil, a_w), BF16) for dil in dils]
        + [SDS((dil, t_len // dil, LANES), F32) for dil in dils],
        scratch_shapes=[_groups(tm, a_w)] * n_r + [_groups(tm, LANES)] * n_r + [_groups(tm, a_w), _groups(tm, LANES)],
        compiler_params=_params(1),
    )(outs[0].reshape(t_len, a_w), lses[0].reshape(t_len, LANES), *outs[1:], *lses[1:], ycv, expand)


def _mixer_out(cat, x, prm, woutg, l, tm):
    t_len, d_model = x.shape
    r8 = woutg.shape[-2]

    def body(cat_ref, x_ref, prm_ref, w_ref, xo_ref, y_ref):
        w = w_ref[...].reshape(N_DEV * r8, d_model)
        for rows in _row_halves(tm):
            yv = _dot(cat_ref[rows, :], w)
            y_ref[rows, :] = yv.astype(BF16)
            xo_ref[rows, :] = x_ref[rows, :] + prm_ref[3:4, :] * yv

    tile = pl.BlockSpec((tm, d_model), lambda m: (m, 0))
    return pl.pallas_call(
        body, name=f"mixer_out_{l}", grid=(t_len // tm,),
        in_specs=[tile, tile, pl.BlockSpec((8, d_model), lambda m: (0, 0)),
                  pl.BlockSpec((N_DEV, r8, d_model), lambda m: (0, 0, 0))],
        out_specs=[tile, tile],
        out_shape=[SDS((t_len, d_model), F32), SDS((t_len, d_model), BF16)],
        compiler_params=_params(1),
    )(cat, x, prm, woutg)


def _mixer_bwd_out(dxo, ymix, prm, woutg, proj, cprm, l, tm):
    t_len, d_model = dxo.shape
    a_w = d_model // 2
    c_w = d_model - a_w
    r8 = woutg.shape[-2]
    n_m = t_len // tm
    hb = tm // 8

    dils = DILATIONS[1:]

    def body(dxo_ref, ym_ref, prm_ref, w_ref, pc_ref, halo_ref, cprm_ref,
             dya_ref, dpc_ref, dy_ref, st_ref, cs_ref, *rest):
        dya_res, (carry, dsc) = rest[:len(dils)], rest[len(dils):]
        step = pl.program_id(0)
        tile_i = n_m - 1 - step

        @pl.when(step == 0)
        def _():
            st_ref[...] = jnp.zeros_like(st_ref)
            cs_ref[...] = jnp.zeros_like(cs_ref)
            carry[...] = jnp.zeros_like(carry)

        dxo_v = dxo_ref[...]
        dy = (prm_ref[3:4, :] * dxo_v).astype(BF16)
        dy_ref[...] = dy
        st_ref[3:4, :] += jnp.sum(dxo_v * ym_ref[...].astype(F32), axis=0, keepdims=True)
        w = w_ref[...].reshape(N_DEV * r8, d_model)
        dcat = jnp.concatenate([_dot_nt(dy[rows, :], w) for rows in _row_halves(tm)], axis=0)
        _to_groups(dcat[:, :a_w], dsc)
        dya_ref[...] = dcat[:, :a_w].astype(BF16)
        for i, dil in enumerate(dils):
            _split_residues(dsc, dya_res[i], dil, tm)
        dyc = dcat[:, a_w:]
        gb = pc_ref[:, :c_w].astype(F32)
        gc = pc_ref[:, c_w:2 * c_w].astype(F32)
        u = pc_ref[:, 2 * c_w:].astype(F32)
        z = gc * u
        before = halo_ref[:, c_w:2 * c_w].astype(F32) * halo_ref[:, 2 * c_w:].astype(F32)
        before = jnp.where(tile_i > 0, before, jnp.zeros_like(before))
        z1, z2 = _shift_rows(z, before, tm)
        w0, w1, w2 = cprm_ref[0:1, :], cprm_ref[1:2, :], cprm_ref[2:3, :]
        cv = w0 * z2 + w1 * z1 + w2 * z + cprm_ref[3:4, :]
        dcv = dyc * gb
        cs_ref[0:1, :] += jnp.sum(dcv * z2, axis=0, keepdims=True)
        cs_ref[1:2, :] += jnp.sum(dcv * z1, axis=0, keepdims=True)
        cs_ref[2:3, :] += jnp.sum(dcv * z, axis=0, keepdims=True)
        cs_ref[3:4, :] += jnp.sum(dcv, axis=0, keepdims=True)
        row = lax.broadcasted_iota(jnp.int32, dcv.shape, 0)
        after = carry[...]
        d1 = jnp.where(row == tm - 1, after[0:1, :], pltpu.roll(dcv, tm - 1, 0))
        d2 = jnp.where(row == tm - 2, after[0:1, :],
                       jnp.where(row == tm - 1, after[1:2, :], pltpu.roll(dcv, tm - 2, 0)))
        dz = w2 * dcv + w1 * d1 + w0 * d2
        dpc_ref[:, :c_w] = (dyc * cv).astype(BF16)
        dpc_ref[:, c_w:2 * c_w] = (dz * u).astype(BF16)
        dpc_ref[:, 2 * c_w:] = (dz * gc).astype(BF16)
        carry[...] = dcv[0:8, :]

    def rev(width, col=0):
        return pl.BlockSpec((tm, width), lambda s: (n_m - 1 - s, col))

    fixed_d = pl.BlockSpec((8, d_model), lambda s: (0, 0))
    fixed_c = pl.BlockSpec((8, c_w), lambda s: (0, 0))
    return pl.pallas_call(
        body, name=f"mixer_bwd_out_{l}", grid=(n_m,),
        in_specs=[rev(d_model), rev(d_model), fixed_d,
                  pl.BlockSpec((N_DEV, r8, d_model), lambda s: (0, 0, 0)),
                  rev(3 * c_w, 1),
                  pl.BlockSpec((8, 3 * c_w), lambda s: (jnp.maximum((n_m - 1 - s) * hb - 1, 0), 1)),
                  fixed_c],
        out_specs=[rev(a_w), rev(3 * c_w), rev(d_model), fixed_d, fixed_c]
        + [_res_spec(dil, tm, a_w, lambda s: n_m - 1 - s) for dil in dils],
        out_shape=[SDS((t_len, a_w), BF16), SDS((t_len, 3 * c_w), BF16), SDS((t_len, d_model), BF16),
                   SDS((8, d_model), F32), SDS((8, c_w), F32)]
        + [SDS((dil, t_len // dil, a_w), BF16) for dil in dils],
        scratch_shapes=[pltpu.VMEM((8, c_w), F32), _groups(tm, a_w)],
        compiler_params=_params(1),
    )(dxo, ymix, prm, woutg, proj, proj, cprm)


def _attn_bwd(qk, v, v_col, dya, ya, lse, l):
    dil, seq, a2 = qk.shape
    a_w = a2 // 2
    n_sb = _blocks_per_step(seq)
    rows_step = n_sb * BLOCK
    n_step = seq // rows_step
    last = _block_rows(n_sb - 1)
    lse_w = LANES // (a_w // HEAD_DIM)
    single = n_step == 1

    def body(q_ref, kc_ref, vc_ref, do_ref, ya_ref, lse_ref,
             dq_ref, dk_ref, dv_ref, hold_k, hold_v, acc_k, acc_v, kp_ref, vp_ref):
        j = pl.program_id(1)

        @pl.when(j == 0)
        def _():
            kp_ref[...] = jnp.zeros_like(kp_ref)
            vp_ref[...] = jnp.zeros_like(vp_ref)

        @pl.when((pl.program_id(0) == 0) & (j == 0))
        def _():
            hold_k[...] = jnp.zeros_like(hold_k)
            hold_v[...] = jnp.zeros_like(hold_v)

        @pl.when(j < n_step)
        def _():
            masks = [_band_mask(j == 0)] + [_band_mask(False)] * (n_sb - 1)
            low = lax.broadcasted_iota(jnp.int32, (BLOCK, LANES), 1) < HEAD_DIM
            units = [(sb, slice(p * LANES, (p + 1) * LANES)) for sb in range(n_sb) for p in range(a_w // LANES)]
            acc_k[...] = jnp.zeros_like(acc_k)
            acc_v[...] = jnp.zeros_like(acc_v)
            heads, scores, dps = [], [], []
            for sb, sl in units:
                qp = q_ref[_block_rows(sb), sl]
                k2 = _with_previous(kp_ref, kc_ref, sb, sl)
                v2 = _with_previous(vp_ref, vc_ref, sb, sl)
                dop = do_ref[_block_rows(sb), sl]
                prod = dop.astype(F32) * ya_ref[_block_rows(sb), sl].astype(F32)
                lsep = lse_ref[_block_rows(sb), :]
                first = 2 * (sl.start // LANES) * lse_w
                for hh, half in enumerate((low, ~low)):
                    qh = jnp.where(half, qp, jnp.zeros_like(qp))
                    doh = jnp.where(half, dop, jnp.zeros_like(dop))
                    dsum = jnp.sum(jnp.where(half, prod, 0.0), axis=-1, keepdims=True)
                    heads.append((qh, doh, dsum, lsep[:, first + hh * lse_w:first + hh * lse_w + 1], masks[sb]))
                    scores.append(_dot_nt(qh, k2))
                    dps.append(_dot_nt(doh, v2))
            dss, prs = [], []
            for (qh, doh, dsum, lse_h, valid), s, dp in zip(heads, scores, dps):
                pr = jnp.exp(jnp.where(valid, s * SM_SCALE, NEG) - lse_h)
                dss.append((pr * (dp - dsum) * SM_SCALE).astype(BF16))
                prs.append(pr.astype(BF16))
            for i, (sb, sl) in enumerate(units):
                k2 = _with_previous(kp_ref, kc_ref, sb, sl)
                h0, h1 = heads[2 * i], heads[2 * i + 1]
                dq_ref[_block_rows(sb), sl] = jnp.where(low, _dot(dss[2 * i], k2),
                                                        _dot(dss[2 * i + 1], k2)).astype(BF16)
                two = slice(sb * BLOCK, (sb + 2) * BLOCK)
                acc_k[two, sl] += _dot_tn(dss[2 * i], h0[0]) + _dot_tn(dss[2 * i + 1], h1[0])
                acc_v[two, sl] += _dot_tn(prs[2 * i], h0[1]) + _dot_tn(prs[2 * i + 1], h1[1])
            for out_ref, hold, acc in ((dk_ref, hold_k, acc_k), (dv_ref, hold_v, acc_v)):
                if single:
                    out_ref[...] = acc[BLOCK:, :].astype(BF16)
                    continue
                if n_sb > 1:
                    out_ref[:rows_step - BLOCK, :] = hold[:rows_step - BLOCK, :].astype(BF16)
                out_ref[last, :] = (hold[last, :] + acc[:BLOCK, :]).astype(BF16)
                hold[...] = acc[BLOCK:, :]
            kp_ref[...] = kc_ref[last, :]
            vp_ref[...] = vc_ref[last, :]

        @pl.when(j == n_step)
        def _():
            dk_ref[...] = hold_k[...].astype(BF16)
            dv_ref[...] = hold_v[...].astype(BF16)

    def blk(f):
        return pl.BlockSpec((None, rows_step, a_w), f)

    def cur(j):
        return jnp.minimum(j, n_step - 1)

    def late(j):
        return jnp.maximum(j - 1, 0)

    return pl.pallas_call(
        body, name=f"attn_bwd_{l}_d{dil}", grid=(dil, 1 if single else n_step + 1),
        in_specs=[blk(lambda r, j: (r, cur(j), 0)),
                  blk(lambda r, j: (r, cur(j), 1)),
                  blk(lambda r, j: (r, cur(j), v_col)),
                  blk(lambda r, j: (r, cur(j), 0)),
                  blk(lambda r, j: (r, cur(j), 0)),
                  pl.BlockSpec((None, rows_step, LANES), lambda r, j: (r, cur(j), 0))],
        out_specs=[blk(lambda r, j: (r, cur(j), 0)), blk(lambda r, j: (r, late(j), 0)),
                   blk(lambda r, j: (r, late(j), 0))],
        out_shape=[SDS((dil, seq, a_w), BF16)] * 3,
        scratch_shapes=[pltpu.VMEM((rows_step, a_w), F32), pltpu.VMEM((rows_step, a_w), F32),
                        pltpu.VMEM((rows_step + BLOCK, a_w), F32), pltpu.VMEM((rows_step + BLOCK, a_w), F32),
                        pltpu.VMEM((BLOCK, a_w), BF16), pltpu.VMEM((BLOCK, a_w), BF16)],
        compiler_params=_params(2),
    )(qk, qk, v, dya, ya, lse)


def _mixer_bwd_in(parts, proj, dpc, dxo, x_in, prm, wing, qkg, bd, l, tm):
    t_len, d_model = x_in.shape
    a_w = d_model // 2
    c_w = d_model - a_w
    c8 = wing.shape[-1]
    p_c = N_DEV * c8

    dils = DILATIONS[1:]
    n_in = 3 * len(DILATIONS)

    def body(*refs):
        nat, res = refs[:3], refs[3:n_in]
        (pa_ref, dpc_ref, dxo_ref, x_ref, prm_ref, w_ref, qkg_ref, bd_ref,
         dx_ref, dproj_ref, st_ref, qs_ref) = refs[n_in:n_in + 12]
        sums, wcat, wsems = refs[n_in + 12:n_in + 15], refs[n_in + 15], refs[n_in + 16]

        @pl.when(pl.program_id(0) == 0)
        def _():
            st_ref[...] = jnp.zeros_like(st_ref)
            qs_ref[...] = jnp.zeros_like(qs_ref)
            _join_columns(w_ref, wcat, wsems)

        for t in range(3):
            _to_groups(nat[t][...].astype(F32), sums[t])
            for i, dil in enumerate(dils):
                _merge_residues(res[3 * i + t], sums[t], dil, tm, add=True)
        bdm = bd_ref[...]
        grads = [_from_groups(sums[0]), _from_groups(sums[1])]
        for i in range(2):
            t = pa_ref[:, i * a_w:(i + 1) * a_w].astype(F32)
            rstd = lax.rsqrt(_head_mean(t * t, bdm) + EPS)
            that = t * rstd
            qs_ref[i:i + 1, :] += jnp.sum(grads[i] * that, axis=0, keepdims=True)
            dhat = grads[i] * qkg_ref[i:i + 1, :]
            dt = rstd * (dhat - that * _head_mean(dhat * that, bdm))
            dproj_ref[:, i * a_w:(i + 1) * a_w] = dt.astype(BF16)
        dproj_ref[:, 2 * a_w:3 * a_w] = _from_groups(sums[2]).astype(BF16)
        dproj_ref[:, 3 * a_w:] = dpc_ref[...]
        dh = jnp.concatenate([_dot_nt(dproj_ref[rows, :], wcat[...]) for rows in _row_halves(tm)], axis=0)
        dx, dshift, dscale, dng = _ada_bwd(dh, x_ref[...], prm_ref)
        dx_ref[...] = dxo_ref[...] + dx
        st_ref[0:1, :] += dshift
        st_ref[1:2, :] += dscale
        st_ref[2:3, :] += dng

    ta = pl.BlockSpec((tm, a_w), lambda m: (m, 0))
    tile = pl.BlockSpec((tm, d_model), lambda m: (m, 0))
    return pl.pallas_call(
        body, name=f"mixer_bwd_in_{l}", grid=(t_len // tm,),
        in_specs=[ta] * 3 + [_res_spec(dil, tm, a_w) for dil in dils for _ in range(3)]
                 + [pl.BlockSpec((tm, 3 * a_w), lambda m: (m, 0)),
                             pl.BlockSpec((tm, 3 * c_w), lambda m: (m, 0)),
                             tile, tile, pl.BlockSpec((8, d_model), lambda m: (0, 0)),
                             ANY,
                             pl.BlockSpec((8, a_w), lambda m: (0, 0)),
                             pl.BlockSpec(bd.shape, lambda m: (0, 0))],
        out_specs=[tile, pl.BlockSpec((tm, p_c), lambda m: (m, 0)),
                   pl.BlockSpec((8, d_model), lambda m: (0, 0)), pl.BlockSpec((8, a_w), lambda m: (0, 0))],
        out_shape=[SDS((t_len, d_model), F32), SDS((t_len, p_c), BF16), SDS((8, d_model), F32), SDS((8, a_w), F32)],
        scratch_shapes=[_groups(tm, a_w)] * 3 + [pltpu.VMEM((d_model, p_c), BF16), pltpu.SemaphoreType.DMA((N_DEV,))],
        compiler_params=_params(1),
    )(*[t.reshape(t_len, a_w) for t in parts[0]], *[t for p in parts[1:] for t in p],
      proj, dpc, dxo, x_in, prm, wing, qkg, bd)


def _loss_head(x, target, tm):
    t_len, d_model = x.shape

    def body(x_ref, t_ref, dx_ref, loss_ref):
        @pl.when(pl.program_id(0) == 0)
        def _():
            loss_ref[...] = jnp.zeros_like(loss_ref)

        diff = x_ref[...] - t_ref[...]
        dx_ref[...] = diff * (1.0 / d_model)
        per_token = jnp.sum(diff * diff, axis=-1, keepdims=True) * (1.0 / d_model)
        loss_ref[...] += 0.5 * jnp.sum(per_token)

    tile = pl.BlockSpec((tm, d_model), lambda m: (m, 0))
    return pl.pallas_call(
        body, name="loss_head", grid=(t_len // tm,),
        in_specs=[tile, tile],
        out_specs=[tile, pl.BlockSpec((8, LANES), lambda m: (0, 0))],
        out_shape=[SDS((t_len, d_model), F32), SDS((8, LANES), F32)],
        compiler_params=_params(1),
    )(x, target)


def _row_tile(rows, cols):
    best = None
    for t in range(16, rows + 1, 16):
        if rows % t == 0 and t * cols * 4 <= (1 << 20):
            best = t
    return best if best is not None else rows


def _adamw(pieces, w, m, v, name, first_q=0, earlier=None):
    n_q = len(pieces)
    n_p, rows, cols = pieces[0].shape
    tr = _row_tile(rows, cols)
    n_i = rows // tr
    n_e = 0 if earlier is None else 4
    c1 = 1.0 - ADAM_B1 ** ADAM_STEP
    c2 = 1.0 - ADAM_B2 ** ADAM_STEP

    def body(*refs):
        p_refs = refs[:n_q]
        w_ref, m_ref, v_ref = refs[n_q:n_q + 3]
        g_ref, d_ref, nm_ref, nv_ref = refs[n_q + 3 + n_e:]
        for q in range(n_q):
            @pl.when(pl.program_id(0) == q)
            def _():
                g = p_refs[q][0].astype(F32)
                for i in range(1, n_p):
                    g = g + p_refs[q][i].astype(F32)
                g_ref[...] = g
                nm = ADAM_B1 * m_ref[...] + (1.0 - ADAM_B1) * g
                nv = ADAM_B2 * v_ref[...] + (1.0 - ADAM_B2) * (g * g)
                nm_ref[...] = nm
                nv_ref[...] = nv
                d_ref[...] = -ADAM_LR * ((nm / c1) / (jnp.sqrt(nv / c2) + ADAM_EPS) + ADAM_WD * w_ref[...])

    def piece_spec(q):
        return pl.BlockSpec((n_p, tr, cols), lambda a, i: (0, jnp.where(a == q, i, 0), 0))

    tile = pl.BlockSpec((tr, cols), lambda a, i: ((first_q + a) * n_i + i, 0))
    return pl.pallas_call(
        body, name=name, grid=(n_q, n_i),
        in_specs=[piece_spec(q) for q in range(n_q)] + [tile, tile, tile] + [ANY] * n_e,
        out_specs=[tile] * 4,
        out_shape=[SDS(w.shape, F32)] * 4,
        input_output_aliases={n_q + 3 + k: k for k in range(n_e)},
        compiler_params=_params(2),
    )(*pieces, w, m, v, *(earlier or []))


def _pack(vecs):
    flat = jnp.concatenate([v.reshape(-1).astype(F32) for v in vecs])
    rows = -(-flat.shape[0] // (8 * LANES)) * 8
    return jnp.pad(flat, (0, rows * LANES - flat.shape[0])).reshape(rows, LANES)


def _unpack(packed, shapes):
    lead = packed.shape[:-2]
    flat = packed.reshape(lead + (-1,))
    out, off = [], 0
    for s in shapes:
        size = 1
        for d in s:
            size *= d
        out.append(flat[..., off:off + size].reshape(lead + tuple(s)))
        off += size
    return out


def kernel(x, c, w_ada, b_ada, norm_g, w_in, q_norm_g, k_norm_g, conv_w, conv_b, w_out, ffn_w1, ffn_w2, loss_target, m_w_ada, m_b_ada, m_norm_g, m_w_in, m_q_norm_g, m_k_norm_g, m_conv_w, m_conv_b, m_w_out, m_ffn_w1, m_ffn_w2, v_w_ada, v_b_ada, v_norm_g, v_w_in, v_q_norm_g, v_k_norm_g, v_conv_w, v_conv_b, v_w_out, v_ffn_w1, v_ffn_w2):
    _, t_len, d_model = x.shape
    n_layer = w_ada.shape[0]
    a_w = d_model // 2
    c_w = d_model - a_w
    n_head = a_w // HEAD_DIM
    ada_cols = w_ada.shape[-1]
    tm = min(TOKEN_TILE, t_len)
    tm_ffn = min(FFN_FWD_TOKEN_TILE, t_len)
    tm_ffn_bwd = min(FFN_BWD_TOKEN_TILE, t_len)
    me = _index(_my_place())
    x2 = x.reshape(t_len, d_model)
    target = loss_target.reshape(t_len, d_model)

    small_shapes = [(d_model,), norm_g.shape, conv_w.shape]
    gathered = _allgather_small(_pack([c, norm_g, conv_w]), "gather_small_inputs")
    c_all, ng_parts, cw_parts = _unpack(gathered, small_shapes)
    norm_g_full = jnp.moveaxis(ng_parts, 0, 2).reshape(n_layer, 3, d_model)
    conv_w_full = jnp.moveaxis(cw_parts, 0, 2).reshape(n_layer, 3, c_w)

    b_loc = lax.dynamic_slice_in_dim(b_ada, me * ada_cols, ada_cols, axis=1).reshape(n_layer, 1, ada_cols)
    mod_part, silu_c = _mod_part(c_all, w_ada, b_loc)
    mod_all = _allgather_small(_pack([mod_part]), "gather_mod")
    mod_all = _unpack(mod_all, [mod_part.shape])[0]
    mod_mine = lax.dynamic_index_in_dim(mod_all, me, axis=2, keepdims=False)
    mod = jnp.moveaxis(mod_mine, 0, 1).reshape(n_layer, 3, 3, d_model)

    def prm_of(l, sub):
        rows = jnp.stack([norm_g_full[l, sub], mod[l, sub, 0], mod[l, sub, 1], mod[l, sub, 2]])
        return jnp.pad(rows, ((0, 4), (0, 0)))

    prm = [[prm_of(l, sub) for sub in range(3)] for l in range(n_layer)]
    qkg = [jnp.pad(jnp.stack([jnp.tile(q_norm_g[l], n_head), jnp.tile(k_norm_g[l], n_head)]), ((0, 6), (0, 0)))
           for l in range(n_layer)]
    cprm = [jnp.pad(jnp.concatenate([conv_w_full[l], conv_b[l][None]]), ((0, 4), (0, 0))) for l in range(n_layer)]
    head_of = jnp.arange(min(MXU_WIDTH, a_w)) // HEAD_DIM
    bd = (head_of[:, None] == head_of[None, :]).astype(BF16)
    expand = (jnp.arange(LANES)[:, None] == (jnp.arange(a_w)[None, :] // HEAD_DIM) * (LANES // n_head)).astype(BF16)

    f8 = ffn_w1.shape[-1]
    r8 = ffn_w2.shape[-2]
    c8 = w_in.shape[-1]
    o8 = w_out.shape[-2]
    me_arr = jnp.reshape(me, (1,)).astype(jnp.int32)
    win_b = [_place_own(w_in, (l,), me_arr, f"own_w_in_{l}") for l in range(n_layer)]
    wout_b = [_place_own(w_out, (l,), me_arr, f"own_w_out_{l}") for l in range(n_layer)]
    w1_b = [[_place_own(ffn_w1, (l, s), me_arr, f"own_w1_{l}_{s}") for s in range(2)] for l in range(n_layer)]
    w2_b = [[_place_own(ffn_w2, (l, s), me_arr, f"own_w2_{l}_{s}") for s in range(2)] for l in range(n_layer)]
    w1_first, w2_first = _allgather_big([w1_b[0][0], w2_b[0][0]], "gather_first_weights")
    groups = [[win_b[0], wout_b[0]], [w1_b[0][1], w2_b[0][1]]]
    for l in range(1, n_layer):
        groups += [[w1_b[l][0], w2_b[l][0]], [win_b[l], wout_b[l]], [w1_b[l][1], w2_b[l][1]]]
    flat = [a for grp in groups for a in grp]
    w_sends, w_recvs, w_lands, w_token = _send_start(flat, None, w1_first, "weights_start")

    def gathered(gi, after):
        lo = sum(len(grp) for grp in groups[:gi])
        hi = lo + len(groups[gi])
        return _send_wait(w_lands[lo:hi], w_sends[lo:hi], w_recvs[lo:hi], after, f"weights_wait_{gi}")

    def chunked(w1g):
        return w1g.reshape(2, N_CHUNK, d_model, f8)

    saved = []
    xc = x2
    for l in range(n_layer):
        keep = {"x0": xc}
        if l == 0:
            w1g, w2g = w1_first, w2_first
            prm_first = prm[0][0] + w_token[0, 0]
        else:
            w1g, w2g = gathered(3 * l - 1, xc)
            prm_first = prm[l][0]
        keep["w_ffn0"] = (chunked(w1g), w2g)
        keep["act0"], keep["h0"], keep["y0"], xc = _ffn_fwd(xc, prm_first, *keep["w_ffn0"], l, 0, tm_ffn)
        keep["x1"] = xc
        wing, woutg = gathered(0 if l == 0 else 3 * l, xc)
        keep.update(wing=wing, woutg=woutg)
        proj, qkn, keep["h1"], ycv, *res = _mixer_in(xc, prm[l][1], wing, qkg[l], cprm[l], bd, l, tm)
        qkv = [(qkn[None], proj[None], 2)] + [(res[2 * i], res[2 * i + 1], 0) for i in range(len(DILATIONS) - 1)]
        branches = [_attn_fwd(*t, l) for t in qkv]
        cat, lse, *res = _attn_combine([b[0] for b in branches], [b[1] for b in branches], ycv, expand, tm, l)
        n_r = len(DILATIONS) - 1
        keep["ya"] = [cat[None]] + res[:n_r]
        keep["lse"] = [lse[None]] + res[n_r:]
        xc, keep["ymix"] = _mixer_out(cat, xc, prm[l][1], woutg, l, tm)
        keep.update(proj=proj, qkv=qkv, cat=cat)
        keep["x2"] = xc
        w1g, w2g = gathered(1 if l == 0 else 3 * l + 1, xc)
        keep["w_ffn2"] = (chunked(w1g), w2g)
        keep["act2"], keep["h2"], keep["y2"], xc = _ffn_fwd(xc, prm[l][2], *keep["w_ffn2"], l, 1, tm_ffn)
        saved.append(keep)

    dx, loss_blk = _loss_head(xc, target, tm)
    loss = lax.psum(loss_blk[0, 0], ("x", "y", "c"))

    tk = min(GRAD_TOKEN_TILE, t_len)
    stats =[[None] * 3 for _ in range(n_layer)]
    conv_stats = [None] * n_layer
    qk_stats = [None] * n_layer

    started = {}
    dummy = jnp.zeros((8, LANES), F32)

    def start_exchange(g, name):
        sends, recvs, thru, token = _send_start(
            [_place_own(g, None, me_arr, f"own_grad_{name}")], [g], dummy, f"grads_start_{name}")
        started[name] = (thru[0], thru[1], sends[0], recvs[0], token)
        return token

    def wait_exchange(names, after, name):
        ent = [started[k] for k in names]
        done = _send_wait([e[0] for e in ent] + [e[1] for e in ent], [e[2] for e in ent], [e[3] for e in ent],
                          after, name)
        return dict(zip(names, done))

    def ffn_backward(dxo, keep, l, sub, which, prm_dep):
        dx_in, da, dy, st = _ffn_bwd(dxo, keep[f"y{which}"], keep[f"x{which}"], prm_dep,
                                     keep[f"act{which}"], *keep[f"w_ffn{which}"], l, sub, tm_ffn_bwd)
        g_w2 = _matmul_tn(
            keep[f"act{which}"], dy,
            pl.BlockSpec((None, None, tk, f8), lambda p, k: (2, p, k, 0)),
            pl.BlockSpec((tk, d_model), lambda p, k: (k, 0)),
            SDS((N_CHUNK, f8, d_model), BF16), pl.BlockSpec((None, f8, d_model), lambda p, k: (p, 0, 0)),
            N_CHUNK, t_len, tk, f"grad_w2_{l}_{sub}", dummy)
        token = start_exchange(g_w2.reshape(N_DEV, r8, d_model), f"w2_{l}_{sub}")
        g_w1 = _matmul_tn(
            da.reshape(N_DEV, t_len, f8), keep[f"h{which}"],
            pl.BlockSpec((None, tk, f8), lambda p, k: (p, k, 0)),
            pl.BlockSpec((tk, d_model), lambda p, k: (k, 0)),
            SDS((N_DEV, f8, d_model), BF16), pl.BlockSpec((None, f8, d_model), lambda p, k: (p, 0, 0)),
            N_DEV, t_len, tk, f"grad_w1_{l}_{sub}", token)
        token = start_exchange(g_w1, f"w1_{l}_{sub}")
        return dx_in, st, token[0, 0]

    token = 0.0
    for l in reversed(range(n_layer)):
        keep = saved[l]
        wing, woutg = keep["wing"], keep["woutg"]
        dx, stats[l][2], token = ffn_backward(dx, keep, l, 1, 2, prm[l][2] + token)
        dya, dpc, dy, st_gate, conv_stats[l], *dya_res = _mixer_bwd_out(
            dx, keep["ymix"], prm[l][1] + token, woutg, keep["proj"], cprm[l], l, tm)
        dyas = [dya[None]] + dya_res
        g_wout = _matmul_tn(
            keep["cat"], dy,
            pl.BlockSpec((tk, d_model), lambda p, k: (k, 0)),
            pl.BlockSpec((tk, d_model), lambda p, k: (k, 0)),
            SDS((d_model, d_model), BF16), pl.BlockSpec((d_model, d_model), lambda p, k: (0, 0)),
            1, t_len, tk, f"grad_wout_{l}", dummy)
        token = start_exchange(g_wout.reshape(N_DEV, o8, d_model), f"wout_{l}")[0, 0]
        parts = [_attn_bwd(*keep["qkv"][i], dyas[i], keep["ya"][i], keep["lse"][i], l)
                 for i in range(len(DILATIONS))]
        dx, dproj, st_norm, qk_stats[l] = _mixer_bwd_in(
            parts, keep["proj"], dpc, dx, keep["x1"], prm[l][1] + token, wing, qkg[l], bd, l, tm)
        stats[l][1] = st_norm + st_gate
        g_win = _matmul_tn(
            keep["h1"], dproj,
            pl.BlockSpec((tk, d_model), lambda p, k: (k, 0)),
            pl.BlockSpec((tk, N_DEV // 2 * c8), lambda p, k: (k, p)),
            SDS((N_DEV, d_model, c8), BF16), pl.BlockSpec((N_DEV // 2, d_model, c8), lambda p, k: (p, 0, 0)),
            2, t_len, tk, f"grad_win_{l}", dummy)
        token = start_exchange(g_win, f"win_{l}")[0, 0]
        dx, stats[l][0], token = ffn_backward(dx, keep, l, 0, 0, prm[l][0] + token)
    grad_x = dx.reshape(x.shape)

    def update(pieces, w, m, v, name):
        shape = w.shape
        cols = shape[-1]
        outs = _adamw([p.reshape(p.shape[0], -1, cols) for p in pieces], w.reshape(-1, cols),
                      m.reshape(-1, cols), v.reshape(-1, cols), name)
        return [o.reshape(shape) for o in outs]

    last = "w1_0_0"
    lands = wait_exchange([k for k in started if k != last], started[last][4], "grads_wait_early")
    res = {
        "w_in": update([lands[f"win_{l}"] for l in range(n_layer)], w_in, m_w_in, v_w_in, "adamw_w_in"),
        "w_out": update([lands[f"wout_{l}"] for l in range(n_layer)], w_out, m_w_out, v_w_out, "adamw_w_out"),
        "ffn_w2": update([lands[f"w2_{l}_{s}"] for l in range(n_layer) for s in range(2)],
                         ffn_w2, m_ffn_w2, v_ffn_w2, "adamw_ffn_w2"),
    }

    dmod = jnp.stack([jnp.stack([stats[l][sub][r] for sub in range(3) for r in (0, 1, 3)]) for l in range(n_layer)])
    dng = jnp.stack([jnp.stack([stats[l][sub][2] for sub in range(3)]) for l in range(n_layer)])
    dcw = jnp.stack([conv_stats[l][0:3] for l in range(n_layer)])
    dcb = jnp.stack([conv_stats[l][3] for l in range(n_layer)])
    dqk = jnp.stack([qk_stats[l][0:2].reshape(2, n_head, HEAD_DIM).sum(axis=1) for l in range(n_layer)])
    part_shapes = [dmod.shape, dng.shape, dcw.shape, dcb.shape, dqk.shape]
    parts_all = _allgather_small(_pack([dmod, dng, dcw, dcb, dqk]), "gather_small_grads")
    dmod_all = _unpack(parts_all, part_shapes)[0].reshape(N_DEV, n_layer, 9 * d_model)
    summed = _unpack(_sum_devices(parts_all), part_shapes)
    g_b_ada = summed[0].reshape(n_layer, 9 * d_model)
    ng_cols = norm_g.shape[-1]
    g_norm_g = lax.dynamic_slice_in_dim(summed[1], me * ng_cols, ng_cols, axis=2)
    cw_cols = conv_w.shape[-1]
    g_conv_w = lax.dynamic_slice_in_dim(summed[2], me * cw_cols, cw_cols, axis=2)
    g_conv_b = summed[3]
    g_q, g_k = summed[4][:, 0], summed[4][:, 1]
    dmod_loc = lax.dynamic_slice_in_dim(dmod_all, me * ada_cols, ada_cols, axis=2)
    g_w_ada = _w_ada_grad(silu_c.T, jnp.moveaxis(dmod_loc, 0, 1))
    res["w_ada"] = update([g_w_ada[None]], w_ada, m_w_ada, v_w_ada, "adamw_w_ada")

    names = ["b_ada", "norm_g", "q_norm_g", "k_norm_g", "conv_w", "conv_b"]
    g_small = [g_b_ada, g_norm_g, g_q, g_k, g_conv_w, g_conv_b]
    w_small = [b_ada, norm_g, q_norm_g, k_norm_g, conv_w, conv_b]
    m_small = [m_b_ada, m_norm_g, m_q_norm_g, m_k_norm_g, m_conv_w, m_conv_b]
    v_small = [v_b_ada, v_norm_g, v_q_norm_g, v_k_norm_g, v_conv_w, v_conv_b]
    outs = _adamw([_pack(g_small)[None]], _pack(w_small), _pack(m_small), _pack(v_small), "adamw_small")
    shapes = [w.shape for w in w_small]
    unpacked = [_unpack(o, shapes) for o in outs]
    for i, nme in enumerate(names):
        res[nme] = [unpacked[k][i] for k in range(4)]

    w1_t = [jnp.swapaxes(t, -1, -2) for t in (ffn_w1, m_ffn_w1, v_ffn_w1)]
    w1_flat = [t.reshape(-1, d_model) for t in w1_t]
    w1_keys = [f"w1_{l}_{s}" for l in range(n_layer) for s in range(2)]
    assert w1_keys[0] == last
    early = _adamw([lands[k] for k in w1_keys[1:]], *w1_flat, "adamw_ffn_w1_early", first_q=1)
    follows = (res["w_in"][1][0, :1, :1] + res["w_out"][1][0, :1, :1] + res["ffn_w2"][1][0, 0, :1, :1]
               + res["w_ada"][1][0, :1, :1] + outs[1][:1, :1] + early[1][-1:, :1])
    lands.update(wait_exchange([last], follows, "grads_wait_last"))
    done = _adamw([lands[last]], *w1_flat, "adamw_ffn_w1_last", first_q=0, earlier=early)
    res["ffn_w1"] = [jnp.swapaxes(t.reshape(w1_t[0].shape), -1, -2) for t in done]

    order = ["w_ada", "b_ada", "norm_g", "w_in", "q_norm_g", "k_norm_g", "conv_w", "conv_b", "w_out", "ffn_w1", "ffn_w2"]
    return (loss, grad_x, *[res[n][0] for n in order], *[res[n][1] for n in order],
            *[res[n][2] for n in order], *[res[n][3] for n in order])
```

```python
import jax
import jax.numpy as jnp
from jax import lax
from jax.experimental import pallas as pl
from jax.experimental.pallas import tpu as pltpu

F32 = jnp.float32
BF16 = jnp.bfloat16
SDS = jax.ShapeDtypeStruct

N_DEV = 8
HEAD_DIM = 64
BLOCK = 128
DILATIONS = (1, 4, 16)
ATTN_BLOCKS_PER_STEP = 4
N_CHUNK = 4
EPS = 1e-6
NEG = -1e30
SM_SCALE = HEAD_DIM ** -0.5
LANES = 128
MXU_WIDTH = 256
TOKEN_TILE = 256
FFN_ROW_PART = 256
FFN_FWD_TOKEN_TILE = 256
FFN_BWD_TOKEN_TILE = 256
GRAD_TOKEN_TILE = 2048
VMEM_LIMIT_BYTES = 56 * 1024 * 1024

ADAM_LR = 0.001
ADAM_B1 = 0.9
ADAM_B2 = 0.999
ADAM_EPS = 1e-08
ADAM_WD = 0.01
ADAM_STEP = 10

MESH_ID = pl.DeviceIdType.MESH
ANY = pl.BlockSpec(memory_space=pl.ANY)
HBM_SPEC = pl.BlockSpec(memory_space=pltpu.HBM)
SEM_SPEC = pl.BlockSpec(memory_space=pltpu.SEMAPHORE)
SIDE_EFFECT = pltpu.SideEffectType.DATAFLOW_SIDE_EFFECTING


def _params(n_axes):
    return pltpu.CompilerParams(dimension_semantics=("arbitrary",) * n_axes,
                                vmem_limit_bytes=VMEM_LIMIT_BYTES)


def _dot(a, b):
    return jnp.dot(a, b, preferred_element_type=F32)


def _dot_nt(a, b):
    return lax.dot_general(a, b, (((1,), (1,)), ((), ())), preferred_element_type=F32)


def _dot_tn(a, b):
    return lax.dot_general(a, b, (((0,), (0,)), ((), ())), preferred_element_type=F32)


def _row_halves(n):
    return (slice(0, n // 2), slice(n // 2, n))


def _split_bf16(t):
    hi = t.astype(BF16)
    return hi, (t - hi.astype(F32)).astype(BF16)


def _my_place():
    x, y, c = lax.axis_index("x"), lax.axis_index("y"), lax.axis_index("c")
    return x, y, c


def _peer(place, k):
    x, y, c = place
    return ((1 - x) if k & 4 else x, (1 - y) if k & 2 else y, (1 - c) if k & 1 else c)


def _index(place):
    return 4 * place[0] + 2 * place[1] + place[2]


def _allgather_small(v, name):
    rows, cols = v.shape

    def body(x_ref, out_ref, send_sems, recv_sems, local_sem):
        me = _my_place()
        mine = pltpu.make_async_copy(x_ref, out_ref.at[_index(me)], local_sem)
        mine.start()
        sends = []
        for k in range(1, N_DEV):
            cp = pltpu.make_async_remote_copy(
                src_ref=x_ref, dst_ref=out_ref.at[_index(me)], send_sem=send_sems.at[k - 1],
                recv_sem=recv_sems.at[k - 1], device_id=_peer(me, k), device_id_type=MESH_ID)
            cp.start()
            sends.append(cp)
        for k in range(1, N_DEV):
            pltpu.make_async_remote_copy(
                src_ref=x_ref, dst_ref=out_ref.at[_index(_peer(me, k))], send_sem=send_sems.at[k - 1],
                recv_sem=recv_sems.at[k - 1], device_id=_peer(me, k), device_id_type=MESH_ID).wait_recv()
        for cp in sends:
            cp.wait_send()
        mine.wait()

    return pl.pallas_call(
        body, name=name,
        out_shape=SDS((N_DEV, rows, cols), v.dtype),
        in_specs=[pl.BlockSpec(memory_space=pltpu.VMEM)],
        out_specs=pl.BlockSpec(memory_space=pltpu.VMEM),
        scratch_shapes=[pltpu.SemaphoreType.DMA((N_DEV - 1,)), pltpu.SemaphoreType.DMA((N_DEV - 1,)),
                        pltpu.SemaphoreType.DMA],
    )(v)


def _allgather_big(lands, name):
    n = len(lands)

    def body(*refs):
        outs = refs[n:2 * n]
        send_sems, recv_sems = refs[2 * n:]
        x, y, c = _my_place()
        me, sib = (x, y, c), (x, y, 1 - c)
        chips = [(1 - x, y), (x, 1 - y), (1 - x, 1 - y)]

        def copy(i, k, block, to):
            dst = outs[i].at[_index(block)]
            return pltpu.make_async_remote_copy(
                src_ref=dst, dst_ref=dst, send_sem=send_sems.at[7 * i + k],
                recv_sem=recv_sems.at[7 * i + k], device_id=to, device_id_type=MESH_ID)

        first = []
        for i in range(n):
            first.append(copy(i, 0, me, sib))
            for j, chip in enumerate(chips):
                first.append(copy(i, 1 + j, me, (*chip, c)))
        for cp in first:
            cp.start()
        passed = []
        for i in range(n):
            for j, chip in enumerate(chips):
                copy(i, 1 + j, (*chip, c), me).wait_recv()
                fwd = copy(i, 4 + j, (*chip, c), sib)
                fwd.start()
                passed.append(fwd)
        for i in range(n):
            copy(i, 0, sib, me).wait_recv()
            for j, chip in enumerate(chips):
                copy(i, 4 + j, (*chip, 1 - c), me).wait_recv()
        for cp in first + passed:
            cp.wait_send()

    return pl.pallas_call(
        body, name=name,
        out_shape=[SDS(a.shape, a.dtype) for a in lands],
        in_specs=[ANY] * n, out_specs=[ANY] * n,
        input_output_aliases={i: i for i in range(n)},
        scratch_shapes=[pltpu.SemaphoreType.DMA((7 * n,)), pltpu.SemaphoreType.DMA((7 * n,))],
    )(*lands)


def _place_own(src, lead, me_arr, name):
    rows, cols = src.shape[-2:]
    tr = _row_tile(rows, cols)

    def body(me_ref, s_ref, o_ref):
        o_ref[...] = s_ref[...].astype(BF16)

    if lead is None:
        in_spec = pl.BlockSpec((None, tr, cols), lambda i, me_ref: (me_ref[0], i, 0))
    else:
        in_spec = pl.BlockSpec((None,) * len(lead) + (tr, cols), lambda i, me_ref: (*lead, i, 0))
    return pl.pallas_call(
        body, name=name,
        grid_spec=pltpu.PrefetchScalarGridSpec(
            num_scalar_prefetch=1, grid=(rows // tr,), in_specs=[in_spec],
            out_specs=pl.BlockSpec((None, tr, cols), lambda i, me_ref: (me_ref[0], i, 0))),
        out_shape=SDS((N_DEV, rows, cols), BF16),
        compiler_params=_params(1),
    )(me_arr, src)


def _send_start(lands, pieces, after, name):
    n = len(lands)
    arrs = list(lands) + list(pieces or [])
    n_a = len(arrs)

    def body(*refs):
        land_r, piece_r = refs[:n], refs[n:n_a]
        send, recv = refs[n_a + 1:n_a + 1 + n], refs[n_a + 1 + n:n_a + 1 + 2 * n]
        token = refs[-1]
        me = _my_place()
        for i in range(n):
            for k in range(1, N_DEV):
                peer = _peer(me, k)
                src = piece_r[i].at[_index(peer)] if pieces else land_r[i].at[_index(me)]
                pltpu.make_async_remote_copy(
                    src_ref=src, dst_ref=land_r[i].at[_index(me)], send_sem=send[i], recv_sem=recv[i],
                    device_id=peer, device_id_type=MESH_ID).start()
        token[...] = jnp.zeros_like(token)

    outs = pl.pallas_call(
        body, name=name,
        out_shape=[pltpu.SemaphoreType.DMA(())] * (2 * n) + [pltpu.HBM(a.shape, a.dtype) for a in arrs]
        + [SDS((8, LANES), F32)],
        in_specs=[HBM_SPEC] * n_a + [ANY],
        out_specs=[SEM_SPEC] * (2 * n) + [HBM_SPEC] * n_a + [pl.BlockSpec(memory_space=pltpu.VMEM)],
        input_output_aliases={i: 2 * n + i for i in range(n_a)},
        compiler_params=pltpu.CompilerParams(has_side_effects=SIDE_EFFECT),
    )(*[pltpu.with_memory_space_constraint(a, pltpu.HBM) for a in arrs], after)
    return outs[:n], outs[n:2 * n], outs[2 * n:2 * n + n_a], outs[-1]


def _send_wait(thru, sends, recvs, after, name):
    n = len(sends)
    n_a = len(thru)

    def body(*refs):
        land_r = refs[:n]
        send, recv = refs[n_a:n_a + n], refs[n_a + n:n_a + 2 * n]
        me = _my_place()
        for i in range(n):
            seven = land_r[i].at[pl.ds(0, N_DEV - 1)]
            cp = pltpu.make_async_remote_copy(src_ref=seven, dst_ref=seven, send_sem=send[i], recv_sem=recv[i],
                                              device_id=_peer(me, 1), device_id_type=MESH_ID)
            cp.wait_send()
            cp.wait_recv()

    outs = pl.pallas_call(
        body, name=name,
        out_shape=[pltpu.HBM(a.shape, a.dtype) for a in thru],
        in_specs=[HBM_SPEC] * n_a + [SEM_SPEC] * (2 * n) + [ANY],
        out_specs=[HBM_SPEC] * n_a,
        input_output_aliases={i: i for i in range(n_a)},
        compiler_params=pltpu.CompilerParams(has_side_effects=SIDE_EFFECT),
    )(*thru, *sends, *recvs, after)
    return outs[:n]


def _ada_fwd(x, prm_ref):
    rstd = lax.rsqrt(jnp.mean(x * x, axis=-1, keepdims=True) + EPS)
    return (x * rstd * prm_ref[0:1, :]) * (1.0 + prm_ref[2:3, :]) + prm_ref[1:2, :]


def _ada_bwd(dh, x, prm_ref):
    ng, ops = prm_ref[0:1, :], 1.0 + prm_ref[2:3, :]
    rstd = lax.rsqrt(jnp.mean(x * x, axis=-1, keepdims=True) + EPS)
    xhat = x * rstd
    dxhat = dh * (ops * ng)
    dx = rstd * (dxhat - xhat * jnp.mean(dxhat * xhat, axis=-1, keepdims=True))
    dshift = jnp.sum(dh, axis=0, keepdims=True)
    dscale = jnp.sum(dh * (xhat * ng), axis=0, keepdims=True)
    dng = jnp.sum(dh * (xhat * ops), axis=0, keepdims=True)
    return dx, dshift, dscale, dng


def _head_mean(t, bd):
    hi = t.astype(BF16)
    g = bd.shape[0]
    parts = [_dot(hi[:, i:i + g], bd) for i in range(0, t.shape[1], g)]
    return jnp.concatenate(parts, axis=1) * (1.0 / HEAD_DIM)


def _join_columns(w_hbm, dst_ref, sems):
    cols = w_hbm.shape[2]
    copies = [pltpu.make_async_copy(w_hbm.at[j], dst_ref.at[:, pl.ds(j * cols, cols)], sems.at[j])
              for j in range(w_hbm.shape[0])]
    for cp in copies:
        cp.start()
    for cp in copies:
        cp.wait()


def _shift_rows(z, before, tm):
    row = lax.broadcasted_iota(jnp.int32, z.shape, 0)
    z1 = jnp.where(row == 0, before[7:8, :], pltpu.roll(z, 1, 0))
    z2 = jnp.where(row == 0, before[6:7, :], jnp.where(row == 1, before[7:8, :], pltpu.roll(z, 2, 0)))
    return z1, z2


def _mod_part(c_all, w_ada, b_loc):
    n_layer, d_model, cols = w_ada.shape

    def body(c_ref, w_ref, b_ref, out_ref, sc_ref):
        cv = c_ref[...]
        sc = cv * jax.nn.sigmoid(cv)
        sc_ref[...] = sc
        a_hi, a_lo = _split_bf16(sc)
        w_hi, w_lo = _split_bf16(w_ref[...])
        out_ref[...] = _dot(a_hi, w_hi) + _dot(a_hi, w_lo) + _dot(a_lo, w_hi) + b_ref[...]

    return pl.pallas_call(
        body, name="mod_part", grid=(n_layer,),
        in_specs=[pl.BlockSpec((N_DEV, d_model), lambda l: (0, 0)),
                  pl.BlockSpec((None, d_model, cols), lambda l: (l, 0, 0)),
                  pl.BlockSpec((None, 1, cols), lambda l: (l, 0, 0))],
        out_specs=[pl.BlockSpec((None, N_DEV, cols), lambda l: (l, 0, 0)),
                   pl.BlockSpec((N_DEV, d_model), lambda l: (0, 0))],
        out_shape=[SDS((n_layer, N_DEV, cols), F32), SDS((N_DEV, d_model), F32)],
        compiler_params=_params(1),
    )(c_all, w_ada, b_loc)


def _w_ada_grad(sc_t, dmod_loc):
    d_model = sc_t.shape[0]
    n_layer, _, cols = dmod_loc.shape

    def body(s_ref, d_ref, out_ref):
        acc = s_ref[:, 0:1] * d_ref[0:1, :]
        for b in range(1, N_DEV):
            acc = acc + s_ref[:, b:b + 1] * d_ref[b:b + 1, :]
        out_ref[...] = acc

    return pl.pallas_call(
        body, name="w_ada_grad", grid=(n_layer,),
        in_specs=[pl.BlockSpec((d_model, N_DEV), lambda l: (0, 0)),
                  pl.BlockSpec((None, N_DEV, cols), lambda l: (l, 0, 0))],
        out_specs=pl.BlockSpec((None, d_model, cols), lambda l: (l, 0, 0)),
        out_shape=SDS((n_layer, d_model, cols), F32),
        compiler_params=_params(1),
    )(sc_t, dmod_loc)


def _sum_devices(g):
    _, rows, cols = g.shape

    def body(g_ref, out_ref):
        acc = g_ref[0]
        for i in range(1, N_DEV):
            acc = acc + g_ref[i]
        out_ref[...] = acc

    return pl.pallas_call(body, name="sum_devices", out_shape=SDS((rows, cols), F32))(g)


def _ffn_fwd(x, prm, w1r, w2g, l, sub, tm):
    t_len, d_model = x.shape
    f8, r8 = w1r.shape[-1], w2g.shape[-2]
    n_m = t_len // tm

    def body(x_ref, xn_ref, prm_ref, w1_ref, w2_ref, act_ref, h_ref, y_ref, xo_ref, h_sc):
        @pl.when(pl.program_id(0) == 0)
        def _():
            h_sc[...] = _ada_fwd(x_ref[...], prm_ref).astype(BF16)

        h = h_sc[...]
        h_ref[...] = h

        def first(j):
            return _dot(h, w1_ref[0, j]), _dot(h, w1_ref[1, j])

        gu = first(0)
        yv = None
        for j in range(N_CHUNK):
            g, u = gu
            if j + 1 < N_CHUNK:
                gu = first(j + 1)
            else:
                h_next = _ada_fwd(xn_ref[...], prm_ref).astype(BF16)
            sig = jax.nn.sigmoid(g)
            sil = g * sig
            s = (sil * u).astype(BF16)
            act_ref[0, j] = (u * (sig * (1.0 + g * (1.0 - sig)))).astype(BF16)
            act_ref[1, j] = sil.astype(BF16)
            act_ref[2, j] = s
            part = _dot(s, w2_ref[2 * j:2 * j + 2].reshape(2 * r8, d_model))
            yv = part if yv is None else yv + part
        y_ref[...] = yv.astype(BF16)
        xo_ref[...] = x_ref[...] + (0.5 * prm_ref[3:4, :]) * yv
        h_sc[...] = h_next

    tile = pl.BlockSpec((tm, d_model), lambda m: (m, 0))
    return pl.pallas_call(
        body, name=f"ffn_fwd_{l}_{sub}", grid=(n_m,),
        in_specs=[tile, pl.BlockSpec((tm, d_model), lambda m: (jnp.minimum(m + 1, n_m - 1), 0)),
                  pl.BlockSpec((8, d_model), lambda m: (0, 0)),
                  pl.BlockSpec((2, N_CHUNK, d_model, f8), lambda m: (0, 0, 0, 0), pipeline_mode=pl.Buffered(1)),
                  pl.BlockSpec((N_DEV, r8, d_model), lambda m: (0, 0, 0), pipeline_mode=pl.Buffered(1))],
        out_specs=[pl.BlockSpec((3, N_CHUNK, tm, f8), lambda m: (0, 0, m, 0)), tile, tile, tile],
        out_shape=[SDS((3, N_CHUNK, t_len, f8), BF16), SDS((t_len, d_model), BF16),
                   SDS((t_len, d_model), BF16), SDS((t_len, d_model), F32)],
        scratch_shapes=[pltpu.VMEM((tm, d_model), BF16)],
        compiler_params=_params(1),
    )(x, x, prm, w1r, w2g)


def _ffn_bwd(dxo, y, x_in, prm, act, w1r, w2g, l, sub, tm):
    t_len, d_model = x_in.shape
    f8, r8 = w1r.shape[-1], w2g.shape[-2]
    n_m = t_len // tm

    def body(dxo_ref, y_ref, x_ref, prm_ref, act_ref, w1_ref, w2_ref, dx_ref, da_ref, dy_ref, st_ref):
        @pl.when(pl.program_id(0) == 0)
        def _():
            st_ref[...] = jnp.zeros_like(st_ref)

        dxo_v = dxo_ref[...]
        dy = ((0.5 * prm_ref[3:4, :]) * dxo_v).astype(BF16)
        dy_ref[...] = dy
        st_ref[3:4, :] += jnp.sum(0.5 * dxo_v * y_ref[...].astype(F32), axis=0, keepdims=True)

        def first(j):
            return _dot_nt(dy, w2_ref[2 * j:2 * j + 2].reshape(2 * r8, d_model))

        ds = first(0)
        dh = None
        for j in range(N_CHUNK):
            ds_now = ds
            if j + 1 < N_CHUNK:
                ds = first(j + 1)
            dg = (ds_now * act_ref[0, j].astype(F32)).astype(BF16)
            du = (ds_now * act_ref[1, j].astype(F32)).astype(BF16)
            da_ref[0, j] = dg
            da_ref[1, j] = du
            part = _dot_nt(dg, w1_ref[0, j]) + _dot_nt(du, w1_ref[1, j])
            dh = part if dh is None else dh + part
        dx, dshift, dscale, dng = _ada_bwd(dh, x_ref[...], prm_ref)
        dx_ref[...] = dxo_v + dx
        st_ref[0:1, :] += dshift
        st_ref[1:2, :] += dscale
        st_ref[2:3, :] += dng

    tile = pl.BlockSpec((tm, d_model), lambda m: (m, 0))
    fixed = pl.BlockSpec((8, d_model), lambda m: (0, 0))
    return pl.pallas_call(
        body, name=f"ffn_bwd_{l}_{sub}", grid=(n_m,),
        in_specs=[tile, tile, tile, fixed,
                  pl.BlockSpec((2, N_CHUNK, tm, f8), lambda m: (0, 0, m, 0)),
                  pl.BlockSpec((2, N_CHUNK, d_model, f8), lambda m: (0, 0, 0, 0), pipeline_mode=pl.Buffered(1)),
                  pl.BlockSpec((N_DEV, r8, d_model), lambda m: (0, 0, 0), pipeline_mode=pl.Buffered(1))],
        out_specs=[tile, pl.BlockSpec((2, N_CHUNK, tm, f8), lambda m: (0, 0, m, 0)), tile, fixed],
        out_shape=[SDS((t_len, d_model), F32), SDS((2, N_CHUNK, t_len, f8), BF16),
                   SDS((t_len, d_model), BF16), SDS((8, d_model), F32)],
        compiler_params=_params(1),
    )(dxo, y, x_in, prm, act, w1r, w2g)


def _matmul_tn(a, b, a_spec, b_spec, out_shape, out_spec, n_piece, t_len, tk, name, after):
    def body(a_ref, b_ref, after_ref, o_ref, acc):
        k = pl.program_id(1)

        @pl.when(k == 0)
        def _():
            acc[...] = jnp.zeros_like(acc)

        m_out, n_out = acc.shape
        if (m_out // 2) % LANES == 0:
            for rows in _row_halves(m_out):
                acc[rows, :] += _dot_tn(a_ref[:, rows], b_ref[...])
        else:
            for cols in _row_halves(n_out):
                acc[:, cols] += _dot_tn(a_ref[...], b_ref[:, cols])

        @pl.when(k == pl.num_programs(1) - 1)
        def _():
            if len(o_ref.shape) == 3:
                cols = o_ref.shape[2]
                for q in range(o_ref.shape[0]):
                    o_ref[q] = acc[:, q * cols:(q + 1) * cols].astype(o_ref.dtype)
            else:
                o_ref[...] = acc[...].astype(o_ref.dtype)

    blk = tuple(d for d in out_spec.block_shape if d is not None)
    acc_shape = blk if len(blk) == 2 else (blk[1], blk[0] * blk[2])
    return pl.pallas_call(
        body, name=name, grid=(n_piece, t_len // tk),
        in_specs=[a_spec, b_spec, ANY], out_specs=out_spec, out_shape=out_shape,
        scratch_shapes=[pltpu.VMEM(acc_shape, F32)],
        compiler_params=_params(2),
    )(a, b, after)


def _groups(tm, width):
    return pltpu.VMEM((width // LANES, tm, LANES), F32)


def _to_groups(val, dst_ref, first=0):
    for g in range(val.shape[1] // LANES):
        dst_ref[first + g] = val[:, g * LANES:(g + 1) * LANES]


def _from_groups(src_ref):
    return jnp.concatenate([src_ref[g] for g in range(src_ref.shape[0])], axis=1)


def _split_residues(src_ref, out_ref, dil, tm):
    for g in range(src_ref.shape[0]):
        for r in range(dil):
            out_ref[r, :, g * LANES:(g + 1) * LANES] = (
                src_ref.at[g][pl.ds(r, tm // dil, stride=dil), :].astype(out_ref.dtype))


def _merge_residues(in_ref, dst_ref, dil, tm, add=False):
    for g in range(dst_ref.shape[0]):
        for r in range(dil):
            rows = pl.ds(r, tm // dil, stride=dil)
            val = in_ref[r, :, g * LANES:(g + 1) * LANES].astype(F32)
            dst = dst_ref.at[g]
            dst[rows, :] = (dst[rows, :] + val) if add else val


def _res_spec(dil, tm, width, index=lambda m: m):
    return pl.BlockSpec((dil, tm // dil, width), lambda m: (0, index(m), 0))


def _mixer_in(x, prm, wing, qkg, cprm, bd, l, tm):
    t_len, d_model = x.shape
    a_w = d_model // 2
    c_w = d_model - a_w
    c8 = wing.shape[-1]
    p_c = N_DEV * c8
    dils = DILATIONS[1:]

    def body(x_ref, prm_ref, w_ref, qkg_ref, cprm_ref, bd_ref, proj_ref, qkn_ref, h_ref, ycv_ref, *rest):
        res_refs, (pf, qsc, vsc, carry, wcat, wsems) = rest[:2 * len(dils)], rest[2 * len(dils):]

        @pl.when(pl.program_id(0) == 0)
        def _():
            carry[...] = jnp.zeros_like(carry)
            _join_columns(w_ref, wcat, wsems)

        h = _ada_fwd(x_ref[...], prm_ref).astype(BF16)
        h_ref[...] = h
        for cols in _row_halves(p_c):
            pf[:, cols] = _dot(h, wcat[:, cols])
        proj_ref[...] = pf[...].astype(BF16)
        bdm = bd_ref[...]
        for i in range(2):
            t = pf[:, i * a_w:(i + 1) * a_w]
            rstd = lax.rsqrt(_head_mean(t * t, bdm) + EPS)
            tn = t * rstd * qkg_ref[i:i + 1, :]
            qkn_ref[:, i * a_w:(i + 1) * a_w] = tn.astype(BF16)
            _to_groups(tn, qsc, i * (a_w // LANES))
        _to_groups(pf[:, 2 * a_w:3 * a_w], vsc)
        for i, dil in enumerate(dils):
            _split_residues(qsc, res_refs[2 * i], dil, tm)
            _split_residues(vsc, res_refs[2 * i + 1], dil, tm)
        gb = pf[:, 3 * a_w:3 * a_w + c_w]
        z = pf[:, 3 * a_w + c_w:3 * a_w + 2 * c_w] * pf[:, 3 * a_w + 2 * c_w:3 * a_w + 3 * c_w]
        z1, z2 = _shift_rows(z, carry[...], tm)
        cv = cprm_ref[0:1, :] * z2 + cprm_ref[1:2, :] * z1 + cprm_ref[2:3, :] * z + cprm_ref[3:4, :]
        ycv_ref[...] = (gb * cv).astype(BF16)
        carry[...] = z[tm - 8:tm, :]

    res_specs, res_shapes = [], []
    for dil in dils:
        res_specs += [_res_spec(dil, tm, 2 * a_w), _res_spec(dil, tm, a_w)]
        res_shapes += [SDS((dil, t_len // dil, 2 * a_w), BF16), SDS((dil, t_len // dil, a_w), BF16)]
    return pl.pallas_call(
        body, name=f"mixer_in_{l}", grid=(t_len // tm,),
        in_specs=[pl.BlockSpec((tm, d_model), lambda m: (m, 0)),
                  pl.BlockSpec((8, d_model), lambda m: (0, 0)),
                  ANY,
                  pl.BlockSpec((8, a_w), lambda m: (0, 0)),
                  pl.BlockSpec((8, c_w), lambda m: (0, 0)),
                  pl.BlockSpec(bd.shape, lambda m: (0, 0))],
        out_specs=[pl.BlockSpec((tm, p_c), lambda m: (m, 0)),
                   pl.BlockSpec((tm, 2 * a_w), lambda m: (m, 0)),
                   pl.BlockSpec((tm, d_model), lambda m: (m, 0)),
                   pl.BlockSpec((tm, c_w), lambda m: (m, 0))] + res_specs,
        out_shape=[SDS((t_len, p_c), BF16), SDS((t_len, 2 * a_w), BF16),
                   SDS((t_len, d_model), BF16), SDS((t_len, c_w), BF16)] + res_shapes,
        scratch_shapes=[pltpu.VMEM((tm, p_c), F32), _groups(tm, 2 * a_w), _groups(tm, a_w),
                        pltpu.VMEM((8, c_w), F32), pltpu.VMEM((d_model, p_c), BF16),
                        pltpu.SemaphoreType.DMA((N_DEV,))],
        compiler_params=_params(1),
    )(x, prm, wing, qkg, cprm, bd)


def _band_mask(first_block):
    qi = lax.broadcasted_iota(jnp.int32, (BLOCK, 2 * BLOCK), 0)
    kj = lax.broadcasted_iota(jnp.int32, (BLOCK, 2 * BLOCK), 1)
    lowest = jnp.where(first_block, BLOCK, 0)
    return (kj >= qi) & (kj <= qi + BLOCK) & (kj >= lowest)


def _blocks_per_step(seq):
    n_sb = min(ATTN_BLOCKS_PER_STEP, seq // BLOCK)
    assert seq % (n_sb * BLOCK) == 0
    return n_sb


def _block_rows(sb):
    return slice(sb * BLOCK, (sb + 1) * BLOCK)


def _with_previous(prev_ref, cur_ref, sb, sl):
    before = prev_ref[:, sl] if sb == 0 else cur_ref[_block_rows(sb - 1), sl]
    return jnp.concatenate([before, cur_ref[_block_rows(sb), sl]], axis=0)


def _heads_on_lanes(cols):
    width = LANES // len(cols)
    lane = lax.broadcasted_iota(jnp.int32, (BLOCK, LANES), 1)
    out = jnp.broadcast_to(cols[-1], (BLOCK, LANES))
    for h in range(len(cols) - 2, -1, -1):
        out = jnp.where(lane < (h + 1) * width, cols[h], out)
    return out


def _attn_fwd(qk, v, v_col, l):
    dil, seq, a2 = qk.shape
    a_w = a2 // 2
    n_sb = _blocks_per_step(seq)
    rows_step = n_sb * BLOCK
    n_pair = a_w // LANES

    def body(q_ref, kc_ref, vc_ref, o_ref, lse_ref, kp_ref, vp_ref):
        @pl.when(pl.program_id(1) == 0)
        def _():
            kp_ref[...] = jnp.zeros_like(kp_ref)
            vp_ref[...] = jnp.zeros_like(vp_ref)

        masks = [_band_mask(pl.program_id(1) == 0)] + [_band_mask(False)] * (n_sb - 1)
        low = lax.broadcasted_iota(jnp.int32, (BLOCK, LANES), 1) < HEAD_DIM
        units = [(sb, slice(p * LANES, (p + 1) * LANES)) for sb in range(n_sb) for p in range(a_w // LANES)]
        scores = []
        for sb, sl in units:
            qp = q_ref[_block_rows(sb), sl]
            k2 = _with_previous(kp_ref, kc_ref, sb, sl)
            scores.append([_dot_nt(jnp.where(half, qp, jnp.zeros_like(qp)), k2) for half in (low, ~low)])
        probs, lses = [], []
        for (sb, sl), pair in zip(units, scores):
            pr, ls = [], []
            for s in pair:
                s = jnp.where(masks[sb], s * SM_SCALE, NEG)
                mx = jnp.max(s, axis=-1, keepdims=True)
                e = jnp.exp(s - mx)
                den = jnp.sum(e, axis=-1, keepdims=True)
                pr.append((e * (1.0 / den)).astype(BF16))
                ls.append(mx + jnp.log(den))
            probs.append(pr)
            lses.append(ls)
        for (sb, sl), pr, ls in zip(units, probs, lses):
            v2 = _with_previous(vp_ref, vc_ref, sb, sl)
            o_ref[_block_rows(sb), sl] = jnp.where(low, _dot(pr[0], v2), _dot(pr[1], v2)).astype(BF16)
        for sb in range(n_sb):
            cols = [c for ls in lses[sb * n_pair:(sb + 1) * n_pair] for c in ls]
            lse_ref[_block_rows(sb), :] = _heads_on_lanes(cols)
        kp_ref[...] = kc_ref[_block_rows(n_sb - 1), :]
        vp_ref[...] = vc_ref[_block_rows(n_sb - 1), :]

    def blk(f):
        return pl.BlockSpec((None, rows_step, a_w), f)

    return pl.pallas_call(
        body, name=f"attn_fwd_{l}_d{dil}", grid=(dil, seq // rows_step),
        in_specs=[blk(lambda r, b: (r, b, 0)),
                  blk(lambda r, b: (r, b, 1)),
                  blk(lambda r, b: (r, b, v_col))],
        out_specs=[blk(lambda r, b: (r, b, 0)), pl.BlockSpec((None, rows_step, LANES), lambda r, b: (r, b, 0))],
        out_shape=[SDS((dil, seq, a_w), BF16), SDS((dil, seq, LANES), F32)],
        scratch_shapes=[pltpu.VMEM((BLOCK, a_w), BF16), pltpu.VMEM((BLOCK, a_w), BF16)],
        compiler_params=_params(2),
    )(qk, qk, v)


def _attn_combine(outs, lses, ycv, expand, tm, l):
    t_len, c_w = ycv.shape
    a_w = outs[0].shape[-1]
    dils = DILATIONS[1:]
    n_r = len(dils)

    def body(o1, l1, *rest):
        o_res, l_res, y_ref, ex_ref = rest[:n_r], rest[n_r:2 * n_r], rest[2 * n_r], rest[2 * n_r + 1]
        cat_ref, lse_ref = rest[2 * n_r + 2], rest[2 * n_r + 3]
        ya_res, lse_res = rest[2 * n_r + 4:3 * n_r + 4], rest[3 * n_r + 4:4 * n_r + 4]
        scr = rest[4 * n_r + 4:]
        so, sl, sya, slse = scr[:n_r], scr[n_r:2 * n_r], scr[2 * n_r], scr[2 * n_r + 1]
        for i, dil in enumerate(dils):
            _merge_residues(o_res[i], so[i], dil, tm)
            _merge_residues(l_res[i], sl[i], dil, tm)
        ls = [l1[...]] + [s[0] for s in sl]
        mx = jnp.maximum(jnp.maximum(ls[0], ls[1]), ls[2])
        es = [jnp.exp(t - mx) for t in ls]
        den = es[0] + es[1] + es[2]
        inv = 1.0 / den
        slse[0] = mx + jnp.log(den)
        lse_ref[...] = slse[0]
        ex = ex_ref[...]
        ov = [o1[...].astype(F32)] + [_from_groups(s) for s in so]
        ya = jnp.zeros((tm, a_w), F32)
        for e, o in zip(es, ov):
            hi, lo = _split_bf16(e * inv)
            ya = ya + (_dot(hi, ex) + _dot(lo, ex)) * o
        _to_groups(ya, sya)
        cat_ref[:, :a_w] = ya.astype(BF16)
        cat_ref[:, a_w:] = y_ref[...]
        for i, dil in enumerate(dils):
            _split_residues(sya, ya_res[i], dil, tm)
            _split_residues(slse, lse_res[i], dil, tm)

    ta = pl.BlockSpec((tm, a_w), lambda m: (m, 0))
    tl = pl.BlockSpec((tm, LANES), lambda m: (m, 0))
    res_a = [_res_spec(dil, tm, a_w) for dil in dils]
    res_l = [_res_spec(dil, tm, LANES) for dil in dils]
    return pl.pallas_call(
        body, name=f"attn_combine_{l}", grid=(t_len // tm,),
        in_specs=[ta, tl] + res_a + res_l + [pl.BlockSpec((tm, c_w), lambda m: (m, 0)),
                                            pl.BlockSpec((LANES, a_w), lambda m: (0, 0))],
        out_specs=[pl.BlockSpec((tm, a_w + c_w), lambda m: (m, 0)), tl] + res_a + res_l,
        out_shape=[SDS((t_len, a_w + c_w), BF16), SDS((t_len, LANES), F32)]
        + [SDS((dil, t_len // dil, a_w), BF16) for dil in dils]
        + [SDS((dil, t_len // dil, LANES), F32) for dil in dils],
        scratch_shapes=[_groups(tm, a_w)] * n_r + [_groups(tm, LANES)] * n_r + [_groups(tm, a_w), _groups(tm, LANES)],
        compiler_params=_params(1),
    )(outs[0].reshape(t_len, a_w), lses[0].reshape(t_len, LANES), *outs[1:], *lses[1:], ycv, expand)


def _mixer_out(cat, x, prm, woutg, l, tm):
    t_len, d_model = x.shape
    r8 = woutg.shape[-2]

    def body(cat_ref, x_ref, prm_ref, w_ref, xo_ref, y_ref):
        w = w_ref[...].reshape(N_DEV * r8, d_model)
        for rows in _row_halves(tm):
            yv = _dot(cat_ref[rows, :], w)
            y_ref[rows, :] = yv.astype(BF16)
            xo_ref[rows, :] = x_ref[rows, :] + prm_ref[3:4, :] * yv

    tile = pl.BlockSpec((tm, d_model), lambda m: (m, 0))
    return pl.pallas_call(
        body, name=f"mixer_out_{l}", grid=(t_len // tm,),
        in_specs=[tile, tile, pl.BlockSpec((8, d_model), lambda m: (0, 0)),
                  pl.BlockSpec((N_DEV, r8, d_model), lambda m: (0, 0, 0))],
        out_specs=[tile, tile],
        out_shape=[SDS((t_len, d_model), F32), SDS((t_len, d_model), BF16)],
        compiler_params=_params(1),
    )(cat, x, prm, woutg)


def _mixer_bwd_out(dxo, ymix, prm, woutg, proj, cprm, l, tm):
    t_len, d_model = dxo.shape
    a_w = d_model // 2
    c_w = d_model - a_w
    r8 = woutg.shape[-2]
    n_m = t_len // tm
    hb = tm // 8

    dils = DILATIONS[1:]

    def body(dxo_ref, ym_ref, prm_ref, w_ref, pc_ref, halo_ref, cprm_ref,
             dya_ref, dpc_ref, dy_ref, st_ref, cs_ref, *rest):
        dya_res, (carry, dsc) = rest[:len(dils)], rest[len(dils):]
        step = pl.program_id(0)
        tile_i = n_m - 1 - step

        @pl.when(step == 0)
        def _():
            st_ref[...] = jnp.zeros_like(st_ref)
            cs_ref[...] = jnp.zeros_like(cs_ref)
            carry[...] = jnp.zeros_like(carry)

        dxo_v = dxo_ref[...]
        dy = (prm_ref[3:4, :] * dxo_v).astype(BF16)
        dy_ref[...] = dy
        st_ref[3:4, :] += jnp.sum(dxo_v * ym_ref[...].astype(F32), axis=0, keepdims=True)
        w = w_ref[...].reshape(N_DEV * r8, d_model)
        dcat = jnp.concatenate([_dot_nt(dy[rows, :], w) for rows in _row_halves(tm)], axis=0)
        _to_groups(dcat[:, :a_w], dsc)
        dya_ref[...] = dcat[:, :a_w].astype(BF16)
        for i, dil in enumerate(dils):
            _split_residues(dsc, dya_res[i], dil, tm)
        dyc = dcat[:, a_w:]
        gb = pc_ref[:, :c_w].astype(F32)
        gc = pc_ref[:, c_w:2 * c_w].astype(F32)
        u = pc_ref[:, 2 * c_w:].astype(F32)
        z = gc * u
        before = halo_ref[:, c_w:2 * c_w].astype(F32) * halo_ref[:, 2 * c_w:].astype(F32)
        before = jnp.where(tile_i > 0, before, jnp.zeros_like(before))
        z1, z2 = _shift_rows(z, before, tm)
        w0, w1, w2 = cprm_ref[0:1, :], cprm_ref[1:2, :], cprm_ref[2:3, :]
        cv = w0 * z2 + w1 * z1 + w2 * z + cprm_ref[3:4, :]
        dcv = dyc * gb
        cs_ref[0:1, :] += jnp.sum(dcv * z2, axis=0, keepdims=True)
        cs_ref[1:2, :] += jnp.sum(dcv * z1, axis=0, keepdims=True)
        cs_ref[2:3, :] += jnp.sum(dcv * z, axis=0, keepdims=True)
        cs_ref[3:4, :] += jnp.sum(dcv, axis=0, keepdims=True)
        row = lax.broadcasted_iota(jnp.int32, dcv.shape, 0)
        after = carry[...]
        d1 = jnp.where(row == tm - 1, after[0:1, :], pltpu.roll(dcv, tm - 1, 0))
        d2 = jnp.where(row == tm - 2, after[0:1, :],
                       jnp.where(row == tm - 1, after[1:2, :], pltpu.roll(dcv, tm - 2, 0)))
        dz = w2 * dcv + w1 * d1 + w0 * d2
        dpc_ref[:, :c_w] = (dyc * cv).astype(BF16)
        dpc_ref[:, c_w:2 * c_w] = (dz * u).astype(BF16)
        dpc_ref[:, 2 * c_w:] = (dz * gc).astype(BF16)
        carry[...] = dcv[0:8, :]

    def rev(width, col=0):
        return pl.BlockSpec((tm, width), lambda s: (n_m - 1 - s, col))

    fixed_d = pl.BlockSpec((8, d_model), lambda s: (0, 0))
    fixed_c = pl.BlockSpec((8, c_w), lambda s: (0, 0))
    return pl.pallas_call(
        body, name=f"mixer_bwd_out_{l}", grid=(n_m,),
        in_specs=[rev(d_model), rev(d_model), fixed_d,
                  pl.BlockSpec((N_DEV, r8, d_model), lambda s: (0, 0, 0)),
                  rev(3 * c_w, 1),
                  pl.BlockSpec((8, 3 * c_w), lambda s: (jnp.maximum((n_m - 1 - s) * hb - 1, 0), 1)),
                  fixed_c],
        out_specs=[rev(a_w), rev(3 * c_w), rev(d_model), fixed_d, fixed_c]
        + [_res_spec(dil, tm, a_w, lambda s: n_m - 1 - s) for dil in dils],
        out_shape=[SDS((t_len, a_w), BF16), SDS((t_len, 3 * c_w), BF16), SDS((t_len, d_model), BF16),
                   SDS((8, d_model), F32), SDS((8, c_w), F32)]
        + [SDS((dil, t_len // dil, a_w), BF16) for dil in dils],
        scratch_shapes=[pltpu.VMEM((8, c_w), F32), _groups(tm, a_w)],
        compiler_params=_params(1),
    )(dxo, ymix, prm, woutg, proj, proj, cprm)


def _attn_bwd(qk, v, v_col, dya, ya, lse, l):
    dil, seq, a2 = qk.shape
    a_w = a2 // 2
    n_sb = _blocks_per_step(seq)
    rows_step = n_sb * BLOCK
    n_step = seq // rows_step
    last = _block_rows(n_sb - 1)
    lse_w = LANES // (a_w // HEAD_DIM)
    single = n_step == 1

    def body(q_ref, kc_ref, vc_ref, do_ref, ya_ref, lse_ref,
             dq_ref, dk_ref, dv_ref, hold_k, hold_v, acc_k, acc_v, kp_ref, vp_ref):
        j = pl.program_id(1)

        @pl.when(j == 0)
        def _():
            kp_ref[...] = jnp.zeros_like(kp_ref)
            vp_ref[...] = jnp.zeros_like(vp_ref)

        @pl.when((pl.program_id(0) == 0) & (j == 0))
        def _():
            hold_k[...] = jnp.zeros_like(hold_k)
            hold_v[...] = jnp.zeros_like(hold_v)

        @pl.when(j < n_step)
        def _():
            masks = [_band_mask(j == 0)] + [_band_mask(False)] * (n_sb - 1)
            low = lax.broadcasted_iota(jnp.int32, (BLOCK, LANES), 1) < HEAD_DIM
            units = [(sb, slice(p * LANES, (p + 1) * LANES)) for sb in range(n_sb) for p in range(a_w // LANES)]
            acc_k[...] = jnp.zeros_like(acc_k)
            acc_v[...] = jnp.zeros_like(acc_v)
            heads, scores, dps = [], [], []
            for sb, sl in units:
                qp = q_ref[_block_rows(sb), sl]
                k2 = _with_previous(kp_ref, kc_ref, sb, sl)
                v2 = _with_previous(vp_ref, vc_ref, sb, sl)
                dop = do_ref[_block_rows(sb), sl]
                prod = dop.astype(F32) * ya_ref[_block_rows(sb), sl].astype(F32)
                lsep = lse_ref[_block_rows(sb), :]
                first = 2 * (sl.start // LANES) * lse_w
                for hh, half in enumerate((low, ~low)):
                    qh = jnp.where(half, qp, jnp.zeros_like(qp))
                    doh = jnp.where(half, dop, jnp.zeros_like(dop))
                    dsum = jnp.sum(jnp.where(half, prod, 0.0), axis=-1, keepdims=True)
                    heads.append((qh, doh, dsum, lsep[:, first + hh * lse_w:first + hh * lse_w + 1], masks[sb]))
                    scores.append(_dot_nt(qh, k2))
                    dps.append(_dot_nt(doh, v2))
            dss, prs = [], []
            for (qh, doh, dsum, lse_h, valid), s, dp in zip(heads, scores, dps):
                pr = jnp.exp(jnp.where(valid, s * SM_SCALE, NEG) - lse_h)
                dss.append((pr * (dp - dsum) * SM_SCALE).astype(BF16))
                prs.append(pr.astype(BF16))
            for i, (sb, sl) in enumerate(units):
                k2 = _with_previous(kp_ref, kc_ref, sb, sl)
                h0, h1 = heads[2 * i], heads[2 * i + 1]
                dq_ref[_block_rows(sb), sl] = jnp.where(low, _dot(dss[2 * i], k2),
                                                        _dot(dss[2 * i + 1], k2)).astype(BF16)
                two = slice(sb * BLOCK, (sb + 2) * BLOCK)
                acc_k[two, sl] += _dot_tn(dss[2 * i], h0[0]) + _dot_tn(dss[2 * i + 1], h1[0])
                acc_v[two, sl] += _dot_tn(prs[2 * i], h0[1]) + _dot_tn(prs[2 * i + 1], h1[1])
            for out_ref, hold, acc in ((dk_ref, hold_k, acc_k), (dv_ref, hold_v, acc_v)):
                if single:
                    out_ref[...] = acc[BLOCK:, :].astype(BF16)
                    continue
                if n_sb > 1:
                    out_ref[:rows_step - BLOCK, :] = hold[:rows_step - BLOCK, :].astype(BF16)
                out_ref[last, :] = (hold[last, :] + acc[:BLOCK, :]).astype(BF16)
                hold[...] = acc[BLOCK:, :]
            kp_ref[...] = kc_ref[last, :]
            vp_ref[...] = vc_ref[last, :]

        @pl.when(j == n_step)
        def _():
            dk_ref[...] = hold_k[...].astype(BF16)
            dv_ref[...] = hold_v[...].astype(BF16)

    def blk(f):
        return pl.BlockSpec((None, rows_step, a_w), f)

    def cur(j):
        return jnp.minimum(j, n_step - 1)

    def late(j):
        return jnp.maximum(j - 1, 0)

    return pl.pallas_call(
        body, name=f"attn_bwd_{l}_d{dil}", grid=(dil, 1 if single else n_step + 1),
        in_specs=[blk(lambda r, j: (r, cur(j), 0)),
                  blk(lambda r, j: (r, cur(j), 1)),
                  blk(lambda r, j: (r, cur(j), v_col)),
                  blk(lambda r, j: (r, cur(j), 0)),
                  blk(lambda r, j: (r, cur(j), 0)),
                  pl.BlockSpec((None, rows_step, LANES), lambda r, j: (r, cur(j), 0))],
        out_specs=[blk(lambda r, j: (r, cur(j), 0)), blk(lambda r, j: (r, late(j), 0)),
                   blk(lambda r, j: (r, late(j), 0))],
        out_shape=[SDS((dil, seq, a_w), BF16)] * 3,
        scratch_shapes=[pltpu.VMEM((rows_step, a_w), F32), pltpu.VMEM((rows_step, a_w), F32),
                        pltpu.VMEM((rows_step + BLOCK, a_w), F32), pltpu.VMEM((rows_step + BLOCK, a_w), F32),
                        pltpu.VMEM((BLOCK, a_w), BF16), pltpu.VMEM((BLOCK, a_w), BF16)],
        compiler_params=_params(2),
    )(qk, qk, v, dya, ya, lse)


def _mixer_bwd_in(parts, proj, dpc, dxo, x_in, prm, wing, qkg, bd, l, tm):
    t_len, d_model = x_in.shape
    a_w = d_model // 2
    c_w = d_model - a_w
    c8 = wing.shape[-1]
    p_c = N_DEV * c8

    dils = DILATIONS[1:]
    n_in = 3 * len(DILATIONS)

    def body(*refs):
        nat, res = refs[:3], refs[3:n_in]
        (pa_ref, dpc_ref, dxo_ref, x_ref, prm_ref, w_ref, qkg_ref, bd_ref,
         dx_ref, dproj_ref, st_ref, qs_ref) = refs[n_in:n_in + 12]
        sums, wcat, wsems = refs[n_in + 12:n_in + 15], refs[n_in + 15], refs[n_in + 16]

        @pl.when(pl.program_id(0) == 0)
        def _():
            st_ref[...] = jnp.zeros_like(st_ref)
            qs_ref[...] = jnp.zeros_like(qs_ref)
            _join_columns(w_ref, wcat, wsems)

        for t in range(3):
            _to_groups(nat[t][...].astype(F32), sums[t])
            for i, dil in enumerate(dils):
                _merge_residues(res[3 * i + t], sums[t], dil, tm, add=True)
        bdm = bd_ref[...]
        grads = [_from_groups(sums[0]), _from_groups(sums[1])]
        for i in range(2):
            t = pa_ref[:, i * a_w:(i + 1) * a_w].astype(F32)
            rstd = lax.rsqrt(_head_mean(t * t, bdm) + EPS)
            that = t * rstd
            qs_ref[i:i + 1, :] += jnp.sum(grads[i] * that, axis=0, keepdims=True)
            dhat = grads[i] * qkg_ref[i:i + 1, :]
            dt = rstd * (dhat - that * _head_mean(dhat * that, bdm))
            dproj_ref[:, i * a_w:(i + 1) * a_w] = dt.astype(BF16)
        dproj_ref[:, 2 * a_w:3 * a_w] = _from_groups(sums[2]).astype(BF16)
        dproj_ref[:, 3 * a_w:] = dpc_ref[...]
        dh = jnp.concatenate([_dot_nt(dproj_ref[rows, :], wcat[...]) for rows in _row_halves(tm)], axis=0)
        dx, dshift, dscale, dng = _ada_bwd(dh, x_ref[...], prm_ref)
        dx_ref[...] = dxo_ref[...] + dx
        st_ref[0:1, :] += dshift
        st_ref[1:2, :] += dscale
        st_ref[2:3, :] += dng

    ta = pl.BlockSpec((tm, a_w), lambda m: (m, 0))
    tile = pl.BlockSpec((tm, d_model), lambda m: (m, 0))
    return pl.pallas_call(
        body, name=f"mixer_bwd_in_{l}", grid=(t_len // tm,),
        in_specs=[ta] * 3 + [_res_spec(dil, tm, a_w) for dil in dils for _ in range(3)]
                 + [pl.BlockSpec((tm, 3 * a_w), lambda m: (m, 0)),
                             pl.BlockSpec((tm, 3 * c_w), lambda m: (m, 0)),
                             tile, tile, pl.BlockSpec((8, d_model), lambda m: (0, 0)),
                             ANY,
                             pl.BlockSpec((8, a_w), lambda m: (0, 0)),
                             pl.BlockSpec(bd.shape, lambda m: (0, 0))],
        out_specs=[tile, pl.BlockSpec((tm, p_c), lambda m: (m, 0)),
                   pl.BlockSpec((8, d_model), lambda m: (0, 0)), pl.BlockSpec((8, a_w), lambda m: (0, 0))],
        out_shape=[SDS((t_len, d_model), F32), SDS((t_len, p_c), BF16), SDS((8, d_model), F32), SDS((8, a_w), F32)],
        scratch_shapes=[_groups(tm, a_w)] * 3 + [pltpu.VMEM((d_model, p_c), BF16), pltpu.SemaphoreType.DMA((N_DEV,))],
        compiler_params=_params(1),
    )(*[t.reshape(t_len, a_w) for t in parts[0]], *[t for p in parts[1:] for t in p],
      proj, dpc, dxo, x_in, prm, wing, qkg, bd)


def _loss_head(x, target, tm):
    t_len, d_model = x.shape

    def body(x_ref, t_ref, dx_ref, loss_ref):
        @pl.when(pl.program_id(0) == 0)
        def _():
            loss_ref[...] = jnp.zeros_like(loss_ref)

        diff = x_ref[...] - t_ref[...]
        dx_ref[...] = diff * (1.0 / d_model)
        per_token = jnp.sum(diff * diff, axis=-1, keepdims=True) * (1.0 / d_model)
        loss_ref[...] += 0.5 * jnp.sum(per_token)

    tile = pl.BlockSpec((tm, d_model), lambda m: (m, 0))
    return pl.pallas_call(
        body, name="loss_head", grid=(t_len // tm,),
        in_specs=[tile, tile],
        out_specs=[tile, pl.BlockSpec((8, LANES), lambda m: (0, 0))],
        out_shape=[SDS((t_len, d_model), F32), SDS((8, LANES), F32)],
        compiler_params=_params(1),
    )(x, target)


def _row_tile(rows, cols):
    best = None
    for t in range(16, rows + 1, 16):
        if rows % t == 0 and t * cols * 4 <= (1 << 20):
            best = t
    return best if best is not None else rows


def _adamw(pieces, w, m, v, name, first_q=0, earlier=None):
    n_q = len(pieces)
    n_p, rows, cols = pieces[0].shape
    tr = _row_tile(rows, cols)
    n_i = rows // tr
    n_e = 0 if earlier is None else 4
    c1 = 1.0 - ADAM_B1 ** ADAM_STEP
    c2 = 1.0 - ADAM_B2 ** ADAM_STEP

    def body(*refs):
        p_refs = refs[:n_q]
        w_ref, m_ref, v_ref = refs[n_q:n_q + 3]
        g_ref, d_ref, nm_ref, nv_ref = refs[n_q + 3 + n_e:]
        for q in range(n_q):
            @pl.when(pl.program_id(0) == q)
            def _():
                g = p_refs[q][0].astype(F32)
                for i in range(1, n_p):
                    g = g + p_refs[q][i].astype(F32)
                g_ref[...] = g
                nm = ADAM_B1 * m_ref[...] + (1.0 - ADAM_B1) * g
                nv = ADAM_B2 * v_ref[...] + (1.0 - ADAM_B2) * (g * g)
                nm_ref[...] = nm
                nv_ref[...] = nv
                d_ref[...] = -ADAM_LR * ((nm / c1) / (jnp.sqrt(nv / c2) + ADAM_EPS) + ADAM_WD * w_ref[...])

    def piece_spec(q):
        return pl.BlockSpec((n_p, tr, cols), lambda a, i: (0, jnp.where(a == q, i, 0), 0))

    tile = pl.BlockSpec((tr, cols), lambda a, i: ((first_q + a) * n_i + i, 0))
    return pl.pallas_call(
        body, name=name, grid=(n_q, n_i),
        in_specs=[piece_spec(q) for q in range(n_q)] + [tile, tile, tile] + [ANY] * n_e,
        out_specs=[tile] * 4,
        out_shape=[SDS(w.shape, F32)] * 4,
        input_output_aliases={n_q + 3 + k: k for k in range(n_e)},
        compiler_params=_params(2),
    )(*pieces, w, m, v, *(earlier or []))


def _pack(vecs):
    flat = jnp.concatenate([v.reshape(-1).astype(F32) for v in vecs])
    rows = -(-flat.shape[0] // (8 * LANES)) * 8
    return jnp.pad(flat, (0, rows * LANES - flat.shape[0])).reshape(rows, LANES)


def _unpack(packed, shapes):
    lead = packed.shape[:-2]
    flat = packed.reshape(lead + (-1,))
    out, off = [], 0
    for s in shapes:
        size = 1
        for d in s:
            size *= d
        out.append(flat[..., off:off + size].reshape(lead + tuple(s)))
        off += size
    return out


def kernel(x, c, w_ada, b_ada, norm_g, w_in, q_norm_g, k_norm_g, conv_w, conv_b, w_out, ffn_w1, ffn_w2, loss_target, m_w_ada, m_b_ada, m_norm_g, m_w_in, m_q_norm_g, m_k_norm_g, m_conv_w, m_conv_b, m_w_out, m_ffn_w1, m_ffn_w2, v_w_ada, v_b_ada, v_norm_g, v_w_in, v_q_norm_g, v_k_norm_g, v_conv_w, v_conv_b, v_w_out, v_ffn_w1, v_ffn_w2):
    _, t_len, d_model = x.shape
    n_layer = w_ada.shape[0]
    a_w = d_model // 2
    c_w = d_model - a_w
    n_head = a_w // HEAD_DIM
    ada_cols = w_ada.shape[-1]
    tm = min(TOKEN_TILE, t_len)
    tm_ffn = min(FFN_FWD_TOKEN_TILE, t_len)
    tm_ffn_bwd = min(FFN_BWD_TOKEN_TILE, t_len)
    me = _index(_my_place())
    x2 = x.reshape(t_len, d_model)
    target = loss_target.reshape(t_len, d_model)

    small_shapes = [(d_model,), norm_g.shape, conv_w.shape]
    gathered = _allgather_small(_pack([c, norm_g, conv_w]), "gather_small_inputs")
    c_all, ng_parts, cw_parts = _unpack(gathered, small_shapes)
    norm_g_full = jnp.moveaxis(ng_parts, 0, 2).reshape(n_layer, 3, d_model)
    conv_w_full = jnp.moveaxis(cw_parts, 0, 2).reshape(n_layer, 3, c_w)

    b_loc = lax.dynamic_slice_in_dim(b_ada, me * ada_cols, ada_cols, axis=1).reshape(n_layer, 1, ada_cols)
    mod_part, silu_c = _mod_part(c_all, w_ada, b_loc)
    mod_all = _allgather_small(_pack([mod_part]), "gather_mod")
    mod_all = _unpack(mod_all, [mod_part.shape])[0]
    mod_mine = lax.dynamic_index_in_dim(mod_all, me, axis=2, keepdims=False)
    mod = jnp.moveaxis(mod_mine, 0, 1).reshape(n_layer, 3, 3, d_model)

    def prm_of(l, sub):
        rows = jnp.stack([norm_g_full[l, sub], mod[l, sub, 0], mod[l, sub, 1], mod[l, sub, 2]])
        return jnp.pad(rows, ((0, 4), (0, 0)))

    prm = [[prm_of(l, sub) for sub in range(3)] for l in range(n_layer)]
    qkg = [jnp.pad(jnp.stack([jnp.tile(q_norm_g[l], n_head), jnp.tile(k_norm_g[l], n_head)]), ((0, 6), (0, 0)))
           for l in range(n_layer)]
    cprm = [jnp.pad(jnp.concatenate([conv_w_full[l], conv_b[l][None]]), ((0, 4), (0, 0))) for l in range(n_layer)]
    head_of = jnp.arange(min(MXU_WIDTH, a_w)) // HEAD_DIM
    bd = (head_of[:, None] == head_of[None, :]).astype(BF16)
    expand = (jnp.arange(LANES)[:, None] == (jnp.arange(a_w)[None, :] // HEAD_DIM) * (LANES // n_head)).astype(BF16)

    f8 = ffn_w1.shape[-1]
    r8 = ffn_w2.shape[-2]
    c8 = w_in.shape[-1]
    o8 = w_out.shape[-2]
    me_arr = jnp.reshape(me, (1,)).astype(jnp.int32)
    win_b = [_place_own(w_in, (l,), me_arr, f"own_w_in_{l}") for l in range(n_layer)]
    wout_b = [_place_own(w_out, (l,), me_arr, f"own_w_out_{l}") for l in range(n_layer)]
    w1_b = [[_place_own(ffn_w1, (l, s), me_arr, f"own_w1_{l}_{s}") for s in range(2)] for l in range(n_layer)]
    w2_b = [[_place_own(ffn_w2, (l, s), me_arr, f"own_w2_{l}_{s}") for s in range(2)] for l in range(n_layer)]
    w1_first, w2_first = _allgather_big([w1_b[0][0], w2_b[0][0]], "gather_first_weights")
    groups = [[win_b[0], wout_b[0]], [w1_b[0][1], w2_b[0][1]]]
    for l in range(1, n_layer):
        groups += [[w1_b[l][0], w2_b[l][0]], [win_b[l], wout_b[l]], [w1_b[l][1], w2_b[l][1]]]
    flat = [a for grp in groups for a in grp]
    w_sends, w_recvs, w_lands, w_token = _send_start(flat, None, w1_first, "weights_start")

    def gathered(gi, after):
        lo = sum(len(grp) for grp in groups[:gi])
        hi = lo + len(groups[gi])
        return _send_wait(w_lands[lo:hi], w_sends[lo:hi], w_recvs[lo:hi], after, f"weights_wait_{gi}")

    def chunked(w1g):
        return w1g.reshape(2, N_CHUNK, d_model, f8)

    saved = []
    xc = x2
    for l in range(n_layer):
        keep = {"x0": xc}
        if l == 0:
            w1g, w2g = w1_first, w2_first
            prm_first = prm[0][0] + w_token[0, 0]
        else:
            w1g, w2g = gathered(3 * l - 1, xc)
            prm_first = prm[l][0]
        keep["w_ffn0"] = (chunked(w1g), w2g)
        keep["act0"], keep["h0"], keep["y0"], xc = _ffn_fwd(xc, prm_first, *keep["w_ffn0"], l, 0, tm_ffn)
        keep["x1"] = xc
        wing, woutg = gathered(0 if l == 0 else 3 * l, xc)
        keep.update(wing=wing, woutg=woutg)
        proj, qkn, keep["h1"], ycv, *res = _mixer_in(xc, prm[l][1], wing, qkg[l], cprm[l], bd, l, tm)
        qkv = [(qkn[None], proj[None], 2)] + [(res[2 * i], res[2 * i + 1], 0) for i in range(len(DILATIONS) - 1)]
        branches = [_attn_fwd(*t, l) for t in qkv]
        cat, lse, *res = _attn_combine([b[0] for b in branches], [b[1] for b in branches], ycv, expand, tm, l)
        n_r = len(DILATIONS) - 1
        keep["ya"] = [cat[None]] + res[:n_r]
        keep["lse"] = [lse[None]] + res[n_r:]
        xc, keep["ymix"] = _mixer_out(cat, xc, prm[l][1], woutg, l, tm)
        keep.update(proj=proj, qkv=qkv, cat=cat)
        keep["x2"] = xc
        w1g, w2g = gathered(1 if l == 0 else 3 * l + 1, xc)
        keep["w_ffn2"] = (chunked(w1g), w2g)
        keep["act2"], keep["h2"], keep["y2"], xc = _ffn_fwd(xc, prm[l][2], *keep["w_ffn2"], l, 1, tm_ffn)
        saved.append(keep)

    dx, loss_blk = _loss_head(xc, target, tm)
    loss = lax.psum(loss_blk[0, 0], ("x", "y", "c"))

    tk = min(GRAD_TOKEN_TILE, t_len)
    stats =[[None] * 3 for _ in range(n_layer)]
    conv_stats = [None] * n_layer
    qk_stats = [None] * n_layer

    started = {}
    dummy = jnp.zeros((8, LANES), F32)

    def start_exchange(g, name):
        sends, recvs, thru, token = _send_start(
            [_place_own(g, None, me_arr, f"own_grad_{name}")], [g], dummy, f"grads_start_{name}")
        started[name] = (thru[0], thru[1], sends[0], recvs[0], token)
        return token

    def wait_exchange(names, after, name):
        ent = [started[k] for k in names]
        done = _send_wait([e[0] for e in ent] + [e[1] for e in ent], [e[2] for e in ent], [e[3] for e in ent],
                          after, name)
        return dict(zip(names, done))

    def ffn_backward(dxo, keep, l, sub, which, prm_dep):
        dx_in, da, dy, st = _ffn_bwd(dxo, keep[f"y{which}"], keep[f"x{which}"], prm_dep,
                                     keep[f"act{which}"], *keep[f"w_ffn{which}"], l, sub, tm_ffn_bwd)
        g_w2 = _matmul_tn(
            keep[f"act{which}"], dy,
            pl.BlockSpec((None, None, tk, f8), lambda p, k: (2, p, k, 0)),
            pl.BlockSpec((tk, d_model), lambda p, k: (k, 0)),
            SDS((N_CHUNK, f8, d_model), BF16), pl.BlockSpec((None, f8, d_model), lambda p, k: (p, 0, 0)),
            N_CHUNK, t_len, tk, f"grad_w2_{l}_{sub}", dummy)
        token = start_exchange(g_w2.reshape(N_DEV, r8, d_model), f"w2_{l}_{sub}")
        g_w1 = _matmul_tn(
            da.reshape(N_DEV, t_len, f8), keep[f"h{which}"],
            pl.BlockSpec((None, tk, f8), lambda p, k: (p, k, 0)),
            pl.BlockSpec((tk, d_model), lambda p, k: (k, 0)),
            SDS((N_DEV, f8, d_model), BF16), pl.BlockSpec((None, f8, d_model), lambda p, k: (p, 0, 0)),
            N_DEV, t_len, tk, f"grad_w1_{l}_{sub}", token)
        token = start_exchange(g_w1, f"w1_{l}_{sub}")
        return dx_in, st, token[0, 0]

    token = 0.0
    for l in reversed(range(n_layer)):
        keep = saved[l]
        wing, woutg = keep["wing"], keep["woutg"]
        dx, stats[l][2], token = ffn_backward(dx, keep, l, 1, 2, prm[l][2] + token)
        dya, dpc, dy, st_gate, conv_stats[l], *dya_res = _mixer_bwd_out(
            dx, keep["ymix"], prm[l][1] + token, woutg, keep["proj"], cprm[l], l, tm)
        dyas = [dya[None]] + dya_res
        g_wout = _matmul_tn(
            keep["cat"], dy,
            pl.BlockSpec((tk, d_model), lambda p, k: (k, 0)),
            pl.BlockSpec((tk, d_model), lambda p, k: (k, 0)),
            SDS((d_model, d_model), BF16), pl.BlockSpec((d_model, d_model), lambda p, k: (0, 0)),
            1, t_len, tk, f"grad_wout_{l}", dummy)
        token = start_exchange(g_wout.reshape(N_DEV, o8, d_model), f"wout_{l}")[0, 0]
        parts = [_attn_bwd(*keep["qkv"][i], dyas[i], keep["ya"][i], keep["lse"][i], l)
                 for i in range(len(DILATIONS))]
        dx, dproj, st_norm, qk_stats[l] = _mixer_bwd_in(
            parts, keep["proj"], dpc, dx, keep["x1"], prm[l][1] + token, wing, qkg[l], bd, l, tm)
        stats[l][1] = st_norm + st_gate
        g_win = _matmul_tn(
            keep["h1"], dproj,
            pl.BlockSpec((tk, d_model), lambda p, k: (k, 0)),
            pl.BlockSpec((tk, N_DEV // 2 * c8), lambda p, k: (k, p)),
            SDS((N_DEV, d_model, c8), BF16), pl.BlockSpec((N_DEV // 2, d_model, c8), lambda p, k: (p, 0, 0)),
            2, t_len, tk, f"grad_win_{l}", dummy)
        token = start_exchange(g_win, f"win_{l}")[0, 0]
        dx, stats[l][0], token = ffn_backward(dx, keep, l, 0, 0, prm[l][0] + token)
    grad_x = dx.reshape(x.shape)

    def update(pieces, w, m, v, name):
        shape = w.shape
        cols = shape[-1]
        outs = _adamw([p.reshape(p.shape[0], -1, cols) for p in pieces], w.reshape(-1, cols),
                      m.reshape(-1, cols), v.reshape(-1, cols), name)
        return [o.reshape(shape) for o in outs]

    last = "w1_0_0"
    lands = wait_exchange([k for k in started if k != last], started[last][4], "grads_wait_early")
    res = {
        "w_in": update([lands[f"win_{l}"] for l in range(n_layer)], w_in, m_w_in, v_w_in, "adamw_w_in"),
        "w_out": update([lands[f"wout_{l}"] for l in range(n_layer)], w_out, m_w_out, v_w_out, "adamw_w_out"),
        "ffn_w2": update([lands[f"w2_{l}_{s}"] for l in range(n_layer) for s in range(2)],
                         ffn_w2, m_ffn_w2, v_ffn_w2, "adamw_ffn_w2"),
    }

    dmod = jnp.stack([jnp.stack([stats[l][sub][r] for sub in range(3) for r in (0, 1, 3)]) for l in range(n_layer)])
    dng = jnp.stack([jnp.stack([stats[l][sub][2] for sub in range(3)]) for l in range(n_layer)])
    dcw = jnp.stack([conv_stats[l][0:3] for l in range(n_layer)])
    dcb = jnp.stack([conv_stats[l][3] for l in range(n_layer)])
    dqk = jnp.stack([qk_stats[l][0:2].reshape(2, n_head, HEAD_DIM).sum(axis=1) for l in range(n_layer)])
    part_shapes = [dmod.shape, dng.shape, dcw.shape, dcb.shape, dqk.shape]
    parts_all = _allgather_small(_pack([dmod, dng, dcw, dcb, dqk]), "gather_small_grads")
    dmod_all = _unpack(parts_all, part_shapes)[0].reshape(N_DEV, n_layer, 9 * d_model)
    summed = _unpack(_sum_devices(parts_all), part_shapes)
    g_b_ada = summed[0].reshape(n_layer, 9 * d_model)
    ng_cols = norm_g.shape[-1]
    g_norm_g = lax.dynamic_slice_in_dim(summed[1], me * ng_cols, ng_cols, axis=2)
    cw_cols = conv_w.shape[-1]
    g_conv_w = lax.dynamic_slice_in_dim(summed[2], me * cw_cols, cw_cols, axis=2)
    g_conv_b = summed[3]
    g_q, g_k = summed[4][:, 0], summed[4][:, 1]
    dmod_loc = lax.dynamic_slice_in_dim(dmod_all, me * ada_cols, ada_cols, axis=2)
    g_w_ada = _w_ada_grad(silu_c.T, jnp.moveaxis(dmod_loc, 0, 1))
    res["w_ada"] = update([g_w_ada[None]], w_ada, m_w_ada, v_w_ada, "adamw_w_ada")

    names = ["b_ada", "norm_g", "q_norm_g", "k_norm_g", "conv_w", "conv_b"]
    g_small = [g_b_ada, g_norm_g, g_q, g_k, g_conv_w, g_conv_b]
    w_small = [b_ada, norm_g, q_norm_g, k_norm_g, conv_w, conv_b]
    m_small = [m_b_ada, m_norm_g, m_q_norm_g, m_k_norm_g, m_conv_w, m_conv_b]
    v_small = [v_b_ada, v_norm_g, v_q_norm_g, v_k_norm_g, v_conv_w, v_conv_b]
    outs = _adamw([_pack(g_small)[None]], _pack(w_small), _pack(m_small), _pack(v_small), "adamw_small")
    shapes = [w.shape for w in w_small]
    unpacked = [_unpack(o, shapes) for o in outs]
    for i, nme in enumerate(names):
        res[nme] = [unpacked[k][i] for k in range(4)]

    w1_t = [jnp.swapaxes(t, -1, -2) for t in (ffn_w1, m_ffn_w1, v_ffn_w1)]
    w1_flat = [t.reshape(-1, d_model) for t in w1_t]
    w1_keys = [f"w1_{l}_{s}" for l in range(n_layer) for s in range(2)]
    assert w1_keys[0] == last
    early = _adamw([lands[k] for k in w1_keys[1:]], *w1_flat, "adamw_ffn_w1_early", first_q=1)
    follows = (res["w_in"][1][0, :1, :1] + res["w_out"][1][0, :1, :1] + res["ffn_w2"][1][0, 0, :1, :1]
               + res["w_ada"][1][0, :1, :1] + outs[1][:1, :1] + early[1][-1:, :1])
    lands.update(wait_exchange([last], follows, "grads_wait_last"))
    done = _adamw([lands[last]], *w1_flat, "adamw_ffn_w1_last", first_q=0, earlier=early)
    res["ffn_w1"] = [jnp.swapaxes(t.reshape(w1_t[0].shape), -1, -2) for t in done]

    order = ["w_ada", "b_ada", "norm_g", "w_in", "q_norm_g", "k_norm_g", "conv_w", "conv_b", "w_out", "ffn_w1", "ffn_w2"]
    return (loss, grad_x, *[res[n][0] for n in order], *[res[n][1] for n in order],
            *[res[n][2] for n in order], *[res[n][3] for n in order])
```

```python
import jax
import jax.numpy as jnp
from jax import lax
from jax.experimental import pallas as pl
from jax.experimental.pallas import tpu as pltpu

F32 = jnp.float32
BF16 = jnp.bfloat16
SDS = jax.ShapeDtypeStruct

N_DEV = 8
HEAD_DIM = 64
BLOCK = 128
DILATIONS = (1, 4, 16)
ATTN_BLOCKS_PER_STEP = 4
N_CHUNK = 4
EPS = 1e-6
NEG = -1e30
SM_SCALE = HEAD_DIM ** -0.5
LANES = 128
MXU_WIDTH = 256
TOKEN_TILE = 512
FFN_ROW_PART = 256
FFN_FWD_TOKEN_TILE = 256
FFN_BWD_TOKEN_TILE = 256
GRAD_TOKEN_TILE = 4096
VMEM_LIMIT_BYTES = 56 * 1024 * 1024

ADAM_LR = 0.001
ADAM_B1 = 0.9
ADAM_B2 = 0.999
ADAM_EPS = 1e-08
ADAM_WD = 0.01
ADAM_STEP = 10

MESH_ID = pl.DeviceIdType.MESH
ANY = pl.BlockSpec(memory_space=pl.ANY)
HBM_SPEC = pl.BlockSpec(memory_space=pltpu.HBM)
SEM_SPEC = pl.BlockSpec(memory_space=pltpu.SEMAPHORE)
SIDE_EFFECT = pltpu.SideEffectType.DATAFLOW_SIDE_EFFECTING


def _params(n_axes):
    return pltpu.CompilerParams(dimension_semantics=("arbitrary",) * n_axes,
                                vmem_limit_bytes=VMEM_LIMIT_BYTES)


def _dot(a, b):
    return jnp.dot(a, b, preferred_element_type=F32)


def _dot_nt(a, b):
    return lax.dot_general(a, b, (((1,), (1,)), ((), ())), preferred_element_type=F32)


def _dot_tn(a, b):
    return lax.dot_general(a, b, (((0,), (0,)), ((), ())), preferred_element_type=F32)


def _row_halves(n):
    return (slice(0, n // 2), slice(n // 2, n))


def _split_bf16(t):
    hi = t.astype(BF16)
    return hi, (t - hi.astype(F32)).astype(BF16)


def _my_place():
    x, y, c = lax.axis_index("x"), lax.axis_index("y"), lax.axis_index("c")
    return x, y, c


def _peer(place, k):
    x, y, c = place
    return ((1 - x) if k & 4 else x, (1 - y) if k & 2 else y, (1 - c) if k & 1 else c)


def _index(place):
    return 4 * place[0] + 2 * place[1] + place[2]


def _allgather_small(v, name):
    rows, cols = v.shape

    def body(x_ref, out_ref, send_sems, recv_sems, local_sem):
        me = _my_place()
        mine = pltpu.make_async_copy(x_ref, out_ref.at[_index(me)], local_sem)
        mine.start()
        sends = []
        for k in range(1, N_DEV):
            cp = pltpu.make_async_remote_copy(
                src_ref=x_ref, dst_ref=out_ref.at[_index(me)], send_sem=send_sems.at[k - 1],
                recv_sem=recv_sems.at[k - 1], device_id=_peer(me, k), device_id_type=MESH_ID)
            cp.start()
            sends.append(cp)
        for k in range(1, N_DEV):
            pltpu.make_async_remote_copy(
                src_ref=x_ref, dst_ref=out_ref.at[_index(_peer(me, k))], send_sem=send_sems.at[k - 1],
                recv_sem=recv_sems.at[k - 1], device_id=_peer(me, k), device_id_type=MESH_ID).wait_recv()
        for cp in sends:
            cp.wait_send()
        mine.wait()

    return pl.pallas_call(
        body, name=name,
        out_shape=SDS((N_DEV, rows, cols), v.dtype),
        in_specs=[pl.BlockSpec(memory_space=pltpu.VMEM)],
        out_specs=pl.BlockSpec(memory_space=pltpu.VMEM),
        scratch_shapes=[pltpu.SemaphoreType.DMA((N_DEV - 1,)), pltpu.SemaphoreType.DMA((N_DEV - 1,)),
                        pltpu.SemaphoreType.DMA],
    )(v)


def _allgather_big(lands, name):
    n = len(lands)

    def body(*refs):
        outs = refs[n:2 * n]
        send_sems, recv_sems = refs[2 * n:]
        x, y, c = _my_place()
        me, sib = (x, y, c), (x, y, 1 - c)
        chips = [(1 - x, y), (x, 1 - y), (1 - x, 1 - y)]

        def copy(i, k, block, to):
            dst = outs[i].at[_index(block)]
            return pltpu.make_async_remote_copy(
                src_ref=dst, dst_ref=dst, send_sem=send_sems.at[7 * i + k],
                recv_sem=recv_sems.at[7 * i + k], device_id=to, device_id_type=MESH_ID)

        first = []
        for i in range(n):
            first.append(copy(i, 0, me, sib))
            for j, chip in enumerate(chips):
                first.append(copy(i, 1 + j, me, (*chip, c)))
        for cp in first:
            cp.start()
        passed = []
        for i in range(n):
            for j, chip in enumerate(chips):
                copy(i, 1 + j, (*chip, c), me).wait_recv()
                fwd = copy(i, 4 + j, (*chip, c), sib)
                fwd.start()
                passed.append(fwd)
        for i in range(n):
            copy(i, 0, sib, me).wait_recv()
            for j, chip in enumerate(chips):
                copy(i, 4 + j, (*chip, 1 - c), me).wait_recv()
        for cp in first + passed:
            cp.wait_send()

    return pl.pallas_call(
        body, name=name,
        out_shape=[SDS(a.shape, a.dtype) for a in lands],
        in_specs=[ANY] * n, out_specs=[ANY] * n,
        input_output_aliases={i: i for i in range(n)},
        scratch_shapes=[pltpu.SemaphoreType.DMA((7 * n,)), pltpu.SemaphoreType.DMA((7 * n,))],
    )(*lands)


def _place_own(src, lead, me_arr, name):
    rows, cols = src.shape[-2:]
    tr = _row_tile(rows, cols)

    def body(me_ref, s_ref, o_ref):
        o_ref[...] = s_ref[...].astype(BF16)

    if lead is None:
        in_spec = pl.BlockSpec((None, tr, cols), lambda i, me_ref: (me_ref[0], i, 0))
    else:
        in_spec = pl.BlockSpec((None,) * len(lead) + (tr, cols), lambda i, me_ref: (*lead, i, 0))
    return pl.pallas_call(
        body, name=name,
        grid_spec=pltpu.PrefetchScalarGridSpec(
            num_scalar_prefetch=1, grid=(rows // tr,), in_specs=[in_spec],
            out_specs=pl.BlockSpec((None, tr, cols), lambda i, me_ref: (me_ref[0], i, 0))),
        out_shape=SDS((N_DEV, rows, cols), BF16),
        compiler_params=_params(1),
    )(me_arr, src)


def _send_start(lands, pieces, after, name):
    n = len(lands)
    arrs = list(lands) + list(pieces or [])
    n_a = len(arrs)

    def body(*refs):
        land_r, piece_r = refs[:n], refs[n:n_a]
        send, recv = refs[n_a + 1:n_a + 1 + n], refs[n_a + 1 + n:n_a + 1 + 2 * n]
        token = refs[-1]
        me = _my_place()
        for i in range(n):
            for k in range(1, N_DEV):
                peer = _peer(me, k)
                src = piece_r[i].at[_index(peer)] if pieces else land_r[i].at[_index(me)]
                pltpu.make_async_remote_copy(
                    src_ref=src, dst_ref=land_r[i].at[_index(me)], send_sem=send[i], recv_sem=recv[i],
                    device_id=peer, device_id_type=MESH_ID).start()
        token[...] = jnp.zeros_like(token)

    outs = pl.pallas_call(
        body, name=name,
        out_shape=[pltpu.SemaphoreType.DMA(())] * (2 * n) + [pltpu.HBM(a.shape, a.dtype) for a in arrs]
        + [SDS((8, LANES), F32)],
        in_specs=[HBM_SPEC] * n_a + [ANY],
        out_specs=[SEM_SPEC] * (2 * n) + [HBM_SPEC] * n_a + [pl.BlockSpec(memory_space=pltpu.VMEM)],
        input_output_aliases={i: 2 * n + i for i in range(n_a)},
        compiler_params=pltpu.CompilerParams(has_side_effects=SIDE_EFFECT),
    )(*[pltpu.with_memory_space_constraint(a, pltpu.HBM) for a in arrs], after)
    return outs[:n], outs[n:2 * n], outs[2 * n:2 * n + n_a], outs[-1]


def _send_wait(thru, sends, recvs, after, name):
    n = len(sends)
    n_a = len(thru)

    def body(*refs):
        land_r = refs[:n]
        send, recv = refs[n_a:n_a + n], refs[n_a + n:n_a + 2 * n]
        me = _my_place()
        for i in range(n):
            seven = land_r[i].at[pl.ds(0, N_DEV - 1)]
            cp = pltpu.make_async_remote_copy(src_ref=seven, dst_ref=seven, send_sem=send[i], recv_sem=recv[i],
                                              device_id=_peer(me, 1), device_id_type=MESH_ID)
            cp.wait_send()
            cp.wait_recv()

    outs = pl.pallas_call(
        body, name=name,
        out_shape=[pltpu.HBM(a.shape, a.dtype) for a in thru],
        in_specs=[HBM_SPEC] * n_a + [SEM_SPEC] * (2 * n) + [ANY],
        out_specs=[HBM_SPEC] * n_a,
        input_output_aliases={i: i for i in range(n_a)},
        compiler_params=pltpu.CompilerParams(has_side_effects=SIDE_EFFECT),
    )(*thru, *sends, *recvs, after)
    return outs[:n]


def _ada_fwd(x, prm_ref):
    rstd = lax.rsqrt(jnp.mean(x * x, axis=-1, keepdims=True) + EPS)
    return (x * rstd * prm_ref[0:1, :]) * (1.0 + prm_ref[2:3, :]) + prm_ref[1:2, :]


def _ada_bwd(dh, x, prm_ref):
    ng, ops = prm_ref[0:1, :], 1.0 + prm_ref[2:3, :]
    rstd = lax.rsqrt(jnp.mean(x * x, axis=-1, keepdims=True) + EPS)
    xhat = x * rstd
    dxhat = dh * (ops * ng)
    dx = rstd * (dxhat - xhat * jnp.mean(dxhat * xhat, axis=-1, keepdims=True))
    dshift = jnp.sum(dh, axis=0, keepdims=True)
    dscale = jnp.sum(dh * (xhat * ng), axis=0, keepdims=True)
    dng = jnp.sum(dh * (xhat * ops), axis=0, keepdims=True)
    return dx, dshift, dscale, dng


def _head_mean(t, bd):
    hi = t.astype(BF16)
    g = bd.shape[0]
    parts = [_dot(hi[:, i:i + g], bd) for i in range(0, t.shape[1], g)]
    return jnp.concatenate(parts, axis=1) * (1.0 / HEAD_DIM)


def _join_columns(w_hbm, dst_ref, sems):
    cols = w_hbm.shape[2]
    copies = [pltpu.make_async_copy(w_hbm.at[j], dst_ref.at[:, pl.ds(j * cols, cols)], sems.at[j])
              for j in range(w_hbm.shape[0])]
    for cp in copies:
        cp.start()
    for cp in copies:
        cp.wait()


def _shift_rows(z, before, tm):
    row = lax.broadcasted_iota(jnp.int32, z.shape, 0)
    z1 = jnp.where(row == 0, before[7:8, :], pltpu.roll(z, 1, 0))
    z2 = jnp.where(row == 0, before[6:7, :], jnp.where(row == 1, before[7:8, :], pltpu.roll(z, 2, 0)))
    return z1, z2


def _mod_part(c_all, w_ada, b_loc):
    n_layer, d_model, cols = w_ada.shape

    def body(c_ref, w_ref, b_ref, out_ref, sc_ref):
        cv = c_ref[...]
        sc = cv * jax.nn.sigmoid(cv)
        sc_ref[...] = sc
        a_hi, a_lo = _split_bf16(sc)
        w_hi, w_lo = _split_bf16(w_ref[...])
        out_ref[...] = _dot(a_hi, w_hi) + _dot(a_hi, w_lo) + _dot(a_lo, w_hi) + b_ref[...]

    return pl.pallas_call(
        body, name="mod_part", grid=(n_layer,),
        in_specs=[pl.BlockSpec((N_DEV, d_model), lambda l: (0, 0)),
                  pl.BlockSpec((None, d_model, cols), lambda l: (l, 0, 0)),
                  pl.BlockSpec((None, 1, cols), lambda l: (l, 0, 0))],
        out_specs=[pl.BlockSpec((None, N_DEV, cols), lambda l: (l, 0, 0)),
                   pl.BlockSpec((N_DEV, d_model), lambda l: (0, 0))],
        out_shape=[SDS((n_layer, N_DEV, cols), F32), SDS((N_DEV, d_model), F32)],
        compiler_params=_params(1),
    )(c_all, w_ada, b_loc)


def _w_ada_grad(sc_t, dmod_loc):
    d_model = sc_t.shape[0]
    n_layer, _, cols = dmod_loc.shape

    def body(s_ref, d_ref, out_ref):
        acc = s_ref[:, 0:1] * d_ref[0:1, :]
        for b in range(1, N_DEV):
            acc = acc + s_ref[:, b:b + 1] * d_ref[b:b + 1, :]
        out_ref[...] = acc

    return pl.pallas_call(
        body, name="w_ada_grad", grid=(n_layer,),
        in_specs=[pl.BlockSpec((d_model, N_DEV), lambda l: (0, 0)),
                  pl.BlockSpec((None, N_DEV, cols), lambda l: (l, 0, 0))],
        out_specs=pl.BlockSpec((None, d_model, cols), lambda l: (l, 0, 0)),
        out_shape=SDS((n_layer, d_model, cols), F32),
        compiler_params=_params(1),
    )(sc_t, dmod_loc)


def _sum_devices(g):
    _, rows, cols = g.shape

    def body(g_ref, out_ref):
        acc = g_ref[0]
        for i in range(1, N_DEV):
            acc = acc + g_ref[i]
        out_ref[...] = acc

    return pl.pallas_call(body, name="sum_devices", out_shape=SDS((rows, cols), F32))(g)


def _ffn_fwd(x, prm, w1r, w2g, l, sub, tm):
    t_len, d_model = x.shape
    f8, r8 = w1r.shape[-1], w2g.shape[-2]
    n_m = t_len // tm

    def body(x_ref, xn_ref, prm_ref, w1_ref, w2_ref, act_ref, h_ref, y_ref, xo_ref, h_sc):
        @pl.when(pl.program_id(0) == 0)
        def _():
            h_sc[...] = _ada_fwd(x_ref[...], prm_ref).astype(BF16)

        h = h_sc[...]
        h_ref[...] = h

        def first(j):
            return _dot(h, w1_ref[0, j]), _dot(h, w1_ref[1, j])

        gu = first(0)
        yv = None
        for j in range(N_CHUNK):
            g, u = gu
            if j + 1 < N_CHUNK:
                gu = first(j + 1)
            else:
                h_next = _ada_fwd(xn_ref[...], prm_ref).astype(BF16)
            sig = jax.nn.sigmoid(g)
            sil = g * sig
            s = (sil * u).astype(BF16)
            act_ref[0, j] = (u * (sig * (1.0 + g * (1.0 - sig)))).astype(BF16)
            act_ref[1, j] = sil.astype(BF16)
            act_ref[2, j] = s
            part = _dot(s, w2_ref[2 * j:2 * j + 2].reshape(2 * r8, d_model))
            yv = part if yv is None else yv + part
        y_ref[...] = yv.astype(BF16)
        xo_ref[...] = x_ref[...] + (0.5 * prm_ref[3:4, :]) * yv
        h_sc[...] = h_next

    tile = pl.BlockSpec((tm, d_model), lambda m: (m, 0))
    return pl.pallas_call(
        body, name=f"ffn_fwd_{l}_{sub}", grid=(n_m,),
        in_specs=[tile, pl.BlockSpec((tm, d_model), lambda m: (jnp.minimum(m + 1, n_m - 1), 0)),
                  pl.BlockSpec((8, d_model), lambda m: (0, 0)),
                  pl.BlockSpec((2, N_CHUNK, d_model, f8), lambda m: (0, 0, 0, 0), pipeline_mode=pl.Buffered(1)),
                  pl.BlockSpec((N_DEV, r8, d_model), lambda m: (0, 0, 0), pipeline_mode=pl.Buffered(1))],
        out_specs=[pl.BlockSpec((3, N_CHUNK, tm, f8), lambda m: (0, 0, m, 0)), tile, tile, tile],
        out_shape=[SDS((3, N_CHUNK, t_len, f8), BF16), SDS((t_len, d_model), BF16),
                   SDS((t_len, d_model), BF16), SDS((t_len, d_model), F32)],
        scratch_shapes=[pltpu.VMEM((tm, d_model), BF16)],
        compiler_params=_params(1),
    )(x, x, prm, w1r, w2g)


def _ffn_bwd(dxo, y, x_in, prm, act, w1r, w2g, l, sub, tm):
    t_len, d_model = x_in.shape
    f8, r8 = w1r.shape[-1], w2g.shape[-2]
    n_m = t_len // tm

    def body(dxo_ref, y_ref, x_ref, prm_ref, act_ref, w1_ref, w2_ref, dx_ref, da_ref, dy_ref, st_ref):
        @pl.when(pl.program_id(0) == 0)
        def _():
            st_ref[...] = jnp.zeros_like(st_ref)

        dxo_v = dxo_ref[...]
        dy = ((0.5 * prm_ref[3:4, :]) * dxo_v).astype(BF16)
        dy_ref[...] = dy
        st_ref[3:4, :] += jnp.sum(0.5 * dxo_v * y_ref[...].astype(F32), axis=0, keepdims=True)

        def first(j):
            return _dot_nt(dy, w2_ref[2 * j:2 * j + 2].reshape(2 * r8, d_model))

        ds = first(0)
        dh = None
        for j in range(N_CHUNK):
            ds_now = ds
            if j + 1 < N_CHUNK:
                ds = first(j + 1)
            dg = (ds_now * act_ref[0, j].astype(F32)).astype(BF16)
            du = (ds_now * act_ref[1, j].astype(F32)).astype(BF16)
            da_ref[0, j] = dg
            da_ref[1, j] = du
            part = _dot_nt(dg, w1_ref[0, j]) + _dot_nt(du, w1_ref[1, j])
            dh = part if dh is None else dh + part
        dx, dshift, dscale, dng = _ada_bwd(dh, x_ref[...], prm_ref)
        dx_ref[...] = dxo_v + dx
        st_ref[0:1, :] += dshift
        st_ref[1:2, :] += dscale
        st_ref[2:3, :] += dng

    tile = pl.BlockSpec((tm, d_model), lambda m: (m, 0))
    fixed = pl.BlockSpec((8, d_model), lambda m: (0, 0))
    return pl.pallas_call(
        body, name=f"ffn_bwd_{l}_{sub}", grid=(n_m,),
        in_specs=[tile, tile, tile, fixed,
                  pl.BlockSpec((2, N_CHUNK, tm, f8), lambda m: (0, 0, m, 0)),
                  pl.BlockSpec((2, N_CHUNK, d_model, f8), lambda m: (0, 0, 0, 0), pipeline_mode=pl.Buffered(1)),
                  pl.BlockSpec((N_DEV, r8, d_model), lambda m: (0, 0, 0), pipeline_mode=pl.Buffered(1))],
        out_specs=[tile, pl.BlockSpec((2, N_CHUNK, tm, f8), lambda m: (0, 0, m, 0)), tile, fixed],
        out_shape=[SDS((t_len, d_model), F32), SDS((2, N_CHUNK, t_len, f8), BF16),
                   SDS((t_len, d_model), BF16), SDS((8, d_model), F32)],
        compiler_params=_params(1),
    )(dxo, y, x_in, prm, act, w1r, w2g)


def _matmul_tn(a, b, a_spec, b_spec, out_shape, out_spec, n_piece, t_len, tk, name, after):
    def body(a_ref, b_ref, after_ref, o_ref, acc):
        k = pl.program_id(1)

        @pl.when(k == 0)
        def _():
            acc[...] = jnp.zeros_like(acc)

        m_out, n_out = acc.shape
        if (m_out // 2) % LANES == 0:
            for rows in _row_halves(m_out):
                acc[rows, :] += _dot_tn(a_ref[:, rows], b_ref[...])
        else:
            for cols in _row_halves(n_out):
                acc[:, cols] += _dot_tn(a_ref[...], b_ref[:, cols])

        @pl.when(k == pl.num_programs(1) - 1)
        def _():
            if len(o_ref.shape) == 3:
                cols = o_ref.shape[2]
                for q in range(o_ref.shape[0]):
                    o_ref[q] = acc[:, q * cols:(q + 1) * cols].astype(o_ref.dtype)
            else:
                o_ref[...] = acc[...].astype(o_ref.dtype)

    blk = tuple(d for d in out_spec.block_shape if d is not None)
    acc_shape = blk if len(blk) == 2 else (blk[1], blk[0] * blk[2])
    return pl.pallas_call(
        body, name=name, grid=(n_piece, t_len // tk),
        in_specs=[a_spec, b_spec, ANY], out_specs=out_spec, out_shape=out_shape,
        scratch_shapes=[pltpu.VMEM(acc_shape, F32)],
        compiler_params=_params(2),
    )(a, b, after)


def _groups(tm, width):
    return pltpu.VMEM((width // LANES, tm, LANES), F32)


def _to_groups(val, dst_ref, first=0):
    for g in range(val.shape[1] // LANES):
        dst_ref[first + g] = val[:, g * LANES:(g + 1) * LANES]


def _from_groups(src_ref):
    return jnp.concatenate([src_ref[g] for g in range(src_ref.shape[0])], axis=1)


def _split_residues(src_ref, out_ref, dil, tm):
    for g in range(src_ref.shape[0]):
        for r in range(dil):
            out_ref[r, :, g * LANES:(g + 1) * LANES] = (
                src_ref.at[g][pl.ds(r, tm // dil, stride=dil), :].astype(out_ref.dtype))


def _merge_residues(in_ref, dst_ref, dil, tm, add=False):
    for g in range(dst_ref.shape[0]):
        for r in range(dil):
            rows = pl.ds(r, tm // dil, stride=dil)
            val = in_ref[r, :, g * LANES:(g + 1) * LANES].astype(F32)
            dst = dst_ref.at[g]
            dst[rows, :] = (dst[rows, :] + val) if add else val


def _res_spec(dil, tm, width, index=lambda m: m):
    return pl.BlockSpec((dil, tm // dil, width), lambda m: (0, index(m), 0))


def _mixer_in(x, prm, wing, qkg, cprm, bd, l, tm):
    t_len, d_model = x.shape
    a_w = d_model // 2
    c_w = d_model - a_w
    c8 = wing.shape[-1]
    p_c = N_DEV * c8
    dils = DILATIONS[1:]

    def body(x_ref, prm_ref, w_ref, qkg_ref, cprm_ref, bd_ref, proj_ref, qkn_ref, h_ref, ycv_ref, *rest):
        res_refs, (pf, qsc, vsc, carry, wcat, wsems) = rest[:2 * len(dils)], rest[2 * len(dils):]

        @pl.when(pl.program_id(0) == 0)
        def _():
            carry[...] = jnp.zeros_like(carry)
            _join_columns(w_ref, wcat, wsems)

        h = _ada_fwd(x_ref[...], prm_ref).astype(BF16)
        h_ref[...] = h
        for cols in _row_halves(p_c):
            pf[:, cols] = _dot(h, wcat[:, cols])
        proj_ref[...] = pf[...].astype(BF16)
        bdm = bd_ref[...]
        for i in range(2):
            t = pf[:, i * a_w:(i + 1) * a_w]
            rstd = lax.rsqrt(_head_mean(t * t, bdm) + EPS)
            tn = t * rstd * qkg_ref[i:i + 1, :]
            qkn_ref[:, i * a_w:(i + 1) * a_w] = tn.astype(BF16)
            _to_groups(tn, qsc, i * (a_w // LANES))
        _to_groups(pf[:, 2 * a_w:3 * a_w], vsc)
        for i, dil in enumerate(dils):
            _split_residues(qsc, res_refs[2 * i], dil, tm)
            _split_residues(vsc, res_refs[2 * i + 1], dil, tm)
        gb = pf[:, 3 * a_w:3 * a_w + c_w]
        z = pf[:, 3 * a_w + c_w:3 * a_w + 2 * c_w] * pf[:, 3 * a_w + 2 * c_w:3 * a_w + 3 * c_w]
        z1, z2 = _shift_rows(z, carry[...], tm)
        cv = cprm_ref[0:1, :] * z2 + cprm_ref[1:2, :] * z1 + cprm_ref[2:3, :] * z + cprm_ref[3:4, :]
        ycv_ref[...] = (gb * cv).astype(BF16)
        carry[...] = z[tm - 8:tm, :]

    res_specs, res_shapes = [], []
    for dil in dils:
        res_specs += [_res_spec(dil, tm, 2 * a_w), _res_spec(dil, tm, a_w)]
        res_shapes += [SDS((dil, t_len // dil, 2 * a_w), BF16), SDS((dil, t_len // dil, a_w), BF16)]
    return pl.pallas_call(
        body, name=f"mixer_in_{l}", grid=(t_len // tm,),
        in_specs=[pl.BlockSpec((tm, d_model), lambda m: (m, 0)),
                  pl.BlockSpec((8, d_model), lambda m: (0, 0)),
                  ANY,
                  pl.BlockSpec((8, a_w), lambda m: (0, 0)),
                  pl.BlockSpec((8, c_w), lambda m: (0, 0)),
                  pl.BlockSpec(bd.shape, lambda m: (0, 0))],
        out_specs=[pl.BlockSpec((tm, p_c), lambda m: (m, 0)),
                   pl.BlockSpec((tm, 2 * a_w), lambda m: (m, 0)),
                   pl.BlockSpec((tm, d_model), lambda m: (m, 0)),
                   pl.BlockSpec((tm, c_w), lambda m: (m, 0))] + res_specs,
        out_shape=[SDS((t_len, p_c), BF16), SDS((t_len, 2 * a_w), BF16),
                   SDS((t_len, d_model), BF16), SDS((t_len, c_w), BF16)] + res_shapes,
        scratch_shapes=[pltpu.VMEM((tm, p_c), F32), _groups(tm, 2 * a_w), _groups(tm, a_w),
                        pltpu.VMEM((8, c_w), F32), pltpu.VMEM((d_model, p_c), BF16),
                        pltpu.SemaphoreType.DMA((N_DEV,))],
        compiler_params=_params(1),
    )(x, prm, wing, qkg, cprm, bd)


def _band_mask(first_block):
    qi = lax.broadcasted_iota(jnp.int32, (BLOCK, 2 * BLOCK), 0)
    kj = lax.broadcasted_iota(jnp.int32, (BLOCK, 2 * BLOCK), 1)
    lowest = jnp.where(first_block, BLOCK, 0)
    return (kj >= qi) & (kj <= qi + BLOCK) & (kj >= lowest)


def _blocks_per_step(seq):
    n_sb = min(ATTN_BLOCKS_PER_STEP, seq // BLOCK)
    assert seq % (n_sb * BLOCK) == 0
    return n_sb


def _block_rows(sb):
    return slice(sb * BLOCK, (sb + 1) * BLOCK)


def _with_previous(prev_ref, cur_ref, sb, sl):
    before = prev_ref[:, sl] if sb == 0 else cur_ref[_block_rows(sb - 1), sl]
    return jnp.concatenate([before, cur_ref[_block_rows(sb), sl]], axis=0)


def _heads_on_lanes(cols):
    width = LANES // len(cols)
    lane = lax.broadcasted_iota(jnp.int32, (BLOCK, LANES), 1)
    out = jnp.broadcast_to(cols[-1], (BLOCK, LANES))
    for h in range(len(cols) - 2, -1, -1):
        out = jnp.where(lane < (h + 1) * width, cols[h], out)
    return out


def _attn_fwd(qk, v, v_col, l):
    dil, seq, a2 = qk.shape
    a_w = a2 // 2
    n_sb = _blocks_per_step(seq)
    rows_step = n_sb * BLOCK
    n_pair = a_w // LANES

    def body(q_ref, kc_ref, vc_ref, o_ref, lse_ref, kp_ref, vp_ref):
        @pl.when(pl.program_id(1) == 0)
        def _():
            kp_ref[...] = jnp.zeros_like(kp_ref)
            vp_ref[...] = jnp.zeros_like(vp_ref)

        masks = [_band_mask(pl.program_id(1) == 0)] + [_band_mask(False)] * (n_sb - 1)
        low = lax.broadcasted_iota(jnp.int32, (BLOCK, LANES), 1) < HEAD_DIM
        units = [(sb, slice(p * LANES, (p + 1) * LANES)) for sb in range(n_sb) for p in range(a_w // LANES)]
        scores = []
        for sb, sl in units:
            qp = q_ref[_block_rows(sb), sl]
            k2 = _with_previous(kp_ref, kc_ref, sb, sl)
            scores.append([_dot_nt(jnp.where(half, qp, jnp.zeros_like(qp)), k2) for half in (low, ~low)])
        probs, lses = [], []
        for (sb, sl), pair in zip(units, scores):
            pr, ls = [], []
            for s in pair:
                s = jnp.where(masks[sb], s * SM_SCALE, NEG)
                mx = jnp.max(s, axis=-1, keepdims=True)
                e = jnp.exp(s - mx)
                den = jnp.sum(e, axis=-1, keepdims=True)
                pr.append((e * (1.0 / den)).astype(BF16))
                ls.append(mx + jnp.log(den))
            probs.append(pr)
            lses.append(ls)
        for (sb, sl), pr, ls in zip(units, probs, lses):
            v2 = _with_previous(vp_ref, vc_ref, sb, sl)
            o_ref[_block_rows(sb), sl] = jnp.where(low, _dot(pr[0], v2), _dot(pr[1], v2)).astype(BF16)
        for sb in range(n_sb):
            cols = [c for ls in lses[sb * n_pair:(sb + 1) * n_pair] for c in ls]
            lse_ref[_block_rows(sb), :] = _heads_on_lanes(cols)
        kp_ref[...] = kc_ref[_block_rows(n_sb - 1), :]
        vp_ref[...] = vc_ref[_block_rows(n_sb - 1), :]

    def blk(f):
        return pl.BlockSpec((None, rows_step, a_w), f)

    return pl.pallas_call(
        body, name=f"attn_fwd_{l}_d{dil}", grid=(dil, seq // rows_step),
        in_specs=[blk(lambda r, b: (r, b, 0)),
                  blk(lambda r, b: (r, b, 1)),
                  blk(lambda r, b: (r, b, v_col))],
        out_specs=[blk(lambda r, b: (r, b, 0)), pl.BlockSpec((None, rows_step, LANES), lambda r, b: (r, b, 0))],
        out_shape=[SDS((dil, seq, a_w), BF16), SDS((dil, seq, LANES), F32)],
        scratch_shapes=[pltpu.VMEM((BLOCK, a_w), BF16), pltpu.VMEM((BLOCK, a_w), BF16)],
        compiler_params=_params(2),
    )(qk, qk, v)


def _attn_combine(outs, lses, ycv, expand, tm, l):
    t_len, c_w = ycv.shape
    a_w = outs[0].shape[-1]
    dils = DILATIONS[1:]
    n_r = len(dils)

    def body(o1, l1, *rest):
        o_res, l_res, y_ref, ex_ref = rest[:n_r], rest[n_r:2 * n_r], rest[2 * n_r], rest[2 * n_r + 1]
        cat_ref, lse_ref = rest[2 * n_r + 2], rest[2 * n_r + 3]
        ya_res, lse_res = rest[2 * n_r + 4:3 * n_r + 4], rest[3 * n_r + 4:4 * n_r + 4]
        scr = rest[4 * n_r + 4:]
        so, sl, sya, slse = scr[:n_r], scr[n_r:2 * n_r], scr[2 * n_r], scr[2 * n_r + 1]
        for i, dil in enumerate(dils):
            _merge_residues(o_res[i], so[i], dil, tm)
            _merge_residues(l_res[i], sl[i], dil, tm)
        ls = [l1[...]] + [s[0] for s in sl]
        mx = jnp.maximum(jnp.maximum(ls[0], ls[1]), ls[2])
        es = [jnp.exp(t - mx) for t in ls]
        den = es[0] + es[1] + es[2]
        inv = 1.0 / den
        slse[0] = mx + jnp.log(den)
        lse_ref[...] = slse[0]
        ex = ex_ref[...]
        ov = [o1[...].astype(F32)] + [_from_groups(s) for s in so]
        ya = jnp.zeros((tm, a_w), F32)
        for e, o in zip(es, ov):
            hi, lo = _split_bf16(e * inv)
            ya = ya + (_dot(hi, ex) + _dot(lo, ex)) * o
        _to_groups(ya, sya)
        cat_ref[:, :a_w] = ya.astype(BF16)
        cat_ref[:, a_w:] = y_ref[...]
        for i, dil in enumerate(dils):
            _split_residues(sya, ya_res[i], dil, tm)
            _split_residues(slse, lse_res[i], dil, tm)

    ta = pl.BlockSpec((tm, a_w), lambda m: (m, 0))
    tl = pl.BlockSpec((tm, LANES), lambda m: (m, 0))
    res_a = [_res_spec(dil, tm, a_w) for dil in dils]
    res_l = [_res_spec(dil, tm, LANES) for dil in dils]
    return pl.pallas_call(
        body, name=f"attn_combine_{l}", grid=(t_len // tm,),
        in_specs=[ta, tl] + res_a + res_l + [pl.BlockSpec((tm, c_w), lambda m: (m, 0)),
                                            pl.BlockSpec((LANES, a_w), lambda m: (0, 0))],
        out_specs=[pl.BlockSpec((tm, a_w + c_w), lambda m: (m, 0)), tl] + res_a + res_l,
        out_shape=[SDS((t_len, a_w + c_w), BF16), SDS((t_len, LANES), F32)]
        + [SDS((dil, t_len // dil, a_w), BF16) for dil in dils]
        + [SDS((dil, t_len // dil, LANES), F32) for dil in dils],
        scratch_shapes=[_groups(tm, a_w)] * n_r + [_groups(tm, LANES)] * n_r + [_groups(tm, a_w), _groups(tm, LANES)],
        compiler_params=_params(1),
    )(outs[0].reshape(t_len, a_w), lses[0].reshape(t_len, LANES), *outs[1:], *lses[1:], ycv, expand)


def _mixer_out(cat, x, prm, woutg, l, tm):
    t_len, d_model = x.shape
    r8 = woutg.shape[-2]

    def body(cat_ref, x_ref, prm_ref, w_ref, xo_ref, y_ref):
        w = w_ref[...].reshape(N_DEV * r8, d_model)
        for rows in _row_halves(tm):
            yv = _dot(cat_ref[rows, :], w)
            y_ref[rows, :] = yv.astype(BF16)
            xo_ref[rows, :] = x_ref[rows, :] + prm_ref[3:4, :] * yv

    tile = pl.BlockSpec((tm, d_model), lambda m: (m, 0))
    return pl.pallas_call(
        body, name=f"mixer_out_{l}", grid=(t_len // tm,),
        in_specs=[tile, tile, pl.BlockSpec((8, d_model), lambda m: (0, 0)),
                  pl.BlockSpec((N_DEV, r8, d_model), lambda m: (0, 0, 0))],
        out_specs=[tile, tile],
        out_shape=[SDS((t_len, d_model), F32), SDS((t_len, d_model), BF16)],
        compiler_params=_params(1),
    )(cat, x, prm, woutg)


def _mixer_bwd_out(dxo, ymix, prm, woutg, proj, cprm, l, tm):
    t_len, d_model = dxo.shape
    a_w = d_model // 2
    c_w = d_model - a_w
    r8 = woutg.shape[-2]
    n_m = t_len // tm
    hb = tm // 8

    dils = DILATIONS[1:]

    def body(dxo_ref, ym_ref, prm_ref, w_ref, pc_ref, halo_ref, cprm_ref,
             dya_ref, dpc_ref, dy_ref, st_ref, cs_ref, *rest):
        dya_res, (carry, dsc) = rest[:len(dils)], rest[len(dils):]
        step = pl.program_id(0)
        tile_i = n_m - 1 - step

        @pl.when(step == 0)
        def _():
            st_ref[...] = jnp.zeros_like(st_ref)
            cs_ref[...] = jnp.zeros_like(cs_ref)
            carry[...] = jnp.zeros_like(carry)

        dxo_v = dxo_ref[...]
        dy = (prm_ref[3:4, :] * dxo_v).astype(BF16)
        dy_ref[...] = dy
        st_ref[3:4, :] += jnp.sum(dxo_v * ym_ref[...].astype(F32), axis=0, keepdims=True)
        w = w_ref[...].reshape(N_DEV * r8, d_model)
        dcat = jnp.concatenate([_dot_nt(dy[rows, :], w) for rows in _row_halves(tm)], axis=0)
        _to_groups(dcat[:, :a_w], dsc)
        dya_ref[...] = dcat[:, :a_w].astype(BF16)
        for i, dil in enumerate(dils):
            _split_residues(dsc, dya_res[i], dil, tm)
        dyc = dcat[:, a_w:]
        gb = pc_ref[:, :c_w].astype(F32)
        gc = pc_ref[:, c_w:2 * c_w].astype(F32)
        u = pc_ref[:, 2 * c_w:].astype(F32)
        z = gc * u
        before = halo_ref[:, c_w:2 * c_w].astype(F32) * halo_ref[:, 2 * c_w:].astype(F32)
        before = jnp.where(tile_i > 0, before, jnp.zeros_like(before))
        z1, z2 = _shift_rows(z, before, tm)
        w0, w1, w2 = cprm_ref[0:1, :], cprm_ref[1:2, :], cprm_ref[2:3, :]
        cv = w0 * z2 + w1 * z1 + w2 * z + cprm_ref[3:4, :]
        dcv = dyc * gb
        cs_ref[0:1, :] += jnp.sum(dcv * z2, axis=0, keepdims=True)
        cs_ref[1:2, :] += jnp.sum(dcv * z1, axis=0, keepdims=True)
        cs_ref[2:3, :] += jnp.sum(dcv * z, axis=0, keepdims=True)
        cs_ref[3:4, :] += jnp.sum(dcv, axis=0, keepdims=True)
        row = lax.broadcasted_iota(jnp.int32, dcv.shape, 0)
        after = carry[...]
        d1 = jnp.where(row == tm - 1, after[0:1, :], pltpu.roll(dcv, tm - 1, 0))
        d2 = jnp.where(row == tm - 2, after[0:1, :],
                       jnp.where(row == tm - 1, after[1:2, :], pltpu.roll(dcv, tm - 2, 0)))
        dz = w2 * dcv + w1 * d1 + w0 * d2
        dpc_ref[:, :c_w] = (dyc * cv).astype(BF16)
        dpc_ref[:, c_w:2 * c_w] = (dz * u).astype(BF16)
        dpc_ref[:, 2 * c_w:] = (dz * gc).astype(BF16)
        carry[...] = dcv[0:8, :]

    def rev(width, col=0):
        return pl.BlockSpec((tm, width), lambda s: (n_m - 1 - s, col))

    fixed_d = pl.BlockSpec((8, d_model), lambda s: (0, 0))
    fixed_c = pl.BlockSpec((8, c_w), lambda s: (0, 0))
    return pl.pallas_call(
        body, name=f"mixer_bwd_out_{l}", grid=(n_m,),
        in_specs=[rev(d_model), rev(d_model), fixed_d,
                  pl.BlockSpec((N_DEV, r8, d_model), lambda s: (0, 0, 0)),
                  rev(3 * c_w, 1),
                  pl.BlockSpec((8, 3 * c_w), lambda s: (jnp.maximum((n_m - 1 - s) * hb - 1, 0), 1)),
                  fixed_c],
        out_specs=[rev(a_w), rev(3 * c_w), rev(d_model), fixed_d, fixed_c]
        + [_res_spec(dil, tm, a_w, lambda s: n_m - 1 - s) for dil in dils],
        out_shape=[SDS((t_len, a_w), BF16), SDS((t_len, 3 * c_w), BF16), SDS((t_len, d_model), BF16),
                   SDS((8, d_model), F32), SDS((8, c_w), F32)]
        + [SDS((dil, t_len // dil, a_w), BF16) for dil in dils],
        scratch_shapes=[pltpu.VMEM((8, c_w), F32), _groups(tm, a_w)],
        compiler_params=_params(1),
    )(dxo, ymix, prm, woutg, proj, proj, cprm)


def _attn_bwd(qk, v, v_col, dya, ya, lse, l):
    dil, seq, a2 = qk.shape
    a_w = a2 // 2
    n_sb = _blocks_per_step(seq)
    rows_step = n_sb * BLOCK
    n_step = seq // rows_step
    last = _block_rows(n_sb - 1)
    lse_w = LANES // (a_w // HEAD_DIM)
    single = n_step == 1

    def body(q_ref, kc_ref, vc_ref, do_ref, ya_ref, lse_ref,
             dq_ref, dk_ref, dv_ref, hold_k, hold_v, acc_k, acc_v, kp_ref, vp_ref):
        j = pl.program_id(1)

        @pl.when(j == 0)
        def _():
            kp_ref[...] = jnp.zeros_like(kp_ref)
            vp_ref[...] = jnp.zeros_like(vp_ref)

        @pl.when((pl.program_id(0) == 0) & (j == 0))
        def _():
            hold_k[...] = jnp.zeros_like(hold_k)
            hold_v[...] = jnp.zeros_like(hold_v)

        @pl.when(j < n_step)
        def _():
            masks = [_band_mask(j == 0)] + [_band_mask(False)] * (n_sb - 1)
            low = lax.broadcasted_iota(jnp.int32, (BLOCK, LANES), 1) < HEAD_DIM
            units = [(sb, slice(p * LANES, (p + 1) * LANES)) for sb in range(n_sb) for p in range(a_w // LANES)]
            acc_k[...] = jnp.zeros_like(acc_k)
            acc_v[...] = jnp.zeros_like(acc_v)
            heads, scores, dps = [], [], []
            for sb, sl in units:
                qp = q_ref[_block_rows(sb), sl]
                k2 = _with_previous(kp_ref, kc_ref, sb, sl)
                v2 = _with_previous(vp_ref, vc_ref, sb, sl)
                dop = do_ref[_block_rows(sb), sl]
                prod = dop.astype(F32) * ya_ref[_block_rows(sb), sl].astype(F32)
                lsep = lse_ref[_block_rows(sb), :]
                first = 2 * (sl.start // LANES) * lse_w
                for hh, half in enumerate((low, ~low)):
                    qh = jnp.where(half, qp, jnp.zeros_like(qp))
                    doh = jnp.where(half, dop, jnp.zeros_like(dop))
                    dsum = jnp.sum(jnp.where(half, prod, 0.0), axis=-1, keepdims=True)
                    heads.append((qh, doh, dsum, lsep[:, first + hh * lse_w:first + hh * lse_w + 1], masks[sb]))
                    scores.append(_dot_nt(qh, k2))
                    dps.append(_dot_nt(doh, v2))
            dss, prs = [], []
            for (qh, doh, dsum, lse_h, valid), s, dp in zip(heads, scores, dps):
                pr = jnp.exp(jnp.where(valid, s * SM_SCALE, NEG) - lse_h)
                dss.append((pr * (dp - dsum) * SM_SCALE).astype(BF16))
                prs.append(pr.astype(BF16))
            for i, (sb, sl) in enumerate(units):
                k2 = _with_previous(kp_ref, kc_ref, sb, sl)
                h0, h1 = heads[2 * i], heads[2 * i + 1]
                dq_ref[_block_rows(sb), sl] = jnp.where(low, _dot(dss[2 * i], k2),
                                                        _dot(dss[2 * i + 1], k2)).astype(BF16)
                two = slice(sb * BLOCK, (sb + 2) * BLOCK)
                acc_k[two, sl] += _dot_tn(dss[2 * i], h0[0]) + _dot_tn(dss[2 * i + 1], h1[0])
                acc_v[two, sl] += _dot_tn(prs[2 * i], h0[1]) + _dot_tn(prs[2 * i + 1], h1[1])
            for out_ref, hold, acc in ((dk_ref, hold_k, acc_k), (dv_ref, hold_v, acc_v)):
                if single:
                    out_ref[...] = acc[BLOCK:, :].astype(BF16)
                    continue
                if n_sb > 1:
                    out_ref[:rows_step - BLOCK, :] = hold[:rows_step - BLOCK, :].astype(BF16)
                out_ref[last, :] = (hold[last, :] + acc[:BLOCK, :]).astype(BF16)
                hold[...] = acc[BLOCK:, :]
            kp_ref[...] = kc_ref[last, :]
            vp_ref[...] = vc_ref[last, :]

        @pl.when(j == n_step)
        def _():
            dk_ref[...] = hold_k[...].astype(BF16)
            dv_ref[...] = hold_v[...].astype(BF16)

    def blk(f):
        return pl.BlockSpec((None, rows_step, a_w), f)

    def cur(j):
        return jnp.minimum(j, n_step - 1)

    def late(j):
        return jnp.maximum(j - 1, 0)

    return pl.pallas_call(
        body, name=f"attn_bwd_{l}_d{dil}", grid=(dil, 1 if single else n_step + 1),
        in_specs=[blk(lambda r, j: (r, cur(j), 0)),
                  blk(lambda r, j: (r, cur(j), 1)),
                  blk(lambda r, j: (r, cur(j), v_col)),
                  blk(lambda r, j: (r, cur(j), 0)),
                  blk(lambda r, j: (r, cur(j), 0)),
                  pl.BlockSpec((None, rows_step, LANES), lambda r, j: (r, cur(j), 0))],
        out_specs=[blk(lambda r, j: (r, cur(j), 0)), blk(lambda r, j: (r, late(j), 0)),
                   blk(lambda r, j: (r, late(j), 0))],
        out_shape=[SDS((dil, seq, a_w), BF16)] * 3,
        scratch_shapes=[pltpu.VMEM((rows_step, a_w), F32), pltpu.VMEM((rows_step, a_w), F32),
                        pltpu.VMEM((rows_step + BLOCK, a_w), F32), pltpu.VMEM((rows_step + BLOCK, a_w), F32),
                        pltpu.VMEM((BLOCK, a_w), BF16), pltpu.VMEM((BLOCK, a_w), BF16)],
        compiler_params=_params(2),
    )(qk, qk, v, dya, ya, lse)


def _mixer_bwd_in(parts, proj, dpc, dxo, x_in, prm, wing, qkg, bd, l, tm):
    t_len, d_model = x_in.shape
    a_w = d_model // 2
    c_w = d_model - a_w
    c8 = wing.shape[-1]
    p_c = N_DEV * c8

    dils = DILATIONS[1:]
    n_in = 3 * len(DILATIONS)

    def body(*refs):
        nat, res = refs[:3], refs[3:n_in]
        (pa_ref, dpc_ref, dxo_ref, x_ref, prm_ref, w_ref, qkg_ref, bd_ref,
         dx_ref, dproj_ref, st_ref, qs_ref) = refs[n_in:n_in + 12]
        sums, wcat, wsems = refs[n_in + 12:n_in + 15], refs[n_in + 15], refs[n_in + 16]

        @pl.when(pl.program_id(0) == 0)
        def _():
            st_ref[...] = jnp.zeros_like(st_ref)
            qs_ref[...] = jnp.zeros_like(qs_ref)
            _join_columns(w_ref, wcat, wsems)

        for t in range(3):
            _to_groups(nat[t][...].astype(F32), sums[t])
            for i, dil in enumerate(dils):
                _merge_residues(res[3 * i + t], sums[t], dil, tm, add=True)
        bdm = bd_ref[...]
        grads = [_from_groups(sums[0]), _from_groups(sums[1])]
        for i in range(2):
            t = pa_ref[:, i * a_w:(i + 1) * a_w].astype(F32)
            rstd = lax.rsqrt(_head_mean(t * t, bdm) + EPS)
            that = t * rstd
            qs_ref[i:i + 1, :] += jnp.sum(grads[i] * that, axis=0, keepdims=True)
            dhat = grads[i] * qkg_ref[i:i + 1, :]
            dt = rstd * (dhat - that * _head_mean(dhat * that, bdm))
            dproj_ref[:, i * a_w:(i + 1) * a_w] = dt.astype(BF16)
        dproj_ref[:, 2 * a_w:3 * a_w] = _from_groups(sums[2]).astype(BF16)
        dproj_ref[:, 3 * a_w:] = dpc_ref[...]
        dh = jnp.concatenate([_dot_nt(dproj_ref[rows, :], wcat[...]) for rows in _row_halves(tm)], axis=0)
        dx, dshift, dscale, dng = _ada_bwd(dh, x_ref[...], prm_ref)
        dx_ref[...] = dxo_ref[...] + dx
        st_ref[0:1, :] += dshift
        st_ref[1:2, :] += dscale
        st_ref[2:3, :] += dng

    ta = pl.BlockSpec((tm, a_w), lambda m: (m, 0))
    tile = pl.BlockSpec((tm, d_model), lambda m: (m, 0))
    return pl.pallas_call(
        body, name=f"mixer_bwd_in_{l}", grid=(t_len // tm,),
        in_specs=[ta] * 3 + [_res_spec(dil, tm, a_w) for dil in dils for _ in range(3)]
                 + [pl.BlockSpec((tm, 3 * a_w), lambda m: (m, 0)),
                             pl.BlockSpec((tm, 3 * c_w), lambda m: (m, 0)),
                             tile, tile, pl.BlockSpec((8, d_model), lambda m: (0, 0)),
                             ANY,
                             pl.BlockSpec((8, a_w), lambda m: (0, 0)),
                             pl.BlockSpec(bd.shape, lambda m: (0, 0))],
        out_specs=[tile, pl.BlockSpec((tm, p_c), lambda m: (m, 0)),
                   pl.BlockSpec((8, d_model), lambda m: (0, 0)), pl.BlockSpec((8, a_w), lambda m: (0, 0))],
        out_shape=[SDS((t_len, d_model), F32), SDS((t_len, p_c), BF16), SDS((8, d_model), F32), SDS((8, a_w), F32)],
        scratch_shapes=[_groups(tm, a_w)] * 3 + [pltpu.VMEM((d_model, p_c), BF16), pltpu.SemaphoreType.DMA((N_DEV,))],
        compiler_params=_params(1),
    )(*[t.reshape(t_len, a_w) for t in parts[0]], *[t for p in parts[1:] for t in p],
      proj, dpc, dxo, x_in, prm, wing, qkg, bd)


def _loss_head(x, target, tm):
    t_len, d_model = x.shape

    def body(x_ref, t_ref, dx_ref, loss_ref):
        @pl.when(pl.program_id(0) == 0)
        def _():
            loss_ref[...] = jnp.zeros_like(loss_ref)

        diff = x_ref[...] - t_ref[...]
        dx_ref[...] = diff * (1.0 / d_model)
        per_token = jnp.sum(diff * diff, axis=-1, keepdims=True) * (1.0 / d_model)
        loss_ref[...] += 0.5 * jnp.sum(per_token)

    tile = pl.BlockSpec((tm, d_model), lambda m: (m, 0))
    return pl.pallas_call(
        body, name="loss_head", grid=(t_len // tm,),
        in_specs=[tile, tile],
        out_specs=[tile, pl.BlockSpec((8, LANES), lambda m: (0, 0))],
        out_shape=[SDS((t_len, d_model), F32), SDS((8, LANES), F32)],
        compiler_params=_params(1),
    )(x, target)


def _row_tile(rows, cols):
    best = None
    for t in range(16, rows + 1, 16):
        if rows % t == 0 and t * cols * 4 <= (1 << 20):
            best = t
    return best if best is not None else rows


def _adamw(pieces, w, m, v, name, first_q=0, earlier=None):
    n_q = len(pieces)
    n_p, rows, cols = pieces[0].shape
    tr = _row_tile(rows, cols)
    n_i = rows // tr
    n_e = 0 if earlier is None else 4
    c1 = 1.0 - ADAM_B1 ** ADAM_STEP
    c2 = 1.0 - ADAM_B2 ** ADAM_STEP

    def body(*refs):
        p_refs = refs[:n_q]
        w_ref, m_ref, v_ref = refs[n_q:n_q + 3]
        g_ref, d_ref, nm_ref, nv_ref = refs[n_q + 3 + n_e:]
        for q in range(n_q):
            @pl.when(pl.program_id(0) == q)
            def _():
                g = p_refs[q][0].astype(F32)
                for i in range(1, n_p):
                    g = g + p_refs[q][i].astype(F32)
                g_ref[...] = g
                nm = ADAM_B1 * m_ref[...] + (1.0 - ADAM_B1) * g
                nv = ADAM_B2 * v_ref[...] + (1.0 - ADAM_B2) * (g * g)
                nm_ref[...] = nm
                nv_ref[...] = nv
                d_ref[...] = -ADAM_LR * ((nm / c1) / (jnp.sqrt(nv / c2) + ADAM_EPS) + ADAM_WD * w_ref[...])

    def piece_spec(q):
        return pl.BlockSpec((n_p, tr, cols), lambda a, i: (0, jnp.where(a == q, i, 0), 0))

    tile = pl.BlockSpec((tr, cols), lambda a, i: ((first_q + a) * n_i + i, 0))
    return pl.pallas_call(
        body, name=name, grid=(n_q, n_i),
        in_specs=[piece_spec(q) for q in range(n_q)] + [tile, tile, tile] + [ANY] * n_e,
        out_specs=[tile] * 4,
        out_shape=[SDS(w.shape, F32)] * 4,
        input_output_aliases={n_q + 3 + k: k for k in range(n_e)},
        compiler_params=_params(2),
    )(*pieces, w, m, v, *(earlier or []))


def _pack(vecs):
    flat = jnp.concatenate([v.reshape(-1).astype(F32) for v in vecs])
    rows = -(-flat.shape[0] // (8 * LANES)) * 8
    return jnp.pad(flat, (0, rows * LANES - flat.shape[0])).reshape(rows, LANES)


def _unpack(packed, shapes):
    lead = packed.shape[:-2]
    flat = packed.reshape(lead + (-1,))
    out, off = [], 0
    for s in shapes:
        size = 1
        for d in s:
            size *= d
        out.append(flat[..., off:off + size].reshape(lead + tuple(s)))
        off += size
    return out


def kernel(x, c, w_ada, b_ada, norm_g, w_in, q_norm_g, k_norm_g, conv_w, conv_b, w_out, ffn_w1, ffn_w2, loss_target, m_w_ada, m_b_ada, m_norm_g, m_w_in, m_q_norm_g, m_k_norm_g, m_conv_w, m_conv_b, m_w_out, m_ffn_w1, m_ffn_w2, v_w_ada, v_b_ada, v_norm_g, v_w_in, v_q_norm_g, v_k_norm_g, v_conv_w, v_conv_b, v_w_out, v_ffn_w1, v_ffn_w2):
    _, t_len, d_model = x.shape
    n_layer = w_ada.shape[0]
    a_w = d_model // 2
    c_w = d_model - a_w
    n_head = a_w // HEAD_DIM
    ada_cols = w_ada.shape[-1]
    tm = min(TOKEN_TILE, t_len)
    tm_ffn = min(FFN_FWD_TOKEN_TILE, t_len)
    tm_ffn_bwd = min(FFN_BWD_TOKEN_TILE, t_len)
    me = _index(_my_place())
    x2 = x.reshape(t_len, d_model)
    target = loss_target.reshape(t_len, d_model)

    small_shapes = [(d_model,), norm_g.shape, conv_w.shape]
    gathered = _allgather_small(_pack([c, norm_g, conv_w]), "gather_small_inputs")
    c_all, ng_parts, cw_parts = _unpack(gathered, small_shapes)
    norm_g_full = jnp.moveaxis(ng_parts, 0, 2).reshape(n_layer, 3, d_model)
    conv_w_full = jnp.moveaxis(cw_parts, 0, 2).reshape(n_layer, 3, c_w)

    b_loc = lax.dynamic_slice_in_dim(b_ada, me * ada_cols, ada_cols, axis=1).reshape(n_layer, 1, ada_cols)
    mod_part, silu_c = _mod_part(c_all, w_ada, b_loc)
    mod_all = _allgather_small(_pack([mod_part]), "gather_mod")
    mod_all = _unpack(mod_all, [mod_part.shape])[0]
    mod_mine = lax.dynamic_index_in_dim(mod_all, me, axis=2, keepdims=False)
    mod = jnp.moveaxis(mod_mine, 0, 1).reshape(n_layer, 3, 3, d_model)

    def prm_of(l, sub):
        rows = jnp.stack([norm_g_full[l, sub], mod[l, sub, 0], mod[l, sub, 1], mod[l, sub, 2]])
        return jnp.pad(rows, ((0, 4), (0, 0)))

    prm = [[prm_of(l, sub) for sub in range(3)] for l in range(n_layer)]
    qkg = [jnp.pad(jnp.stack([jnp.tile(q_norm_g[l], n_head), jnp.tile(k_norm_g[l], n_head)]), ((0, 6), (0, 0)))
           for l in range(n_layer)]
    cprm = [jnp.pad(jnp.concatenate([conv_w_full[l], conv_b[l][None]]), ((0, 4), (0, 0))) for l in range(n_layer)]
    head_of = jnp.arange(min(MXU_WIDTH, a_w)) // HEAD_DIM
    bd = (head_of[:, None] == head_of[None, :]).astype(BF16)
    expand = (jnp.arange(LANES)[:, None] == (jnp.arange(a_w)[None, :] // HEAD_DIM) * (LANES // n_head)).astype(BF16)

    f8 = ffn_w1.shape[-1]
    r8 = ffn_w2.shape[-2]
    c8 = w_in.shape[-1]
    o8 = w_out.shape[-2]
    me_arr = jnp.reshape(me, (1,)).astype(jnp.int32)
    win_b = [_place_own(w_in, (l,), me_arr, f"own_w_in_{l}") for l in range(n_layer)]
    wout_b = [_place_own(w_out, (l,), me_arr, f"own_w_out_{l}") for l in range(n_layer)]
    w1_b = [[_place_own(ffn_w1, (l, s), me_arr, f"own_w1_{l}_{s}") for s in range(2)] for l in range(n_layer)]
    w2_b = [[_place_own(ffn_w2, (l, s), me_arr, f"own_w2_{l}_{s}") for s in range(2)] for l in range(n_layer)]
    w1_first, w2_first = _allgather_big([w1_b[0][0], w2_b[0][0]], "gather_first_weights")
    groups = [[win_b[0], wout_b[0]], [w1_b[0][1], w2_b[0][1]]]
    for l in range(1, n_layer):
        groups += [[w1_b[l][0], w2_b[l][0]], [win_b[l], wout_b[l]], [w1_b[l][1], w2_b[l][1]]]
    flat = [a for grp in groups for a in grp]
    w_sends, w_recvs, w_lands, w_token = _send_start(flat, None, w1_first, "weights_start")

    def gathered(gi, after):
        lo = sum(len(grp) for grp in groups[:gi])
        hi = lo + len(groups[gi])
        return _send_wait(w_lands[lo:hi], w_sends[lo:hi], w_recvs[lo:hi], after, f"weights_wait_{gi}")

    def chunked(w1g):
        return w1g.reshape(2, N_CHUNK, d_model, f8)

    saved = []
    xc = x2
    for l in range(n_layer):
        keep = {"x0": xc}
        if l == 0:
            w1g, w2g = w1_first, w2_first
            prm_first = prm[0][0] + w_token[0, 0]
        else:
            w1g, w2g = gathered(3 * l - 1, xc)
            prm_first = prm[l][0]
        keep["w_ffn0"] = (chunked(w1g), w2g)
        keep["act0"], keep["h0"], keep["y0"], xc = _ffn_fwd(xc, prm_first, *keep["w_ffn0"], l, 0, tm_ffn)
        keep["x1"] = xc
        wing, woutg = gathered(0 if l == 0 else 3 * l, xc)
        keep.update(wing=wing, woutg=woutg)
        proj, qkn, keep["h1"], ycv, *res = _mixer_in(xc, prm[l][1], wing, qkg[l], cprm[l], bd, l, tm)
        qkv = [(qkn[None], proj[None], 2)] + [(res[2 * i], res[2 * i + 1], 0) for i in range(len(DILATIONS) - 1)]
        branches = [_attn_fwd(*t, l) for t in qkv]
        cat, lse, *res = _attn_combine([b[0] for b in branches], [b[1] for b in branches], ycv, expand, tm, l)
        n_r = len(DILATIONS) - 1
        keep["ya"] = [cat[None]] + res[:n_r]
        keep["lse"] = [lse[None]] + res[n_r:]
        xc, keep["ymix"] = _mixer_out(cat, xc, prm[l][1], woutg, l, tm)
        keep.update(proj=proj, qkv=qkv, cat=cat)
        keep["x2"] = xc
        w1g, w2g = gathered(1 if l == 0 else 3 * l + 1, xc)
        keep["w_ffn2"] = (chunked(w1g), w2g)
        keep["act2"], keep["h2"], keep["y2"], xc = _ffn_fwd(xc, prm[l][2], *keep["w_ffn2"], l, 1, tm_ffn)
        saved.append(keep)

    dx, loss_blk = _loss_head(xc, target, tm)
    loss = lax.psum(loss_blk[0, 0], ("x", "y", "c"))

    tk = min(GRAD_TOKEN_TILE, t_len)
    stats =[[None] * 3 for _ in range(n_layer)]
    conv_stats = [None] * n_layer
    qk_stats = [None] * n_layer

    started = {}
    dummy = jnp.zeros((8, LANES), F32)

    def start_exchange(g, name):
        sends, recvs, thru, token = _send_start(
            [_place_own(g, None, me_arr, f"own_grad_{name}")], [g], dummy, f"grads_start_{name}")
        started[name] = (thru[0], thru[1], sends[0], recvs[0], token)
        return token

    def wait_exchange(names, after, name):
        ent = [started[k] for k in names]
        done = _send_wait([e[0] for e in ent] + [e[1] for e in ent], [e[2] for e in ent], [e[3] for e in ent],
                          after, name)
        return dict(zip(names, done))

    def ffn_backward(dxo, keep, l, sub, which, prm_dep):
        dx_in, da, dy, st = _ffn_bwd(dxo, keep[f"y{which}"], keep[f"x{which}"], prm_dep,
                                     keep[f"act{which}"], *keep[f"w_ffn{which}"], l, sub, tm_ffn_bwd)
        g_w2 = _matmul_tn(
            keep[f"act{which}"], dy,
            pl.BlockSpec((None, None, tk, f8), lambda p, k: (2, p, k, 0)),
            pl.BlockSpec((tk, d_model), lambda p, k: (k, 0)),
            SDS((N_CHUNK, f8, d_model), BF16), pl.BlockSpec((None, f8, d_model), lambda p, k: (p, 0, 0)),
            N_CHUNK, t_len, tk, f"grad_w2_{l}_{sub}", dummy)
        token = start_exchange(g_w2.reshape(N_DEV, r8, d_model), f"w2_{l}_{sub}")
        g_w1 = _matmul_tn(
            da.reshape(N_DEV, t_len, f8), keep[f"h{which}"],
            pl.BlockSpec((None, tk, f8), lambda p, k: (p, k, 0)),
            pl.BlockSpec((tk, d_model), lambda p, k: (k, 0)),
            SDS((N_DEV, f8, d_model), BF16), pl.BlockSpec((None, f8, d_model), lambda p, k: (p, 0, 0)),
            N_DEV, t_len, tk, f"grad_w1_{l}_{sub}", token)
        token = start_exchange(g_w1, f"w1_{l}_{sub}")
        return dx_in, st, token[0, 0]

    token = 0.0
    for l in reversed(range(n_layer)):
        keep = saved[l]
        wing, woutg = keep["wing"], keep["woutg"]
        dx, stats[l][2], token = ffn_backward(dx, keep, l, 1, 2, prm[l][2] + token)
        dya, dpc, dy, st_gate, conv_stats[l], *dya_res = _mixer_bwd_out(
            dx, keep["ymix"], prm[l][1] + token, woutg, keep["proj"], cprm[l], l, tm)
        dyas = [dya[None]] + dya_res
        g_wout = _matmul_tn(
            keep["cat"], dy,
            pl.BlockSpec((tk, d_model), lambda p, k: (k, 0)),
            pl.BlockSpec((tk, d_model), lambda p, k: (k, 0)),
            SDS((d_model, d_model), BF16), pl.BlockSpec((d_model, d_model), lambda p, k: (0, 0)),
            1, t_len, tk, f"grad_wout_{l}", dummy)
        token = start_exchange(g_wout.reshape(N_DEV, o8, d_model), f"wout_{l}")[0, 0]
        parts = [_attn_bwd(*keep["qkv"][i], dyas[i], keep["ya"][i], keep["lse"][i], l)
                 for i in range(len(DILATIONS))]
        dx, dproj, st_norm, qk_stats[l] = _mixer_bwd_in(
            parts, keep["proj"], dpc, dx, keep["x1"], prm[l][1] + token, wing, qkg[l], bd, l, tm)
        stats[l][1] = st_norm + st_gate
        g_win = _matmul_tn(
            keep["h1"], dproj,
            pl.BlockSpec((tk, d_model), lambda p, k: (k, 0)),
            pl.BlockSpec((tk, N_DEV // 2 * c8), lambda p, k: (k, p)),
            SDS((N_DEV, d_model, c8), BF16), pl.BlockSpec((N_DEV // 2, d_model, c8), lambda p, k: (p, 0, 0)),
            2, t_len, tk, f"grad_win_{l}", dummy)
        token = start_exchange(g_win, f"win_{l}")[0, 0]
        dx, stats[l][0], token = ffn_backward(dx, keep, l, 0, 0, prm[l][0] + token)
    grad_x = dx.reshape(x.shape)

    def update(pieces, w, m, v, name):
        shape = w.shape
        cols = shape[-1]
        outs = _adamw([p.reshape(p.shape[0], -1, cols) for p in pieces], w.reshape(-1, cols),
                      m.reshape(-1, cols), v.reshape(-1, cols), name)
        return [o.reshape(shape) for o in outs]

    last = "w1_0_0"
    lands = wait_exchange([k for k in started if k != last], started[last][4], "grads_wait_early")
    res = {
        "w_in": update([lands[f"win_{l}"] for l in range(n_layer)], w_in, m_w_in, v_w_in, "adamw_w_in"),
        "w_out": update([lands[f"wout_{l}"] for l in range(n_layer)], w_out, m_w_out, v_w_out, "adamw_w_out"),
        "ffn_w2": update([lands[f"w2_{l}_{s}"] for l in range(n_layer) for s in range(2)],
                         ffn_w2, m_ffn_w2, v_ffn_w2, "adamw_ffn_w2"),
    }

    dmod = jnp.stack([jnp.stack([stats[l][sub][r] for sub in range(3) for r in (0, 1, 3)]) for l in range(n_layer)])
    dng = jnp.stack([jnp.stack([stats[l][sub][2] for sub in range(3)]) for l in range(n_layer)])
    dcw = jnp.stack([conv_stats[l][0:3] for l in range(n_layer)])
    dcb = jnp.stack([conv_stats[l][3] for l in range(n_layer)])
    dqk = jnp.stack([qk_stats[l][0:2].reshape(2, n_head, HEAD_DIM).sum(axis=1) for l in range(n_layer)])
    part_shapes = [dmod.shape, dng.shape, dcw.shape, dcb.shape, dqk.shape]
    parts_all = _allgather_small(_pack([dmod, dng, dcw, dcb, dqk]), "gather_small_grads")
    dmod_all = _unpack(parts_all, part_shapes)[0].reshape(N_DEV, n_layer, 9 * d_model)
    summed = _unpack(_sum_devices(parts_all), part_shapes)
    g_b_ada = summed[0].reshape(n_layer, 9 * d_model)
    ng_cols = norm_g.shape[-1]
    g_norm_g = lax.dynamic_slice_in_dim(summed[1], me * ng_cols, ng_cols, axis=2)
    cw_cols = conv_w.shape[-1]
    g_conv_w = lax.dynamic_slice_in_dim(summed[2], me * cw_cols, cw_cols, axis=2)
    g_conv_b = summed[3]
    g_q, g_k = summed[4][:, 0], summed[4][:, 1]
    dmod_loc = lax.dynamic_slice_in_dim(dmod_all, me * ada_cols, ada_cols, axis=2)
    g_w_ada = _w_ada_grad(silu_c.T, jnp.moveaxis(dmod_loc, 0, 1))
    res["w_ada"] = update([g_w_ada[None]], w_ada, m_w_ada, v_w_ada, "adamw_w_ada")

    names = ["b_ada", "norm_g", "q_norm_g", "k_norm_g", "conv_w", "conv_b"]
    g_small = [g_b_ada, g_norm_g, g_q, g_k, g_conv_w, g_conv_b]
    w_small = [b_ada, norm_g, q_norm_g, k_norm_g, conv_w, conv_b]
    m_small = [m_b_ada, m_norm_g, m_q_norm_g, m_k_norm_g, m_conv_w, m_conv_b]
    v_small = [v_b_ada, v_norm_g, v_q_norm_g, v_k_norm_g, v_conv_w, v_conv_b]
    outs = _adamw([_pack(g_small)[None]], _pack(w_small), _pack(m_small), _pack(v_small), "adamw_small")
    shapes = [w.shape for w in w_small]
    unpacked = [_unpack(o, shapes) for o in outs]
    for i, nme in enumerate(names):
        res[nme] = [unpacked[k][i] for k in range(4)]

    w1_t = [jnp.swapaxes(t, -1, -2) for t in (ffn_w1, m_ffn_w1, v_ffn_w1)]
    w1_flat = [t.reshape(-1, d_model) for t in w1_t]
    w1_keys = [f"w1_{l}_{s}" for l in range(n_layer) for s in range(2)]
    assert w1_keys[0] == last
    early = _adamw([lands[k] for k in w1_keys[1:]], *w1_flat, "adamw_ffn_w1_early", first_q=1)
    follows = (res["w_in"][1][0, :1, :1] + res["w_out"][1][0, :1, :1] + res["ffn_w2"][1][0, 0, :1, :1]
               + res["w_ada"][1][0, :1, :1] + outs[1][:1, :1] + early[1][-1:, :1])
    lands.update(wait_exchange([last], follows, "grads_wait_last"))
    done = _adamw([lands[last]], *w1_flat, "adamw_ffn_w1_last", first_q=0, earlier=early)
    res["ffn_w1"] = [jnp.swapaxes(t.reshape(w1_t[0].shape), -1, -2) for t in done]

    order = ["w_ada", "b_ada", "norm_g", "w_in", "q_norm_g", "k_norm_g", "conv_w", "conv_b", "w_out", "ffn_w1", "ffn_w2"]
    return (loss, grad_x, *[res[n][0] for n in order], *[res[n][1] for n in order],
            *[res[n][2] for n in order], *[res[n][3] for n in order])
```
